```python
import math
import numpy as np
import jax
import jax.numpy as jnp
from jax import lax

D_MODEL = 1024
BATCH = 16
SEQ = 2048
DEPTH = 2

DN_ALPHA = (2.0 * DEPTH) ** 0.25
DN_BETA = (8.0 * DEPTH) ** -0.25
LN_EPS = 1e-5
NEG = -1e30
BIG = 1e6

POOL_WINDOWS = (2, 4, 8, 16)
POOL_GROUP = D_MODEL // 8
POOL_WIDTH = POOL_GROUP * len(POOL_WINDOWS)

NSA_HEAD_DIM = 64
NSA_HEADS = D_MODEL // 128
NSA_KV_HEADS = 2
NSA_HPG = NSA_HEADS // NSA_KV_HEADS
NSA_WIDTH = NSA_HEADS * NSA_HEAD_DIM
NSA_KV_WIDTH = 3 * 2 * NSA_KV_HEADS * NSA_HEAD_DIM
CMP_LEN = 32
CMP_STRIDE = 16
CMP_HIDDEN = 2 * NSA_HEAD_DIM
SLC_LEN = 64
SLC_TOPN = 8
WIN_LEN = 512
Q_BLOCK = 128

REL_BUCKETS = 32
REL_MAX_DIST = 128

SSD_D_INNER = D_MODEL
SSD_HEAD_DIM = 64
SSD_HEADS = SSD_D_INNER // SSD_HEAD_DIM
SSD_GROUPS = 4
SSD_HPG = SSD_HEADS // SSD_GROUPS
SSD_STATE = 128
SSD_CONV = 4
SSD_CHUNK = 128
SSD_XBC = SSD_D_INNER + 2 * SSD_GROUPS * SSD_STATE
SSD_IN_WIDTH = SSD_D_INNER + SSD_XBC + SSD_HEADS

N_EXPERTS = 16
N_EXPERT_GROUPS = 4
EXPERTS_PER_GROUP = N_EXPERTS // N_EXPERT_GROUPS
TOP_K = 2
D_EXPERT = 512

IN_SPLITS = (POOL_WIDTH, NSA_WIDTH, NSA_KV_WIDTH, 3 * NSA_HEADS, SSD_IN_WIDTH, 3 * D_MODEL)
D_IN = POOL_WIDTH + NSA_WIDTH + NSA_KV_WIDTH + 3 * NSA_HEADS + SSD_IN_WIDTH + 3 * D_MODEL

kernel_name = 'hybrid_pool_nsa_ssd_moe_deepnorm'


def layer_norm(x, g, b):
    xf = x.astype(jnp.float32)
    mu = jnp.mean(xf, -1, keepdims=True)
    var = jnp.mean(jnp.square(xf - mu), -1, keepdims=True)
    return ((xf - mu) * lax.rsqrt(var + LN_EPS) * g + b).astype(x.dtype)


def rel_bucket(dist):
    n = jnp.maximum(dist, 0)
    max_exact = REL_BUCKETS // 2
    nf = jnp.maximum(n, 1).astype(jnp.float32)
    large = max_exact + (jnp.log(nf / max_exact) / math.log(REL_MAX_DIST / max_exact)
                         * (REL_BUCKETS - max_exact)).astype(jnp.int32)
    large = jnp.minimum(large, REL_BUCKETS - 1)
    return jnp.where(n < max_exact, n, large)


def masked_softmax(s, mask):
    p = jax.nn.softmax(jnp.where(mask, s.astype(jnp.float32), NEG), axis=-1)
    return p * mask


def pool_mixer(u, w_grp, scale):
    bsz, s, _ = u.shape
    uf = u.astype(jnp.float32)
    csum = jnp.concatenate([jnp.zeros_like(uf[:, :1]), jnp.cumsum(uf, axis=1)], axis=1)
    t = jnp.arange(s)
    outs = []
    for gi, w in enumerate(POOL_WINDOWS):
        sl = slice(gi * POOL_GROUP, (gi + 1) * POOL_GROUP)
        lo = jnp.maximum(t + 1 - w, 0)
        cnt = jnp.minimum(t + 1, w).astype(jnp.float32)
        mean = (csum[:, 1:, sl] - csum[:, lo, sl]) / cnt[None, :, None]
        outs.append(mean - uf[..., sl])
    r = jnp.stack(outs, axis=2).astype(u.dtype)
    y = jnp.einsum('bsgc,gcd->bsgd', r, w_grp).reshape(bsz, s, POOL_WIDTH)
    return y * scale


def nsa_mixer(q, kv, gate_logits, pe_k, w1_k, w2_k, pe_v, w1_v, w2_v, rel_bias):
    bsz, s, _ = q.shape
    G, hpg, dh = NSA_KV_HEADS, NSA_HPG, NSA_HEAD_DIM
    q = q.reshape(bsz, s, G, hpg, dh) * (dh ** -0.5)
    kv = kv.reshape(bsz, s, 3, 2, G, dh)
    k_cmp_tok, v_cmp_tok = kv[:, :, 0, 0], kv[:, :, 0, 1]
    k_slc, v_slc = kv[:, :, 1, 0], kv[:, :, 1, 1]
    k_win, v_win = kv[:, :, 2, 0], kv[:, :, 2, 1]
    gates = jax.nn.sigmoid(gate_logits.astype(jnp.float32)).reshape(bsz, s, 3, G, hpg).astype(q.dtype)
    tbl = rel_bias.astype(jnp.float32).reshape(REL_BUCKETS, G, hpg)

    n_cmp = (s - CMP_LEN) // CMP_STRIDE + 1
    cidx = np.arange(n_cmp)[:, None] * CMP_STRIDE + np.arange(CMP_LEN)[None, :]

    def compress(tok, pe, w1, w2):
        blk = tok[:, cidx] + pe[None, None, :, None, :]
        blk = blk.transpose(0, 1, 3, 2, 4).reshape(bsz, n_cmp, G, CMP_LEN * dh)
        return jax.nn.gelu(blk @ w1) @ w2

    k_c = compress(k_cmp_tok, pe_k, w1_k, w2_k)
    v_c = compress(v_cmp_tok, pe_v, w1_v, w2_v)
    cmp_end = jnp.arange(n_cmp) * CMP_STRIDE + CMP_LEN - 1

    n_slc = s // SLC_LEN
    n_top = min(SLC_TOPN, n_slc)
    c0 = np.arange(n_cmp) * CMP_STRIDE
    s0s = np.arange(n_slc) * SLC_LEN
    ov = np.clip(np.minimum(c0[:, None] + CMP_LEN, s0s[None, :] + SLC_LEN)
                 - np.maximum(c0[:, None], s0s[None, :]), 0, None) / CMP_LEN
    overlap = jnp.asarray(ov, dtype=jnp.float32)
    kb = k_slc.reshape(bsz, n_slc, SLC_LEN, G, dh).transpose(0, 3, 1, 2, 4)
    vb = v_slc.reshape(bsz, n_slc, SLC_LEN, G, dh).transpose(0, 3, 1, 2, 4)
    gather_blocks = jax.vmap(jax.vmap(lambda blocks, ids: blocks[ids]))

    pad = ((0, 0), (WIN_LEN, 0), (0, 0), (0, 0))
    k_wp = jnp.pad(k_win, pad)
    v_wp = jnp.pad(v_win, pad)
    span = WIN_LEN + Q_BLOCK
    jpos = jnp.arange(n_slc)

    def block(bi):
        st = bi * Q_BLOCK
        t = st + jnp.arange(Q_BLOCK)
        qb = lax.dynamic_slice_in_dim(q, st, Q_BLOCK, axis=1)
        gb = lax.dynamic_slice_in_dim(gates, st, Q_BLOCK, axis=1)
        dist_c = t[:, None] - cmp_end[None, :]
        sc = jnp.einsum('bqghd,bngd->bghqn', qb, k_c).astype(jnp.float32) \
            + tbl[rel_bucket(dist_c)].transpose(2, 3, 0, 1)
        p_c = masked_softmax(sc, dist_c >= 0)
        o_c = jnp.einsum('bghqn,bngd->bqghd', p_c.astype(v_c.dtype), v_c)
        imp = jnp.einsum('bghqn,nj->bgqj', p_c, overlap)
        jt = t // SLC_LEN
        valid = jpos[None, :] * SLC_LEN <= t[:, None]
        forced = (jpos[None, :] == 0) | (jpos[None, :] == jt[:, None]) | (jpos[None, :] == jt[:, None] - 1)
        imp = jnp.where(forced, BIG, jnp.where(valid, imp, -BIG))
        _, sel = lax.top_k(imp, n_top)
        k_sel = gather_blocks(kb, sel).reshape(bsz, G, Q_BLOCK, n_top * SLC_LEN, dh)
        v_sel = gather_blocks(vb, sel).reshape(bsz, G, Q_BLOCK, n_top * SLC_LEN, dh)
        kpos = (sel[..., None] * SLC_LEN + jnp.arange(SLC_LEN)).reshape(bsz, G, Q_BLOCK, n_top * SLC_LEN)
        dist_s = t[None, None, :, None] - kpos
        bias_s = jax.vmap(lambda tb, bk: tb[bk], in_axes=(1, 1), out_axes=1)(tbl, rel_bucket(dist_s))
        ss = jnp.einsum('bqghd,bgqkd->bghqk', qb, k_sel).astype(jnp.float32) + jnp.moveaxis(bias_s, -1, 2)
        p_s = masked_softmax(ss, (dist_s >= 0)[:, :, None])
        o_s = jnp.einsum('bghqk,bgqkd->bqghd', p_s.astype(v_sel.dtype), v_sel)
        kw = lax.dynamic_slice_in_dim(k_wp, st, span, axis=1)
        vw = lax.dynamic_slice_in_dim(v_wp, st, span, axis=1)
        kpos_w = st - WIN_LEN + jnp.arange(span)
        dist_w = t[:, None] - kpos_w[None, :]
        mask_w = (dist_w >= 0) & (dist_w < WIN_LEN) & (kpos_w[None, :] >= 0)
        sw = jnp.einsum('bqghd,bkgd->bghqk', qb, kw).astype(jnp.float32) \
            + tbl[rel_bucket(dist_w)].transpose(2, 3, 0, 1)
        p_w = masked_softmax(sw, mask_w)
        o_w = jnp.einsum('bghqk,bkgd->bqghd', p_w.astype(vw.dtype), vw)
        return (gb[:, :, 0, ..., None] * o_c + gb[:, :, 1, ..., None] * o_s
                + gb[:, :, 2, ..., None] * o_w)

    out = lax.map(block, jnp.arange(s // Q_BLOCK))
    return out.transpose(1, 0, 2, 3, 4, 5).reshape(bsz, s, NSA_WIDTH)


def ssd_mixer(zxbcdt, conv_w, conv_b, dt_bias, a_log, d_skip, norm_g):
    bsz, s, _ = zxbcdt.shape
    G, E, P, N, L = SSD_GROUPS, SSD_HPG, SSD_HEAD_DIM, SSD_STATE, SSD_CHUNK
    nc = s // L
    z = zxbcdt[..., :SSD_D_INNER]
    xbc = zxbcdt[..., SSD_D_INNER:SSD_D_INNER + SSD_XBC]
    dt = zxbcdt[..., SSD_D_INNER + SSD_XBC:]
    xbc = lax.conv_general_dilated(xbc, conv_w[:, None, :], window_strides=(1,),
                                   padding=[(SSD_CONV - 1, 0)],
                                   dimension_numbers=('NWC', 'WIO', 'NWC'),
                                   feature_group_count=SSD_XBC) + conv_b
    xbc = jax.nn.silu(xbc)
    xs = xbc[..., :SSD_D_INNER].reshape(bsz, s, G, E, P)
    Bc = xbc[..., SSD_D_INNER:SSD_D_INNER + G * N].reshape(bsz, nc, L, G, N)
    Cc = xbc[..., SSD_D_INNER + G * N:].reshape(bsz, nc, L, G, N)
    dt = jax.nn.softplus(dt.astype(jnp.float32) + dt_bias)
    A = -jnp.exp(a_log.astype(jnp.float32))
    a = (dt * A).reshape(bsz, nc, L, G, E)
    xdt = (xs * dt.reshape(bsz, s, G, E)[..., None]).reshape(bsz, nc, L, G, E, P)
    acs = jnp.cumsum(a, axis=2)
    acs_t = jnp.moveaxis(acs, 2, -1)
    tri = jnp.tril(jnp.ones((L, L), dtype=bool))
    seg = jnp.exp(jnp.where(tri, acs_t[..., :, None] - acs_t[..., None, :], -jnp.inf))
    cb = jnp.einsum('bclgn,bcsgn->bcgls', Cc, Bc)
    y_diag = jnp.einsum('bcgls,bcgels,bcsgep->bclgep', cb, seg, xdt)
    decay_in = jnp.exp(acs[:, :, -1:] - acs)
    states = jnp.einsum('bclgn,bclge,bclgep->bcgepn', Bc, decay_in, xdt)
    chunk_decay = jnp.exp(acs[:, :, -1])

    def step(h, inp):
        dec, st = inp
        return dec[..., None, None] * h + st, h

    _, h_in = lax.scan(step, jnp.zeros_like(states[:, 0]),
                       (jnp.moveaxis(chunk_decay, 1, 0), jnp.moveaxis(states, 1, 0)))
    h_in = jnp.moveaxis(h_in, 0, 1)
    y_off = jnp.einsum('bclgn,bcgepn,bclge->bclgep', Cc, h_in, jnp.exp(acs))
    y = (y_diag + y_off).reshape(bsz, s, G, E, P) + xs * d_skip.reshape(G, E)[..., None]
    y = y.reshape(bsz, s, SSD_D_INNER) * jax.nn.silu(z.astype(jnp.float32))
    yg = y.reshape(bsz, s, SSD_GROUPS, -1)
    yg = yg * lax.rsqrt(jnp.mean(yg * yg, -1, keepdims=True) + LN_EPS)
    return (yg.reshape(bsz, s, SSD_D_INNER) * norm_g).astype(zxbcdt.dtype)


def moe(x, router_w, router_b, w1, w3, w2):
    bsz, s, d = x.shape
    xt = x.reshape(-1, d)
    probs = jax.nn.softmax((xt @ router_w).astype(jnp.float32), axis=-1)
    sel_score = (probs + router_b).reshape(-1, N_EXPERT_GROUPS, EXPERTS_PER_GROUP)
    grp_score = jnp.sum(lax.top_k(sel_score, TOP_K)[0], axis=-1)
    grp = jnp.argmax(grp_score, axis=-1)
    in_grp = grp[:, None] == jnp.arange(N_EXPERT_GROUPS)[None, :]
    masked = jnp.where(in_grp[..., None], sel_score, -jnp.inf).reshape(-1, N_EXPERTS)
    _, idx = lax.top_k(masked, TOP_K)
    wsel = jnp.take_along_axis(probs, idx, axis=-1)
    wsel = wsel / jnp.sum(wsel, -1, keepdims=True)
    combine = jnp.sum(jax.nn.one_hot(idx, N_EXPERTS, dtype=jnp.float32) * wsel[..., None], axis=1)
    y = jnp.zeros(xt.shape, jnp.float32)
    for e in range(N_EXPERTS):
        h = jax.nn.silu(xt @ w1[e]) * (xt @ w3[e])
        y = y + combine[:, e:e + 1] * (h @ w2[e])
    return y.reshape(bsz, s, d).astype(x.dtype)


def setup_inputs(seed: int = 0) -> dict:
    key = jax.random.key(seed)
    ks = iter(jax.random.split(key, 40))

    def nrm(shape, scale):
        return scale * jax.random.normal(next(ks), shape, jnp.float32)

    Ld = DEPTH
    dt0 = jnp.exp(jax.random.uniform(next(ks), (Ld, SSD_HEADS), jnp.float32)
                  * (math.log(0.1) - math.log(0.001)) + math.log(0.001))
    return {
        'x': nrm((BATCH, SEQ, D_MODEL), 1.0),
        'ln0_g': 1.0 + nrm((D_MODEL,), 0.1),
        'ln0_b': nrm((D_MODEL,), 0.02),
        'w_in': nrm((Ld, D_MODEL, D_IN), D_MODEL ** -0.5),
        'pool_w': nrm((Ld, len(POOL_WINDOWS), POOL_GROUP, POOL_GROUP), POOL_GROUP ** -0.5),
        'pool_scale': 1.0 + nrm((Ld, POOL_WIDTH), 0.1),
        'cmp_k_pe': nrm((Ld, CMP_LEN, NSA_HEAD_DIM), 0.1),
        'cmp_k_w1': nrm((Ld, CMP_LEN * NSA_HEAD_DIM, CMP_HIDDEN), (CMP_LEN * NSA_HEAD_DIM) ** -0.5),
        'cmp_k_w2': nrm((Ld, CMP_HIDDEN, NSA_HEAD_DIM), CMP_HIDDEN ** -0.5),
        'cmp_v_pe': nrm((Ld, CMP_LEN, NSA_HEAD_DIM), 0.1),
        'cmp_v_w1': nrm((Ld, CMP_LEN * NSA_HEAD_DIM, CMP_HIDDEN), (CMP_LEN * NSA_HEAD_DIM) ** -0.5),
        'cmp_v_w2': nrm((Ld, CMP_HIDDEN, NSA_HEAD_DIM), CMP_HIDDEN ** -0.5),
        'rel_bias': nrm((REL_BUCKETS, NSA_HEADS), 0.3),
        'conv_w': nrm((Ld, SSD_CONV, SSD_XBC), SSD_CONV ** -0.5),
        'conv_b': nrm((Ld, SSD_XBC), 0.02),
        'dt_bias': dt0 + jnp.log(-jnp.expm1(-dt0)),
        'a_log': jnp.log(jax.random.uniform(next(ks), (Ld, SSD_HEADS), jnp.float32, 1.0, 16.0)),
        'd_skip': 1.0 + nrm((Ld, SSD_HEADS), 0.1),
        'ssd_norm_g': 1.0 + nrm((Ld, SSD_D_INNER), 0.1),
        'w_br_pool': nrm((Ld, POOL_WIDTH, D_MODEL), POOL_WIDTH ** -0.5),
        'w_br_nsa': nrm((Ld, NSA_WIDTH, D_MODEL), NSA_WIDTH ** -0.5),
        'w_br_ssd': nrm((Ld, SSD_D_INNER, D_MODEL), SSD_D_INNER ** -0.5),
        'w_out': nrm((Ld, D_MODEL, D_MODEL), DN_BETA * D_MODEL ** -0.5),
        'ln1_g': 1.0 + nrm((Ld, D_MODEL), 0.1),
        'ln1_b': nrm((Ld, D_MODEL), 0.02),
        'router_w': nrm((D_MODEL, N_EXPERTS), D_MODEL ** -0.5),
        'router_b': nrm((N_EXPERTS,), 0.01),
        'exp_w1': nrm((Ld, N_EXPERTS, D_MODEL, D_EXPERT), D_MODEL ** -0.5),
        'exp_w3': nrm((Ld, N_EXPERTS, D_MODEL, D_EXPERT), D_MODEL ** -0.5),
        'exp_w2': nrm((Ld, N_EXPERTS, D_EXPERT, D_MODEL), DN_BETA * D_EXPERT ** -0.5),
        'ln2_g': 1.0 + nrm((Ld, D_MODEL), 0.1),
        'ln2_b': nrm((Ld, D_MODEL), 0.02),
    }


def reference(x, ln0_g, ln0_b, w_in, pool_w, pool_scale, cmp_k_pe, cmp_k_w1, cmp_k_w2,
              cmp_v_pe, cmp_v_w1, cmp_v_w2, rel_bias, conv_w, conv_b, dt_bias, a_log, d_skip,
              ssd_norm_g, w_br_pool, w_br_nsa, w_br_ssd, w_out, ln1_g, ln1_b, router_w, router_b,
              exp_w1, exp_w3, exp_w2, ln2_g, ln2_b):
    split_points = np.cumsum(IN_SPLITS)[:-1].tolist()
    h = layer_norm(x, ln0_g, ln0_b)
    for i in range(DEPTH):
        proj = h @ w_in[i]
        u_pool, u_q, u_kv, u_ng, u_ssd, u_gate = jnp.split(proj, split_points, axis=-1)
        br_a = pool_mixer(u_pool, pool_w[i], pool_scale[i]) @ w_br_pool[i]
        br_b = nsa_mixer(u_q, u_kv, u_ng, cmp_k_pe[i], cmp_k_w1[i], cmp_k_w2[i],
                         cmp_v_pe[i], cmp_v_w1[i], cmp_v_w2[i], rel_bias) @ w_br_nsa[i]
        br_c = ssd_mixer(u_ssd, conv_w[i], conv_b[i], dt_bias[i], a_log[i], d_skip[i],
                         ssd_norm_g[i]) @ w_br_ssd[i]
        g_a, g_b, g_c = jnp.split(jax.nn.sigmoid(u_gate), 3, axis=-1)
        mix = (g_a * br_a + g_b * br_b + g_c * br_c) @ w_out[i]
        h = layer_norm(DN_ALPHA * h + mix, ln1_g[i], ln1_b[i])
        h = layer_norm(DN_ALPHA * h + moe(h, router_w, router_b, exp_w1[i], exp_w3[i], exp_w2[i]),
                       ln2_g[i], ln2_b[i])
    return h
```

```python
import functools
import math

import numpy as np
import jax
import jax.numpy as jnp
from jax import lax
from jax.experimental import pallas as pl
from jax.experimental.pallas import tpu as pltpu

F32 = jnp.float32
BF16 = jnp.bfloat16

D_MODEL = 1024
DEPTH = 2
DN_ALPHA = (2.0 * DEPTH) ** 0.25
LN_EPS = 1e-5
NEG = -1e30
BIG = 1e6

POOL_WINDOWS = (2, 4, 8, 16)
POOL_GROUP = 128
POOL_WIDTH = 512
MAX_POOL_WINDOW = 16

NSA_HEAD_DIM = 64
NSA_KV_HEADS = 2
NSA_HPG = 4
NSA_HEADS = 8
NSA_WIDTH = 512
CMP_LEN = 32
CMP_STRIDE = 16
CMP_HIDDEN = 128
SLC_LEN = 64
SLC_TOPN = 8
WIN_LEN = 512
Q_BLOCK = 128
REL_BUCKETS = 32
REL_MAX_DIST = 128

SSD_D_INNER = 1024
SSD_HEAD_DIM = 64
SSD_HEADS = 16
SSD_GROUPS = 4
SSD_STATE = 128
SSD_CONV = 4
SSD_CHUNK = 128
SSD_XBC = 2048

N_EXPERTS = 16
N_EXPERT_GROUPS = 4
EXPERTS_PER_GROUP = 4
D_EXPERT = 512
PAIRS = ((0, 1), (0, 2), (0, 3), (1, 2), (1, 3), (2, 3))
N_BUCKETS = N_EXPERT_GROUPS * len(PAIRS)

LANE = 128
SUBLANE = 8
VMEM_LIMIT = 48 * 1024 * 1024

C_POOL = 0
C_Q = 512
C_KV = 1024
C_NG = 1792
C_DT = 1920
C_XBC = 2048
C_Z = 4096
C_GATE = 5120
D_PAD = 8192

PROJ_TM = 1024
PROJ_TN = 1024
MIX_TS = 256
ROUTER_TM = 1024
MOE_TM = 256


def _dot(a, b):
    return jnp.dot(a, b, preferred_element_type=F32)


def _dot_nt(a, b):
    return lax.dot_general(a, b, (((1,), (1,)), ((), ())), preferred_element_type=F32)


def _dot_tn(a, b):
    return lax.dot_general(a, b, (((0,), (0,)), ((), ())), preferred_element_type=F32)


def _split3(x):
    x1 = x.astype(BF16)
    r1 = x - x1.astype(F32)
    x2 = r1.astype(BF16)
    x3 = (r1 - x2.astype(F32)).astype(BF16)
    return x1, x2, x3


def _layer_norm(x, g, b):
    mu = jnp.mean(x, -1, keepdims=True)
    xc = x - mu
    var = jnp.mean(xc * xc, -1, keepdims=True)
    return xc * lax.rsqrt(var + LN_EPS) * g + b


def _cparams(sem):
    return pltpu.CompilerParams(dimension_semantics=sem, vmem_limit_bytes=VMEM_LIMIT)


def _ln_kernel(x_ref, g_ref, b_ref, o_ref):
    o_ref[...] = _layer_norm(x_ref[...], g_ref[...], b_ref[...])


def _ln(x2d, g, b):
    t, d = x2d.shape
    tm = 512
    return pl.pallas_call(
        _ln_kernel,
        out_shape=jax.ShapeDtypeStruct((t, d), F32),
        grid=(t // tm,),
        in_specs=[pl.BlockSpec((tm, d), lambda i: (i, 0)),
                  pl.BlockSpec((1, d), lambda i: (0, 0)),
                  pl.BlockSpec((1, d), lambda i: (0, 0))],
        out_specs=pl.BlockSpec((tm, d), lambda i: (i, 0)),
        compiler_params=_cparams(("parallel",)),
        name="ln0",
    )(x2d, g.reshape(1, d), b.reshape(1, d))


def _proj_kernel(h_ref, w_ref, o_ref, hb_ref):
    @pl.when(pl.program_id(1) == 0)
    def _():
        hb_ref[...] = h_ref[...].astype(BF16)

    o_ref[...] = _dot(hb_ref[...], w_ref[...])


def _proj(h2d, w_pad):
    t, d = h2d.shape
    return pl.pallas_call(
        _proj_kernel,
        out_shape=jax.ShapeDtypeStruct((t, D_PAD), F32),
        grid=(t // PROJ_TM, D_PAD // PROJ_TN),
        in_specs=[pl.BlockSpec((PROJ_TM, d), lambda i, j: (i, 0)),
                  pl.BlockSpec((d, PROJ_TN), lambda i, j: (0, j))],
        out_specs=pl.BlockSpec((PROJ_TM, PROJ_TN), lambda i, j: (i, j)),
        scratch_shapes=[pltpu.VMEM((PROJ_TM, d), BF16)],
        compiler_params=_cparams(("parallel", "arbitrary")),
        name="proj",
    )(h2d, w_pad)


def _pad_cols(w, n):
    return jnp.pad(w, ((0, 0), (0, n - w.shape[1])))


def _prep_w_in(w_in):
    d = w_in.shape[0]
    o = 0
    w_pool = w_in[:, o:o + POOL_WIDTH]; o += POOL_WIDTH
    w_q = w_in[:, o:o + NSA_WIDTH]; o += NSA_WIDTH
    w_kv = w_in[:, o:o + 768]; o += 768
    w_ng = w_in[:, o:o + 24]; o += 24
    w_z = w_in[:, o:o + SSD_D_INNER]; o += SSD_D_INNER
    w_xbc = w_in[:, o:o + SSD_XBC]; o += SSD_XBC
    w_dt = w_in[:, o:o + SSD_HEADS]; o += SSD_HEADS
    w_gate = w_in[:, o:o + 3 * D_MODEL]
    w_q = w_q.reshape(d, NSA_KV_HEADS, NSA_HPG, NSA_HEAD_DIM).transpose(0, 2, 1, 3).reshape(d, NSA_WIDTH)
    w = jnp.concatenate([w_pool, w_q, w_kv, _pad_cols(w_ng, LANE), _pad_cols(w_dt, LANE),
                         w_xbc, w_z, w_gate], axis=1)
    return w.astype(BF16)


def _compress_one(tok_ref, pe_ref, w1_ref, w2_ref):
    n16 = tok_ref.shape[0] // CMP_STRIDE
    acc_a = jnp.zeros((n16, 2 * CMP_HIDDEN), F32)
    acc_b = jnp.zeros((n16, 2 * CMP_HIDDEN), F32)
    for l in range(CMP_STRIDE):
        rows = tok_ref[pl.ds(l, n16, stride=CMP_STRIDE), :]
        xa = (rows + pe_ref[l:l + 1, :]).astype(BF16)
        xb = (rows + pe_ref[CMP_STRIDE + l:CMP_STRIDE + l + 1, :]).astype(BF16)
        acc_a = acc_a + _dot(xa, w1_ref[l])
        acc_b = acc_b + _dot(xb, w1_ref[CMP_STRIDE + l])
    hid = acc_a + pltpu.roll(acc_b, n16 - 1, axis=0)
    hid = jax.nn.gelu(hid)
    return _dot(hid.astype(BF16), w2_ref[...])


def _compress_kernel(kt_ref, vt_ref, pek_ref, w1k_ref, w2k_ref, pev_ref, w1v_ref, w2v_ref,
                     kc_ref, vc_ref):
    kc_ref[0] = _compress_one(kt_ref, pek_ref, w1k_ref, w2k_ref).astype(BF16)
    vc_ref[0] = _compress_one(vt_ref, pev_ref, w1v_ref, w2v_ref).astype(BF16)


def _blockdiag2(w):
    z = jnp.zeros_like(w)
    return jnp.concatenate([jnp.concatenate([w, z], -1), jnp.concatenate([z, w], -1)], -2)


def _compress(proj2d, bsz, s, pe_k, w1_k, w2_k, pe_v, w1_v, w2_v):
    n16 = s // CMP_STRIDE

    def prep(pe, w1, w2):
        pe2 = jnp.concatenate([pe, pe], axis=-1)
        w1b = _blockdiag2(w1.reshape(CMP_LEN, NSA_HEAD_DIM, CMP_HIDDEN)).astype(BF16)
        w2b = _blockdiag2(w2).astype(BF16)
        return pe2, w1b, w2b

    pk = prep(pe_k, w1_k, w2_k)
    pv = prep(pe_v, w1_v, w2_v)
    full = lambda shape: pl.BlockSpec(shape, lambda b: (0,) * len(shape))
    wspecs = [full((CMP_LEN, LANE)), full((CMP_LEN, LANE, 2 * CMP_HIDDEN)), full((2 * CMP_HIDDEN, LANE))]
    return pl.pallas_call(
        _compress_kernel,
        out_shape=(jax.ShapeDtypeStruct((bsz, n16, LANE), BF16),
                   jax.ShapeDtypeStruct((bsz, n16, LANE), BF16)),
        grid=(bsz,),
        in_specs=[pl.BlockSpec((s, LANE), lambda b: (b, C_KV // LANE)),
                  pl.BlockSpec((s, LANE), lambda b: (b, C_KV // LANE + 1))] + wspecs + wspecs,
        out_specs=(pl.BlockSpec((1, n16, LANE), lambda b: (b, 0, 0)),
                   pl.BlockSpec((1, n16, LANE), lambda b: (b, 0, 0))),
        compiler_params=_cparams(("parallel",)),
        name="nsa_compress",
    )(proj2d, proj2d, *pk, *pv)


def _rel_bucket(dist):
    n = jnp.maximum(dist, 0)
    max_exact = REL_BUCKETS // 2
    nf = jnp.maximum(n, 1).astype(F32)
    large = max_exact + (jnp.log(nf / max_exact) / math.log(REL_MAX_DIST / max_exact)
                         * (REL_BUCKETS - max_exact)).astype(jnp.int32)
    large = jnp.minimum(large, REL_BUCKETS - 1)
    return jnp.where(n < max_exact, n, large)


def _nsa_bias_tables(rel_bias, nq):
    tbl = rel_bias.astype(F32)
    i = np.arange(Q_BLOCK)[:, None]
    j = np.arange(Q_BLOCK)[None, :]
    dist = np.stack([i - j, Q_BLOCK + i - j])
    tb = tbl[_rel_bucket(jnp.asarray(dist))].transpose(3, 0, 1, 2)
    qb = np.arange(nq)[:, None, None]
    dist_c = qb * Q_BLOCK + i[None] - (CMP_STRIDE * j[None] + CMP_LEN - 1)
    bc = tbl[_rel_bucket(jnp.asarray(dist_c))].transpose(0, 3, 1, 2)
    cv = jnp.broadcast_to(tbl[REL_BUCKETS - 1][:, None], (NSA_HEADS, LANE))
    return tb, bc, cv


def _nsa_consts(s):
    n_slc = s // SLC_LEN
    e = (np.arange(LANE)[:, None] == (np.arange(s)[None, :] // SLC_LEN)).astype(np.float32)
    n_cmp = (s - CMP_LEN) // CMP_STRIDE + 1
    c0 = np.arange(n_cmp) * CMP_STRIDE
    s0 = np.arange(n_slc) * SLC_LEN
    ov = np.clip(np.minimum(c0[:, None] + CMP_LEN, s0[None, :] + SLC_LEN)
                 - np.maximum(c0[:, None], s0[None, :]), 0, None) / CMP_LEN
    ovp = np.zeros((LANE, LANE), np.float32)
    ovp[:n_cmp, :n_slc] = ov
    return jnp.asarray(e, BF16), jnp.asarray(ovp, BF16)


def _nsa_kernel(q_ref, ng_ref, ks_ref, vs_ref, kw_ref, vw_ref, kc_ref, vc_ref,
                tb_ref, bc_ref, cv_ref, e_ref, ov_ref, o_ref,
                msel_ref, m_ref, l_ref, acc_ref, *, n_slc):
    qi = pl.program_id(1)
    hq = NSA_HPG * Q_BLOCK
    row = lax.broadcasted_iota(jnp.int32, (Q_BLOCK, LANE), 0)
    lane = lax.broadcasted_iota(jnp.int32, (Q_BLOCK, LANE), 1)
    lane_lo = lane < NSA_HEAD_DIM
    causal = lane <= row
    t = qi * Q_BLOCK + row
    gates = jax.nn.sigmoid(ng_ref[...])
    qall = q_ref[...]

    def softmax_step(s3, maskb, v_bf):
        maskf = maskb.astype(F32)
        s3 = jnp.where(maskb[None], s3, NEG)
        m_old = m_ref[...]
        m_new = jnp.maximum(m_old, jnp.max(s3, -1, keepdims=True))
        alpha = jnp.exp(m_old - m_new)
        p = jnp.exp(s3 - m_new) * maskf[None]
        l_ref[...] = alpha * l_ref[...] + jnp.sum(p, -1, keepdims=True)
        pv = _dot(p.astype(BF16).reshape(hq, LANE), v_bf).reshape(NSA_HPG, Q_BLOCK, LANE)
        acc_ref[...] = alpha * acc_ref[...] + pv
        m_ref[...] = m_new

    def reset():
        m_ref[...] = jnp.full(m_ref.shape, NEG, F32)
        l_ref[...] = jnp.zeros(l_ref.shape, F32)
        acc_ref[...] = jnp.zeros(acc_ref.shape, F32)

    def finish():
        l = l_ref[...]
        return acc_ref[...] * jnp.where(l > 0, 1.0 / l, 0.0)

    def key_tile(k_ref, v_ref, kt):
        off = pl.multiple_of(kt * Q_BLOCK, Q_BLOCK)
        return k_ref[pl.ds(off, Q_BLOCK), :].astype(BF16), v_ref[pl.ds(off, Q_BLOCK), :].astype(BF16)

    outs = []
    for g in range(NSA_KV_HEADS):
        half = lane_lo if g == 0 else jnp.logical_not(lane_lo)
        hs = slice(g * NSA_HPG, (g + 1) * NSA_HPG)
        qg = jnp.concatenate(
            [jnp.where(half, qall[:, h * LANE:(h + 1) * LANE], 0.0) for h in range(NSA_HPG)], axis=0)
        qg = (qg * (NSA_HEAD_DIM ** -0.5)).astype(BF16)
        bias_far = cv_ref[hs, :][:, None, :]

        def scores(k_bf):
            return _dot_nt(qg, k_bf).reshape(NSA_HPG, Q_BLOCK, LANE)

        mask_c = (t - (CMP_STRIDE * lane + CMP_LEN - 1)) >= 0
        s3 = jnp.where(mask_c[None], scores(kc_ref[0]) + bc_ref[0, hs], NEG)
        mx = jnp.max(s3, -1, keepdims=True)
        p = jnp.exp(s3 - mx) * mask_c.astype(F32)[None]
        lsum = jnp.sum(p, -1, keepdims=True)
        p = p * jnp.where(lsum > 0, 1.0 / lsum, 0.0)
        pb = p.astype(BF16)
        o_c = _dot(pb.reshape(hq, LANE), vc_ref[0]).reshape(NSA_HPG, Q_BLOCK, LANE)

        imp = _dot(pb[0], ov_ref[...])
        for h in range(1, NSA_HPG):
            imp = imp + _dot(pb[h], ov_ref[...])
        jt = t // SLC_LEN
        forced = (lane == 0) | (lane == jt) | (lane == jt - 1)
        valid = lane * SLC_LEN <= t
        val = jnp.where(forced, BIG, jnp.where(valid, imp, -BIG))
        val = jnp.where(lane < n_slc, val, -jnp.inf)
        rank = jnp.zeros((Q_BLOCK, LANE), F32)
        for i in range(n_slc):
            ci = val[:, i:i + 1]
            rank = rank + jnp.where(ci > val, 1.0, 0.0) + jnp.where(ci == val, (lane > i).astype(F32), 0.0)
        sel = jnp.where((rank < min(SLC_TOPN, n_slc)) & (lane < n_slc), 1.0, 0.0)
        msel_ref[...] = _dot(sel.astype(BF16), e_ref[...])

        def sel_mask(kt):
            off = pl.multiple_of(kt * Q_BLOCK, Q_BLOCK)
            return msel_ref[:, pl.ds(off, Q_BLOCK)] > 0.5

        reset()

        def far_body(kt, carry):
            kb, vb = key_tile(ks_ref, vs_ref, kt)
            softmax_step(scores(kb) + bias_far, sel_mask(kt), vb)
            return carry

        lax.fori_loop(0, jnp.maximum(qi - 1, 0), far_body, 0)

        @pl.when(qi >= 1)
        def _():
            kb, vb = key_tile(ks_ref, vs_ref, qi - 1)
            softmax_step(scores(kb) + tb_ref[hs, 1], sel_mask(qi - 1), vb)

        kb, vb = key_tile(ks_ref, vs_ref, qi)
        softmax_step(scores(kb) + tb_ref[hs, 0], sel_mask(qi) & causal, vb)
        o_s = finish()

        reset()
        n_back = WIN_LEN // Q_BLOCK
        for d in range(n_back, -1, -1):
            if d == n_back:
                mk = jnp.logical_not(causal)
            elif d == 0:
                mk = causal
            else:
                mk = jnp.full((Q_BLOCK, LANE), True)
            bias = bias_far if d >= 2 else tb_ref[hs, d]

            def win_tile(d=d, mk=mk, bias=bias):
                kb, vb = key_tile(kw_ref, vw_ref, qi - d)
                softmax_step(scores(kb) + bias, mk, vb)

            if d == 0:
                win_tile()
            else:
                pl.when(qi >= d)(win_tile)
        o_w = finish()

        og = []
        for h in range(NSA_HPG):
            c = g * NSA_HPG + h
            og.append(gates[:, c:c + 1] * o_c[h] + gates[:, NSA_HEADS + c:NSA_HEADS + c + 1] * o_s[h]
                      + gates[:, 2 * NSA_HEADS + c:2 * NSA_HEADS + c + 1] * o_w[h])
        outs.append(og)

    for h in range(NSA_HPG):
        o_ref[:, h * LANE:(h + 1) * LANE] = jnp.where(lane_lo, outs[0][h], outs[1][h])


def _nsa(proj2d, kc, vc, tb, bc, cv, e_exp, ovp, bsz, s):
    nq = s // Q_BLOCK
    kvb = C_KV // LANE
    kv_spec = lambda blk: pl.BlockSpec((s, LANE), lambda b, qi: (b, kvb + blk))
    full = lambda shape: pl.BlockSpec(shape, lambda b, qi: (0,) * len(shape))
    return pl.pallas_call(
        functools.partial(_nsa_kernel, n_slc=s // SLC_LEN),
        out_shape=jax.ShapeDtypeStruct((bsz * s, NSA_WIDTH), F32),
        grid=(bsz, nq),
        in_specs=[pl.BlockSpec((Q_BLOCK, NSA_WIDTH), lambda b, qi: (b * nq + qi, C_Q // NSA_WIDTH)),
                  pl.BlockSpec((Q_BLOCK, LANE), lambda b, qi: (b * nq + qi, C_NG // LANE)),
                  kv_spec(2), kv_spec(3), kv_spec(4), kv_spec(5),
                  pl.BlockSpec((1, s // CMP_STRIDE, LANE), lambda b, qi: (b, 0, 0)),
                  pl.BlockSpec((1, s // CMP_STRIDE, LANE), lambda b, qi: (b, 0, 0)),
                  full((NSA_HEADS, 2, Q_BLOCK, LANE)),
                  pl.BlockSpec((1, NSA_HEADS, Q_BLOCK, LANE), lambda b, qi: (qi, 0, 0, 0)),
                  full((NSA_HEADS, LANE)),
                  full((LANE, s)),
                  full((LANE, LANE))],
        out_specs=pl.BlockSpec((Q_BLOCK, NSA_WIDTH), lambda b, qi: (b * nq + qi, 0)),
        scratch_shapes=[pltpu.VMEM((Q_BLOCK, s), F32),
                        pltpu.VMEM((NSA_HPG, Q_BLOCK, 1), F32),
                        pltpu.VMEM((NSA_HPG, Q_BLOCK, 1), F32),
                        pltpu.VMEM((NSA_HPG, Q_BLOCK, LANE), F32)],
        compiler_params=_cparams(("parallel", "arbitrary")),
        name="nsa_attn",
    )(proj2d, proj2d, proj2d, proj2d, proj2d, proj2d, kc, vc, tb, bc, cv, e_exp, ovp)


def _ssd_kernel(xbc_ref, prev_ref, z_ref, dt_ref, cw_ref, cb_ref, dtb_ref, alog_ref, alogx_ref,
                dskip_ref, ng_ref, e16_ref, tril_ref, o_ref, xcat_ref, state_ref):
    c = pl.program_id(1)
    L = SSD_CHUNK
    pair_w = 2 * SSD_HEAD_DIM

    @pl.when(c == 0)
    def _():
        state_ref[...] = jnp.zeros(state_ref.shape, F32)

    xcat_ref[0:SUBLANE, :] = jnp.where(c > 0, prev_ref[...], 0.0)
    xcat_ref[SUBLANE:, :] = xbc_ref[...]
    conv = cb_ref[...]
    for k in range(SSD_CONV):
        conv = conv + cw_ref[k:k + 1, :] * xcat_ref[pl.ds(SUBLANE - (SSD_CONV - 1) + k, L), :]
    xbc = jax.nn.silu(conv)
    xs = xbc[:, :SSD_D_INNER]
    bmat = xbc[:, SSD_D_INNER:SSD_D_INNER + SSD_GROUPS * SSD_STATE]
    cmat = xbc[:, SSD_D_INNER + SSD_GROUPS * SSD_STATE:]

    dt = jax.nn.softplus(dt_ref[...] + dtb_ref[...])
    tril = tril_ref[...]
    e16 = e16_ref[...]
    d1, d2, d3 = _split3(dt)
    dt_x = _dot(d1, e16) + _dot(d2, e16) + _dot(d3, e16)
    a_c = dt * (-jnp.exp(alog_ref[...]))
    a_x = dt_x * (-jnp.exp(alogx_ref[...]))
    c1, c2, c3 = _split3(a_c)
    acs = _dot(tril, c1) + _dot(tril, c2) + _dot(tril, c3)
    x1, x2, x3 = _split3(a_x)
    acs_x = _dot(tril, x1) + _dot(tril, x2) + _dot(tril, x3)
    acs_t = acs.T
    last = acs_x[L - 1:L, :]
    eacs = jnp.exp(acs_x)
    decay_in = jnp.exp(last - acs_x)
    cdec = jnp.exp(last)
    xdt = xs * dt_x
    wst = xdt * decay_in

    row = lax.broadcasted_iota(jnp.int32, (L, L), 0)
    lane = lax.broadcasted_iota(jnp.int32, (L, L), 1)
    tri = row >= lane
    lo = lane < SSD_HEAD_DIM

    ys = []
    for g in range(SSD_GROUPS):
        cg = cmat[:, g * SSD_STATE:(g + 1) * SSD_STATE].astype(BF16)
        bg = bmat[:, g * SSD_STATE:(g + 1) * SSD_STATE].astype(BF16)
        cb = _dot_nt(cg, bg)
        for pr in range(2):
            pp = g * 2 + pr
            ls = slice(pp * pair_w, (pp + 1) * pair_w)
            xp = xdt[:, ls]
            yd = None
            for e in range(2):
                hd = 2 * pp + e
                diff = acs[:, hd:hd + 1] - acs_t[hd:hd + 1, :]
                seg = jnp.exp(jnp.where(tri, diff, NEG))
                pm = (cb * seg).astype(BF16)
                xe = jnp.where(lo if e == 0 else jnp.logical_not(lo), xp, 0.0).astype(BF16)
                term = _dot(pm, xe)
                yd = term if yd is None else yd + term
            st = state_ref[pp]
            yo = _dot(cg, st.astype(BF16)) * eacs[:, ls]
            ys.append(yd + yo)
            state_ref[pp] = cdec[:, ls] * st + _dot_tn(bg, wst[:, ls].astype(BF16))

    y = jnp.concatenate(ys, axis=1) + xs * dskip_ref[...]
    y = y * jax.nn.silu(z_ref[...])
    gw = SSD_D_INNER // SSD_GROUPS
    outs = []
    for gi in range(SSD_GROUPS):
        yg = y[:, gi * gw:(gi + 1) * gw]
        outs.append(yg * lax.rsqrt(jnp.mean(yg * yg, -1, keepdims=True) + LN_EPS))
    o_ref[...] = jnp.concatenate(outs, axis=1) * ng_ref[...]


def _ssd(proj2d, bsz, s, conv_w, conv_b, dt_bias, a_log, d_skip, norm_g):
    nc = s // SSD_CHUNK
    L = SSD_CHUNK
    pad16 = lambda v: jnp.pad(v.reshape(1, SSD_HEADS), ((0, 0), (0, LANE - SSD_HEADS)))
    rep64 = lambda v: jnp.repeat(v, SSD_HEAD_DIM).reshape(1, SSD_D_INNER)
    e16 = np.zeros((LANE, SSD_D_INNER), np.float32)
    e16[np.arange(SSD_D_INNER) // SSD_HEAD_DIM, np.arange(SSD_D_INNER)] = 1.0
    tril = np.tril(np.ones((L, L), np.float32))
    full = lambda shape: pl.BlockSpec(shape, lambda b, c: (0,) * len(shape))
    rows8 = s // SUBLANE
    return pl.pallas_call(
        _ssd_kernel,
        out_shape=jax.ShapeDtypeStruct((bsz * s, SSD_D_INNER), F32),
        grid=(bsz, nc),
        in_specs=[pl.BlockSpec((L, SSD_XBC), lambda b, c: (b * nc + c, C_XBC // SSD_XBC)),
                  pl.BlockSpec((SUBLANE, SSD_XBC),
                               lambda b, c: (jnp.maximum(b * rows8 + c * (L // SUBLANE) - 1, 0), C_XBC // SSD_XBC)),
                  pl.BlockSpec((L, SSD_D_INNER), lambda b, c: (b * nc + c, C_Z // SSD_D_INNER)),
                  pl.BlockSpec((L, LANE), lambda b, c: (b * nc + c, C_DT // LANE)),
                  full((SSD_CONV, SSD_XBC)), full((1, SSD_XBC)), full((1, LANE)), full((1, LANE)),
                  full((1, SSD_D_INNER)), full((1, SSD_D_INNER)), full((1, SSD_D_INNER)),
                  full((LANE, SSD_D_INNER)), full((L, L))],
        out_specs=pl.BlockSpec((L, SSD_D_INNER), lambda b, c: (b * nc + c, 0)),
        scratch_shapes=[pltpu.VMEM((L + SUBLANE, SSD_XBC), F32),
                        pltpu.VMEM((SSD_HEADS // 2, SSD_STATE, 2 * SSD_HEAD_DIM), F32)],
        compiler_params=_cparams(("parallel", "arbitrary")),
        name="ssd",
    )(proj2d, proj2d, proj2d, proj2d, conv_w, conv_b.reshape(1, SSD_XBC), pad16(dt_bias), pad16(a_log),
      rep64(a_log), rep64(d_skip), norm_g.reshape(1, SSD_D_INNER),
      jnp.asarray(e16, BF16), jnp.asarray(tril, BF16))


def _mix_kernel(up_ref, upprev_ref, nsa_ref, ssd_ref, ga_ref, gb_ref, gc_ref, h_ref,
                pw_ref, ps_ref, wbp_ref, wbn_ref, wbs_ref, wo_ref, lg_ref, lb_ref, o_ref, *, ts):
    si = pl.program_id(1)
    w0 = MAX_POOL_WINDOW
    prev = jnp.where(si > 0, upprev_ref[...], 0.0)
    xcat = jnp.concatenate([prev, up_ref[...]], axis=0)
    tpos = si * ts + lax.broadcasted_iota(jnp.int32, (ts, POOL_GROUP), 0)
    ygs = []
    for gi, w in enumerate(POOL_WINDOWS):
        x = xcat[:, gi * POOL_GROUP:(gi + 1) * POOL_GROUP]
        acc = x
        d = 1
        while d < w:
            acc = acc + pltpu.roll(acc, d, axis=0)
            d *= 2
        cnt = jnp.minimum(tpos + 1, w).astype(F32)
        r = acc[w0:, :] / cnt - x[w0:, :]
        ygs.append(_dot(r.astype(BF16), pw_ref[gi]))
    y = jnp.concatenate(ygs, axis=1) * ps_ref[...]
    br_a = _dot(y.astype(BF16), wbp_ref[...])
    br_b = _dot(nsa_ref[...].astype(BF16), wbn_ref[...])
    br_c = _dot(ssd_ref[...].astype(BF16), wbs_ref[...])
    mixed = (jax.nn.sigmoid(ga_ref[...]) * br_a + jax.nn.sigmoid(gb_ref[...]) * br_b
             + jax.nn.sigmoid(gc_ref[...]) * br_c)
    mix = _dot(mixed.astype(BF16), wo_ref[...])
    o_ref[...] = _layer_norm(DN_ALPHA * h_ref[...] + mix, lg_ref[...], lb_ref[...])


def _mix(proj2d, nsa_out, ssd_y, h2d, bsz, s, pool_w, pool_scale, w_br_pool, w_br_nsa, w_br_ssd,
         w_out, ln_g, ln_b):
    ts = MIX_TS
    ns = s // ts
    d = D_MODEL
    w0 = MAX_POOL_WINDOW
    wbn = w_br_nsa.reshape(NSA_KV_HEADS, NSA_HPG, NSA_HEAD_DIM, d).transpose(1, 0, 2, 3).reshape(NSA_WIDTH, d)
    full = lambda shape: pl.BlockSpec(shape, lambda b, si: (0,) * len(shape))
    row = lambda width, cb: pl.BlockSpec((ts, width), lambda b, si: (b * ns + si, cb))
    return pl.pallas_call(
        functools.partial(_mix_kernel, ts=ts),
        out_shape=jax.ShapeDtypeStruct((bsz * s, d), F32),
        grid=(bsz, ns),
        in_specs=[row(POOL_WIDTH, C_POOL // POOL_WIDTH),
                  pl.BlockSpec((w0, POOL_WIDTH),
                               lambda b, si: (jnp.maximum((b * s + si * ts) // w0 - 1, 0), C_POOL // POOL_WIDTH)),
                  row(NSA_WIDTH, 0), row(d, 0),
                  row(d, C_GATE // d), row(d, C_GATE // d + 1), row(d, C_GATE // d + 2),
                  row(d, 0),
                  full((len(POOL_WINDOWS), POOL_GROUP, POOL_GROUP)), full((1, POOL_WIDTH)),
                  full((POOL_WIDTH, d)), full((NSA_WIDTH, d)), full((d, d)), full((d, d)),
                  full((1, d)), full((1, d))],
        out_specs=pl.BlockSpec((ts, d), lambda b, si: (b * ns + si, 0)),
        compiler_params=_cparams(("parallel", "arbitrary")),
        name="mix",
    )(proj2d, proj2d, nsa_out, ssd_y, proj2d, proj2d, proj2d, h2d,
      pool_w.astype(BF16), pool_scale.reshape(1, POOL_WIDTH), w_br_pool.astype(BF16), wbn.astype(BF16),
      w_br_ssd.astype(BF16), w_out.astype(BF16), ln_g.reshape(1, d), ln_b.reshape(1, d))


def _router_kernel(h_ref, rw_ref, rb_ref, bucket_ref, wlo_ref, whi_ref):
    logits = _dot_nt(rw_ref[...], h_ref[...].astype(BF16))
    lg = [logits[e:e + 1, :] for e in range(N_EXPERTS)]
    mx = lg[0]
    for e in range(1, N_EXPERTS):
        mx = jnp.maximum(mx, lg[e])
    ex = [jnp.exp(v - mx) for v in lg]
    den = ex[0]
    for e in range(1, N_EXPERTS):
        den = den + ex[e]
    probs = [v / den for v in ex]
    score = [probs[e] + rb_ref[e:e + 1, :] for e in range(N_EXPERTS)]

    def group_vals(vals, grp):
        out = []
        for k in range(EXPERTS_PER_GROUP):
            v = vals[k]
            for gi in range(1, N_EXPERT_GROUPS):
                v = jnp.where(grp == gi, vals[gi * EXPERTS_PER_GROUP + k], v)
            out.append(v)
        return out

    gscore = []
    for gi in range(N_EXPERT_GROUPS):
        sc = score[gi * EXPERTS_PER_GROUP:(gi + 1) * EXPERTS_PER_GROUP]
        best = None
        for a, b in PAIRS:
            v = sc[a] + sc[b]
            best = v if best is None else jnp.maximum(best, v)
        gscore.append(best)
    grp = jnp.zeros(gscore[0].shape, jnp.int32)
    best = gscore[0]
    for gi in range(1, N_EXPERT_GROUPS):
        better = gscore[gi] > best
        grp = jnp.where(better, gi, grp)
        best = jnp.where(better, gscore[gi], best)

    sc = group_vals(score, grp)
    pr = group_vals(probs, grp)
    first = jnp.zeros(grp.shape, jnp.int32)
    fv = sc[0]
    for k in range(1, EXPERTS_PER_GROUP):
        better = sc[k] > fv
        first = jnp.where(better, k, first)
        fv = jnp.where(better, sc[k], fv)
    second = jnp.full(grp.shape, -1, jnp.int32)
    sv = jnp.full(fv.shape, -jnp.inf, F32)
    for k in range(EXPERTS_PER_GROUP):
        better = (first != k) & ((sc[k] > sv) | (second < 0))
        second = jnp.where(better, k, second)
        sv = jnp.where(better, sc[k], sv)
    lo = jnp.minimum(first, second)
    hi = jnp.maximum(first, second)
    p_lo = pr[0]
    p_hi = pr[0]
    for k in range(1, EXPERTS_PER_GROUP):
        p_lo = jnp.where(lo == k, pr[k], p_lo)
        p_hi = jnp.where(hi == k, pr[k], p_hi)
    tot = p_lo + p_hi
    pair = jnp.zeros(grp.shape, jnp.int32)
    for pi, (a, b) in enumerate(PAIRS):
        pair = jnp.where((lo == a) & (hi == b), pi, pair)
    bucket_ref[...] = grp * len(PAIRS) + pair
    wlo_ref[...] = p_lo / tot
    whi_ref[...] = p_hi / tot


def _router(h2d, router_w, router_b):
    t, d = h2d.shape
    tm = ROUTER_TM
    rw = jnp.pad(router_w.T, ((0, LANE - N_EXPERTS), (0, 0))).astype(BF16)
    rb = jnp.broadcast_to(jnp.pad(router_b, (0, LANE - N_EXPERTS))[:, None], (LANE, tm))
    vec = lambda dt: jax.ShapeDtypeStruct((1, t), dt)
    return pl.pallas_call(
        _router_kernel,
        out_shape=(vec(jnp.int32), vec(F32), vec(F32)),
        grid=(t // tm,),
        in_specs=[pl.BlockSpec((tm, d), lambda i: (i, 0)),
                  pl.BlockSpec((LANE, d), lambda i: (0, 0)),
                  pl.BlockSpec((LANE, tm), lambda i: (0, 0))],
        out_specs=(pl.BlockSpec((1, tm), lambda i: (0, i)),) * 3,
        compiler_params=_cparams(("parallel",)),
        name="router",
    )(h2d, rw, rb)


def _moe_plan(bucket, w_lo, w_hi, t):
    tm = MOE_TM
    n_tiles = t // tm + N_BUCKETS
    p_rows = n_tiles * tm
    onehot = (bucket[:, None] == jnp.arange(N_BUCKETS)[None, :]).astype(jnp.int32)
    csum = jnp.cumsum(onehot, axis=0)
    counts = csum[-1]
    rank = jnp.sum(onehot * csum, axis=1) - 1
    tiles_per = (counts + tm - 1) // tm
    tile_end = jnp.cumsum(tiles_per)
    tile_start = tile_end - tiles_per
    dest = tile_start[bucket] * tm + rank
    src = jnp.zeros((p_rows,), jnp.int32).at[dest].set(jnp.arange(t, dtype=jnp.int32))
    wl = jnp.zeros((p_rows,), F32).at[dest].set(w_lo)
    wh = jnp.zeros((p_rows,), F32).at[dest].set(w_hi)
    n_used = tile_end[-1]
    tile_ids = jnp.arange(n_tiles)
    tb = jnp.searchsorted(tile_end, tile_ids, side="right").astype(jnp.int32)
    tb = jnp.where(tile_ids < n_used, tb, tb[jnp.maximum(n_used - 1, 0)])
    tb = jnp.minimum(tb, N_BUCKETS - 1)
    pairs = np.asarray(PAIRS, np.int32)
    grp = tb // len(PAIRS)
    ea = grp * EXPERTS_PER_GROUP + jnp.asarray(pairs[:, 0])[tb % len(PAIRS)]
    eb = grp * EXPERTS_PER_GROUP + jnp.asarray(pairs[:, 1])[tb % len(PAIRS)]
    nvalid = jnp.clip(counts[tb] - (tile_ids - tile_start[tb]) * tm, 0, tm)
    nvalid = jnp.where(tile_ids < n_used, nvalid, 0).astype(jnp.int32)
    return (src.reshape(n_tiles, 1, tm), wl.reshape(p_rows, 1), wh.reshape(p_rows, 1),
            ea.astype(jnp.int32), eb.astype(jnp.int32), nvalid, n_tiles)


def _moe_kernel(ea_ref, eb_ref, nv_ref, src_ref, wl_ref, wh_ref, h_hbm,
                w1a_ref, w3a_ref, w2a_ref, w1b_ref, w3b_ref, w2b_ref, lg_ref, lb_ref,
                o_hbm, xbuf, obuf, gsem, ssem):
    i = pl.program_id(0)
    tm = xbuf.shape[0]
    nv = nv_ref[i]

    def row_in(r):
        return pltpu.make_async_copy(h_hbm.at[pl.ds(src_ref[0, 0, r], 1)], xbuf.at[pl.ds(r, 1)], gsem)

    def row_out(r):
        return pltpu.make_async_copy(obuf.at[pl.ds(r, 1)], o_hbm.at[pl.ds(src_ref[0, 0, r], 1)], ssem)

    @pl.when(nv > 0)
    def _():
        def start_in(r, carry):
            row_in(r).start()
            return carry

        def wait_in(r, carry):
            row_in(r).wait()
            return carry

        lax.fori_loop(0, tm, start_in, 0)
        lax.fori_loop(0, tm, wait_in, 0)

        x = xbuf[...]
        xb = x.astype(BF16)

        def expert(w1_ref, w3_ref, w2_ref):
            hmid = jax.nn.silu(_dot(xb, w1_ref[0])) * _dot(xb, w3_ref[0])
            return _dot(hmid.astype(BF16), w2_ref[0])

        y = wl_ref[...] * expert(w1a_ref, w3a_ref, w2a_ref) + wh_ref[...] * expert(w1b_ref, w3b_ref, w2b_ref)
        obuf[...] = _layer_norm(DN_ALPHA * x + y, lg_ref[...], lb_ref[...])

        def start_out(r, carry):
            row_out(r).start()
            return carry

        def wait_out(r, carry):
            row_out(r).wait()
            return carry

        lax.fori_loop(0, nv, start_out, 0)
        lax.fori_loop(0, nv, wait_out, 0)


def _moe(h2d, bucket, w_lo, w_hi, w1, w3, w2, ln_g, ln_b):
    t, d = h2d.shape
    tm = MOE_TM
    src, wl, wh, ea, eb, nvalid, n_tiles = _moe_plan(bucket, w_lo, w_hi, t)
    wspec = lambda shape, which: pl.BlockSpec(
        shape, (lambda i, ea_r, eb_r, nv_r: (ea_r[i], 0, 0)) if which == 0
        else (lambda i, ea_r, eb_r, nv_r: (eb_r[i], 0, 0)))
    w13 = (1, d, D_EXPERT)
    w2s = (1, D_EXPERT, d)
    grid_spec = pltpu.PrefetchScalarGridSpec(
        num_scalar_prefetch=3,
        grid=(n_tiles,),
        in_specs=[pl.BlockSpec((1, 1, tm), lambda i, *_: (i, 0, 0), memory_space=pltpu.SMEM),
                  pl.BlockSpec((tm, 1), lambda i, *_: (i, 0)),
                  pl.BlockSpec((tm, 1), lambda i, *_: (i, 0)),
                  pl.BlockSpec(memory_space=pl.ANY),
                  wspec(w13, 0), wspec(w13, 0), wspec(w2s, 0),
                  wspec(w13, 1), wspec(w13, 1), wspec(w2s, 1),
                  pl.BlockSpec((1, d), lambda i, *_: (0, 0)),
                  pl.BlockSpec((1, d), lambda i, *_: (0, 0))],
        out_specs=pl.BlockSpec(memory_space=pl.ANY),
        scratch_shapes=[pltpu.VMEM((tm, d), F32), pltpu.VMEM((tm, d), F32),
                        pltpu.SemaphoreType.DMA(()), pltpu.SemaphoreType.DMA(())],
    )
    w1b, w3b, w2b = w1.astype(BF16), w3.astype(BF16), w2.astype(BF16)
    return pl.pallas_call(
        _moe_kernel,
        out_shape=jax.ShapeDtypeStruct((t, d), F32),
        grid_spec=grid_spec,
        compiler_params=_cparams(("arbitrary",)),
        name="moe",
    )(ea, eb, nvalid, src, wl, wh, h2d, w1b, w3b, w2b, w1b, w3b, w2b,
      ln_g.reshape(1, d), ln_b.reshape(1, d))


def kernel(x, ln0_g, ln0_b, w_in, pool_w, pool_scale, cmp_k_pe, cmp_k_w1, cmp_k_w2, cmp_v_pe, cmp_v_w1, cmp_v_w2, rel_bias, conv_w, conv_b, dt_bias, a_log, d_skip, ssd_norm_g, w_br_pool, w_br_nsa, w_br_ssd, w_out, ln1_g, ln1_b, router_w, router_b, exp_w1, exp_w3, exp_w2, ln2_g, ln2_b):
    bsz, s, d = x.shape
    assert d == D_MODEL and s % MIX_TS == 0 and s // CMP_STRIDE == LANE and (bsz * s) % PROJ_TM == 0
    t = bsz * s
    nq = s // Q_BLOCK
    tb, bc, cv = _nsa_bias_tables(rel_bias, nq)
    e_exp, ovp = _nsa_consts(s)
    h = _ln(x.reshape(t, d), ln0_g, ln0_b)
    for i in range(DEPTH):
        proj = _proj(h, _prep_w_in(w_in[i]))
        kc, vc = _compress(proj, bsz, s, cmp_k_pe[i], cmp_k_w1[i], cmp_k_w2[i],
                           cmp_v_pe[i], cmp_v_w1[i], cmp_v_w2[i])
        nsa_out = _nsa(proj, kc, vc, tb, bc, cv, e_exp, ovp, bsz, s)
        ssd_y = _ssd(proj, bsz, s, conv_w[i], conv_b[i], dt_bias[i], a_log[i], d_skip[i], ssd_norm_g[i])
        h1 = _mix(proj, nsa_out, ssd_y, h, bsz, s, pool_w[i], pool_scale[i], w_br_pool[i], w_br_nsa[i],
                  w_br_ssd[i], w_out[i], ln1_g[i], ln1_b[i])
        bucket, w_lo, w_hi = _router(h1, router_w, router_b)
        h = _moe(h1, bucket[0], w_lo[0], w_hi[0], exp_w1[i], exp_w3[i], exp_w2[i], ln2_g[i], ln2_b[i])
    return h.reshape(bsz, s, d)
```

```python
import functools
import math

import numpy as np
import jax
import jax.numpy as jnp
from jax import lax
from jax.experimental import pallas as pl
from jax.experimental.pallas import tpu as pltpu

F32 = jnp.float32
BF16 = jnp.bfloat16

D_MODEL = 1024
DEPTH = 2
DN_ALPHA = (2.0 * DEPTH) ** 0.25
LN_EPS = 1e-5
NEG = -1e30
BIG = 1e6

POOL_WINDOWS = (2, 4, 8, 16)
POOL_GROUP = 128
POOL_WIDTH = 512
MAX_POOL_WINDOW = 16

NSA_HEAD_DIM = 64
NSA_KV_HEADS = 2
NSA_HPG = 4
NSA_HEADS = 8
NSA_WIDTH = 512
CMP_LEN = 32
CMP_STRIDE = 16
CMP_HIDDEN = 128
SLC_LEN = 64
SLC_TOPN = 8
WIN_LEN = 512
Q_BLOCK = 128
REL_BUCKETS = 32
REL_MAX_DIST = 128

SSD_D_INNER = 1024
SSD_HEAD_DIM = 64
SSD_HEADS = 16
SSD_GROUPS = 4
SSD_STATE = 128
SSD_CONV = 4
SSD_CHUNK = 128
SSD_XBC = 2048

N_EXPERTS = 16
N_EXPERT_GROUPS = 4
EXPERTS_PER_GROUP = 4
D_EXPERT = 512
PAIRS = ((0, 1), (0, 2), (0, 3), (1, 2), (1, 3), (2, 3))
N_BUCKETS = N_EXPERT_GROUPS * len(PAIRS)

LANE = 128
SUBLANE = 8
VMEM_LIMIT = 48 * 1024 * 1024

C_POOL = 0
C_Q = 512
C_KV = 1024
C_NG = 1792
C_DT = 1920
C_XBC = 2048
C_Z = 4096
C_GATE = 5120
D_PAD = 8192

PROJ_TM = 1024
PROJ_TN = 1024
MIX_TS = 256
ROUTER_TM = 1024
MOE_TM = 256


def _dot(a, b):
    return jnp.dot(a, b, preferred_element_type=F32)


def _dot_nt(a, b):
    return lax.dot_general(a, b, (((1,), (1,)), ((), ())), preferred_element_type=F32)


def _dot_tn(a, b):
    return lax.dot_general(a, b, (((0,), (0,)), ((), ())), preferred_element_type=F32)


def _split3(x):
    x1 = x.astype(BF16)
    r1 = x - x1.astype(F32)
    x2 = r1.astype(BF16)
    x3 = (r1 - x2.astype(F32)).astype(BF16)
    return x1, x2, x3


def _layer_norm(x, g, b):
    mu = jnp.mean(x, -1, keepdims=True)
    xc = x - mu
    var = jnp.mean(xc * xc, -1, keepdims=True)
    return xc * lax.rsqrt(var + LN_EPS) * g + b


def _cparams(sem):
    return pltpu.CompilerParams(dimension_semantics=sem, vmem_limit_bytes=VMEM_LIMIT)


def _ln_kernel(x_ref, g_ref, b_ref, o_ref):
    o_ref[...] = _layer_norm(x_ref[...], g_ref[...], b_ref[...])


def _ln(x2d, g, b):
    t, d = x2d.shape
    tm = 512
    return pl.pallas_call(
        _ln_kernel,
        out_shape=jax.ShapeDtypeStruct((t, d), F32),
        grid=(t // tm,),
        in_specs=[pl.BlockSpec((tm, d), lambda i: (i, 0)),
                  pl.BlockSpec((1, d), lambda i: (0, 0)),
                  pl.BlockSpec((1, d), lambda i: (0, 0))],
        out_specs=pl.BlockSpec((tm, d), lambda i: (i, 0)),
        compiler_params=_cparams(("parallel",)),
        name="ln0",
    )(x2d, g.reshape(1, d), b.reshape(1, d))


def _proj_kernel(h_ref, w_ref, o_ref, hb_ref):
    @pl.when(pl.program_id(1) == 0)
    def _():
        hb_ref[...] = h_ref[...].astype(BF16)

    o_ref[...] = _dot(hb_ref[...], w_ref[...])


def _proj(h2d, w_pad):
    t, d = h2d.shape
    return pl.pallas_call(
        _proj_kernel,
        out_shape=jax.ShapeDtypeStruct((t, D_PAD), F32),
        grid=(t // PROJ_TM, D_PAD // PROJ_TN),
        in_specs=[pl.BlockSpec((PROJ_TM, d), lambda i, j: (i, 0)),
                  pl.BlockSpec((d, PROJ_TN), lambda i, j: (0, j))],
        out_specs=pl.BlockSpec((PROJ_TM, PROJ_TN), lambda i, j: (i, j)),
        scratch_shapes=[pltpu.VMEM((PROJ_TM, d), BF16)],
        compiler_params=_cparams(("parallel", "arbitrary")),
        name="proj",
    )(h2d, w_pad)


def _pad_cols(w, n):
    return jnp.pad(w, ((0, 0), (0, n - w.shape[1])))


def _prep_w_in(w_in):
    d = w_in.shape[0]
    o = 0
    w_pool = w_in[:, o:o + POOL_WIDTH]; o += POOL_WIDTH
    w_q = w_in[:, o:o + NSA_WIDTH]; o += NSA_WIDTH
    w_kv = w_in[:, o:o + 768]; o += 768
    w_ng = w_in[:, o:o + 24]; o += 24
    w_z = w_in[:, o:o + SSD_D_INNER]; o += SSD_D_INNER
    w_xbc = w_in[:, o:o + SSD_XBC]; o += SSD_XBC
    w_dt = w_in[:, o:o + SSD_HEADS]; o += SSD_HEADS
    w_gate = w_in[:, o:o + 3 * D_MODEL]
    w_q = w_q.reshape(d, NSA_KV_HEADS, NSA_HPG, NSA_HEAD_DIM).transpose(0, 2, 1, 3).reshape(d, NSA_WIDTH)
    w = jnp.concatenate([w_pool, w_q, w_kv, _pad_cols(w_ng, LANE), _pad_cols(w_dt, LANE),
                         w_xbc, w_z, w_gate], axis=1)
    return w.astype(BF16)


def _compress_one(tok_ref, pe_ref, w1_ref, w2_ref):
    n16 = tok_ref.shape[0] // CMP_STRIDE
    acc_a = jnp.zeros((n16, 2 * CMP_HIDDEN), F32)
    acc_b = jnp.zeros((n16, 2 * CMP_HIDDEN), F32)
    for l in range(CMP_STRIDE):
        rows = tok_ref[pl.ds(l, n16, stride=CMP_STRIDE), :]
        xa = (rows + pe_ref[l:l + 1, :]).astype(BF16)
        xb = (rows + pe_ref[CMP_STRIDE + l:CMP_STRIDE + l + 1, :]).astype(BF16)
        acc_a = acc_a + _dot(xa, w1_ref[l])
        acc_b = acc_b + _dot(xb, w1_ref[CMP_STRIDE + l])
    hid = acc_a + pltpu.roll(acc_b, n16 - 1, axis=0)
    hid = jax.nn.gelu(hid)
    return _dot(hid.astype(BF16), w2_ref[...])


def _compress_kernel(kt_ref, vt_ref, pek_ref, w1k_ref, w2k_ref, pev_ref, w1v_ref, w2v_ref,
                     kc_ref, vc_ref):
    kc_ref[0] = _compress_one(kt_ref, pek_ref, w1k_ref, w2k_ref).astype(BF16)
    vc_ref[0] = _compress_one(vt_ref, pev_ref, w1v_ref, w2v_ref).T.astype(BF16)


def _blockdiag2(w):
    z = jnp.zeros_like(w)
    return jnp.concatenate([jnp.concatenate([w, z], -1), jnp.concatenate([z, w], -1)], -2)


def _compress(proj2d, bsz, s, pe_k, w1_k, w2_k, pe_v, w1_v, w2_v):
    n16 = s // CMP_STRIDE

    def prep(pe, w1, w2):
        pe2 = jnp.concatenate([pe, pe], axis=-1)
        w1b = _blockdiag2(w1.reshape(CMP_LEN, NSA_HEAD_DIM, CMP_HIDDEN)).astype(BF16)
        w2b = _blockdiag2(w2).astype(BF16)
        return pe2, w1b, w2b

    pk = prep(pe_k, w1_k, w2_k)
    pv = prep(pe_v, w1_v, w2_v)
    full = lambda shape: pl.BlockSpec(shape, lambda b: (0,) * len(shape))
    wspecs = [full((CMP_LEN, LANE)), full((CMP_LEN, LANE, 2 * CMP_HIDDEN)), full((2 * CMP_HIDDEN, LANE))]
    return pl.pallas_call(
        _compress_kernel,
        out_shape=(jax.ShapeDtypeStruct((bsz, n16, LANE), BF16),
                   jax.ShapeDtypeStruct((bsz, n16, LANE), BF16)),
        grid=(bsz,),
        in_specs=[pl.BlockSpec((s, LANE), lambda b: (b, C_KV // LANE)),
                  pl.BlockSpec((s, LANE), lambda b: (b, C_KV // LANE + 1))] + wspecs + wspecs,
        out_specs=(pl.BlockSpec((1, n16, LANE), lambda b: (b, 0, 0)),
                   pl.BlockSpec((1, n16, LANE), lambda b: (b, 0, 0))),
        compiler_params=_cparams(("parallel",)),
        name="nsa_compress",
    )(proj2d, proj2d, *pk, *pv)


def _rel_bucket(dist):
    n = jnp.maximum(dist, 0)
    max_exact = REL_BUCKETS // 2
    nf = jnp.maximum(n, 1).astype(F32)
    large = max_exact + (jnp.log(nf / max_exact) / math.log(REL_MAX_DIST / max_exact)
                         * (REL_BUCKETS - max_exact)).astype(jnp.int32)
    large = jnp.minimum(large, REL_BUCKETS - 1)
    return jnp.where(n < max_exact, n, large)


WIN_BACK = WIN_LEN // Q_BLOCK
WIN_KEYS = (WIN_BACK + 1) * Q_BLOCK
WIN_STRIP_TILES = 2 * WIN_BACK + 1
CMP_BIAS_ROWS = 2 * LANE
SEL_SUPER = 2 * Q_BLOCK


def _nsa_bias_tables(rel_bias):
    tbl = rel_bias.astype(F32)

    def lookup(dist):
        onehot = (_rel_bucket(jnp.asarray(dist))[..., None] == jnp.arange(REL_BUCKETS)).astype(F32)
        return jnp.einsum("...b,bh->...h", onehot, tbl, precision=lax.Precision.HIGHEST)

    hq = NSA_HPG * Q_BLOCK
    k = np.arange(Q_BLOCK)[None, :, None]
    q = np.arange(Q_BLOCK)[None, None, :]
    d = np.arange(WIN_BACK, -WIN_BACK - 1, -1)[:, None, None]
    ws = lookup(d * Q_BLOCK + q - k) - tbl[REL_BUCKETS - 1]
    ws = ws.reshape(WIN_STRIP_TILES, Q_BLOCK, Q_BLOCK, NSA_KV_HEADS, NSA_HPG).transpose(3, 0, 1, 4, 2)
    ws = ws.reshape(NSA_KV_HEADS, WIN_STRIP_TILES, Q_BLOCK, hq)
    tb = ws[:, WIN_BACK - 2:WIN_BACK + 1][:, ::-1]
    ws = ws.reshape(NSA_KV_HEADS, WIN_STRIP_TILES * Q_BLOCK, hq)
    r = np.arange(CMP_BIAS_ROWS)[:, None]
    bc = lookup(q[0] - CMP_STRIDE * (r - LANE) - (CMP_LEN - 1))
    bc = bc.reshape(CMP_BIAS_ROWS, Q_BLOCK, NSA_KV_HEADS, NSA_HPG).transpose(2, 0, 3, 1)
    bc = bc.reshape(NSA_KV_HEADS, CMP_BIAS_ROWS, hq)
    return tb, ws, bc


def _nsa_consts(s):
    n_slc = s // SLC_LEN
    et = (np.arange(s)[:, None] // SLC_LEN == np.arange(LANE)[None, :]).astype(np.float32)
    n_cmp = (s - CMP_LEN) // CMP_STRIDE + 1
    c0 = np.arange(n_cmp) * CMP_STRIDE
    s0 = np.arange(n_slc) * SLC_LEN
    ov = np.clip(np.minimum(c0[None, :] + CMP_LEN, s0[:, None] + SLC_LEN)
                 - np.maximum(c0[None, :], s0[:, None]), 0, None) / CMP_LEN
    ovt = np.zeros((n_slc, LANE), np.float32)
    ovt[:, :n_cmp] = ov
    return jnp.asarray(et, BF16), jnp.asarray(ovt, BF16)


def _nsa_kernel(q_ref, ng_ref, ks_ref, vs_ref, kw_ref, vw_ref, kc_ref, vct_ref,
                tb_ref, ws_ref, bc_ref, et_ref, ovt_ref, o_ref,
                ksb_ref, kwb_ref, vst_ref, vwt_ref, msel_ref, *, nq, n_slc):
    qi = pl.program_id(1)
    hq = NSA_HPG * Q_BLOCK
    dh = NSA_HEAD_DIM

    @pl.when(qi == 0)
    def _():
        ksb_ref[...] = ks_ref[...].astype(BF16)
        kwb_ref[...] = kw_ref[...].astype(BF16)
        for kt in range(nq):
            sl = slice(kt * Q_BLOCK, (kt + 1) * Q_BLOCK)
            vst_ref[:, sl] = vs_ref[sl, :].T.astype(BF16)
            vwt_ref[:, sl] = vw_ref[sl, :].T.astype(BF16)

    laneq = lax.broadcasted_iota(jnp.int32, (Q_BLOCK, LANE), 1)
    rowk = lax.broadcasted_iota(jnp.int32, (Q_BLOCK, LANE), 0)
    tq = qi * Q_BLOCK + laneq
    row_lo = lax.broadcasted_iota(jnp.int32, (LANE, hq), 0) < dh
    groups = range(NSA_KV_HEADS)
    gsl = [slice(g * dh, (g + 1) * dh) for g in groups]

    def heads4(m):
        return jnp.concatenate([m] * NSA_HPG, axis=1)

    def key_pos(start, n):
        return start + lax.broadcasted_iota(jnp.int32, (n, LANE), 0)

    gates_t = jax.nn.sigmoid(ng_ref[...]).T
    q_t = jnp.concatenate(
        [(q_ref[:, h * LANE:(h + 1) * LANE] * (dh ** -0.5)).T for h in range(NSA_HPG)], axis=1)
    qg_t = [jnp.where(row_lo if g == 0 else jnp.logical_not(row_lo), q_t, 0.0).astype(BF16) for g in groups]

    mask_c = heads4((tq - (CMP_STRIDE * rowk + CMP_LEN - 1)) >= 0)
    mask_cf = mask_c.astype(F32)
    boff = pl.multiple_of(LANE - SUBLANE * qi, SUBLANE)
    jrow = lax.broadcasted_iota(jnp.int32, (n_slc, LANE), 0)
    tj = qi * Q_BLOCK + lax.broadcasted_iota(jnp.int32, (n_slc, LANE), 1)
    jt = lax.shift_right_logical(tj, SLC_LEN.bit_length() - 1)
    forced = (jrow == 0) | (jrow == jt) | (jrow == jt - 1)
    valid = jrow * SLC_LEN <= tj
    o_c = []
    for g in groups:
        s = _dot(kc_ref[0], qg_t[g]) + bc_ref[g, pl.ds(boff, LANE), :]
        s = jnp.where(mask_c, s, NEG)
        mx = jnp.max(s, axis=0, keepdims=True)
        p = jnp.exp(s - mx) * mask_cf
        lsum = jnp.sum(p, axis=0, keepdims=True)
        pb = (p * jnp.where(lsum > 0, 1.0 / lsum, 0.0)).astype(BF16)
        o_c.append(_dot(vct_ref[0, gsl[g], :], pb))
        imp4 = _dot(ovt_ref[...], pb)
        imp = imp4[:, 0:LANE]
        for h in range(1, NSA_HPG):
            imp = imp + imp4[:, h * LANE:(h + 1) * LANE]
        val = jnp.where(forced, BIG, jnp.where(valid, imp, -BIG))
        rank = jnp.zeros((n_slc, LANE), F32)
        for i in range(n_slc):
            ci = val[i:i + 1, :]
            rank = rank + jnp.where(ci > val, 1.0, 0.0) + jnp.where(ci == val, (jrow > i).astype(F32), 0.0)
        sel_t = jnp.where(rank < SLC_TOPN, 1.0, 0.0)
        sel_t = jnp.concatenate([sel_t, jnp.zeros((LANE - n_slc, LANE), F32)], axis=0).astype(BF16)
        msel_ref[g] = _dot(et_ref[...], sel_t)

    n_super = (qi + 2) // 2

    def sel_step(st, carry, near):
        off = pl.multiple_of(st * SEL_SUPER, SEL_SUPER)
        k_bf = ksb_ref[pl.ds(off, SEL_SUPER), :]
        out = []
        for g in groups:
            m, l, acc = carry[g]
            s = _dot(k_bf, qg_t[g])
            mask = msel_ref[g, pl.ds(off, SEL_SUPER), :] > 0.5
            if near:
                da = qi - 2 * st
                s = s + jnp.concatenate([tb_ref[g, jnp.clip(da, 0, 2)], tb_ref[g, jnp.clip(da - 1, 0, 2)]], axis=0)
                mask = mask & (key_pos(off, SEL_SUPER) <= jnp.concatenate([tq, tq], axis=0))
            s = jnp.where(heads4(mask), s, NEG)
            m_new = jnp.maximum(m, jnp.max(s, axis=0, keepdims=True))
            alpha = jnp.exp(m - m_new)
            p = jnp.exp(s - m_new)
            l = alpha * l + jnp.sum(p, axis=0, keepdims=True)
            acc = alpha * acc + _dot(vst_ref[gsl[g], pl.ds(off, SEL_SUPER)], p.astype(BF16))
            out.append((m_new, l, acc))
        return tuple(out)

    init = tuple((jnp.full((1, hq), NEG, F32), jnp.zeros((1, hq), F32), jnp.zeros((dh, hq), F32))
                 for _ in groups)
    n_near = jnp.minimum(n_super, 2)
    carry = lax.fori_loop(0, n_near, lambda it, c: sel_step(n_super - 1 - it, c, True), init)
    carry = lax.fori_loop(0, n_super - n_near, lambda it, c: sel_step(it, c, False), carry)
    o_s = [carry[g][2] * (1.0 / carry[g][1]) for g in groups]

    w0 = jnp.maximum(qi - WIN_BACK, 0)
    woff = pl.multiple_of(w0 * Q_BLOCK, Q_BLOCK)
    boff_w = pl.multiple_of((WIN_BACK - (qi - w0)) * Q_BLOCK, Q_BLOCK)
    dist_w = jnp.concatenate([tq] * (WIN_BACK + 1), axis=0) - key_pos(woff, WIN_KEYS)
    mask_w = heads4((dist_w >= 0) & (dist_w < WIN_LEN))
    kw_bf = kwb_ref[pl.ds(woff, WIN_KEYS), :]
    o_w = []
    for g in groups:
        s = _dot(kw_bf, qg_t[g]) + ws_ref[g, pl.ds(boff_w, WIN_KEYS), :]
        s = jnp.where(mask_w, s, NEG)
        p = jnp.exp(s - jnp.max(s, axis=0, keepdims=True))
        lsum = jnp.sum(p, axis=0, keepdims=True)
        o_w.append(_dot(vwt_ref[gsl[g], pl.ds(woff, WIN_KEYS)], p.astype(BF16)) * (1.0 / lsum))

    for h in range(NSA_HPG):
        hl = slice(h * LANE, (h + 1) * LANE)
        halves = []
        for g in groups:
            c = g * NSA_HPG + h
            halves.append(gates_t[c:c + 1, :] * o_c[g][:, hl]
                          + gates_t[NSA_HEADS + c:NSA_HEADS + c + 1, :] * o_s[g][:, hl]
                          + gates_t[2 * NSA_HEADS + c:2 * NSA_HEADS + c + 1, :] * o_w[g][:, hl])
        o_ref[:, hl] = jnp.concatenate(halves, axis=0).T


def _nsa(proj2d, kc, vct, tb, ws, bc, et, ovt, bsz, s):
    nq = s // Q_BLOCK
    n_slc = s // SLC_LEN
    hq = NSA_HPG * Q_BLOCK
    kvb = C_KV // LANE
    kv_spec = lambda blk: pl.BlockSpec((s, LANE), lambda b, qi: (b, kvb + blk))
    full = lambda shape: pl.BlockSpec(shape, lambda b, qi: (0,) * len(shape))
    return pl.pallas_call(
        functools.partial(_nsa_kernel, nq=nq, n_slc=n_slc),
        out_shape=jax.ShapeDtypeStruct((bsz * s, NSA_WIDTH), F32),
        grid=(bsz, nq),
        in_specs=[pl.BlockSpec((Q_BLOCK, NSA_WIDTH), lambda b, qi: (b * nq + qi, C_Q // NSA_WIDTH)),
                  pl.BlockSpec((Q_BLOCK, LANE), lambda b, qi: (b * nq + qi, C_NG // LANE)),
                  kv_spec(2), kv_spec(3), kv_spec(4), kv_spec(5),
                  pl.BlockSpec((1, s // CMP_STRIDE, LANE), lambda b, qi: (b, 0, 0)),
                  pl.BlockSpec((1, LANE, s // CMP_STRIDE), lambda b, qi: (b, 0, 0)),
                  full((NSA_KV_HEADS, 3, Q_BLOCK, hq)),
                  full((NSA_KV_HEADS, WIN_STRIP_TILES * Q_BLOCK, hq)),
                  full((NSA_KV_HEADS, CMP_BIAS_ROWS, hq)),
                  full((s, LANE)),
                  full((n_slc, LANE))],
        out_specs=pl.BlockSpec((Q_BLOCK, NSA_WIDTH), lambda b, qi: (b * nq + qi, 0)),
        scratch_shapes=[pltpu.VMEM((s, LANE), BF16), pltpu.VMEM((s, LANE), BF16),
                        pltpu.VMEM((LANE, s), BF16), pltpu.VMEM((LANE, s), BF16),
                        pltpu.VMEM((NSA_KV_HEADS, s, LANE), F32)],
        compiler_params=_cparams(("parallel", "arbitrary")),
        name="nsa_attn",
    )(proj2d, proj2d, proj2d, proj2d, proj2d, proj2d, kc, vct, tb, ws, bc, et, ovt)


def _ssd_kernel(xbc_ref, prev_ref, z_ref, dt_ref, cw_ref, cb_ref, dtb_ref, alog_ref, alogx_ref,
                dskip_ref, ng_ref, e16_ref, tril_ref, o_ref, xcat_ref, state_ref):
    c = pl.program_id(1)
    L = SSD_CHUNK
    pair_w = 2 * SSD_HEAD_DIM

    @pl.when(c == 0)
    def _():
        state_ref[...] = jnp.zeros(state_ref.shape, F32)

    xcat_ref[0:SUBLANE, :] = jnp.where(c > 0, prev_ref[...], 0.0)
    xcat_ref[SUBLANE:, :] = xbc_ref[...]
    conv = cb_ref[...]
    for k in range(SSD_CONV):
        conv = conv + cw_ref[k:k + 1, :] * xcat_ref[pl.ds(SUBLANE - (SSD_CONV - 1) + k, L), :]
    xbc = jax.nn.silu(conv)
    xs = xbc[:, :SSD_D_INNER]
    bmat = xbc[:, SSD_D_INNER:SSD_D_INNER + SSD_GROUPS * SSD_STATE]
    cmat = xbc[:, SSD_D_INNER + SSD_GROUPS * SSD_STATE:]

    dt = jax.nn.softplus(dt_ref[...] + dtb_ref[...])
    tril = tril_ref[...]
    e16 = e16_ref[...]
    d1, d2, d3 = _split3(dt)
    dt_x = _dot(d1, e16) + _dot(d2, e16) + _dot(d3, e16)
    a_c = dt * (-jnp.exp(alog_ref[...]))
    a_x = dt_x * (-jnp.exp(alogx_ref[...]))
    c1, c2, c3 = _split3(a_c)
    acs = _dot(tril, c1) + _dot(tril, c2) + _dot(tril, c3)
    x1, x2, x3 = _split3(a_x)
    acs_x = _dot(tril, x1) + _dot(tril, x2) + _dot(tril, x3)
    acs_t = acs.T
    last = acs_x[L - 1:L, :]
    eacs = jnp.exp(acs_x)
    decay_in = jnp.exp(last - acs_x)
    cdec = jnp.exp(last)
    xdt = xs * dt_x
    wst = xdt * decay_in

    row = lax.broadcasted_iota(jnp.int32, (L, L), 0)
    lane = lax.broadcasted_iota(jnp.int32, (L, L), 1)
    tri = row >= lane
    lo = lane < SSD_HEAD_DIM

    ys = []
    for g in range(SSD_GROUPS):
        cg = cmat[:, g * SSD_STATE:(g + 1) * SSD_STATE].astype(BF16)
        bg = bmat[:, g * SSD_STATE:(g + 1) * SSD_STATE].astype(BF16)
        cb = _dot_nt(cg, bg)
        for pr in range(2):
            pp = g * 2 + pr
            ls = slice(pp * pair_w, (pp + 1) * pair_w)
            xp = xdt[:, ls]
            yd = None
            for e in range(2):
                hd = 2 * pp + e
                diff = acs[:, hd:hd + 1] - acs_t[hd:hd + 1, :]
                seg = jnp.exp(jnp.where(tri, diff, NEG))
                pm = (cb * seg).astype(BF16)
                xe = jnp.where(lo if e == 0 else jnp.logical_not(lo), xp, 0.0).astype(BF16)
                term = _dot(pm, xe)
                yd = term if yd is None else yd + term
            st = state_ref[pp]
            yo = _dot(cg, st.astype(BF16)) * eacs[:, ls]
            ys.append(yd + yo)
            state_ref[pp] = cdec[:, ls] * st + _dot_tn(bg, wst[:, ls].astype(BF16))

    y = jnp.concatenate(ys, axis=1) + xs * dskip_ref[...]
    y = y * jax.nn.silu(z_ref[...])
    gw = SSD_D_INNER // SSD_GROUPS
    outs = []
    for gi in range(SSD_GROUPS):
        yg = y[:, gi * gw:(gi + 1) * gw]
        outs.append(yg * lax.rsqrt(jnp.mean(yg * yg, -1, keepdims=True) + LN_EPS))
    o_ref[...] = jnp.concatenate(outs, axis=1) * ng_ref[...]


def _ssd(proj2d, bsz, s, conv_w, conv_b, dt_bias, a_log, d_skip, norm_g):
    nc = s // SSD_CHUNK
    L = SSD_CHUNK
    pad16 = lambda v: jnp.pad(v.reshape(1, SSD_HEADS), ((0, 0), (0, LANE - SSD_HEADS)))
    rep64 = lambda v: jnp.repeat(v, SSD_HEAD_DIM).reshape(1, SSD_D_INNER)
    e16 = np.zeros((LANE, SSD_D_INNER), np.float32)
    e16[np.arange(SSD_D_INNER) // SSD_HEAD_DIM, np.arange(SSD_D_INNER)] = 1.0
    tril = np.tril(np.ones((L, L), np.float32))
    full = lambda shape: pl.BlockSpec(shape, lambda b, c: (0,) * len(shape))
    rows8 = s // SUBLANE
    return pl.pallas_call(
        _ssd_kernel,
        out_shape=jax.ShapeDtypeStruct((bsz * s, SSD_D_INNER), F32),
        grid=(bsz, nc),
        in_specs=[pl.BlockSpec((L, SSD_XBC), lambda b, c: (b * nc + c, C_XBC // SSD_XBC)),
                  pl.BlockSpec((SUBLANE, SSD_XBC),
                               lambda b, c: (jnp.maximum(b * rows8 + c * (L // SUBLANE) - 1, 0), C_XBC // SSD_XBC)),
                  pl.BlockSpec((L, SSD_D_INNER), lambda b, c: (b * nc + c, C_Z // SSD_D_INNER)),
                  pl.BlockSpec((L, LANE), lambda b, c: (b * nc + c, C_DT // LANE)),
                  full((SSD_CONV, SSD_XBC)), full((1, SSD_XBC)), full((1, LANE)), full((1, LANE)),
                  full((1, SSD_D_INNER)), full((1, SSD_D_INNER)), full((1, SSD_D_INNER)),
                  full((LANE, SSD_D_INNER)), full((L, L))],
        out_specs=pl.BlockSpec((L, SSD_D_INNER), lambda b, c: (b * nc + c, 0)),
        scratch_shapes=[pltpu.VMEM((L + SUBLANE, SSD_XBC), F32),
                        pltpu.VMEM((SSD_HEADS // 2, SSD_STATE, 2 * SSD_HEAD_DIM), F32)],
        compiler_params=_cparams(("parallel", "arbitrary")),
        name="ssd",
    )(proj2d, proj2d, proj2d, proj2d, conv_w, conv_b.reshape(1, SSD_XBC), pad16(dt_bias), pad16(a_log),
      rep64(a_log), rep64(d_skip), norm_g.reshape(1, SSD_D_INNER),
      jnp.asarray(e16, BF16), jnp.asarray(tril, BF16))


def _mix_kernel(up_ref, upprev_ref, nsa_ref, ssd_ref, ga_ref, gb_ref, gc_ref, h_ref,
                pw_ref, ps_ref, wbp_ref, wbn_ref, wbs_ref, wo_ref, lg_ref, lb_ref, o_ref, *, ts):
    si = pl.program_id(1)
    w0 = MAX_POOL_WINDOW
    prev = jnp.where(si > 0, upprev_ref[...], 0.0)
    xcat = jnp.concatenate([prev, up_ref[...]], axis=0)
    tpos = si * ts + lax.broadcasted_iota(jnp.int32, (ts, POOL_GROUP), 0)
    ygs = []
    for gi, w in enumerate(POOL_WINDOWS):
        x = xcat[:, gi * POOL_GROUP:(gi + 1) * POOL_GROUP]
        acc = x
        d = 1
        while d < w:
            acc = acc + pltpu.roll(acc, d, axis=0)
            d *= 2
        cnt = jnp.minimum(tpos + 1, w).astype(F32)
        r = acc[w0:, :] / cnt - x[w0:, :]
        ygs.append(_dot(r.astype(BF16), pw_ref[gi]))
    y = jnp.concatenate(ygs, axis=1) * ps_ref[...]
    br_a = _dot(y.astype(BF16), wbp_ref[...])
    br_b = _dot(nsa_ref[...].astype(BF16), wbn_ref[...])
    br_c = _dot(ssd_ref[...].astype(BF16), wbs_ref[...])
    mixed = (jax.nn.sigmoid(ga_ref[...]) * br_a + jax.nn.sigmoid(gb_ref[...]) * br_b
             + jax.nn.sigmoid(gc_ref[...]) * br_c)
    mix = _dot(mixed.astype(BF16), wo_ref[...])
    o_ref[...] = _layer_norm(DN_ALPHA * h_ref[...] + mix, lg_ref[...], lb_ref[...])


def _mix(proj2d, nsa_out, ssd_y, h2d, bsz, s, pool_w, pool_scale, w_br_pool, w_br_nsa, w_br_ssd,
         w_out, ln_g, ln_b):
    ts = MIX_TS
    ns = s // ts
    d = D_MODEL
    w0 = MAX_POOL_WINDOW
    wbn = w_br_nsa.reshape(NSA_KV_HEADS, NSA_HPG, NSA_HEAD_DIM, d).transpose(1, 0, 2, 3).reshape(NSA_WIDTH, d)
    full = lambda shape: pl.BlockSpec(shape, lambda b, si: (0,) * len(shape))
    row = lambda width, cb: pl.BlockSpec((ts, width), lambda b, si: (b * ns + si, cb))
    return pl.pallas_call(
        functools.partial(_mix_kernel, ts=ts),
        out_shape=jax.ShapeDtypeStruct((bsz * s, d), F32),
        grid=(bsz, ns),
        in_specs=[row(POOL_WIDTH, C_POOL // POOL_WIDTH),
                  pl.BlockSpec((w0, POOL_WIDTH),
                               lambda b, si: (jnp.maximum((b * s + si * ts) // w0 - 1, 0), C_POOL // POOL_WIDTH)),
                  row(NSA_WIDTH, 0), row(d, 0),
                  row(d, C_GATE // d), row(d, C_GATE // d + 1), row(d, C_GATE // d + 2),
                  row(d, 0),
                  full((len(POOL_WINDOWS), POOL_GROUP, POOL_GROUP)), full((1, POOL_WIDTH)),
                  full((POOL_WIDTH, d)), full((NSA_WIDTH, d)), full((d, d)), full((d, d)),
                  full((1, d)), full((1, d))],
        out_specs=pl.BlockSpec((ts, d), lambda b, si: (b * ns + si, 0)),
        compiler_params=_cparams(("parallel", "arbitrary")),
        name="mix",
    )(proj2d, proj2d, nsa_out, ssd_y, proj2d, proj2d, proj2d, h2d,
      pool_w.astype(BF16), pool_scale.reshape(1, POOL_WIDTH), w_br_pool.astype(BF16), wbn.astype(BF16),
      w_br_ssd.astype(BF16), w_out.astype(BF16), ln_g.reshape(1, d), ln_b.reshape(1, d))


def _router_kernel(h_ref, rw_ref, rb_ref, bucket_ref, wlo_ref, whi_ref):
    logits = _dot_nt(rw_ref[...], h_ref[...].astype(BF16))
    lg = [logits[e:e + 1, :] for e in range(N_EXPERTS)]
    mx = lg[0]
    for e in range(1, N_EXPERTS):
        mx = jnp.maximum(mx, lg[e])
    ex = [jnp.exp(v - mx) for v in lg]
    den = ex[0]
    for e in range(1, N_EXPERTS):
        den = den + ex[e]
    probs = [v / den for v in ex]
    score = [probs[e] + rb_ref[e:e + 1, :] for e in range(N_EXPERTS)]

    def group_vals(vals, grp):
        out = []
        for k in range(EXPERTS_PER_GROUP):
            v = vals[k]
            for gi in range(1, N_EXPERT_GROUPS):
                v = jnp.where(grp == gi, vals[gi * EXPERTS_PER_GROUP + k], v)
            out.append(v)
        return out

    gscore = []
    for gi in range(N_EXPERT_GROUPS):
        sc = score[gi * EXPERTS_PER_GROUP:(gi + 1) * EXPERTS_PER_GROUP]
        best = None
        for a, b in PAIRS:
            v = sc[a] + sc[b]
            best = v if best is None else jnp.maximum(best, v)
        gscore.append(best)
    grp = jnp.zeros(gscore[0].shape, jnp.int32)
    best = gscore[0]
    for gi in range(1, N_EXPERT_GROUPS):
        better = gscore[gi] > best
        grp = jnp.where(better, gi, grp)
        best = jnp.where(better, gscore[gi], best)

    sc = group_vals(score, grp)
    pr = group_vals(probs, grp)
    first = jnp.zeros(grp.shape, jnp.int32)
    fv = sc[0]
    for k in range(1, EXPERTS_PER_GROUP):
        better = sc[k] > fv
        first = jnp.where(better, k, first)
        fv = jnp.where(better, sc[k], fv)
    second = jnp.full(grp.shape, -1, jnp.int32)
    sv = jnp.full(fv.shape, -jnp.inf, F32)
    for k in range(EXPERTS_PER_GROUP):
        better = (first != k) & ((sc[k] > sv) | (second < 0))
        second = jnp.where(better, k, second)
        sv = jnp.where(better, sc[k], sv)
    lo = jnp.minimum(first, second)
    hi = jnp.maximum(first, second)
    p_lo = pr[0]
    p_hi = pr[0]
    for k in range(1, EXPERTS_PER_GROUP):
        p_lo = jnp.where(lo == k, pr[k], p_lo)
        p_hi = jnp.where(hi == k, pr[k], p_hi)
    tot = p_lo + p_hi
    pair = jnp.zeros(grp.shape, jnp.int32)
    for pi, (a, b) in enumerate(PAIRS):
        pair = jnp.where((lo == a) & (hi == b), pi, pair)
    bucket_ref[...] = grp * len(PAIRS) + pair
    wlo_ref[...] = p_lo / tot
    whi_ref[...] = p_hi / tot


def _router(h2d, router_w, router_b):
    t, d = h2d.shape
    tm = ROUTER_TM
    rw = jnp.pad(router_w.T, ((0, LANE - N_EXPERTS), (0, 0))).astype(BF16)
    rb = jnp.broadcast_to(jnp.pad(router_b, (0, LANE - N_EXPERTS))[:, None], (LANE, tm))
    vec = lambda dt: jax.ShapeDtypeStruct((1, t), dt)
    return pl.pallas_call(
        _router_kernel,
        out_shape=(vec(jnp.int32), vec(F32), vec(F32)),
        grid=(t // tm,),
        in_specs=[pl.BlockSpec((tm, d), lambda i: (i, 0)),
                  pl.BlockSpec((LANE, d), lambda i: (0, 0)),
                  pl.BlockSpec((LANE, tm), lambda i: (0, 0))],
        out_specs=(pl.BlockSpec((1, tm), lambda i: (0, i)),) * 3,
        compiler_params=_cparams(("parallel",)),
        name="router",
    )(h2d, rw, rb)


def _moe_plan(bucket, w_lo, w_hi, t):
    tm = MOE_TM
    n_tiles = t // tm + N_BUCKETS
    p_rows = n_tiles * tm
    onehot = (bucket[:, None] == jnp.arange(N_BUCKETS)[None, :]).astype(jnp.int32)
    csum = jnp.cumsum(onehot, axis=0)
    counts = csum[-1]
    rank = jnp.sum(onehot * csum, axis=1) - 1
    tiles_per = (counts + tm - 1) // tm
    tile_end = jnp.cumsum(tiles_per)
    tile_start = tile_end - tiles_per
    dest = tile_start[bucket] * tm + rank
    src = jnp.zeros((p_rows,), jnp.int32).at[dest].set(jnp.arange(t, dtype=jnp.int32), unique_indices=True)
    row_w = jnp.stack([w_lo, w_hi], axis=1)[src]
    wl, wh = row_w[:, 0], row_w[:, 1]
    n_used = tile_end[-1]
    tile_ids = jnp.arange(n_tiles)
    tb = jnp.searchsorted(tile_end, tile_ids, side="right").astype(jnp.int32)
    tb = jnp.where(tile_ids < n_used, tb, tb[jnp.maximum(n_used - 1, 0)])
    tb = jnp.minimum(tb, N_BUCKETS - 1)
    pairs = np.asarray(PAIRS, np.int32)
    grp = tb // len(PAIRS)
    ea = grp * EXPERTS_PER_GROUP + jnp.asarray(pairs[:, 0])[tb % len(PAIRS)]
    eb = grp * EXPERTS_PER_GROUP + jnp.asarray(pairs[:, 1])[tb % len(PAIRS)]
    nvalid = jnp.clip(counts[tb] - (tile_ids - tile_start[tb]) * tm, 0, tm)
    nvalid = jnp.where(tile_ids < n_used, nvalid, 0).astype(jnp.int32)
    return (src.reshape(n_tiles, 1, tm), wl.reshape(p_rows, 1), wh.reshape(p_rows, 1),
            ea.astype(jnp.int32), eb.astype(jnp.int32), nvalid, n_tiles)


DMA_UNROLL = 8


def _for_rows(n, fn):
    nfull = n // DMA_UNROLL

    def group(c, carry):
        for u in range(DMA_UNROLL):
            fn(c * DMA_UNROLL + u)
        return carry

    def single(r, carry):
        fn(r)
        return carry

    lax.fori_loop(0, nfull, group, 0)
    lax.fori_loop(nfull * DMA_UNROLL, n, single, 0)


def _moe_kernel(ea_ref, eb_ref, nv_ref, src_ref, srcn_ref, wl_ref, wh_ref, h_hbm,
                w1a_ref, w3a_ref, w2a_ref, w1b_ref, w3b_ref, w2b_ref, lg_ref, lb_ref,
                o_hbm, xbuf, obuf, gsem, ssem):
    i = pl.program_id(0)
    n = pl.num_programs(0)
    tm = xbuf.shape[1]
    slot = lax.rem(i, 2)
    nslot = 1 - slot
    nv = nv_ref[i]

    def gather_start(idx_ref, sl):
        def one(r):
            pltpu.make_async_copy(h_hbm.at[pl.ds(idx_ref[0, 0, r], 1)], xbuf.at[sl, pl.ds(r, 1)],
                                  gsem.at[sl]).start()
        _for_rows(tm, one)

    def gather_wait(sl):
        def one(r):
            pltpu.make_async_copy(h_hbm.at[pl.ds(0, 1)], xbuf.at[sl, pl.ds(r, 1)], gsem.at[sl]).wait()
        _for_rows(tm, one)

    def scatter_start(count, sl):
        def one(r):
            pltpu.make_async_copy(obuf.at[sl, pl.ds(r, 1)], o_hbm.at[pl.ds(src_ref[0, 0, r], 1)],
                                  ssem.at[sl]).start()
        _for_rows(count, one)

    def scatter_wait(count, sl):
        def one(r):
            pltpu.make_async_copy(obuf.at[sl, pl.ds(r, 1)], o_hbm.at[pl.ds(0, 1)], ssem.at[sl]).wait()
        _for_rows(count, one)

    @pl.when((i == 0) & (nv > 0))
    def _():
        gather_start(src_ref, slot)

    nv_next = nv_ref[jnp.minimum(i + 1, n - 1)]

    @pl.when((i + 1 < n) & (nv_next > 0))
    def _():
        gather_start(srcn_ref, nslot)

    nv_prev2 = nv_ref[jnp.maximum(i - 2, 0)]

    @pl.when((i >= 2) & (nv_prev2 > 0))
    def _():
        scatter_wait(nv_prev2, slot)

    @pl.when(nv > 0)
    def _():
        gather_wait(slot)
        x = xbuf[slot]
        xb = x.astype(BF16)

        def expert(w1_ref, w3_ref, w2_ref):
            hmid = jax.nn.silu(_dot(xb, w1_ref[0])) * _dot(xb, w3_ref[0])
            return _dot(hmid.astype(BF16), w2_ref[0])

        y = wl_ref[...] * expert(w1a_ref, w3a_ref, w2a_ref) + wh_ref[...] * expert(w1b_ref, w3b_ref, w2b_ref)
        obuf[slot] = _layer_norm(DN_ALPHA * x + y, lg_ref[...], lb_ref[...])
        scatter_start(nv, slot)

    @pl.when(i == n - 1)
    def _():
        nv_prev1 = nv_ref[jnp.maximum(i - 1, 0)]

        @pl.when((i >= 1) & (nv_prev1 > 0))
        def _():
            scatter_wait(nv_prev1, nslot)

        @pl.when(nv > 0)
        def _():
            scatter_wait(nv, slot)


def _moe(h2d, bucket, w_lo, w_hi, w1, w3, w2, ln_g, ln_b):
    t, d = h2d.shape
    tm = MOE_TM
    src, wl, wh, ea, eb, nvalid, n_tiles = _moe_plan(bucket, w_lo, w_hi, t)
    wspec = lambda shape, which: pl.BlockSpec(
        shape, (lambda i, ea_r, eb_r, nv_r: (ea_r[i], 0, 0)) if which == 0
        else (lambda i, ea_r, eb_r, nv_r: (eb_r[i], 0, 0)))
    w13 = (1, d, D_EXPERT)
    w2s = (1, D_EXPERT, d)
    grid_spec = pltpu.PrefetchScalarGridSpec(
        num_scalar_prefetch=3,
        grid=(n_tiles,),
        in_specs=[pl.BlockSpec((1, 1, tm), lambda i, *_: (i, 0, 0), memory_space=pltpu.SMEM),
                  pl.BlockSpec((1, 1, tm), lambda i, *_: (jnp.minimum(i + 1, n_tiles - 1), 0, 0),
                               memory_space=pltpu.SMEM),
                  pl.BlockSpec((tm, 1), lambda i, *_: (i, 0)),
                  pl.BlockSpec((tm, 1), lambda i, *_: (i, 0)),
                  pl.BlockSpec(memory_space=pl.ANY),
                  wspec(w13, 0), wspec(w13, 0), wspec(w2s, 0),
                  wspec(w13, 1), wspec(w13, 1), wspec(w2s, 1),
                  pl.BlockSpec((1, d), lambda i, *_: (0, 0)),
                  pl.BlockSpec((1, d), lambda i, *_: (0, 0))],
        out_specs=pl.BlockSpec(memory_space=pl.ANY),
        scratch_shapes=[pltpu.VMEM((2, tm, d), F32), pltpu.VMEM((2, tm, d), F32),
                        pltpu.SemaphoreType.DMA((2,)), pltpu.SemaphoreType.DMA((2,))],
    )
    w1b, w3b, w2b = w1.astype(BF16), w3.astype(BF16), w2.astype(BF16)
    return pl.pallas_call(
        _moe_kernel,
        out_shape=jax.ShapeDtypeStruct((t, d), F32),
        grid_spec=grid_spec,
        compiler_params=_cparams(("arbitrary",)),
        name="moe",
    )(ea, eb, nvalid, src, src, wl, wh, h2d, w1b, w3b, w2b, w1b, w3b, w2b,
      ln_g.reshape(1, d), ln_b.reshape(1, d))


def kernel(x, ln0_g, ln0_b, w_in, pool_w, pool_scale, cmp_k_pe, cmp_k_w1, cmp_k_w2, cmp_v_pe, cmp_v_w1, cmp_v_w2, rel_bias, conv_w, conv_b, dt_bias, a_log, d_skip, ssd_norm_g, w_br_pool, w_br_nsa, w_br_ssd, w_out, ln1_g, ln1_b, router_w, router_b, exp_w1, exp_w3, exp_w2, ln2_g, ln2_b):
    bsz, s, d = x.shape
    assert d == D_MODEL and s % MIX_TS == 0 and s // CMP_STRIDE == LANE and (bsz * s) % PROJ_TM == 0
    t = bsz * s
    tb, ws, bc = _nsa_bias_tables(rel_bias)
    et, ovt = _nsa_consts(s)
    h = _ln(x.reshape(t, d), ln0_g, ln0_b)
    for i in range(DEPTH):
        proj = _proj(h, _prep_w_in(w_in[i]))
        kc, vc = _compress(proj, bsz, s, cmp_k_pe[i], cmp_k_w1[i], cmp_k_w2[i],
                           cmp_v_pe[i], cmp_v_w1[i], cmp_v_w2[i])
        nsa_out = _nsa(proj, kc, vc, tb, ws, bc, et, ovt, bsz, s)
        ssd_y = _ssd(proj, bsz, s, conv_w[i], conv_b[i], dt_bias[i], a_log[i], d_skip[i], ssd_norm_g[i])
        h1 = _mix(proj, nsa_out, ssd_y, h, bsz, s, pool_w[i], pool_scale[i], w_br_pool[i], w_br_nsa[i],
                  w_br_ssd[i], w_out[i], ln1_g[i], ln1_b[i])
        bucket, w_lo, w_hi = _router(h1, router_w, router_b)
        h = _moe(h1, bucket[0], w_lo[0], w_hi[0], exp_w1[i], exp_w3[i], exp_w2[i], ln2_g[i], ln2_b[i])
    return h.reshape(bsz, s, d)
```

```python
import functools
import math

import numpy as np
import jax
import jax.numpy as jnp
from jax import lax
from jax.experimental import pallas as pl
from jax.experimental.pallas import tpu as pltpu

F32 = jnp.float32
BF16 = jnp.bfloat16

D_MODEL = 1024
DEPTH = 2
DN_ALPHA = (2.0 * DEPTH) ** 0.25
LN_EPS = 1e-5
NEG = -1e30
BIG = 1e6

POOL_WINDOWS = (2, 4, 8, 16)
POOL_GROUP = 128
POOL_WIDTH = 512
MAX_POOL_WINDOW = 16

NSA_HEAD_DIM = 64
NSA_KV_HEADS = 2
NSA_HPG = 4
NSA_HEADS = 8
NSA_WIDTH = 512
CMP_LEN = 32
CMP_STRIDE = 16
CMP_HIDDEN = 128
SLC_LEN = 64
SLC_TOPN = 8
WIN_LEN = 512
Q_BLOCK = 128
REL_BUCKETS = 32
REL_MAX_DIST = 128

SSD_D_INNER = 1024
SSD_HEAD_DIM = 64
SSD_HEADS = 16
SSD_GROUPS = 4
SSD_STATE = 128
SSD_CONV = 4
SSD_CHUNK = 128
SSD_XBC = 2048

N_EXPERTS = 16
N_EXPERT_GROUPS = 4
EXPERTS_PER_GROUP = 4
D_EXPERT = 512
PAIRS = ((0, 1), (0, 2), (0, 3), (1, 2), (1, 3), (2, 3))
N_BUCKETS = N_EXPERT_GROUPS * len(PAIRS)

LANE = 128
SUBLANE = 8
VMEM_LIMIT = 48 * 1024 * 1024

C_POOL = 0
C_Q = 512
C_KV = 1024
C_NG = 1792
C_DT = 1920
C_XBC = 2048
C_Z = 4096
C_GATE = 5120
D_PAD = 8192

PROJ_TM = 1024
PROJ_TN = 1024
MIX_TS = 256
ROUTER_TM = 1024
MOE_TM = 256


def _dot(a, b):
    return jnp.dot(a, b, preferred_element_type=F32)


def _dot_nt(a, b):
    return lax.dot_general(a, b, (((1,), (1,)), ((), ())), preferred_element_type=F32)


def _dot_tn(a, b):
    return lax.dot_general(a, b, (((0,), (0,)), ((), ())), preferred_element_type=F32)


def _split3(x):
    x1 = x.astype(BF16)
    r1 = x - x1.astype(F32)
    x2 = r1.astype(BF16)
    x3 = (r1 - x2.astype(F32)).astype(BF16)
    return x1, x2, x3


def _layer_norm(x, g, b):
    mu = jnp.mean(x, -1, keepdims=True)
    xc = x - mu
    var = jnp.mean(xc * xc, -1, keepdims=True)
    return xc * lax.rsqrt(var + LN_EPS) * g + b


def _cparams(sem):
    return pltpu.CompilerParams(dimension_semantics=sem, vmem_limit_bytes=VMEM_LIMIT)


def _ln_kernel(x_ref, g_ref, b_ref, o_ref):
    o_ref[...] = _layer_norm(x_ref[...], g_ref[...], b_ref[...])


def _ln(x2d, g, b):
    t, d = x2d.shape
    tm = 512
    return pl.pallas_call(
        _ln_kernel,
        out_shape=jax.ShapeDtypeStruct((t, d), F32),
        grid=(t // tm,),
        in_specs=[pl.BlockSpec((tm, d), lambda i: (i, 0)),
                  pl.BlockSpec((1, d), lambda i: (0, 0)),
                  pl.BlockSpec((1, d), lambda i: (0, 0))],
        out_specs=pl.BlockSpec((tm, d), lambda i: (i, 0)),
        compiler_params=_cparams(("parallel",)),
        name="ln0",
    )(x2d, g.reshape(1, d), b.reshape(1, d))


def _proj_kernel(h_ref, w_ref, o_ref, hb_ref):
    @pl.when(pl.program_id(1) == 0)
    def _():
        hb_ref[...] = h_ref[...].astype(BF16)

    o_ref[...] = _dot(hb_ref[...], w_ref[...])


def _proj(h2d, w_pad):
    t, d = h2d.shape
    return pl.pallas_call(
        _proj_kernel,
        out_shape=jax.ShapeDtypeStruct((t, D_PAD), F32),
        grid=(t // PROJ_TM, D_PAD // PROJ_TN),
        in_specs=[pl.BlockSpec((PROJ_TM, d), lambda i, j: (i, 0)),
                  pl.BlockSpec((d, PROJ_TN), lambda i, j: (0, j))],
        out_specs=pl.BlockSpec((PROJ_TM, PROJ_TN), lambda i, j: (i, j)),
        scratch_shapes=[pltpu.VMEM((PROJ_TM, d), BF16)],
        compiler_params=_cparams(("parallel", "arbitrary")),
        name="proj",
    )(h2d, w_pad)


def _pad_cols(w, n):
    return jnp.pad(w, ((0, 0), (0, n - w.shape[1])))


def _prep_w_in(w_in):
    d = w_in.shape[0]
    o = 0
    w_pool = w_in[:, o:o + POOL_WIDTH]; o += POOL_WIDTH
    w_q = w_in[:, o:o + NSA_WIDTH]; o += NSA_WIDTH
    w_kv = w_in[:, o:o + 768]; o += 768
    w_ng = w_in[:, o:o + 24]; o += 24
    w_z = w_in[:, o:o + SSD_D_INNER]; o += SSD_D_INNER
    w_xbc = w_in[:, o:o + SSD_XBC]; o += SSD_XBC
    w_dt = w_in[:, o:o + SSD_HEADS]; o += SSD_HEADS
    w_gate = w_in[:, o:o + 3 * D_MODEL]
    w_q = w_q.reshape(d, NSA_KV_HEADS, NSA_HPG, NSA_HEAD_DIM).transpose(0, 2, 1, 3).reshape(d, NSA_WIDTH)
    w = jnp.concatenate([w_pool, w_q, w_kv, _pad_cols(w_ng, LANE), _pad_cols(w_dt, LANE),
                         w_xbc, w_z, w_gate], axis=1)
    return w.astype(BF16)


def _compress_one(tok_ref, pe_ref, w1_ref, w2_ref):
    n16 = tok_ref.shape[0] // CMP_STRIDE
    acc_a = jnp.zeros((n16, 2 * CMP_HIDDEN), F32)
    acc_b = jnp.zeros((n16, 2 * CMP_HIDDEN), F32)
    for l in range(CMP_STRIDE):
        rows = tok_ref[pl.ds(l, n16, stride=CMP_STRIDE), :]
        xa = (rows + pe_ref[l:l + 1, :]).astype(BF16)
        xb = (rows + pe_ref[CMP_STRIDE + l:CMP_STRIDE + l + 1, :]).astype(BF16)
        acc_a = acc_a + _dot(xa, w1_ref[l])
        acc_b = acc_b + _dot(xb, w1_ref[CMP_STRIDE + l])
    hid = acc_a + pltpu.roll(acc_b, n16 - 1, axis=0)
    hid = jax.nn.gelu(hid)
    return _dot(hid.astype(BF16), w2_ref[...])


def _compress_kernel(kt_ref, vt_ref, pek_ref, w1k_ref, w2k_ref, pev_ref, w1v_ref, w2v_ref,
                     kc_ref, vc_ref):
    kc_ref[0] = _compress_one(kt_ref, pek_ref, w1k_ref, w2k_ref).astype(BF16)
    vc_ref[0] = _compress_one(vt_ref, pev_ref, w1v_ref, w2v_ref).T.astype(BF16)


def _blockdiag2(w):
    z = jnp.zeros_like(w)
    return jnp.concatenate([jnp.concatenate([w, z], -1), jnp.concatenate([z, w], -1)], -2)


def _compress(proj2d, bsz, s, pe_k, w1_k, w2_k, pe_v, w1_v, w2_v):
    n16 = s // CMP_STRIDE

    def prep(pe, w1, w2):
        pe2 = jnp.concatenate([pe, pe], axis=-1)
        w1b = _blockdiag2(w1.reshape(CMP_LEN, NSA_HEAD_DIM, CMP_HIDDEN)).astype(BF16)
        w2b = _blockdiag2(w2).astype(BF16)
        return pe2, w1b, w2b

    pk = prep(pe_k, w1_k, w2_k)
    pv = prep(pe_v, w1_v, w2_v)
    full = lambda shape: pl.BlockSpec(shape, lambda b: (0,) * len(shape))
    wspecs = [full((CMP_LEN, LANE)), full((CMP_LEN, LANE, 2 * CMP_HIDDEN)), full((2 * CMP_HIDDEN, LANE))]
    return pl.pallas_call(
        _compress_kernel,
        out_shape=(jax.ShapeDtypeStruct((bsz, n16, LANE), BF16),
                   jax.ShapeDtypeStruct((bsz, n16, LANE), BF16)),
        grid=(bsz,),
        in_specs=[pl.BlockSpec((s, LANE), lambda b: (b, C_KV // LANE)),
                  pl.BlockSpec((s, LANE), lambda b: (b, C_KV // LANE + 1))] + wspecs + wspecs,
        out_specs=(pl.BlockSpec((1, n16, LANE), lambda b: (b, 0, 0)),
                   pl.BlockSpec((1, n16, LANE), lambda b: (b, 0, 0))),
        compiler_params=_cparams(("parallel",)),
        name="nsa_compress",
    )(proj2d, proj2d, *pk, *pv)


def _rel_bucket(dist):
    n = jnp.maximum(dist, 0)
    max_exact = REL_BUCKETS // 2
    nf = jnp.maximum(n, 1).astype(F32)
    large = max_exact + (jnp.log(nf / max_exact) / math.log(REL_MAX_DIST / max_exact)
                         * (REL_BUCKETS - max_exact)).astype(jnp.int32)
    large = jnp.minimum(large, REL_BUCKETS - 1)
    return jnp.where(n < max_exact, n, large)


LOG2E = math.log2(math.e)
WIN_BACK = WIN_LEN // Q_BLOCK
WIN_KEYS = (WIN_BACK + 1) * Q_BLOCK
WIN_STRIP_TILES = 2 * WIN_BACK + 1
SEL_TILES = 4
SEL_KEYS = SEL_TILES * Q_BLOCK
SEL_STRIP_TILES = 3 * SEL_TILES - 1
CMP_BIAS_ROWS = 2 * LANE
PV_ROWS = NSA_HEAD_DIM + 16


def _nsa_bias_tables(rel_bias):
    tbl = rel_bias.astype(F32)
    far = tbl[REL_BUCKETS - 1]
    hq = NSA_HPG * Q_BLOCK

    def lookup(dist):
        onehot = (_rel_bucket(jnp.asarray(dist))[..., None] == jnp.arange(REL_BUCKETS)).astype(F32)
        return jnp.einsum("...b,bh->...h", onehot, tbl, precision=lax.Precision.HIGHEST)

    k = np.arange(Q_BLOCK)[None, :, None]
    q = np.arange(Q_BLOCK)[None, None, :]

    def strip(first, last, visible):
        n = first - last + 1
        dist = np.arange(first, last - 1, -1)[:, None, None] * Q_BLOCK + q - k
        v = jnp.where(jnp.asarray(visible(dist))[..., None], (lookup(dist) - far) * LOG2E, NEG)
        v = v.reshape(n, Q_BLOCK, Q_BLOCK, NSA_KV_HEADS, NSA_HPG).transpose(3, 0, 1, 4, 2)
        return v.reshape(NSA_KV_HEADS, n * Q_BLOCK, hq)

    ts = strip(2 * SEL_TILES - 1, 1 - SEL_TILES, lambda d: d >= 0)
    ws = strip(WIN_BACK, -WIN_BACK, lambda d: (d >= 0) & (d < WIN_LEN))
    r = np.arange(CMP_BIAS_ROWS)[:, None]
    dist_c = q[0] - CMP_STRIDE * (r - LANE) - (CMP_LEN - 1)
    bc = jnp.where(jnp.asarray(dist_c >= 0)[..., None], lookup(dist_c) * LOG2E, NEG)
    bc = bc.reshape(CMP_BIAS_ROWS, Q_BLOCK, NSA_KV_HEADS, NSA_HPG).transpose(2, 0, 3, 1)
    bc = bc.reshape(NSA_KV_HEADS, CMP_BIAS_ROWS, hq)
    return ts, ws, bc


def _nsa_consts(s):
    n_slc = s // SLC_LEN
    et = (np.arange(s)[:, None] // SLC_LEN == np.arange(LANE)[None, :]).astype(np.float32)
    n_cmp = (s - CMP_LEN) // CMP_STRIDE + 1
    c0 = np.arange(n_cmp) * CMP_STRIDE
    s0 = np.arange(n_slc) * SLC_LEN
    ov = np.clip(np.minimum(c0[None, :] + CMP_LEN, s0[:, None] + SLC_LEN)
                 - np.maximum(c0[None, :], s0[:, None]), 0, None) / CMP_LEN
    ovt = np.zeros((n_slc, LANE), np.float32)
    ovt[:, :n_cmp] = ov
    return jnp.asarray(et, BF16), jnp.asarray(ovt, BF16)


def _nsa_kernel(q_ref, ng_ref, ks_ref, vs_ref, kw_ref, vw_ref, kc_ref, vct_ref,
                ts_ref, ws_ref, bc_ref, et_ref, ovt_ref, o_ref,
                ksb_ref, kwb_ref, vst_ref, vwt_ref, msel_ref, *, nq, n_slc):
    qi = pl.program_id(1)
    hq = NSA_HPG * Q_BLOCK
    dh = NSA_HEAD_DIM
    groups = range(NSA_KV_HEADS)
    gsl = [slice(g * dh, (g + 1) * dh) for g in groups]

    @pl.when(qi == 0)
    def _():
        ksb_ref[...] = ks_ref[...].astype(BF16)
        kwb_ref[...] = kw_ref[...].astype(BF16)
        ones_rows = jnp.where(lax.broadcasted_iota(jnp.int32, (PV_ROWS - dh, ks_ref.shape[0]), 0) == 0, 1.0, 0.0)
        for g in groups:
            vst_ref[g, dh:, :] = ones_rows.astype(BF16)
            vwt_ref[g, dh:, :] = ones_rows.astype(BF16)
        for kt in range(nq):
            sl = slice(kt * Q_BLOCK, (kt + 1) * Q_BLOCK)
            vs_t = vs_ref[sl, :].T
            vw_t = vw_ref[sl, :].T
            for g in groups:
                vst_ref[g, 0:dh, sl] = vs_t[gsl[g], :].astype(BF16)
                vwt_ref[g, 0:dh, sl] = vw_t[gsl[g], :].astype(BF16)

    row_lo = lax.broadcasted_iota(jnp.int32, (LANE, hq), 0) < dh

    def heads4(m):
        return jnp.concatenate([m] * NSA_HPG, axis=1)

    gates_t = jax.nn.sigmoid(ng_ref[...]).T
    q_t = jnp.concatenate(
        [(q_ref[:, h * LANE:(h + 1) * LANE] * (dh ** -0.5 * LOG2E)).T for h in range(NSA_HPG)], axis=1)
    qg_t = [jnp.where(row_lo if g == 0 else jnp.logical_not(row_lo), q_t, 0.0).astype(BF16) for g in groups]

    boff = pl.multiple_of(LANE - SUBLANE * qi, SUBLANE)
    jrow = lax.broadcasted_iota(jnp.int32, (n_slc, LANE), 0)
    tj = qi * Q_BLOCK + lax.broadcasted_iota(jnp.int32, (n_slc, LANE), 1)
    jt = lax.shift_right_logical(tj, SLC_LEN.bit_length() - 1)
    forced = (jrow == 0) | (jrow == jt) | (jrow == jt - 1)
    valid = jrow * SLC_LEN <= tj
    o_c = []
    for g in groups:
        s = _dot(kc_ref[0], qg_t[g]) + bc_ref[g, pl.ds(boff, LANE), :]
        mx = jnp.max(s, axis=0, keepdims=True)
        p = jnp.exp2(s - mx)
        lsum = jnp.sum(p, axis=0, keepdims=True)
        pb = (p * jnp.where(mx > 0.5 * NEG, 1.0 / lsum, 0.0)).astype(BF16)
        o_c.append(_dot(vct_ref[0, gsl[g], :], pb))
        imp4 = _dot(ovt_ref[...], pb)
        imp = imp4[:, 0:LANE]
        for h in range(1, NSA_HPG):
            imp = imp + imp4[:, h * LANE:(h + 1) * LANE]
        val = jnp.where(forced, BIG, jnp.where(valid, imp, -BIG))
        rank = jnp.zeros((n_slc, LANE), F32)
        for i in range(n_slc):
            ci = val[i:i + 1, :]
            rank = rank + jnp.where(ci > val, 1.0, 0.0) + jnp.where(ci == val, (jrow > i).astype(F32), 0.0)
        sel_t = jnp.where(rank < SLC_TOPN, 1.0, 0.0)
        sel_t = jnp.concatenate([sel_t, jnp.zeros((LANE - n_slc, LANE), F32)], axis=0).astype(BF16)
        msel_ref[g] = (_dot(et_ref[...], sel_t) - 1.0) * (-NEG)

    n_super = qi // SEL_TILES + 1

    def sel_step(st, carry, near):
        off = pl.multiple_of(st * SEL_KEYS, SEL_KEYS)
        k_bf = ksb_ref[pl.ds(off, SEL_KEYS), :]
        out = []
        for g in groups:
            m, acc = carry[g]
            s = _dot(k_bf, qg_t[g]) + heads4(msel_ref[g, pl.ds(off, SEL_KEYS), :])
            if near:
                toff = pl.multiple_of((2 * SEL_TILES - 1 - (qi - SEL_TILES * st)) * Q_BLOCK, Q_BLOCK)
                s = s + ts_ref[g, pl.ds(toff, SEL_KEYS), :]
            m_new = jnp.maximum(m, jnp.max(s, axis=0, keepdims=True))
            p = jnp.exp2(s - m_new).astype(BF16)
            acc = jnp.exp2(m - m_new) * acc + _dot(vst_ref[g, :, pl.ds(off, SEL_KEYS)], p)
            out.append((m_new, acc))
        return tuple(out)

    init = tuple((jnp.full((1, hq), NEG, F32), jnp.zeros((PV_ROWS, hq), F32)) for _ in groups)
    n_near = jnp.minimum(n_super, 2)
    carry = lax.fori_loop(0, n_near, lambda it, c: sel_step(n_super - 1 - it, c, True), init)
    carry = lax.fori_loop(0, n_super - n_near, lambda it, c: sel_step(it, c, False), carry)
    o_s = [carry[g][1][0:dh] * (1.0 / carry[g][1][dh:dh + 1]) for g in groups]

    w0 = jnp.maximum(qi - WIN_BACK, 0)
    woff = pl.multiple_of(w0 * Q_BLOCK, Q_BLOCK)
    boff_w = pl.multiple_of((WIN_BACK - (qi - w0)) * Q_BLOCK, Q_BLOCK)
    kw_bf = kwb_ref[pl.ds(woff, WIN_KEYS), :]
    o_w = []
    for g in groups:
        s = _dot(kw_bf, qg_t[g]) + ws_ref[g, pl.ds(boff_w, WIN_KEYS), :]
        p = jnp.exp2(s - jnp.max(s, axis=0, keepdims=True)).astype(BF16)
        acc = _dot(vwt_ref[g, :, pl.ds(woff, WIN_KEYS)], p)
        o_w.append(acc[0:dh] * (1.0 / acc[dh:dh + 1]))

    for h in range(NSA_HPG):
        hl = slice(h * LANE, (h + 1) * LANE)
        halves = []
        for g in groups:
            c = g * NSA_HPG + h
            halves.append(gates_t[c:c + 1, :] * o_c[g][:, hl]
                          + gates_t[NSA_HEADS + c:NSA_HEADS + c + 1, :] * o_s[g][:, hl]
                          + gates_t[2 * NSA_HEADS + c:2 * NSA_HEADS + c + 1, :] * o_w[g][:, hl])
        o_ref[:, hl] = jnp.concatenate(halves, axis=0).T


def _nsa(proj2d, kc, vct, ts, ws, bc, et, ovt, bsz, s):
    nq = s // Q_BLOCK
    n_slc = s // SLC_LEN
    hq = NSA_HPG * Q_BLOCK
    kvb = C_KV // LANE
    kv_spec = lambda blk: pl.BlockSpec((s, LANE), lambda b, qi: (b, kvb + blk))
    full = lambda shape: pl.BlockSpec(shape, lambda b, qi: (0,) * len(shape), pipeline_mode=pl.Buffered(1))
    return pl.pallas_call(
        functools.partial(_nsa_kernel, nq=nq, n_slc=n_slc),
        out_shape=jax.ShapeDtypeStruct((bsz * s, NSA_WIDTH), F32),
        grid=(bsz, nq),
        in_specs=[pl.BlockSpec((Q_BLOCK, NSA_WIDTH), lambda b, qi: (b * nq + qi, C_Q // NSA_WIDTH)),
                  pl.BlockSpec((Q_BLOCK, LANE), lambda b, qi: (b * nq + qi, C_NG // LANE)),
                  kv_spec(2), kv_spec(3), kv_spec(4), kv_spec(5),
                  pl.BlockSpec((1, s // CMP_STRIDE, LANE), lambda b, qi: (b, 0, 0)),
                  pl.BlockSpec((1, LANE, s // CMP_STRIDE), lambda b, qi: (b, 0, 0)),
                  full((NSA_KV_HEADS, SEL_STRIP_TILES * Q_BLOCK, hq)),
                  full((NSA_KV_HEADS, WIN_STRIP_TILES * Q_BLOCK, hq)),
                  full((NSA_KV_HEADS, CMP_BIAS_ROWS, hq)),
                  full((s, LANE)),
                  full((n_slc, LANE))],
        out_specs=pl.BlockSpec((Q_BLOCK, NSA_WIDTH), lambda b, qi: (b * nq + qi, 0)),
        scratch_shapes=[pltpu.VMEM((s, LANE), BF16), pltpu.VMEM((s, LANE), BF16),
                        pltpu.VMEM((NSA_KV_HEADS, PV_ROWS, s), BF16), pltpu.VMEM((NSA_KV_HEADS, PV_ROWS, s), BF16),
                        pltpu.VMEM((NSA_KV_HEADS, s, LANE), F32)],
        compiler_params=_cparams(("parallel", "arbitrary")),
        name="nsa_attn",
    )(proj2d, proj2d, proj2d, proj2d, proj2d, proj2d, kc, vct, ts, ws, bc, et, ovt)


def _ssd_kernel(xbc_ref, prev_ref, z_ref, dt_ref, cw_ref, cb_ref, dtb_ref, alog_ref, alogx_ref,
                dskip_ref, ng_ref, e16_ref, tril_ref, o_ref, xcat_ref, state_ref):
    c = pl.program_id(1)
    L = SSD_CHUNK
    pair_w = 2 * SSD_HEAD_DIM

    @pl.when(c == 0)
    def _():
        state_ref[...] = jnp.zeros(state_ref.shape, F32)

    xcat_ref[0:SUBLANE, :] = jnp.where(c > 0, prev_ref[...], 0.0)
    xcat_ref[SUBLANE:, :] = xbc_ref[...]
    conv = cb_ref[...]
    for k in range(SSD_CONV):
        conv = conv + cw_ref[k:k + 1, :] * xcat_ref[pl.ds(SUBLANE - (SSD_CONV - 1) + k, L), :]
    xbc = jax.nn.silu(conv)
    xs = xbc[:, :SSD_D_INNER]
    bmat = xbc[:, SSD_D_INNER:SSD_D_INNER + SSD_GROUPS * SSD_STATE]
    cmat = xbc[:, SSD_D_INNER + SSD_GROUPS * SSD_STATE:]

    dt = jax.nn.softplus(dt_ref[...] + dtb_ref[...])
    tril = tril_ref[...]
    e16 = e16_ref[...]
    d1, d2, d3 = _split3(dt)
    dt_x = _dot(d1, e16) + _dot(d2, e16) + _dot(d3, e16)
    a_c = dt * (-jnp.exp(alog_ref[...]))
    a_x = dt_x * (-jnp.exp(alogx_ref[...]))
    c1, c2, c3 = _split3(a_c)
    acs = _dot(tril, c1) + _dot(tril, c2) + _dot(tril, c3)
    x1, x2, x3 = _split3(a_x)
    acs_x = _dot(tril, x1) + _dot(tril, x2) + _dot(tril, x3)
    acs_t = acs.T
    last = acs_x[L - 1:L, :]
    eacs = jnp.exp(acs_x)
    decay_in = jnp.exp(last - acs_x)
    cdec = jnp.exp(last)
    xdt = xs * dt_x
    wst = xdt * decay_in

    row = lax.broadcasted_iota(jnp.int32, (L, L), 0)
    lane = lax.broadcasted_iota(jnp.int32, (L, L), 1)
    tri = row >= lane
    lo = lane < SSD_HEAD_DIM

    ys = []
    for g in range(SSD_GROUPS):
        cg = cmat[:, g * SSD_STATE:(g + 1) * SSD_STATE].astype(BF16)
        bg = bmat[:, g * SSD_STATE:(g + 1) * SSD_STATE].astype(BF16)
        cb = _dot_nt(cg, bg)
        for pr in range(2):
            pp = g * 2 + pr
            ls = slice(pp * pair_w, (pp + 1) * pair_w)
            xp = xdt[:, ls]
            yd = None
            for e in range(2):
                hd = 2 * pp + e
                diff = acs[:, hd:hd + 1] - acs_t[hd:hd + 1, :]
                seg = jnp.exp(jnp.where(tri, diff, NEG))
                pm = (cb * seg).astype(BF16)
                xe = jnp.where(lo if e == 0 else jnp.logical_not(lo), xp, 0.0).astype(BF16)
                term = _dot(pm, xe)
                yd = term if yd is None else yd + term
            st = state_ref[pp]
            yo = _dot(cg, st.astype(BF16)) * eacs[:, ls]
            ys.append(yd + yo)
            state_ref[pp] = cdec[:, ls] * st + _dot_tn(bg, wst[:, ls].astype(BF16))

    y = jnp.concatenate(ys, axis=1) + xs * dskip_ref[...]
    y = y * jax.nn.silu(z_ref[...])
    gw = SSD_D_INNER // SSD_GROUPS
    outs = []
    for gi in range(SSD_GROUPS):
        yg = y[:, gi * gw:(gi + 1) * gw]
        outs.append(yg * lax.rsqrt(jnp.mean(yg * yg, -1, keepdims=True) + LN_EPS))
    o_ref[...] = jnp.concatenate(outs, axis=1) * ng_ref[...]


def _ssd(proj2d, bsz, s, conv_w, conv_b, dt_bias, a_log, d_skip, norm_g):
    nc = s // SSD_CHUNK
    L = SSD_CHUNK
    pad16 = lambda v: jnp.pad(v.reshape(1, SSD_HEADS), ((0, 0), (0, LANE - SSD_HEADS)))
    rep64 = lambda v: jnp.repeat(v, SSD_HEAD_DIM).reshape(1, SSD_D_INNER)
    e16 = np.zeros((LANE, SSD_D_INNER), np.float32)
    e16[np.arange(SSD_D_INNER) // SSD_HEAD_DIM, np.arange(SSD_D_INNER)] = 1.0
    tril = np.tril(np.ones((L, L), np.float32))
    full = lambda shape: pl.BlockSpec(shape, lambda b, c: (0,) * len(shape))
    rows8 = s // SUBLANE
    return pl.pallas_call(
        _ssd_kernel,
        out_shape=jax.ShapeDtypeStruct((bsz * s, SSD_D_INNER), F32),
        grid=(bsz, nc),
        in_specs=[pl.BlockSpec((L, SSD_XBC), lambda b, c: (b * nc + c, C_XBC // SSD_XBC)),
                  pl.BlockSpec((SUBLANE, SSD_XBC),
                               lambda b, c: (jnp.maximum(b * rows8 + c * (L // SUBLANE) - 1, 0), C_XBC // SSD_XBC)),
                  pl.BlockSpec((L, SSD_D_INNER), lambda b, c: (b * nc + c, C_Z // SSD_D_INNER)),
                  pl.BlockSpec((L, LANE), lambda b, c: (b * nc + c, C_DT // LANE)),
                  full((SSD_CONV, SSD_XBC)), full((1, SSD_XBC)), full((1, LANE)), full((1, LANE)),
                  full((1, SSD_D_INNER)), full((1, SSD_D_INNER)), full((1, SSD_D_INNER)),
                  full((LANE, SSD_D_INNER)), full((L, L))],
        out_specs=pl.BlockSpec((L, SSD_D_INNER), lambda b, c: (b * nc + c, 0)),
        scratch_shapes=[pltpu.VMEM((L + SUBLANE, SSD_XBC), F32),
                        pltpu.VMEM((SSD_HEADS // 2, SSD_STATE, 2 * SSD_HEAD_DIM), F32)],
        compiler_params=_cparams(("parallel", "arbitrary")),
        name="ssd",
    )(proj2d, proj2d, proj2d, proj2d, conv_w, conv_b.reshape(1, SSD_XBC), pad16(dt_bias), pad16(a_log),
      rep64(a_log), rep64(d_skip), norm_g.reshape(1, SSD_D_INNER),
      jnp.asarray(e16, BF16), jnp.asarray(tril, BF16))


def _mix_kernel(up_ref, upprev_ref, nsa_ref, ssd_ref, ga_ref, gb_ref, gc_ref, h_ref,
                pw_ref, ps_ref, wbp_ref, wbn_ref, wbs_ref, wo_ref, lg_ref, lb_ref, o_ref, *, ts):
    si = pl.program_id(1)
    w0 = MAX_POOL_WINDOW
    prev = jnp.where(si > 0, upprev_ref[...], 0.0)
    xcat = jnp.concatenate([prev, up_ref[...]], axis=0)
    tpos = si * ts + lax.broadcasted_iota(jnp.int32, (ts, POOL_GROUP), 0)
    ygs = []
    for gi, w in enumerate(POOL_WINDOWS):
        x = xcat[:, gi * POOL_GROUP:(gi + 1) * POOL_GROUP]
        acc = x
        d = 1
        while d < w:
            acc = acc + pltpu.roll(acc, d, axis=0)
            d *= 2
        cnt = jnp.minimum(tpos + 1, w).astype(F32)
        r = acc[w0:, :] / cnt - x[w0:, :]
        ygs.append(_dot(r.astype(BF16), pw_ref[gi]))
    y = jnp.concatenate(ygs, axis=1) * ps_ref[...]
    br_a = _dot(y.astype(BF16), wbp_ref[...])
    br_b = _dot(nsa_ref[...].astype(BF16), wbn_ref[...])
    br_c = _dot(ssd_ref[...].astype(BF16), wbs_ref[...])
    mixed = (jax.nn.sigmoid(ga_ref[...]) * br_a + jax.nn.sigmoid(gb_ref[...]) * br_b
             + jax.nn.sigmoid(gc_ref[...]) * br_c)
    mix = _dot(mixed.astype(BF16), wo_ref[...])
    o_ref[...] = _layer_norm(DN_ALPHA * h_ref[...] + mix, lg_ref[...], lb_ref[...])


def _mix(proj2d, nsa_out, ssd_y, h2d, bsz, s, pool_w, pool_scale, w_br_pool, w_br_nsa, w_br_ssd,
         w_out, ln_g, ln_b):
    ts = MIX_TS
    ns = s // ts
    d = D_MODEL
    w0 = MAX_POOL_WINDOW
    wbn = w_br_nsa.reshape(NSA_KV_HEADS, NSA_HPG, NSA_HEAD_DIM, d).transpose(1, 0, 2, 3).reshape(NSA_WIDTH, d)
    full = lambda shape: pl.BlockSpec(shape, lambda b, si: (0,) * len(shape))
    row = lambda width, cb: pl.BlockSpec((ts, width), lambda b, si: (b * ns + si, cb))
    return pl.pallas_call(
        functools.partial(_mix_kernel, ts=ts),
        out_shape=jax.ShapeDtypeStruct((bsz * s, d), F32),
        grid=(bsz, ns),
        in_specs=[row(POOL_WIDTH, C_POOL // POOL_WIDTH),
                  pl.BlockSpec((w0, POOL_WIDTH),
                               lambda b, si: (jnp.maximum((b * s + si * ts) // w0 - 1, 0), C_POOL // POOL_WIDTH)),
                  row(NSA_WIDTH, 0), row(d, 0),
                  row(d, C_GATE // d), row(d, C_GATE // d + 1), row(d, C_GATE // d + 2),
                  row(d, 0),
                  full((len(POOL_WINDOWS), POOL_GROUP, POOL_GROUP)), full((1, POOL_WIDTH)),
                  full((POOL_WIDTH, d)), full((NSA_WIDTH, d)), full((d, d)), full((d, d)),
                  full((1, d)), full((1, d))],
        out_specs=pl.BlockSpec((ts, d), lambda b, si: (b * ns + si, 0)),
        compiler_params=_cparams(("parallel", "arbitrary")),
        name="mix",
    )(proj2d, proj2d, nsa_out, ssd_y, proj2d, proj2d, proj2d, h2d,
      pool_w.astype(BF16), pool_scale.reshape(1, POOL_WIDTH), w_br_pool.astype(BF16), wbn.astype(BF16),
      w_br_ssd.astype(BF16), w_out.astype(BF16), ln_g.reshape(1, d), ln_b.reshape(1, d))


RANK_CHUNK = 256
BUCKET_ROWS = 32


def _router_kernel(h_ref, rw_ref, rb_ref, tri_ref, bucket_ref, wlo_ref, whi_ref, rank_ref, cnt_ref, run_ref):
    @pl.when(pl.program_id(0) == 0)
    def _():
        run_ref[...] = jnp.zeros(run_ref.shape, F32)

    logits = _dot_nt(rw_ref[...], h_ref[...].astype(BF16))
    lg = [logits[e:e + 1, :] for e in range(N_EXPERTS)]
    mx = lg[0]
    for e in range(1, N_EXPERTS):
        mx = jnp.maximum(mx, lg[e])
    ex = [jnp.exp(v - mx) for v in lg]
    den = ex[0]
    for e in range(1, N_EXPERTS):
        den = den + ex[e]
    probs = [v / den for v in ex]
    score = [probs[e] + rb_ref[e:e + 1, :] for e in range(N_EXPERTS)]

    def group_vals(vals, grp):
        out = []
        for k in range(EXPERTS_PER_GROUP):
            v = vals[k]
            for gi in range(1, N_EXPERT_GROUPS):
                v = jnp.where(grp == gi, vals[gi * EXPERTS_PER_GROUP + k], v)
            out.append(v)
        return out

    gscore = []
    for gi in range(N_EXPERT_GROUPS):
        sc = score[gi * EXPERTS_PER_GROUP:(gi + 1) * EXPERTS_PER_GROUP]
        best = None
        for a, b in PAIRS:
            v = sc[a] + sc[b]
            best = v if best is None else jnp.maximum(best, v)
        gscore.append(best)
    grp = jnp.zeros(gscore[0].shape, jnp.int32)
    best = gscore[0]
    for gi in range(1, N_EXPERT_GROUPS):
        better = gscore[gi] > best
        grp = jnp.where(better, gi, grp)
        best = jnp.where(better, gscore[gi], best)

    sc = group_vals(score, grp)
    pr = group_vals(probs, grp)
    first = jnp.zeros(grp.shape, jnp.int32)
    fv = sc[0]
    for k in range(1, EXPERTS_PER_GROUP):
        better = sc[k] > fv
        first = jnp.where(better, k, first)
        fv = jnp.where(better, sc[k], fv)
    second = jnp.full(grp.shape, -1, jnp.int32)
    sv = jnp.full(fv.shape, -jnp.inf, F32)
    for k in range(EXPERTS_PER_GROUP):
        better = (first != k) & ((sc[k] > sv) | (second < 0))
        second = jnp.where(better, k, second)
        sv = jnp.where(better, sc[k], sv)
    lo = jnp.minimum(first, second)
    hi = jnp.maximum(first, second)
    p_lo = pr[0]
    p_hi = pr[0]
    for k in range(1, EXPERTS_PER_GROUP):
        p_lo = jnp.where(lo == k, pr[k], p_lo)
        p_hi = jnp.where(hi == k, pr[k], p_hi)
    tot = p_lo + p_hi
    pair = jnp.zeros(grp.shape, jnp.int32)
    for pi, (a, b) in enumerate(PAIRS):
        pair = jnp.where((lo == a) & (hi == b), pi, pair)
    bucket = grp * len(PAIRS) + pair
    bucket_ref[...] = bucket
    wlo_ref[...] = p_lo / tot
    whi_ref[...] = p_hi / tot

    tm = bucket.shape[1]
    onehot = jnp.where(lax.broadcasted_iota(jnp.int32, (BUCKET_ROWS, tm), 0) == bucket, 1.0, 0.0)
    run = run_ref[...]
    ranks = []
    for c in range(tm // RANK_CHUNK):
        oc = onehot[:, c * RANK_CHUNK:(c + 1) * RANK_CHUNK]
        before = _dot(oc.astype(BF16), tri_ref[...]) + jnp.concatenate([run] * (RANK_CHUNK // LANE), axis=1)
        ranks.append(jnp.sum(oc * before, axis=0, keepdims=True))
        run = run + jnp.broadcast_to(jnp.sum(oc, axis=1, keepdims=True), run.shape)
    rank_ref[...] = jnp.concatenate(ranks, axis=1).astype(jnp.int32)
    run_ref[...] = run
    cnt_ref[...] = run


def _router(h2d, router_w, router_b):
    t, d = h2d.shape
    tm = ROUTER_TM
    rw = jnp.pad(router_w.T, ((0, LANE - N_EXPERTS), (0, 0))).astype(BF16)
    rb = jnp.broadcast_to(jnp.pad(router_b, (0, LANE - N_EXPERTS))[:, None], (LANE, tm))
    tri = jnp.asarray(np.triu(np.ones((RANK_CHUNK, RANK_CHUNK), np.float32), 1), BF16)
    vec = lambda dt: jax.ShapeDtypeStruct((1, t), dt)
    row = pl.BlockSpec((1, tm), lambda i: (0, i))
    return pl.pallas_call(
        _router_kernel,
        out_shape=(vec(jnp.int32), vec(F32), vec(F32), vec(jnp.int32),
                   jax.ShapeDtypeStruct((BUCKET_ROWS, LANE), F32)),
        grid=(t // tm,),
        in_specs=[pl.BlockSpec((tm, d), lambda i: (i, 0)),
                  pl.BlockSpec((LANE, d), lambda i: (0, 0)),
                  pl.BlockSpec((LANE, tm), lambda i: (0, 0)),
                  pl.BlockSpec((RANK_CHUNK, RANK_CHUNK), lambda i: (0, 0))],
        out_specs=(row, row, row, row, pl.BlockSpec((BUCKET_ROWS, LANE), lambda i: (0, 0))),
        scratch_shapes=[pltpu.VMEM((BUCKET_ROWS, LANE), F32)],
        compiler_params=_cparams(("arbitrary",)),
        name="router",
    )(h2d, rw, rb, tri)


def _moe_plan(bucket, rank, counts, w_lo, w_hi, t):
    tm = MOE_TM
    n_tiles = t // tm + N_BUCKETS
    p_rows = n_tiles * tm
    tiles_per = (counts + tm - 1) // tm
    tile_end = jnp.cumsum(tiles_per)
    tile_start = tile_end - tiles_per
    in_bucket = bucket[:, None] == jnp.arange(N_BUCKETS)[None, :]
    dest = jnp.sum(jnp.where(in_bucket, (tile_start * tm)[None, :], 0), axis=1) + rank
    payload = jnp.stack([jnp.arange(t, dtype=jnp.int32), lax.bitcast_convert_type(w_lo, jnp.int32),
                         lax.bitcast_convert_type(w_hi, jnp.int32)], axis=1)
    plan = jnp.zeros((p_rows, 3), jnp.int32).at[dest].set(payload, unique_indices=True)
    src = plan[:, 0]
    wl = lax.bitcast_convert_type(plan[:, 1], F32)
    wh = lax.bitcast_convert_type(plan[:, 2], F32)
    n_used = tile_end[-1]
    tile_ids = jnp.arange(n_tiles)
    tb = jnp.searchsorted(tile_end, tile_ids, side="right").astype(jnp.int32)
    tb = jnp.where(tile_ids < n_used, tb, tb[jnp.maximum(n_used - 1, 0)])
    tb = jnp.minimum(tb, N_BUCKETS - 1)
    pairs = np.asarray(PAIRS, np.int32)
    grp = tb // len(PAIRS)
    ea = grp * EXPERTS_PER_GROUP + jnp.asarray(pairs[:, 0])[tb % len(PAIRS)]
    eb = grp * EXPERTS_PER_GROUP + jnp.asarray(pairs[:, 1])[tb % len(PAIRS)]
    nvalid = jnp.clip(counts[tb] - (tile_ids - tile_start[tb]) * tm, 0, tm)
    nvalid = jnp.where(tile_ids < n_used, nvalid, 0).astype(jnp.int32)
    return (src.reshape(n_tiles, 1, tm), wl.reshape(p_rows, 1), wh.reshape(p_rows, 1),
            ea.astype(jnp.int32), eb.astype(jnp.int32), nvalid, n_tiles)


DMA_UNROLL = 8


def _for_rows(n, fn):
    nfull = n // DMA_UNROLL

    def group(c, carry):
        for u in range(DMA_UNROLL):
            fn(c * DMA_UNROLL + u)
        return carry

    def single(r, carry):
        fn(r)
        return carry

    lax.fori_loop(0, nfull, group, 0)
    lax.fori_loop(nfull * DMA_UNROLL, n, single, 0)


def _moe_kernel(ea_ref, eb_ref, nv_ref, src_ref, srcn_ref, wl_ref, wh_ref, h_hbm,
                w1a_ref, w3a_ref, w2a_ref, w1b_ref, w3b_ref, w2b_ref, lg_ref, lb_ref,
                o_hbm, xbuf, obuf, gsem, ssem):
    i = pl.program_id(0)
    n = pl.num_programs(0)
    tm = xbuf.shape[1]
    slot = lax.rem(i, 2)
    nslot = 1 - slot
    nv = nv_ref[i]

    def gather_start(idx_ref, sl):
        def one(r):
            pltpu.make_async_copy(h_hbm.at[pl.ds(idx_ref[0, 0, r], 1)], xbuf.at[sl, pl.ds(r, 1)],
                                  gsem.at[sl]).start()
        _for_rows(tm, one)

    def gather_wait(sl):
        def one(r):
            pltpu.make_async_copy(h_hbm.at[pl.ds(0, 1)], xbuf.at[sl, pl.ds(r, 1)], gsem.at[sl]).wait()
        _for_rows(tm, one)

    def scatter_start(count, sl):
        def one(r):
            pltpu.make_async_copy(obuf.at[sl, pl.ds(r, 1)], o_hbm.at[pl.ds(src_ref[0, 0, r], 1)],
                                  ssem.at[sl]).start()
        _for_rows(count, one)

    def scatter_wait(count, sl):
        def one(r):
            pltpu.make_async_copy(obuf.at[sl, pl.ds(r, 1)], o_hbm.at[pl.ds(0, 1)], ssem.at[sl]).wait()
        _for_rows(count, one)

    @pl.when((i == 0) & (nv > 0))
    def _():
        gather_start(src_ref, slot)

    nv_next = nv_ref[jnp.minimum(i + 1, n - 1)]

    @pl.when((i + 1 < n) & (nv_next > 0))
    def _():
        gather_start(srcn_ref, nslot)

    nv_prev2 = nv_ref[jnp.maximum(i - 2, 0)]

    @pl.when((i >= 2) & (nv_prev2 > 0))
    def _():
        scatter_wait(nv_prev2, slot)

    @pl.when(nv > 0)
    def _():
        gather_wait(slot)
        x = xbuf[slot]
        xb = x.astype(BF16)

        def expert(w1_ref, w3_ref, w2_ref):
            hmid = jax.nn.silu(_dot(xb, w1_ref[0])) * _dot(xb, w3_ref[0])
            return _dot(hmid.astype(BF16), w2_ref[0])

        y = wl_ref[...] * expert(w1a_ref, w3a_ref, w2a_ref) + wh_ref[...] * expert(w1b_ref, w3b_ref, w2b_ref)
        obuf[slot] = _layer_norm(DN_ALPHA * x + y, lg_ref[...], lb_ref[...])
        scatter_start(nv, slot)

    @pl.when(i == n - 1)
    def _():
        nv_prev1 = nv_ref[jnp.maximum(i - 1, 0)]

        @pl.when((i >= 1) & (nv_prev1 > 0))
        def _():
            scatter_wait(nv_prev1, nslot)

        @pl.when(nv > 0)
        def _():
            scatter_wait(nv, slot)


def _moe(h2d, bucket, rank, counts, w_lo, w_hi, w1, w3, w2, ln_g, ln_b):
    t, d = h2d.shape
    tm = MOE_TM
    src, wl, wh, ea, eb, nvalid, n_tiles = _moe_plan(bucket, rank, counts, w_lo, w_hi, t)
    wspec = lambda shape, which: pl.BlockSpec(
        shape, (lambda i, ea_r, eb_r, nv_r: (ea_r[i], 0, 0)) if which == 0
        else (lambda i, ea_r, eb_r, nv_r: (eb_r[i], 0, 0)))
    w13 = (1, d, D_EXPERT)
    w2s = (1, D_EXPERT, d)
    grid_spec = pltpu.PrefetchScalarGridSpec(
        num_scalar_prefetch=3,
        grid=(n_tiles,),
        in_specs=[pl.BlockSpec((1, 1, tm), lambda i, *_: (i, 0, 0), memory_space=pltpu.SMEM),
                  pl.BlockSpec((1, 1, tm), lambda i, *_: (jnp.minimum(i + 1, n_tiles - 1), 0, 0),
                               memory_space=pltpu.SMEM),
                  pl.BlockSpec((tm, 1), lambda i, *_: (i, 0)),
                  pl.BlockSpec((tm, 1), lambda i, *_: (i, 0)),
                  pl.BlockSpec(memory_space=pl.ANY),
                  wspec(w13, 0), wspec(w13, 0), wspec(w2s, 0),
                  wspec(w13, 1), wspec(w13, 1), wspec(w2s, 1),
                  pl.BlockSpec((1, d), lambda i, *_: (0, 0)),
                  pl.BlockSpec((1, d), lambda i, *_: (0, 0))],
        out_specs=pl.BlockSpec(memory_space=pl.ANY),
        scratch_shapes=[pltpu.VMEM((2, tm, d), F32), pltpu.VMEM((2, tm, d), F32),
                        pltpu.SemaphoreType.DMA((2,)), pltpu.SemaphoreType.DMA((2,))],
    )
    w1b, w3b, w2b = w1.astype(BF16), w3.astype(BF16), w2.astype(BF16)
    return pl.pallas_call(
        _moe_kernel,
        out_shape=jax.ShapeDtypeStruct((t, d), F32),
        grid_spec=grid_spec,
        compiler_params=_cparams(("arbitrary",)),
        name="moe",
    )(ea, eb, nvalid, src, src, wl, wh, h2d, w1b, w3b, w2b, w1b, w3b, w2b,
      ln_g.reshape(1, d), ln_b.reshape(1, d))


def kernel(x, ln0_g, ln0_b, w_in, pool_w, pool_scale, cmp_k_pe, cmp_k_w1, cmp_k_w2, cmp_v_pe, cmp_v_w1, cmp_v_w2, rel_bias, conv_w, conv_b, dt_bias, a_log, d_skip, ssd_norm_g, w_br_pool, w_br_nsa, w_br_ssd, w_out, ln1_g, ln1_b, router_w, router_b, exp_w1, exp_w3, exp_w2, ln2_g, ln2_b):
    bsz, s, d = x.shape
    assert d == D_MODEL and s % MIX_TS == 0 and s // CMP_STRIDE == LANE and (bsz * s) % PROJ_TM == 0
    t = bsz * s
    ts, ws, bc = _nsa_bias_tables(rel_bias)
    et, ovt = _nsa_consts(s)
    h = _ln(x.reshape(t, d), ln0_g, ln0_b)
    for i in range(DEPTH):
        proj = _proj(h, _prep_w_in(w_in[i]))
        kc, vc = _compress(proj, bsz, s, cmp_k_pe[i], cmp_k_w1[i], cmp_k_w2[i],
                           cmp_v_pe[i], cmp_v_w1[i], cmp_v_w2[i])
        nsa_out = _nsa(proj, kc, vc, ts, ws, bc, et, ovt, bsz, s)
        ssd_y = _ssd(proj, bsz, s, conv_w[i], conv_b[i], dt_bias[i], a_log[i], d_skip[i], ssd_norm_g[i])
        h1 = _mix(proj, nsa_out, ssd_y, h, bsz, s, pool_w[i], pool_scale[i], w_br_pool[i], w_br_nsa[i],
                  w_br_ssd[i], w_out[i], ln1_g[i], ln1_b[i])
        bucket, w_lo, w_hi, rank, cnt = _router(h1, router_w, router_b)
        counts = cnt[:N_BUCKETS, 0].astype(jnp.int32)
        h = _moe(h1, bucket[0], rank[0], counts, w_lo[0], w_hi[0], exp_w1[i], exp_w3[i], exp_w2[i],
                 ln2_g[i], ln2_b[i])
    return h.reshape(bsz, s, d)
```

```python
import functools
import math

import numpy as np
import jax
import jax.numpy as jnp
from jax import lax
from jax.experimental import pallas as pl
from jax.experimental.pallas import tpu as pltpu

F32 = jnp.float32
BF16 = jnp.bfloat16

D_MODEL = 1024
DEPTH = 2
DN_ALPHA = (2.0 * DEPTH) ** 0.25
LN_EPS = 1e-5
NEG = -1e30
BIG = 1e6

POOL_WINDOWS = (2, 4, 8, 16)
POOL_GROUP = 128
POOL_WIDTH = 512
MAX_POOL_WINDOW = 16

NSA_HEAD_DIM = 64
NSA_KV_HEADS = 2
NSA_HPG = 4
NSA_HEADS = 8
NSA_WIDTH = 512
CMP_LEN = 32
CMP_STRIDE = 16
CMP_HIDDEN = 128
SLC_LEN = 64
SLC_TOPN = 8
WIN_LEN = 512
Q_BLOCK = 128
REL_BUCKETS = 32
REL_MAX_DIST = 128

SSD_D_INNER = 1024
SSD_HEAD_DIM = 64
SSD_HEADS = 16
SSD_GROUPS = 4
SSD_STATE = 128
SSD_CONV = 4
SSD_CHUNK = 128
SSD_XBC = 2048

N_EXPERTS = 16
N_EXPERT_GROUPS = 4
EXPERTS_PER_GROUP = 4
D_EXPERT = 512
PAIRS = ((0, 1), (0, 2), (0, 3), (1, 2), (1, 3), (2, 3))
N_BUCKETS = N_EXPERT_GROUPS * len(PAIRS)

LANE = 128
SUBLANE = 8
VMEM_LIMIT = 48 * 1024 * 1024

C_POOL = 0
C_Q = 512
C_KV = 1024
C_NG = 1792
C_DT = 1920
C_XBC = 2048
C_Z = 4096
C_GATE = 5120
D_PAD = 8192

PROJ_TM = 1024
PROJ_TN = 1024
MIX_TS = 256
ROUTER_TM = 1024
MOE_TM = 256


def _dot(a, b):
    return jnp.dot(a, b, preferred_element_type=F32)


def _dot_nt(a, b):
    return lax.dot_general(a, b, (((1,), (1,)), ((), ())), preferred_element_type=F32)


def _dot_tn(a, b):
    return lax.dot_general(a, b, (((0,), (0,)), ((), ())), preferred_element_type=F32)


def _split3(x):
    x1 = x.astype(BF16)
    r1 = x - x1.astype(F32)
    x2 = r1.astype(BF16)
    x3 = (r1 - x2.astype(F32)).astype(BF16)
    return x1, x2, x3


def _layer_norm(x, g, b):
    mu = jnp.mean(x, -1, keepdims=True)
    xc = x - mu
    var = jnp.mean(xc * xc, -1, keepdims=True)
    return xc * lax.rsqrt(var + LN_EPS) * g + b


def _cparams(sem):
    return pltpu.CompilerParams(dimension_semantics=sem, vmem_limit_bytes=VMEM_LIMIT)


def _ln_kernel(x_ref, g_ref, b_ref, o_ref):
    o_ref[...] = _layer_norm(x_ref[...], g_ref[...], b_ref[...])


def _ln(x2d, g, b):
    t, d = x2d.shape
    tm = 512
    return pl.pallas_call(
        _ln_kernel,
        out_shape=jax.ShapeDtypeStruct((t, d), F32),
        grid=(t // tm,),
        in_specs=[pl.BlockSpec((tm, d), lambda i: (i, 0)),
                  pl.BlockSpec((1, d), lambda i: (0, 0)),
                  pl.BlockSpec((1, d), lambda i: (0, 0))],
        out_specs=pl.BlockSpec((tm, d), lambda i: (i, 0)),
        compiler_params=_cparams(("parallel",)),
        name="ln0",
    )(x2d, g.reshape(1, d), b.reshape(1, d))


def _proj_kernel(h_ref, w_ref, o_ref, hb_ref):
    @pl.when(pl.program_id(1) == 0)
    def _():
        hb_ref[...] = h_ref[...].astype(BF16)

    o_ref[...] = _dot(hb_ref[...], w_ref[...])


def _proj(h2d, w_pad):
    t, d = h2d.shape
    return pl.pallas_call(
        _proj_kernel,
        out_shape=jax.ShapeDtypeStruct((t, D_PAD), F32),
        grid=(t // PROJ_TM, D_PAD // PROJ_TN),
        in_specs=[pl.BlockSpec((PROJ_TM, d), lambda i, j: (i, 0)),
                  pl.BlockSpec((d, PROJ_TN), lambda i, j: (0, j))],
        out_specs=pl.BlockSpec((PROJ_TM, PROJ_TN), lambda i, j: (i, j)),
        scratch_shapes=[pltpu.VMEM((PROJ_TM, d), BF16)],
        compiler_params=_cparams(("parallel", "arbitrary")),
        name="proj",
    )(h2d, w_pad)


def _pad_cols(w, n):
    return jnp.pad(w, ((0, 0), (0, n - w.shape[1])))


def _prep_w_in(w_in):
    d = w_in.shape[0]
    o = 0
    w_pool = w_in[:, o:o + POOL_WIDTH]; o += POOL_WIDTH
    w_q = w_in[:, o:o + NSA_WIDTH]; o += NSA_WIDTH
    w_kv = w_in[:, o:o + 768]; o += 768
    w_ng = w_in[:, o:o + 24]; o += 24
    w_z = w_in[:, o:o + SSD_D_INNER]; o += SSD_D_INNER
    w_xbc = w_in[:, o:o + SSD_XBC]; o += SSD_XBC
    w_dt = w_in[:, o:o + SSD_HEADS]; o += SSD_HEADS
    w_gate = w_in[:, o:o + 3 * D_MODEL]
    w_q = w_q.reshape(d, NSA_KV_HEADS, NSA_HPG, NSA_HEAD_DIM).transpose(0, 2, 1, 3).reshape(d, NSA_WIDTH)
    w = jnp.concatenate([w_pool, w_q, w_kv, _pad_cols(w_ng, LANE), _pad_cols(w_dt, LANE),
                         w_xbc, w_z, w_gate], axis=1)
    return w.astype(BF16)


def _compress_one(tok_ref, pe_ref, w1_ref, w2_ref):
    n16 = tok_ref.shape[0] // CMP_STRIDE
    acc_a = jnp.zeros((n16, 2 * CMP_HIDDEN), F32)
    acc_b = jnp.zeros((n16, 2 * CMP_HIDDEN), F32)
    for l in range(CMP_STRIDE):
        rows = tok_ref[pl.ds(l, n16, stride=CMP_STRIDE), :]
        xa = (rows + pe_ref[l:l + 1, :]).astype(BF16)
        xb = (rows + pe_ref[CMP_STRIDE + l:CMP_STRIDE + l + 1, :]).astype(BF16)
        acc_a = acc_a + _dot(xa, w1_ref[l])
        acc_b = acc_b + _dot(xb, w1_ref[CMP_STRIDE + l])
    hid = acc_a + pltpu.roll(acc_b, n16 - 1, axis=0)
    hid = jax.nn.gelu(hid)
    return _dot(hid.astype(BF16), w2_ref[...])


def _compress_kernel(kt_ref, vt_ref, pek_ref, w1k_ref, w2k_ref, pev_ref, w1v_ref, w2v_ref,
                     kc_ref, vc_ref):
    kc_ref[0] = _compress_one(kt_ref, pek_ref, w1k_ref, w2k_ref).astype(BF16)
    vc_ref[0] = _compress_one(vt_ref, pev_ref, w1v_ref, w2v_ref).T.astype(BF16)


def _blockdiag2(w):
    z = jnp.zeros_like(w)
    return jnp.concatenate([jnp.concatenate([w, z], -1), jnp.concatenate([z, w], -1)], -2)


def _compress(proj2d, bsz, s, pe_k, w1_k, w2_k, pe_v, w1_v, w2_v):
    n16 = s // CMP_STRIDE

    def prep(pe, w1, w2):
        pe2 = jnp.concatenate([pe, pe], axis=-1)
        w1b = _blockdiag2(w1.reshape(CMP_LEN, NSA_HEAD_DIM, CMP_HIDDEN)).astype(BF16)
        w2b = _blockdiag2(w2).astype(BF16)
        return pe2, w1b, w2b

    pk = prep(pe_k, w1_k, w2_k)
    pv = prep(pe_v, w1_v, w2_v)
    full = lambda shape: pl.BlockSpec(shape, lambda b: (0,) * len(shape))
    wspecs = [full((CMP_LEN, LANE)), full((CMP_LEN, LANE, 2 * CMP_HIDDEN)), full((2 * CMP_HIDDEN, LANE))]
    return pl.pallas_call(
        _compress_kernel,
        out_shape=(jax.ShapeDtypeStruct((bsz, n16, LANE), BF16),
                   jax.ShapeDtypeStruct((bsz, n16, LANE), BF16)),
        grid=(bsz,),
        in_specs=[pl.BlockSpec((s, LANE), lambda b: (b, C_KV // LANE)),
                  pl.BlockSpec((s, LANE), lambda b: (b, C_KV // LANE + 1))] + wspecs + wspecs,
        out_specs=(pl.BlockSpec((1, n16, LANE), lambda b: (b, 0, 0)),
                   pl.BlockSpec((1, n16, LANE), lambda b: (b, 0, 0))),
        compiler_params=_cparams(("parallel",)),
        name="nsa_compress",
    )(proj2d, proj2d, *pk, *pv)


def _rel_bucket(dist):
    n = jnp.maximum(dist, 0)
    max_exact = REL_BUCKETS // 2
    nf = jnp.maximum(n, 1).astype(F32)
    large = max_exact + (jnp.log(nf / max_exact) / math.log(REL_MAX_DIST / max_exact)
                         * (REL_BUCKETS - max_exact)).astype(jnp.int32)
    large = jnp.minimum(large, REL_BUCKETS - 1)
    return jnp.where(n < max_exact, n, large)


LOG2E = math.log2(math.e)
WIN_BACK = WIN_LEN // Q_BLOCK
WIN_KEYS = (WIN_BACK + 1) * Q_BLOCK
WIN_STRIP_TILES = 2 * WIN_BACK + 1
SEL_TILES = 4
SEL_KEYS = SEL_TILES * Q_BLOCK
SEL_STRIP_TILES = 3 * SEL_TILES - 1
CMP_BIAS_ROWS = 2 * LANE
PV_ROWS = NSA_HEAD_DIM + 16


def _nsa_bias_tables(rel_bias):
    tbl = rel_bias.astype(F32)
    far = tbl[REL_BUCKETS - 1]
    hq = NSA_HPG * Q_BLOCK

    def lookup(dist):
        onehot = (_rel_bucket(jnp.asarray(dist))[..., None] == jnp.arange(REL_BUCKETS)).astype(F32)
        return jnp.einsum("...b,bh->...h", onehot, tbl, precision=lax.Precision.HIGHEST)

    k = np.arange(Q_BLOCK)[None, :, None]
    q = np.arange(Q_BLOCK)[None, None, :]

    def strip(first, last, visible):
        n = first - last + 1
        dist = np.arange(first, last - 1, -1)[:, None, None] * Q_BLOCK + q - k
        v = jnp.where(jnp.asarray(visible(dist))[..., None], (lookup(dist) - far) * LOG2E, NEG)
        v = v.reshape(n, Q_BLOCK, Q_BLOCK, NSA_KV_HEADS, NSA_HPG).transpose(3, 0, 1, 4, 2)
        return v.reshape(NSA_KV_HEADS, n * Q_BLOCK, hq)

    ts = strip(2 * SEL_TILES - 1, 1 - SEL_TILES, lambda d: d >= 0)
    ws = strip(WIN_BACK, -WIN_BACK, lambda d: (d >= 0) & (d < WIN_LEN))
    r = np.arange(CMP_BIAS_ROWS)[:, None]
    dist_c = q[0] - CMP_STRIDE * (r - LANE) - (CMP_LEN - 1)
    bc = jnp.where(jnp.asarray(dist_c >= 0)[..., None], lookup(dist_c) * LOG2E, NEG)
    bc = bc.reshape(CMP_BIAS_ROWS, Q_BLOCK, NSA_KV_HEADS, NSA_HPG).transpose(2, 0, 3, 1)
    bc = bc.reshape(NSA_KV_HEADS, CMP_BIAS_ROWS, hq)
    return ts, ws, bc


def _nsa_consts(s):
    n_slc = s // SLC_LEN
    et = (np.arange(s)[:, None] // SLC_LEN == np.arange(LANE)[None, :]).astype(np.float32)
    n_cmp = (s - CMP_LEN) // CMP_STRIDE + 1
    c0 = np.arange(n_cmp) * CMP_STRIDE
    s0 = np.arange(n_slc) * SLC_LEN
    ov = np.clip(np.minimum(c0[None, :] + CMP_LEN, s0[:, None] + SLC_LEN)
                 - np.maximum(c0[None, :], s0[:, None]), 0, None) / CMP_LEN
    ovt = np.zeros((n_slc, LANE), np.float32)
    ovt[:, :n_cmp] = ov
    return jnp.asarray(et, BF16), jnp.asarray(ovt, BF16)


def _nsa_kernel(q_ref, ng_ref, ks_ref, vs_ref, kw_ref, vw_ref, kc_ref, vct_ref,
                ts_ref, ws_ref, bc_ref, et_ref, ovt_ref, o_ref,
                ksb_ref, kwb_ref, vst_ref, vwt_ref, os_ref, *, nq, n_slc):
    qi = pl.program_id(1)
    hq = NSA_HPG * Q_BLOCK
    dh = NSA_HEAD_DIM
    groups = range(NSA_KV_HEADS)
    gsl = [slice(g * dh, (g + 1) * dh) for g in groups]

    @pl.when(qi == 0)
    def _():
        ksb_ref[:, 0:LANE] = ks_ref[...].astype(BF16)
        ksb_ref[:, LANE:] = et_ref[...]
        kwb_ref[...] = kw_ref[...].astype(BF16)
        ones_rows = jnp.where(lax.broadcasted_iota(jnp.int32, (PV_ROWS - dh, ks_ref.shape[0]), 0) == 0, 1.0, 0.0)
        for g in groups:
            vst_ref[g, dh:, :] = ones_rows.astype(BF16)
            vwt_ref[g, dh:, :] = ones_rows.astype(BF16)
        for kt in range(nq):
            sl = slice(kt * Q_BLOCK, (kt + 1) * Q_BLOCK)
            vs_t = vs_ref[sl, :].T
            vw_t = vw_ref[sl, :].T
            for g in groups:
                vst_ref[g, 0:dh, sl] = vs_t[gsl[g], :].astype(BF16)
                vwt_ref[g, 0:dh, sl] = vw_t[gsl[g], :].astype(BF16)

    row_lo = lax.broadcasted_iota(jnp.int32, (LANE, hq), 0) < dh

    def heads4(m):
        return jnp.concatenate([m] * NSA_HPG, axis=1)

    gates_t = jax.nn.sigmoid(ng_ref[...]).T
    q_t = jnp.concatenate(
        [(q_ref[:, h * LANE:(h + 1) * LANE] * (dh ** -0.5 * LOG2E)).T for h in range(NSA_HPG)], axis=1)
    qg_t = [jnp.where(row_lo if g == 0 else jnp.logical_not(row_lo), q_t, 0.0).astype(BF16) for g in groups]

    boff = pl.multiple_of(LANE - SUBLANE * qi, SUBLANE)
    jrow = lax.broadcasted_iota(jnp.int32, (n_slc, LANE), 0)
    tj = qi * Q_BLOCK + lax.broadcasted_iota(jnp.int32, (n_slc, LANE), 1)
    jt = lax.shift_right_logical(tj, SLC_LEN.bit_length() - 1)
    forced = (jrow == 0) | (jrow == jt) | (jrow == jt - 1)
    valid = jrow * SLC_LEN <= tj
    o_c = []
    q_aug = []
    for g in groups:
        s = _dot(kc_ref[0], qg_t[g]) + bc_ref[g, pl.ds(boff, LANE), :]
        mx = jnp.max(s, axis=0, keepdims=True)
        p = jnp.exp2(s - mx)
        lsum = jnp.sum(p, axis=0, keepdims=True)
        pb = (p * jnp.where(mx > 0.5 * NEG, 1.0 / lsum, 0.0)).astype(BF16)
        o_c.append(_dot(vct_ref[0, gsl[g], :], pb))
        imp4 = _dot(ovt_ref[...], pb)
        imp = imp4[:, 0:LANE]
        for h in range(1, NSA_HPG):
            imp = imp + imp4[:, h * LANE:(h + 1) * LANE]
        val = jnp.where(forced, BIG, jnp.where(valid, imp, -BIG))
        rank = jnp.zeros((n_slc, LANE), F32)
        for i in range(n_slc):
            ci = val[i:i + 1, :]
            rank = rank + jnp.where(ci > val, 1.0, 0.0) + jnp.where(ci == val, (jrow > i).astype(F32), 0.0)
        block_rows = heads4(jnp.where(rank < SLC_TOPN, 0.0, NEG)).astype(BF16)
        q_aug.append(jnp.concatenate(
            [qg_t[g], block_rows, jnp.zeros((LANE - n_slc, hq), BF16)], axis=0))

    n_super = qi // SEL_TILES + 1
    diag = qi - SEL_TILES * (n_super - 1)

    def selected(n):
        nk = n * SEL_KEYS
        lo = max(n - 2, 0) * SEL_KEYS
        toff = pl.multiple_of((2 * SEL_TILES - 1 - diag) * Q_BLOCK - (nk - lo - SEL_KEYS), Q_BLOCK)
        for g in groups:
            parts = [(lo, nk, _dot(ksb_ref[lo:nk, :], q_aug[g]) + ts_ref[g, pl.ds(toff, nk - lo), :])]
            if lo > 0:
                parts.append((0, lo, _dot(ksb_ref[0:lo, :], q_aug[g])))
            m = parts[0][2].max(axis=0, keepdims=True)
            for _, _, s in parts[1:]:
                m = jnp.maximum(m, s.max(axis=0, keepdims=True))
            acc = None
            for a, b, s in parts:
                pv = _dot(vst_ref[g, :, a:b], jnp.exp2(s - m).astype(BF16))
                acc = pv if acc is None else acc + pv
            os_ref[g] = acc[0:dh] * (1.0 / acc[dh:dh + 1])

    for n in range(1, nq // SEL_TILES + 1):
        pl.when(n_super == n)(functools.partial(selected, n))
    o_s = [os_ref[g] for g in groups]

    w0 = jnp.maximum(qi - WIN_BACK, 0)
    woff = pl.multiple_of(w0 * Q_BLOCK, Q_BLOCK)
    boff_w = pl.multiple_of((WIN_BACK - (qi - w0)) * Q_BLOCK, Q_BLOCK)
    kw_bf = kwb_ref[pl.ds(woff, WIN_KEYS), :]
    o_w = []
    for g in groups:
        s = _dot(kw_bf, qg_t[g]) + ws_ref[g, pl.ds(boff_w, WIN_KEYS), :]
        p = jnp.exp2(s - jnp.max(s, axis=0, keepdims=True)).astype(BF16)
        acc = _dot(vwt_ref[g, :, pl.ds(woff, WIN_KEYS)], p)
        o_w.append(acc[0:dh] * (1.0 / acc[dh:dh + 1]))

    for h in range(NSA_HPG):
        hl = slice(h * LANE, (h + 1) * LANE)
        halves = []
        for g in groups:
            c = g * NSA_HPG + h
            halves.append(gates_t[c:c + 1, :] * o_c[g][:, hl]
                          + gates_t[NSA_HEADS + c:NSA_HEADS + c + 1, :] * o_s[g][:, hl]
                          + gates_t[2 * NSA_HEADS + c:2 * NSA_HEADS + c + 1, :] * o_w[g][:, hl])
        o_ref[:, hl] = jnp.concatenate(halves, axis=0).T


def _nsa(proj2d, kc, vct, ts, ws, bc, et, ovt, bsz, s):
    nq = s // Q_BLOCK
    n_slc = s // SLC_LEN
    hq = NSA_HPG * Q_BLOCK
    kvb = C_KV // LANE
    kv_spec = lambda blk: pl.BlockSpec((s, LANE), lambda b, qi: (b, kvb + blk))
    full = lambda shape: pl.BlockSpec(shape, lambda b, qi: (0,) * len(shape), pipeline_mode=pl.Buffered(1))
    return pl.pallas_call(
        functools.partial(_nsa_kernel, nq=nq, n_slc=n_slc),
        out_shape=jax.ShapeDtypeStruct((bsz * s, NSA_WIDTH), F32),
        grid=(bsz, nq),
        in_specs=[pl.BlockSpec((Q_BLOCK, NSA_WIDTH), lambda b, qi: (b * nq + qi, C_Q // NSA_WIDTH)),
                  pl.BlockSpec((Q_BLOCK, LANE), lambda b, qi: (b * nq + qi, C_NG // LANE)),
                  kv_spec(2), kv_spec(3), kv_spec(4), kv_spec(5),
                  pl.BlockSpec((1, s // CMP_STRIDE, LANE), lambda b, qi: (b, 0, 0)),
                  pl.BlockSpec((1, LANE, s // CMP_STRIDE), lambda b, qi: (b, 0, 0)),
                  full((NSA_KV_HEADS, SEL_STRIP_TILES * Q_BLOCK, hq)),
                  full((NSA_KV_HEADS, WIN_STRIP_TILES * Q_BLOCK, hq)),
                  full((NSA_KV_HEADS, CMP_BIAS_ROWS, hq)),
                  full((s, LANE)),
                  full((n_slc, LANE))],
        out_specs=pl.BlockSpec((Q_BLOCK, NSA_WIDTH), lambda b, qi: (b * nq + qi, 0)),
        scratch_shapes=[pltpu.VMEM((s, 2 * LANE), BF16), pltpu.VMEM((s, LANE), BF16),
                        pltpu.VMEM((NSA_KV_HEADS, PV_ROWS, s), BF16), pltpu.VMEM((NSA_KV_HEADS, PV_ROWS, s), BF16),
                        pltpu.VMEM((NSA_KV_HEADS, NSA_HEAD_DIM, hq), F32)],
        compiler_params=_cparams(("parallel", "arbitrary")),
        name="nsa_attn",
    )(proj2d, proj2d, proj2d, proj2d, proj2d, proj2d, kc, vct, ts, ws, bc, et, ovt)


def _ssd_kernel(xbc_ref, prev_ref, z_ref, dt_ref, cw_ref, cb_ref, dtb_ref, alog_ref, alogx_ref,
                dskip_ref, ng_ref, e16_ref, tril_ref, o_ref, xcat_ref, state_ref):
    c = pl.program_id(1)
    L = SSD_CHUNK
    pair_w = 2 * SSD_HEAD_DIM

    @pl.when(c == 0)
    def _():
        state_ref[...] = jnp.zeros(state_ref.shape, F32)

    xcat_ref[0:SUBLANE, :] = jnp.where(c > 0, prev_ref[...], 0.0)
    xcat_ref[SUBLANE:, :] = xbc_ref[...]
    conv = cb_ref[...]
    for k in range(SSD_CONV):
        conv = conv + cw_ref[k:k + 1, :] * xcat_ref[pl.ds(SUBLANE - (SSD_CONV - 1) + k, L), :]
    xbc = jax.nn.silu(conv)
    xs = xbc[:, :SSD_D_INNER]
    bmat = xbc[:, SSD_D_INNER:SSD_D_INNER + SSD_GROUPS * SSD_STATE]
    cmat = xbc[:, SSD_D_INNER + SSD_GROUPS * SSD_STATE:]

    dt = jax.nn.softplus(dt_ref[...] + dtb_ref[...])
    tril = tril_ref[...]
    e16 = e16_ref[...]
    d1, d2, d3 = _split3(dt)
    dt_x = _dot(d1, e16) + _dot(d2, e16) + _dot(d3, e16)
    a_c = dt * (-jnp.exp(alog_ref[...]))
    a_x = dt_x * (-jnp.exp(alogx_ref[...]))
    c1, c2, c3 = _split3(a_c)
    acs = _dot(tril, c1) + _dot(tril, c2) + _dot(tril, c3)
    x1, x2, x3 = _split3(a_x)
    acs_x = _dot(tril, x1) + _dot(tril, x2) + _dot(tril, x3)
    acs_t = acs.T
    last = acs_x[L - 1:L, :]
    eacs = jnp.exp(acs_x)
    decay_in = jnp.exp(last - acs_x)
    cdec = jnp.exp(last)
    xdt = xs * dt_x
    wst = xdt * decay_in

    row = lax.broadcasted_iota(jnp.int32, (L, L), 0)
    lane = lax.broadcasted_iota(jnp.int32, (L, L), 1)
    tri = row >= lane
    lo = lane < SSD_HEAD_DIM

    ys = []
    for g in range(SSD_GROUPS):
        cg = cmat[:, g * SSD_STATE:(g + 1) * SSD_STATE].astype(BF16)
        bg = bmat[:, g * SSD_STATE:(g + 1) * SSD_STATE].astype(BF16)
        cb = _dot_nt(cg, bg)
        for pr in range(2):
            pp = g * 2 + pr
            ls = slice(pp * pair_w, (pp + 1) * pair_w)
            xp = xdt[:, ls]
            yd = None
            for e in range(2):
                hd = 2 * pp + e
                diff = acs[:, hd:hd + 1] - acs_t[hd:hd + 1, :]
                seg = jnp.exp(jnp.where(tri, diff, NEG))
                pm = (cb * seg).astype(BF16)
                xe = jnp.where(lo if e == 0 else jnp.logical_not(lo), xp, 0.0).astype(BF16)
                term = _dot(pm, xe)
                yd = term if yd is None else yd + term
            st = state_ref[pp]
            yo = _dot(cg, st.astype(BF16)) * eacs[:, ls]
            ys.append(yd + yo)
            state_ref[pp] = cdec[:, ls] * st + _dot_tn(bg, wst[:, ls].astype(BF16))

    y = jnp.concatenate(ys, axis=1) + xs * dskip_ref[...]
    y = y * jax.nn.silu(z_ref[...])
    gw = SSD_D_INNER // SSD_GROUPS
    outs = []
    for gi in range(SSD_GROUPS):
        yg = y[:, gi * gw:(gi + 1) * gw]
        outs.append(yg * lax.rsqrt(jnp.mean(yg * yg, -1, keepdims=True) + LN_EPS))
    o_ref[...] = jnp.concatenate(outs, axis=1) * ng_ref[...]


def _ssd(proj2d, bsz, s, conv_w, conv_b, dt_bias, a_log, d_skip, norm_g):
    nc = s // SSD_CHUNK
    L = SSD_CHUNK
    pad16 = lambda v: jnp.pad(v.reshape(1, SSD_HEADS), ((0, 0), (0, LANE - SSD_HEADS)))
    rep64 = lambda v: jnp.repeat(v, SSD_HEAD_DIM).reshape(1, SSD_D_INNER)
    e16 = np.zeros((LANE, SSD_D_INNER), np.float32)
    e16[np.arange(SSD_D_INNER) // SSD_HEAD_DIM, np.arange(SSD_D_INNER)] = 1.0
    tril = np.tril(np.ones((L, L), np.float32))
    full = lambda shape: pl.BlockSpec(shape, lambda b, c: (0,) * len(shape))
    rows8 = s // SUBLANE
    return pl.pallas_call(
        _ssd_kernel,
        out_shape=jax.ShapeDtypeStruct((bsz * s, SSD_D_INNER), F32),
        grid=(bsz, nc),
        in_specs=[pl.BlockSpec((L, SSD_XBC), lambda b, c: (b * nc + c, C_XBC // SSD_XBC)),
                  pl.BlockSpec((SUBLANE, SSD_XBC),
                               lambda b, c: (jnp.maximum(b * rows8 + c * (L // SUBLANE) - 1, 0), C_XBC // SSD_XBC)),
                  pl.BlockSpec((L, SSD_D_INNER), lambda b, c: (b * nc + c, C_Z // SSD_D_INNER)),
                  pl.BlockSpec((L, LANE), lambda b, c: (b * nc + c, C_DT // LANE)),
                  full((SSD_CONV, SSD_XBC)), full((1, SSD_XBC)), full((1, LANE)), full((1, LANE)),
                  full((1, SSD_D_INNER)), full((1, SSD_D_INNER)), full((1, SSD_D_INNER)),
                  full((LANE, SSD_D_INNER)), full((L, L))],
        out_specs=pl.BlockSpec((L, SSD_D_INNER), lambda b, c: (b * nc + c, 0)),
        scratch_shapes=[pltpu.VMEM((L + SUBLANE, SSD_XBC), F32),
                        pltpu.VMEM((SSD_HEADS // 2, SSD_STATE, 2 * SSD_HEAD_DIM), F32)],
        compiler_params=_cparams(("parallel", "arbitrary")),
        name="ssd",
    )(proj2d, proj2d, proj2d, proj2d, conv_w, conv_b.reshape(1, SSD_XBC), pad16(dt_bias), pad16(a_log),
      rep64(a_log), rep64(d_skip), norm_g.reshape(1, SSD_D_INNER),
      jnp.asarray(e16, BF16), jnp.asarray(tril, BF16))


def _mix_kernel(up_ref, upprev_ref, nsa_ref, ssd_ref, ga_ref, gb_ref, gc_ref, h_ref,
                pw_ref, ps_ref, wbp_ref, wbn_ref, wbs_ref, wo_ref, lg_ref, lb_ref, o_ref, *, ts):
    si = pl.program_id(1)
    w0 = MAX_POOL_WINDOW
    prev = jnp.where(si > 0, upprev_ref[...], 0.0)
    xcat = jnp.concatenate([prev, up_ref[...]], axis=0)
    tpos = si * ts + lax.broadcasted_iota(jnp.int32, (ts, POOL_GROUP), 0)
    ygs = []
    for gi, w in enumerate(POOL_WINDOWS):
        x = xcat[:, gi * POOL_GROUP:(gi + 1) * POOL_GROUP]
        acc = x
        d = 1
        while d < w:
            acc = acc + pltpu.roll(acc, d, axis=0)
            d *= 2
        cnt = jnp.minimum(tpos + 1, w).astype(F32)
        r = acc[w0:, :] / cnt - x[w0:, :]
        ygs.append(_dot(r.astype(BF16), pw_ref[gi]))
    y = jnp.concatenate(ygs, axis=1) * ps_ref[...]
    br_a = _dot(y.astype(BF16), wbp_ref[...])
    br_b = _dot(nsa_ref[...].astype(BF16), wbn_ref[...])
    br_c = _dot(ssd_ref[...].astype(BF16), wbs_ref[...])
    mixed = (jax.nn.sigmoid(ga_ref[...]) * br_a + jax.nn.sigmoid(gb_ref[...]) * br_b
             + jax.nn.sigmoid(gc_ref[...]) * br_c)
    mix = _dot(mixed.astype(BF16), wo_ref[...])
    o_ref[...] = _layer_norm(DN_ALPHA * h_ref[...] + mix, lg_ref[...], lb_ref[...])


def _mix(proj2d, nsa_out, ssd_y, h2d, bsz, s, pool_w, pool_scale, w_br_pool, w_br_nsa, w_br_ssd,
         w_out, ln_g, ln_b):
    ts = MIX_TS
    ns = s // ts
    d = D_MODEL
    w0 = MAX_POOL_WINDOW
    wbn = w_br_nsa.reshape(NSA_KV_HEADS, NSA_HPG, NSA_HEAD_DIM, d).transpose(1, 0, 2, 3).reshape(NSA_WIDTH, d)
    full = lambda shape: pl.BlockSpec(shape, lambda b, si: (0,) * len(shape))
    row = lambda width, cb: pl.BlockSpec((ts, width), lambda b, si: (b * ns + si, cb))
    return pl.pallas_call(
        functools.partial(_mix_kernel, ts=ts),
        out_shape=jax.ShapeDtypeStruct((bsz * s, d), F32),
        grid=(bsz, ns),
        in_specs=[row(POOL_WIDTH, C_POOL // POOL_WIDTH),
                  pl.BlockSpec((w0, POOL_WIDTH),
                               lambda b, si: (jnp.maximum((b * s + si * ts) // w0 - 1, 0), C_POOL // POOL_WIDTH)),
                  row(NSA_WIDTH, 0), row(d, 0),
                  row(d, C_GATE // d), row(d, C_GATE // d + 1), row(d, C_GATE // d + 2),
                  row(d, 0),
                  full((len(POOL_WINDOWS), POOL_GROUP, POOL_GROUP)), full((1, POOL_WIDTH)),
                  full((POOL_WIDTH, d)), full((NSA_WIDTH, d)), full((d, d)), full((d, d)),
                  full((1, d)), full((1, d))],
        out_specs=pl.BlockSpec((ts, d), lambda b, si: (b * ns + si, 0)),
        compiler_params=_cparams(("parallel", "arbitrary")),
        name="mix",
    )(proj2d, proj2d, nsa_out, ssd_y, proj2d, proj2d, proj2d, h2d,
      pool_w.astype(BF16), pool_scale.reshape(1, POOL_WIDTH), w_br_pool.astype(BF16), wbn.astype(BF16),
      w_br_ssd.astype(BF16), w_out.astype(BF16), ln_g.reshape(1, d), ln_b.reshape(1, d))


RANK_CHUNK = 256
BUCKET_ROWS = 32


def _router_kernel(h_ref, rw_ref, rb_ref, tri_ref, bucket_ref, wlo_ref, whi_ref, rank_ref, cnt_ref, run_ref):
    @pl.when(pl.program_id(0) == 0)
    def _():
        run_ref[...] = jnp.zeros(run_ref.shape, F32)

    logits = _dot_nt(rw_ref[...], h_ref[...].astype(BF16))
    lg = [logits[e:e + 1, :] for e in range(N_EXPERTS)]
    mx = lg[0]
    for e in range(1, N_EXPERTS):
        mx = jnp.maximum(mx, lg[e])
    ex = [jnp.exp(v - mx) for v in lg]
    den = ex[0]
    for e in range(1, N_EXPERTS):
        den = den + ex[e]
    probs = [v / den for v in ex]
    score = [probs[e] + rb_ref[e:e + 1, :] for e in range(N_EXPERTS)]

    def group_vals(vals, grp):
        out = []
        for k in range(EXPERTS_PER_GROUP):
            v = vals[k]
            for gi in range(1, N_EXPERT_GROUPS):
                v = jnp.where(grp == gi, vals[gi * EXPERTS_PER_GROUP + k], v)
            out.append(v)
        return out

    gscore = []
    for gi in range(N_EXPERT_GROUPS):
        sc = score[gi * EXPERTS_PER_GROUP:(gi + 1) * EXPERTS_PER_GROUP]
        best = None
        for a, b in PAIRS:
            v = sc[a] + sc[b]
            best = v if best is None else jnp.maximum(best, v)
        gscore.append(best)
    grp = jnp.zeros(gscore[0].shape, jnp.int32)
    best = gscore[0]
    for gi in range(1, N_EXPERT_GROUPS):
        better = gscore[gi] > best
        grp = jnp.where(better, gi, grp)
        best = jnp.where(better, gscore[gi], best)

    sc = group_vals(score, grp)
    pr = group_vals(probs, grp)
    first = jnp.zeros(grp.shape, jnp.int32)
    fv = sc[0]
    for k in range(1, EXPERTS_PER_GROUP):
        better = sc[k] > fv
        first = jnp.where(better, k, first)
        fv = jnp.where(better, sc[k], fv)
    second = jnp.full(grp.shape, -1, jnp.int32)
    sv = jnp.full(fv.shape, -jnp.inf, F32)
    for k in range(EXPERTS_PER_GROUP):
        better = (first != k) & ((sc[k] > sv) | (second < 0))
        second = jnp.where(better, k, second)
        sv = jnp.where(better, sc[k], sv)
    lo = jnp.minimum(first, second)
    hi = jnp.maximum(first, second)
    p_lo = pr[0]
    p_hi = pr[0]
    for k in range(1, EXPERTS_PER_GROUP):
        p_lo = jnp.where(lo == k, pr[k], p_lo)
        p_hi = jnp.where(hi == k, pr[k], p_hi)
    tot = p_lo + p_hi
    pair = jnp.zeros(grp.shape, jnp.int32)
    for pi, (a, b) in enumerate(PAIRS):
        pair = jnp.where((lo == a) & (hi == b), pi, pair)
    bucket = grp * len(PAIRS) + pair
    bucket_ref[...] = bucket
    wlo_ref[...] = p_lo / tot
    whi_ref[...] = p_hi / tot

    tm = bucket.shape[1]
    onehot = jnp.where(lax.broadcasted_iota(jnp.int32, (BUCKET_ROWS, tm), 0) == bucket, 1.0, 0.0)
    run = run_ref[...]
    ranks = []
    for c in range(tm // RANK_CHUNK):
        oc = onehot[:, c * RANK_CHUNK:(c + 1) * RANK_CHUNK]
        before = _dot(oc.astype(BF16), tri_ref[...]) + jnp.concatenate([run] * (RANK_CHUNK // LANE), axis=1)
        ranks.append(jnp.sum(oc * before, axis=0, keepdims=True))
        run = run + jnp.broadcast_to(jnp.sum(oc, axis=1, keepdims=True), run.shape)
    rank_ref[...] = jnp.concatenate(ranks, axis=1).astype(jnp.int32)
    run_ref[...] = run
    cnt_ref[...] = run


def _router(h2d, router_w, router_b):
    t, d = h2d.shape
    tm = ROUTER_TM
    rw = jnp.pad(router_w.T, ((0, LANE - N_EXPERTS), (0, 0))).astype(BF16)
    rb = jnp.broadcast_to(jnp.pad(router_b, (0, LANE - N_EXPERTS))[:, None], (LANE, tm))
    tri = jnp.asarray(np.triu(np.ones((RANK_CHUNK, RANK_CHUNK), np.float32), 1), BF16)
    vec = lambda dt: jax.ShapeDtypeStruct((1, t), dt)
    row = pl.BlockSpec((1, tm), lambda i: (0, i))
    return pl.pallas_call(
        _router_kernel,
        out_shape=(vec(jnp.int32), vec(F32), vec(F32), vec(jnp.int32),
                   jax.ShapeDtypeStruct((BUCKET_ROWS, LANE), F32)),
        grid=(t // tm,),
        in_specs=[pl.BlockSpec((tm, d), lambda i: (i, 0)),
                  pl.BlockSpec((LANE, d), lambda i: (0, 0)),
                  pl.BlockSpec((LANE, tm), lambda i: (0, 0)),
                  pl.BlockSpec((RANK_CHUNK, RANK_CHUNK), lambda i: (0, 0))],
        out_specs=(row, row, row, row, pl.BlockSpec((BUCKET_ROWS, LANE), lambda i: (0, 0))),
        scratch_shapes=[pltpu.VMEM((BUCKET_ROWS, LANE), F32)],
        compiler_params=_cparams(("arbitrary",)),
        name="router",
    )(h2d, rw, rb, tri)


def _moe_plan(bucket, rank, counts, w_lo, w_hi, t):
    tm = MOE_TM
    n_tiles = t // tm + N_BUCKETS
    p_rows = n_tiles * tm
    tiles_per = (counts + tm - 1) // tm
    tile_end = jnp.cumsum(tiles_per)
    tile_start = tile_end - tiles_per
    in_bucket = bucket[:, None] == jnp.arange(N_BUCKETS)[None, :]
    dest = jnp.sum(jnp.where(in_bucket, (tile_start * tm)[None, :], 0), axis=1) + rank
    payload = jnp.stack([jnp.arange(t, dtype=jnp.int32), lax.bitcast_convert_type(w_lo, jnp.int32),
                         lax.bitcast_convert_type(w_hi, jnp.int32)], axis=1)
    plan = jnp.zeros((p_rows, 3), jnp.int32).at[dest].set(payload, unique_indices=True)
    src = plan[:, 0]
    wl = lax.bitcast_convert_type(plan[:, 1], F32)
    wh = lax.bitcast_convert_type(plan[:, 2], F32)
    n_used = tile_end[-1]
    tile_ids = jnp.arange(n_tiles)
    tb = jnp.sum((tile_end[None, :] <= tile_ids[:, None]).astype(jnp.int32), axis=1)
    tb = jnp.where(tile_ids < n_used, tb, tb[jnp.maximum(n_used - 1, 0)])
    tb = jnp.minimum(tb, N_BUCKETS - 1)
    pairs = np.asarray(PAIRS, np.int32)
    grp = tb // len(PAIRS)
    ea = grp * EXPERTS_PER_GROUP + jnp.asarray(pairs[:, 0])[tb % len(PAIRS)]
    eb = grp * EXPERTS_PER_GROUP + jnp.asarray(pairs[:, 1])[tb % len(PAIRS)]
    nvalid = jnp.clip(counts[tb] - (tile_ids - tile_start[tb]) * tm, 0, tm)
    nvalid = jnp.where(tile_ids < n_used, nvalid, 0).astype(jnp.int32)
    return (src.reshape(n_tiles, 1, tm), wl.reshape(p_rows, 1), wh.reshape(p_rows, 1),
            ea.astype(jnp.int32), eb.astype(jnp.int32), nvalid, n_tiles)


DMA_UNROLL = 8


def _for_rows(n, fn):
    nfull = n // DMA_UNROLL

    def group(c, carry):
        for u in range(DMA_UNROLL):
            fn(c * DMA_UNROLL + u)
        return carry

    def single(r, carry):
        fn(r)
        return carry

    lax.fori_loop(0, nfull, group, 0)
    lax.fori_loop(nfull * DMA_UNROLL, n, single, 0)


def _moe_kernel(ea_ref, eb_ref, nv_ref, src_ref, srcn_ref, wl_ref, wh_ref, h_hbm,
                w1a_ref, w3a_ref, w2a_ref, w1b_ref, w3b_ref, w2b_ref, lg_ref, lb_ref,
                o_hbm, xbuf, obuf, gsem, ssem):
    i = pl.program_id(0)
    n = pl.num_programs(0)
    tm = xbuf.shape[1]
    slot = lax.rem(i, 2)
    nslot = 1 - slot
    nv = nv_ref[i]

    def gather_start(idx_ref, sl):
        def one(r):
            pltpu.make_async_copy(h_hbm.at[pl.ds(idx_ref[0, 0, r], 1)], xbuf.at[sl, pl.ds(r, 1)],
                                  gsem.at[sl]).start()
        _for_rows(tm, one)

    def gather_wait(sl):
        def one(r):
            pltpu.make_async_copy(h_hbm.at[pl.ds(0, 1)], xbuf.at[sl, pl.ds(r, 1)], gsem.at[sl]).wait()
        _for_rows(tm, one)

    def scatter_start(count, sl):
        def one(r):
            pltpu.make_async_copy(obuf.at[sl, pl.ds(r, 1)], o_hbm.at[pl.ds(src_ref[0, 0, r], 1)],
                                  ssem.at[sl]).start()
        _for_rows(count, one)

    def scatter_wait(count, sl):
        def one(r):
            pltpu.make_async_copy(obuf.at[sl, pl.ds(r, 1)], o_hbm.at[pl.ds(0, 1)], ssem.at[sl]).wait()
        _for_rows(count, one)

    @pl.when((i == 0) & (nv > 0))
    def _():
        gather_start(src_ref, slot)

    nv_next = nv_ref[jnp.minimum(i + 1, n - 1)]

    @pl.when((i + 1 < n) & (nv_next > 0))
    def _():
        gather_start(srcn_ref, nslot)

    nv_prev2 = nv_ref[jnp.maximum(i - 2, 0)]

    @pl.when((i >= 2) & (nv_prev2 > 0))
    def _():
        scatter_wait(nv_prev2, slot)

    @pl.when(nv > 0)
    def _():
        gather_wait(slot)
        x = xbuf[slot]
        xb = x.astype(BF16)

        def expert(w1_ref, w3_ref, w2_ref):
            hmid = jax.nn.silu(_dot(xb, w1_ref[0])) * _dot(xb, w3_ref[0])
            return _dot(hmid.astype(BF16), w2_ref[0])

        y = wl_ref[...] * expert(w1a_ref, w3a_ref, w2a_ref) + wh_ref[...] * expert(w1b_ref, w3b_ref, w2b_ref)
        obuf[slot] = _layer_norm(DN_ALPHA * x + y, lg_ref[...], lb_ref[...])
        scatter_start(nv, slot)

    @pl.when(i == n - 1)
    def _():
        nv_prev1 = nv_ref[jnp.maximum(i - 1, 0)]

        @pl.when((i >= 1) & (nv_prev1 > 0))
        def _():
            scatter_wait(nv_prev1, nslot)

        @pl.when(nv > 0)
        def _():
            scatter_wait(nv, slot)


def _moe(h2d, bucket, rank, counts, w_lo, w_hi, w1, w3, w2, ln_g, ln_b):
    t, d = h2d.shape
    tm = MOE_TM
    src, wl, wh, ea, eb, nvalid, n_tiles = _moe_plan(bucket, rank, counts, w_lo, w_hi, t)
    wspec = lambda shape, which: pl.BlockSpec(
        shape, (lambda i, ea_r, eb_r, nv_r: (ea_r[i], 0, 0)) if which == 0
        else (lambda i, ea_r, eb_r, nv_r: (eb_r[i], 0, 0)))
    w13 = (1, d, D_EXPERT)
    w2s = (1, D_EXPERT, d)
    grid_spec = pltpu.PrefetchScalarGridSpec(
        num_scalar_prefetch=3,
        grid=(n_tiles,),
        in_specs=[pl.BlockSpec((1, 1, tm), lambda i, *_: (i, 0, 0), memory_space=pltpu.SMEM),
                  pl.BlockSpec((1, 1, tm), lambda i, *_: (jnp.minimum(i + 1, n_tiles - 1), 0, 0),
                               memory_space=pltpu.SMEM),
                  pl.BlockSpec((tm, 1), lambda i, *_: (i, 0)),
                  pl.BlockSpec((tm, 1), lambda i, *_: (i, 0)),
                  pl.BlockSpec(memory_space=pl.ANY),
                  wspec(w13, 0), wspec(w13, 0), wspec(w2s, 0),
                  wspec(w13, 1), wspec(w13, 1), wspec(w2s, 1),
                  pl.BlockSpec((1, d), lambda i, *_: (0, 0)),
                  pl.BlockSpec((1, d), lambda i, *_: (0, 0))],
        out_specs=pl.BlockSpec(memory_space=pl.ANY),
        scratch_shapes=[pltpu.VMEM((2, tm, d), F32), pltpu.VMEM((2, tm, d), F32),
                        pltpu.SemaphoreType.DMA((2,)), pltpu.SemaphoreType.DMA((2,))],
    )
    w1b, w3b, w2b = w1.astype(BF16), w3.astype(BF16), w2.astype(BF16)
    return pl.pallas_call(
        _moe_kernel,
        out_shape=jax.ShapeDtypeStruct((t, d), F32),
        grid_spec=grid_spec,
        compiler_params=_cparams(("arbitrary",)),
        name="moe",
    )(ea, eb, nvalid, src, src, wl, wh, h2d, w1b, w3b, w2b, w1b, w3b, w2b,
      ln_g.reshape(1, d), ln_b.reshape(1, d))


def kernel(x, ln0_g, ln0_b, w_in, pool_w, pool_scale, cmp_k_pe, cmp_k_w1, cmp_k_w2, cmp_v_pe, cmp_v_w1, cmp_v_w2, rel_bias, conv_w, conv_b, dt_bias, a_log, d_skip, ssd_norm_g, w_br_pool, w_br_nsa, w_br_ssd, w_out, ln1_g, ln1_b, router_w, router_b, exp_w1, exp_w3, exp_w2, ln2_g, ln2_b):
    bsz, s, d = x.shape
    assert d == D_MODEL and s % MIX_TS == 0 and s // CMP_STRIDE == LANE and (bsz * s) % PROJ_TM == 0
    t = bsz * s
    ts, ws, bc = _nsa_bias_tables(rel_bias)
    et, ovt = _nsa_consts(s)
    h = _ln(x.reshape(t, d), ln0_g, ln0_b)
    for i in range(DEPTH):
        proj = _proj(h, _prep_w_in(w_in[i]))
        kc, vc = _compress(proj, bsz, s, cmp_k_pe[i], cmp_k_w1[i], cmp_k_w2[i],
                           cmp_v_pe[i], cmp_v_w1[i], cmp_v_w2[i])
        nsa_out = _nsa(proj, kc, vc, ts, ws, bc, et, ovt, bsz, s)
        ssd_y = _ssd(proj, bsz, s, conv_w[i], conv_b[i], dt_bias[i], a_log[i], d_skip[i], ssd_norm_g[i])
        h1 = _mix(proj, nsa_out, ssd_y, h, bsz, s, pool_w[i], pool_scale[i], w_br_pool[i], w_br_nsa[i],
                  w_br_ssd[i], w_out[i], ln1_g[i], ln1_b[i])
        bucket, w_lo, w_hi, rank, cnt = _router(h1, router_w, router_b)
        counts = cnt[:N_BUCKETS, 0].astype(jnp.int32)
        h = _moe(h1, bucket[0], rank[0], counts, w_lo[0], w_hi[0], exp_w1[i], exp_w3[i], exp_w2[i],
                 ln2_g[i], ln2_b[i])
    return h.reshape(bsz, s, d)
```

```python
import functools
import math

import numpy as np
import jax
import jax.numpy as jnp
from jax import lax
from jax.experimental import pallas as pl
from jax.experimental.pallas import tpu as pltpu

F32 = jnp.float32
BF16 = jnp.bfloat16

D_MODEL = 1024
DEPTH = 2
DN_ALPHA = (2.0 * DEPTH) ** 0.25
LN_EPS = 1e-5
NEG = -1e30
BIG = 1e6

POOL_WINDOWS = (2, 4, 8, 16)
POOL_GROUP = 128
POOL_WIDTH = 512
MAX_POOL_WINDOW = 16

NSA_HEAD_DIM = 64
NSA_KV_HEADS = 2
NSA_HPG = 4
NSA_HEADS = 8
NSA_WIDTH = 512
CMP_LEN = 32
CMP_STRIDE = 16
CMP_HIDDEN = 128
SLC_LEN = 64
SLC_TOPN = 8
WIN_LEN = 512
Q_BLOCK = 128
REL_BUCKETS = 32
REL_MAX_DIST = 128

SSD_D_INNER = 1024
SSD_HEAD_DIM = 64
SSD_HEADS = 16
SSD_GROUPS = 4
SSD_STATE = 128
SSD_CONV = 4
SSD_CHUNK = 128
SSD_XBC = 2048

N_EXPERTS = 16
N_EXPERT_GROUPS = 4
EXPERTS_PER_GROUP = 4
D_EXPERT = 512
PAIRS = ((0, 1), (0, 2), (0, 3), (1, 2), (1, 3), (2, 3))
N_BUCKETS = N_EXPERT_GROUPS * len(PAIRS)

LANE = 128
SUBLANE = 8
VMEM_LIMIT = 48 * 1024 * 1024

C_POOL = 0
C_Q = 512
C_KV = 1024
C_NG = 1792
C_DT = 1920
C_XBC = 2048
C_Z = 4096
C_GATE = 5120
D_PAD = 8192

PROJ_TM = 1024
PROJ_TN = 2048
MIX_TS = 256
ROUTER_TM = 1024
MOE_TM = 256


def _dot(a, b):
    return jnp.dot(a, b, preferred_element_type=F32)


def _dot_nt(a, b):
    return lax.dot_general(a, b, (((1,), (1,)), ((), ())), preferred_element_type=F32)


def _dot_tn(a, b):
    return lax.dot_general(a, b, (((0,), (0,)), ((), ())), preferred_element_type=F32)


def _split3(x):
    x1 = x.astype(BF16)
    r1 = x - x1.astype(F32)
    x2 = r1.astype(BF16)
    x3 = (r1 - x2.astype(F32)).astype(BF16)
    return x1, x2, x3


def _layer_norm(x, g, b):
    mu = jnp.mean(x, -1, keepdims=True)
    xc = x - mu
    var = jnp.mean(xc * xc, -1, keepdims=True)
    return xc * lax.rsqrt(var + LN_EPS) * g + b


def _cparams(sem):
    return pltpu.CompilerParams(dimension_semantics=sem, vmem_limit_bytes=VMEM_LIMIT)


def _ln_kernel(x_ref, g_ref, b_ref, o_ref):
    o_ref[...] = _layer_norm(x_ref[...], g_ref[...], b_ref[...])


def _ln(x2d, g, b):
    t, d = x2d.shape
    tm = 512
    return pl.pallas_call(
        _ln_kernel,
        out_shape=jax.ShapeDtypeStruct((t, d), F32),
        grid=(t // tm,),
        in_specs=[pl.BlockSpec((tm, d), lambda i: (i, 0)),
                  pl.BlockSpec((1, d), lambda i: (0, 0)),
                  pl.BlockSpec((1, d), lambda i: (0, 0))],
        out_specs=pl.BlockSpec((tm, d), lambda i: (i, 0)),
        compiler_params=_cparams(("parallel",)),
        name="ln0",
    )(x2d, g.reshape(1, d), b.reshape(1, d))


def _proj_kernel(h_ref, w_ref, o_ref, hb_ref):
    @pl.when(pl.program_id(1) == 0)
    def _():
        hb_ref[...] = h_ref[...].astype(BF16)

    o_ref[...] = _dot(hb_ref[...], w_ref[...])


def _proj(h2d, w_pad, t):
    d = h2d.shape[1]
    return pl.pallas_call(
        _proj_kernel,
        out_shape=jax.ShapeDtypeStruct((t, D_PAD), F32),
        grid=(t // PROJ_TM, D_PAD // PROJ_TN),
        in_specs=[pl.BlockSpec((PROJ_TM, d), lambda i, j: (i, 0)),
                  pl.BlockSpec((d, PROJ_TN), lambda i, j: (0, j))],
        out_specs=pl.BlockSpec((PROJ_TM, PROJ_TN), lambda i, j: (i, j)),
        scratch_shapes=[pltpu.VMEM((PROJ_TM, d), BF16)],
        compiler_params=_cparams(("parallel", "arbitrary")),
        name="proj",
    )(h2d, w_pad)


def _pad_cols(w, n):
    return jnp.pad(w, ((0, 0), (0, n - w.shape[1])))


def _prep_w_in(w_in):
    d = w_in.shape[0]
    o = 0
    w_pool = w_in[:, o:o + POOL_WIDTH]; o += POOL_WIDTH
    w_q = w_in[:, o:o + NSA_WIDTH]; o += NSA_WIDTH
    w_kv = w_in[:, o:o + 768]; o += 768
    w_ng = w_in[:, o:o + 24]; o += 24
    w_z = w_in[:, o:o + SSD_D_INNER]; o += SSD_D_INNER
    w_xbc = w_in[:, o:o + SSD_XBC]; o += SSD_XBC
    w_dt = w_in[:, o:o + SSD_HEADS]; o += SSD_HEADS
    w_gate = w_in[:, o:o + 3 * D_MODEL]
    w_q = w_q.reshape(d, NSA_KV_HEADS, NSA_HPG, NSA_HEAD_DIM).transpose(0, 2, 1, 3).reshape(d, NSA_WIDTH)
    w = jnp.concatenate([w_pool, w_q, w_kv, _pad_cols(w_ng, LANE), _pad_cols(w_dt, LANE),
                         w_xbc, w_z, w_gate], axis=1)
    return w.astype(BF16)


def _compress_one(tok_ref, pe_ref, w1_ref, w2_ref):
    n16 = tok_ref.shape[0] // CMP_STRIDE
    acc_a = jnp.zeros((n16, 2 * CMP_HIDDEN), F32)
    acc_b = jnp.zeros((n16, 2 * CMP_HIDDEN), F32)
    for l in range(CMP_STRIDE):
        rows = tok_ref[pl.ds(l, n16, stride=CMP_STRIDE), :]
        xa = (rows + pe_ref[l:l + 1, :]).astype(BF16)
        xb = (rows + pe_ref[CMP_STRIDE + l:CMP_STRIDE + l + 1, :]).astype(BF16)
        acc_a = acc_a + _dot(xa, w1_ref[l])
        acc_b = acc_b + _dot(xb, w1_ref[CMP_STRIDE + l])
    hid = acc_a + pltpu.roll(acc_b, n16 - 1, axis=0)
    hid = jax.nn.gelu(hid)
    return _dot(hid.astype(BF16), w2_ref[...])


def _compress_kernel(kt_ref, vt_ref, pek_ref, w1k_ref, w2k_ref, pev_ref, w1v_ref, w2v_ref,
                     kc_ref, vc_ref):
    kc_ref[0] = _compress_one(kt_ref, pek_ref, w1k_ref, w2k_ref).astype(BF16)
    vc_ref[0] = _compress_one(vt_ref, pev_ref, w1v_ref, w2v_ref).T.astype(BF16)


def _blockdiag2(w):
    z = jnp.zeros_like(w)
    return jnp.concatenate([jnp.concatenate([w, z], -1), jnp.concatenate([z, w], -1)], -2)


def _compress(proj2d, bsz, s, pe_k, w1_k, w2_k, pe_v, w1_v, w2_v):
    n16 = s // CMP_STRIDE

    def prep(pe, w1, w2):
        pe2 = jnp.concatenate([pe, pe], axis=-1)
        w1b = _blockdiag2(w1.reshape(CMP_LEN, NSA_HEAD_DIM, CMP_HIDDEN)).astype(BF16)
        w2b = _blockdiag2(w2).astype(BF16)
        return pe2, w1b, w2b

    pk = prep(pe_k, w1_k, w2_k)
    pv = prep(pe_v, w1_v, w2_v)
    full = lambda shape: pl.BlockSpec(shape, lambda b: (0,) * len(shape))
    wspecs = [full((CMP_LEN, LANE)), full((CMP_LEN, LANE, 2 * CMP_HIDDEN)), full((2 * CMP_HIDDEN, LANE))]
    return pl.pallas_call(
        _compress_kernel,
        out_shape=(jax.ShapeDtypeStruct((bsz, n16, LANE), BF16),
                   jax.ShapeDtypeStruct((bsz, n16, LANE), BF16)),
        grid=(bsz,),
        in_specs=[pl.BlockSpec((s, LANE), lambda b: (b, C_KV // LANE)),
                  pl.BlockSpec((s, LANE), lambda b: (b, C_KV // LANE + 1))] + wspecs + wspecs,
        out_specs=(pl.BlockSpec((1, n16, LANE), lambda b: (b, 0, 0)),
                   pl.BlockSpec((1, n16, LANE), lambda b: (b, 0, 0))),
        compiler_params=_cparams(("parallel",)),
        name="nsa_compress",
    )(proj2d, proj2d, *pk, *pv)


def _rel_bucket(dist):
    n = jnp.maximum(dist, 0)
    max_exact = REL_BUCKETS // 2
    nf = jnp.maximum(n, 1).astype(F32)
    large = max_exact + (jnp.log(nf / max_exact) / math.log(REL_MAX_DIST / max_exact)
                         * (REL_BUCKETS - max_exact)).astype(jnp.int32)
    large = jnp.minimum(large, REL_BUCKETS - 1)
    return jnp.where(n < max_exact, n, large)


LOG2E = math.log2(math.e)
WIN_BACK = WIN_LEN // Q_BLOCK
WIN_KEYS = (WIN_BACK + 1) * Q_BLOCK
WIN_STRIP_TILES = 2 * WIN_BACK + 1
SEL_TILES = 4
SEL_KEYS = SEL_TILES * Q_BLOCK
SEL_STRIP_TILES = 3 * SEL_TILES - 1
CMP_BIAS_ROWS = 2 * LANE
PV_ROWS = NSA_HEAD_DIM + 16


def _nsa_bias_tables(rel_bias):
    tbl = rel_bias.astype(F32)
    far = tbl[REL_BUCKETS - 1]
    hq = NSA_HPG * Q_BLOCK

    def lookup(dist):
        onehot = (_rel_bucket(jnp.asarray(dist))[..., None] == jnp.arange(REL_BUCKETS)).astype(F32)
        return jnp.einsum("...b,bh->...h", onehot, tbl, precision=lax.Precision.HIGHEST)

    k = np.arange(Q_BLOCK)[None, :, None]
    q = np.arange(Q_BLOCK)[None, None, :]

    def strip(first, last, visible):
        n = first - last + 1
        dist = np.arange(first, last - 1, -1)[:, None, None] * Q_BLOCK + q - k
        v = jnp.where(jnp.asarray(visible(dist))[..., None], (lookup(dist) - far) * LOG2E, NEG)
        v = v.reshape(n, Q_BLOCK, Q_BLOCK, NSA_KV_HEADS, NSA_HPG).transpose(3, 0, 1, 4, 2)
        return v.reshape(NSA_KV_HEADS, n * Q_BLOCK, hq)

    ts = strip(2 * SEL_TILES - 1, 1 - SEL_TILES, lambda d: d >= 0)
    ws = strip(WIN_BACK, -WIN_BACK, lambda d: (d >= 0) & (d < WIN_LEN))
    r = np.arange(CMP_BIAS_ROWS)[:, None]
    dist_c = q[0] - CMP_STRIDE * (r - LANE) - (CMP_LEN - 1)
    bc = jnp.where(jnp.asarray(dist_c >= 0)[..., None], lookup(dist_c) * LOG2E, NEG)
    bc = bc.reshape(CMP_BIAS_ROWS, Q_BLOCK, NSA_KV_HEADS, NSA_HPG).transpose(2, 0, 3, 1)
    bc = bc.reshape(NSA_KV_HEADS, CMP_BIAS_ROWS, hq)
    return ts, ws, bc


def _nsa_consts(s):
    n_slc = s // SLC_LEN
    et = (np.arange(s)[:, None] // SLC_LEN == np.arange(LANE)[None, :]).astype(np.float32)
    n_cmp = (s - CMP_LEN) // CMP_STRIDE + 1
    c0 = np.arange(n_cmp) * CMP_STRIDE
    s0 = np.arange(n_slc) * SLC_LEN
    ov = np.clip(np.minimum(c0[None, :] + CMP_LEN, s0[:, None] + SLC_LEN)
                 - np.maximum(c0[None, :], s0[:, None]), 0, None) / CMP_LEN
    ovt = np.zeros((n_slc, LANE), np.float32)
    ovt[:, :n_cmp] = ov
    return jnp.asarray(et, BF16), jnp.asarray(ovt, BF16)


def _nsa_kernel(q_ref, ng_ref, ks_ref, vs_ref, kw_ref, vw_ref, kc_ref, vct_ref,
                ts_ref, ws_ref, bc_ref, et_ref, ovt_ref, o_ref,
                ksb_ref, kwb_ref, vst_ref, vwt_ref, os_ref, *, nq, n_slc):
    qi = pl.program_id(1)
    hq = NSA_HPG * Q_BLOCK
    dh = NSA_HEAD_DIM
    groups = range(NSA_KV_HEADS)
    gsl = [slice(g * dh, (g + 1) * dh) for g in groups]

    @pl.when(qi == 0)
    def _():
        ksb_ref[:, 0:LANE] = ks_ref[...].astype(BF16)
        ksb_ref[:, LANE:] = et_ref[...]
        kwb_ref[...] = kw_ref[...].astype(BF16)
        ones_rows = jnp.where(lax.broadcasted_iota(jnp.int32, (PV_ROWS - dh, ks_ref.shape[0]), 0) == 0, 1.0, 0.0)
        for g in groups:
            vst_ref[g, dh:, :] = ones_rows.astype(BF16)
            vwt_ref[g, dh:, :] = ones_rows.astype(BF16)
        for kt in range(nq):
            sl = slice(kt * Q_BLOCK, (kt + 1) * Q_BLOCK)
            vs_t = vs_ref[sl, :].T
            vw_t = vw_ref[sl, :].T
            for g in groups:
                vst_ref[g, 0:dh, sl] = vs_t[gsl[g], :].astype(BF16)
                vwt_ref[g, 0:dh, sl] = vw_t[gsl[g], :].astype(BF16)

    row_lo = lax.broadcasted_iota(jnp.int32, (LANE, hq), 0) < dh

    def heads4(m):
        return jnp.concatenate([m] * NSA_HPG, axis=1)

    gates_t = jax.nn.sigmoid(ng_ref[...]).T
    q_t = jnp.concatenate(
        [(q_ref[:, h * LANE:(h + 1) * LANE] * (dh ** -0.5 * LOG2E)).T for h in range(NSA_HPG)], axis=1)
    qg_t = [jnp.where(row_lo if g == 0 else jnp.logical_not(row_lo), q_t, 0.0).astype(BF16) for g in groups]

    boff = pl.multiple_of(LANE - SUBLANE * qi, SUBLANE)
    jrow = lax.broadcasted_iota(jnp.int32, (n_slc, LANE), 0)
    tj = qi * Q_BLOCK + lax.broadcasted_iota(jnp.int32, (n_slc, LANE), 1)
    jt = lax.shift_right_logical(tj, SLC_LEN.bit_length() - 1)
    forced = (jrow == 0) | (jrow == jt) | (jrow == jt - 1)
    valid = jrow * SLC_LEN <= tj
    o_c = []
    q_aug = []
    for g in groups:
        s = _dot(kc_ref[0], qg_t[g]) + bc_ref[g, pl.ds(boff, LANE), :]
        mx = jnp.max(s, axis=0, keepdims=True)
        p = jnp.exp2(s - mx)
        lsum = jnp.sum(p, axis=0, keepdims=True)
        pb = (p * jnp.where(mx > 0.5 * NEG, 1.0 / lsum, 0.0)).astype(BF16)
        o_c.append(_dot(vct_ref[0, gsl[g], :], pb))
        imp4 = _dot(ovt_ref[...], pb)
        imp = imp4[:, 0:LANE]
        for h in range(1, NSA_HPG):
            imp = imp + imp4[:, h * LANE:(h + 1) * LANE]
        val = jnp.where(forced, BIG, jnp.where(valid, imp, -BIG))
        rank = jnp.zeros((n_slc, LANE), F32)
        for i in range(n_slc):
            ci = val[i:i + 1, :]
            rank = rank + jnp.where(ci > val, 1.0, 0.0) + jnp.where(ci == val, (jrow > i).astype(F32), 0.0)
        block_rows = heads4(jnp.where(rank < SLC_TOPN, 0.0, NEG)).astype(BF16)
        q_aug.append(jnp.concatenate(
            [qg_t[g], block_rows, jnp.zeros((LANE - n_slc, hq), BF16)], axis=0))

    n_super = qi // SEL_TILES + 1
    diag = qi - SEL_TILES * (n_super - 1)

    def selected(n):
        nk = n * SEL_KEYS
        lo = max(n - 2, 0) * SEL_KEYS
        toff = pl.multiple_of((2 * SEL_TILES - 1 - diag) * Q_BLOCK - (nk - lo - SEL_KEYS), Q_BLOCK)
        for g in groups:
            parts = [(lo, nk, _dot(ksb_ref[lo:nk, :], q_aug[g]) + ts_ref[g, pl.ds(toff, nk - lo), :])]
            if lo > 0:
                parts.append((0, lo, _dot(ksb_ref[0:lo, :], q_aug[g])))
            m = parts[0][2].max(axis=0, keepdims=True)
            for _, _, s in parts[1:]:
                m = jnp.maximum(m, s.max(axis=0, keepdims=True))
            acc = None
            for a, b, s in parts:
                pv = _dot(vst_ref[g, :, a:b], jnp.exp2(s - m).astype(BF16))
                acc = pv if acc is None else acc + pv
            os_ref[g] = acc[0:dh] * (1.0 / acc[dh:dh + 1])

    for n in range(1, nq // SEL_TILES + 1):
        pl.when(n_super == n)(functools.partial(selected, n))
    o_s = [os_ref[g] for g in groups]

    w0 = jnp.maximum(qi - WIN_BACK, 0)
    woff = pl.multiple_of(w0 * Q_BLOCK, Q_BLOCK)
    boff_w = pl.multiple_of((WIN_BACK - (qi - w0)) * Q_BLOCK, Q_BLOCK)
    kw_bf = kwb_ref[pl.ds(woff, WIN_KEYS), :]
    o_w = []
    for g in groups:
        s = _dot(kw_bf, qg_t[g]) + ws_ref[g, pl.ds(boff_w, WIN_KEYS), :]
        p = jnp.exp2(s - jnp.max(s, axis=0, keepdims=True)).astype(BF16)
        acc = _dot(vwt_ref[g, :, pl.ds(woff, WIN_KEYS)], p)
        o_w.append(acc[0:dh] * (1.0 / acc[dh:dh + 1]))

    for h in range(NSA_HPG):
        hl = slice(h * LANE, (h + 1) * LANE)
        halves = []
        for g in groups:
            c = g * NSA_HPG + h
            halves.append(gates_t[c:c + 1, :] * o_c[g][:, hl]
                          + gates_t[NSA_HEADS + c:NSA_HEADS + c + 1, :] * o_s[g][:, hl]
                          + gates_t[2 * NSA_HEADS + c:2 * NSA_HEADS + c + 1, :] * o_w[g][:, hl])
        o_ref[:, hl] = jnp.concatenate(halves, axis=0).T


def _nsa(proj2d, kc, vct, ts, ws, bc, et, ovt, bsz, s):
    nq = s // Q_BLOCK
    n_slc = s // SLC_LEN
    hq = NSA_HPG * Q_BLOCK
    kvb = C_KV // LANE
    kv_spec = lambda blk: pl.BlockSpec((s, LANE), lambda b, qi: (b, kvb + blk))
    full = lambda shape: pl.BlockSpec(shape, lambda b, qi: (0,) * len(shape), pipeline_mode=pl.Buffered(1))
    return pl.pallas_call(
        functools.partial(_nsa_kernel, nq=nq, n_slc=n_slc),
        out_shape=jax.ShapeDtypeStruct((bsz * s, NSA_WIDTH), F32),
        grid=(bsz, nq),
        in_specs=[pl.BlockSpec((Q_BLOCK, NSA_WIDTH), lambda b, qi: (b * nq + qi, C_Q // NSA_WIDTH)),
                  pl.BlockSpec((Q_BLOCK, LANE), lambda b, qi: (b * nq + qi, C_NG // LANE)),
                  kv_spec(2), kv_spec(3), kv_spec(4), kv_spec(5),
                  pl.BlockSpec((1, s // CMP_STRIDE, LANE), lambda b, qi: (b, 0, 0)),
                  pl.BlockSpec((1, LANE, s // CMP_STRIDE), lambda b, qi: (b, 0, 0)),
                  full((NSA_KV_HEADS, SEL_STRIP_TILES * Q_BLOCK, hq)),
                  full((NSA_KV_HEADS, WIN_STRIP_TILES * Q_BLOCK, hq)),
                  full((NSA_KV_HEADS, CMP_BIAS_ROWS, hq)),
                  full((s, LANE)),
                  full((n_slc, LANE))],
        out_specs=pl.BlockSpec((Q_BLOCK, NSA_WIDTH), lambda b, qi: (b * nq + qi, 0)),
        scratch_shapes=[pltpu.VMEM((s, 2 * LANE), BF16), pltpu.VMEM((s, LANE), BF16),
                        pltpu.VMEM((NSA_KV_HEADS, PV_ROWS, s), BF16), pltpu.VMEM((NSA_KV_HEADS, PV_ROWS, s), BF16),
                        pltpu.VMEM((NSA_KV_HEADS, NSA_HEAD_DIM, hq), F32)],
        compiler_params=_cparams(("parallel", "arbitrary")),
        name="nsa_attn",
    )(proj2d, proj2d, proj2d, proj2d, proj2d, proj2d, kc, vct, ts, ws, bc, et, ovt)


def _ssd_kernel(xbc_ref, prev_ref, z_ref, dt_ref, cw_ref, cb_ref, dtb_ref, alog_ref, alogx_ref,
                dskip_ref, ng_ref, e16_ref, tril_ref, o_ref, xcat_ref, state_ref):
    c = pl.program_id(1)
    L = SSD_CHUNK
    pair_w = 2 * SSD_HEAD_DIM

    @pl.when(c == 0)
    def _():
        state_ref[...] = jnp.zeros(state_ref.shape, F32)

    xcat_ref[0:SUBLANE, :] = jnp.where(c > 0, prev_ref[...], 0.0)
    xcat_ref[SUBLANE:, :] = xbc_ref[...]
    conv = cb_ref[...]
    for k in range(SSD_CONV):
        conv = conv + cw_ref[k:k + 1, :] * xcat_ref[pl.ds(SUBLANE - (SSD_CONV - 1) + k, L), :]
    xbc = jax.nn.silu(conv)
    xs = xbc[:, :SSD_D_INNER]
    bmat = xbc[:, SSD_D_INNER:SSD_D_INNER + SSD_GROUPS * SSD_STATE]
    cmat = xbc[:, SSD_D_INNER + SSD_GROUPS * SSD_STATE:]

    dt = jax.nn.softplus(dt_ref[...] + dtb_ref[...])
    tril = tril_ref[...]
    e16 = e16_ref[...]
    d1, d2, d3 = _split3(dt)
    dt_x = _dot(d1, e16) + _dot(d2, e16) + _dot(d3, e16)
    a_c = dt * (-jnp.exp(alog_ref[...]))
    a_x = dt_x * (-jnp.exp(alogx_ref[...]))
    c1, c2, c3 = _split3(a_c)
    acs = _dot(tril, c1) + _dot(tril, c2) + _dot(tril, c3)
    x1, x2, x3 = _split3(a_x)
    acs_x = _dot(tril, x1) + _dot(tril, x2) + _dot(tril, x3)
    acs_t = acs.T
    last = acs_x[L - 1:L, :]
    eacs = jnp.exp(acs_x)
    decay_in = jnp.exp(last - acs_x)
    cdec = jnp.exp(last)
    xdt = xs * dt_x
    wst = xdt * decay_in

    row = lax.broadcasted_iota(jnp.int32, (L, L), 0)
    lane = lax.broadcasted_iota(jnp.int32, (L, L), 1)
    tri = row >= lane
    lo = lane < SSD_HEAD_DIM

    ys = []
    for g in range(SSD_GROUPS):
        cg = cmat[:, g * SSD_STATE:(g + 1) * SSD_STATE].astype(BF16)
        bg = bmat[:, g * SSD_STATE:(g + 1) * SSD_STATE].astype(BF16)
        cb = _dot_nt(cg, bg)
        for pr in range(2):
            pp = g * 2 + pr
            ls = slice(pp * pair_w, (pp + 1) * pair_w)
            xp = xdt[:, ls]
            yd = None
            for e in range(2):
                hd = 2 * pp + e
                diff = acs[:, hd:hd + 1] - acs_t[hd:hd + 1, :]
                seg = jnp.exp(jnp.where(tri, diff, NEG))
                pm = (cb * seg).astype(BF16)
                xe = jnp.where(lo if e == 0 else jnp.logical_not(lo), xp, 0.0).astype(BF16)
                term = _dot(pm, xe)
                yd = term if yd is None else yd + term
            st = state_ref[pp]
            yo = _dot(cg, st.astype(BF16)) * eacs[:, ls]
            ys.append(yd + yo)
            state_ref[pp] = cdec[:, ls] * st + _dot_tn(bg, wst[:, ls].astype(BF16))

    y = jnp.concatenate(ys, axis=1) + xs * dskip_ref[...]
    y = y * jax.nn.silu(z_ref[...])
    gw = SSD_D_INNER // SSD_GROUPS
    outs = []
    for gi in range(SSD_GROUPS):
        yg = y[:, gi * gw:(gi + 1) * gw]
        outs.append(yg * lax.rsqrt(jnp.mean(yg * yg, -1, keepdims=True) + LN_EPS))
    o_ref[...] = jnp.concatenate(outs, axis=1) * ng_ref[...]


def _ssd(proj2d, bsz, s, conv_w, conv_b, dt_bias, a_log, d_skip, norm_g):
    nc = s // SSD_CHUNK
    L = SSD_CHUNK
    pad16 = lambda v: jnp.pad(v.reshape(1, SSD_HEADS), ((0, 0), (0, LANE - SSD_HEADS)))
    rep64 = lambda v: jnp.repeat(v, SSD_HEAD_DIM).reshape(1, SSD_D_INNER)
    e16 = np.zeros((LANE, SSD_D_INNER), np.float32)
    e16[np.arange(SSD_D_INNER) // SSD_HEAD_DIM, np.arange(SSD_D_INNER)] = 1.0
    tril = np.tril(np.ones((L, L), np.float32))
    full = lambda shape: pl.BlockSpec(shape, lambda b, c: (0,) * len(shape))
    rows8 = s // SUBLANE
    return pl.pallas_call(
        _ssd_kernel,
        out_shape=jax.ShapeDtypeStruct((bsz * s, SSD_D_INNER), F32),
        grid=(bsz, nc),
        in_specs=[pl.BlockSpec((L, SSD_XBC), lambda b, c: (b * nc + c, C_XBC // SSD_XBC)),
                  pl.BlockSpec((SUBLANE, SSD_XBC),
                               lambda b, c: (jnp.maximum(b * rows8 + c * (L // SUBLANE) - 1, 0), C_XBC // SSD_XBC)),
                  pl.BlockSpec((L, SSD_D_INNER), lambda b, c: (b * nc + c, C_Z // SSD_D_INNER)),
                  pl.BlockSpec((L, LANE), lambda b, c: (b * nc + c, C_DT // LANE)),
                  full((SSD_CONV, SSD_XBC)), full((1, SSD_XBC)), full((1, LANE)), full((1, LANE)),
                  full((1, SSD_D_INNER)), full((1, SSD_D_INNER)), full((1, SSD_D_INNER)),
                  full((LANE, SSD_D_INNER)), full((L, L))],
        out_specs=pl.BlockSpec((L, SSD_D_INNER), lambda b, c: (b * nc + c, 0)),
        scratch_shapes=[pltpu.VMEM((L + SUBLANE, SSD_XBC), F32),
                        pltpu.VMEM((SSD_HEADS // 2, SSD_STATE, 2 * SSD_HEAD_DIM), F32)],
        compiler_params=_cparams(("parallel", "arbitrary")),
        name="ssd",
    )(proj2d, proj2d, proj2d, proj2d, conv_w, conv_b.reshape(1, SSD_XBC), pad16(dt_bias), pad16(a_log),
      rep64(a_log), rep64(d_skip), norm_g.reshape(1, SSD_D_INNER),
      jnp.asarray(e16, BF16), jnp.asarray(tril, BF16))


def _mix_kernel(up_ref, upprev_ref, nsa_ref, ssd_ref, ga_ref, gb_ref, gc_ref, h_ref,
                pw_ref, ps_ref, wbp_ref, wbn_ref, wbs_ref, wo_ref, lg_ref, lb_ref, o_ref, *, ts):
    si = pl.program_id(1)
    w0 = MAX_POOL_WINDOW
    prev = jnp.where(si > 0, upprev_ref[...], 0.0)
    xcat = jnp.concatenate([prev, up_ref[...]], axis=0)
    tpos = si * ts + lax.broadcasted_iota(jnp.int32, (ts, POOL_GROUP), 0)
    ygs = []
    for gi, w in enumerate(POOL_WINDOWS):
        x = xcat[:, gi * POOL_GROUP:(gi + 1) * POOL_GROUP]
        acc = x
        d = 1
        while d < w:
            acc = acc + pltpu.roll(acc, d, axis=0)
            d *= 2
        cnt = jnp.minimum(tpos + 1, w).astype(F32)
        r = acc[w0:, :] / cnt - x[w0:, :]
        ygs.append(_dot(r.astype(BF16), pw_ref[gi]))
    y = jnp.concatenate(ygs, axis=1) * ps_ref[...]
    br_a = _dot(y.astype(BF16), wbp_ref[...])
    br_b = _dot(nsa_ref[...].astype(BF16), wbn_ref[...])
    br_c = _dot(ssd_ref[...].astype(BF16), wbs_ref[...])
    mixed = (jax.nn.sigmoid(ga_ref[...]) * br_a + jax.nn.sigmoid(gb_ref[...]) * br_b
             + jax.nn.sigmoid(gc_ref[...]) * br_c)
    mix = _dot(mixed.astype(BF16), wo_ref[...])
    o_ref[...] = _layer_norm(DN_ALPHA * h_ref[...] + mix, lg_ref[...], lb_ref[...])


def _mix(proj2d, nsa_out, ssd_y, h2d, bsz, s, pool_w, pool_scale, w_br_pool, w_br_nsa, w_br_ssd,
         w_out, ln_g, ln_b):
    ts = MIX_TS
    ns = s // ts
    d = D_MODEL
    w0 = MAX_POOL_WINDOW
    wbn = w_br_nsa.reshape(NSA_KV_HEADS, NSA_HPG, NSA_HEAD_DIM, d).transpose(1, 0, 2, 3).reshape(NSA_WIDTH, d)
    full = lambda shape: pl.BlockSpec(shape, lambda b, si: (0,) * len(shape))
    row = lambda width, cb: pl.BlockSpec((ts, width), lambda b, si: (b * ns + si, cb))
    return pl.pallas_call(
        functools.partial(_mix_kernel, ts=ts),
        out_shape=jax.ShapeDtypeStruct((bsz * s, d), F32),
        grid=(bsz, ns),
        in_specs=[row(POOL_WIDTH, C_POOL // POOL_WIDTH),
                  pl.BlockSpec((w0, POOL_WIDTH),
                               lambda b, si: (jnp.maximum((b * s + si * ts) // w0 - 1, 0), C_POOL // POOL_WIDTH)),
                  row(NSA_WIDTH, 0), row(d, 0),
                  row(d, C_GATE // d), row(d, C_GATE // d + 1), row(d, C_GATE // d + 2),
                  row(d, 0),
                  full((len(POOL_WINDOWS), POOL_GROUP, POOL_GROUP)), full((1, POOL_WIDTH)),
                  full((POOL_WIDTH, d)), full((NSA_WIDTH, d)), full((d, d)), full((d, d)),
                  full((1, d)), full((1, d))],
        out_specs=pl.BlockSpec((ts, d), lambda b, si: (b * ns + si, 0)),
        compiler_params=_cparams(("parallel", "arbitrary")),
        name="mix",
    )(proj2d, proj2d, nsa_out, ssd_y, proj2d, proj2d, proj2d, h2d,
      pool_w.astype(BF16), pool_scale.reshape(1, POOL_WIDTH), w_br_pool.astype(BF16), wbn.astype(BF16),
      w_br_ssd.astype(BF16), w_out.astype(BF16), ln_g.reshape(1, d), ln_b.reshape(1, d))


RANK_CHUNK = 256
BUCKET_ROWS = 32


def _router_kernel(h_ref, rw_ref, rb_ref, tri_ref, bucket_ref, wlo_ref, whi_ref, rank_ref, cnt_ref, run_ref):
    @pl.when(pl.program_id(0) == 0)
    def _():
        run_ref[...] = jnp.zeros(run_ref.shape, F32)

    logits = _dot_nt(rw_ref[...], h_ref[...].astype(BF16))
    lg = [logits[e:e + 1, :] for e in range(N_EXPERTS)]
    mx = lg[0]
    for e in range(1, N_EXPERTS):
        mx = jnp.maximum(mx, lg[e])
    ex = [jnp.exp(v - mx) for v in lg]
    den = ex[0]
    for e in range(1, N_EXPERTS):
        den = den + ex[e]
    probs = [v / den for v in ex]
    score = [probs[e] + rb_ref[e:e + 1, :] for e in range(N_EXPERTS)]

    def group_vals(vals, grp):
        out = []
        for k in range(EXPERTS_PER_GROUP):
            v = vals[k]
            for gi in range(1, N_EXPERT_GROUPS):
                v = jnp.where(grp == gi, vals[gi * EXPERTS_PER_GROUP + k], v)
            out.append(v)
        return out

    gscore = []
    for gi in range(N_EXPERT_GROUPS):
        sc = score[gi * EXPERTS_PER_GROUP:(gi + 1) * EXPERTS_PER_GROUP]
        best = None
        for a, b in PAIRS:
            v = sc[a] + sc[b]
            best = v if best is None else jnp.maximum(best, v)
        gscore.append(best)
    grp = jnp.zeros(gscore[0].shape, jnp.int32)
    best = gscore[0]
    for gi in range(1, N_EXPERT_GROUPS):
        better = gscore[gi] > best
        grp = jnp.where(better, gi, grp)
        best = jnp.where(better, gscore[gi], best)

    sc = group_vals(score, grp)
    pr = group_vals(probs, grp)
    first = jnp.zeros(grp.shape, jnp.int32)
    fv = sc[0]
    for k in range(1, EXPERTS_PER_GROUP):
        better = sc[k] > fv
        first = jnp.where(better, k, first)
        fv = jnp.where(better, sc[k], fv)
    second = jnp.full(grp.shape, -1, jnp.int32)
    sv = jnp.full(fv.shape, -jnp.inf, F32)
    for k in range(EXPERTS_PER_GROUP):
        better = (first != k) & ((sc[k] > sv) | (second < 0))
        second = jnp.where(better, k, second)
        sv = jnp.where(better, sc[k], sv)
    lo = jnp.minimum(first, second)
    hi = jnp.maximum(first, second)
    p_lo = pr[0]
    p_hi = pr[0]
    for k in range(1, EXPERTS_PER_GROUP):
        p_lo = jnp.where(lo == k, pr[k], p_lo)
        p_hi = jnp.where(hi == k, pr[k], p_hi)
    tot = p_lo + p_hi
    pair = jnp.zeros(grp.shape, jnp.int32)
    for pi, (a, b) in enumerate(PAIRS):
        pair = jnp.where((lo == a) & (hi == b), pi, pair)
    bucket = grp * len(PAIRS) + pair
    bucket_ref[...] = bucket
    wlo_ref[...] = p_lo / tot
    whi_ref[...] = p_hi / tot

    tm = bucket.shape[1]
    onehot = jnp.where(lax.broadcasted_iota(jnp.int32, (BUCKET_ROWS, tm), 0) == bucket, 1.0, 0.0)
    run = run_ref[...]
    ranks = []
    for c in range(tm // RANK_CHUNK):
        oc = onehot[:, c * RANK_CHUNK:(c + 1) * RANK_CHUNK]
        before = _dot(oc.astype(BF16), tri_ref[...]) + jnp.concatenate([run] * (RANK_CHUNK // LANE), axis=1)
        ranks.append(jnp.sum(oc * before, axis=0, keepdims=True))
        run = run + jnp.broadcast_to(jnp.sum(oc, axis=1, keepdims=True), run.shape)
    rank_ref[...] = jnp.concatenate(ranks, axis=1).astype(jnp.int32)
    run_ref[...] = run
    cnt_ref[...] = run


def _router(h2d, router_w, router_b):
    t, d = h2d.shape
    tm = ROUTER_TM
    rw = jnp.pad(router_w.T, ((0, LANE - N_EXPERTS), (0, 0))).astype(BF16)
    rb = jnp.broadcast_to(jnp.pad(router_b, (0, LANE - N_EXPERTS))[:, None], (LANE, tm))
    tri = jnp.asarray(np.triu(np.ones((RANK_CHUNK, RANK_CHUNK), np.float32), 1), BF16)
    vec = lambda dt: jax.ShapeDtypeStruct((1, t), dt)
    row = pl.BlockSpec((1, tm), lambda i: (0, i))
    return pl.pallas_call(
        _router_kernel,
        out_shape=(vec(jnp.int32), vec(F32), vec(F32), vec(jnp.int32),
                   jax.ShapeDtypeStruct((BUCKET_ROWS, LANE), F32)),
        grid=(t // tm,),
        in_specs=[pl.BlockSpec((tm, d), lambda i: (i, 0)),
                  pl.BlockSpec((LANE, d), lambda i: (0, 0)),
                  pl.BlockSpec((LANE, tm), lambda i: (0, 0)),
                  pl.BlockSpec((RANK_CHUNK, RANK_CHUNK), lambda i: (0, 0))],
        out_specs=(row, row, row, row, pl.BlockSpec((BUCKET_ROWS, LANE), lambda i: (0, 0))),
        scratch_shapes=[pltpu.VMEM((BUCKET_ROWS, LANE), F32)],
        compiler_params=_cparams(("arbitrary",)),
        name="router",
    )(h2d, rw, rb, tri)


def _moe_plan(bucket, rank, counts, w_lo, w_hi, t):
    tm = MOE_TM
    n_tiles = t // tm + N_BUCKETS
    p_rows = n_tiles * tm
    tiles_per = (counts + tm - 1) // tm
    tile_end = jnp.cumsum(tiles_per)
    tile_start = tile_end - tiles_per
    in_bucket = bucket[:, None] == jnp.arange(N_BUCKETS)[None, :]
    dest = jnp.sum(jnp.where(in_bucket, (tile_start * tm)[None, :], 0), axis=1) + rank
    payload = jnp.stack([jnp.arange(t, dtype=jnp.int32), lax.bitcast_convert_type(w_lo, jnp.int32),
                         lax.bitcast_convert_type(w_hi, jnp.int32), jnp.ones((t,), jnp.int32)], axis=1)
    plan = jnp.zeros((p_rows, 4), jnp.int32).at[dest].set(payload, unique_indices=True)
    src = plan[:, 0]
    wl = lax.bitcast_convert_type(plan[:, 1], F32)
    wh = lax.bitcast_convert_type(plan[:, 2], F32)
    prow = jnp.arange(p_rows, dtype=jnp.int32)
    spare = t + ((prow // tm) % 2) * tm + prow % tm
    dst = jnp.where(plan[:, 3] > 0, src, spare)
    n_used = tile_end[-1]
    tile_ids = jnp.arange(n_tiles)
    tb = jnp.sum((tile_end[None, :] <= tile_ids[:, None]).astype(jnp.int32), axis=1)
    tb = jnp.where(tile_ids < n_used, tb, tb[jnp.maximum(n_used - 1, 0)])
    tb = jnp.minimum(tb, N_BUCKETS - 1)
    pairs = np.asarray(PAIRS, np.int32)
    grp = tb // len(PAIRS)
    ea = grp * EXPERTS_PER_GROUP + jnp.asarray(pairs[:, 0])[tb % len(PAIRS)]
    eb = grp * EXPERTS_PER_GROUP + jnp.asarray(pairs[:, 1])[tb % len(PAIRS)]
    nvalid = jnp.clip(counts[tb] - (tile_ids - tile_start[tb]) * tm, 0, tm)
    nvalid = jnp.where(tile_ids < n_used, nvalid, 0).astype(jnp.int32)
    return (src.reshape(n_tiles, 1, tm), dst.reshape(n_tiles, 1, tm), wl.reshape(p_rows, 1), wh.reshape(p_rows, 1),
            ea.astype(jnp.int32), eb.astype(jnp.int32), nvalid, n_tiles)


MOE_DMA_GROUPS = 8


def _moe_kernel(ea_ref, eb_ref, nv_ref, src0_ref, srcn_ref, dstp_ref, wl_ref, wh_ref, h_hbm,
                w1a_ref, w3a_ref, w2a_ref, w1b_ref, w3b_ref, w2b_ref, lg_ref, lb_ref,
                o_hbm, xbuf0, xbuf1, obuf0, obuf1, gsem, ssem):
    i = pl.program_id(0)
    tm = xbuf0.shape[0]
    xbufs = (xbuf0, xbuf1)
    obufs = (obuf0, obuf1)
    nv = nv_ref[i]
    nv_p1 = nv_ref[jnp.maximum(i - 1, 0)]
    nv_p2 = nv_ref[jnp.maximum(i - 2, 0)]
    odd = lax.rem(i, 2) == 1
    even = jnp.logical_not(odd)

    def gather(r, p, row):
        return pltpu.make_async_copy(h_hbm.at[pl.ds(row, 1)], xbufs[p].at[pl.ds(r, 1)], gsem.at[p])

    def scatter(r, p, row):
        return pltpu.make_async_copy(obufs[p].at[pl.ds(r, 1)], o_hbm.at[pl.ds(row, 1)], ssem.at[p])

    def wait_gather(p):
        for r in range(tm):
            gather(r, p, 0).wait()

    def wait_scatter(p):
        for r in range(tm):
            scatter(r, p, 0).wait()

    def tile(p, first):
        q = 1 - p
        per = tm // MOE_DMA_GROUPS

        def issue(k):
            for r in range(k * per, (k + 1) * per):
                gather(r, q, srcn_ref[0, 0, r]).start()
                if not first:
                    scatter(r, q, dstp_ref[0, 0, r]).start()

        wait_gather(p)
        x = xbufs[p][...]
        xb = x.astype(BF16)
        issue(0)
        h1 = _dot(xb, w1a_ref[0])
        issue(1)
        h3 = _dot(xb, w3a_ref[0])
        issue(2)
        ya = _dot((jax.nn.silu(h1) * h3).astype(BF16), w2a_ref[0])
        issue(3)
        h1 = _dot(xb, w1b_ref[0])
        issue(4)
        h3 = _dot(xb, w3b_ref[0])
        issue(5)
        yb = _dot((jax.nn.silu(h1) * h3).astype(BF16), w2b_ref[0])
        issue(6)
        y = wl_ref[...] * ya + wh_ref[...] * yb
        issue(7)
        obufs[p][...] = _layer_norm(DN_ALPHA * x + y, lg_ref[...], lb_ref[...])

    @pl.when(i == 0)
    def _():
        obuf1[...] = jnp.zeros(obuf1.shape, F32)
        n_real = o_hbm.shape[0] - 2 * tm
        for blk in range(2):
            cp = pltpu.make_async_copy(obuf1, o_hbm.at[pl.ds(n_real + blk * tm, tm)], ssem.at[1])
            cp.start()
            cp.wait()

    started_p2 = (i >= 2) & (nv_p2 > 0) & (nv_p1 > 0)
    pl.when(started_p2 & even)(functools.partial(wait_scatter, 0))
    pl.when(started_p2 & odd)(functools.partial(wait_scatter, 1))

    @pl.when((nv > 0) & (i == 0))
    def _():
        def start0(r, carry):
            gather(r, 0, src0_ref[0, 0, r]).start()
            return carry
        lax.fori_loop(0, tm, start0, 0)
        tile(0, True)

    pl.when((nv > 0) & (i > 0) & even)(functools.partial(tile, 0, False))
    pl.when((nv > 0) & odd)(functools.partial(tile, 1, False))

    def tail(p):
        q = 1 - p
        wait_gather(p)

        def start(r, carry):
            scatter(r, q, dstp_ref[0, 0, r]).start()
            return carry
        lax.fori_loop(0, tm, start, 0)
        wait_scatter(q)

    is_tail = (nv == 0) & (i >= 1) & (nv_p1 > 0)
    pl.when(is_tail & even)(functools.partial(tail, 0))
    pl.when(is_tail & odd)(functools.partial(tail, 1))


def _moe(h2d, bucket, rank, counts, w_lo, w_hi, w1, w3, w2, ln_g, ln_b):
    t, d = h2d.shape
    tm = MOE_TM
    assert tm % MOE_DMA_GROUPS == 0
    src, dst, wl, wh, ea, eb, nvalid, n_tiles = _moe_plan(bucket, rank, counts, w_lo, w_hi, t)
    wspec = lambda shape, which: pl.BlockSpec(
        shape, (lambda i, ea_r, eb_r, nv_r: (ea_r[i], 0, 0)) if which == 0
        else (lambda i, ea_r, eb_r, nv_r: (eb_r[i], 0, 0)))
    w13 = (1, d, D_EXPERT)
    w2s = (1, D_EXPERT, d)
    smem_tile = lambda fn: pl.BlockSpec((1, 1, tm), lambda i, *_: (fn(i), 0, 0), memory_space=pltpu.SMEM)
    grid_spec = pltpu.PrefetchScalarGridSpec(
        num_scalar_prefetch=3,
        grid=(n_tiles,),
        in_specs=[smem_tile(lambda i: 0),
                  smem_tile(lambda i: jnp.minimum(i + 1, n_tiles - 1)),
                  smem_tile(lambda i: jnp.maximum(i - 1, 0)),
                  pl.BlockSpec((tm, 1), lambda i, *_: (i, 0)),
                  pl.BlockSpec((tm, 1), lambda i, *_: (i, 0)),
                  pl.BlockSpec(memory_space=pl.ANY),
                  wspec(w13, 0), wspec(w13, 0), wspec(w2s, 0),
                  wspec(w13, 1), wspec(w13, 1), wspec(w2s, 1),
                  pl.BlockSpec((1, d), lambda i, *_: (0, 0)),
                  pl.BlockSpec((1, d), lambda i, *_: (0, 0))],
        out_specs=pl.BlockSpec(memory_space=pl.ANY),
        scratch_shapes=[pltpu.VMEM((tm, d), F32), pltpu.VMEM((tm, d), F32),
                        pltpu.VMEM((tm, d), F32), pltpu.VMEM((tm, d), F32),
                        pltpu.SemaphoreType.DMA((2,)), pltpu.SemaphoreType.DMA((2,))],
    )
    w1b, w3b, w2b = w1.astype(BF16), w3.astype(BF16), w2.astype(BF16)
    return pl.pallas_call(
        _moe_kernel,
        out_shape=jax.ShapeDtypeStruct((t + 2 * tm, d), F32),
        grid_spec=grid_spec,
        compiler_params=_cparams(("arbitrary",)),
        name="moe",
    )(ea, eb, nvalid, src, src, dst, wl, wh, h2d, w1b, w3b, w2b, w1b, w3b, w2b,
      ln_g.reshape(1, d), ln_b.reshape(1, d))


def kernel(x, ln0_g, ln0_b, w_in, pool_w, pool_scale, cmp_k_pe, cmp_k_w1, cmp_k_w2, cmp_v_pe, cmp_v_w1, cmp_v_w2, rel_bias, conv_w, conv_b, dt_bias, a_log, d_skip, ssd_norm_g, w_br_pool, w_br_nsa, w_br_ssd, w_out, ln1_g, ln1_b, router_w, router_b, exp_w1, exp_w3, exp_w2, ln2_g, ln2_b):
    bsz, s, d = x.shape
    assert d == D_MODEL and s % MIX_TS == 0 and s // CMP_STRIDE == LANE and (bsz * s) % PROJ_TM == 0
    t = bsz * s
    ts, ws, bc = _nsa_bias_tables(rel_bias)
    et, ovt = _nsa_consts(s)
    h = _ln(x.reshape(t, d), ln0_g, ln0_b)
    for i in range(DEPTH):
        proj = _proj(h, _prep_w_in(w_in[i]), t)
        kc, vc = _compress(proj, bsz, s, cmp_k_pe[i], cmp_k_w1[i], cmp_k_w2[i],
                           cmp_v_pe[i], cmp_v_w1[i], cmp_v_w2[i])
        nsa_out = _nsa(proj, kc, vc, ts, ws, bc, et, ovt, bsz, s)
        ssd_y = _ssd(proj, bsz, s, conv_w[i], conv_b[i], dt_bias[i], a_log[i], d_skip[i], ssd_norm_g[i])
        h1 = _mix(proj, nsa_out, ssd_y, h, bsz, s, pool_w[i], pool_scale[i], w_br_pool[i], w_br_nsa[i],
                  w_br_ssd[i], w_out[i], ln1_g[i], ln1_b[i])
        bucket, w_lo, w_hi, rank, cnt = _router(h1, router_w, router_b)
        counts = cnt[:N_BUCKETS, 0].astype(jnp.int32)
        h = _moe(h1, bucket[0], rank[0], counts, w_lo[0], w_hi[0], exp_w1[i], exp_w3[i], exp_w2[i],
                 ln2_g[i], ln2_b[i])
    return h[:t].reshape(bsz, s, d)
```

```python
import functools
import math

import numpy as np
import jax
import jax.numpy as jnp
from jax import lax
from jax.experimental import pallas as pl
from jax.experimental.pallas import tpu as pltpu

F32 = jnp.float32
BF16 = jnp.bfloat16

D_MODEL = 1024
DEPTH = 2
DN_ALPHA = (2.0 * DEPTH) ** 0.25
LN_EPS = 1e-5
NEG = -1e30
BIG = 1e6

POOL_WINDOWS = (2, 4, 8, 16)
POOL_GROUP = 128
POOL_WIDTH = 512
MAX_POOL_WINDOW = 16

NSA_HEAD_DIM = 64
NSA_KV_HEADS = 2
NSA_HPG = 4
NSA_HEADS = 8
NSA_WIDTH = 512
CMP_LEN = 32
CMP_STRIDE = 16
CMP_HIDDEN = 128
SLC_LEN = 64
SLC_TOPN = 8
WIN_LEN = 512
Q_BLOCK = 128
REL_BUCKETS = 32
REL_MAX_DIST = 128

SSD_D_INNER = 1024
SSD_HEAD_DIM = 64
SSD_HEADS = 16
SSD_GROUPS = 4
SSD_STATE = 128
SSD_CONV = 4
SSD_CHUNK = 128
SSD_XBC = 2048

N_EXPERTS = 16
N_EXPERT_GROUPS = 4
EXPERTS_PER_GROUP = 4
D_EXPERT = 512
PAIRS = ((0, 1), (0, 2), (0, 3), (1, 2), (1, 3), (2, 3))
N_BUCKETS = N_EXPERT_GROUPS * len(PAIRS)

LANE = 128
SUBLANE = 8
VMEM_LIMIT = 48 * 1024 * 1024

C_POOL = 0
C_Q = 512
C_KV = 1024
C_NG = 1792
C_DT = 1920
C_XBC = 2048
C_Z = 4096
C_GATE = 5120
D_PAD = 8192

PROJ_TM = 1024
PROJ_TN = 2048
MIX_TS = 256
ROUTER_TM = 1024
MOE_TM = 256


def _dot(a, b):
    return jnp.dot(a, b, preferred_element_type=F32)


def _dot_nt(a, b):
    return lax.dot_general(a, b, (((1,), (1,)), ((), ())), preferred_element_type=F32)


def _dot_tn(a, b):
    return lax.dot_general(a, b, (((0,), (0,)), ((), ())), preferred_element_type=F32)


def _split3(x):
    x1 = x.astype(BF16)
    r1 = x - x1.astype(F32)
    x2 = r1.astype(BF16)
    x3 = (r1 - x2.astype(F32)).astype(BF16)
    return x1, x2, x3


def _layer_norm(x, g, b):
    mu = jnp.mean(x, -1, keepdims=True)
    xc = x - mu
    var = jnp.mean(xc * xc, -1, keepdims=True)
    return xc * lax.rsqrt(var + LN_EPS) * g + b


def _cparams(sem):
    return pltpu.CompilerParams(dimension_semantics=sem, vmem_limit_bytes=VMEM_LIMIT)


def _ln_kernel(x_ref, g_ref, b_ref, o_ref):
    o_ref[...] = _layer_norm(x_ref[...], g_ref[...], b_ref[...])


def _ln(x2d, g, b):
    t, d = x2d.shape
    tm = 512
    return pl.pallas_call(
        _ln_kernel,
        out_shape=jax.ShapeDtypeStruct((t, d), F32),
        grid=(t // tm,),
        in_specs=[pl.BlockSpec((tm, d), lambda i: (i, 0)),
                  pl.BlockSpec((1, d), lambda i: (0, 0)),
                  pl.BlockSpec((1, d), lambda i: (0, 0))],
        out_specs=pl.BlockSpec((tm, d), lambda i: (i, 0)),
        compiler_params=_cparams(("parallel",)),
        name="ln0",
    )(x2d, g.reshape(1, d), b.reshape(1, d))


def _proj_kernel(h_ref, w_ref, o_ref, hb_ref):
    @pl.when(pl.program_id(1) == 0)
    def _():
        hb_ref[...] = h_ref[...].astype(BF16)

    o_ref[...] = _dot(hb_ref[...], w_ref[...])


def _proj(h2d, w_pad, t):
    d = h2d.shape[1]
    return pl.pallas_call(
        _proj_kernel,
        out_shape=jax.ShapeDtypeStruct((t, D_PAD), F32),
        grid=(t // PROJ_TM, D_PAD // PROJ_TN),
        in_specs=[pl.BlockSpec((PROJ_TM, d), lambda i, j: (i, 0)),
                  pl.BlockSpec((d, PROJ_TN), lambda i, j: (0, j))],
        out_specs=pl.BlockSpec((PROJ_TM, PROJ_TN), lambda i, j: (i, j)),
        scratch_shapes=[pltpu.VMEM((PROJ_TM, d), BF16)],
        compiler_params=_cparams(("parallel", "arbitrary")),
        name="proj",
    )(h2d, w_pad)


def _pad_cols(w, n):
    return jnp.pad(w, ((0, 0), (0, n - w.shape[1])))


def _prep_w_in(w_in):
    d = w_in.shape[0]
    o = 0
    w_pool = w_in[:, o:o + POOL_WIDTH]; o += POOL_WIDTH
    w_q = w_in[:, o:o + NSA_WIDTH]; o += NSA_WIDTH
    w_kv = w_in[:, o:o + 768]; o += 768
    w_ng = w_in[:, o:o + 24]; o += 24
    w_z = w_in[:, o:o + SSD_D_INNER]; o += SSD_D_INNER
    w_xbc = w_in[:, o:o + SSD_XBC]; o += SSD_XBC
    w_dt = w_in[:, o:o + SSD_HEADS]; o += SSD_HEADS
    w_gate = w_in[:, o:o + 3 * D_MODEL]
    w_q = w_q.reshape(d, NSA_KV_HEADS, NSA_HPG, NSA_HEAD_DIM).transpose(0, 2, 1, 3).reshape(d, NSA_WIDTH)
    w = jnp.concatenate([w_pool, w_q, w_kv, _pad_cols(w_ng, LANE), _pad_cols(w_dt, LANE),
                         w_xbc, w_z, w_gate], axis=1)
    return w.astype(BF16)


def _compress_one(tok_ref, pe_ref, w1_ref, w2_ref):
    n16 = tok_ref.shape[0] // CMP_STRIDE
    acc_a = jnp.zeros((n16, 2 * CMP_HIDDEN), F32)
    acc_b = jnp.zeros((n16, 2 * CMP_HIDDEN), F32)
    for l in range(CMP_STRIDE):
        rows = tok_ref[pl.ds(l, n16, stride=CMP_STRIDE), :]
        xa = (rows + pe_ref[l:l + 1, :]).astype(BF16)
        xb = (rows + pe_ref[CMP_STRIDE + l:CMP_STRIDE + l + 1, :]).astype(BF16)
        acc_a = acc_a + _dot(xa, w1_ref[l])
        acc_b = acc_b + _dot(xb, w1_ref[CMP_STRIDE + l])
    hid = acc_a + pltpu.roll(acc_b, n16 - 1, axis=0)
    hid = jax.nn.gelu(hid)
    return _dot(hid.astype(BF16), w2_ref[...])


def _compress_kernel(kt_ref, vt_ref, pek_ref, w1k_ref, w2k_ref, pev_ref, w1v_ref, w2v_ref,
                     kc_ref, vc_ref):
    kc_ref[0] = _compress_one(kt_ref, pek_ref, w1k_ref, w2k_ref).astype(BF16)
    vc_ref[0] = _compress_one(vt_ref, pev_ref, w1v_ref, w2v_ref).T.astype(BF16)


def _blockdiag2(w):
    z = jnp.zeros_like(w)
    return jnp.concatenate([jnp.concatenate([w, z], -1), jnp.concatenate([z, w], -1)], -2)


def _compress(proj2d, bsz, s, pe_k, w1_k, w2_k, pe_v, w1_v, w2_v):
    n16 = s // CMP_STRIDE

    def prep(pe, w1, w2):
        pe2 = jnp.concatenate([pe, pe], axis=-1)
        w1b = _blockdiag2(w1.reshape(CMP_LEN, NSA_HEAD_DIM, CMP_HIDDEN)).astype(BF16)
        w2b = _blockdiag2(w2).astype(BF16)
        return pe2, w1b, w2b

    pk = prep(pe_k, w1_k, w2_k)
    pv = prep(pe_v, w1_v, w2_v)
    full = lambda shape: pl.BlockSpec(shape, lambda b: (0,) * len(shape))
    wspecs = [full((CMP_LEN, LANE)), full((CMP_LEN, LANE, 2 * CMP_HIDDEN)), full((2 * CMP_HIDDEN, LANE))]
    return pl.pallas_call(
        _compress_kernel,
        out_shape=(jax.ShapeDtypeStruct((bsz, n16, LANE), BF16),
                   jax.ShapeDtypeStruct((bsz, n16, LANE), BF16)),
        grid=(bsz,),
        in_specs=[pl.BlockSpec((s, LANE), lambda b: (b, C_KV // LANE)),
                  pl.BlockSpec((s, LANE), lambda b: (b, C_KV // LANE + 1))] + wspecs + wspecs,
        out_specs=(pl.BlockSpec((1, n16, LANE), lambda b: (b, 0, 0)),
                   pl.BlockSpec((1, n16, LANE), lambda b: (b, 0, 0))),
        compiler_params=_cparams(("parallel",)),
        name="nsa_compress",
    )(proj2d, proj2d, *pk, *pv)


def _rel_bucket(dist):
    n = jnp.maximum(dist, 0)
    max_exact = REL_BUCKETS // 2
    nf = jnp.maximum(n, 1).astype(F32)
    large = max_exact + (jnp.log(nf / max_exact) / math.log(REL_MAX_DIST / max_exact)
                         * (REL_BUCKETS - max_exact)).astype(jnp.int32)
    large = jnp.minimum(large, REL_BUCKETS - 1)
    return jnp.where(n < max_exact, n, large)


LOG2E = math.log2(math.e)
WIN_BACK = WIN_LEN // Q_BLOCK
WIN_KEYS = (WIN_BACK + 1) * Q_BLOCK
WIN_STRIP_TILES = 2 * WIN_BACK + 1
SEL_TILES = 4
SEL_KEYS = SEL_TILES * Q_BLOCK
SEL_STRIP_TILES = 3 * SEL_TILES - 1
CMP_BIAS_ROWS = 2 * LANE
PV_ROWS = NSA_HEAD_DIM + 16


def _nsa_bias_tables(rel_bias):
    tbl = rel_bias.astype(F32)
    far = tbl[REL_BUCKETS - 1]
    hq = NSA_HPG * Q_BLOCK

    def lookup(dist):
        onehot = (_rel_bucket(jnp.asarray(dist))[..., None] == jnp.arange(REL_BUCKETS)).astype(F32)
        return jnp.einsum("...b,bh->...h", onehot, tbl, precision=lax.Precision.HIGHEST)

    k = np.arange(Q_BLOCK)[None, :, None]
    q = np.arange(Q_BLOCK)[None, None, :]

    def strip(first, last, visible):
        n = first - last + 1
        dist = np.arange(first, last - 1, -1)[:, None, None] * Q_BLOCK + q - k
        v = jnp.where(jnp.asarray(visible(dist))[..., None], (lookup(dist) - far) * LOG2E, NEG)
        v = v.reshape(n, Q_BLOCK, Q_BLOCK, NSA_KV_HEADS, NSA_HPG).transpose(3, 0, 1, 4, 2)
        return v.reshape(NSA_KV_HEADS, n * Q_BLOCK, hq)

    ts = strip(2 * SEL_TILES - 1, 1 - SEL_TILES, lambda d: d >= 0)
    ws = strip(WIN_BACK, -WIN_BACK, lambda d: (d >= 0) & (d < WIN_LEN))
    r = np.arange(CMP_BIAS_ROWS)[:, None]
    dist_c = q[0] - CMP_STRIDE * (r - LANE) - (CMP_LEN - 1)
    bc = jnp.where(jnp.asarray(dist_c >= 0)[..., None], lookup(dist_c) * LOG2E, NEG)
    bc = bc.reshape(CMP_BIAS_ROWS, Q_BLOCK, NSA_KV_HEADS, NSA_HPG).transpose(2, 0, 3, 1)
    bc = bc.reshape(NSA_KV_HEADS, CMP_BIAS_ROWS, hq)
    return ts, ws, bc


def _nsa_consts(s):
    n_slc = s // SLC_LEN
    et = (np.arange(s)[:, None] // SLC_LEN == np.arange(LANE)[None, :]).astype(np.float32)
    n_cmp = (s - CMP_LEN) // CMP_STRIDE + 1
    c0 = np.arange(n_cmp) * CMP_STRIDE
    s0 = np.arange(n_slc) * SLC_LEN
    ov = np.clip(np.minimum(c0[None, :] + CMP_LEN, s0[:, None] + SLC_LEN)
                 - np.maximum(c0[None, :], s0[:, None]), 0, None) / CMP_LEN
    ovt = np.zeros((n_slc, LANE), np.float32)
    ovt[:, :n_cmp] = ov
    return jnp.asarray(et, BF16), jnp.asarray(ovt, BF16)


def _nsa_kernel(q_ref, ng_ref, ks_ref, vs_ref, kw_ref, vw_ref, kc_ref, vct_ref,
                ts_ref, ws_ref, bc_ref, et_ref, ovt_ref, o_ref,
                ksb_ref, kwb_ref, vst_ref, vwt_ref, os_ref, *, nq, n_slc):
    qi = pl.program_id(1)
    hq = NSA_HPG * Q_BLOCK
    dh = NSA_HEAD_DIM
    groups = range(NSA_KV_HEADS)
    gsl = [slice(g * dh, (g + 1) * dh) for g in groups]

    @pl.when(qi == 0)
    def _():
        ksb_ref[:, 0:LANE] = ks_ref[...].astype(BF16)
        ksb_ref[:, LANE:] = et_ref[...]
        kwb_ref[...] = kw_ref[...].astype(BF16)
        ones_rows = jnp.where(lax.broadcasted_iota(jnp.int32, (PV_ROWS - dh, ks_ref.shape[0]), 0) == 0, 1.0, 0.0)
        for g in groups:
            vst_ref[g, dh:, :] = ones_rows.astype(BF16)
            vwt_ref[g, dh:, :] = ones_rows.astype(BF16)
        for kt in range(nq):
            sl = slice(kt * Q_BLOCK, (kt + 1) * Q_BLOCK)
            vs_t = vs_ref[sl, :].T
            vw_t = vw_ref[sl, :].T
            for g in groups:
                vst_ref[g, 0:dh, sl] = vs_t[gsl[g], :].astype(BF16)
                vwt_ref[g, 0:dh, sl] = vw_t[gsl[g], :].astype(BF16)

    row_lo = lax.broadcasted_iota(jnp.int32, (LANE, hq), 0) < dh

    def heads4(m):
        return jnp.concatenate([m] * NSA_HPG, axis=1)

    gates_t = jax.nn.sigmoid(ng_ref[...]).T
    q_t = jnp.concatenate(
        [(q_ref[:, h * LANE:(h + 1) * LANE] * (dh ** -0.5 * LOG2E)).T for h in range(NSA_HPG)], axis=1)
    qg_t = [jnp.where(row_lo if g == 0 else jnp.logical_not(row_lo), q_t, 0.0).astype(BF16) for g in groups]

    boff = pl.multiple_of(LANE - SUBLANE * qi, SUBLANE)
    jrow = lax.broadcasted_iota(jnp.int32, (n_slc, LANE), 0)
    tj = qi * Q_BLOCK + lax.broadcasted_iota(jnp.int32, (n_slc, LANE), 1)
    jt = lax.shift_right_logical(tj, SLC_LEN.bit_length() - 1)
    forced = (jrow == 0) | (jrow == jt) | (jrow == jt - 1)
    valid = jrow * SLC_LEN <= tj
    o_c = []
    q_aug = []
    for g in groups:
        s = _dot(kc_ref[0], qg_t[g]) + bc_ref[g, pl.ds(boff, LANE), :]
        mx = jnp.max(s, axis=0, keepdims=True)
        p = jnp.exp2(s - mx)
        lsum = jnp.sum(p, axis=0, keepdims=True)
        pb = (p * jnp.where(mx > 0.5 * NEG, 1.0 / lsum, 0.0)).astype(BF16)
        o_c.append(_dot(vct_ref[0, gsl[g], :], pb))
        imp4 = _dot(ovt_ref[...], pb)
        imp = imp4[:, 0:LANE]
        for h in range(1, NSA_HPG):
            imp = imp + imp4[:, h * LANE:(h + 1) * LANE]
        val = jnp.where(forced, BIG, jnp.where(valid, imp, -BIG))
        rank = jnp.zeros((n_slc, LANE), F32)
        for i in range(n_slc):
            ci = val[i:i + 1, :]
            rank = rank + jnp.where(ci > val, 1.0, 0.0) + jnp.where(ci == val, (jrow > i).astype(F32), 0.0)
        block_rows = heads4(jnp.where(rank < SLC_TOPN, 0.0, NEG)).astype(BF16)
        q_aug.append(jnp.concatenate(
            [qg_t[g], block_rows, jnp.zeros((LANE - n_slc, hq), BF16)], axis=0))

    n_super = qi // SEL_TILES + 1
    diag = qi - SEL_TILES * (n_super - 1)

    def selected(n):
        nk = n * SEL_KEYS
        lo = max(n - 2, 0) * SEL_KEYS
        toff = pl.multiple_of((2 * SEL_TILES - 1 - diag) * Q_BLOCK - (nk - lo - SEL_KEYS), Q_BLOCK)
        for g in groups:
            parts = [(lo, nk, _dot(ksb_ref[lo:nk, :], q_aug[g]) + ts_ref[g, pl.ds(toff, nk - lo), :])]
            if lo > 0:
                parts.append((0, lo, _dot(ksb_ref[0:lo, :], q_aug[g])))
            m = parts[0][2].max(axis=0, keepdims=True)
            for _, _, s in parts[1:]:
                m = jnp.maximum(m, s.max(axis=0, keepdims=True))
            acc = None
            for a, b, s in parts:
                pv = _dot(vst_ref[g, :, a:b], jnp.exp2(s - m).astype(BF16))
                acc = pv if acc is None else acc + pv
            os_ref[g] = acc[0:dh] * (1.0 / acc[dh:dh + 1])

    for n in range(1, nq // SEL_TILES + 1):
        pl.when(n_super == n)(functools.partial(selected, n))
    o_s = [os_ref[g] for g in groups]

    w0 = jnp.maximum(qi - WIN_BACK, 0)
    woff = pl.multiple_of(w0 * Q_BLOCK, Q_BLOCK)
    boff_w = pl.multiple_of((WIN_BACK - (qi - w0)) * Q_BLOCK, Q_BLOCK)
    kw_bf = kwb_ref[pl.ds(woff, WIN_KEYS), :]
    o_w = []
    for g in groups:
        s = _dot(kw_bf, qg_t[g]) + ws_ref[g, pl.ds(boff_w, WIN_KEYS), :]
        p = jnp.exp2(s - jnp.max(s, axis=0, keepdims=True)).astype(BF16)
        acc = _dot(vwt_ref[g, :, pl.ds(woff, WIN_KEYS)], p)
        o_w.append(acc[0:dh] * (1.0 / acc[dh:dh + 1]))

    for h in range(NSA_HPG):
        hl = slice(h * LANE, (h + 1) * LANE)
        halves = []
        for g in groups:
            c = g * NSA_HPG + h
            halves.append(gates_t[c:c + 1, :] * o_c[g][:, hl]
                          + gates_t[NSA_HEADS + c:NSA_HEADS + c + 1, :] * o_s[g][:, hl]
                          + gates_t[2 * NSA_HEADS + c:2 * NSA_HEADS + c + 1, :] * o_w[g][:, hl])
        o_ref[:, hl] = jnp.concatenate(halves, axis=0).T


def _nsa(proj2d, kc, vct, ts, ws, bc, et, ovt, bsz, s):
    nq = s // Q_BLOCK
    n_slc = s // SLC_LEN
    hq = NSA_HPG * Q_BLOCK
    kvb = C_KV // LANE
    kv_spec = lambda blk: pl.BlockSpec((s, LANE), lambda b, qi: (b, kvb + blk))
    full = lambda shape: pl.BlockSpec(shape, lambda b, qi: (0,) * len(shape), pipeline_mode=pl.Buffered(1))
    return pl.pallas_call(
        functools.partial(_nsa_kernel, nq=nq, n_slc=n_slc),
        out_shape=jax.ShapeDtypeStruct((bsz * s, NSA_WIDTH), F32),
        grid=(bsz, nq),
        in_specs=[pl.BlockSpec((Q_BLOCK, NSA_WIDTH), lambda b, qi: (b * nq + qi, C_Q // NSA_WIDTH)),
                  pl.BlockSpec((Q_BLOCK, LANE), lambda b, qi: (b * nq + qi, C_NG // LANE)),
                  kv_spec(2), kv_spec(3), kv_spec(4), kv_spec(5),
                  pl.BlockSpec((1, s // CMP_STRIDE, LANE), lambda b, qi: (b, 0, 0)),
                  pl.BlockSpec((1, LANE, s // CMP_STRIDE), lambda b, qi: (b, 0, 0)),
                  full((NSA_KV_HEADS, SEL_STRIP_TILES * Q_BLOCK, hq)),
                  full((NSA_KV_HEADS, WIN_STRIP_TILES * Q_BLOCK, hq)),
                  full((NSA_KV_HEADS, CMP_BIAS_ROWS, hq)),
                  full((s, LANE)),
                  full((n_slc, LANE))],
        out_specs=pl.BlockSpec((Q_BLOCK, NSA_WIDTH), lambda b, qi: (b * nq + qi, 0)),
        scratch_shapes=[pltpu.VMEM((s, 2 * LANE), BF16), pltpu.VMEM((s, LANE), BF16),
                        pltpu.VMEM((NSA_KV_HEADS, PV_ROWS, s), BF16), pltpu.VMEM((NSA_KV_HEADS, PV_ROWS, s), BF16),
                        pltpu.VMEM((NSA_KV_HEADS, NSA_HEAD_DIM, hq), F32)],
        compiler_params=_cparams(("parallel", "arbitrary")),
        name="nsa_attn",
    )(proj2d, proj2d, proj2d, proj2d, proj2d, proj2d, kc, vct, ts, ws, bc, et, ovt)


def _ssd_kernel(xbc_ref, prev_ref, z_ref, dt_ref, cw_ref, cb_ref, dtb_ref, alog_ref, alogx_ref,
                dskip_ref, ng_ref, e16_ref, tril_ref, o_ref, xcat_ref, state_ref):
    c = pl.program_id(1)
    L = SSD_CHUNK
    pair_w = 2 * SSD_HEAD_DIM

    @pl.when(c == 0)
    def _():
        state_ref[...] = jnp.zeros(state_ref.shape, F32)

    xcat_ref[0:SUBLANE, :] = jnp.where(c > 0, prev_ref[...], 0.0)
    xcat_ref[SUBLANE:, :] = xbc_ref[...]
    conv = cb_ref[...]
    for k in range(SSD_CONV):
        conv = conv + cw_ref[k:k + 1, :] * xcat_ref[pl.ds(SUBLANE - (SSD_CONV - 1) + k, L), :]
    xbc = jax.nn.silu(conv)
    xs = xbc[:, :SSD_D_INNER]
    bmat = xbc[:, SSD_D_INNER:SSD_D_INNER + SSD_GROUPS * SSD_STATE]
    cmat = xbc[:, SSD_D_INNER + SSD_GROUPS * SSD_STATE:]

    dt = jax.nn.softplus(dt_ref[...] + dtb_ref[...])
    tril = tril_ref[...]
    e16 = e16_ref[...]
    d1, d2, d3 = _split3(dt)
    dt_x = _dot(d1, e16) + _dot(d2, e16) + _dot(d3, e16)
    a_c = dt * (-jnp.exp(alog_ref[...]))
    a_x = dt_x * (-jnp.exp(alogx_ref[...]))
    c1, c2, c3 = _split3(a_c)
    acs = _dot(tril, c1) + _dot(tril, c2) + _dot(tril, c3)
    x1, x2, x3 = _split3(a_x)
    acs_x = _dot(tril, x1) + _dot(tril, x2) + _dot(tril, x3)
    acs_t = acs.T
    last = acs_x[L - 1:L, :]
    eacs = jnp.exp(acs_x)
    decay_in = jnp.exp(last - acs_x)
    cdec = jnp.exp(last)
    xdt = xs * dt_x
    wst = xdt * decay_in

    row = lax.broadcasted_iota(jnp.int32, (L, L), 0)
    lane = lax.broadcasted_iota(jnp.int32, (L, L), 1)
    tri = row >= lane
    lo = lane < SSD_HEAD_DIM

    ys = []
    for g in range(SSD_GROUPS):
        cg = cmat[:, g * SSD_STATE:(g + 1) * SSD_STATE].astype(BF16)
        bg = bmat[:, g * SSD_STATE:(g + 1) * SSD_STATE].astype(BF16)
        cb = _dot_nt(cg, bg)
        for pr in range(2):
            pp = g * 2 + pr
            ls = slice(pp * pair_w, (pp + 1) * pair_w)
            xp = xdt[:, ls]
            yd = None
            for e in range(2):
                hd = 2 * pp + e
                diff = acs[:, hd:hd + 1] - acs_t[hd:hd + 1, :]
                seg = jnp.exp(jnp.where(tri, diff, NEG))
                pm = (cb * seg).astype(BF16)
                xe = jnp.where(lo if e == 0 else jnp.logical_not(lo), xp, 0.0).astype(BF16)
                term = _dot(pm, xe)
                yd = term if yd is None else yd + term
            st = state_ref[pp]
            yo = _dot(cg, st.astype(BF16)) * eacs[:, ls]
            ys.append(yd + yo)
            state_ref[pp] = cdec[:, ls] * st + _dot_tn(bg, wst[:, ls].astype(BF16))

    y = jnp.concatenate(ys, axis=1) + xs * dskip_ref[...]
    y = y * jax.nn.silu(z_ref[...])
    gw = SSD_D_INNER // SSD_GROUPS
    outs = []
    for gi in range(SSD_GROUPS):
        yg = y[:, gi * gw:(gi + 1) * gw]
        outs.append(yg * lax.rsqrt(jnp.mean(yg * yg, -1, keepdims=True) + LN_EPS))
    o_ref[...] = jnp.concatenate(outs, axis=1) * ng_ref[...]


def _ssd(proj2d, bsz, s, conv_w, conv_b, dt_bias, a_log, d_skip, norm_g):
    nc = s // SSD_CHUNK
    L = SSD_CHUNK
    pad16 = lambda v: jnp.pad(v.reshape(1, SSD_HEADS), ((0, 0), (0, LANE - SSD_HEADS)))
    rep64 = lambda v: jnp.repeat(v, SSD_HEAD_DIM).reshape(1, SSD_D_INNER)
    e16 = np.zeros((LANE, SSD_D_INNER), np.float32)
    e16[np.arange(SSD_D_INNER) // SSD_HEAD_DIM, np.arange(SSD_D_INNER)] = 1.0
    tril = np.tril(np.ones((L, L), np.float32))
    full = lambda shape: pl.BlockSpec(shape, lambda b, c: (0,) * len(shape))
    rows8 = s // SUBLANE
    return pl.pallas_call(
        _ssd_kernel,
        out_shape=jax.ShapeDtypeStruct((bsz * s, SSD_D_INNER), F32),
        grid=(bsz, nc),
        in_specs=[pl.BlockSpec((L, SSD_XBC), lambda b, c: (b * nc + c, C_XBC // SSD_XBC)),
                  pl.BlockSpec((SUBLANE, SSD_XBC),
                               lambda b, c: (jnp.maximum(b * rows8 + c * (L // SUBLANE) - 1, 0), C_XBC // SSD_XBC)),
                  pl.BlockSpec((L, SSD_D_INNER), lambda b, c: (b * nc + c, C_Z // SSD_D_INNER)),
                  pl.BlockSpec((L, LANE), lambda b, c: (b * nc + c, C_DT // LANE)),
                  full((SSD_CONV, SSD_XBC)), full((1, SSD_XBC)), full((1, LANE)), full((1, LANE)),
                  full((1, SSD_D_INNER)), full((1, SSD_D_INNER)), full((1, SSD_D_INNER)),
                  full((LANE, SSD_D_INNER)), full((L, L))],
        out_specs=pl.BlockSpec((L, SSD_D_INNER), lambda b, c: (b * nc + c, 0)),
        scratch_shapes=[pltpu.VMEM((L + SUBLANE, SSD_XBC), F32),
                        pltpu.VMEM((SSD_HEADS // 2, SSD_STATE, 2 * SSD_HEAD_DIM), F32)],
        compiler_params=_cparams(("parallel", "arbitrary")),
        name="ssd",
    )(proj2d, proj2d, proj2d, proj2d, conv_w, conv_b.reshape(1, SSD_XBC), pad16(dt_bias), pad16(a_log),
      rep64(a_log), rep64(d_skip), norm_g.reshape(1, SSD_D_INNER),
      jnp.asarray(e16, BF16), jnp.asarray(tril, BF16))


def _mix_kernel(up_ref, upprev_ref, nsa_ref, ssd_ref, ga_ref, gb_ref, gc_ref, h_ref,
                pw_ref, ps_ref, wbp_ref, wbn_ref, wbs_ref, wo_ref, lg_ref, lb_ref, o_ref, *, ts):
    si = pl.program_id(1)
    w0 = MAX_POOL_WINDOW
    prev = jnp.where(si > 0, upprev_ref[...], 0.0)
    xcat = jnp.concatenate([prev, up_ref[...]], axis=0)
    tpos = si * ts + lax.broadcasted_iota(jnp.int32, (ts, POOL_GROUP), 0)
    ygs = []
    for gi, w in enumerate(POOL_WINDOWS):
        x = xcat[:, gi * POOL_GROUP:(gi + 1) * POOL_GROUP]
        acc = x
        d = 1
        while d < w:
            acc = acc + pltpu.roll(acc, d, axis=0)
            d *= 2
        cnt = jnp.minimum(tpos + 1, w).astype(F32)
        r = acc[w0:, :] / cnt - x[w0:, :]
        ygs.append(_dot(r.astype(BF16), pw_ref[gi]))
    y = jnp.concatenate(ygs, axis=1) * ps_ref[...]
    br_a = _dot(y.astype(BF16), wbp_ref[...])
    br_b = _dot(nsa_ref[...].astype(BF16), wbn_ref[...])
    br_c = _dot(ssd_ref[...].astype(BF16), wbs_ref[...])
    mixed = (jax.nn.sigmoid(ga_ref[...]) * br_a + jax.nn.sigmoid(gb_ref[...]) * br_b
             + jax.nn.sigmoid(gc_ref[...]) * br_c)
    mix = _dot(mixed.astype(BF16), wo_ref[...])
    o_ref[...] = _layer_norm(DN_ALPHA * h_ref[...] + mix, lg_ref[...], lb_ref[...])


def _mix(proj2d, nsa_out, ssd_y, h2d, bsz, s, pool_w, pool_scale, w_br_pool, w_br_nsa, w_br_ssd,
         w_out, ln_g, ln_b):
    ts = MIX_TS
    ns = s // ts
    d = D_MODEL
    w0 = MAX_POOL_WINDOW
    wbn = w_br_nsa.reshape(NSA_KV_HEADS, NSA_HPG, NSA_HEAD_DIM, d).transpose(1, 0, 2, 3).reshape(NSA_WIDTH, d)
    full = lambda shape: pl.BlockSpec(shape, lambda b, si: (0,) * len(shape))
    row = lambda width, cb: pl.BlockSpec((ts, width), lambda b, si: (b * ns + si, cb))
    return pl.pallas_call(
        functools.partial(_mix_kernel, ts=ts),
        out_shape=jax.ShapeDtypeStruct((bsz * s, d), F32),
        grid=(bsz, ns),
        in_specs=[row(POOL_WIDTH, C_POOL // POOL_WIDTH),
                  pl.BlockSpec((w0, POOL_WIDTH),
                               lambda b, si: (jnp.maximum((b * s + si * ts) // w0 - 1, 0), C_POOL // POOL_WIDTH)),
                  row(NSA_WIDTH, 0), row(d, 0),
                  row(d, C_GATE // d), row(d, C_GATE // d + 1), row(d, C_GATE // d + 2),
                  row(d, 0),
                  full((len(POOL_WINDOWS), POOL_GROUP, POOL_GROUP)), full((1, POOL_WIDTH)),
                  full((POOL_WIDTH, d)), full((NSA_WIDTH, d)), full((d, d)), full((d, d)),
                  full((1, d)), full((1, d))],
        out_specs=pl.BlockSpec((ts, d), lambda b, si: (b * ns + si, 0)),
        compiler_params=_cparams(("parallel", "arbitrary")),
        name="mix",
    )(proj2d, proj2d, nsa_out, ssd_y, proj2d, proj2d, proj2d, h2d,
      pool_w.astype(BF16), pool_scale.reshape(1, POOL_WIDTH), w_br_pool.astype(BF16), wbn.astype(BF16),
      w_br_ssd.astype(BF16), w_out.astype(BF16), ln_g.reshape(1, d), ln_b.reshape(1, d))


RANK_CHUNK = 256
BUCKET_ROWS = 32


def _router_kernel(h_ref, rw_ref, rb_ref, tri_ref, bucket_ref, wlo_ref, whi_ref, rank_ref, cnt_ref, run_ref):
    @pl.when(pl.program_id(0) == 0)
    def _():
        run_ref[...] = jnp.zeros(run_ref.shape, F32)

    logits = _dot_nt(rw_ref[...], h_ref[...].astype(BF16))
    lg = [logits[e:e + 1, :] for e in range(N_EXPERTS)]
    mx = lg[0]
    for e in range(1, N_EXPERTS):
        mx = jnp.maximum(mx, lg[e])
    ex = [jnp.exp(v - mx) for v in lg]
    den = ex[0]
    for e in range(1, N_EXPERTS):
        den = den + ex[e]
    probs = [v / den for v in ex]
    score = [probs[e] + rb_ref[e:e + 1, :] for e in range(N_EXPERTS)]

    def group_vals(vals, grp):
        out = []
        for k in range(EXPERTS_PER_GROUP):
            v = vals[k]
            for gi in range(1, N_EXPERT_GROUPS):
                v = jnp.where(grp == gi, vals[gi * EXPERTS_PER_GROUP + k], v)
            out.append(v)
        return out

    gscore = []
    for gi in range(N_EXPERT_GROUPS):
        sc = score[gi * EXPERTS_PER_GROUP:(gi + 1) * EXPERTS_PER_GROUP]
        best = None
        for a, b in PAIRS:
            v = sc[a] + sc[b]
            best = v if best is None else jnp.maximum(best, v)
        gscore.append(best)
    grp = jnp.zeros(gscore[0].shape, jnp.int32)
    best = gscore[0]
    for gi in range(1, N_EXPERT_GROUPS):
        better = gscore[gi] > best
        grp = jnp.where(better, gi, grp)
        best = jnp.where(better, gscore[gi], best)

    sc = group_vals(score, grp)
    pr = group_vals(probs, grp)
    first = jnp.zeros(grp.shape, jnp.int32)
    fv = sc[0]
    for k in range(1, EXPERTS_PER_GROUP):
        better = sc[k] > fv
        first = jnp.where(better, k, first)
        fv = jnp.where(better, sc[k], fv)
    second = jnp.full(grp.shape, -1, jnp.int32)
    sv = jnp.full(fv.shape, -jnp.inf, F32)
    for k in range(EXPERTS_PER_GROUP):
        better = (first != k) & ((sc[k] > sv) | (second < 0))
        second = jnp.where(better, k, second)
        sv = jnp.where(better, sc[k], sv)
    lo = jnp.minimum(first, second)
    hi = jnp.maximum(first, second)
    p_lo = pr[0]
    p_hi = pr[0]
    for k in range(1, EXPERTS_PER_GROUP):
        p_lo = jnp.where(lo == k, pr[k], p_lo)
        p_hi = jnp.where(hi == k, pr[k], p_hi)
    tot = p_lo + p_hi
    pair = jnp.zeros(grp.shape, jnp.int32)
    for pi, (a, b) in enumerate(PAIRS):
        pair = jnp.where((lo == a) & (hi == b), pi, pair)
    bucket = grp * len(PAIRS) + pair
    bucket_ref[...] = bucket
    wlo_ref[...] = p_lo / tot
    whi_ref[...] = p_hi / tot

    tm = bucket.shape[1]
    onehot = jnp.where(lax.broadcasted_iota(jnp.int32, (BUCKET_ROWS, tm), 0) == bucket, 1.0, 0.0)
    run = run_ref[...]
    ranks = []
    for c in range(tm // RANK_CHUNK):
        oc = onehot[:, c * RANK_CHUNK:(c + 1) * RANK_CHUNK]
        before = _dot(oc.astype(BF16), tri_ref[...]) + jnp.concatenate([run] * (RANK_CHUNK // LANE), axis=1)
        ranks.append(jnp.sum(oc * before, axis=0, keepdims=True))
        run = run + jnp.broadcast_to(jnp.sum(oc, axis=1, keepdims=True), run.shape)
    rank_ref[...] = jnp.concatenate(ranks, axis=1).astype(jnp.int32)
    run_ref[...] = run
    cnt_ref[...] = run


def _router(h2d, router_w, router_b):
    t, d = h2d.shape
    tm = ROUTER_TM
    rw = jnp.pad(router_w.T, ((0, LANE - N_EXPERTS), (0, 0))).astype(BF16)
    rb = jnp.broadcast_to(jnp.pad(router_b, (0, LANE - N_EXPERTS))[:, None], (LANE, tm))
    tri = jnp.asarray(np.triu(np.ones((RANK_CHUNK, RANK_CHUNK), np.float32), 1), BF16)
    vec = lambda dt: jax.ShapeDtypeStruct((1, t), dt)
    row = pl.BlockSpec((1, tm), lambda i: (0, i))
    return pl.pallas_call(
        _router_kernel,
        out_shape=(vec(jnp.int32), vec(F32), vec(F32), vec(jnp.int32),
                   jax.ShapeDtypeStruct((BUCKET_ROWS, LANE), F32)),
        grid=(t // tm,),
        in_specs=[pl.BlockSpec((tm, d), lambda i: (i, 0)),
                  pl.BlockSpec((LANE, d), lambda i: (0, 0)),
                  pl.BlockSpec((LANE, tm), lambda i: (0, 0)),
                  pl.BlockSpec((RANK_CHUNK, RANK_CHUNK), lambda i: (0, 0))],
        out_specs=(row, row, row, row, pl.BlockSpec((BUCKET_ROWS, LANE), lambda i: (0, 0))),
        scratch_shapes=[pltpu.VMEM((BUCKET_ROWS, LANE), F32)],
        compiler_params=_cparams(("arbitrary",)),
        name="router",
    )(h2d, rw, rb, tri)


def _moe_plan(bucket, rank, counts, w_lo, w_hi, t):
    tm = MOE_TM
    n_tiles = t // tm + N_BUCKETS
    p_rows = n_tiles * tm
    tiles_per = (counts + tm - 1) // tm
    tile_end = jnp.cumsum(tiles_per)
    tile_start = tile_end - tiles_per
    in_bucket = bucket[:, None] == jnp.arange(N_BUCKETS)[None, :]
    dest = jnp.sum(jnp.where(in_bucket, (tile_start * tm)[None, :], 0), axis=1) + rank
    payload = jnp.stack([jnp.arange(t, dtype=jnp.int32), lax.bitcast_convert_type(w_lo, jnp.int32),
                         lax.bitcast_convert_type(w_hi, jnp.int32), jnp.ones((t,), jnp.int32)], axis=1)
    plan = jnp.zeros((p_rows, 4), jnp.int32).at[dest].set(payload, unique_indices=True)
    src = plan[:, 0]
    wl = lax.bitcast_convert_type(plan[:, 1], F32)
    wh = lax.bitcast_convert_type(plan[:, 2], F32)
    prow = jnp.arange(p_rows, dtype=jnp.int32)
    spare = t + ((prow // tm) % 2) * tm + prow % tm
    dst = jnp.where(plan[:, 3] > 0, src, spare)
    n_used = tile_end[-1]
    tile_ids = jnp.arange(n_tiles)
    tb = jnp.sum((tile_end[None, :] <= tile_ids[:, None]).astype(jnp.int32), axis=1)
    tb = jnp.where(tile_ids < n_used, tb, tb[jnp.maximum(n_used - 1, 0)])
    tb = jnp.minimum(tb, N_BUCKETS - 1)
    pairs = np.asarray(PAIRS, np.int32)
    grp = tb // len(PAIRS)
    ea = grp * EXPERTS_PER_GROUP + jnp.asarray(pairs[:, 0])[tb % len(PAIRS)]
    eb = grp * EXPERTS_PER_GROUP + jnp.asarray(pairs[:, 1])[tb % len(PAIRS)]
    nvalid = jnp.clip(counts[tb] - (tile_ids - tile_start[tb]) * tm, 0, tm)
    nvalid = jnp.where(tile_ids < n_used, nvalid, 0).astype(jnp.int32)
    return (src.reshape(n_tiles, 1, tm), dst.reshape(n_tiles, 1, tm), wl.reshape(p_rows, 1), wh.reshape(p_rows, 1),
            ea.astype(jnp.int32), eb.astype(jnp.int32), nvalid, n_tiles)


MOE_DMA_GROUPS = 8


def _moe_kernel(ea_ref, eb_ref, nv_ref, src0_ref, srcn_ref, dstp_ref, wl_ref, wh_ref, h_hbm,
                w1a_ref, w3a_ref, w2a_ref, w1b_ref, w3b_ref, w2b_ref, lg_ref, lb_ref,
                o_hbm, xbuf0, xbuf1, obuf0, obuf1, gsem, ssem):
    i = pl.program_id(0)
    tm = xbuf0.shape[0]
    xbufs = (xbuf0, xbuf1)
    obufs = (obuf0, obuf1)
    nv = nv_ref[i]
    nv_p1 = nv_ref[jnp.maximum(i - 1, 0)]
    nv_p2 = nv_ref[jnp.maximum(i - 2, 0)]
    odd = lax.rem(i, 2) == 1
    even = jnp.logical_not(odd)

    def gather(r, p, row):
        return pltpu.make_async_copy(h_hbm.at[pl.ds(row, 1)], xbufs[p].at[pl.ds(r, 1)], gsem.at[p])

    def scatter(r, p, row):
        return pltpu.make_async_copy(obufs[p].at[pl.ds(r, 1)], o_hbm.at[pl.ds(row, 1)], ssem.at[p])

    def wait_gather(p):
        for r in range(tm):
            gather(r, p, 0).wait()

    def wait_scatter(p):
        for r in range(tm):
            scatter(r, p, 0).wait()

    def tile(p, first):
        q = 1 - p
        per = tm // MOE_DMA_GROUPS

        def issue(k):
            for r in range(k * per, (k + 1) * per):
                gather(r, q, srcn_ref[0, 0, r]).start(priority=r % 2)
                if not first:
                    scatter(r, q, dstp_ref[0, 0, r]).start(priority=r % 2)

        wait_gather(p)
        x = xbufs[p][...]
        xb = x.astype(BF16)
        issue(0)
        h1 = _dot(xb, w1a_ref[0])
        issue(1)
        h3 = _dot(xb, w3a_ref[0])
        issue(2)
        ya = _dot((jax.nn.silu(h1) * h3).astype(BF16), w2a_ref[0])
        issue(3)
        h1 = _dot(xb, w1b_ref[0])
        issue(4)
        h3 = _dot(xb, w3b_ref[0])
        issue(5)
        yb = _dot((jax.nn.silu(h1) * h3).astype(BF16), w2b_ref[0])
        issue(6)
        y = wl_ref[...] * ya + wh_ref[...] * yb
        issue(7)
        obufs[p][...] = _layer_norm(DN_ALPHA * x + y, lg_ref[...], lb_ref[...])

    @pl.when(i == 0)
    def _():
        obuf1[...] = jnp.zeros(obuf1.shape, F32)
        n_real = o_hbm.shape[0] - 2 * tm
        for blk in range(2):
            cp = pltpu.make_async_copy(obuf1, o_hbm.at[pl.ds(n_real + blk * tm, tm)], ssem.at[1])
            cp.start()
            cp.wait()

    started_p2 = (i >= 2) & (nv_p2 > 0) & (nv_p1 > 0)
    pl.when(started_p2 & even)(functools.partial(wait_scatter, 0))
    pl.when(started_p2 & odd)(functools.partial(wait_scatter, 1))

    @pl.when((nv > 0) & (i == 0))
    def _():
        def start0(r, carry):
            gather(r, 0, src0_ref[0, 0, r]).start()
            return carry
        lax.fori_loop(0, tm, start0, 0)
        tile(0, True)

    pl.when((nv > 0) & (i > 0) & even)(functools.partial(tile, 0, False))
    pl.when((nv > 0) & odd)(functools.partial(tile, 1, False))

    def tail(p):
        q = 1 - p
        wait_gather(p)

        def start(r, carry):
            scatter(r, q, dstp_ref[0, 0, r]).start()
            return carry
        lax.fori_loop(0, tm, start, 0)
        wait_scatter(q)

    is_tail = (nv == 0) & (i >= 1) & (nv_p1 > 0)
    pl.when(is_tail & even)(functools.partial(tail, 0))
    pl.when(is_tail & odd)(functools.partial(tail, 1))


def _moe(h2d, bucket, rank, counts, w_lo, w_hi, w1, w3, w2, ln_g, ln_b):
    t, d = h2d.shape
    tm = MOE_TM
    assert tm % MOE_DMA_GROUPS == 0
    src, dst, wl, wh, ea, eb, nvalid, n_tiles = _moe_plan(bucket, rank, counts, w_lo, w_hi, t)
    wspec = lambda shape, which: pl.BlockSpec(
        shape, (lambda i, ea_r, eb_r, nv_r: (ea_r[i], 0, 0)) if which == 0
        else (lambda i, ea_r, eb_r, nv_r: (eb_r[i], 0, 0)))
    w13 = (1, d, D_EXPERT)
    w2s = (1, D_EXPERT, d)
    smem_tile = lambda fn: pl.BlockSpec((1, 1, tm), lambda i, *_: (fn(i), 0, 0), memory_space=pltpu.SMEM)
    grid_spec = pltpu.PrefetchScalarGridSpec(
        num_scalar_prefetch=3,
        grid=(n_tiles,),
        in_specs=[smem_tile(lambda i: 0),
                  smem_tile(lambda i: jnp.minimum(i + 1, n_tiles - 1)),
                  smem_tile(lambda i: jnp.maximum(i - 1, 0)),
                  pl.BlockSpec((tm, 1), lambda i, *_: (i, 0)),
                  pl.BlockSpec((tm, 1), lambda i, *_: (i, 0)),
                  pl.BlockSpec(memory_space=pl.ANY),
                  wspec(w13, 0), wspec(w13, 0), wspec(w2s, 0),
                  wspec(w13, 1), wspec(w13, 1), wspec(w2s, 1),
                  pl.BlockSpec((1, d), lambda i, *_: (0, 0)),
                  pl.BlockSpec((1, d), lambda i, *_: (0, 0))],
        out_specs=pl.BlockSpec(memory_space=pl.ANY),
        scratch_shapes=[pltpu.VMEM((tm, d), F32), pltpu.VMEM((tm, d), F32),
                        pltpu.VMEM((tm, d), F32), pltpu.VMEM((tm, d), F32),
                        pltpu.SemaphoreType.DMA((2,)), pltpu.SemaphoreType.DMA((2,))],
    )
    w1b, w3b, w2b = w1.astype(BF16), w3.astype(BF16), w2.astype(BF16)
    return pl.pallas_call(
        _moe_kernel,
        out_shape=jax.ShapeDtypeStruct((t + 2 * tm, d), F32),
        grid_spec=grid_spec,
        compiler_params=_cparams(("arbitrary",)),
        name="moe",
    )(ea, eb, nvalid, src, src, dst, wl, wh, h2d, w1b, w3b, w2b, w1b, w3b, w2b,
      ln_g.reshape(1, d), ln_b.reshape(1, d))


def kernel(x, ln0_g, ln0_b, w_in, pool_w, pool_scale, cmp_k_pe, cmp_k_w1, cmp_k_w2, cmp_v_pe, cmp_v_w1, cmp_v_w2, rel_bias, conv_w, conv_b, dt_bias, a_log, d_skip, ssd_norm_g, w_br_pool, w_br_nsa, w_br_ssd, w_out, ln1_g, ln1_b, router_w, router_b, exp_w1, exp_w3, exp_w2, ln2_g, ln2_b):
    bsz, s, d = x.shape
    assert d == D_MODEL and s % MIX_TS == 0 and s // CMP_STRIDE == LANE and (bsz * s) % PROJ_TM == 0
    t = bsz * s
    ts, ws, bc = _nsa_bias_tables(rel_bias)
    et, ovt = _nsa_consts(s)
    h = _ln(x.reshape(t, d), ln0_g, ln0_b)
    for i in range(DEPTH):
        proj = _proj(h, _prep_w_in(w_in[i]), t)
        kc, vc = _compress(proj, bsz, s, cmp_k_pe[i], cmp_k_w1[i], cmp_k_w2[i],
                           cmp_v_pe[i], cmp_v_w1[i], cmp_v_w2[i])
        nsa_out = _nsa(proj, kc, vc, ts, ws, bc, et, ovt, bsz, s)
        ssd_y = _ssd(proj, bsz, s, conv_w[i], conv_b[i], dt_bias[i], a_log[i], d_skip[i], ssd_norm_g[i])
        h1 = _mix(proj, nsa_out, ssd_y, h, bsz, s, pool_w[i], pool_scale[i], w_br_pool[i], w_br_nsa[i],
                  w_br_ssd[i], w_out[i], ln1_g[i], ln1_b[i])
        bucket, w_lo, w_hi, rank, cnt = _router(h1, router_w, router_b)
        counts = cnt[:N_BUCKETS, 0].astype(jnp.int32)
        h = _moe(h1, bucket[0], rank[0], counts, w_lo[0], w_hi[0], exp_w1[i], exp_w3[i], exp_w2[i],
                 ln2_g[i], ln2_b[i])
    return h[:t].reshape(bsz, s, d)
```

```python
import functools
import math

import numpy as np
import jax
import jax.numpy as jnp
from jax import lax
from jax.experimental import pallas as pl
from jax.experimental.pallas import tpu as pltpu

F32 = jnp.float32
BF16 = jnp.bfloat16

D_MODEL = 1024
DEPTH = 2
DN_ALPHA = (2.0 * DEPTH) ** 0.25
LN_EPS = 1e-5
NEG = -1e30
BIG = 1e6

POOL_WINDOWS = (2, 4, 8, 16)
POOL_GROUP = 128
POOL_WIDTH = 512
MAX_POOL_WINDOW = 16

NSA_HEAD_DIM = 64
NSA_KV_HEADS = 2
NSA_HPG = 4
NSA_HEADS = 8
NSA_WIDTH = 512
CMP_LEN = 32
CMP_STRIDE = 16
CMP_HIDDEN = 128
SLC_LEN = 64
SLC_TOPN = 8
WIN_LEN = 512
Q_BLOCK = 128
REL_BUCKETS = 32
REL_MAX_DIST = 128

SSD_D_INNER = 1024
SSD_HEAD_DIM = 64
SSD_HEADS = 16
SSD_GROUPS = 4
SSD_STATE = 128
SSD_CONV = 4
SSD_CHUNK = 128
SSD_XBC = 2048

N_EXPERTS = 16
N_EXPERT_GROUPS = 4
EXPERTS_PER_GROUP = 4
D_EXPERT = 512
PAIRS = ((0, 1), (0, 2), (0, 3), (1, 2), (1, 3), (2, 3))
N_BUCKETS = N_EXPERT_GROUPS * len(PAIRS)

LANE = 128
SUBLANE = 8
VMEM_LIMIT = 48 * 1024 * 1024

C_POOL = 0
C_Q = 512
C_KV = 1024
C_NG = 1792
C_DT = 1920
C_XBC = 2048
C_Z = 4096
C_GATE = 5120
D_PAD = 8192

PROJ_TM = 1024
PROJ_TN = 2048
MIX_TS = 256
ROUTER_TM = 1024
MOE_TM = 256


def _dot(a, b):
    return jnp.dot(a, b, preferred_element_type=F32)


def _dot_nt(a, b):
    return lax.dot_general(a, b, (((1,), (1,)), ((), ())), preferred_element_type=F32)


def _dot_tn(a, b):
    return lax.dot_general(a, b, (((0,), (0,)), ((), ())), preferred_element_type=F32)


def _split3(x):
    x1 = x.astype(BF16)
    r1 = x - x1.astype(F32)
    x2 = r1.astype(BF16)
    x3 = (r1 - x2.astype(F32)).astype(BF16)
    return x1, x2, x3


def _layer_norm(x, g, b):
    mu = jnp.mean(x, -1, keepdims=True)
    xc = x - mu
    var = jnp.mean(xc * xc, -1, keepdims=True)
    return xc * lax.rsqrt(var + LN_EPS) * g + b


def _cparams(sem):
    return pltpu.CompilerParams(dimension_semantics=sem, vmem_limit_bytes=VMEM_LIMIT)


ROW_TILE = (D_MODEL // LANE, LANE)


def _load_rows(ref):
    return jnp.concatenate([ref[:, j, :] for j in range(ROW_TILE[0])], axis=1)


def _store_rows(ref, val):
    for j in range(ROW_TILE[0]):
        ref[:, j, :] = val[:, j * LANE:(j + 1) * LANE]


def _row_spec(n, index):
    return pl.BlockSpec((n,) + ROW_TILE, lambda *ids: (index(*ids), 0, 0))


def _ln_kernel(x_ref, g_ref, b_ref, o_ref):
    _store_rows(o_ref, _layer_norm(x_ref[...], g_ref[...], b_ref[...]))


def _ln(x2d, g, b):
    t, d = x2d.shape
    tm = 512
    return pl.pallas_call(
        _ln_kernel,
        out_shape=jax.ShapeDtypeStruct((t,) + ROW_TILE, F32),
        grid=(t // tm,),
        in_specs=[pl.BlockSpec((tm, d), lambda i: (i, 0)),
                  pl.BlockSpec((1, d), lambda i: (0, 0)),
                  pl.BlockSpec((1, d), lambda i: (0, 0))],
        out_specs=_row_spec(tm, lambda i: i),
        compiler_params=_cparams(("parallel",)),
        name="ln0",
    )(x2d, g.reshape(1, d), b.reshape(1, d))


def _proj_kernel(h_ref, w_ref, o_ref, hb_ref):
    @pl.when(pl.program_id(1) == 0)
    def _():
        hb_ref[...] = _load_rows(h_ref).astype(BF16)

    o_ref[...] = _dot(hb_ref[...], w_ref[...])


def _proj(h3, w_pad, t):
    d = D_MODEL
    return pl.pallas_call(
        _proj_kernel,
        out_shape=jax.ShapeDtypeStruct((t, D_PAD), F32),
        grid=(t // PROJ_TM, D_PAD // PROJ_TN),
        in_specs=[_row_spec(PROJ_TM, lambda i, j: i),
                  pl.BlockSpec((d, PROJ_TN), lambda i, j: (0, j))],
        out_specs=pl.BlockSpec((PROJ_TM, PROJ_TN), lambda i, j: (i, j)),
        scratch_shapes=[pltpu.VMEM((PROJ_TM, d), BF16)],
        compiler_params=_cparams(("parallel", "arbitrary")),
        name="proj",
    )(h3, w_pad)


def _pad_cols(w, n):
    return jnp.pad(w, ((0, 0), (0, n - w.shape[1])))


def _prep_w_in(w_in):
    d = w_in.shape[0]
    o = 0
    w_pool = w_in[:, o:o + POOL_WIDTH]; o += POOL_WIDTH
    w_q = w_in[:, o:o + NSA_WIDTH]; o += NSA_WIDTH
    w_kv = w_in[:, o:o + 768]; o += 768
    w_ng = w_in[:, o:o + 24]; o += 24
    w_z = w_in[:, o:o + SSD_D_INNER]; o += SSD_D_INNER
    w_xbc = w_in[:, o:o + SSD_XBC]; o += SSD_XBC
    w_dt = w_in[:, o:o + SSD_HEADS]; o += SSD_HEADS
    w_gate = w_in[:, o:o + 3 * D_MODEL]
    w_q = w_q.reshape(d, NSA_KV_HEADS, NSA_HPG, NSA_HEAD_DIM).transpose(0, 2, 1, 3).reshape(d, NSA_WIDTH)
    w = jnp.concatenate([w_pool, w_q, w_kv, _pad_cols(w_ng, LANE), _pad_cols(w_dt, LANE),
                         w_xbc, w_z, w_gate], axis=1)
    return w.astype(BF16)


def _compress_one(tok_ref, pe_ref, w1_ref, w2_ref):
    n16 = tok_ref.shape[0] // CMP_STRIDE
    acc_a = jnp.zeros((n16, 2 * CMP_HIDDEN), F32)
    acc_b = jnp.zeros((n16, 2 * CMP_HIDDEN), F32)
    for l in range(CMP_STRIDE):
        rows = tok_ref[pl.ds(l, n16, stride=CMP_STRIDE), :]
        xa = (rows + pe_ref[l:l + 1, :]).astype(BF16)
        xb = (rows + pe_ref[CMP_STRIDE + l:CMP_STRIDE + l + 1, :]).astype(BF16)
        acc_a = acc_a + _dot(xa, w1_ref[l])
        acc_b = acc_b + _dot(xb, w1_ref[CMP_STRIDE + l])
    hid = acc_a + pltpu.roll(acc_b, n16 - 1, axis=0)
    hid = jax.nn.gelu(hid)
    return _dot(hid.astype(BF16), w2_ref[...])


def _compress_kernel(kt_ref, vt_ref, pek_ref, w1k_ref, w2k_ref, pev_ref, w1v_ref, w2v_ref,
                     kc_ref, vc_ref):
    kc_ref[0] = _compress_one(kt_ref, pek_ref, w1k_ref, w2k_ref).astype(BF16)
    vc_ref[0] = _compress_one(vt_ref, pev_ref, w1v_ref, w2v_ref).T.astype(BF16)


def _blockdiag2(w):
    z = jnp.zeros_like(w)
    return jnp.concatenate([jnp.concatenate([w, z], -1), jnp.concatenate([z, w], -1)], -2)


def _compress(proj2d, bsz, s, pe_k, w1_k, w2_k, pe_v, w1_v, w2_v):
    n16 = s // CMP_STRIDE

    def prep(pe, w1, w2):
        pe2 = jnp.concatenate([pe, pe], axis=-1)
        w1b = _blockdiag2(w1.reshape(CMP_LEN, NSA_HEAD_DIM, CMP_HIDDEN)).astype(BF16)
        w2b = _blockdiag2(w2).astype(BF16)
        return pe2, w1b, w2b

    pk = prep(pe_k, w1_k, w2_k)
    pv = prep(pe_v, w1_v, w2_v)
    full = lambda shape: pl.BlockSpec(shape, lambda b: (0,) * len(shape))
    wspecs = [full((CMP_LEN, LANE)), full((CMP_LEN, LANE, 2 * CMP_HIDDEN)), full((2 * CMP_HIDDEN, LANE))]
    return pl.pallas_call(
        _compress_kernel,
        out_shape=(jax.ShapeDtypeStruct((bsz, n16, LANE), BF16),
                   jax.ShapeDtypeStruct((bsz, n16, LANE), BF16)),
        grid=(bsz,),
        in_specs=[pl.BlockSpec((s, LANE), lambda b: (b, C_KV // LANE)),
                  pl.BlockSpec((s, LANE), lambda b: (b, C_KV // LANE + 1))] + wspecs + wspecs,
        out_specs=(pl.BlockSpec((1, n16, LANE), lambda b: (b, 0, 0)),
                   pl.BlockSpec((1, n16, LANE), lambda b: (b, 0, 0))),
        compiler_params=_cparams(("parallel",)),
        name="nsa_compress",
    )(proj2d, proj2d, *pk, *pv)


def _rel_bucket(dist):
    n = jnp.maximum(dist, 0)
    max_exact = REL_BUCKETS // 2
    nf = jnp.maximum(n, 1).astype(F32)
    large = max_exact + (jnp.log(nf / max_exact) / math.log(REL_MAX_DIST / max_exact)
                         * (REL_BUCKETS - max_exact)).astype(jnp.int32)
    large = jnp.minimum(large, REL_BUCKETS - 1)
    return jnp.where(n < max_exact, n, large)


LOG2E = math.log2(math.e)
WIN_BACK = WIN_LEN // Q_BLOCK
WIN_KEYS = (WIN_BACK + 1) * Q_BLOCK
WIN_STRIP_TILES = 2 * WIN_BACK + 1
SEL_TILES = 4
SEL_KEYS = SEL_TILES * Q_BLOCK
SEL_STRIP_TILES = 3 * SEL_TILES - 1
CMP_BIAS_ROWS = 2 * LANE
PV_ROWS = NSA_HEAD_DIM + 16


def _nsa_bias_tables(rel_bias):
    tbl = rel_bias.astype(F32)
    far = tbl[REL_BUCKETS - 1]
    hq = NSA_HPG * Q_BLOCK

    def lookup(dist):
        onehot = (_rel_bucket(jnp.asarray(dist))[..., None] == jnp.arange(REL_BUCKETS)).astype(F32)
        return jnp.einsum("...b,bh->...h", onehot, tbl, precision=lax.Precision.HIGHEST)

    k = np.arange(Q_BLOCK)[None, :, None]
    q = np.arange(Q_BLOCK)[None, None, :]

    def strip(first, last, visible):
        n = first - last + 1
        dist = np.arange(first, last - 1, -1)[:, None, None] * Q_BLOCK + q - k
        v = jnp.where(jnp.asarray(visible(dist))[..., None], (lookup(dist) - far) * LOG2E, NEG)
        v = v.reshape(n, Q_BLOCK, Q_BLOCK, NSA_KV_HEADS, NSA_HPG).transpose(3, 0, 1, 4, 2)
        return v.reshape(NSA_KV_HEADS, n * Q_BLOCK, hq)

    ts = strip(2 * SEL_TILES - 1, 1 - SEL_TILES, lambda d: d >= 0)
    ws = strip(WIN_BACK, -WIN_BACK, lambda d: (d >= 0) & (d < WIN_LEN))
    r = np.arange(CMP_BIAS_ROWS)[:, None]
    dist_c = q[0] - CMP_STRIDE * (r - LANE) - (CMP_LEN - 1)
    bc = jnp.where(jnp.asarray(dist_c >= 0)[..., None], lookup(dist_c) * LOG2E, NEG)
    bc = bc.reshape(CMP_BIAS_ROWS, Q_BLOCK, NSA_KV_HEADS, NSA_HPG).transpose(2, 0, 3, 1)
    bc = bc.reshape(NSA_KV_HEADS, CMP_BIAS_ROWS, hq)
    return ts, ws, bc


def _nsa_consts(s):
    n_slc = s // SLC_LEN
    et = (np.arange(s)[:, None] // SLC_LEN == np.arange(LANE)[None, :]).astype(np.float32)
    n_cmp = (s - CMP_LEN) // CMP_STRIDE + 1
    c0 = np.arange(n_cmp) * CMP_STRIDE
    s0 = np.arange(n_slc) * SLC_LEN
    ov = np.clip(np.minimum(c0[None, :] + CMP_LEN, s0[:, None] + SLC_LEN)
                 - np.maximum(c0[None, :], s0[:, None]), 0, None) / CMP_LEN
    ovt = np.zeros((n_slc, LANE), np.float32)
    ovt[:, :n_cmp] = ov
    return jnp.asarray(et, BF16), jnp.asarray(ovt, BF16)


def _nsa_kernel(q_ref, ng_ref, ks_ref, vs_ref, kw_ref, vw_ref, kc_ref, vct_ref,
                ts_ref, ws_ref, bc_ref, et_ref, ovt_ref, o_ref,
                ksb_ref, kwb_ref, vst_ref, vwt_ref, os_ref, *, nq, n_slc):
    qi = pl.program_id(1)
    hq = NSA_HPG * Q_BLOCK
    dh = NSA_HEAD_DIM
    groups = range(NSA_KV_HEADS)
    gsl = [slice(g * dh, (g + 1) * dh) for g in groups]

    @pl.when(qi == 0)
    def _():
        ksb_ref[:, 0:LANE] = ks_ref[...].astype(BF16)
        ksb_ref[:, LANE:] = et_ref[...]
        kwb_ref[...] = kw_ref[...].astype(BF16)
        ones_rows = jnp.where(lax.broadcasted_iota(jnp.int32, (PV_ROWS - dh, ks_ref.shape[0]), 0) == 0, 1.0, 0.0)
        for g in groups:
            vst_ref[g, dh:, :] = ones_rows.astype(BF16)
            vwt_ref[g, dh:, :] = ones_rows.astype(BF16)
        for kt in range(nq):
            sl = slice(kt * Q_BLOCK, (kt + 1) * Q_BLOCK)
            vs_t = vs_ref[sl, :].T
            vw_t = vw_ref[sl, :].T
            for g in groups:
                vst_ref[g, 0:dh, sl] = vs_t[gsl[g], :].astype(BF16)
                vwt_ref[g, 0:dh, sl] = vw_t[gsl[g], :].astype(BF16)

    row_lo = lax.broadcasted_iota(jnp.int32, (LANE, hq), 0) < dh

    def heads4(m):
        return jnp.concatenate([m] * NSA_HPG, axis=1)

    gates_t = jax.nn.sigmoid(ng_ref[...]).T
    q_t = jnp.concatenate(
        [(q_ref[:, h * LANE:(h + 1) * LANE] * (dh ** -0.5 * LOG2E)).T for h in range(NSA_HPG)], axis=1)
    qg_t = [jnp.where(row_lo if g == 0 else jnp.logical_not(row_lo), q_t, 0.0).astype(BF16) for g in groups]

    boff = pl.multiple_of(LANE - SUBLANE * qi, SUBLANE)
    jrow = lax.broadcasted_iota(jnp.int32, (n_slc, LANE), 0)
    tj = qi * Q_BLOCK + lax.broadcasted_iota(jnp.int32, (n_slc, LANE), 1)
    jt = lax.shift_right_logical(tj, SLC_LEN.bit_length() - 1)
    forced = (jrow == 0) | (jrow == jt) | (jrow == jt - 1)
    valid = jrow * SLC_LEN <= tj
    o_c = []
    q_aug = []
    for g in groups:
        s = _dot(kc_ref[0], qg_t[g]) + bc_ref[g, pl.ds(boff, LANE), :]
        mx = jnp.max(s, axis=0, keepdims=True)
        p = jnp.exp2(s - mx)
        lsum = jnp.sum(p, axis=0, keepdims=True)
        pb = (p * jnp.where(mx > 0.5 * NEG, 1.0 / lsum, 0.0)).astype(BF16)
        o_c.append(_dot(vct_ref[0, gsl[g], :], pb))
        imp4 = _dot(ovt_ref[...], pb)
        imp = imp4[:, 0:LANE]
        for h in range(1, NSA_HPG):
            imp = imp + imp4[:, h * LANE:(h + 1) * LANE]
        val = jnp.where(forced, BIG, jnp.where(valid, imp, -BIG))
        rank = jnp.zeros((n_slc, LANE), F32)
        for i in range(n_slc):
            ci = val[i:i + 1, :]
            rank = rank + jnp.where(ci > val, 1.0, 0.0) + jnp.where(ci == val, (jrow > i).astype(F32), 0.0)
        block_rows = heads4(jnp.where(rank < SLC_TOPN, 0.0, NEG)).astype(BF16)
        q_aug.append(jnp.concatenate(
            [qg_t[g], block_rows, jnp.zeros((LANE - n_slc, hq), BF16)], axis=0))

    n_super = qi // SEL_TILES + 1
    diag = qi - SEL_TILES * (n_super - 1)

    def selected(n):
        nk = n * SEL_KEYS
        lo = max(n - 2, 0) * SEL_KEYS
        toff = pl.multiple_of((2 * SEL_TILES - 1 - diag) * Q_BLOCK - (nk - lo - SEL_KEYS), Q_BLOCK)
        for g in groups:
            parts = [(lo, nk, _dot(ksb_ref[lo:nk, :], q_aug[g]) + ts_ref[g, pl.ds(toff, nk - lo), :])]
            if lo > 0:
                parts.append((0, lo, _dot(ksb_ref[0:lo, :], q_aug[g])))
            m = parts[0][2].max(axis=0, keepdims=True)
            for _, _, s in parts[1:]:
                m = jnp.maximum(m, s.max(axis=0, keepdims=True))
            acc = None
            for a, b, s in parts:
                pv = _dot(vst_ref[g, :, a:b], jnp.exp2(s - m).astype(BF16))
                acc = pv if acc is None else acc + pv
            os_ref[g] = acc[0:dh] * (1.0 / acc[dh:dh + 1])

    for n in range(1, nq // SEL_TILES + 1):
        pl.when(n_super == n)(functools.partial(selected, n))
    o_s = [os_ref[g] for g in groups]

    w0 = jnp.maximum(qi - WIN_BACK, 0)
    woff = pl.multiple_of(w0 * Q_BLOCK, Q_BLOCK)
    boff_w = pl.multiple_of((WIN_BACK - (qi - w0)) * Q_BLOCK, Q_BLOCK)
    kw_bf = kwb_ref[pl.ds(woff, WIN_KEYS), :]
    o_w = []
    for g in groups:
        s = _dot(kw_bf, qg_t[g]) + ws_ref[g, pl.ds(boff_w, WIN_KEYS), :]
        p = jnp.exp2(s - jnp.max(s, axis=0, keepdims=True)).astype(BF16)
        acc = _dot(vwt_ref[g, :, pl.ds(woff, WIN_KEYS)], p)
        o_w.append(acc[0:dh] * (1.0 / acc[dh:dh + 1]))

    for h in range(NSA_HPG):
        hl = slice(h * LANE, (h + 1) * LANE)
        halves = []
        for g in groups:
            c = g * NSA_HPG + h
            halves.append(gates_t[c:c + 1, :] * o_c[g][:, hl]
                          + gates_t[NSA_HEADS + c:NSA_HEADS + c + 1, :] * o_s[g][:, hl]
                          + gates_t[2 * NSA_HEADS + c:2 * NSA_HEADS + c + 1, :] * o_w[g][:, hl])
        o_ref[:, hl] = jnp.concatenate(halves, axis=0).T


def _nsa(proj2d, kc, vct, ts, ws, bc, et, ovt, bsz, s):
    nq = s // Q_BLOCK
    n_slc = s // SLC_LEN
    hq = NSA_HPG * Q_BLOCK
    kvb = C_KV // LANE
    kv_spec = lambda blk: pl.BlockSpec((s, LANE), lambda b, qi: (b, kvb + blk))
    full = lambda shape: pl.BlockSpec(shape, lambda b, qi: (0,) * len(shape), pipeline_mode=pl.Buffered(1))
    return pl.pallas_call(
        functools.partial(_nsa_kernel, nq=nq, n_slc=n_slc),
        out_shape=jax.ShapeDtypeStruct((bsz * s, NSA_WIDTH), F32),
        grid=(bsz, nq),
        in_specs=[pl.BlockSpec((Q_BLOCK, NSA_WIDTH), lambda b, qi: (b * nq + qi, C_Q // NSA_WIDTH)),
                  pl.BlockSpec((Q_BLOCK, LANE), lambda b, qi: (b * nq + qi, C_NG // LANE)),
                  kv_spec(2), kv_spec(3), kv_spec(4), kv_spec(5),
                  pl.BlockSpec((1, s // CMP_STRIDE, LANE), lambda b, qi: (b, 0, 0)),
                  pl.BlockSpec((1, LANE, s // CMP_STRIDE), lambda b, qi: (b, 0, 0)),
                  full((NSA_KV_HEADS, SEL_STRIP_TILES * Q_BLOCK, hq)),
                  full((NSA_KV_HEADS, WIN_STRIP_TILES * Q_BLOCK, hq)),
                  full((NSA_KV_HEADS, CMP_BIAS_ROWS, hq)),
                  full((s, LANE)),
                  full((n_slc, LANE))],
        out_specs=pl.BlockSpec((Q_BLOCK, NSA_WIDTH), lambda b, qi: (b * nq + qi, 0)),
        scratch_shapes=[pltpu.VMEM((s, 2 * LANE), BF16), pltpu.VMEM((s, LANE), BF16),
                        pltpu.VMEM((NSA_KV_HEADS, PV_ROWS, s), BF16), pltpu.VMEM((NSA_KV_HEADS, PV_ROWS, s), BF16),
                        pltpu.VMEM((NSA_KV_HEADS, NSA_HEAD_DIM, hq), F32)],
        compiler_params=_cparams(("parallel", "arbitrary")),
        name="nsa_attn",
    )(proj2d, proj2d, proj2d, proj2d, proj2d, proj2d, kc, vct, ts, ws, bc, et, ovt)


def _ssd_kernel(xbc_ref, prev_ref, z_ref, dt_ref, cw_ref, cb_ref, dtb_ref, alog_ref, alogx_ref,
                dskip_ref, ng_ref, e16_ref, tril_ref, o_ref, xcat_ref, state_ref):
    c = pl.program_id(1)
    L = SSD_CHUNK
    pair_w = 2 * SSD_HEAD_DIM

    @pl.when(c == 0)
    def _():
        state_ref[...] = jnp.zeros(state_ref.shape, F32)

    xcat_ref[0:SUBLANE, :] = jnp.where(c > 0, prev_ref[...], 0.0)
    xcat_ref[SUBLANE:, :] = xbc_ref[...]
    conv = cb_ref[...]
    for k in range(SSD_CONV):
        conv = conv + cw_ref[k:k + 1, :] * xcat_ref[pl.ds(SUBLANE - (SSD_CONV - 1) + k, L), :]
    xbc = jax.nn.silu(conv)
    xs = xbc[:, :SSD_D_INNER]
    bmat = xbc[:, SSD_D_INNER:SSD_D_INNER + SSD_GROUPS * SSD_STATE]
    cmat = xbc[:, SSD_D_INNER + SSD_GROUPS * SSD_STATE:]

    dt = jax.nn.softplus(dt_ref[...] + dtb_ref[...])
    tril = tril_ref[...]
    e16 = e16_ref[...]
    d1, d2, d3 = _split3(dt)
    dt_x = _dot(d1, e16) + _dot(d2, e16) + _dot(d3, e16)
    a_c = dt * (-jnp.exp(alog_ref[...]))
    a_x = dt_x * (-jnp.exp(alogx_ref[...]))
    c1, c2, c3 = _split3(a_c)
    acs = _dot(tril, c1) + _dot(tril, c2) + _dot(tril, c3)
    x1, x2, x3 = _split3(a_x)
    acs_x = _dot(tril, x1) + _dot(tril, x2) + _dot(tril, x3)
    acs_t = acs.T
    last = acs_x[L - 1:L, :]
    eacs = jnp.exp(acs_x)
    decay_in = jnp.exp(last - acs_x)
    cdec = jnp.exp(last)
    xdt = xs * dt_x
    wst = xdt * decay_in

    row = lax.broadcasted_iota(jnp.int32, (L, L), 0)
    lane = lax.broadcasted_iota(jnp.int32, (L, L), 1)
    tri = row >= lane
    lo = lane < SSD_HEAD_DIM

    ys = []
    for g in range(SSD_GROUPS):
        cg = cmat[:, g * SSD_STATE:(g + 1) * SSD_STATE].astype(BF16)
        bg = bmat[:, g * SSD_STATE:(g + 1) * SSD_STATE].astype(BF16)
        cb = _dot_nt(cg, bg)
        for pr in range(2):
            pp = g * 2 + pr
            ls = slice(pp * pair_w, (pp + 1) * pair_w)
            xp = xdt[:, ls]
            yd = None
            for e in range(2):
                hd = 2 * pp + e
                diff = acs[:, hd:hd + 1] - acs_t[hd:hd + 1, :]
                seg = jnp.exp(jnp.where(tri, diff, NEG))
                pm = (cb * seg).astype(BF16)
                xe = jnp.where(lo if e == 0 else jnp.logical_not(lo), xp, 0.0).astype(BF16)
                term = _dot(pm, xe)
                yd = term if yd is None else yd + term
            st = state_ref[pp]
            yo = _dot(cg, st.astype(BF16)) * eacs[:, ls]
            ys.append(yd + yo)
            state_ref[pp] = cdec[:, ls] * st + _dot_tn(bg, wst[:, ls].astype(BF16))

    y = jnp.concatenate(ys, axis=1) + xs * dskip_ref[...]
    y = y * jax.nn.silu(z_ref[...])
    gw = SSD_D_INNER // SSD_GROUPS
    outs = []
    for gi in range(SSD_GROUPS):
        yg = y[:, gi * gw:(gi + 1) * gw]
        outs.append(yg * lax.rsqrt(jnp.mean(yg * yg, -1, keepdims=True) + LN_EPS))
    o_ref[...] = jnp.concatenate(outs, axis=1) * ng_ref[...]


def _ssd(proj2d, bsz, s, conv_w, conv_b, dt_bias, a_log, d_skip, norm_g):
    nc = s // SSD_CHUNK
    L = SSD_CHUNK
    pad16 = lambda v: jnp.pad(v.reshape(1, SSD_HEADS), ((0, 0), (0, LANE - SSD_HEADS)))
    rep64 = lambda v: jnp.repeat(v, SSD_HEAD_DIM).reshape(1, SSD_D_INNER)
    e16 = np.zeros((LANE, SSD_D_INNER), np.float32)
    e16[np.arange(SSD_D_INNER) // SSD_HEAD_DIM, np.arange(SSD_D_INNER)] = 1.0
    tril = np.tril(np.ones((L, L), np.float32))
    full = lambda shape: pl.BlockSpec(shape, lambda b, c: (0,) * len(shape))
    rows8 = s // SUBLANE
    return pl.pallas_call(
        _ssd_kernel,
        out_shape=jax.ShapeDtypeStruct((bsz * s, SSD_D_INNER), F32),
        grid=(bsz, nc),
        in_specs=[pl.BlockSpec((L, SSD_XBC), lambda b, c: (b * nc + c, C_XBC // SSD_XBC)),
                  pl.BlockSpec((SUBLANE, SSD_XBC),
                               lambda b, c: (jnp.maximum(b * rows8 + c * (L // SUBLANE) - 1, 0), C_XBC // SSD_XBC)),
                  pl.BlockSpec((L, SSD_D_INNER), lambda b, c: (b * nc + c, C_Z // SSD_D_INNER)),
                  pl.BlockSpec((L, LANE), lambda b, c: (b * nc + c, C_DT // LANE)),
                  full((SSD_CONV, SSD_XBC)), full((1, SSD_XBC)), full((1, LANE)), full((1, LANE)),
                  full((1, SSD_D_INNER)), full((1, SSD_D_INNER)), full((1, SSD_D_INNER)),
                  full((LANE, SSD_D_INNER)), full((L, L))],
        out_specs=pl.BlockSpec((L, SSD_D_INNER), lambda b, c: (b * nc + c, 0)),
        scratch_shapes=[pltpu.VMEM((L + SUBLANE, SSD_XBC), F32),
                        pltpu.VMEM((SSD_HEADS // 2, SSD_STATE, 2 * SSD_HEAD_DIM), F32)],
        compiler_params=_cparams(("parallel", "arbitrary")),
        name="ssd",
    )(proj2d, proj2d, proj2d, proj2d, conv_w, conv_b.reshape(1, SSD_XBC), pad16(dt_bias), pad16(a_log),
      rep64(a_log), rep64(d_skip), norm_g.reshape(1, SSD_D_INNER),
      jnp.asarray(e16, BF16), jnp.asarray(tril, BF16))


def _mix_kernel(up_ref, upprev_ref, nsa_ref, ssd_ref, ga_ref, gb_ref, gc_ref, h_ref,
                pw_ref, ps_ref, wbp_ref, wbn_ref, wbs_ref, wo_ref, lg_ref, lb_ref, o_ref, *, ts):
    si = pl.program_id(1)
    w0 = MAX_POOL_WINDOW
    prev = jnp.where(si > 0, upprev_ref[...], 0.0)
    xcat = jnp.concatenate([prev, up_ref[...]], axis=0)
    tpos = si * ts + lax.broadcasted_iota(jnp.int32, (ts, POOL_GROUP), 0)
    ygs = []
    for gi, w in enumerate(POOL_WINDOWS):
        x = xcat[:, gi * POOL_GROUP:(gi + 1) * POOL_GROUP]
        acc = x
        d = 1
        while d < w:
            acc = acc + pltpu.roll(acc, d, axis=0)
            d *= 2
        cnt = jnp.minimum(tpos + 1, w).astype(F32)
        r = acc[w0:, :] / cnt - x[w0:, :]
        ygs.append(_dot(r.astype(BF16), pw_ref[gi]))
    y = jnp.concatenate(ygs, axis=1) * ps_ref[...]
    br_a = _dot(y.astype(BF16), wbp_ref[...])
    br_b = _dot(nsa_ref[...].astype(BF16), wbn_ref[...])
    br_c = _dot(ssd_ref[...].astype(BF16), wbs_ref[...])
    mixed = (jax.nn.sigmoid(ga_ref[...]) * br_a + jax.nn.sigmoid(gb_ref[...]) * br_b
             + jax.nn.sigmoid(gc_ref[...]) * br_c)
    mix = _dot(mixed.astype(BF16), wo_ref[...])
    _store_rows(o_ref, _layer_norm(DN_ALPHA * _load_rows(h_ref) + mix, lg_ref[...], lb_ref[...]))


def _mix(proj2d, nsa_out, ssd_y, h3, bsz, s, pool_w, pool_scale, w_br_pool, w_br_nsa, w_br_ssd,
         w_out, ln_g, ln_b):
    ts = MIX_TS
    ns = s // ts
    d = D_MODEL
    w0 = MAX_POOL_WINDOW
    wbn = w_br_nsa.reshape(NSA_KV_HEADS, NSA_HPG, NSA_HEAD_DIM, d).transpose(1, 0, 2, 3).reshape(NSA_WIDTH, d)
    full = lambda shape: pl.BlockSpec(shape, lambda b, si: (0,) * len(shape))
    row = lambda width, cb: pl.BlockSpec((ts, width), lambda b, si: (b * ns + si, cb))
    return pl.pallas_call(
        functools.partial(_mix_kernel, ts=ts),
        out_shape=jax.ShapeDtypeStruct((bsz * s,) + ROW_TILE, F32),
        grid=(bsz, ns),
        in_specs=[row(POOL_WIDTH, C_POOL // POOL_WIDTH),
                  pl.BlockSpec((w0, POOL_WIDTH),
                               lambda b, si: (jnp.maximum((b * s + si * ts) // w0 - 1, 0), C_POOL // POOL_WIDTH)),
                  row(NSA_WIDTH, 0), row(d, 0),
                  row(d, C_GATE // d), row(d, C_GATE // d + 1), row(d, C_GATE // d + 2),
                  _row_spec(ts, lambda b, si: b * ns + si),
                  full((len(POOL_WINDOWS), POOL_GROUP, POOL_GROUP)), full((1, POOL_WIDTH)),
                  full((POOL_WIDTH, d)), full((NSA_WIDTH, d)), full((d, d)), full((d, d)),
                  full((1, d)), full((1, d))],
        out_specs=_row_spec(ts, lambda b, si: b * ns + si),
        compiler_params=_cparams(("parallel", "arbitrary")),
        name="mix",
    )(proj2d, proj2d, nsa_out, ssd_y, proj2d, proj2d, proj2d, h3,
      pool_w.astype(BF16), pool_scale.reshape(1, POOL_WIDTH), w_br_pool.astype(BF16), wbn.astype(BF16),
      w_br_ssd.astype(BF16), w_out.astype(BF16), ln_g.reshape(1, d), ln_b.reshape(1, d))


RANK_CHUNK = 256
BUCKET_ROWS = 32


def _router_kernel(h_ref, rw_ref, rb_ref, tri_ref, bucket_ref, wlo_ref, whi_ref, rank_ref, cnt_ref, run_ref):
    @pl.when(pl.program_id(0) == 0)
    def _():
        run_ref[...] = jnp.zeros(run_ref.shape, F32)

    logits = _dot_nt(rw_ref[...], _load_rows(h_ref).astype(BF16))
    lg = [logits[e:e + 1, :] for e in range(N_EXPERTS)]
    mx = lg[0]
    for e in range(1, N_EXPERTS):
        mx = jnp.maximum(mx, lg[e])
    ex = [jnp.exp(v - mx) for v in lg]
    den = ex[0]
    for e in range(1, N_EXPERTS):
        den = den + ex[e]
    probs = [v / den for v in ex]
    score = [probs[e] + rb_ref[e:e + 1, :] for e in range(N_EXPERTS)]

    def group_vals(vals, grp):
        out = []
        for k in range(EXPERTS_PER_GROUP):
            v = vals[k]
            for gi in range(1, N_EXPERT_GROUPS):
                v = jnp.where(grp == gi, vals[gi * EXPERTS_PER_GROUP + k], v)
            out.append(v)
        return out

    gscore = []
    for gi in range(N_EXPERT_GROUPS):
        sc = score[gi * EXPERTS_PER_GROUP:(gi + 1) * EXPERTS_PER_GROUP]
        best = None
        for a, b in PAIRS:
            v = sc[a] + sc[b]
            best = v if best is None else jnp.maximum(best, v)
        gscore.append(best)
    grp = jnp.zeros(gscore[0].shape, jnp.int32)
    best = gscore[0]
    for gi in range(1, N_EXPERT_GROUPS):
        better = gscore[gi] > best
        grp = jnp.where(better, gi, grp)
        best = jnp.where(better, gscore[gi], best)

    sc = group_vals(score, grp)
    pr = group_vals(probs, grp)
    first = jnp.zeros(grp.shape, jnp.int32)
    fv = sc[0]
    for k in range(1, EXPERTS_PER_GROUP):
        better = sc[k] > fv
        first = jnp.where(better, k, first)
        fv = jnp.where(better, sc[k], fv)
    second = jnp.full(grp.shape, -1, jnp.int32)
    sv = jnp.full(fv.shape, -jnp.inf, F32)
    for k in range(EXPERTS_PER_GROUP):
        better = (first != k) & ((sc[k] > sv) | (second < 0))
        second = jnp.where(better, k, second)
        sv = jnp.where(better, sc[k], sv)
    lo = jnp.minimum(first, second)
    hi = jnp.maximum(first, second)
    p_lo = pr[0]
    p_hi = pr[0]
    for k in range(1, EXPERTS_PER_GROUP):
        p_lo = jnp.where(lo == k, pr[k], p_lo)
        p_hi = jnp.where(hi == k, pr[k], p_hi)
    tot = p_lo + p_hi
    pair = jnp.zeros(grp.shape, jnp.int32)
    for pi, (a, b) in enumerate(PAIRS):
        pair = jnp.where((lo == a) & (hi == b), pi, pair)
    bucket = grp * len(PAIRS) + pair
    bucket_ref[...] = bucket
    wlo_ref[...] = p_lo / tot
    whi_ref[...] = p_hi / tot

    tm = bucket.shape[1]
    onehot = jnp.where(lax.broadcasted_iota(jnp.int32, (BUCKET_ROWS, tm), 0) == bucket, 1.0, 0.0)
    run = run_ref[...]
    ranks = []
    for c in range(tm // RANK_CHUNK):
        oc = onehot[:, c * RANK_CHUNK:(c + 1) * RANK_CHUNK]
        before = _dot(oc.astype(BF16), tri_ref[...]) + jnp.concatenate([run] * (RANK_CHUNK // LANE), axis=1)
        ranks.append(jnp.sum(oc * before, axis=0, keepdims=True))
        run = run + jnp.broadcast_to(jnp.sum(oc, axis=1, keepdims=True), run.shape)
    rank_ref[...] = jnp.concatenate(ranks, axis=1).astype(jnp.int32)
    run_ref[...] = run
    cnt_ref[...] = run


def _router(h3, router_w, router_b):
    t, d = h3.shape[0], D_MODEL
    tm = ROUTER_TM
    rw = jnp.pad(router_w.T, ((0, LANE - N_EXPERTS), (0, 0))).astype(BF16)
    rb = jnp.broadcast_to(jnp.pad(router_b, (0, LANE - N_EXPERTS))[:, None], (LANE, tm))
    tri = jnp.asarray(np.triu(np.ones((RANK_CHUNK, RANK_CHUNK), np.float32), 1), BF16)
    vec = lambda dt: jax.ShapeDtypeStruct((1, t), dt)
    row = pl.BlockSpec((1, tm), lambda i: (0, i))
    return pl.pallas_call(
        _router_kernel,
        out_shape=(vec(jnp.int32), vec(F32), vec(F32), vec(jnp.int32),
                   jax.ShapeDtypeStruct((BUCKET_ROWS, LANE), F32)),
        grid=(t // tm,),
        in_specs=[_row_spec(tm, lambda i: i),
                  pl.BlockSpec((LANE, d), lambda i: (0, 0)),
                  pl.BlockSpec((LANE, tm), lambda i: (0, 0)),
                  pl.BlockSpec((RANK_CHUNK, RANK_CHUNK), lambda i: (0, 0))],
        out_specs=(row, row, row, row, pl.BlockSpec((BUCKET_ROWS, LANE), lambda i: (0, 0))),
        scratch_shapes=[pltpu.VMEM((BUCKET_ROWS, LANE), F32)],
        compiler_params=_cparams(("arbitrary",)),
        name="router",
    )(h3, rw, rb, tri)


def _moe_plan(bucket, rank, counts, w_lo, w_hi, t):
    tm = MOE_TM
    n_tiles = t // tm + N_BUCKETS
    p_rows = n_tiles * tm
    tiles_per = (counts + tm - 1) // tm
    tile_end = jnp.cumsum(tiles_per)
    tile_start = tile_end - tiles_per
    in_bucket = bucket[:, None] == jnp.arange(N_BUCKETS)[None, :]
    dest = jnp.sum(jnp.where(in_bucket, (tile_start * tm)[None, :], 0), axis=1) + rank
    payload = jnp.stack([jnp.arange(t, dtype=jnp.int32), lax.bitcast_convert_type(w_lo, jnp.int32),
                         lax.bitcast_convert_type(w_hi, jnp.int32), jnp.ones((t,), jnp.int32)], axis=1)
    plan = jnp.zeros((p_rows, 4), jnp.int32).at[dest].set(payload, unique_indices=True)
    src = plan[:, 0]
    wl = lax.bitcast_convert_type(plan[:, 1], F32)
    wh = lax.bitcast_convert_type(plan[:, 2], F32)
    prow = jnp.arange(p_rows, dtype=jnp.int32)
    spare = t + ((prow // tm) % 2) * tm + prow % tm
    dst = jnp.where(plan[:, 3] > 0, src, spare)
    n_used = tile_end[-1]
    tile_ids = jnp.arange(n_tiles)
    tb = jnp.sum((tile_end[None, :] <= tile_ids[:, None]).astype(jnp.int32), axis=1)
    tb = jnp.where(tile_ids < n_used, tb, tb[jnp.maximum(n_used - 1, 0)])
    tb = jnp.minimum(tb, N_BUCKETS - 1)
    pairs = np.asarray(PAIRS, np.int32)
    grp = tb // len(PAIRS)
    ea = grp * EXPERTS_PER_GROUP + jnp.asarray(pairs[:, 0])[tb % len(PAIRS)]
    eb = grp * EXPERTS_PER_GROUP + jnp.asarray(pairs[:, 1])[tb % len(PAIRS)]
    nvalid = jnp.clip(counts[tb] - (tile_ids - tile_start[tb]) * tm, 0, tm)
    nvalid = jnp.where(tile_ids < n_used, nvalid, 0).astype(jnp.int32)
    return (src.reshape(n_tiles, 1, tm), dst.reshape(n_tiles, 1, tm), wl.reshape(p_rows, 1), wh.reshape(p_rows, 1),
            ea.astype(jnp.int32), eb.astype(jnp.int32), nvalid, n_tiles)


MOE_DMA_GROUPS = 8


def _moe_kernel(ea_ref, eb_ref, nv_ref, src0_ref, srcn_ref, dstp_ref, wl_ref, wh_ref, h_hbm,
                w1a_ref, w3a_ref, w2a_ref, w1b_ref, w3b_ref, w2b_ref, lg_ref, lb_ref,
                o_hbm, xbuf0, xbuf1, obuf0, obuf1, gsem, ssem):
    i = pl.program_id(0)
    tm = xbuf0.shape[0]
    xbufs = (xbuf0, xbuf1)
    obufs = (obuf0, obuf1)
    nv = nv_ref[i]
    nv_p1 = nv_ref[jnp.maximum(i - 1, 0)]
    nv_p2 = nv_ref[jnp.maximum(i - 2, 0)]
    odd = lax.rem(i, 2) == 1
    even = jnp.logical_not(odd)

    def gather(r, p, row):
        return pltpu.make_async_copy(h_hbm.at[pl.ds(row, 1)], xbufs[p].at[pl.ds(r, 1)], gsem.at[p])

    def scatter(r, p, row):
        return pltpu.make_async_copy(obufs[p].at[pl.ds(r, 1)], o_hbm.at[pl.ds(row, 1)], ssem.at[p])

    def wait_gather(p):
        for r in range(tm):
            gather(r, p, 0).wait()

    def wait_scatter(p):
        for r in range(tm):
            scatter(r, p, 0).wait()

    def tile(p, first):
        q = 1 - p
        per = tm // MOE_DMA_GROUPS

        def issue(k):
            for r in range(k * per, (k + 1) * per):
                gather(r, q, srcn_ref[0, 0, r]).start()
                if not first:
                    scatter(r, q, dstp_ref[0, 0, r]).start()

        wait_gather(p)
        x = _load_rows(xbufs[p])
        xb = x.astype(BF16)
        issue(0)
        h1 = _dot(xb, w1a_ref[0])
        issue(1)
        h3 = _dot(xb, w3a_ref[0])
        issue(2)
        ya = _dot((jax.nn.silu(h1) * h3).astype(BF16), w2a_ref[0])
        issue(3)
        h1 = _dot(xb, w1b_ref[0])
        issue(4)
        h3 = _dot(xb, w3b_ref[0])
        issue(5)
        yb = _dot((jax.nn.silu(h1) * h3).astype(BF16), w2b_ref[0])
        issue(6)
        y = wl_ref[...] * ya + wh_ref[...] * yb
        issue(7)
        _store_rows(obufs[p], _layer_norm(DN_ALPHA * x + y, lg_ref[...], lb_ref[...]))

    @pl.when(i == 0)
    def _():
        obuf1[...] = jnp.zeros(obuf1.shape, F32)
        n_real = o_hbm.shape[0] - 2 * tm
        for blk in range(2):
            cp = pltpu.make_async_copy(obuf1, o_hbm.at[pl.ds(n_real + blk * tm, tm)], ssem.at[1])
            cp.start()
            cp.wait()

    started_p2 = (i >= 2) & (nv_p2 > 0) & (nv_p1 > 0)
    pl.when(started_p2 & even)(functools.partial(wait_scatter, 0))
    pl.when(started_p2 & odd)(functools.partial(wait_scatter, 1))

    @pl.when((nv > 0) & (i == 0))
    def _():
        def start0(r, carry):
            gather(r, 0, src0_ref[0, 0, r]).start()
            return carry
        lax.fori_loop(0, tm, start0, 0)
        tile(0, True)

    pl.when((nv > 0) & (i > 0) & even)(functools.partial(tile, 0, False))
    pl.when((nv > 0) & odd)(functools.partial(tile, 1, False))

    def tail(p):
        q = 1 - p
        wait_gather(p)

        def start(r, carry):
            scatter(r, q, dstp_ref[0, 0, r]).start()
            return carry
        lax.fori_loop(0, tm, start, 0)
        wait_scatter(q)

    is_tail = (nv == 0) & (i >= 1) & (nv_p1 > 0)
    pl.when(is_tail & even)(functools.partial(tail, 0))
    pl.when(is_tail & odd)(functools.partial(tail, 1))


def _moe(h3, bucket, rank, counts, w_lo, w_hi, w1, w3, w2, ln_g, ln_b):
    t, d = h3.shape[0], D_MODEL
    tm = MOE_TM
    assert tm % MOE_DMA_GROUPS == 0
    src, dst, wl, wh, ea, eb, nvalid, n_tiles = _moe_plan(bucket, rank, counts, w_lo, w_hi, t)
    wspec = lambda shape, which: pl.BlockSpec(
        shape, (lambda i, ea_r, eb_r, nv_r: (ea_r[i], 0, 0)) if which == 0
        else (lambda i, ea_r, eb_r, nv_r: (eb_r[i], 0, 0)))
    w13 = (1, d, D_EXPERT)
    w2s = (1, D_EXPERT, d)
    smem_tile = lambda fn: pl.BlockSpec((1, 1, tm), lambda i, *_: (fn(i), 0, 0), memory_space=pltpu.SMEM)
    grid_spec = pltpu.PrefetchScalarGridSpec(
        num_scalar_prefetch=3,
        grid=(n_tiles,),
        in_specs=[smem_tile(lambda i: 0),
                  smem_tile(lambda i: jnp.minimum(i + 1, n_tiles - 1)),
                  smem_tile(lambda i: jnp.maximum(i - 1, 0)),
                  pl.BlockSpec((tm, 1), lambda i, *_: (i, 0)),
                  pl.BlockSpec((tm, 1), lambda i, *_: (i, 0)),
                  pl.BlockSpec(memory_space=pl.ANY),
                  wspec(w13, 0), wspec(w13, 0), wspec(w2s, 0),
                  wspec(w13, 1), wspec(w13, 1), wspec(w2s, 1),
                  pl.BlockSpec((1, d), lambda i, *_: (0, 0)),
                  pl.BlockSpec((1, d), lambda i, *_: (0, 0))],
        out_specs=pl.BlockSpec(memory_space=pl.ANY),
        scratch_shapes=[pltpu.VMEM((tm,) + ROW_TILE, F32)] * 4
        + [pltpu.SemaphoreType.DMA((2,)), pltpu.SemaphoreType.DMA((2,))],
    )
    w1b, w3b, w2b = w1.astype(BF16), w3.astype(BF16), w2.astype(BF16)
    return pl.pallas_call(
        _moe_kernel,
        out_shape=jax.ShapeDtypeStruct((t + 2 * tm,) + ROW_TILE, F32),
        grid_spec=grid_spec,
        compiler_params=_cparams(("arbitrary",)),
        name="moe",
    )(ea, eb, nvalid, src, src, dst, wl, wh, h3, w1b, w3b, w2b, w1b, w3b, w2b,
      ln_g.reshape(1, d), ln_b.reshape(1, d))


def kernel(x, ln0_g, ln0_b, w_in, pool_w, pool_scale, cmp_k_pe, cmp_k_w1, cmp_k_w2, cmp_v_pe, cmp_v_w1, cmp_v_w2, rel_bias, conv_w, conv_b, dt_bias, a_log, d_skip, ssd_norm_g, w_br_pool, w_br_nsa, w_br_ssd, w_out, ln1_g, ln1_b, router_w, router_b, exp_w1, exp_w3, exp_w2, ln2_g, ln2_b):
    bsz, s, d = x.shape
    assert d == D_MODEL and s % MIX_TS == 0 and s // CMP_STRIDE == LANE and (bsz * s) % PROJ_TM == 0
    t = bsz * s
    ts, ws, bc = _nsa_bias_tables(rel_bias)
    et, ovt = _nsa_consts(s)
    h = _ln(x.reshape(t, d), ln0_g, ln0_b)
    for i in range(DEPTH):
        proj = _proj(h, _prep_w_in(w_in[i]), t)
        kc, vc = _compress(proj, bsz, s, cmp_k_pe[i], cmp_k_w1[i], cmp_k_w2[i],
                           cmp_v_pe[i], cmp_v_w1[i], cmp_v_w2[i])
        nsa_out = _nsa(proj, kc, vc, ts, ws, bc, et, ovt, bsz, s)
        ssd_y = _ssd(proj, bsz, s, conv_w[i], conv_b[i], dt_bias[i], a_log[i], d_skip[i], ssd_norm_g[i])
        h1 = _mix(proj, nsa_out, ssd_y, h, bsz, s, pool_w[i], pool_scale[i], w_br_pool[i], w_br_nsa[i],
                  w_br_ssd[i], w_out[i], ln1_g[i], ln1_b[i])
        bucket, w_lo, w_hi, rank, cnt = _router(h1, router_w, router_b)
        counts = cnt[:N_BUCKETS, 0].astype(jnp.int32)
        h = _moe(h1, bucket[0], rank[0], counts, w_lo[0], w_hi[0], exp_w1[i], exp_w3[i], exp_w2[i],
                 ln2_g[i], ln2_b[i])
    return h[:t].reshape(bsz, s, d)
```

```python
import functools
import math

import numpy as np
import jax
import jax.numpy as jnp
from jax import lax
from jax.experimental import pallas as pl
from jax.experimental.pallas import tpu as pltpu

F32 = jnp.float32
BF16 = jnp.bfloat16

D_MODEL = 1024
DEPTH = 2
DN_ALPHA = (2.0 * DEPTH) ** 0.25
LN_EPS = 1e-5
NEG = -1e30
BIG = 1e6

POOL_WINDOWS = (2, 4, 8, 16)
POOL_GROUP = 128
POOL_WIDTH = 512
MAX_POOL_WINDOW = 16

NSA_HEAD_DIM = 64
NSA_KV_HEADS = 2
NSA_HPG = 4
NSA_HEADS = 8
NSA_WIDTH = 512
CMP_LEN = 32
CMP_STRIDE = 16
CMP_HIDDEN = 128
SLC_LEN = 64
SLC_TOPN = 8
WIN_LEN = 512
Q_BLOCK = 128
REL_BUCKETS = 32
REL_MAX_DIST = 128

SSD_D_INNER = 1024
SSD_HEAD_DIM = 64
SSD_HEADS = 16
SSD_GROUPS = 4
SSD_STATE = 128
SSD_CONV = 4
SSD_CHUNK = 128
SSD_XBC = 2048

N_EXPERTS = 16
N_EXPERT_GROUPS = 4
EXPERTS_PER_GROUP = 4
D_EXPERT = 512
PAIRS = ((0, 1), (0, 2), (0, 3), (1, 2), (1, 3), (2, 3))
N_BUCKETS = N_EXPERT_GROUPS * len(PAIRS)

LANE = 128
SUBLANE = 8
VMEM_LIMIT = 48 * 1024 * 1024

C_POOL = 0
C_Q = 512
C_KV = 1024
C_NG = 1792
C_DT = 1920
C_XBC = 2048
C_Z = 4096
C_GATE = 5120
D_PAD = 8192

PROJ_TM = 1024
PROJ_TN = 2048
MIX_TS = 256
ROUTER_TM = 1024
MOE_TM = 256


def _dot(a, b):
    return jnp.dot(a, b, preferred_element_type=F32)


def _dot_nt(a, b):
    return lax.dot_general(a, b, (((1,), (1,)), ((), ())), preferred_element_type=F32)


def _dot_tn(a, b):
    return lax.dot_general(a, b, (((0,), (0,)), ((), ())), preferred_element_type=F32)


def _split3(x):
    x1 = x.astype(BF16)
    r1 = x - x1.astype(F32)
    x2 = r1.astype(BF16)
    x3 = (r1 - x2.astype(F32)).astype(BF16)
    return x1, x2, x3


def _layer_norm(x, g, b):
    mu = jnp.mean(x, -1, keepdims=True)
    xc = x - mu
    var = jnp.mean(xc * xc, -1, keepdims=True)
    return xc * lax.rsqrt(var + LN_EPS) * g + b


def _cparams(sem):
    return pltpu.CompilerParams(dimension_semantics=sem, vmem_limit_bytes=VMEM_LIMIT)


ROW_TILE = (D_MODEL,)


def _load_rows(ref):
    return ref[...]


def _store_rows(ref, val):
    ref[...] = val


def _row_spec(n, index):
    return pl.BlockSpec((n,) + ROW_TILE, lambda *ids: (index(*ids), 0))


def _ln_kernel(x_ref, g_ref, b_ref, o_ref):
    _store_rows(o_ref, _layer_norm(x_ref[...], g_ref[...], b_ref[...]))


def _ln(x2d, g, b):
    t, d = x2d.shape
    tm = 512
    return pl.pallas_call(
        _ln_kernel,
        out_shape=jax.ShapeDtypeStruct((t,) + ROW_TILE, F32),
        grid=(t // tm,),
        in_specs=[pl.BlockSpec((tm, d), lambda i: (i, 0)),
                  pl.BlockSpec((1, d), lambda i: (0, 0)),
                  pl.BlockSpec((1, d), lambda i: (0, 0))],
        out_specs=_row_spec(tm, lambda i: i),
        compiler_params=_cparams(("parallel",)),
        name="ln0",
    )(x2d, g.reshape(1, d), b.reshape(1, d))


def _proj_kernel(h_ref, w_ref, o_ref, hb_ref):
    @pl.when(pl.program_id(1) == 0)
    def _():
        hb_ref[...] = _load_rows(h_ref).astype(BF16)

    o_ref[...] = _dot(hb_ref[...], w_ref[...])


def _proj(h3, w_pad, t):
    d = D_MODEL
    return pl.pallas_call(
        _proj_kernel,
        out_shape=jax.ShapeDtypeStruct((t, D_PAD), F32),
        grid=(t // PROJ_TM, D_PAD // PROJ_TN),
        in_specs=[_row_spec(PROJ_TM, lambda i, j: i),
                  pl.BlockSpec((d, PROJ_TN), lambda i, j: (0, j))],
        out_specs=pl.BlockSpec((PROJ_TM, PROJ_TN), lambda i, j: (i, j)),
        scratch_shapes=[pltpu.VMEM((PROJ_TM, d), BF16)],
        compiler_params=_cparams(("parallel", "arbitrary")),
        name="proj",
    )(h3, w_pad)


def _pad_cols(w, n):
    return jnp.pad(w, ((0, 0), (0, n - w.shape[1])))


def _prep_w_in(w_in):
    d = w_in.shape[0]
    o = 0
    w_pool = w_in[:, o:o + POOL_WIDTH]; o += POOL_WIDTH
    w_q = w_in[:, o:o + NSA_WIDTH]; o += NSA_WIDTH
    w_kv = w_in[:, o:o + 768]; o += 768
    w_ng = w_in[:, o:o + 24]; o += 24
    w_z = w_in[:, o:o + SSD_D_INNER]; o += SSD_D_INNER
    w_xbc = w_in[:, o:o + SSD_XBC]; o += SSD_XBC
    w_dt = w_in[:, o:o + SSD_HEADS]; o += SSD_HEADS
    w_gate = w_in[:, o:o + 3 * D_MODEL]
    w_q = w_q.reshape(d, NSA_KV_HEADS, NSA_HPG, NSA_HEAD_DIM).transpose(0, 2, 1, 3).reshape(d, NSA_WIDTH)
    w = jnp.concatenate([w_pool, w_q, w_kv, _pad_cols(w_ng, LANE), _pad_cols(w_dt, LANE),
                         w_xbc, w_z, w_gate], axis=1)
    return w.astype(BF16)


def _compress_one(tok_ref, pe_ref, w1_ref, w2_ref):
    n16 = tok_ref.shape[0] // CMP_STRIDE
    acc_a = jnp.zeros((n16, 2 * CMP_HIDDEN), F32)
    acc_b = jnp.zeros((n16, 2 * CMP_HIDDEN), F32)
    for l in range(CMP_STRIDE):
        rows = tok_ref[pl.ds(l, n16, stride=CMP_STRIDE), :]
        xa = (rows + pe_ref[l:l + 1, :]).astype(BF16)
        xb = (rows + pe_ref[CMP_STRIDE + l:CMP_STRIDE + l + 1, :]).astype(BF16)
        acc_a = acc_a + _dot(xa, w1_ref[l])
        acc_b = acc_b + _dot(xb, w1_ref[CMP_STRIDE + l])
    hid = acc_a + pltpu.roll(acc_b, n16 - 1, axis=0)
    hid = jax.nn.gelu(hid)
    return _dot(hid.astype(BF16), w2_ref[...])


def _compress_kernel(kt_ref, vt_ref, pek_ref, w1k_ref, w2k_ref, pev_ref, w1v_ref, w2v_ref,
                     kc_ref, vc_ref):
    kc_ref[0] = _compress_one(kt_ref, pek_ref, w1k_ref, w2k_ref).astype(BF16)
    vc_ref[0] = _compress_one(vt_ref, pev_ref, w1v_ref, w2v_ref).T.astype(BF16)


def _blockdiag2(w):
    z = jnp.zeros_like(w)
    return jnp.concatenate([jnp.concatenate([w, z], -1), jnp.concatenate([z, w], -1)], -2)


def _compress(proj2d, bsz, s, pe_k, w1_k, w2_k, pe_v, w1_v, w2_v):
    n16 = s // CMP_STRIDE

    def prep(pe, w1, w2):
        pe2 = jnp.concatenate([pe, pe], axis=-1)
        w1b = _blockdiag2(w1.reshape(CMP_LEN, NSA_HEAD_DIM, CMP_HIDDEN)).astype(BF16)
        w2b = _blockdiag2(w2).astype(BF16)
        return pe2, w1b, w2b

    pk = prep(pe_k, w1_k, w2_k)
    pv = prep(pe_v, w1_v, w2_v)
    full = lambda shape: pl.BlockSpec(shape, lambda b: (0,) * len(shape))
    wspecs = [full((CMP_LEN, LANE)), full((CMP_LEN, LANE, 2 * CMP_HIDDEN)), full((2 * CMP_HIDDEN, LANE))]
    return pl.pallas_call(
        _compress_kernel,
        out_shape=(jax.ShapeDtypeStruct((bsz, n16, LANE), BF16),
                   jax.ShapeDtypeStruct((bsz, n16, LANE), BF16)),
        grid=(bsz,),
        in_specs=[pl.BlockSpec((s, LANE), lambda b: (b, C_KV // LANE)),
                  pl.BlockSpec((s, LANE), lambda b: (b, C_KV // LANE + 1))] + wspecs + wspecs,
        out_specs=(pl.BlockSpec((1, n16, LANE), lambda b: (b, 0, 0)),
                   pl.BlockSpec((1, n16, LANE), lambda b: (b, 0, 0))),
        compiler_params=_cparams(("parallel",)),
        name="nsa_compress",
    )(proj2d, proj2d, *pk, *pv)


def _rel_bucket(dist):
    n = jnp.maximum(dist, 0)
    max_exact = REL_BUCKETS // 2
    nf = jnp.maximum(n, 1).astype(F32)
    large = max_exact + (jnp.log(nf / max_exact) / math.log(REL_MAX_DIST / max_exact)
                         * (REL_BUCKETS - max_exact)).astype(jnp.int32)
    large = jnp.minimum(large, REL_BUCKETS - 1)
    return jnp.where(n < max_exact, n, large)


LOG2E = math.log2(math.e)
WIN_BACK = WIN_LEN // Q_BLOCK
WIN_KEYS = (WIN_BACK + 1) * Q_BLOCK
WIN_STRIP_TILES = 2 * WIN_BACK + 1
SEL_TILES = 4
SEL_KEYS = SEL_TILES * Q_BLOCK
SEL_STRIP_TILES = 3 * SEL_TILES - 1
CMP_BIAS_ROWS = 2 * LANE
PV_ROWS = NSA_HEAD_DIM + 16


def _nsa_bias_tables(rel_bias):
    tbl = rel_bias.astype(F32)
    far = tbl[REL_BUCKETS - 1]
    hq = NSA_HPG * Q_BLOCK

    def lookup(dist):
        onehot = (_rel_bucket(jnp.asarray(dist))[..., None] == jnp.arange(REL_BUCKETS)).astype(F32)
        return jnp.einsum("...b,bh->...h", onehot, tbl, precision=lax.Precision.HIGHEST)

    k = np.arange(Q_BLOCK)[None, :, None]
    q = np.arange(Q_BLOCK)[None, None, :]

    def strip(first, last, visible):
        n = first - last + 1
        dist = np.arange(first, last - 1, -1)[:, None, None] * Q_BLOCK + q - k
        v = jnp.where(jnp.asarray(visible(dist))[..., None], (lookup(dist) - far) * LOG2E, NEG)
        v = v.reshape(n, Q_BLOCK, Q_BLOCK, NSA_KV_HEADS, NSA_HPG).transpose(3, 0, 1, 4, 2)
        return v.reshape(NSA_KV_HEADS, n * Q_BLOCK, hq)

    ts = strip(2 * SEL_TILES - 1, 1 - SEL_TILES, lambda d: d >= 0)
    ws = strip(WIN_BACK, -WIN_BACK, lambda d: (d >= 0) & (d < WIN_LEN))
    r = np.arange(CMP_BIAS_ROWS)[:, None]
    dist_c = q[0] - CMP_STRIDE * (r - LANE) - (CMP_LEN - 1)
    bc = jnp.where(jnp.asarray(dist_c >= 0)[..., None], lookup(dist_c) * LOG2E, NEG)
    bc = bc.reshape(CMP_BIAS_ROWS, Q_BLOCK, NSA_KV_HEADS, NSA_HPG).transpose(2, 0, 3, 1)
    bc = bc.reshape(NSA_KV_HEADS, CMP_BIAS_ROWS, hq)
    return ts, ws, bc


def _nsa_consts(s):
    n_slc = s // SLC_LEN
    et = (np.arange(s)[:, None] // SLC_LEN == np.arange(LANE)[None, :]).astype(np.float32)
    n_cmp = (s - CMP_LEN) // CMP_STRIDE + 1
    c0 = np.arange(n_cmp) * CMP_STRIDE
    s0 = np.arange(n_slc) * SLC_LEN
    ov = np.clip(np.minimum(c0[None, :] + CMP_LEN, s0[:, None] + SLC_LEN)
                 - np.maximum(c0[None, :], s0[:, None]), 0, None) / CMP_LEN
    ovt = np.zeros((n_slc, LANE), np.float32)
    ovt[:, :n_cmp] = ov
    return jnp.asarray(et, BF16), jnp.asarray(ovt, BF16)


def _nsa_kernel(q_ref, ng_ref, ks_ref, vs_ref, kw_ref, vw_ref, kc_ref, vct_ref,
                ts_ref, ws_ref, bc_ref, et_ref, ovt_ref, o_ref,
                ksb_ref, kwb_ref, vst_ref, vwt_ref, os_ref, *, nq, n_slc):
    qi = pl.program_id(1)
    hq = NSA_HPG * Q_BLOCK
    dh = NSA_HEAD_DIM
    groups = range(NSA_KV_HEADS)
    gsl = [slice(g * dh, (g + 1) * dh) for g in groups]

    @pl.when(qi == 0)
    def _():
        ksb_ref[:, 0:LANE] = ks_ref[...].astype(BF16)
        ksb_ref[:, LANE:] = et_ref[...]
        kwb_ref[...] = kw_ref[...].astype(BF16)
        ones_rows = jnp.where(lax.broadcasted_iota(jnp.int32, (PV_ROWS - dh, ks_ref.shape[0]), 0) == 0, 1.0, 0.0)
        for g in groups:
            vst_ref[g, dh:, :] = ones_rows.astype(BF16)
            vwt_ref[g, dh:, :] = ones_rows.astype(BF16)
        for kt in range(nq):
            sl = slice(kt * Q_BLOCK, (kt + 1) * Q_BLOCK)
            vs_t = vs_ref[sl, :].T
            vw_t = vw_ref[sl, :].T
            for g in groups:
                vst_ref[g, 0:dh, sl] = vs_t[gsl[g], :].astype(BF16)
                vwt_ref[g, 0:dh, sl] = vw_t[gsl[g], :].astype(BF16)

    row_lo = lax.broadcasted_iota(jnp.int32, (LANE, hq), 0) < dh

    def heads4(m):
        return jnp.concatenate([m] * NSA_HPG, axis=1)

    gates_t = jax.nn.sigmoid(ng_ref[...]).T
    q_t = jnp.concatenate(
        [(q_ref[:, h * LANE:(h + 1) * LANE] * (dh ** -0.5 * LOG2E)).T for h in range(NSA_HPG)], axis=1)
    qg_t = [jnp.where(row_lo if g == 0 else jnp.logical_not(row_lo), q_t, 0.0).astype(BF16) for g in groups]

    boff = pl.multiple_of(LANE - SUBLANE * qi, SUBLANE)
    jrow = lax.broadcasted_iota(jnp.int32, (n_slc, LANE), 0)
    tj = qi * Q_BLOCK + lax.broadcasted_iota(jnp.int32, (n_slc, LANE), 1)
    jt = lax.shift_right_logical(tj, SLC_LEN.bit_length() - 1)
    forced = (jrow == 0) | (jrow == jt) | (jrow == jt - 1)
    valid = jrow * SLC_LEN <= tj
    o_c = []
    q_aug = []
    for g in groups:
        s = _dot(kc_ref[0], qg_t[g]) + bc_ref[g, pl.ds(boff, LANE), :]
        mx = jnp.max(s, axis=0, keepdims=True)
        p = jnp.exp2(s - mx)
        lsum = jnp.sum(p, axis=0, keepdims=True)
        pb = (p * jnp.where(mx > 0.5 * NEG, 1.0 / lsum, 0.0)).astype(BF16)
        o_c.append(_dot(vct_ref[0, gsl[g], :], pb))
        imp4 = _dot(ovt_ref[...], pb)
        imp = imp4[:, 0:LANE]
        for h in range(1, NSA_HPG):
            imp = imp + imp4[:, h * LANE:(h + 1) * LANE]
        val = jnp.where(forced, BIG, jnp.where(valid, imp, -BIG))
        rank = jnp.zeros((n_slc, LANE), F32)
        for i in range(n_slc):
            ci = val[i:i + 1, :]
            rank = rank + jnp.where(ci > val, 1.0, 0.0) + jnp.where(ci == val, (jrow > i).astype(F32), 0.0)
        block_rows = heads4(jnp.where(rank < SLC_TOPN, 0.0, NEG)).astype(BF16)
        q_aug.append(jnp.concatenate(
            [qg_t[g], block_rows, jnp.zeros((LANE - n_slc, hq), BF16)], axis=0))

    n_super = qi // SEL_TILES + 1
    diag = qi - SEL_TILES * (n_super - 1)

    def selected(n):
        nk = n * SEL_KEYS
        lo = max(n - 2, 0) * SEL_KEYS
        toff = pl.multiple_of((2 * SEL_TILES - 1 - diag) * Q_BLOCK - (nk - lo - SEL_KEYS), Q_BLOCK)
        for g in groups:
            parts = [(lo, nk, _dot(ksb_ref[lo:nk, :], q_aug[g]) + ts_ref[g, pl.ds(toff, nk - lo), :])]
            if lo > 0:
                parts.append((0, lo, _dot(ksb_ref[0:lo, :], q_aug[g])))
            m = parts[0][2].max(axis=0, keepdims=True)
            for _, _, s in parts[1:]:
                m = jnp.maximum(m, s.max(axis=0, keepdims=True))
            acc = None
            for a, b, s in parts:
                pv = _dot(vst_ref[g, :, a:b], jnp.exp2(s - m).astype(BF16))
                acc = pv if acc is None else acc + pv
            os_ref[g] = acc[0:dh] * (1.0 / acc[dh:dh + 1])

    w0 = jnp.maximum(qi - WIN_BACK, 0)
    woff = pl.multiple_of(w0 * Q_BLOCK, Q_BLOCK)
    boff_w = pl.multiple_of((WIN_BACK - (qi - w0)) * Q_BLOCK, Q_BLOCK)
    kw_bf = kwb_ref[pl.ds(woff, WIN_KEYS), :]
    o_w = []
    for g in groups:
        s = _dot(kw_bf, qg_t[g]) + ws_ref[g, pl.ds(boff_w, WIN_KEYS), :]
        p = jnp.exp2(s - jnp.max(s, axis=0, keepdims=True)).astype(BF16)
        acc = _dot(vwt_ref[g, :, pl.ds(woff, WIN_KEYS)], p)
        o_w.append(acc[0:dh] * (1.0 / acc[dh:dh + 1]))

    for n in range(1, nq // SEL_TILES + 1):
        pl.when(n_super == n)(functools.partial(selected, n))
    o_s = [os_ref[g] for g in groups]

    for h in range(NSA_HPG):
        hl = slice(h * LANE, (h + 1) * LANE)
        halves = []
        for g in groups:
            c = g * NSA_HPG + h
            halves.append(gates_t[c:c + 1, :] * o_c[g][:, hl]
                          + gates_t[NSA_HEADS + c:NSA_HEADS + c + 1, :] * o_s[g][:, hl]
                          + gates_t[2 * NSA_HEADS + c:2 * NSA_HEADS + c + 1, :] * o_w[g][:, hl])
        o_ref[:, hl] = jnp.concatenate(halves, axis=0).T


def _nsa(proj2d, kc, vct, ts, ws, bc, et, ovt, bsz, s):
    nq = s // Q_BLOCK
    n_slc = s // SLC_LEN
    hq = NSA_HPG * Q_BLOCK
    kvb = C_KV // LANE
    kv_spec = lambda blk: pl.BlockSpec((s, LANE), lambda b, qi: (b, kvb + blk))
    full = lambda shape: pl.BlockSpec(shape, lambda b, qi: (0,) * len(shape), pipeline_mode=pl.Buffered(1))
    return pl.pallas_call(
        functools.partial(_nsa_kernel, nq=nq, n_slc=n_slc),
        out_shape=jax.ShapeDtypeStruct((bsz * s, NSA_WIDTH), F32),
        grid=(bsz, nq),
        in_specs=[pl.BlockSpec((Q_BLOCK, NSA_WIDTH), lambda b, qi: (b * nq + qi, C_Q // NSA_WIDTH)),
                  pl.BlockSpec((Q_BLOCK, LANE), lambda b, qi: (b * nq + qi, C_NG // LANE)),
                  kv_spec(2), kv_spec(3), kv_spec(4), kv_spec(5),
                  pl.BlockSpec((1, s // CMP_STRIDE, LANE), lambda b, qi: (b, 0, 0)),
                  pl.BlockSpec((1, LANE, s // CMP_STRIDE), lambda b, qi: (b, 0, 0)),
                  full((NSA_KV_HEADS, SEL_STRIP_TILES * Q_BLOCK, hq)),
                  full((NSA_KV_HEADS, WIN_STRIP_TILES * Q_BLOCK, hq)),
                  full((NSA_KV_HEADS, CMP_BIAS_ROWS, hq)),
                  full((s, LANE)),
                  full((n_slc, LANE))],
        out_specs=pl.BlockSpec((Q_BLOCK, NSA_WIDTH), lambda b, qi: (b * nq + qi, 0)),
        scratch_shapes=[pltpu.VMEM((s, 2 * LANE), BF16), pltpu.VMEM((s, LANE), BF16),
                        pltpu.VMEM((NSA_KV_HEADS, PV_ROWS, s), BF16), pltpu.VMEM((NSA_KV_HEADS, PV_ROWS, s), BF16),
                        pltpu.VMEM((NSA_KV_HEADS, NSA_HEAD_DIM, hq), F32)],
        compiler_params=_cparams(("parallel", "arbitrary")),
        name="nsa_attn",
    )(proj2d, proj2d, proj2d, proj2d, proj2d, proj2d, kc, vct, ts, ws, bc, et, ovt)


def _ssd_kernel(xbc_ref, prev_ref, z_ref, dt_ref, cw_ref, cb_ref, dtb_ref, alog_ref,
                dskip_ref, ng_ref, e16_ref, tril_ref, o_ref, state_ref):
    c = pl.program_id(1)
    L = SSD_CHUNK
    pair_w = 2 * SSD_HEAD_DIM

    @pl.when(c == 0)
    def _():
        state_ref[...] = jnp.zeros(state_ref.shape, F32)

    xcat = jnp.concatenate([jnp.where(c > 0, prev_ref[...], 0.0), xbc_ref[...]], axis=0)
    acc = cw_ref[0:1, :] * xcat
    for k in range(1, SSD_CONV):
        acc = pltpu.roll(acc, 1, axis=0) + cw_ref[k:k + 1, :] * xcat
    xbc = jax.nn.silu(acc[SUBLANE:, :] + cb_ref[...])
    xs = xbc[:, :SSD_D_INNER]
    bmat = xbc[:, SSD_D_INNER:SSD_D_INNER + SSD_GROUPS * SSD_STATE]
    cmat = xbc[:, SSD_D_INNER + SSD_GROUPS * SSD_STATE:]

    dt = jax.nn.softplus(dt_ref[...] + dtb_ref[...])
    tril = tril_ref[...]
    e16 = e16_ref[...]
    d1, d2, d3 = _split3(dt)
    dt_x = _dot(d1, e16) + _dot(d2, e16) + _dot(d3, e16)
    a_c = dt * (-jnp.exp(alog_ref[...]))
    c1, c2, c3 = _split3(a_c)
    acs = _dot(tril, c1) + _dot(tril, c2) + _dot(tril, c3)
    x1, x2, x3 = _split3(acs)
    acs_x = _dot(x1, e16) + _dot(x2, e16) + _dot(x3, e16)
    acs_t = acs.T
    last = acs_x[L - 1:L, :]
    eacs = jnp.exp(acs_x)
    decay_in = jnp.exp(last - acs_x)
    cdec = jnp.exp(last)
    xdt = xs * dt_x
    wst = xdt * decay_in

    row = lax.broadcasted_iota(jnp.int32, (L, L), 0)
    lane = lax.broadcasted_iota(jnp.int32, (L, L), 1)
    tri = row >= lane
    lo = lane < SSD_HEAD_DIM

    ys = []
    for g in range(SSD_GROUPS):
        cg = cmat[:, g * SSD_STATE:(g + 1) * SSD_STATE].astype(BF16)
        bg = bmat[:, g * SSD_STATE:(g + 1) * SSD_STATE].astype(BF16)
        cb = _dot_nt(cg, bg)
        for pr in range(2):
            pp = g * 2 + pr
            ls = slice(pp * pair_w, (pp + 1) * pair_w)
            xp = xdt[:, ls]
            yd = None
            for e in range(2):
                hd = 2 * pp + e
                diff = acs[:, hd:hd + 1] - acs_t[hd:hd + 1, :]
                seg = jnp.exp(jnp.where(tri, diff, NEG))
                pm = (cb * seg).astype(BF16)
                xe = jnp.where(lo if e == 0 else jnp.logical_not(lo), xp, 0.0).astype(BF16)
                term = _dot(pm, xe)
                yd = term if yd is None else yd + term
            st = state_ref[pp]
            yo = _dot(cg, st.astype(BF16)) * eacs[:, ls]
            ys.append(yd + yo)
            state_ref[pp] = cdec[:, ls] * st + _dot_tn(bg, wst[:, ls].astype(BF16))

    y = jnp.concatenate(ys, axis=1) + xs * dskip_ref[...]
    y = y * jax.nn.silu(z_ref[...])
    gw = SSD_D_INNER // SSD_GROUPS
    outs = []
    for gi in range(SSD_GROUPS):
        yg = y[:, gi * gw:(gi + 1) * gw]
        outs.append(yg * lax.rsqrt(jnp.mean(yg * yg, -1, keepdims=True) + LN_EPS))
    o_ref[...] = jnp.concatenate(outs, axis=1) * ng_ref[...]


def _ssd(proj2d, bsz, s, conv_w, conv_b, dt_bias, a_log, d_skip, norm_g):
    nc = s // SSD_CHUNK
    L = SSD_CHUNK
    pad16 = lambda v: jnp.pad(v.reshape(1, SSD_HEADS), ((0, 0), (0, LANE - SSD_HEADS)))
    rep64 = lambda v: jnp.repeat(v, SSD_HEAD_DIM).reshape(1, SSD_D_INNER)
    e16 = np.zeros((LANE, SSD_D_INNER), np.float32)
    e16[np.arange(SSD_D_INNER) // SSD_HEAD_DIM, np.arange(SSD_D_INNER)] = 1.0
    tril = np.tril(np.ones((L, L), np.float32))
    full = lambda shape: pl.BlockSpec(shape, lambda b, c: (0,) * len(shape))
    rows8 = s // SUBLANE
    return pl.pallas_call(
        _ssd_kernel,
        out_shape=jax.ShapeDtypeStruct((bsz * s, SSD_D_INNER), F32),
        grid=(bsz, nc),
        in_specs=[pl.BlockSpec((L, SSD_XBC), lambda b, c: (b * nc + c, C_XBC // SSD_XBC)),
                  pl.BlockSpec((SUBLANE, SSD_XBC),
                               lambda b, c: (jnp.maximum(b * rows8 + c * (L // SUBLANE) - 1, 0), C_XBC // SSD_XBC)),
                  pl.BlockSpec((L, SSD_D_INNER), lambda b, c: (b * nc + c, C_Z // SSD_D_INNER)),
                  pl.BlockSpec((L, LANE), lambda b, c: (b * nc + c, C_DT // LANE)),
                  full((SSD_CONV, SSD_XBC)), full((1, SSD_XBC)), full((1, LANE)), full((1, LANE)),
                  full((1, SSD_D_INNER)), full((1, SSD_D_INNER)),
                  full((LANE, SSD_D_INNER)), full((L, L))],
        out_specs=pl.BlockSpec((L, SSD_D_INNER), lambda b, c: (b * nc + c, 0)),
        scratch_shapes=[pltpu.VMEM((SSD_HEADS // 2, SSD_STATE, 2 * SSD_HEAD_DIM), F32)],
        compiler_params=_cparams(("parallel", "arbitrary")),
        name="ssd",
    )(proj2d, proj2d, proj2d, proj2d, conv_w, conv_b.reshape(1, SSD_XBC), pad16(dt_bias), pad16(a_log),
      rep64(d_skip), norm_g.reshape(1, SSD_D_INNER),
      jnp.asarray(e16, BF16), jnp.asarray(tril, BF16))


def _mix_kernel(up_ref, upprev_ref, nsa_ref, ssd_ref, ga_ref, gb_ref, gc_ref, h_ref,
                pw_ref, ps_ref, wbp_ref, wbn_ref, wbs_ref, wo_ref, lg_ref, lb_ref, o_ref, *, ts):
    si = pl.program_id(1)
    w0 = MAX_POOL_WINDOW
    prev = jnp.where(si > 0, upprev_ref[...], 0.0)
    xcat = jnp.concatenate([prev, up_ref[...]], axis=0)
    tpos = si * ts + lax.broadcasted_iota(jnp.int32, (ts, POOL_GROUP), 0)
    ygs = []
    for gi, w in enumerate(POOL_WINDOWS):
        x = xcat[:, gi * POOL_GROUP:(gi + 1) * POOL_GROUP]
        acc = x
        d = 1
        while d < w:
            acc = acc + pltpu.roll(acc, d, axis=0)
            d *= 2
        cnt = jnp.minimum(tpos + 1, w).astype(F32)
        r = acc[w0:, :] / cnt - x[w0:, :]
        ygs.append(_dot(r.astype(BF16), pw_ref[gi]))
    y = jnp.concatenate(ygs, axis=1) * ps_ref[...]
    br_a = _dot(y.astype(BF16), wbp_ref[...])
    br_b = _dot(nsa_ref[...].astype(BF16), wbn_ref[...])
    br_c = _dot(ssd_ref[...].astype(BF16), wbs_ref[...])
    mixed = (jax.nn.sigmoid(ga_ref[...]) * br_a + jax.nn.sigmoid(gb_ref[...]) * br_b
             + jax.nn.sigmoid(gc_ref[...]) * br_c)
    mix = _dot(mixed.astype(BF16), wo_ref[...])
    _store_rows(o_ref, _layer_norm(DN_ALPHA * _load_rows(h_ref) + mix, lg_ref[...], lb_ref[...]))


def _mix(proj2d, nsa_out, ssd_y, h3, bsz, s, pool_w, pool_scale, w_br_pool, w_br_nsa, w_br_ssd,
         w_out, ln_g, ln_b):
    ts = MIX_TS
    ns = s // ts
    d = D_MODEL
    w0 = MAX_POOL_WINDOW
    wbn = w_br_nsa.reshape(NSA_KV_HEADS, NSA_HPG, NSA_HEAD_DIM, d).transpose(1, 0, 2, 3).reshape(NSA_WIDTH, d)
    full = lambda shape: pl.BlockSpec(shape, lambda b, si: (0,) * len(shape))
    row = lambda width, cb: pl.BlockSpec((ts, width), lambda b, si: (b * ns + si, cb))
    return pl.pallas_call(
        functools.partial(_mix_kernel, ts=ts),
        out_shape=jax.ShapeDtypeStruct((bsz * s,) + ROW_TILE, F32),
        grid=(bsz, ns),
        in_specs=[row(POOL_WIDTH, C_POOL // POOL_WIDTH),
                  pl.BlockSpec((w0, POOL_WIDTH),
                               lambda b, si: (jnp.maximum((b * s + si * ts) // w0 - 1, 0), C_POOL // POOL_WIDTH)),
                  row(NSA_WIDTH, 0), row(d, 0),
                  row(d, C_GATE // d), row(d, C_GATE // d + 1), row(d, C_GATE // d + 2),
                  _row_spec(ts, lambda b, si: b * ns + si),
                  full((len(POOL_WINDOWS), POOL_GROUP, POOL_GROUP)), full((1, POOL_WIDTH)),
                  full((POOL_WIDTH, d)), full((NSA_WIDTH, d)), full((d, d)), full((d, d)),
                  full((1, d)), full((1, d))],
        out_specs=_row_spec(ts, lambda b, si: b * ns + si),
        compiler_params=_cparams(("parallel", "arbitrary")),
        name="mix",
    )(proj2d, proj2d, nsa_out, ssd_y, proj2d, proj2d, proj2d, h3,
      pool_w.astype(BF16), pool_scale.reshape(1, POOL_WIDTH), w_br_pool.astype(BF16), wbn.astype(BF16),
      w_br_ssd.astype(BF16), w_out.astype(BF16), ln_g.reshape(1, d), ln_b.reshape(1, d))


RANK_CHUNK = 256
BUCKET_ROWS = 32


def _router_kernel(h_ref, rw_ref, rb_ref, tri_ref, bucket_ref, wlo_ref, whi_ref, rank_ref, cnt_ref, run_ref):
    @pl.when(pl.program_id(0) == 0)
    def _():
        run_ref[...] = jnp.zeros(run_ref.shape, F32)

    logits = _dot_nt(rw_ref[...], _load_rows(h_ref).astype(BF16))
    lg = [logits[e:e + 1, :] for e in range(N_EXPERTS)]
    mx = lg[0]
    for e in range(1, N_EXPERTS):
        mx = jnp.maximum(mx, lg[e])
    ex = [jnp.exp(v - mx) for v in lg]
    den = ex[0]
    for e in range(1, N_EXPERTS):
        den = den + ex[e]
    probs = [v / den for v in ex]
    score = [probs[e] + rb_ref[e:e + 1, :] for e in range(N_EXPERTS)]

    def group_vals(vals, grp):
        out = []
        for k in range(EXPERTS_PER_GROUP):
            v = vals[k]
            for gi in range(1, N_EXPERT_GROUPS):
                v = jnp.where(grp == gi, vals[gi * EXPERTS_PER_GROUP + k], v)
            out.append(v)
        return out

    gscore = []
    for gi in range(N_EXPERT_GROUPS):
        sc = score[gi * EXPERTS_PER_GROUP:(gi + 1) * EXPERTS_PER_GROUP]
        best = None
        for a, b in PAIRS:
            v = sc[a] + sc[b]
            best = v if best is None else jnp.maximum(best, v)
        gscore.append(best)
    grp = jnp.zeros(gscore[0].shape, jnp.int32)
    best = gscore[0]
    for gi in range(1, N_EXPERT_GROUPS):
        better = gscore[gi] > best
        grp = jnp.where(better, gi, grp)
        best = jnp.where(better, gscore[gi], best)

    sc = group_vals(score, grp)
    pr = group_vals(probs, grp)
    first = jnp.zeros(grp.shape, jnp.int32)
    fv = sc[0]
    for k in range(1, EXPERTS_PER_GROUP):
        better = sc[k] > fv
        first = jnp.where(better, k, first)
        fv = jnp.where(better, sc[k], fv)
    second = jnp.full(grp.shape, -1, jnp.int32)
    sv = jnp.full(fv.shape, -jnp.inf, F32)
    for k in range(EXPERTS_PER_GROUP):
        better = (first != k) & ((sc[k] > sv) | (second < 0))
        second = jnp.where(better, k, second)
        sv = jnp.where(better, sc[k], sv)
    lo = jnp.minimum(first, second)
    hi = jnp.maximum(first, second)
    p_lo = pr[0]
    p_hi = pr[0]
    for k in range(1, EXPERTS_PER_GROUP):
        p_lo = jnp.where(lo == k, pr[k], p_lo)
        p_hi = jnp.where(hi == k, pr[k], p_hi)
    tot = p_lo + p_hi
    pair = jnp.zeros(grp.shape, jnp.int32)
    for pi, (a, b) in enumerate(PAIRS):
        pair = jnp.where((lo == a) & (hi == b), pi, pair)
    bucket = grp * len(PAIRS) + pair
    bucket_ref[...] = bucket
    wlo_ref[...] = p_lo / tot
    whi_ref[...] = p_hi / tot

    tm = bucket.shape[1]
    onehot = jnp.where(lax.broadcasted_iota(jnp.int32, (BUCKET_ROWS, tm), 0) == bucket, 1.0, 0.0)
    run = run_ref[...]
    ranks = []
    for c in range(tm // RANK_CHUNK):
        oc = onehot[:, c * RANK_CHUNK:(c + 1) * RANK_CHUNK]
        before = _dot(oc.astype(BF16), tri_ref[...]) + jnp.concatenate([run] * (RANK_CHUNK // LANE), axis=1)
        ranks.append(jnp.sum(oc * before, axis=0, keepdims=True))
        run = run + jnp.broadcast_to(jnp.sum(oc, axis=1, keepdims=True), run.shape)
    rank_ref[...] = jnp.concatenate(ranks, axis=1).astype(jnp.int32)
    run_ref[...] = run
    cnt_ref[...] = run


def _router(h3, router_w, router_b):
    t, d = h3.shape[0], D_MODEL
    tm = ROUTER_TM
    rw = jnp.pad(router_w.T, ((0, LANE - N_EXPERTS), (0, 0))).astype(BF16)
    rb = jnp.broadcast_to(jnp.pad(router_b, (0, LANE - N_EXPERTS))[:, None], (LANE, tm))
    tri = jnp.asarray(np.triu(np.ones((RANK_CHUNK, RANK_CHUNK), np.float32), 1), BF16)
    vec = lambda dt: jax.ShapeDtypeStruct((1, t), dt)
    row = pl.BlockSpec((1, tm), lambda i: (0, i))
    return pl.pallas_call(
        _router_kernel,
        out_shape=(vec(jnp.int32), vec(F32), vec(F32), vec(jnp.int32),
                   jax.ShapeDtypeStruct((BUCKET_ROWS, LANE), F32)),
        grid=(t // tm,),
        in_specs=[_row_spec(tm, lambda i: i),
                  pl.BlockSpec((LANE, d), lambda i: (0, 0)),
                  pl.BlockSpec((LANE, tm), lambda i: (0, 0)),
                  pl.BlockSpec((RANK_CHUNK, RANK_CHUNK), lambda i: (0, 0))],
        out_specs=(row, row, row, row, pl.BlockSpec((BUCKET_ROWS, LANE), lambda i: (0, 0))),
        scratch_shapes=[pltpu.VMEM((BUCKET_ROWS, LANE), F32)],
        compiler_params=_cparams(("arbitrary",)),
        name="router",
    )(h3, rw, rb, tri)


def _moe_plan(bucket, rank, counts, w_lo, w_hi, t):
    tm = MOE_TM
    n_tiles = t // tm + N_BUCKETS
    p_rows = n_tiles * tm
    tiles_per = (counts + tm - 1) // tm
    tile_end = jnp.cumsum(tiles_per)
    tile_start = tile_end - tiles_per
    in_bucket = bucket[:, None] == jnp.arange(N_BUCKETS)[None, :]
    dest = jnp.sum(jnp.where(in_bucket, (tile_start * tm)[None, :], 0), axis=1) + rank
    payload = jnp.stack([jnp.arange(t, dtype=jnp.int32), lax.bitcast_convert_type(w_lo, jnp.int32),
                         lax.bitcast_convert_type(w_hi, jnp.int32), jnp.ones((t,), jnp.int32)], axis=1)
    plan = jnp.zeros((p_rows, 4), jnp.int32).at[dest].set(payload, unique_indices=True)
    src = plan[:, 0]
    wl = lax.bitcast_convert_type(plan[:, 1], F32)
    wh = lax.bitcast_convert_type(plan[:, 2], F32)
    prow = jnp.arange(p_rows, dtype=jnp.int32)
    spare = t + ((prow // tm) % 2) * tm + prow % tm
    dst = jnp.where(plan[:, 3] > 0, src, spare)
    n_used = tile_end[-1]
    tile_ids = jnp.arange(n_tiles)
    tb = jnp.sum((tile_end[None, :] <= tile_ids[:, None]).astype(jnp.int32), axis=1)
    tb = jnp.where(tile_ids < n_used, tb, tb[jnp.maximum(n_used - 1, 0)])
    tb = jnp.minimum(tb, N_BUCKETS - 1)
    pairs = np.asarray(PAIRS, np.int32)
    grp = tb // len(PAIRS)
    ea = grp * EXPERTS_PER_GROUP + jnp.asarray(pairs[:, 0])[tb % len(PAIRS)]
    eb = grp * EXPERTS_PER_GROUP + jnp.asarray(pairs[:, 1])[tb % len(PAIRS)]
    nvalid = jnp.clip(counts[tb] - (tile_ids - tile_start[tb]) * tm, 0, tm)
    nvalid = jnp.where(tile_ids < n_used, nvalid, 0).astype(jnp.int32)
    return (src.reshape(n_tiles, 1, tm), dst.reshape(n_tiles, 1, tm), wl.reshape(p_rows, 1), wh.reshape(p_rows, 1),
            ea.astype(jnp.int32), eb.astype(jnp.int32), nvalid, n_tiles)


MOE_DMA_GROUPS = 8


def _moe_kernel(ea_ref, eb_ref, nv_ref, src0_ref, srcn_ref, dstp_ref, wl_ref, wh_ref, h_hbm,
                w1a_ref, w3a_ref, w2a_ref, w1b_ref, w3b_ref, w2b_ref, lg_ref, lb_ref,
                o_hbm, xbuf0, xbuf1, obuf0, obuf1, gsem, ssem):
    i = pl.program_id(0)
    tm = xbuf0.shape[0]
    xbufs = (xbuf0, xbuf1)
    obufs = (obuf0, obuf1)
    nv = nv_ref[i]
    nv_p1 = nv_ref[jnp.maximum(i - 1, 0)]
    nv_p2 = nv_ref[jnp.maximum(i - 2, 0)]
    odd = lax.rem(i, 2) == 1
    even = jnp.logical_not(odd)

    def gather(r, p, row):
        return pltpu.make_async_copy(h_hbm.at[pl.ds(row, 1)], xbufs[p].at[pl.ds(r, 1)], gsem.at[p])

    def scatter(r, p, row):
        return pltpu.make_async_copy(obufs[p].at[pl.ds(r, 1)], o_hbm.at[pl.ds(row, 1)], ssem.at[p])

    def wait_gather(p):
        for r in range(tm):
            gather(r, p, 0).wait()

    def wait_scatter(p):
        for r in range(tm):
            scatter(r, p, 0).wait()

    def tile(p, first):
        q = 1 - p
        per = tm // MOE_DMA_GROUPS

        def issue(k):
            for r in range(k * per, (k + 1) * per):
                gather(r, q, srcn_ref[0, 0, r]).start()
                if not first:
                    scatter(r, q, dstp_ref[0, 0, r]).start()

        wait_gather(p)
        x = _load_rows(xbufs[p])
        xb = x.astype(BF16)
        issue(0)
        h1 = _dot(xb, w1a_ref[0])
        issue(1)
        h3 = _dot(xb, w3a_ref[0])
        issue(2)
        ya = _dot((jax.nn.silu(h1) * h3).astype(BF16), w2a_ref[0])
        issue(3)
        h1 = _dot(xb, w1b_ref[0])
        issue(4)
        h3 = _dot(xb, w3b_ref[0])
        issue(5)
        yb = _dot((jax.nn.silu(h1) * h3).astype(BF16), w2b_ref[0])
        issue(6)
        y = wl_ref[...] * ya + wh_ref[...] * yb
        issue(7)
        _store_rows(obufs[p], _layer_norm(DN_ALPHA * x + y, lg_ref[...], lb_ref[...]))

    @pl.when(i == 0)
    def _():
        obuf1[...] = jnp.zeros(obuf1.shape, F32)
        n_real = o_hbm.shape[0] - 2 * tm
        for blk in range(2):
            cp = pltpu.make_async_copy(obuf1, o_hbm.at[pl.ds(n_real + blk * tm, tm)], ssem.at[1])
            cp.start()
            cp.wait()

    started_p2 = (i >= 2) & (nv_p2 > 0) & (nv_p1 > 0)
    pl.when(started_p2 & even)(functools.partial(wait_scatter, 0))
    pl.when(started_p2 & odd)(functools.partial(wait_scatter, 1))

    @pl.when((nv > 0) & (i == 0))
    def _():
        def start0(r, carry):
            gather(r, 0, src0_ref[0, 0, r]).start()
            return carry
        lax.fori_loop(0, tm, start0, 0)
        tile(0, True)

    pl.when((nv > 0) & (i > 0) & even)(functools.partial(tile, 0, False))
    pl.when((nv > 0) & odd)(functools.partial(tile, 1, False))

    def tail(p):
        q = 1 - p
        wait_gather(p)

        def start(r, carry):
            scatter(r, q, dstp_ref[0, 0, r]).start()
            return carry
        lax.fori_loop(0, tm, start, 0)
        wait_scatter(q)

    is_tail = (nv == 0) & (i >= 1) & (nv_p1 > 0)
    pl.when(is_tail & even)(functools.partial(tail, 0))
    pl.when(is_tail & odd)(functools.partial(tail, 1))


def _moe(h3, bucket, rank, counts, w_lo, w_hi, w1, w3, w2, ln_g, ln_b):
    t, d = h3.shape[0], D_MODEL
    tm = MOE_TM
    assert tm % MOE_DMA_GROUPS == 0
    src, dst, wl, wh, ea, eb, nvalid, n_tiles = _moe_plan(bucket, rank, counts, w_lo, w_hi, t)
    wspec = lambda shape, which: pl.BlockSpec(
        shape, (lambda i, ea_r, eb_r, nv_r: (ea_r[i], 0, 0)) if which == 0
        else (lambda i, ea_r, eb_r, nv_r: (eb_r[i], 0, 0)))
    w13 = (1, d, D_EXPERT)
    w2s = (1, D_EXPERT, d)
    smem_tile = lambda fn: pl.BlockSpec((1, 1, tm), lambda i, *_: (fn(i), 0, 0), memory_space=pltpu.SMEM)
    grid_spec = pltpu.PrefetchScalarGridSpec(
        num_scalar_prefetch=3,
        grid=(n_tiles,),
        in_specs=[smem_tile(lambda i: 0),
                  smem_tile(lambda i: jnp.minimum(i + 1, n_tiles - 1)),
                  smem_tile(lambda i: jnp.maximum(i - 1, 0)),
                  pl.BlockSpec((tm, 1), lambda i, *_: (i, 0)),
                  pl.BlockSpec((tm, 1), lambda i, *_: (i, 0)),
                  pl.BlockSpec(memory_space=pl.ANY),
                  wspec(w13, 0), wspec(w13, 0), wspec(w2s, 0),
                  wspec(w13, 1), wspec(w13, 1), wspec(w2s, 1),
                  pl.BlockSpec((1, d), lambda i, *_: (0, 0)),
                  pl.BlockSpec((1, d), lambda i, *_: (0, 0))],
        out_specs=pl.BlockSpec(memory_space=pl.ANY),
        scratch_shapes=[pltpu.VMEM((tm,) + ROW_TILE, F32)] * 4
        + [pltpu.SemaphoreType.DMA((2,)), pltpu.SemaphoreType.DMA((2,))],
    )
    w1b, w3b, w2b = w1.astype(BF16), w3.astype(BF16), w2.astype(BF16)
    return pl.pallas_call(
        _moe_kernel,
        out_shape=jax.ShapeDtypeStruct((t + 2 * tm,) + ROW_TILE, F32),
        grid_spec=grid_spec,
        compiler_params=_cparams(("arbitrary",)),
        name="moe",
    )(ea, eb, nvalid, src, src, dst, wl, wh, h3, w1b, w3b, w2b, w1b, w3b, w2b,
      ln_g.reshape(1, d), ln_b.reshape(1, d))


def kernel(x, ln0_g, ln0_b, w_in, pool_w, pool_scale, cmp_k_pe, cmp_k_w1, cmp_k_w2, cmp_v_pe, cmp_v_w1, cmp_v_w2, rel_bias, conv_w, conv_b, dt_bias, a_log, d_skip, ssd_norm_g, w_br_pool, w_br_nsa, w_br_ssd, w_out, ln1_g, ln1_b, router_w, router_b, exp_w1, exp_w3, exp_w2, ln2_g, ln2_b):
    bsz, s, d = x.shape
    assert d == D_MODEL and s % MIX_TS == 0 and s // CMP_STRIDE == LANE and (bsz * s) % PROJ_TM == 0
    t = bsz * s
    ts, ws, bc = _nsa_bias_tables(rel_bias)
    et, ovt = _nsa_consts(s)
    h = _ln(x.reshape(t, d), ln0_g, ln0_b)
    for i in range(DEPTH):
        proj = _proj(h, _prep_w_in(w_in[i]), t)
        kc, vc = _compress(proj, bsz, s, cmp_k_pe[i], cmp_k_w1[i], cmp_k_w2[i],
                           cmp_v_pe[i], cmp_v_w1[i], cmp_v_w2[i])
        nsa_out = _nsa(proj, kc, vc, ts, ws, bc, et, ovt, bsz, s)
        ssd_y = _ssd(proj, bsz, s, conv_w[i], conv_b[i], dt_bias[i], a_log[i], d_skip[i], ssd_norm_g[i])
        h1 = _mix(proj, nsa_out, ssd_y, h, bsz, s, pool_w[i], pool_scale[i], w_br_pool[i], w_br_nsa[i],
                  w_br_ssd[i], w_out[i], ln1_g[i], ln1_b[i])
        bucket, w_lo, w_hi, rank, cnt = _router(h1, router_w, router_b)
        counts = cnt[:N_BUCKETS, 0].astype(jnp.int32)
        h = _moe(h1, bucket[0], rank[0], counts, w_lo[0], w_hi[0], exp_w1[i], exp_w3[i], exp_w2[i],
                 ln2_g[i], ln2_b[i])
    return h[:t].reshape(bsz, s, d)
```

```python
import functools
import math

import numpy as np
import jax
import jax.numpy as jnp
from jax import lax
from jax.experimental import pallas as pl
from jax.experimental.pallas import tpu as pltpu

F32 = jnp.float32
BF16 = jnp.bfloat16

D_MODEL = 1024
DEPTH = 2
DN_ALPHA = (2.0 * DEPTH) ** 0.25
LN_EPS = 1e-5
NEG = -1e30
BIG = 1e6

POOL_WINDOWS = (2, 4, 8, 16)
POOL_GROUP = 128
POOL_WIDTH = 512
MAX_POOL_WINDOW = 16

NSA_HEAD_DIM = 64
NSA_KV_HEADS = 2
NSA_HPG = 4
NSA_HEADS = 8
NSA_WIDTH = 512
CMP_LEN = 32
CMP_STRIDE = 16
CMP_HIDDEN = 128
SLC_LEN = 64
SLC_TOPN = 8
WIN_LEN = 512
Q_BLOCK = 128
REL_BUCKETS = 32
REL_MAX_DIST = 128

SSD_D_INNER = 1024
SSD_HEAD_DIM = 64
SSD_HEADS = 16
SSD_GROUPS = 4
SSD_STATE = 128
SSD_CONV = 4
SSD_CHUNK = 128
SSD_XBC = 2048

N_EXPERTS = 16
N_EXPERT_GROUPS = 4
EXPERTS_PER_GROUP = 4
D_EXPERT = 512
PAIRS = ((0, 1), (0, 2), (0, 3), (1, 2), (1, 3), (2, 3))
N_BUCKETS = N_EXPERT_GROUPS * len(PAIRS)

LANE = 128
SUBLANE = 8
VMEM_LIMIT = 48 * 1024 * 1024

C_POOL = 0
C_Q = 512
C_KV = 1024
C_NG = 1792
C_DT = 1920
C_XBC = 2048
C_Z = 4096
C_GATE = 5120
D_PAD = 8192

PROJ_TM = 1024
PROJ_TN = 2048
MIX_TS = 256
MOE_TM = 256


def _dot(a, b):
    return jnp.dot(a, b, preferred_element_type=F32)


def _dot_nt(a, b):
    return lax.dot_general(a, b, (((1,), (1,)), ((), ())), preferred_element_type=F32)


def _dot_tn(a, b):
    return lax.dot_general(a, b, (((0,), (0,)), ((), ())), preferred_element_type=F32)


def _split3(x):
    x1 = x.astype(BF16)
    r1 = x - x1.astype(F32)
    x2 = r1.astype(BF16)
    x3 = (r1 - x2.astype(F32)).astype(BF16)
    return x1, x2, x3


def _layer_norm(x, g, b):
    mu = jnp.mean(x, -1, keepdims=True)
    xc = x - mu
    var = jnp.mean(xc * xc, -1, keepdims=True)
    return xc * lax.rsqrt(var + LN_EPS) * g + b


def _cparams(sem):
    return pltpu.CompilerParams(dimension_semantics=sem, vmem_limit_bytes=VMEM_LIMIT)


ROW_TILE = (D_MODEL,)


def _load_rows(ref):
    return ref[...]


def _store_rows(ref, val):
    ref[...] = val


def _row_spec(n, index):
    return pl.BlockSpec((n,) + ROW_TILE, lambda *ids: (index(*ids), 0))


def _proj_kernel(h_ref, w_ref, o_ref, hb_ref):
    @pl.when(pl.program_id(1) == 0)
    def _():
        hb_ref[...] = _load_rows(h_ref).astype(BF16)

    o_ref[...] = _dot(hb_ref[...], w_ref[...])


def _ln_proj_kernel(x_ref, g_ref, b_ref, w_ref, o_ref, h_ref, hb_ref):
    @pl.when(pl.program_id(1) == 0)
    def _():
        h = _layer_norm(x_ref[...], g_ref[...], b_ref[...])
        _store_rows(h_ref, h)
        hb_ref[...] = h.astype(BF16)

    o_ref[...] = _dot(hb_ref[...], w_ref[...])


def _proj(h2d, w_pad, t, ln=None):
    d = D_MODEL
    grid = (t // PROJ_TM, D_PAD // PROJ_TN)
    w_spec = pl.BlockSpec((d, PROJ_TN), lambda i, j: (0, j))
    o_spec = pl.BlockSpec((PROJ_TM, PROJ_TN), lambda i, j: (i, j))
    o_shape = jax.ShapeDtypeStruct((t, D_PAD), F32)
    scratch = [pltpu.VMEM((PROJ_TM, d), BF16)]
    params = _cparams(("parallel", "arbitrary"))
    if ln is None:
        return pl.pallas_call(
            _proj_kernel, out_shape=o_shape, grid=grid,
            in_specs=[_row_spec(PROJ_TM, lambda i, j: i), w_spec], out_specs=o_spec,
            scratch_shapes=scratch, compiler_params=params, name="proj",
        )(h2d, w_pad)
    vec = pl.BlockSpec((1, d), lambda i, j: (0, 0))
    return pl.pallas_call(
        _ln_proj_kernel, out_shape=(o_shape, jax.ShapeDtypeStruct((t,) + ROW_TILE, F32)), grid=grid,
        in_specs=[_row_spec(PROJ_TM, lambda i, j: i), vec, vec, w_spec],
        out_specs=(o_spec, _row_spec(PROJ_TM, lambda i, j: i)),
        scratch_shapes=scratch, compiler_params=params, name="ln_proj",
    )(h2d, ln[0].reshape(1, d), ln[1].reshape(1, d), w_pad)


def _pad_cols(w, n):
    return jnp.pad(w, ((0, 0), (0, n - w.shape[1])))


def _prep_w_in(w_in):
    d = w_in.shape[0]
    o = 0
    w_pool = w_in[:, o:o + POOL_WIDTH]; o += POOL_WIDTH
    w_q = w_in[:, o:o + NSA_WIDTH]; o += NSA_WIDTH
    w_kv = w_in[:, o:o + 768]; o += 768
    w_ng = w_in[:, o:o + 24]; o += 24
    w_z = w_in[:, o:o + SSD_D_INNER]; o += SSD_D_INNER
    w_xbc = w_in[:, o:o + SSD_XBC]; o += SSD_XBC
    w_dt = w_in[:, o:o + SSD_HEADS]; o += SSD_HEADS
    w_gate = w_in[:, o:o + 3 * D_MODEL]
    w_q = w_q.reshape(d, NSA_KV_HEADS, NSA_HPG, NSA_HEAD_DIM).transpose(0, 2, 1, 3).reshape(d, NSA_WIDTH)
    w = jnp.concatenate([w_pool, w_q, w_kv, _pad_cols(w_ng, LANE), _pad_cols(w_dt, LANE),
                         w_xbc, w_z, w_gate], axis=1)
    return w.astype(BF16)


def _compress_one(tok_ref, pe_ref, w1_ref, w2_ref):
    n16 = tok_ref.shape[0] // CMP_STRIDE
    acc_a = jnp.zeros((n16, 2 * CMP_HIDDEN), F32)
    acc_b = jnp.zeros((n16, 2 * CMP_HIDDEN), F32)
    for l in range(CMP_STRIDE):
        rows = tok_ref[pl.ds(l, n16, stride=CMP_STRIDE), :]
        xa = (rows + pe_ref[l:l + 1, :]).astype(BF16)
        xb = (rows + pe_ref[CMP_STRIDE + l:CMP_STRIDE + l + 1, :]).astype(BF16)
        acc_a = acc_a + _dot(xa, w1_ref[l])
        acc_b = acc_b + _dot(xb, w1_ref[CMP_STRIDE + l])
    hid = acc_a + pltpu.roll(acc_b, n16 - 1, axis=0)
    hid = jax.nn.gelu(hid)
    return _dot(hid.astype(BF16), w2_ref[...])


def _compress_kernel(kt_ref, vt_ref, pek_ref, w1k_ref, w2k_ref, pev_ref, w1v_ref, w2v_ref,
                     kc_ref, vc_ref):
    kc_ref[0] = _compress_one(kt_ref, pek_ref, w1k_ref, w2k_ref).astype(BF16)
    vc_ref[0] = _compress_one(vt_ref, pev_ref, w1v_ref, w2v_ref).T.astype(BF16)


def _blockdiag2(w):
    z = jnp.zeros_like(w)
    return jnp.concatenate([jnp.concatenate([w, z], -1), jnp.concatenate([z, w], -1)], -2)


def _compress(proj2d, bsz, s, pe_k, w1_k, w2_k, pe_v, w1_v, w2_v):
    n16 = s // CMP_STRIDE

    def prep(pe, w1, w2):
        pe2 = jnp.concatenate([pe, pe], axis=-1)
        w1b = _blockdiag2(w1.reshape(CMP_LEN, NSA_HEAD_DIM, CMP_HIDDEN)).astype(BF16)
        w2b = _blockdiag2(w2).astype(BF16)
        return pe2, w1b, w2b

    pk = prep(pe_k, w1_k, w2_k)
    pv = prep(pe_v, w1_v, w2_v)
    full = lambda shape: pl.BlockSpec(shape, lambda b: (0,) * len(shape))
    wspecs = [full((CMP_LEN, LANE)), full((CMP_LEN, LANE, 2 * CMP_HIDDEN)), full((2 * CMP_HIDDEN, LANE))]
    return pl.pallas_call(
        _compress_kernel,
        out_shape=(jax.ShapeDtypeStruct((bsz, n16, LANE), BF16),
                   jax.ShapeDtypeStruct((bsz, n16, LANE), BF16)),
        grid=(bsz,),
        in_specs=[pl.BlockSpec((s, LANE), lambda b: (b, C_KV // LANE)),
                  pl.BlockSpec((s, LANE), lambda b: (b, C_KV // LANE + 1))] + wspecs + wspecs,
        out_specs=(pl.BlockSpec((1, n16, LANE), lambda b: (b, 0, 0)),
                   pl.BlockSpec((1, n16, LANE), lambda b: (b, 0, 0))),
        compiler_params=_cparams(("parallel",)),
        name="nsa_compress",
    )(proj2d, proj2d, *pk, *pv)


def _rel_bucket(dist):
    n = jnp.maximum(dist, 0)
    max_exact = REL_BUCKETS // 2
    nf = jnp.maximum(n, 1).astype(F32)
    large = max_exact + (jnp.log(nf / max_exact) / math.log(REL_MAX_DIST / max_exact)
                         * (REL_BUCKETS - max_exact)).astype(jnp.int32)
    large = jnp.minimum(large, REL_BUCKETS - 1)
    return jnp.where(n < max_exact, n, large)


LOG2E = math.log2(math.e)
WIN_BACK = WIN_LEN // Q_BLOCK
WIN_KEYS = (WIN_BACK + 1) * Q_BLOCK
WIN_STRIP_TILES = 2 * WIN_BACK + 1
SEL_TILES = 4
SEL_KEYS = SEL_TILES * Q_BLOCK
SEL_STRIP_TILES = 3 * SEL_TILES - 1
CMP_BIAS_ROWS = 2 * LANE
PV_ROWS = NSA_HEAD_DIM + 16


def _nsa_bias_tables(rel_bias):
    tbl = rel_bias.astype(F32)
    far = tbl[REL_BUCKETS - 1]
    hq = NSA_HPG * Q_BLOCK

    def lookup(dist):
        onehot = (_rel_bucket(jnp.asarray(dist))[..., None] == jnp.arange(REL_BUCKETS)).astype(F32)
        return jnp.einsum("...b,bh->...h", onehot, tbl, precision=lax.Precision.HIGHEST)

    k = np.arange(Q_BLOCK)[None, :, None]
    q = np.arange(Q_BLOCK)[None, None, :]

    def strip(first, last, visible):
        n = first - last + 1
        dist = np.arange(first, last - 1, -1)[:, None, None] * Q_BLOCK + q - k
        v = jnp.where(jnp.asarray(visible(dist))[..., None], (lookup(dist) - far) * LOG2E, NEG)
        v = v.reshape(n, Q_BLOCK, Q_BLOCK, NSA_KV_HEADS, NSA_HPG).transpose(3, 0, 1, 4, 2)
        return v.reshape(NSA_KV_HEADS, n * Q_BLOCK, hq)

    ts = strip(2 * SEL_TILES - 1, 1 - SEL_TILES, lambda d: d >= 0)
    ws = strip(WIN_BACK, -WIN_BACK, lambda d: (d >= 0) & (d < WIN_LEN))
    r = np.arange(CMP_BIAS_ROWS)[:, None]
    dist_c = q[0] - CMP_STRIDE * (r - LANE) - (CMP_LEN - 1)
    bc = jnp.where(jnp.asarray(dist_c >= 0)[..., None], lookup(dist_c) * LOG2E, NEG)
    bc = bc.reshape(CMP_BIAS_ROWS, Q_BLOCK, NSA_KV_HEADS, NSA_HPG).transpose(2, 0, 3, 1)
    bc = bc.reshape(NSA_KV_HEADS, CMP_BIAS_ROWS, hq)
    return ts, ws, bc


def _nsa_consts(s):
    n_slc = s // SLC_LEN
    et = (np.arange(s)[:, None] // SLC_LEN == np.arange(LANE)[None, :]).astype(np.float32)
    n_cmp = (s - CMP_LEN) // CMP_STRIDE + 1
    c0 = np.arange(n_cmp) * CMP_STRIDE
    s0 = np.arange(n_slc) * SLC_LEN
    ov = np.clip(np.minimum(c0[None, :] + CMP_LEN, s0[:, None] + SLC_LEN)
                 - np.maximum(c0[None, :], s0[:, None]), 0, None) / CMP_LEN
    ovt = np.zeros((n_slc, LANE), np.float32)
    ovt[:, :n_cmp] = ov
    return jnp.asarray(et, BF16), jnp.asarray(ovt, BF16)


def _nsa_kernel(q_ref, ng_ref, ks_ref, vs_ref, kw_ref, vw_ref, kc_ref, vct_ref,
                ts_ref, ws_ref, bc_ref, et_ref, ovt_ref, o_ref,
                ksb_ref, kwb_ref, vst_ref, vwt_ref, os_ref, *, nq, n_slc):
    qi = pl.program_id(1)
    hq = NSA_HPG * Q_BLOCK
    dh = NSA_HEAD_DIM
    groups = range(NSA_KV_HEADS)
    gsl = [slice(g * dh, (g + 1) * dh) for g in groups]

    @pl.when(qi == 0)
    def _():
        ksb_ref[:, 0:LANE] = ks_ref[...].astype(BF16)
        ksb_ref[:, LANE:] = et_ref[...]
        kwb_ref[...] = kw_ref[...].astype(BF16)
        ones_rows = jnp.where(lax.broadcasted_iota(jnp.int32, (PV_ROWS - dh, ks_ref.shape[0]), 0) == 0, 1.0, 0.0)
        for g in groups:
            vst_ref[g, dh:, :] = ones_rows.astype(BF16)
            vwt_ref[g, dh:, :] = ones_rows.astype(BF16)
        for kt in range(nq):
            sl = slice(kt * Q_BLOCK, (kt + 1) * Q_BLOCK)
            vs_t = vs_ref[sl, :].T
            vw_t = vw_ref[sl, :].T
            for g in groups:
                vst_ref[g, 0:dh, sl] = vs_t[gsl[g], :].astype(BF16)
                vwt_ref[g, 0:dh, sl] = vw_t[gsl[g], :].astype(BF16)

    row_lo = lax.broadcasted_iota(jnp.int32, (LANE, hq), 0) < dh

    def heads4(m):
        return jnp.concatenate([m] * NSA_HPG, axis=1)

    gates_t = jax.nn.sigmoid(ng_ref[...]).T
    q_t = jnp.concatenate(
        [(q_ref[:, h * LANE:(h + 1) * LANE] * (dh ** -0.5 * LOG2E)).T for h in range(NSA_HPG)], axis=1)
    qg_t = [jnp.where(row_lo if g == 0 else jnp.logical_not(row_lo), q_t, 0.0).astype(BF16) for g in groups]

    boff = pl.multiple_of(LANE - SUBLANE * qi, SUBLANE)
    jrow = lax.broadcasted_iota(jnp.int32, (n_slc, LANE), 0)
    tj = qi * Q_BLOCK + lax.broadcasted_iota(jnp.int32, (n_slc, LANE), 1)
    jt = lax.shift_right_logical(tj, SLC_LEN.bit_length() - 1)
    forced = (jrow == 0) | (jrow == jt) | (jrow == jt - 1)
    valid = jrow * SLC_LEN <= tj
    o_c = []
    q_aug = []
    for g in groups:
        s = _dot(kc_ref[0], qg_t[g]) + bc_ref[g, pl.ds(boff, LANE), :]
        mx = jnp.max(s, axis=0, keepdims=True)
        p = jnp.exp2(s - mx)
        lsum = jnp.sum(p, axis=0, keepdims=True)
        pb = (p * jnp.where(mx > 0.5 * NEG, 1.0 / lsum, 0.0)).astype(BF16)
        o_c.append(_dot(vct_ref[0, gsl[g], :], pb))
        imp4 = _dot(ovt_ref[...], pb)
        imp = imp4[:, 0:LANE]
        for h in range(1, NSA_HPG):
            imp = imp + imp4[:, h * LANE:(h + 1) * LANE]
        val = jnp.where(forced, BIG, jnp.where(valid, imp, -BIG))
        rank = jnp.zeros((n_slc, LANE), F32)
        for i in range(n_slc):
            ci = val[i:i + 1, :]
            rank = rank + jnp.where(ci > val, 1.0, 0.0) + jnp.where(ci == val, (jrow > i).astype(F32), 0.0)
        block_rows = heads4(jnp.where(rank < SLC_TOPN, 0.0, NEG)).astype(BF16)
        q_aug.append(jnp.concatenate(
            [qg_t[g], block_rows, jnp.zeros((LANE - n_slc, hq), BF16)], axis=0))

    n_super = qi // SEL_TILES + 1
    diag = qi - SEL_TILES * (n_super - 1)

    def selected(n):
        nk = n * SEL_KEYS
        lo = max(n - 2, 0) * SEL_KEYS
        toff = pl.multiple_of((2 * SEL_TILES - 1 - diag) * Q_BLOCK - (nk - lo - SEL_KEYS), Q_BLOCK)
        for g in groups:
            parts = [(lo, nk, _dot(ksb_ref[lo:nk, :], q_aug[g]) + ts_ref[g, pl.ds(toff, nk - lo), :])]
            if lo > 0:
                parts.append((0, lo, _dot(ksb_ref[0:lo, :], q_aug[g])))
            m = parts[0][2].max(axis=0, keepdims=True)
            for _, _, s in parts[1:]:
                m = jnp.maximum(m, s.max(axis=0, keepdims=True))
            acc = None
            for a, b, s in parts:
                pv = _dot(vst_ref[g, :, a:b], jnp.exp2(s - m).astype(BF16))
                acc = pv if acc is None else acc + pv
            os_ref[g] = acc[0:dh] * (1.0 / acc[dh:dh + 1])

    w0 = jnp.maximum(qi - WIN_BACK, 0)
    woff = pl.multiple_of(w0 * Q_BLOCK, Q_BLOCK)
    boff_w = pl.multiple_of((WIN_BACK - (qi - w0)) * Q_BLOCK, Q_BLOCK)
    kw_bf = kwb_ref[pl.ds(woff, WIN_KEYS), :]
    o_w = []
    for g in groups:
        s = _dot(kw_bf, qg_t[g]) + ws_ref[g, pl.ds(boff_w, WIN_KEYS), :]
        p = jnp.exp2(s - jnp.max(s, axis=0, keepdims=True)).astype(BF16)
        acc = _dot(vwt_ref[g, :, pl.ds(woff, WIN_KEYS)], p)
        o_w.append(acc[0:dh] * (1.0 / acc[dh:dh + 1]))

    for n in range(1, nq // SEL_TILES + 1):
        pl.when(n_super == n)(functools.partial(selected, n))
    o_s = [os_ref[g] for g in groups]

    for h in range(NSA_HPG):
        hl = slice(h * LANE, (h + 1) * LANE)
        halves = []
        for g in groups:
            c = g * NSA_HPG + h
            halves.append(gates_t[c:c + 1, :] * o_c[g][:, hl]
                          + gates_t[NSA_HEADS + c:NSA_HEADS + c + 1, :] * o_s[g][:, hl]
                          + gates_t[2 * NSA_HEADS + c:2 * NSA_HEADS + c + 1, :] * o_w[g][:, hl])
        o_ref[:, hl] = jnp.concatenate(halves, axis=0).T


def _nsa(proj2d, kc, vct, ts, ws, bc, et, ovt, bsz, s):
    nq = s // Q_BLOCK
    n_slc = s // SLC_LEN
    hq = NSA_HPG * Q_BLOCK
    kvb = C_KV // LANE
    kv_spec = lambda blk: pl.BlockSpec((s, LANE), lambda b, qi: (b, kvb + blk))
    full = lambda shape: pl.BlockSpec(shape, lambda b, qi: (0,) * len(shape), pipeline_mode=pl.Buffered(1))
    return pl.pallas_call(
        functools.partial(_nsa_kernel, nq=nq, n_slc=n_slc),
        out_shape=jax.ShapeDtypeStruct((bsz * s, NSA_WIDTH), F32),
        grid=(bsz, nq),
        in_specs=[pl.BlockSpec((Q_BLOCK, NSA_WIDTH), lambda b, qi: (b * nq + qi, C_Q // NSA_WIDTH)),
                  pl.BlockSpec((Q_BLOCK, LANE), lambda b, qi: (b * nq + qi, C_NG // LANE)),
                  kv_spec(2), kv_spec(3), kv_spec(4), kv_spec(5),
                  pl.BlockSpec((1, s // CMP_STRIDE, LANE), lambda b, qi: (b, 0, 0)),
                  pl.BlockSpec((1, LANE, s // CMP_STRIDE), lambda b, qi: (b, 0, 0)),
                  full((NSA_KV_HEADS, SEL_STRIP_TILES * Q_BLOCK, hq)),
                  full((NSA_KV_HEADS, WIN_STRIP_TILES * Q_BLOCK, hq)),
                  full((NSA_KV_HEADS, CMP_BIAS_ROWS, hq)),
                  full((s, LANE)),
                  full((n_slc, LANE))],
        out_specs=pl.BlockSpec((Q_BLOCK, NSA_WIDTH), lambda b, qi: (b * nq + qi, 0)),
        scratch_shapes=[pltpu.VMEM((s, 2 * LANE), BF16), pltpu.VMEM((s, LANE), BF16),
                        pltpu.VMEM((NSA_KV_HEADS, PV_ROWS, s), BF16), pltpu.VMEM((NSA_KV_HEADS, PV_ROWS, s), BF16),
                        pltpu.VMEM((NSA_KV_HEADS, NSA_HEAD_DIM, hq), F32)],
        compiler_params=_cparams(("parallel", "arbitrary")),
        name="nsa_attn",
    )(proj2d, proj2d, proj2d, proj2d, proj2d, proj2d, kc, vct, ts, ws, bc, et, ovt)


def _ssd_kernel(xbc_ref, prev_ref, z_ref, dt_ref, cw_ref, cb_ref, dtb_ref, alog_ref,
                dskip_ref, ng_ref, e16_ref, tril_ref, o_ref, state_ref):
    c = pl.program_id(1)
    L = SSD_CHUNK
    pair_w = 2 * SSD_HEAD_DIM

    @pl.when(c == 0)
    def _():
        state_ref[...] = jnp.zeros(state_ref.shape, F32)

    xcat = jnp.concatenate([jnp.where(c > 0, prev_ref[...], 0.0), xbc_ref[...]], axis=0)
    acc = cw_ref[0:1, :] * xcat
    for k in range(1, SSD_CONV):
        acc = pltpu.roll(acc, 1, axis=0) + cw_ref[k:k + 1, :] * xcat
    xbc = jax.nn.silu(acc[SUBLANE:, :] + cb_ref[...])
    xs = xbc[:, :SSD_D_INNER]
    bmat = xbc[:, SSD_D_INNER:SSD_D_INNER + SSD_GROUPS * SSD_STATE]
    cmat = xbc[:, SSD_D_INNER + SSD_GROUPS * SSD_STATE:]

    dt = jax.nn.softplus(dt_ref[...] + dtb_ref[...])
    tril = tril_ref[...]
    e16 = e16_ref[...]
    d1, d2, d3 = _split3(dt)
    dt_x = _dot(d1, e16) + _dot(d2, e16) + _dot(d3, e16)
    a_c = dt * (-jnp.exp(alog_ref[...]))
    c1, c2, c3 = _split3(a_c)
    acs = _dot(tril, c1) + _dot(tril, c2) + _dot(tril, c3)
    x1, x2, x3 = _split3(acs)
    acs_x = _dot(x1, e16) + _dot(x2, e16) + _dot(x3, e16)
    acs_t = acs.T
    last = acs_x[L - 1:L, :]
    eacs = jnp.exp(acs_x)
    decay_in = jnp.exp(last - acs_x)
    cdec = jnp.exp(last)
    xdt = xs * dt_x
    wst = xdt * decay_in

    row = lax.broadcasted_iota(jnp.int32, (L, L), 0)
    lane = lax.broadcasted_iota(jnp.int32, (L, L), 1)
    tri = row >= lane
    lo = lane < SSD_HEAD_DIM

    ys = []
    for g in range(SSD_GROUPS):
        cg = cmat[:, g * SSD_STATE:(g + 1) * SSD_STATE].astype(BF16)
        bg = bmat[:, g * SSD_STATE:(g + 1) * SSD_STATE].astype(BF16)
        cb = _dot_nt(cg, bg)
        for pr in range(2):
            pp = g * 2 + pr
            ls = slice(pp * pair_w, (pp + 1) * pair_w)
            xp = xdt[:, ls]
            yd = None
            for e in range(2):
                hd = 2 * pp + e
                diff = acs[:, hd:hd + 1] - acs_t[hd:hd + 1, :]
                seg = jnp.exp(jnp.where(tri, diff, NEG))
                pm = (cb * seg).astype(BF16)
                xe = jnp.where(lo if e == 0 else jnp.logical_not(lo), xp, 0.0).astype(BF16)
                term = _dot(pm, xe)
                yd = term if yd is None else yd + term
            st = state_ref[pp]
            yo = _dot(cg, st.astype(BF16)) * eacs[:, ls]
            ys.append(yd + yo)
            state_ref[pp] = cdec[:, ls] * st + _dot_tn(bg, wst[:, ls].astype(BF16))

    y = jnp.concatenate(ys, axis=1) + xs * dskip_ref[...]
    y = y * jax.nn.silu(z_ref[...])
    gw = SSD_D_INNER // SSD_GROUPS
    outs = []
    for gi in range(SSD_GROUPS):
        yg = y[:, gi * gw:(gi + 1) * gw]
        outs.append(yg * lax.rsqrt(jnp.mean(yg * yg, -1, keepdims=True) + LN_EPS))
    o_ref[...] = jnp.concatenate(outs, axis=1) * ng_ref[...]


def _ssd(proj2d, bsz, s, conv_w, conv_b, dt_bias, a_log, d_skip, norm_g):
    nc = s // SSD_CHUNK
    L = SSD_CHUNK
    pad16 = lambda v: jnp.pad(v.reshape(1, SSD_HEADS), ((0, 0), (0, LANE - SSD_HEADS)))
    rep64 = lambda v: jnp.repeat(v, SSD_HEAD_DIM).reshape(1, SSD_D_INNER)
    e16 = np.zeros((LANE, SSD_D_INNER), np.float32)
    e16[np.arange(SSD_D_INNER) // SSD_HEAD_DIM, np.arange(SSD_D_INNER)] = 1.0
    tril = np.tril(np.ones((L, L), np.float32))
    full = lambda shape: pl.BlockSpec(shape, lambda b, c: (0,) * len(shape))
    rows8 = s // SUBLANE
    return pl.pallas_call(
        _ssd_kernel,
        out_shape=jax.ShapeDtypeStruct((bsz * s, SSD_D_INNER), F32),
        grid=(bsz, nc),
        in_specs=[pl.BlockSpec((L, SSD_XBC), lambda b, c: (b * nc + c, C_XBC // SSD_XBC)),
                  pl.BlockSpec((SUBLANE, SSD_XBC),
                               lambda b, c: (jnp.maximum(b * rows8 + c * (L // SUBLANE) - 1, 0), C_XBC // SSD_XBC)),
                  pl.BlockSpec((L, SSD_D_INNER), lambda b, c: (b * nc + c, C_Z // SSD_D_INNER)),
                  pl.BlockSpec((L, LANE), lambda b, c: (b * nc + c, C_DT // LANE)),
                  full((SSD_CONV, SSD_XBC)), full((1, SSD_XBC)), full((1, LANE)), full((1, LANE)),
                  full((1, SSD_D_INNER)), full((1, SSD_D_INNER)),
                  full((LANE, SSD_D_INNER)), full((L, L))],
        out_specs=pl.BlockSpec((L, SSD_D_INNER), lambda b, c: (b * nc + c, 0)),
        scratch_shapes=[pltpu.VMEM((SSD_HEADS // 2, SSD_STATE, 2 * SSD_HEAD_DIM), F32)],
        compiler_params=_cparams(("parallel", "arbitrary")),
        name="ssd",
    )(proj2d, proj2d, proj2d, proj2d, conv_w, conv_b.reshape(1, SSD_XBC), pad16(dt_bias), pad16(a_log),
      rep64(d_skip), norm_g.reshape(1, SSD_D_INNER),
      jnp.asarray(e16, BF16), jnp.asarray(tril, BF16))


def _mix_kernel(up_ref, upprev_ref, nsa_ref, ssd_ref, ga_ref, gb_ref, gc_ref, h_ref,
                pw_ref, ps_ref, wbp_ref, wbn_ref, wbs_ref, wo_ref, lg_ref, lb_ref, o_ref, *, ts):
    si = pl.program_id(1)
    w0 = MAX_POOL_WINDOW
    prev = jnp.where(si > 0, upprev_ref[...], 0.0)
    xcat = jnp.concatenate([prev, up_ref[...]], axis=0)
    tpos = si * ts + lax.broadcasted_iota(jnp.int32, (ts, POOL_GROUP), 0)
    ygs = []
    for gi, w in enumerate(POOL_WINDOWS):
        x = xcat[:, gi * POOL_GROUP:(gi + 1) * POOL_GROUP]
        acc = x
        d = 1
        while d < w:
            acc = acc + pltpu.roll(acc, d, axis=0)
            d *= 2
        cnt = jnp.minimum(tpos + 1, w).astype(F32)
        r = acc[w0:, :] / cnt - x[w0:, :]
        ygs.append(_dot(r.astype(BF16), pw_ref[gi]))
    y = jnp.concatenate(ygs, axis=1) * ps_ref[...]
    br_a = _dot(y.astype(BF16), wbp_ref[...])
    br_b = _dot(nsa_ref[...].astype(BF16), wbn_ref[...])
    br_c = _dot(ssd_ref[...].astype(BF16), wbs_ref[...])
    mixed = (jax.nn.sigmoid(ga_ref[...]) * br_a + jax.nn.sigmoid(gb_ref[...]) * br_b
             + jax.nn.sigmoid(gc_ref[...]) * br_c)
    mix = _dot(mixed.astype(BF16), wo_ref[...])
    _store_rows(o_ref, _layer_norm(DN_ALPHA * _load_rows(h_ref) + mix, lg_ref[...], lb_ref[...]))


def _mix(proj2d, nsa_out, ssd_y, h3, bsz, s, pool_w, pool_scale, w_br_pool, w_br_nsa, w_br_ssd,
         w_out, ln_g, ln_b):
    ts = MIX_TS
    ns = s // ts
    t = bsz * s
    d = D_MODEL
    w0 = MAX_POOL_WINDOW
    wbn = w_br_nsa.reshape(NSA_KV_HEADS, NSA_HPG, NSA_HEAD_DIM, d).transpose(1, 0, 2, 3).reshape(NSA_WIDTH, d)
    full = lambda shape: pl.BlockSpec(shape, lambda b, si: (0,) * len(shape))
    row = lambda width, cb: pl.BlockSpec((ts, width), lambda b, si: (b * ns + si, cb))
    return pl.pallas_call(
        functools.partial(_mix_kernel, ts=ts),
        out_shape=jax.ShapeDtypeStruct((t,) + ROW_TILE, F32),
        grid=(bsz, ns),
        in_specs=[row(POOL_WIDTH, C_POOL // POOL_WIDTH),
                  pl.BlockSpec((w0, POOL_WIDTH),
                               lambda b, si: (jnp.maximum((b * s + si * ts) // w0 - 1, 0), C_POOL // POOL_WIDTH)),
                  row(NSA_WIDTH, 0), row(d, 0),
                  row(d, C_GATE // d), row(d, C_GATE // d + 1), row(d, C_GATE // d + 2),
                  _row_spec(ts, lambda b, si: b * ns + si),
                  full((len(POOL_WINDOWS), POOL_GROUP, POOL_GROUP)), full((1, POOL_WIDTH)),
                  full((POOL_WIDTH, d)), full((NSA_WIDTH, d)), full((d, d)), full((d, d)),
                  full((1, d)), full((1, d))],
        out_specs=_row_spec(ts, lambda b, si: b * ns + si),
        compiler_params=_cparams(("parallel", "arbitrary")),
        name="mix",
    )(proj2d, proj2d, nsa_out, ssd_y, proj2d, proj2d, proj2d, h3,
      pool_w.astype(BF16), pool_scale.reshape(1, POOL_WIDTH), w_br_pool.astype(BF16), wbn.astype(BF16),
      w_br_ssd.astype(BF16), w_out.astype(BF16), ln_g.reshape(1, d), ln_b.reshape(1, d))


RANK_CHUNK = 256
BUCKET_ROWS = 32


def _route(hb, first_step, rw_ref, rb_ref, tri_ref, bucket_ref, wlo_ref, whi_ref, rank_ref, cnt_ref, run_ref):
    @pl.when(first_step)
    def _():
        run_ref[...] = jnp.zeros(run_ref.shape, F32)

    logits = _dot_nt(rw_ref[...], hb)
    lg = [logits[e:e + 1, :] for e in range(N_EXPERTS)]
    mx = lg[0]
    for e in range(1, N_EXPERTS):
        mx = jnp.maximum(mx, lg[e])
    ex = [jnp.exp(v - mx) for v in lg]
    den = ex[0]
    for e in range(1, N_EXPERTS):
        den = den + ex[e]
    probs = [v / den for v in ex]
    score = [probs[e] + rb_ref[e:e + 1, :] for e in range(N_EXPERTS)]

    def group_vals(vals, grp):
        out = []
        for k in range(EXPERTS_PER_GROUP):
            v = vals[k]
            for gi in range(1, N_EXPERT_GROUPS):
                v = jnp.where(grp == gi, vals[gi * EXPERTS_PER_GROUP + k], v)
            out.append(v)
        return out

    gscore = []
    for gi in range(N_EXPERT_GROUPS):
        sc = score[gi * EXPERTS_PER_GROUP:(gi + 1) * EXPERTS_PER_GROUP]
        best = None
        for a, b in PAIRS:
            v = sc[a] + sc[b]
            best = v if best is None else jnp.maximum(best, v)
        gscore.append(best)
    grp = jnp.zeros(gscore[0].shape, jnp.int32)
    best = gscore[0]
    for gi in range(1, N_EXPERT_GROUPS):
        better = gscore[gi] > best
        grp = jnp.where(better, gi, grp)
        best = jnp.where(better, gscore[gi], best)

    sc = group_vals(score, grp)
    pr = group_vals(probs, grp)
    first = jnp.zeros(grp.shape, jnp.int32)
    fv = sc[0]
    for k in range(1, EXPERTS_PER_GROUP):
        better = sc[k] > fv
        first = jnp.where(better, k, first)
        fv = jnp.where(better, sc[k], fv)
    second = jnp.full(grp.shape, -1, jnp.int32)
    sv = jnp.full(fv.shape, -jnp.inf, F32)
    for k in range(EXPERTS_PER_GROUP):
        better = (first != k) & ((sc[k] > sv) | (second < 0))
        second = jnp.where(better, k, second)
        sv = jnp.where(better, sc[k], sv)
    lo = jnp.minimum(first, second)
    hi = jnp.maximum(first, second)
    p_lo = pr[0]
    p_hi = pr[0]
    for k in range(1, EXPERTS_PER_GROUP):
        p_lo = jnp.where(lo == k, pr[k], p_lo)
        p_hi = jnp.where(hi == k, pr[k], p_hi)
    tot = p_lo + p_hi
    pair = jnp.zeros(grp.shape, jnp.int32)
    for pi, (a, b) in enumerate(PAIRS):
        pair = jnp.where((lo == a) & (hi == b), pi, pair)
    bucket = grp * len(PAIRS) + pair
    bucket_ref[...] = bucket
    wlo_ref[...] = p_lo / tot
    whi_ref[...] = p_hi / tot

    tm = bucket.shape[1]
    onehot = jnp.where(lax.broadcasted_iota(jnp.int32, (BUCKET_ROWS, tm), 0) == bucket, 1.0, 0.0)
    run = run_ref[...]
    ranks = []
    for c in range(tm // RANK_CHUNK):
        oc = onehot[:, c * RANK_CHUNK:(c + 1) * RANK_CHUNK]
        before = _dot(oc.astype(BF16), tri_ref[...]) + jnp.concatenate([run] * (RANK_CHUNK // LANE), axis=1)
        ranks.append(jnp.sum(oc * before, axis=0, keepdims=True))
        run = run + jnp.broadcast_to(jnp.sum(oc, axis=1, keepdims=True), run.shape)
    rank_ref[...] = jnp.concatenate(ranks, axis=1).astype(jnp.int32)
    run_ref[...] = run
    cnt_ref[...] = run


ROUTER_TM = 1024


def _router_kernel(h_ref, *refs):
    _route(_load_rows(h_ref).astype(BF16), pl.program_id(0) == 0, *refs)


def _router(h2d, router_w, router_b):
    t, d = h2d.shape[0], D_MODEL
    tm = ROUTER_TM
    rw = jnp.pad(router_w.T, ((0, LANE - N_EXPERTS), (0, 0))).astype(BF16)
    rb = jnp.broadcast_to(jnp.pad(router_b, (0, LANE - N_EXPERTS))[:, None], (LANE, tm))
    tri = jnp.asarray(np.triu(np.ones((RANK_CHUNK, RANK_CHUNK), np.float32), 1), BF16)
    vec = lambda dt: jax.ShapeDtypeStruct((1, t), dt)
    row = pl.BlockSpec((1, tm), lambda i: (0, i))
    return pl.pallas_call(
        _router_kernel,
        out_shape=(vec(jnp.int32), vec(F32), vec(F32), vec(jnp.int32),
                   jax.ShapeDtypeStruct((BUCKET_ROWS, LANE), F32)),
        grid=(t // tm,),
        in_specs=[_row_spec(tm, lambda i: i),
                  pl.BlockSpec((LANE, d), lambda i: (0, 0)),
                  pl.BlockSpec((LANE, tm), lambda i: (0, 0)),
                  pl.BlockSpec((RANK_CHUNK, RANK_CHUNK), lambda i: (0, 0))],
        out_specs=(row, row, row, row, pl.BlockSpec((BUCKET_ROWS, LANE), lambda i: (0, 0))),
        scratch_shapes=[pltpu.VMEM((BUCKET_ROWS, LANE), F32)],
        compiler_params=_cparams(("arbitrary",)),
        name="router",
    )(h2d, rw, rb, tri)


def _moe_plan(bucket, rank, counts, w_lo, w_hi, t):
    tm = MOE_TM
    n_tiles = t // tm + N_BUCKETS
    p_rows = n_tiles * tm
    tiles_per = (counts + tm - 1) // tm
    tile_end = jnp.cumsum(tiles_per)
    tile_start = tile_end - tiles_per
    in_bucket = bucket[:, None] == jnp.arange(N_BUCKETS)[None, :]
    dest = jnp.sum(jnp.where(in_bucket, (tile_start * tm)[None, :], 0), axis=1) + rank
    payload = jnp.stack([jnp.arange(t, dtype=jnp.int32), lax.bitcast_convert_type(w_lo, jnp.int32),
                         lax.bitcast_convert_type(w_hi, jnp.int32), jnp.ones((t,), jnp.int32)], axis=1)
    plan = jnp.zeros((p_rows, 4), jnp.int32).at[dest].set(payload, unique_indices=True)
    src = plan[:, 0]
    wl = lax.bitcast_convert_type(plan[:, 1], F32)
    wh = lax.bitcast_convert_type(plan[:, 2], F32)
    prow = jnp.arange(p_rows, dtype=jnp.int32)
    spare = t + ((prow // tm) % 2) * tm + prow % tm
    dst = jnp.where(plan[:, 3] > 0, src, spare)
    n_used = tile_end[-1]
    tile_ids = jnp.arange(n_tiles)
    tb = jnp.sum((tile_end[None, :] <= tile_ids[:, None]).astype(jnp.int32), axis=1)
    tb = jnp.where(tile_ids < n_used, tb, tb[jnp.maximum(n_used - 1, 0)])
    tb = jnp.minimum(tb, N_BUCKETS - 1)
    pairs = np.asarray(PAIRS, np.int32)
    grp = tb // len(PAIRS)
    ea = grp * EXPERTS_PER_GROUP + jnp.asarray(pairs[:, 0])[tb % len(PAIRS)]
    eb = grp * EXPERTS_PER_GROUP + jnp.asarray(pairs[:, 1])[tb % len(PAIRS)]
    nvalid = jnp.clip(counts[tb] - (tile_ids - tile_start[tb]) * tm, 0, tm)
    nvalid = jnp.where(tile_ids < n_used, nvalid, 0).astype(jnp.int32)
    return (src.reshape(n_tiles, 1, tm), dst.reshape(n_tiles, 1, tm), wl.reshape(p_rows, 1), wh.reshape(p_rows, 1),
            ea.astype(jnp.int32), eb.astype(jnp.int32), nvalid, n_tiles)


MOE_DMA_GROUPS = 8


def _moe_kernel(ea_ref, eb_ref, nv_ref, src0_ref, srcn_ref, dstp_ref, wl_ref, wh_ref, h_hbm,
                w1a_ref, w3a_ref, w2a_ref, w1b_ref, w3b_ref, w2b_ref, lg_ref, lb_ref,
                o_hbm, xbuf0, xbuf1, obuf0, obuf1, gsem, ssem):
    i = pl.program_id(0)
    tm = xbuf0.shape[0]
    xbufs = (xbuf0, xbuf1)
    obufs = (obuf0, obuf1)
    nv = nv_ref[i]
    nv_p1 = nv_ref[jnp.maximum(i - 1, 0)]
    odd = lax.rem(i, 2) == 1
    even = jnp.logical_not(odd)

    def gather(r, p, row):
        return pltpu.make_async_copy(h_hbm.at[pl.ds(row, 1)], xbufs[p].at[pl.ds(r, 1)], gsem.at[p])

    def scatter(r, p, row):
        return pltpu.make_async_copy(obufs[p].at[pl.ds(r, 1)], o_hbm.at[pl.ds(row, 1)], ssem.at[p])

    def wait_gather(p):
        for r in range(tm):
            gather(r, p, 0).wait()

    def wait_scatter(p):
        for r in range(tm):
            scatter(r, p, 0).wait()

    def tile(p, age):
        q = 1 - p
        half = MOE_DMA_GROUPS // 2
        per = tm // half

        def issue(k):
            for r in range((k % half) * per, (k % half + 1) * per):
                if k < half:
                    gather(r, q, srcn_ref[0, 0, r]).start()
                elif age >= 1:
                    scatter(r, q, dstp_ref[0, 0, r]).start()

        wait_gather(p)
        x = _load_rows(xbufs[p])
        xb = x.astype(BF16)
        issue(0)
        h1 = _dot(xb, w1a_ref[0])
        issue(1)
        h3 = _dot(xb, w3a_ref[0])
        issue(2)
        ya = _dot((jax.nn.silu(h1) * h3).astype(BF16), w2a_ref[0])
        issue(3)
        h1 = _dot(xb, w1b_ref[0])
        issue(4)
        h3 = _dot(xb, w3b_ref[0])
        issue(5)
        yb = _dot((jax.nn.silu(h1) * h3).astype(BF16), w2b_ref[0])
        issue(6)
        y = wl_ref[...] * ya + wh_ref[...] * yb
        issue(7)
        out = _layer_norm(DN_ALPHA * x + y, lg_ref[...], lb_ref[...])
        if age >= 2:
            wait_scatter(p)
        _store_rows(obufs[p], out)

    @pl.when(i == 0)
    def _():
        obuf1[...] = jnp.zeros(obuf1.shape, F32)
        n_real = o_hbm.shape[0] - 2 * tm
        for blk in range(2):
            cp = pltpu.make_async_copy(obuf1, o_hbm.at[pl.ds(n_real + blk * tm, tm)], ssem.at[1])
            cp.start()
            cp.wait()

    @pl.when((nv > 0) & (i == 0))
    def _():
        def start0(r, carry):
            gather(r, 0, src0_ref[0, 0, r]).start()
            return carry
        lax.fori_loop(0, tm, start0, 0)
        tile(0, 0)

    pl.when((nv > 0) & (i == 1))(functools.partial(tile, 1, 1))
    pl.when((nv > 0) & (i >= 2) & even)(functools.partial(tile, 0, 2))
    pl.when((nv > 0) & (i >= 2) & odd)(functools.partial(tile, 1, 2))

    def tail(p):
        q = 1 - p
        wait_gather(p)

        @pl.when(i >= 2)
        def _():
            wait_scatter(p)

        def start(r, carry):
            scatter(r, q, dstp_ref[0, 0, r]).start()
            return carry
        lax.fori_loop(0, tm, start, 0)
        wait_scatter(q)

    is_tail = (nv == 0) & (i >= 1) & (nv_p1 > 0)
    pl.when(is_tail & even)(functools.partial(tail, 0))
    pl.when(is_tail & odd)(functools.partial(tail, 1))


def _moe(h3, bucket, rank, counts, w_lo, w_hi, w1, w3, w2, ln_g, ln_b):
    t, d = h3.shape[0], D_MODEL
    tm = MOE_TM
    assert tm % MOE_DMA_GROUPS == 0
    src, dst, wl, wh, ea, eb, nvalid, n_tiles = _moe_plan(bucket, rank, counts, w_lo, w_hi, t)
    wspec = lambda shape, which: pl.BlockSpec(
        shape, (lambda i, ea_r, eb_r, nv_r: (ea_r[i], 0, 0)) if which == 0
        else (lambda i, ea_r, eb_r, nv_r: (eb_r[i], 0, 0)))
    w13 = (1, d, D_EXPERT)
    w2s = (1, D_EXPERT, d)
    smem_tile = lambda fn: pl.BlockSpec((1, 1, tm), lambda i, *_: (fn(i), 0, 0), memory_space=pltpu.SMEM)
    grid_spec = pltpu.PrefetchScalarGridSpec(
        num_scalar_prefetch=3,
        grid=(n_tiles,),
        in_specs=[smem_tile(lambda i: 0),
                  smem_tile(lambda i: jnp.minimum(i + 1, n_tiles - 1)),
                  smem_tile(lambda i: jnp.maximum(i - 1, 0)),
                  pl.BlockSpec((tm, 1), lambda i, *_: (i, 0)),
                  pl.BlockSpec((tm, 1), lambda i, *_: (i, 0)),
                  pl.BlockSpec(memory_space=pl.ANY),
                  wspec(w13, 0), wspec(w13, 0), wspec(w2s, 0),
                  wspec(w13, 1), wspec(w13, 1), wspec(w2s, 1),
                  pl.BlockSpec((1, d), lambda i, *_: (0, 0)),
                  pl.BlockSpec((1, d), lambda i, *_: (0, 0))],
        out_specs=pl.BlockSpec(memory_space=pl.ANY),
        scratch_shapes=[pltpu.VMEM((tm,) + ROW_TILE, F32)] * 4
        + [pltpu.SemaphoreType.DMA((2,)), pltpu.SemaphoreType.DMA((2,))],
    )
    w1b, w3b, w2b = w1.astype(BF16), w3.astype(BF16), w2.astype(BF16)
    return pl.pallas_call(
        _moe_kernel,
        out_shape=jax.ShapeDtypeStruct((t + 2 * tm,) + ROW_TILE, F32),
        grid_spec=grid_spec,
        compiler_params=_cparams(("arbitrary",)),
        name="moe",
    )(ea, eb, nvalid, src, src, dst, wl, wh, h3, w1b, w3b, w2b, w1b, w3b, w2b,
      ln_g.reshape(1, d), ln_b.reshape(1, d))


def kernel(x, ln0_g, ln0_b, w_in, pool_w, pool_scale, cmp_k_pe, cmp_k_w1, cmp_k_w2, cmp_v_pe, cmp_v_w1, cmp_v_w2, rel_bias, conv_w, conv_b, dt_bias, a_log, d_skip, ssd_norm_g, w_br_pool, w_br_nsa, w_br_ssd, w_out, ln1_g, ln1_b, router_w, router_b, exp_w1, exp_w3, exp_w2, ln2_g, ln2_b):
    bsz, s, d = x.shape
    assert d == D_MODEL and s % MIX_TS == 0 and s // CMP_STRIDE == LANE and (bsz * s) % PROJ_TM == 0
    t = bsz * s
    ts, ws, bc = _nsa_bias_tables(rel_bias)
    et, ovt = _nsa_consts(s)
    h = x.reshape(t, d)
    for i in range(DEPTH):
        if i == 0:
            proj, h = _proj(h, _prep_w_in(w_in[i]), t, ln=(ln0_g, ln0_b))
        else:
            proj = _proj(h, _prep_w_in(w_in[i]), t)
        kc, vc = _compress(proj, bsz, s, cmp_k_pe[i], cmp_k_w1[i], cmp_k_w2[i],
                           cmp_v_pe[i], cmp_v_w1[i], cmp_v_w2[i])
        nsa_out = _nsa(proj, kc, vc, ts, ws, bc, et, ovt, bsz, s)
        ssd_y = _ssd(proj, bsz, s, conv_w[i], conv_b[i], dt_bias[i], a_log[i], d_skip[i], ssd_norm_g[i])
        h1 = _mix(proj, nsa_out, ssd_y, h, bsz, s, pool_w[i], pool_scale[i], w_br_pool[i], w_br_nsa[i],
                  w_br_ssd[i], w_out[i], ln1_g[i], ln1_b[i])
        bucket, w_lo, w_hi, rank, cnt = _router(h1, router_w, router_b)
        counts = cnt[:N_BUCKETS, 0].astype(jnp.int32)
        h = _moe(h1, bucket[0], rank[0], counts, w_lo[0], w_hi[0], exp_w1[i], exp_w3[i], exp_w2[i],
                 ln2_g[i], ln2_b[i])
    return h[:t].reshape(bsz, s, d)
```

```python
import functools
import math

import numpy as np
import jax
import jax.numpy as jnp
from jax import lax
from jax.experimental import pallas as pl
from jax.experimental.pallas import tpu as pltpu

F32 = jnp.float32
BF16 = jnp.bfloat16

D_MODEL = 1024
DEPTH = 2
DN_ALPHA = (2.0 * DEPTH) ** 0.25
LN_EPS = 1e-5
NEG = -1e30
BIG = 1e6

POOL_WINDOWS = (2, 4, 8, 16)
POOL_GROUP = 128
POOL_WIDTH = 512
MAX_POOL_WINDOW = 16

NSA_HEAD_DIM = 64
NSA_KV_HEADS = 2
NSA_HPG = 4
NSA_HEADS = 8
NSA_WIDTH = 512
CMP_LEN = 32
CMP_STRIDE = 16
CMP_HIDDEN = 128
SLC_LEN = 64
SLC_TOPN = 8
WIN_LEN = 512
Q_BLOCK = 128
REL_BUCKETS = 32
REL_MAX_DIST = 128

SSD_D_INNER = 1024
SSD_HEAD_DIM = 64
SSD_HEADS = 16
SSD_GROUPS = 4
SSD_STATE = 128
SSD_CONV = 4
SSD_CHUNK = 128
SSD_XBC = 2048

N_EXPERTS = 16
N_EXPERT_GROUPS = 4
EXPERTS_PER_GROUP = 4
D_EXPERT = 512
PAIRS = ((0, 1), (0, 2), (0, 3), (1, 2), (1, 3), (2, 3))
N_BUCKETS = N_EXPERT_GROUPS * len(PAIRS)

LANE = 128
SUBLANE = 8
VMEM_LIMIT = 48 * 1024 * 1024

C_POOL = 0
C_Q = 512
C_KV = 1024
C_NG = 1792
C_DT = 1920
C_XBC = 2048
C_Z = 4096
C_GATE = 5120
D_PAD = 8192

PROJ_TM = 1024
PROJ_TN = 2048
MIX_TS = 256
MOE_TM = 256


def _dot(a, b):
    return jnp.dot(a, b, preferred_element_type=F32)


def _dot_nt(a, b):
    return lax.dot_general(a, b, (((1,), (1,)), ((), ())), preferred_element_type=F32)


def _dot_tn(a, b):
    return lax.dot_general(a, b, (((0,), (0,)), ((), ())), preferred_element_type=F32)


def _split3(x):
    x1 = x.astype(BF16)
    r1 = x - x1.astype(F32)
    x2 = r1.astype(BF16)
    x3 = (r1 - x2.astype(F32)).astype(BF16)
    return x1, x2, x3


def _layer_norm(x, g, b):
    mu = jnp.mean(x, -1, keepdims=True)
    xc = x - mu
    var = jnp.mean(xc * xc, -1, keepdims=True)
    return xc * lax.rsqrt(var + LN_EPS) * g + b


def _cparams(sem):
    return pltpu.CompilerParams(dimension_semantics=sem, vmem_limit_bytes=VMEM_LIMIT)


ROW_TILE = (D_MODEL,)


def _load_rows(ref):
    return ref[...]


def _store_rows(ref, val):
    ref[...] = val


def _row_spec(n, index):
    return pl.BlockSpec((n,) + ROW_TILE, lambda *ids: (index(*ids), 0))


def _proj_kernel(h_ref, w_ref, o_ref, hb_ref):
    @pl.when(pl.program_id(1) == 0)
    def _():
        hb_ref[...] = _load_rows(h_ref).astype(BF16)

    o_ref[...] = _dot(hb_ref[...], w_ref[...])


def _ln_proj_kernel(x_ref, g_ref, b_ref, w_ref, o_ref, h_ref, hb_ref):
    @pl.when(pl.program_id(1) == 0)
    def _():
        h = _layer_norm(x_ref[...], g_ref[...], b_ref[...])
        _store_rows(h_ref, h)
        hb_ref[...] = h.astype(BF16)

    o_ref[...] = _dot(hb_ref[...], w_ref[...])


def _proj(h2d, w_pad, t, ln=None):
    d = D_MODEL
    grid = (t // PROJ_TM, D_PAD // PROJ_TN)
    w_spec = pl.BlockSpec((d, PROJ_TN), lambda i, j: (0, j))
    o_spec = pl.BlockSpec((PROJ_TM, PROJ_TN), lambda i, j: (i, j))
    o_shape = jax.ShapeDtypeStruct((t, D_PAD), F32)
    scratch = [pltpu.VMEM((PROJ_TM, d), BF16)]
    params = _cparams(("parallel", "arbitrary"))
    if ln is None:
        return pl.pallas_call(
            _proj_kernel, out_shape=o_shape, grid=grid,
            in_specs=[_row_spec(PROJ_TM, lambda i, j: i), w_spec], out_specs=o_spec,
            scratch_shapes=scratch, compiler_params=params, name="proj",
        )(h2d, w_pad)
    vec = pl.BlockSpec((1, d), lambda i, j: (0, 0))
    return pl.pallas_call(
        _ln_proj_kernel, out_shape=(o_shape, jax.ShapeDtypeStruct((t,) + ROW_TILE, F32)), grid=grid,
        in_specs=[_row_spec(PROJ_TM, lambda i, j: i), vec, vec, w_spec],
        out_specs=(o_spec, _row_spec(PROJ_TM, lambda i, j: i)),
        scratch_shapes=scratch, compiler_params=params, name="ln_proj",
    )(h2d, ln[0].reshape(1, d), ln[1].reshape(1, d), w_pad)


def _pad_cols(w, n):
    return jnp.pad(w, ((0, 0), (0, n - w.shape[1])))


def _prep_w_in(w_in):
    d = w_in.shape[0]
    o = 0
    w_pool = w_in[:, o:o + POOL_WIDTH]; o += POOL_WIDTH
    w_q = w_in[:, o:o + NSA_WIDTH]; o += NSA_WIDTH
    w_kv = w_in[:, o:o + 768]; o += 768
    w_ng = w_in[:, o:o + 24]; o += 24
    w_z = w_in[:, o:o + SSD_D_INNER]; o += SSD_D_INNER
    w_xbc = w_in[:, o:o + SSD_XBC]; o += SSD_XBC
    w_dt = w_in[:, o:o + SSD_HEADS]; o += SSD_HEADS
    w_gate = w_in[:, o:o + 3 * D_MODEL]
    w_q = w_q.reshape(d, NSA_KV_HEADS, NSA_HPG, NSA_HEAD_DIM).transpose(0, 2, 1, 3).reshape(d, NSA_WIDTH)
    w = jnp.concatenate([w_pool, w_q, w_kv, _pad_cols(w_ng, LANE), _pad_cols(w_dt, LANE),
                         w_xbc, w_z, w_gate], axis=1)
    return w.astype(BF16)


def _compress_one(tok_ref, pe_ref, w1_ref, w2_ref):
    n16 = tok_ref.shape[0] // CMP_STRIDE
    acc_a = jnp.zeros((n16, 2 * CMP_HIDDEN), F32)
    acc_b = jnp.zeros((n16, 2 * CMP_HIDDEN), F32)
    for l in range(CMP_STRIDE):
        rows = tok_ref[pl.ds(l, n16, stride=CMP_STRIDE), :]
        xa = (rows + pe_ref[l:l + 1, :]).astype(BF16)
        xb = (rows + pe_ref[CMP_STRIDE + l:CMP_STRIDE + l + 1, :]).astype(BF16)
        acc_a = acc_a + _dot(xa, w1_ref[l])
        acc_b = acc_b + _dot(xb, w1_ref[CMP_STRIDE + l])
    hid = acc_a + pltpu.roll(acc_b, n16 - 1, axis=0)
    hid = jax.nn.gelu(hid)
    return _dot(hid.astype(BF16), w2_ref[...])


def _compress_kernel(kt_ref, vt_ref, pek_ref, w1k_ref, w2k_ref, pev_ref, w1v_ref, w2v_ref,
                     kc_ref, vc_ref):
    kc_ref[0] = _compress_one(kt_ref, pek_ref, w1k_ref, w2k_ref).astype(BF16)
    vc_ref[0] = _compress_one(vt_ref, pev_ref, w1v_ref, w2v_ref).T.astype(BF16)


def _blockdiag2(w):
    z = jnp.zeros_like(w)
    return jnp.concatenate([jnp.concatenate([w, z], -1), jnp.concatenate([z, w], -1)], -2)


def _compress(proj2d, bsz, s, pe_k, w1_k, w2_k, pe_v, w1_v, w2_v):
    n16 = s // CMP_STRIDE

    def prep(pe, w1, w2):
        pe2 = jnp.concatenate([pe, pe], axis=-1)
        w1b = _blockdiag2(w1.reshape(CMP_LEN, NSA_HEAD_DIM, CMP_HIDDEN)).astype(BF16)
        w2b = _blockdiag2(w2).astype(BF16)
        return pe2, w1b, w2b

    pk = prep(pe_k, w1_k, w2_k)
    pv = prep(pe_v, w1_v, w2_v)
    full = lambda shape: pl.BlockSpec(shape, lambda b: (0,) * len(shape))
    wspecs = [full((CMP_LEN, LANE)), full((CMP_LEN, LANE, 2 * CMP_HIDDEN)), full((2 * CMP_HIDDEN, LANE))]
    return pl.pallas_call(
        _compress_kernel,
        out_shape=(jax.ShapeDtypeStruct((bsz, n16, LANE), BF16),
                   jax.ShapeDtypeStruct((bsz, n16, LANE), BF16)),
        grid=(bsz,),
        in_specs=[pl.BlockSpec((s, LANE), lambda b: (b, C_KV // LANE)),
                  pl.BlockSpec((s, LANE), lambda b: (b, C_KV // LANE + 1))] + wspecs + wspecs,
        out_specs=(pl.BlockSpec((1, n16, LANE), lambda b: (b, 0, 0)),
                   pl.BlockSpec((1, n16, LANE), lambda b: (b, 0, 0))),
        compiler_params=_cparams(("parallel",)),
        name="nsa_compress",
    )(proj2d, proj2d, *pk, *pv)


def _rel_bucket(dist):
    n = jnp.maximum(dist, 0)
    max_exact = REL_BUCKETS // 2
    nf = jnp.maximum(n, 1).astype(F32)
    large = max_exact + (jnp.log(nf / max_exact) / math.log(REL_MAX_DIST / max_exact)
                         * (REL_BUCKETS - max_exact)).astype(jnp.int32)
    large = jnp.minimum(large, REL_BUCKETS - 1)
    return jnp.where(n < max_exact, n, large)


LOG2E = math.log2(math.e)
WIN_BACK = WIN_LEN // Q_BLOCK
WIN_KEYS = (WIN_BACK + 1) * Q_BLOCK
WIN_STRIP_TILES = 2 * WIN_BACK + 1
SEL_TILES = 4
SEL_KEYS = SEL_TILES * Q_BLOCK
SEL_STRIP_TILES = 3 * SEL_TILES - 1
CMP_BIAS_ROWS = 2 * LANE
PV_ROWS = NSA_HEAD_DIM + 16


def _nsa_bias_tables(rel_bias):
    tbl = rel_bias.astype(F32)
    far = tbl[REL_BUCKETS - 1]
    hq = NSA_HPG * Q_BLOCK

    def lookup(dist):
        onehot = (_rel_bucket(jnp.asarray(dist))[..., None] == jnp.arange(REL_BUCKETS)).astype(F32)
        return jnp.einsum("...b,bh->...h", onehot, tbl, precision=lax.Precision.HIGHEST)

    k = np.arange(Q_BLOCK)[None, :, None]
    q = np.arange(Q_BLOCK)[None, None, :]

    def strip(first, last, visible):
        n = first - last + 1
        dist = np.arange(first, last - 1, -1)[:, None, None] * Q_BLOCK + q - k
        v = jnp.where(jnp.asarray(visible(dist))[..., None], (lookup(dist) - far) * LOG2E, NEG)
        v = v.reshape(n, Q_BLOCK, Q_BLOCK, NSA_KV_HEADS, NSA_HPG).transpose(3, 0, 1, 4, 2)
        return v.reshape(NSA_KV_HEADS, n * Q_BLOCK, hq)

    ts = strip(2 * SEL_TILES - 1, 1 - SEL_TILES, lambda d: d >= 0)
    ws = strip(WIN_BACK, -WIN_BACK, lambda d: (d >= 0) & (d < WIN_LEN))
    r = np.arange(CMP_BIAS_ROWS)[:, None]
    dist_c = q[0] - CMP_STRIDE * (r - LANE) - (CMP_LEN - 1)
    bc = jnp.where(jnp.asarray(dist_c >= 0)[..., None], lookup(dist_c) * LOG2E, NEG)
    bc = bc.reshape(CMP_BIAS_ROWS, Q_BLOCK, NSA_KV_HEADS, NSA_HPG).transpose(2, 0, 3, 1)
    bc = bc.reshape(NSA_KV_HEADS, CMP_BIAS_ROWS, hq)
    return ts, ws, bc


def _nsa_consts(s):
    n_slc = s // SLC_LEN
    et = (np.arange(s)[:, None] // SLC_LEN == np.arange(LANE)[None, :]).astype(np.float32)
    n_cmp = (s - CMP_LEN) // CMP_STRIDE + 1
    c0 = np.arange(n_cmp) * CMP_STRIDE
    s0 = np.arange(n_slc) * SLC_LEN
    ov = np.clip(np.minimum(c0[None, :] + CMP_LEN, s0[:, None] + SLC_LEN)
                 - np.maximum(c0[None, :], s0[:, None]), 0, None) / CMP_LEN
    ovt = np.zeros((n_slc, LANE), np.float32)
    ovt[:, :n_cmp] = ov
    return jnp.asarray(et, BF16), jnp.asarray(ovt, BF16)


def _nsa_kernel(q_ref, ng_ref, ks_ref, vs_ref, kw_ref, vw_ref, kc_ref, vct_ref,
                ts_ref, ws_ref, bc_ref, et_ref, ovt_ref, o_ref,
                ksb_ref, kwb_ref, vst_ref, vwt_ref, os_ref, *, nq, n_slc):
    qi = pl.program_id(1)
    hq = NSA_HPG * Q_BLOCK
    dh = NSA_HEAD_DIM
    groups = range(NSA_KV_HEADS)
    gsl = [slice(g * dh, (g + 1) * dh) for g in groups]

    @pl.when(qi == 0)
    def _():
        ksb_ref[:, 0:LANE] = ks_ref[...].astype(BF16)
        ksb_ref[:, LANE:] = et_ref[...]
        kwb_ref[...] = kw_ref[...].astype(BF16)
        ones_rows = jnp.where(lax.broadcasted_iota(jnp.int32, (PV_ROWS - dh, ks_ref.shape[0]), 0) == 0, 1.0, 0.0)
        for g in groups:
            vst_ref[g, dh:, :] = ones_rows.astype(BF16)
            vwt_ref[g, dh:, :] = ones_rows.astype(BF16)
        for kt in range(nq):
            sl = slice(kt * Q_BLOCK, (kt + 1) * Q_BLOCK)
            vs_t = vs_ref[sl, :].T
            vw_t = vw_ref[sl, :].T
            for g in groups:
                vst_ref[g, 0:dh, sl] = vs_t[gsl[g], :].astype(BF16)
                vwt_ref[g, 0:dh, sl] = vw_t[gsl[g], :].astype(BF16)

    row_lo = lax.broadcasted_iota(jnp.int32, (LANE, hq), 0) < dh

    def heads4(m):
        return jnp.concatenate([m] * NSA_HPG, axis=1)

    gates_t = jax.nn.sigmoid(ng_ref[...]).T
    q_t = jnp.concatenate(
        [(q_ref[:, h * LANE:(h + 1) * LANE] * (dh ** -0.5 * LOG2E)).T for h in range(NSA_HPG)], axis=1)
    qg_t = [jnp.where(row_lo if g == 0 else jnp.logical_not(row_lo), q_t, 0.0).astype(BF16) for g in groups]

    boff = pl.multiple_of(LANE - SUBLANE * qi, SUBLANE)
    jrow = lax.broadcasted_iota(jnp.int32, (n_slc, LANE), 0)
    tj = qi * Q_BLOCK + lax.broadcasted_iota(jnp.int32, (n_slc, LANE), 1)
    jt = lax.shift_right_logical(tj, SLC_LEN.bit_length() - 1)
    forced = (jrow == 0) | (jrow == jt) | (jrow == jt - 1)
    valid = jrow * SLC_LEN <= tj
    o_c = []
    q_aug = []
    for g in groups:
        s = _dot(kc_ref[0], qg_t[g]) + bc_ref[g, pl.ds(boff, LANE), :]
        mx = jnp.max(s, axis=0, keepdims=True)
        p = jnp.exp2(s - mx)
        lsum = jnp.sum(p, axis=0, keepdims=True)
        pb = (p * jnp.where(mx > 0.5 * NEG, 1.0 / lsum, 0.0)).astype(BF16)
        o_c.append(_dot(vct_ref[0, gsl[g], :], pb))
        imp4 = _dot(ovt_ref[...], pb)
        imp = imp4[:, 0:LANE]
        for h in range(1, NSA_HPG):
            imp = imp + imp4[:, h * LANE:(h + 1) * LANE]
        val = jnp.where(forced, BIG, jnp.where(valid, imp, -BIG))
        rank = jnp.zeros((n_slc, LANE), F32)
        for i in range(n_slc):
            ci = val[i:i + 1, :]
            rank = rank + jnp.where(ci > val, 1.0, 0.0) + jnp.where(ci == val, (jrow > i).astype(F32), 0.0)
        block_rows = heads4(jnp.where(rank < SLC_TOPN, 0.0, NEG)).astype(BF16)
        q_aug.append(jnp.concatenate(
            [qg_t[g], block_rows, jnp.zeros((LANE - n_slc, hq), BF16)], axis=0))

    n_super = qi // SEL_TILES + 1
    diag = qi - SEL_TILES * (n_super - 1)

    def selected(n):
        nk = n * SEL_KEYS
        lo = max(n - 2, 0) * SEL_KEYS
        toff = pl.multiple_of((2 * SEL_TILES - 1 - diag) * Q_BLOCK - (nk - lo - SEL_KEYS), Q_BLOCK)
        for g in groups:
            parts = [(lo, nk, _dot(ksb_ref[lo:nk, :], q_aug[g]) + ts_ref[g, pl.ds(toff, nk - lo), :])]
            if lo > 0:
                parts.append((0, lo, _dot(ksb_ref[0:lo, :], q_aug[g])))
            m = parts[0][2].max(axis=0, keepdims=True)
            for _, _, s in parts[1:]:
                m = jnp.maximum(m, s.max(axis=0, keepdims=True))
            acc = None
            for a, b, s in parts:
                pv = _dot(vst_ref[g, :, a:b], jnp.exp2(s - m).astype(BF16))
                acc = pv if acc is None else acc + pv
            os_ref[g] = acc[0:dh] * (1.0 / acc[dh:dh + 1])

    w0 = jnp.maximum(qi - WIN_BACK, 0)
    woff = pl.multiple_of(w0 * Q_BLOCK, Q_BLOCK)
    boff_w = pl.multiple_of((WIN_BACK - (qi - w0)) * Q_BLOCK, Q_BLOCK)
    kw_bf = kwb_ref[pl.ds(woff, WIN_KEYS), :]
    o_w = []
    for g in groups:
        s = _dot(kw_bf, qg_t[g]) + ws_ref[g, pl.ds(boff_w, WIN_KEYS), :]
        p = jnp.exp2(s - jnp.max(s, axis=0, keepdims=True)).astype(BF16)
        acc = _dot(vwt_ref[g, :, pl.ds(woff, WIN_KEYS)], p)
        o_w.append(acc[0:dh] * (1.0 / acc[dh:dh + 1]))

    for n in range(1, nq // SEL_TILES + 1):
        pl.when(n_super == n)(functools.partial(selected, n))
    o_s = [os_ref[g] for g in groups]

    for h in range(NSA_HPG):
        hl = slice(h * LANE, (h + 1) * LANE)
        halves = []
        for g in groups:
            c = g * NSA_HPG + h
            halves.append(gates_t[c:c + 1, :] * o_c[g][:, hl]
                          + gates_t[NSA_HEADS + c:NSA_HEADS + c + 1, :] * o_s[g][:, hl]
                          + gates_t[2 * NSA_HEADS + c:2 * NSA_HEADS + c + 1, :] * o_w[g][:, hl])
        o_ref[:, hl] = jnp.concatenate(halves, axis=0).T


def _nsa(proj2d, kc, vct, ts, ws, bc, et, ovt, bsz, s):
    nq = s // Q_BLOCK
    n_slc = s // SLC_LEN
    hq = NSA_HPG * Q_BLOCK
    kvb = C_KV // LANE
    kv_spec = lambda blk: pl.BlockSpec((s, LANE), lambda b, qi: (b, kvb + blk))
    full = lambda shape: pl.BlockSpec(shape, lambda b, qi: (0,) * len(shape), pipeline_mode=pl.Buffered(1))
    return pl.pallas_call(
        functools.partial(_nsa_kernel, nq=nq, n_slc=n_slc),
        out_shape=jax.ShapeDtypeStruct((bsz * s, NSA_WIDTH), F32),
        grid=(bsz, nq),
        in_specs=[pl.BlockSpec((Q_BLOCK, NSA_WIDTH), lambda b, qi: (b * nq + qi, C_Q // NSA_WIDTH)),
                  pl.BlockSpec((Q_BLOCK, LANE), lambda b, qi: (b * nq + qi, C_NG // LANE)),
                  kv_spec(2), kv_spec(3), kv_spec(4), kv_spec(5),
                  pl.BlockSpec((1, s // CMP_STRIDE, LANE), lambda b, qi: (b, 0, 0)),
                  pl.BlockSpec((1, LANE, s // CMP_STRIDE), lambda b, qi: (b, 0, 0)),
                  full((NSA_KV_HEADS, SEL_STRIP_TILES * Q_BLOCK, hq)),
                  full((NSA_KV_HEADS, WIN_STRIP_TILES * Q_BLOCK, hq)),
                  full((NSA_KV_HEADS, CMP_BIAS_ROWS, hq)),
                  full((s, LANE)),
                  full((n_slc, LANE))],
        out_specs=pl.BlockSpec((Q_BLOCK, NSA_WIDTH), lambda b, qi: (b * nq + qi, 0)),
        scratch_shapes=[pltpu.VMEM((s, 2 * LANE), BF16), pltpu.VMEM((s, LANE), BF16),
                        pltpu.VMEM((NSA_KV_HEADS, PV_ROWS, s), BF16), pltpu.VMEM((NSA_KV_HEADS, PV_ROWS, s), BF16),
                        pltpu.VMEM((NSA_KV_HEADS, NSA_HEAD_DIM, hq), F32)],
        compiler_params=_cparams(("parallel", "arbitrary")),
        name="nsa_attn",
    )(proj2d, proj2d, proj2d, proj2d, proj2d, proj2d, kc, vct, ts, ws, bc, et, ovt)


def _ssd_kernel(xbc_ref, prev_ref, z_ref, dt_ref, cw_ref, cb_ref, dtb_ref, alog_ref,
                dskip_ref, ng_ref, e16_ref, tril_ref, o_ref, state_ref):
    c = pl.program_id(1)
    L = SSD_CHUNK
    pair_w = 2 * SSD_HEAD_DIM

    @pl.when(c == 0)
    def _():
        state_ref[...] = jnp.zeros(state_ref.shape, F32)

    xcat = jnp.concatenate([jnp.where(c > 0, prev_ref[...], 0.0), xbc_ref[...]], axis=0)
    acc = cw_ref[0:1, :] * xcat
    for k in range(1, SSD_CONV):
        acc = pltpu.roll(acc, 1, axis=0) + cw_ref[k:k + 1, :] * xcat
    xbc = jax.nn.silu(acc[SUBLANE:, :] + cb_ref[...])
    xs = xbc[:, :SSD_D_INNER]
    bmat = xbc[:, SSD_D_INNER:SSD_D_INNER + SSD_GROUPS * SSD_STATE]
    cmat = xbc[:, SSD_D_INNER + SSD_GROUPS * SSD_STATE:]

    dt = jax.nn.softplus(dt_ref[...] + dtb_ref[...])
    tril = tril_ref[...]
    e16 = e16_ref[...]
    d1, d2, d3 = _split3(dt)
    dt_x = _dot(d1, e16) + _dot(d2, e16) + _dot(d3, e16)
    a_c = dt * (-jnp.exp(alog_ref[...]) * LOG2E)
    c1, c2, c3 = _split3(a_c)
    acs = _dot(tril, c1) + _dot(tril, c2) + _dot(tril, c3)
    x1, x2, x3 = _split3(acs)
    acs_x = _dot(x1, e16) + _dot(x2, e16) + _dot(x3, e16)
    acs_t = acs.T
    last = acs_x[L - 1:L, :]
    eacs = jnp.exp2(acs_x)
    decay_in = jnp.exp2(last - acs_x)
    cdec = jnp.exp2(last)
    xdt = xs * dt_x
    wst = xdt * decay_in

    row = lax.broadcasted_iota(jnp.int32, (L, L), 0)
    lane = lax.broadcasted_iota(jnp.int32, (L, L), 1)
    tri = row >= lane
    lo = lane < SSD_HEAD_DIM

    ys = []
    for g in range(SSD_GROUPS):
        cg = cmat[:, g * SSD_STATE:(g + 1) * SSD_STATE].astype(BF16)
        bg = bmat[:, g * SSD_STATE:(g + 1) * SSD_STATE].astype(BF16)
        cb = _dot_nt(cg, bg)
        for pr in range(2):
            pp = g * 2 + pr
            ls = slice(pp * pair_w, (pp + 1) * pair_w)
            xp = xdt[:, ls]
            yd = None
            for e in range(2):
                hd = 2 * pp + e
                diff = acs[:, hd:hd + 1] - acs_t[hd:hd + 1, :]
                seg = jnp.exp2(jnp.where(tri, diff, NEG))
                pm = (cb * seg).astype(BF16)
                xe = jnp.where(lo if e == 0 else jnp.logical_not(lo), xp, 0.0).astype(BF16)
                term = _dot(pm, xe)
                yd = term if yd is None else yd + term
            st = state_ref[pp]
            yo = _dot(cg, st.astype(BF16)) * eacs[:, ls]
            ys.append(yd + yo)
            state_ref[pp] = cdec[:, ls] * st + _dot_tn(bg, wst[:, ls].astype(BF16))

    y = jnp.concatenate(ys, axis=1) + xs * dskip_ref[...]
    y = y * jax.nn.silu(z_ref[...])
    gw = SSD_D_INNER // SSD_GROUPS
    outs = []
    for gi in range(SSD_GROUPS):
        yg = y[:, gi * gw:(gi + 1) * gw]
        outs.append(yg * lax.rsqrt(jnp.mean(yg * yg, -1, keepdims=True) + LN_EPS))
    o_ref[...] = jnp.concatenate(outs, axis=1) * ng_ref[...]


def _ssd(proj2d, bsz, s, conv_w, conv_b, dt_bias, a_log, d_skip, norm_g):
    nc = s // SSD_CHUNK
    L = SSD_CHUNK
    pad16 = lambda v: jnp.pad(v.reshape(1, SSD_HEADS), ((0, 0), (0, LANE - SSD_HEADS)))
    rep64 = lambda v: jnp.repeat(v, SSD_HEAD_DIM).reshape(1, SSD_D_INNER)
    e16 = np.zeros((LANE, SSD_D_INNER), np.float32)
    e16[np.arange(SSD_D_INNER) // SSD_HEAD_DIM, np.arange(SSD_D_INNER)] = 1.0
    tril = np.tril(np.ones((L, L), np.float32))
    full = lambda shape: pl.BlockSpec(shape, lambda b, c: (0,) * len(shape))
    rows8 = s // SUBLANE
    return pl.pallas_call(
        _ssd_kernel,
        out_shape=jax.ShapeDtypeStruct((bsz * s, SSD_D_INNER), F32),
        grid=(bsz, nc),
        in_specs=[pl.BlockSpec((L, SSD_XBC), lambda b, c: (b * nc + c, C_XBC // SSD_XBC)),
                  pl.BlockSpec((SUBLANE, SSD_XBC),
                               lambda b, c: (jnp.maximum(b * rows8 + c * (L // SUBLANE) - 1, 0), C_XBC // SSD_XBC)),
                  pl.BlockSpec((L, SSD_D_INNER), lambda b, c: (b * nc + c, C_Z // SSD_D_INNER)),
                  pl.BlockSpec((L, LANE), lambda b, c: (b * nc + c, C_DT // LANE)),
                  full((SSD_CONV, SSD_XBC)), full((1, SSD_XBC)), full((1, LANE)), full((1, LANE)),
                  full((1, SSD_D_INNER)), full((1, SSD_D_INNER)),
                  full((LANE, SSD_D_INNER)), full((L, L))],
        out_specs=pl.BlockSpec((L, SSD_D_INNER), lambda b, c: (b * nc + c, 0)),
        scratch_shapes=[pltpu.VMEM((SSD_HEADS // 2, SSD_STATE, 2 * SSD_HEAD_DIM), F32)],
        compiler_params=_cparams(("parallel", "arbitrary")),
        name="ssd",
    )(proj2d, proj2d, proj2d, proj2d, conv_w, conv_b.reshape(1, SSD_XBC), pad16(dt_bias), pad16(a_log),
      rep64(d_skip), norm_g.reshape(1, SSD_D_INNER),
      jnp.asarray(e16, BF16), jnp.asarray(tril, BF16))


def _mix_kernel(up_ref, upprev_ref, nsa_ref, ssd_ref, ga_ref, gb_ref, gc_ref, h_ref,
                pw_ref, ps_ref, wbp_ref, wbn_ref, wbs_ref, wo_ref, lg_ref, lb_ref, o_ref, *, ts):
    si = pl.program_id(1)
    w0 = MAX_POOL_WINDOW
    prev = jnp.where(si > 0, upprev_ref[...], 0.0)
    xcat = jnp.concatenate([prev, up_ref[...]], axis=0)
    tpos = si * ts + lax.broadcasted_iota(jnp.int32, (ts, POOL_GROUP), 0)
    ygs = []
    for gi, w in enumerate(POOL_WINDOWS):
        x = xcat[:, gi * POOL_GROUP:(gi + 1) * POOL_GROUP]
        acc = x
        d = 1
        while d < w:
            acc = acc + pltpu.roll(acc, d, axis=0)
            d *= 2
        cnt = jnp.minimum(tpos + 1, w).astype(F32)
        r = acc[w0:, :] / cnt - x[w0:, :]
        ygs.append(_dot(r.astype(BF16), pw_ref[gi]))
    y = jnp.concatenate(ygs, axis=1) * ps_ref[...]
    br_a = _dot(y.astype(BF16), wbp_ref[...])
    br_b = _dot(nsa_ref[...].astype(BF16), wbn_ref[...])
    br_c = _dot(ssd_ref[...].astype(BF16), wbs_ref[...])
    mixed = (jax.nn.sigmoid(ga_ref[...]) * br_a + jax.nn.sigmoid(gb_ref[...]) * br_b
             + jax.nn.sigmoid(gc_ref[...]) * br_c)
    mix = _dot(mixed.astype(BF16), wo_ref[...])
    _store_rows(o_ref, _layer_norm(DN_ALPHA * _load_rows(h_ref) + mix, lg_ref[...], lb_ref[...]))


def _mix(proj2d, nsa_out, ssd_y, h3, bsz, s, pool_w, pool_scale, w_br_pool, w_br_nsa, w_br_ssd,
         w_out, ln_g, ln_b):
    ts = MIX_TS
    ns = s // ts
    t = bsz * s
    d = D_MODEL
    w0 = MAX_POOL_WINDOW
    wbn = w_br_nsa.reshape(NSA_KV_HEADS, NSA_HPG, NSA_HEAD_DIM, d).transpose(1, 0, 2, 3).reshape(NSA_WIDTH, d)
    full = lambda shape: pl.BlockSpec(shape, lambda b, si: (0,) * len(shape))
    row = lambda width, cb: pl.BlockSpec((ts, width), lambda b, si: (b * ns + si, cb))
    return pl.pallas_call(
        functools.partial(_mix_kernel, ts=ts),
        out_shape=jax.ShapeDtypeStruct((t,) + ROW_TILE, F32),
        grid=(bsz, ns),
        in_specs=[row(POOL_WIDTH, C_POOL // POOL_WIDTH),
                  pl.BlockSpec((w0, POOL_WIDTH),
                               lambda b, si: (jnp.maximum((b * s + si * ts) // w0 - 1, 0), C_POOL // POOL_WIDTH)),
                  row(NSA_WIDTH, 0), row(d, 0),
                  row(d, C_GATE // d), row(d, C_GATE // d + 1), row(d, C_GATE // d + 2),
                  _row_spec(ts, lambda b, si: b * ns + si),
                  full((len(POOL_WINDOWS), POOL_GROUP, POOL_GROUP)), full((1, POOL_WIDTH)),
                  full((POOL_WIDTH, d)), full((NSA_WIDTH, d)), full((d, d)), full((d, d)),
                  full((1, d)), full((1, d))],
        out_specs=_row_spec(ts, lambda b, si: b * ns + si),
        compiler_params=_cparams(("parallel", "arbitrary")),
        name="mix",
    )(proj2d, proj2d, nsa_out, ssd_y, proj2d, proj2d, proj2d, h3,
      pool_w.astype(BF16), pool_scale.reshape(1, POOL_WIDTH), w_br_pool.astype(BF16), wbn.astype(BF16),
      w_br_ssd.astype(BF16), w_out.astype(BF16), ln_g.reshape(1, d), ln_b.reshape(1, d))


RANK_CHUNK = 256
BUCKET_ROWS = 32


def _route(hb, first_step, rw_ref, rb_ref, tri_ref, bucket_ref, wlo_ref, whi_ref, rank_ref, cnt_ref, run_ref):
    @pl.when(first_step)
    def _():
        run_ref[...] = jnp.zeros(run_ref.shape, F32)

    logits = _dot_nt(rw_ref[...], hb)
    lg = [logits[e:e + 1, :] for e in range(N_EXPERTS)]
    mx = lg[0]
    for e in range(1, N_EXPERTS):
        mx = jnp.maximum(mx, lg[e])
    ex = [jnp.exp(v - mx) for v in lg]
    den = ex[0]
    for e in range(1, N_EXPERTS):
        den = den + ex[e]
    probs = [v / den for v in ex]
    score = [probs[e] + rb_ref[e:e + 1, :] for e in range(N_EXPERTS)]

    def group_vals(vals, grp):
        out = []
        for k in range(EXPERTS_PER_GROUP):
            v = vals[k]
            for gi in range(1, N_EXPERT_GROUPS):
                v = jnp.where(grp == gi, vals[gi * EXPERTS_PER_GROUP + k], v)
            out.append(v)
        return out

    gscore = []
    for gi in range(N_EXPERT_GROUPS):
        sc = score[gi * EXPERTS_PER_GROUP:(gi + 1) * EXPERTS_PER_GROUP]
        best = None
        for a, b in PAIRS:
            v = sc[a] + sc[b]
            best = v if best is None else jnp.maximum(best, v)
        gscore.append(best)
    grp = jnp.zeros(gscore[0].shape, jnp.int32)
    best = gscore[0]
    for gi in range(1, N_EXPERT_GROUPS):
        better = gscore[gi] > best
        grp = jnp.where(better, gi, grp)
        best = jnp.where(better, gscore[gi], best)

    sc = group_vals(score, grp)
    pr = group_vals(probs, grp)
    first = jnp.zeros(grp.shape, jnp.int32)
    fv = sc[0]
    for k in range(1, EXPERTS_PER_GROUP):
        better = sc[k] > fv
        first = jnp.where(better, k, first)
        fv = jnp.where(better, sc[k], fv)
    second = jnp.full(grp.shape, -1, jnp.int32)
    sv = jnp.full(fv.shape, -jnp.inf, F32)
    for k in range(EXPERTS_PER_GROUP):
        better = (first != k) & ((sc[k] > sv) | (second < 0))
        second = jnp.where(better, k, second)
        sv = jnp.where(better, sc[k], sv)
    lo = jnp.minimum(first, second)
    hi = jnp.maximum(first, second)
    p_lo = pr[0]
    p_hi = pr[0]
    for k in range(1, EXPERTS_PER_GROUP):
        p_lo = jnp.where(lo == k, pr[k], p_lo)
        p_hi = jnp.where(hi == k, pr[k], p_hi)
    tot = p_lo + p_hi
    pair = jnp.zeros(grp.shape, jnp.int32)
    for pi, (a, b) in enumerate(PAIRS):
        pair = jnp.where((lo == a) & (hi == b), pi, pair)
    bucket = grp * len(PAIRS) + pair
    bucket_ref[...] = bucket
    wlo_ref[...] = p_lo / tot
    whi_ref[...] = p_hi / tot

    tm = bucket.shape[1]
    onehot = jnp.where(lax.broadcasted_iota(jnp.int32, (BUCKET_ROWS, tm), 0) == bucket, 1.0, 0.0)
    run = run_ref[...]
    ranks = []
    for c in range(tm // RANK_CHUNK):
        oc = onehot[:, c * RANK_CHUNK:(c + 1) * RANK_CHUNK]
        before = _dot(oc.astype(BF16), tri_ref[...]) + jnp.concatenate([run] * (RANK_CHUNK // LANE), axis=1)
        ranks.append(jnp.sum(oc * before, axis=0, keepdims=True))
        run = run + jnp.broadcast_to(jnp.sum(oc, axis=1, keepdims=True), run.shape)
    rank_ref[...] = jnp.concatenate(ranks, axis=1).astype(jnp.int32)
    run_ref[...] = run
    cnt_ref[...] = run


ROUTER_TM = 1024


def _router_kernel(h_ref, *refs):
    _route(_load_rows(h_ref).astype(BF16), pl.program_id(0) == 0, *refs)


def _router(h2d, router_w, router_b):
    t, d = h2d.shape[0], D_MODEL
    tm = ROUTER_TM
    rw = jnp.pad(router_w.T, ((0, LANE - N_EXPERTS), (0, 0))).astype(BF16)
    rb = jnp.broadcast_to(jnp.pad(router_b, (0, LANE - N_EXPERTS))[:, None], (LANE, tm))
    tri = jnp.asarray(np.triu(np.ones((RANK_CHUNK, RANK_CHUNK), np.float32), 1), BF16)
    vec = lambda dt: jax.ShapeDtypeStruct((1, t), dt)
    row = pl.BlockSpec((1, tm), lambda i: (0, i))
    return pl.pallas_call(
        _router_kernel,
        out_shape=(vec(jnp.int32), vec(F32), vec(F32), vec(jnp.int32),
                   jax.ShapeDtypeStruct((BUCKET_ROWS, LANE), F32)),
        grid=(t // tm,),
        in_specs=[_row_spec(tm, lambda i: i),
                  pl.BlockSpec((LANE, d), lambda i: (0, 0)),
                  pl.BlockSpec((LANE, tm), lambda i: (0, 0)),
                  pl.BlockSpec((RANK_CHUNK, RANK_CHUNK), lambda i: (0, 0))],
        out_specs=(row, row, row, row, pl.BlockSpec((BUCKET_ROWS, LANE), lambda i: (0, 0))),
        scratch_shapes=[pltpu.VMEM((BUCKET_ROWS, LANE), F32)],
        compiler_params=_cparams(("arbitrary",)),
        name="router",
    )(h2d, rw, rb, tri)


def _moe_plan(bucket, rank, counts, w_lo, w_hi, t):
    tm = MOE_TM
    n_tiles = t // tm + N_BUCKETS
    p_rows = n_tiles * tm
    tiles_per = (counts + tm - 1) // tm
    tile_end = jnp.cumsum(tiles_per)
    tile_start = tile_end - tiles_per
    in_bucket = bucket[:, None] == jnp.arange(N_BUCKETS)[None, :]
    dest = jnp.sum(jnp.where(in_bucket, (tile_start * tm)[None, :], 0), axis=1) + rank
    payload = jnp.stack([jnp.arange(t, dtype=jnp.int32), lax.bitcast_convert_type(w_lo, jnp.int32),
                         lax.bitcast_convert_type(w_hi, jnp.int32), jnp.ones((t,), jnp.int32)], axis=1)
    plan = jnp.zeros((p_rows, 4), jnp.int32).at[dest].set(payload, unique_indices=True)
    src = plan[:, 0]
    wl = lax.bitcast_convert_type(plan[:, 1], F32)
    wh = lax.bitcast_convert_type(plan[:, 2], F32)
    prow = jnp.arange(p_rows, dtype=jnp.int32)
    spare = t + ((prow // tm) % 2) * tm + prow % tm
    dst = jnp.where(plan[:, 3] > 0, src, spare)
    n_used = tile_end[-1]
    tile_ids = jnp.arange(n_tiles)
    tb = jnp.sum((tile_end[None, :] <= tile_ids[:, None]).astype(jnp.int32), axis=1)
    tb = jnp.where(tile_ids < n_used, tb, tb[jnp.maximum(n_used - 1, 0)])
    tb = jnp.minimum(tb, N_BUCKETS - 1)
    pairs = np.asarray(PAIRS, np.int32)
    grp = tb // len(PAIRS)
    ea = grp * EXPERTS_PER_GROUP + jnp.asarray(pairs[:, 0])[tb % len(PAIRS)]
    eb = grp * EXPERTS_PER_GROUP + jnp.asarray(pairs[:, 1])[tb % len(PAIRS)]
    nvalid = jnp.clip(counts[tb] - (tile_ids - tile_start[tb]) * tm, 0, tm)
    nvalid = jnp.where(tile_ids < n_used, nvalid, 0).astype(jnp.int32)
    return (src.reshape(n_tiles, 1, tm), dst.reshape(n_tiles, 1, tm), wl.reshape(p_rows, 1), wh.reshape(p_rows, 1),
            ea.astype(jnp.int32), eb.astype(jnp.int32), nvalid, n_tiles)


MOE_DMA_GROUPS = 8


def _moe_kernel(ea_ref, eb_ref, nv_ref, src0_ref, srcn_ref, dstp_ref, wl_ref, wh_ref, h_hbm,
                w1a_ref, w3a_ref, w2a_ref, w1b_ref, w3b_ref, w2b_ref, lg_ref, lb_ref,
                o_hbm, xbuf0, xbuf1, obuf0, obuf1, wa1, wa3, wa2, wb1, wb3, wb2, gsem, ssem):
    i = pl.program_id(0)
    tm = xbuf0.shape[0]
    xbufs = (xbuf0, xbuf1)
    obufs = (obuf0, obuf1)
    nv = nv_ref[i]
    nv_p1 = nv_ref[jnp.maximum(i - 1, 0)]
    odd = lax.rem(i, 2) == 1
    even = jnp.logical_not(odd)

    def gather(r, p, row):
        return pltpu.make_async_copy(h_hbm.at[pl.ds(row, 1)], xbufs[p].at[pl.ds(r, 1)], gsem.at[p])

    def scatter(r, p, row):
        return pltpu.make_async_copy(obufs[p].at[pl.ds(r, 1)], o_hbm.at[pl.ds(row, 1)], ssem.at[p])

    def wait_gather(p):
        for r in range(tm):
            gather(r, p, 0).wait()

    def wait_scatter(p):
        for r in range(tm):
            scatter(r, p, 0).wait()

    def tile(p, age):
        q = 1 - p
        half = MOE_DMA_GROUPS // 2
        per = tm // half

        def issue(k):
            for r in range((k % half) * per, (k % half + 1) * per):
                if k < half:
                    gather(r, q, srcn_ref[0, 0, r]).start()
                elif age >= 1:
                    scatter(r, q, dstp_ref[0, 0, r]).start()

        wait_gather(p)
        x = _load_rows(xbufs[p])
        xb = x.astype(BF16)
        issue(0)
        h1 = _dot(xb, wa1[...])
        issue(1)
        h3 = _dot(xb, wa3[...])
        issue(2)
        ya = _dot((jax.nn.silu(h1) * h3).astype(BF16), wa2[...])
        issue(3)
        h1 = _dot(xb, wb1[...])
        issue(4)
        h3 = _dot(xb, wb3[...])
        issue(5)
        yb = _dot((jax.nn.silu(h1) * h3).astype(BF16), wb2[...])
        issue(6)
        y = wl_ref[...] * ya + wh_ref[...] * yb
        issue(7)
        out = _layer_norm(DN_ALPHA * x + y, lg_ref[...], lb_ref[...])
        if age >= 2:
            wait_scatter(p)
        _store_rows(obufs[p], out)

    @pl.when(i == 0)
    def _():
        obuf1[...] = jnp.zeros(obuf1.shape, F32)
        n_real = o_hbm.shape[0] - 2 * tm
        for blk in range(2):
            cp = pltpu.make_async_copy(obuf1, o_hbm.at[pl.ds(n_real + blk * tm, tm)], ssem.at[1])
            cp.start()
            cp.wait()

    prev = jnp.maximum(i - 1, 0)

    @pl.when((nv > 0) & ((i == 0) | (ea_ref[i] != ea_ref[prev])))
    def _():
        wa1[...] = w1a_ref[0].astype(BF16)
        wa3[...] = w3a_ref[0].astype(BF16)
        wa2[...] = w2a_ref[0].astype(BF16)

    @pl.when((nv > 0) & ((i == 0) | (eb_ref[i] != eb_ref[prev])))
    def _():
        wb1[...] = w1b_ref[0].astype(BF16)
        wb3[...] = w3b_ref[0].astype(BF16)
        wb2[...] = w2b_ref[0].astype(BF16)

    @pl.when((nv > 0) & (i == 0))
    def _():
        def start0(r, carry):
            gather(r, 0, src0_ref[0, 0, r]).start()
            return carry
        lax.fori_loop(0, tm, start0, 0)
        tile(0, 0)

    pl.when((nv > 0) & (i == 1))(functools.partial(tile, 1, 1))
    pl.when((nv > 0) & (i >= 2) & even)(functools.partial(tile, 0, 2))
    pl.when((nv > 0) & (i >= 2) & odd)(functools.partial(tile, 1, 2))

    def tail(p):
        q = 1 - p
        wait_gather(p)

        @pl.when(i >= 2)
        def _():
            wait_scatter(p)

        def start(r, carry):
            scatter(r, q, dstp_ref[0, 0, r]).start()
            return carry
        lax.fori_loop(0, tm, start, 0)
        wait_scatter(q)

    is_tail = (nv == 0) & (i >= 1) & (nv_p1 > 0)
    pl.when(is_tail & even)(functools.partial(tail, 0))
    pl.when(is_tail & odd)(functools.partial(tail, 1))


def _moe(h3, bucket, rank, counts, w_lo, w_hi, w1, w3, w2, ln_g, ln_b):
    t, d = h3.shape[0], D_MODEL
    tm = MOE_TM
    assert tm % MOE_DMA_GROUPS == 0
    src, dst, wl, wh, ea, eb, nvalid, n_tiles = _moe_plan(bucket, rank, counts, w_lo, w_hi, t)
    wspec = lambda shape, which: pl.BlockSpec(
        shape, (lambda i, ea_r, eb_r, nv_r: (ea_r[i], 0, 0)) if which == 0
        else (lambda i, ea_r, eb_r, nv_r: (eb_r[i], 0, 0)))
    w13 = (1, d, D_EXPERT)
    w2s = (1, D_EXPERT, d)
    smem_tile = lambda fn: pl.BlockSpec((1, 1, tm), lambda i, *_: (fn(i), 0, 0), memory_space=pltpu.SMEM)
    grid_spec = pltpu.PrefetchScalarGridSpec(
        num_scalar_prefetch=3,
        grid=(n_tiles,),
        in_specs=[smem_tile(lambda i: 0),
                  smem_tile(lambda i: jnp.minimum(i + 1, n_tiles - 1)),
                  smem_tile(lambda i: jnp.maximum(i - 1, 0)),
                  pl.BlockSpec((tm, 1), lambda i, *_: (i, 0)),
                  pl.BlockSpec((tm, 1), lambda i, *_: (i, 0)),
                  pl.BlockSpec(memory_space=pl.ANY),
                  wspec(w13, 0), wspec(w13, 0), wspec(w2s, 0),
                  wspec(w13, 1), wspec(w13, 1), wspec(w2s, 1),
                  pl.BlockSpec((1, d), lambda i, *_: (0, 0)),
                  pl.BlockSpec((1, d), lambda i, *_: (0, 0))],
        out_specs=pl.BlockSpec(memory_space=pl.ANY),
        scratch_shapes=[pltpu.VMEM((tm,) + ROW_TILE, F32)] * 4
        + [pltpu.VMEM(w13[1:], BF16), pltpu.VMEM(w13[1:], BF16), pltpu.VMEM(w2s[1:], BF16)] * 2
        + [pltpu.SemaphoreType.DMA((2,)), pltpu.SemaphoreType.DMA((2,))],
    )
    return pl.pallas_call(
        _moe_kernel,
        out_shape=jax.ShapeDtypeStruct((t + 2 * tm,) + ROW_TILE, F32),
        grid_spec=grid_spec,
        compiler_params=_cparams(("arbitrary",)),
        name="moe",
    )(ea, eb, nvalid, src, src, dst, wl, wh, h3, w1, w3, w2, w1, w3, w2,
      ln_g.reshape(1, d), ln_b.reshape(1, d))


def kernel(x, ln0_g, ln0_b, w_in, pool_w, pool_scale, cmp_k_pe, cmp_k_w1, cmp_k_w2, cmp_v_pe, cmp_v_w1, cmp_v_w2, rel_bias, conv_w, conv_b, dt_bias, a_log, d_skip, ssd_norm_g, w_br_pool, w_br_nsa, w_br_ssd, w_out, ln1_g, ln1_b, router_w, router_b, exp_w1, exp_w3, exp_w2, ln2_g, ln2_b):
    bsz, s, d = x.shape
    assert d == D_MODEL and s % MIX_TS == 0 and s // CMP_STRIDE == LANE and (bsz * s) % PROJ_TM == 0
    t = bsz * s
    ts, ws, bc = _nsa_bias_tables(rel_bias)
    et, ovt = _nsa_consts(s)
    h = x.reshape(t, d)
    for i in range(DEPTH):
        if i == 0:
            proj, h = _proj(h, _prep_w_in(w_in[i]), t, ln=(ln0_g, ln0_b))
        else:
            proj = _proj(h, _prep_w_in(w_in[i]), t)
        kc, vc = _compress(proj, bsz, s, cmp_k_pe[i], cmp_k_w1[i], cmp_k_w2[i],
                           cmp_v_pe[i], cmp_v_w1[i], cmp_v_w2[i])
        nsa_out = _nsa(proj, kc, vc, ts, ws, bc, et, ovt, bsz, s)
        ssd_y = _ssd(proj, bsz, s, conv_w[i], conv_b[i], dt_bias[i], a_log[i], d_skip[i], ssd_norm_g[i])
        h1 = _mix(proj, nsa_out, ssd_y, h, bsz, s, pool_w[i], pool_scale[i], w_br_pool[i], w_br_nsa[i],
                  w_br_ssd[i], w_out[i], ln1_g[i], ln1_b[i])
        bucket, w_lo, w_hi, rank, cnt = _router(h1, router_w, router_b)
        counts = cnt[:N_BUCKETS, 0].astype(jnp.int32)
        h = _moe(h1, bucket[0], rank[0], counts, w_lo[0], w_hi[0], exp_w1[i], exp_w3[i], exp_w2[i],
                 ln2_g[i], ln2_b[i])
    return h[:t].reshape(bsz, s, d)
```

```python
import functools
import math

import numpy as np
import jax
import jax.numpy as jnp
from jax import lax
from jax.experimental import pallas as pl
from jax.experimental.pallas import tpu as pltpu

F32 = jnp.float32
BF16 = jnp.bfloat16

D_MODEL = 1024
DEPTH = 2
DN_ALPHA = (2.0 * DEPTH) ** 0.25
LN_EPS = 1e-5
NEG = -1e30
BIG = 1e6

POOL_WINDOWS = (2, 4, 8, 16)
POOL_GROUP = 128
POOL_WIDTH = 512
MAX_POOL_WINDOW = 16

NSA_HEAD_DIM = 64
NSA_KV_HEADS = 2
NSA_HPG = 4
NSA_HEADS = 8
NSA_WIDTH = 512
CMP_LEN = 32
CMP_STRIDE = 16
CMP_HIDDEN = 128
SLC_LEN = 64
SLC_TOPN = 8
WIN_LEN = 512
Q_BLOCK = 128
REL_BUCKETS = 32
REL_MAX_DIST = 128

SSD_D_INNER = 1024
SSD_HEAD_DIM = 64
SSD_HEADS = 16
SSD_GROUPS = 4
SSD_STATE = 128
SSD_CONV = 4
SSD_CHUNK = 128
SSD_XBC = 2048

N_EXPERTS = 16
N_EXPERT_GROUPS = 4
EXPERTS_PER_GROUP = 4
D_EXPERT = 512
PAIRS = ((0, 1), (0, 2), (0, 3), (1, 2), (1, 3), (2, 3))
N_BUCKETS = N_EXPERT_GROUPS * len(PAIRS)

LANE = 128
SUBLANE = 8
VMEM_LIMIT = 48 * 1024 * 1024

C_POOL = 0
C_Q = 512
C_KV = 1024
C_NG = 1792
C_DT = 1920
C_XBC = 2048
C_Z = 4096
C_GATE = 5120
D_PAD = 8192

PROJ_TM = 1024
PROJ_TN = 2048
MIX_TS = 256
MOE_TM = 256


def _dot(a, b):
    return jnp.dot(a, b, preferred_element_type=F32)


def _dot_nt(a, b):
    return lax.dot_general(a, b, (((1,), (1,)), ((), ())), preferred_element_type=F32)


def _dot_tn(a, b):
    return lax.dot_general(a, b, (((0,), (0,)), ((), ())), preferred_element_type=F32)


def _split3(x):
    x1 = x.astype(BF16)
    r1 = x - x1.astype(F32)
    x2 = r1.astype(BF16)
    x3 = (r1 - x2.astype(F32)).astype(BF16)
    return x1, x2, x3


def _layer_norm(x, g, b):
    mu = jnp.mean(x, -1, keepdims=True)
    xc = x - mu
    var = jnp.mean(xc * xc, -1, keepdims=True)
    return xc * lax.rsqrt(var + LN_EPS) * g + b


def _cparams(sem):
    return pltpu.CompilerParams(dimension_semantics=sem, vmem_limit_bytes=VMEM_LIMIT)


ROW_TILE = (D_MODEL,)


def _load_rows(ref):
    return ref[...]


def _store_rows(ref, val):
    ref[...] = val


def _row_spec(n, index):
    return pl.BlockSpec((n,) + ROW_TILE, lambda *ids: (index(*ids), 0))


def _proj_kernel(h_ref, w_ref, o_ref, hb_ref):
    @pl.when(pl.program_id(1) == 0)
    def _():
        hb_ref[...] = _load_rows(h_ref).astype(BF16)

    o_ref[...] = _dot(hb_ref[...], w_ref[...])


def _ln_proj_kernel(x_ref, g_ref, b_ref, w_ref, o_ref, h_ref, hb_ref):
    @pl.when(pl.program_id(1) == 0)
    def _():
        h = _layer_norm(x_ref[...], g_ref[...], b_ref[...])
        _store_rows(h_ref, h)
        hb_ref[...] = h.astype(BF16)

    o_ref[...] = _dot(hb_ref[...], w_ref[...])


def _proj(h2d, w_pad, t, ln=None):
    d = D_MODEL
    grid = (t // PROJ_TM, D_PAD // PROJ_TN)
    w_spec = pl.BlockSpec((d, PROJ_TN), lambda i, j: (0, j))
    o_spec = pl.BlockSpec((PROJ_TM, PROJ_TN), lambda i, j: (i, j))
    o_shape = jax.ShapeDtypeStruct((t, D_PAD), F32)
    scratch = [pltpu.VMEM((PROJ_TM, d), BF16)]
    params = _cparams(("parallel", "arbitrary"))
    if ln is None:
        return pl.pallas_call(
            _proj_kernel, out_shape=o_shape, grid=grid,
            in_specs=[_row_spec(PROJ_TM, lambda i, j: i), w_spec], out_specs=o_spec,
            scratch_shapes=scratch, compiler_params=params, name="proj",
        )(h2d, w_pad)
    vec = pl.BlockSpec((1, d), lambda i, j: (0, 0))
    return pl.pallas_call(
        _ln_proj_kernel, out_shape=(o_shape, jax.ShapeDtypeStruct((t,) + ROW_TILE, F32)), grid=grid,
        in_specs=[_row_spec(PROJ_TM, lambda i, j: i), vec, vec, w_spec],
        out_specs=(o_spec, _row_spec(PROJ_TM, lambda i, j: i)),
        scratch_shapes=scratch, compiler_params=params, name="ln_proj",
    )(h2d, ln[0].reshape(1, d), ln[1].reshape(1, d), w_pad)


def _pad_cols(w, n):
    return jnp.pad(w, ((0, 0), (0, n - w.shape[1])))


def _prep_w_in(w_in):
    d = w_in.shape[0]
    o = 0
    w_pool = w_in[:, o:o + POOL_WIDTH]; o += POOL_WIDTH
    w_q = w_in[:, o:o + NSA_WIDTH]; o += NSA_WIDTH
    w_kv = w_in[:, o:o + 768]; o += 768
    w_ng = w_in[:, o:o + 24]; o += 24
    w_z = w_in[:, o:o + SSD_D_INNER]; o += SSD_D_INNER
    w_xbc = w_in[:, o:o + SSD_XBC]; o += SSD_XBC
    w_dt = w_in[:, o:o + SSD_HEADS]; o += SSD_HEADS
    w_gate = w_in[:, o:o + 3 * D_MODEL]
    w_q = w_q.reshape(d, NSA_KV_HEADS, NSA_HPG, NSA_HEAD_DIM).transpose(0, 2, 1, 3).reshape(d, NSA_WIDTH)
    w = jnp.concatenate([w_pool, w_q, w_kv, _pad_cols(w_ng, LANE), _pad_cols(w_dt, LANE),
                         w_xbc, w_z, w_gate], axis=1)
    return w.astype(BF16)


def _compress_one(tok_ref, pe_ref, w1_ref, w2_ref):
    n16 = tok_ref.shape[0] // CMP_STRIDE
    acc_a = jnp.zeros((n16, 2 * CMP_HIDDEN), F32)
    acc_b = jnp.zeros((n16, 2 * CMP_HIDDEN), F32)
    for l in range(CMP_STRIDE):
        rows = tok_ref[pl.ds(l, n16, stride=CMP_STRIDE), :]
        xa = (rows + pe_ref[l:l + 1, :]).astype(BF16)
        xb = (rows + pe_ref[CMP_STRIDE + l:CMP_STRIDE + l + 1, :]).astype(BF16)
        acc_a = acc_a + _dot(xa, w1_ref[l])
        acc_b = acc_b + _dot(xb, w1_ref[CMP_STRIDE + l])
    hid = acc_a + pltpu.roll(acc_b, n16 - 1, axis=0)
    hid = jax.nn.gelu(hid)
    return _dot(hid.astype(BF16), w2_ref[...])


def _compress_kernel(kt_ref, vt_ref, pek_ref, w1k_ref, w2k_ref, pev_ref, w1v_ref, w2v_ref,
                     kc_ref, vc_ref):
    kc_ref[0] = _compress_one(kt_ref, pek_ref, w1k_ref, w2k_ref).astype(BF16)
    vc_ref[0] = _compress_one(vt_ref, pev_ref, w1v_ref, w2v_ref).T.astype(BF16)


def _blockdiag2(w):
    z = jnp.zeros_like(w)
    return jnp.concatenate([jnp.concatenate([w, z], -1), jnp.concatenate([z, w], -1)], -2)


def _compress(proj2d, bsz, s, pe_k, w1_k, w2_k, pe_v, w1_v, w2_v):
    n16 = s // CMP_STRIDE

    def prep(pe, w1, w2):
        pe2 = jnp.concatenate([pe, pe], axis=-1)
        w1b = _blockdiag2(w1.reshape(CMP_LEN, NSA_HEAD_DIM, CMP_HIDDEN)).astype(BF16)
        w2b = _blockdiag2(w2).astype(BF16)
        return pe2, w1b, w2b

    pk = prep(pe_k, w1_k, w2_k)
    pv = prep(pe_v, w1_v, w2_v)
    full = lambda shape: pl.BlockSpec(shape, lambda b: (0,) * len(shape))
    wspecs = [full((CMP_LEN, LANE)), full((CMP_LEN, LANE, 2 * CMP_HIDDEN)), full((2 * CMP_HIDDEN, LANE))]
    return pl.pallas_call(
        _compress_kernel,
        out_shape=(jax.ShapeDtypeStruct((bsz, n16, LANE), BF16),
                   jax.ShapeDtypeStruct((bsz, n16, LANE), BF16)),
        grid=(bsz,),
        in_specs=[pl.BlockSpec((s, LANE), lambda b: (b, C_KV // LANE)),
                  pl.BlockSpec((s, LANE), lambda b: (b, C_KV // LANE + 1))] + wspecs + wspecs,
        out_specs=(pl.BlockSpec((1, n16, LANE), lambda b: (b, 0, 0)),
                   pl.BlockSpec((1, n16, LANE), lambda b: (b, 0, 0))),
        compiler_params=_cparams(("parallel",)),
        name="nsa_compress",
    )(proj2d, proj2d, *pk, *pv)


def _rel_bucket(dist):
    n = jnp.maximum(dist, 0)
    max_exact = REL_BUCKETS // 2
    nf = jnp.maximum(n, 1).astype(F32)
    large = max_exact + (jnp.log(nf / max_exact) / math.log(REL_MAX_DIST / max_exact)
                         * (REL_BUCKETS - max_exact)).astype(jnp.int32)
    large = jnp.minimum(large, REL_BUCKETS - 1)
    return jnp.where(n < max_exact, n, large)


LOG2E = math.log2(math.e)
WIN_BACK = WIN_LEN // Q_BLOCK
WIN_KEYS = (WIN_BACK + 1) * Q_BLOCK
WIN_STRIP_TILES = 2 * WIN_BACK + 1
SEL_TILES = 4
SEL_KEYS = SEL_TILES * Q_BLOCK
SEL_STRIP_TILES = 3 * SEL_TILES - 1
CMP_BIAS_ROWS = 2 * LANE
PV_ROWS = NSA_HEAD_DIM + 16


def _nsa_bias_tables(rel_bias):
    tbl = rel_bias.astype(F32)
    far = tbl[REL_BUCKETS - 1]
    hq = NSA_HPG * Q_BLOCK

    def lookup(dist):
        onehot = (_rel_bucket(jnp.asarray(dist))[..., None] == jnp.arange(REL_BUCKETS)).astype(F32)
        return jnp.einsum("...b,bh->...h", onehot, tbl, precision=lax.Precision.HIGHEST)

    k = np.arange(Q_BLOCK)[None, :, None]
    q = np.arange(Q_BLOCK)[None, None, :]

    def strip(first, last, visible):
        n = first - last + 1
        dist = np.arange(first, last - 1, -1)[:, None, None] * Q_BLOCK + q - k
        v = jnp.where(jnp.asarray(visible(dist))[..., None], (lookup(dist) - far) * LOG2E, NEG)
        v = v.reshape(n, Q_BLOCK, Q_BLOCK, NSA_KV_HEADS, NSA_HPG).transpose(3, 0, 1, 4, 2)
        return v.reshape(NSA_KV_HEADS, n * Q_BLOCK, hq)

    ts = strip(2 * SEL_TILES - 1, 1 - SEL_TILES, lambda d: d >= 0)
    ws = strip(WIN_BACK, -WIN_BACK, lambda d: (d >= 0) & (d < WIN_LEN))
    r = np.arange(CMP_BIAS_ROWS)[:, None]
    dist_c = q[0] - CMP_STRIDE * (r - LANE) - (CMP_LEN - 1)
    bc = jnp.where(jnp.asarray(dist_c >= 0)[..., None], lookup(dist_c) * LOG2E, NEG)
    bc = bc.reshape(CMP_BIAS_ROWS, Q_BLOCK, NSA_KV_HEADS, NSA_HPG).transpose(2, 0, 3, 1)
    bc = bc.reshape(NSA_KV_HEADS, CMP_BIAS_ROWS, hq)
    return ts, ws, bc


def _nsa_consts(s):
    n_slc = s // SLC_LEN
    et = (np.arange(s)[:, None] // SLC_LEN == np.arange(LANE)[None, :]).astype(np.float32)
    n_cmp = (s - CMP_LEN) // CMP_STRIDE + 1
    c0 = np.arange(n_cmp) * CMP_STRIDE
    s0 = np.arange(n_slc) * SLC_LEN
    ov = np.clip(np.minimum(c0[None, :] + CMP_LEN, s0[:, None] + SLC_LEN)
                 - np.maximum(c0[None, :], s0[:, None]), 0, None) / CMP_LEN
    ovt = np.zeros((n_slc, LANE), np.float32)
    ovt[:, :n_cmp] = ov
    return jnp.asarray(et, BF16), jnp.asarray(ovt, BF16)


def _nsa_kernel(q_ref, ng_ref, ks_ref, vs_ref, kw_ref, vw_ref, kc_ref, vct_ref,
                ts_ref, ws_ref, bc_ref, et_ref, ovt_ref, o_ref,
                ksb_ref, kwb_ref, vst_ref, vwt_ref, os_ref, *, nq, n_slc):
    qi = pl.program_id(1)
    hq = NSA_HPG * Q_BLOCK
    dh = NSA_HEAD_DIM
    groups = range(NSA_KV_HEADS)
    gsl = [slice(g * dh, (g + 1) * dh) for g in groups]

    @pl.when(qi == 0)
    def _():
        ksb_ref[:, 0:LANE] = ks_ref[...].astype(BF16)
        ksb_ref[:, LANE:] = et_ref[...]
        kwb_ref[...] = kw_ref[...].astype(BF16)
        ones_rows = jnp.where(lax.broadcasted_iota(jnp.int32, (PV_ROWS - dh, ks_ref.shape[0]), 0) == 0, 1.0, 0.0)
        for g in groups:
            vst_ref[g, dh:, :] = ones_rows.astype(BF16)
            vwt_ref[g, dh:, :] = ones_rows.astype(BF16)
        for kt in range(nq):
            sl = slice(kt * Q_BLOCK, (kt + 1) * Q_BLOCK)
            vs_t = vs_ref[sl, :].T
            vw_t = vw_ref[sl, :].T
            for g in groups:
                vst_ref[g, 0:dh, sl] = vs_t[gsl[g], :].astype(BF16)
                vwt_ref[g, 0:dh, sl] = vw_t[gsl[g], :].astype(BF16)

    row_lo = lax.broadcasted_iota(jnp.int32, (LANE, hq), 0) < dh

    def heads4(m):
        return jnp.concatenate([m] * NSA_HPG, axis=1)

    gates_t = jax.nn.sigmoid(ng_ref[...]).T
    q_t = jnp.concatenate(
        [(q_ref[:, h * LANE:(h + 1) * LANE] * (dh ** -0.5 * LOG2E)).T for h in range(NSA_HPG)], axis=1)
    qg_t = [jnp.where(row_lo if g == 0 else jnp.logical_not(row_lo), q_t, 0.0).astype(BF16) for g in groups]

    boff = pl.multiple_of(LANE - SUBLANE * qi, SUBLANE)
    jrow = lax.broadcasted_iota(jnp.int32, (n_slc, LANE), 0)
    tj = qi * Q_BLOCK + lax.broadcasted_iota(jnp.int32, (n_slc, LANE), 1)
    jt = lax.shift_right_logical(tj, SLC_LEN.bit_length() - 1)
    forced = (jrow == 0) | (jrow == jt) | (jrow == jt - 1)
    valid = jrow * SLC_LEN <= tj
    o_c = []
    q_aug = []
    for g in groups:
        s = _dot(kc_ref[0], qg_t[g]) + bc_ref[g, pl.ds(boff, LANE), :]
        mx = jnp.max(s, axis=0, keepdims=True)
        p = jnp.exp2(s - mx)
        lsum = jnp.sum(p, axis=0, keepdims=True)
        pb = (p * jnp.where(mx > 0.5 * NEG, 1.0 / lsum, 0.0)).astype(BF16)
        o_c.append(_dot(vct_ref[0, gsl[g], :], pb))
        imp4 = _dot(ovt_ref[...], pb)
        imp = imp4[:, 0:LANE]
        for h in range(1, NSA_HPG):
            imp = imp + imp4[:, h * LANE:(h + 1) * LANE]
        val = jnp.where(forced, BIG, jnp.where(valid, imp, -BIG))
        rank = jnp.zeros((n_slc, LANE), F32)
        for i in range(n_slc):
            ci = val[i:i + 1, :]
            rank = rank + jnp.where(ci > val, 1.0, 0.0) + jnp.where(ci == val, (jrow > i).astype(F32), 0.0)
        block_rows = heads4(jnp.where(rank < SLC_TOPN, 0.0, NEG)).astype(BF16)
        q_aug.append(jnp.concatenate(
            [qg_t[g], block_rows, jnp.zeros((LANE - n_slc, hq), BF16)], axis=0))

    n_super = qi // SEL_TILES + 1
    diag = qi - SEL_TILES * (n_super - 1)

    def selected(n):
        nk = n * SEL_KEYS
        lo = max(n - 2, 0) * SEL_KEYS
        toff = pl.multiple_of((2 * SEL_TILES - 1 - diag) * Q_BLOCK - (nk - lo - SEL_KEYS), Q_BLOCK)
        for g in groups:
            parts = [(lo, nk, _dot(ksb_ref[lo:nk, :], q_aug[g]) + ts_ref[g, pl.ds(toff, nk - lo), :])]
            if lo > 0:
                parts.append((0, lo, _dot(ksb_ref[0:lo, :], q_aug[g])))
            m = parts[0][2].max(axis=0, keepdims=True)
            for _, _, s in parts[1:]:
                m = jnp.maximum(m, s.max(axis=0, keepdims=True))
            acc = None
            for a, b, s in parts:
                pv = _dot(vst_ref[g, :, a:b], jnp.exp2(s - m).astype(BF16))
                acc = pv if acc is None else acc + pv
            os_ref[g] = acc[0:dh] * (1.0 / acc[dh:dh + 1])

    w0 = jnp.maximum(qi - WIN_BACK, 0)
    woff = pl.multiple_of(w0 * Q_BLOCK, Q_BLOCK)
    boff_w = pl.multiple_of((WIN_BACK - (qi - w0)) * Q_BLOCK, Q_BLOCK)
    kw_bf = kwb_ref[pl.ds(woff, WIN_KEYS), :]
    o_w = []
    for g in groups:
        s = _dot(kw_bf, qg_t[g]) + ws_ref[g, pl.ds(boff_w, WIN_KEYS), :]
        p = jnp.exp2(s - jnp.max(s, axis=0, keepdims=True)).astype(BF16)
        acc = _dot(vwt_ref[g, :, pl.ds(woff, WIN_KEYS)], p)
        o_w.append(acc[0:dh] * (1.0 / acc[dh:dh + 1]))

    for n in range(1, nq // SEL_TILES + 1):
        pl.when(n_super == n)(functools.partial(selected, n))
    o_s = [os_ref[g] for g in groups]

    for h in range(NSA_HPG):
        hl = slice(h * LANE, (h + 1) * LANE)
        halves = []
        for g in groups:
            c = g * NSA_HPG + h
            halves.append(gates_t[c:c + 1, :] * o_c[g][:, hl]
                          + gates_t[NSA_HEADS + c:NSA_HEADS + c + 1, :] * o_s[g][:, hl]
                          + gates_t[2 * NSA_HEADS + c:2 * NSA_HEADS + c + 1, :] * o_w[g][:, hl])
        o_ref[:, hl] = jnp.concatenate(halves, axis=0).T


def _nsa(proj2d, kc, vct, ts, ws, bc, et, ovt, bsz, s):
    nq = s // Q_BLOCK
    n_slc = s // SLC_LEN
    hq = NSA_HPG * Q_BLOCK
    kvb = C_KV // LANE
    kv_spec = lambda blk: pl.BlockSpec((s, LANE), lambda b, qi: (b, kvb + blk))
    full = lambda shape: pl.BlockSpec(shape, lambda b, qi: (0,) * len(shape), pipeline_mode=pl.Buffered(1))
    return pl.pallas_call(
        functools.partial(_nsa_kernel, nq=nq, n_slc=n_slc),
        out_shape=jax.ShapeDtypeStruct((bsz * s, NSA_WIDTH), F32),
        grid=(bsz, nq),
        in_specs=[pl.BlockSpec((Q_BLOCK, NSA_WIDTH), lambda b, qi: (b * nq + qi, C_Q // NSA_WIDTH)),
                  pl.BlockSpec((Q_BLOCK, LANE), lambda b, qi: (b * nq + qi, C_NG // LANE)),
                  kv_spec(2), kv_spec(3), kv_spec(4), kv_spec(5),
                  pl.BlockSpec((1, s // CMP_STRIDE, LANE), lambda b, qi: (b, 0, 0)),
                  pl.BlockSpec((1, LANE, s // CMP_STRIDE), lambda b, qi: (b, 0, 0)),
                  full((NSA_KV_HEADS, SEL_STRIP_TILES * Q_BLOCK, hq)),
                  full((NSA_KV_HEADS, WIN_STRIP_TILES * Q_BLOCK, hq)),
                  full((NSA_KV_HEADS, CMP_BIAS_ROWS, hq)),
                  full((s, LANE)),
                  full((n_slc, LANE))],
        out_specs=pl.BlockSpec((Q_BLOCK, NSA_WIDTH), lambda b, qi: (b * nq + qi, 0)),
        scratch_shapes=[pltpu.VMEM((s, 2 * LANE), BF16), pltpu.VMEM((s, LANE), BF16),
                        pltpu.VMEM((NSA_KV_HEADS, PV_ROWS, s), BF16), pltpu.VMEM((NSA_KV_HEADS, PV_ROWS, s), BF16),
                        pltpu.VMEM((NSA_KV_HEADS, NSA_HEAD_DIM, hq), F32)],
        compiler_params=_cparams(("parallel", "arbitrary")),
        name="nsa_attn",
    )(proj2d, proj2d, proj2d, proj2d, proj2d, proj2d, kc, vct, ts, ws, bc, et, ovt)


def _ssd_kernel(xbc_ref, prev_ref, z_ref, dt_ref, cw_ref, cb_ref, dtb_ref, alog_ref,
                dskip_ref, ng_ref, e16_ref, tril_ref, o_ref, state_ref):
    c = pl.program_id(1)
    L = SSD_CHUNK
    pair_w = 2 * SSD_HEAD_DIM

    @pl.when(c == 0)
    def _():
        state_ref[...] = jnp.zeros(state_ref.shape, F32)

    xcat = jnp.concatenate([jnp.where(c > 0, prev_ref[...], 0.0), xbc_ref[...]], axis=0)
    acc = cw_ref[0:1, :] * xcat
    for k in range(1, SSD_CONV):
        acc = pltpu.roll(acc, 1, axis=0) + cw_ref[k:k + 1, :] * xcat
    xbc = jax.nn.silu(acc[SUBLANE:, :] + cb_ref[...])
    xs = xbc[:, :SSD_D_INNER]
    bmat = xbc[:, SSD_D_INNER:SSD_D_INNER + SSD_GROUPS * SSD_STATE]
    cmat = xbc[:, SSD_D_INNER + SSD_GROUPS * SSD_STATE:]

    dt = jax.nn.softplus(dt_ref[...] + dtb_ref[...])
    tril = tril_ref[...]
    e16 = e16_ref[...]
    d1, d2, d3 = _split3(dt)
    dt_x = _dot(d1, e16) + _dot(d2, e16) + _dot(d3, e16)
    a_c = dt * (-jnp.exp(alog_ref[...]) * LOG2E)
    c1, c2, c3 = _split3(a_c)
    acs = _dot(tril, c1) + _dot(tril, c2) + _dot(tril, c3)
    x1, x2, x3 = _split3(acs)
    acs_x = _dot(x1, e16) + _dot(x2, e16) + _dot(x3, e16)
    acs_t = acs.T
    last = acs_x[L - 1:L, :]
    eacs = jnp.exp2(acs_x)
    decay_in = jnp.exp2(last - acs_x)
    cdec = jnp.exp2(last)
    xdt = xs * dt_x
    wst = xdt * decay_in

    row = lax.broadcasted_iota(jnp.int32, (L, L), 0)
    lane = lax.broadcasted_iota(jnp.int32, (L, L), 1)
    tri = row >= lane
    lo = lane < SSD_HEAD_DIM

    ys = []
    for g in range(SSD_GROUPS):
        cg = cmat[:, g * SSD_STATE:(g + 1) * SSD_STATE].astype(BF16)
        bg = bmat[:, g * SSD_STATE:(g + 1) * SSD_STATE].astype(BF16)
        cb = _dot_nt(cg, bg)
        for pr in range(2):
            pp = g * 2 + pr
            ls = slice(pp * pair_w, (pp + 1) * pair_w)
            xp = xdt[:, ls]
            yd = None
            for e in range(2):
                hd = 2 * pp + e
                diff = acs[:, hd:hd + 1] - acs_t[hd:hd + 1, :]
                seg = jnp.exp2(jnp.where(tri, diff, NEG))
                pm = (cb * seg).astype(BF16)
                xe = jnp.where(lo if e == 0 else jnp.logical_not(lo), xp, 0.0).astype(BF16)
                term = _dot(pm, xe)
                yd = term if yd is None else yd + term
            st = state_ref[pp]
            yo = _dot(cg, st.astype(BF16)) * eacs[:, ls]
            ys.append(yd + yo)
            state_ref[pp] = cdec[:, ls] * st + _dot_tn(bg, wst[:, ls].astype(BF16))

    y = jnp.concatenate(ys, axis=1) + xs * dskip_ref[...]
    y = y * jax.nn.silu(z_ref[...])
    gw = SSD_D_INNER // SSD_GROUPS
    outs = []
    for gi in range(SSD_GROUPS):
        yg = y[:, gi * gw:(gi + 1) * gw]
        outs.append(yg * lax.rsqrt(jnp.mean(yg * yg, -1, keepdims=True) + LN_EPS))
    o_ref[...] = jnp.concatenate(outs, axis=1) * ng_ref[...]


def _ssd(proj2d, bsz, s, conv_w, conv_b, dt_bias, a_log, d_skip, norm_g):
    nc = s // SSD_CHUNK
    L = SSD_CHUNK
    pad16 = lambda v: jnp.pad(v.reshape(1, SSD_HEADS), ((0, 0), (0, LANE - SSD_HEADS)))
    rep64 = lambda v: jnp.repeat(v, SSD_HEAD_DIM).reshape(1, SSD_D_INNER)
    e16 = np.zeros((LANE, SSD_D_INNER), np.float32)
    e16[np.arange(SSD_D_INNER) // SSD_HEAD_DIM, np.arange(SSD_D_INNER)] = 1.0
    tril = np.tril(np.ones((L, L), np.float32))
    full = lambda shape: pl.BlockSpec(shape, lambda b, c: (0,) * len(shape))
    rows8 = s // SUBLANE
    return pl.pallas_call(
        _ssd_kernel,
        out_shape=jax.ShapeDtypeStruct((bsz * s, SSD_D_INNER), F32),
        grid=(bsz, nc),
        in_specs=[pl.BlockSpec((L, SSD_XBC), lambda b, c: (b * nc + c, C_XBC // SSD_XBC)),
                  pl.BlockSpec((SUBLANE, SSD_XBC),
                               lambda b, c: (jnp.maximum(b * rows8 + c * (L // SUBLANE) - 1, 0), C_XBC // SSD_XBC)),
                  pl.BlockSpec((L, SSD_D_INNER), lambda b, c: (b * nc + c, C_Z // SSD_D_INNER)),
                  pl.BlockSpec((L, LANE), lambda b, c: (b * nc + c, C_DT // LANE)),
                  full((SSD_CONV, SSD_XBC)), full((1, SSD_XBC)), full((1, LANE)), full((1, LANE)),
                  full((1, SSD_D_INNER)), full((1, SSD_D_INNER)),
                  full((LANE, SSD_D_INNER)), full((L, L))],
        out_specs=pl.BlockSpec((L, SSD_D_INNER), lambda b, c: (b * nc + c, 0)),
        scratch_shapes=[pltpu.VMEM((SSD_HEADS // 2, SSD_STATE, 2 * SSD_HEAD_DIM), F32)],
        compiler_params=_cparams(("parallel", "arbitrary")),
        name="ssd",
    )(proj2d, proj2d, proj2d, proj2d, conv_w, conv_b.reshape(1, SSD_XBC), pad16(dt_bias), pad16(a_log),
      rep64(d_skip), norm_g.reshape(1, SSD_D_INNER),
      jnp.asarray(e16, BF16), jnp.asarray(tril, BF16))


def _mix_kernel(up_ref, upprev_ref, nsa_ref, ssd_ref, ga_ref, gb_ref, gc_ref, h_ref,
                pw_ref, ps_ref, wbp_ref, wbn_ref, wbs_ref, wo_ref, lg_ref, lb_ref, o_ref, *, ts):
    si = pl.program_id(1)
    w0 = MAX_POOL_WINDOW
    prev = jnp.where(si > 0, upprev_ref[...], 0.0)
    xcat = jnp.concatenate([prev, up_ref[...]], axis=0)
    tpos = si * ts + lax.broadcasted_iota(jnp.int32, (ts, POOL_GROUP), 0)
    ygs = []
    for gi, w in enumerate(POOL_WINDOWS):
        x = xcat[:, gi * POOL_GROUP:(gi + 1) * POOL_GROUP]
        acc = x
        d = 1
        while d < w:
            acc = acc + pltpu.roll(acc, d, axis=0)
            d *= 2
        cnt = jnp.minimum(tpos + 1, w).astype(F32)
        r = acc[w0:, :] / cnt - x[w0:, :]
        ygs.append(_dot(r.astype(BF16), pw_ref[gi]))
    y = jnp.concatenate(ygs, axis=1) * ps_ref[...]
    br_a = _dot(y.astype(BF16), wbp_ref[...])
    br_b = _dot(nsa_ref[...].astype(BF16), wbn_ref[...])
    br_c = _dot(ssd_ref[...].astype(BF16), wbs_ref[...])
    mixed = (jax.nn.sigmoid(ga_ref[...]) * br_a + jax.nn.sigmoid(gb_ref[...]) * br_b
             + jax.nn.sigmoid(gc_ref[...]) * br_c)
    mix = _dot(mixed.astype(BF16), wo_ref[...])
    _store_rows(o_ref, _layer_norm(DN_ALPHA * _load_rows(h_ref) + mix, lg_ref[...], lb_ref[...]))


def _mix(proj2d, nsa_out, ssd_y, h3, bsz, s, pool_w, pool_scale, w_br_pool, w_br_nsa, w_br_ssd,
         w_out, ln_g, ln_b):
    ts = MIX_TS
    ns = s // ts
    t = bsz * s
    d = D_MODEL
    w0 = MAX_POOL_WINDOW
    wbn = w_br_nsa.reshape(NSA_KV_HEADS, NSA_HPG, NSA_HEAD_DIM, d).transpose(1, 0, 2, 3).reshape(NSA_WIDTH, d)
    full = lambda shape: pl.BlockSpec(shape, lambda b, si: (0,) * len(shape))
    row = lambda width, cb: pl.BlockSpec((ts, width), lambda b, si: (b * ns + si, cb))
    return pl.pallas_call(
        functools.partial(_mix_kernel, ts=ts),
        out_shape=jax.ShapeDtypeStruct((t,) + ROW_TILE, F32),
        grid=(bsz, ns),
        in_specs=[row(POOL_WIDTH, C_POOL // POOL_WIDTH),
                  pl.BlockSpec((w0, POOL_WIDTH),
                               lambda b, si: (jnp.maximum((b * s + si * ts) // w0 - 1, 0), C_POOL // POOL_WIDTH)),
                  row(NSA_WIDTH, 0), row(d, 0),
                  row(d, C_GATE // d), row(d, C_GATE // d + 1), row(d, C_GATE // d + 2),
                  _row_spec(ts, lambda b, si: b * ns + si),
                  full((len(POOL_WINDOWS), POOL_GROUP, POOL_GROUP)), full((1, POOL_WIDTH)),
                  full((POOL_WIDTH, d)), full((NSA_WIDTH, d)), full((d, d)), full((d, d)),
                  full((1, d)), full((1, d))],
        out_specs=_row_spec(ts, lambda b, si: b * ns + si),
        compiler_params=_cparams(("parallel", "arbitrary")),
        name="mix",
    )(proj2d, proj2d, nsa_out, ssd_y, proj2d, proj2d, proj2d, h3,
      pool_w.astype(BF16), pool_scale.reshape(1, POOL_WIDTH), w_br_pool.astype(BF16), wbn.astype(BF16),
      w_br_ssd.astype(BF16), w_out.astype(BF16), ln_g.reshape(1, d), ln_b.reshape(1, d))


RANK_CHUNK = 256
BUCKET_ROWS = 32


def _route(hb, first_step, rw_ref, rb_ref, tri_ref, bucket_ref, wlo_ref, whi_ref, rank_ref, cnt_ref, run_ref):
    @pl.when(first_step)
    def _():
        run_ref[...] = jnp.zeros(run_ref.shape, F32)

    logits = _dot_nt(rw_ref[...], hb)
    lg = [logits[e:e + 1, :] for e in range(N_EXPERTS)]
    mx = lg[0]
    for e in range(1, N_EXPERTS):
        mx = jnp.maximum(mx, lg[e])
    ex = [jnp.exp(v - mx) for v in lg]
    den = ex[0]
    for e in range(1, N_EXPERTS):
        den = den + ex[e]
    probs = [v / den for v in ex]
    score = [probs[e] + rb_ref[e:e + 1, :] for e in range(N_EXPERTS)]

    def group_vals(vals, grp):
        out = []
        for k in range(EXPERTS_PER_GROUP):
            v = vals[k]
            for gi in range(1, N_EXPERT_GROUPS):
                v = jnp.where(grp == gi, vals[gi * EXPERTS_PER_GROUP + k], v)
            out.append(v)
        return out

    gscore = []
    for gi in range(N_EXPERT_GROUPS):
        sc = score[gi * EXPERTS_PER_GROUP:(gi + 1) * EXPERTS_PER_GROUP]
        best = None
        for a, b in PAIRS:
            v = sc[a] + sc[b]
            best = v if best is None else jnp.maximum(best, v)
        gscore.append(best)
    grp = jnp.zeros(gscore[0].shape, jnp.int32)
    best = gscore[0]
    for gi in range(1, N_EXPERT_GROUPS):
        better = gscore[gi] > best
        grp = jnp.where(better, gi, grp)
        best = jnp.where(better, gscore[gi], best)

    sc = group_vals(score, grp)
    pr = group_vals(probs, grp)
    first = jnp.zeros(grp.shape, jnp.int32)
    fv = sc[0]
    for k in range(1, EXPERTS_PER_GROUP):
        better = sc[k] > fv
        first = jnp.where(better, k, first)
        fv = jnp.where(better, sc[k], fv)
    second = jnp.full(grp.shape, -1, jnp.int32)
    sv = jnp.full(fv.shape, -jnp.inf, F32)
    for k in range(EXPERTS_PER_GROUP):
        better = (first != k) & ((sc[k] > sv) | (second < 0))
        second = jnp.where(better, k, second)
        sv = jnp.where(better, sc[k], sv)
    lo = jnp.minimum(first, second)
    hi = jnp.maximum(first, second)
    p_lo = pr[0]
    p_hi = pr[0]
    for k in range(1, EXPERTS_PER_GROUP):
        p_lo = jnp.where(lo == k, pr[k], p_lo)
        p_hi = jnp.where(hi == k, pr[k], p_hi)
    tot = p_lo + p_hi
    pair = jnp.zeros(grp.shape, jnp.int32)
    for pi, (a, b) in enumerate(PAIRS):
        pair = jnp.where((lo == a) & (hi == b), pi, pair)
    bucket = grp * len(PAIRS) + pair
    bucket_ref[...] = bucket
    wlo_ref[...] = p_lo / tot
    whi_ref[...] = p_hi / tot

    tm = bucket.shape[1]
    onehot = jnp.where(lax.broadcasted_iota(jnp.int32, (BUCKET_ROWS, tm), 0) == bucket, 1.0, 0.0)
    run = run_ref[...]
    ranks = []
    for c in range(tm // RANK_CHUNK):
        oc = onehot[:, c * RANK_CHUNK:(c + 1) * RANK_CHUNK]
        before = _dot(oc.astype(BF16), tri_ref[...]) + jnp.concatenate([run] * (RANK_CHUNK // LANE), axis=1)
        ranks.append(jnp.sum(oc * before, axis=0, keepdims=True))
        run = run + jnp.broadcast_to(jnp.sum(oc, axis=1, keepdims=True), run.shape)
    rank_ref[...] = jnp.concatenate(ranks, axis=1).astype(jnp.int32)
    run_ref[...] = run
    cnt_ref[...] = run


ROUTER_TM = 1024


def _router_kernel(h_ref, *refs):
    _route(_load_rows(h_ref).astype(BF16), pl.program_id(0) == 0, *refs)


def _router(h2d, router_w, router_b):
    t, d = h2d.shape[0], D_MODEL
    tm = ROUTER_TM
    rw = jnp.pad(router_w.T, ((0, LANE - N_EXPERTS), (0, 0))).astype(BF16)
    rb = jnp.broadcast_to(jnp.pad(router_b, (0, LANE - N_EXPERTS))[:, None], (LANE, tm))
    tri = jnp.asarray(np.triu(np.ones((RANK_CHUNK, RANK_CHUNK), np.float32), 1), BF16)
    vec = lambda dt: jax.ShapeDtypeStruct((1, t), dt)
    row = pl.BlockSpec((1, tm), lambda i: (0, i))
    return pl.pallas_call(
        _router_kernel,
        out_shape=(vec(jnp.int32), vec(F32), vec(F32), vec(jnp.int32),
                   jax.ShapeDtypeStruct((BUCKET_ROWS, LANE), F32)),
        grid=(t // tm,),
        in_specs=[_row_spec(tm, lambda i: i),
                  pl.BlockSpec((LANE, d), lambda i: (0, 0)),
                  pl.BlockSpec((LANE, tm), lambda i: (0, 0)),
                  pl.BlockSpec((RANK_CHUNK, RANK_CHUNK), lambda i: (0, 0))],
        out_specs=(row, row, row, row, pl.BlockSpec((BUCKET_ROWS, LANE), lambda i: (0, 0))),
        scratch_shapes=[pltpu.VMEM((BUCKET_ROWS, LANE), F32)],
        compiler_params=_cparams(("arbitrary",)),
        name="router",
    )(h2d, rw, rb, tri)


def _moe_plan(bucket, rank, counts, w_lo, w_hi, t):
    tm = MOE_TM
    n_tiles = t // tm + N_BUCKETS
    p_rows = n_tiles * tm
    tiles_per = (counts + tm - 1) // tm
    tile_end = jnp.cumsum(tiles_per)
    tile_start = tile_end - tiles_per
    in_bucket = bucket[:, None] == jnp.arange(N_BUCKETS)[None, :]
    dest = jnp.sum(jnp.where(in_bucket, (tile_start * tm)[None, :], 0), axis=1) + rank
    payload = jnp.stack([jnp.arange(t, dtype=jnp.int32), lax.bitcast_convert_type(w_lo, jnp.int32),
                         lax.bitcast_convert_type(w_hi, jnp.int32), jnp.ones((t,), jnp.int32)], axis=1)
    plan = jnp.zeros((p_rows, 4), jnp.int32).at[dest].set(payload, unique_indices=True)
    src = plan[:, 0]
    wl = lax.bitcast_convert_type(plan[:, 1], F32)
    wh = lax.bitcast_convert_type(plan[:, 2], F32)
    prow = jnp.arange(p_rows, dtype=jnp.int32)
    spare = t + ((prow // tm) % 2) * tm + prow % tm
    dst = jnp.where(plan[:, 3] > 0, src, spare)
    n_used = tile_end[-1]
    tile_ids = jnp.arange(n_tiles)
    tb = jnp.sum((tile_end[None, :] <= tile_ids[:, None]).astype(jnp.int32), axis=1)
    tb = jnp.where(tile_ids < n_used, tb, tb[jnp.maximum(n_used - 1, 0)])
    tb = jnp.minimum(tb, N_BUCKETS - 1)
    pairs = np.asarray(PAIRS, np.int32)
    grp = tb // len(PAIRS)
    ea = grp * EXPERTS_PER_GROUP + jnp.asarray(pairs[:, 0])[tb % len(PAIRS)]
    eb = grp * EXPERTS_PER_GROUP + jnp.asarray(pairs[:, 1])[tb % len(PAIRS)]
    nvalid = jnp.clip(counts[tb] - (tile_ids - tile_start[tb]) * tm, 0, tm)
    nvalid = jnp.where(tile_ids < n_used, nvalid, 0).astype(jnp.int32)
    return (src.reshape(n_tiles, 1, tm), dst.reshape(n_tiles, 1, tm), wl.reshape(p_rows, 1), wh.reshape(p_rows, 1),
            ea.astype(jnp.int32), eb.astype(jnp.int32), nvalid, n_tiles)


MOE_DMA_GROUPS = 8


def _moe_kernel(ea_ref, eb_ref, nv_ref, src0_ref, srcn_ref, dstp_ref, wl_ref, wh_ref, h_hbm,
                w1a_ref, w3a_ref, w2a_ref, w1b_ref, w3b_ref, w2b_ref, lg_ref, lb_ref,
                o_hbm, xbuf0, xbuf1, obuf0, obuf1, wa1, wa3, wa2, wb1, wb3, wb2, gsem, ssem):
    i = pl.program_id(0)
    tm = xbuf0.shape[0]
    xbufs = (xbuf0, xbuf1)
    obufs = (obuf0, obuf1)
    nv = nv_ref[i]
    nv_p1 = nv_ref[jnp.maximum(i - 1, 0)]
    odd = lax.rem(i, 2) == 1
    even = jnp.logical_not(odd)

    def gather(r, p, row):
        return pltpu.make_async_copy(h_hbm.at[pl.ds(row, 1)], xbufs[p].at[pl.ds(r, 1)], gsem.at[p])

    def scatter(r, p, row):
        return pltpu.make_async_copy(obufs[p].at[pl.ds(r, 1)], o_hbm.at[pl.ds(row, 1)], ssem.at[p])

    def wait_gather(p):
        for r in range(tm):
            gather(r, p, 0).wait()

    def wait_scatter(p):
        for r in range(tm):
            scatter(r, p, 0).wait()

    def tile(p, age):
        q = 1 - p
        half = MOE_DMA_GROUPS // 2
        per = tm // half

        def issue(k):
            for r in range((k % half) * per, (k % half + 1) * per):
                if k < half:
                    gather(r, q, srcn_ref[0, 0, r]).start()
                elif age >= 1:
                    scatter(r, q, dstp_ref[0, 0, r]).start()

        wait_gather(p)
        x = _load_rows(xbufs[p])
        xb = x.astype(BF16)
        issue(0)
        h1 = _dot(xb, wa1[...])
        issue(1)
        h3 = _dot(xb, wa3[...])
        issue(2)
        ya = _dot((jax.nn.silu(h1) * h3).astype(BF16), wa2[...])
        issue(3)
        h1 = _dot(xb, wb1[...])
        issue(4)
        h3 = _dot(xb, wb3[...])
        issue(5)
        yb = _dot((jax.nn.silu(h1) * h3).astype(BF16), wb2[...])
        issue(6)
        y = wl_ref[...] * ya + wh_ref[...] * yb
        issue(7)
        out = _layer_norm(DN_ALPHA * x + y, lg_ref[...], lb_ref[...])
        if age >= 2:
            wait_scatter(p)
        _store_rows(obufs[p], out)

    @pl.when(i == 0)
    def _():
        obuf1[...] = jnp.zeros(obuf1.shape, F32)
        n_real = o_hbm.shape[0] - 2 * tm
        for blk in range(2):
            cp = pltpu.make_async_copy(obuf1, o_hbm.at[pl.ds(n_real + blk * tm, tm)], ssem.at[1])
            cp.start()
            cp.wait()

    prev = jnp.maximum(i - 1, 0)

    @pl.when((nv > 0) & ((i == 0) | (ea_ref[i] != ea_ref[prev])))
    def _():
        wa1[...] = w1a_ref[0, 0].astype(BF16)
        wa3[...] = w3a_ref[0, 0].astype(BF16)
        wa2[...] = w2a_ref[0, 0].astype(BF16)

    @pl.when((nv > 0) & ((i == 0) | (eb_ref[i] != eb_ref[prev])))
    def _():
        wb1[...] = w1b_ref[0, 0].astype(BF16)
        wb3[...] = w3b_ref[0, 0].astype(BF16)
        wb2[...] = w2b_ref[0, 0].astype(BF16)

    @pl.when((nv > 0) & (i == 0))
    def _():
        def start0(r, carry):
            gather(r, 0, src0_ref[0, 0, r]).start()
            return carry
        lax.fori_loop(0, tm, start0, 0)
        tile(0, 0)

    pl.when((nv > 0) & (i == 1))(functools.partial(tile, 1, 1))
    pl.when((nv > 0) & (i >= 2) & even)(functools.partial(tile, 0, 2))
    pl.when((nv > 0) & (i >= 2) & odd)(functools.partial(tile, 1, 2))

    def tail(p):
        q = 1 - p
        wait_gather(p)

        @pl.when(i >= 2)
        def _():
            wait_scatter(p)

        def start(r, carry):
            scatter(r, q, dstp_ref[0, 0, r]).start()
            return carry
        lax.fori_loop(0, tm, start, 0)
        wait_scatter(q)

    is_tail = (nv == 0) & (i >= 1) & (nv_p1 > 0)
    pl.when(is_tail & even)(functools.partial(tail, 0))
    pl.when(is_tail & odd)(functools.partial(tail, 1))


def _moe(h3, bucket, rank, counts, w_lo, w_hi, w1, w3, w2, layer, ln_g, ln_b):
    t, d = h3.shape[0], D_MODEL
    tm = MOE_TM
    assert tm % MOE_DMA_GROUPS == 0
    src, dst, wl, wh, ea, eb, nvalid, n_tiles = _moe_plan(bucket, rank, counts, w_lo, w_hi, t)
    wspec = lambda shape, which: pl.BlockSpec(
        (1,) + shape, (lambda i, ea_r, eb_r, nv_r: (layer, ea_r[i], 0, 0)) if which == 0
        else (lambda i, ea_r, eb_r, nv_r: (layer, eb_r[i], 0, 0)))
    w13 = (1, d, D_EXPERT)
    w2s = (1, D_EXPERT, d)
    smem_tile = lambda fn: pl.BlockSpec((1, 1, tm), lambda i, *_: (fn(i), 0, 0), memory_space=pltpu.SMEM)
    grid_spec = pltpu.PrefetchScalarGridSpec(
        num_scalar_prefetch=3,
        grid=(n_tiles,),
        in_specs=[smem_tile(lambda i: 0),
                  smem_tile(lambda i: jnp.minimum(i + 1, n_tiles - 1)),
                  smem_tile(lambda i: jnp.maximum(i - 1, 0)),
                  pl.BlockSpec((tm, 1), lambda i, *_: (i, 0)),
                  pl.BlockSpec((tm, 1), lambda i, *_: (i, 0)),
                  pl.BlockSpec(memory_space=pl.ANY),
                  wspec(w13, 0), wspec(w13, 0), wspec(w2s, 0),
                  wspec(w13, 1), wspec(w13, 1), wspec(w2s, 1),
                  pl.BlockSpec((1, d), lambda i, *_: (0, 0)),
                  pl.BlockSpec((1, d), lambda i, *_: (0, 0))],
        out_specs=pl.BlockSpec(memory_space=pl.ANY),
        scratch_shapes=[pltpu.VMEM((tm,) + ROW_TILE, F32)] * 4
        + [pltpu.VMEM(w13[1:], BF16), pltpu.VMEM(w13[1:], BF16), pltpu.VMEM(w2s[1:], BF16)] * 2
        + [pltpu.SemaphoreType.DMA((2,)), pltpu.SemaphoreType.DMA((2,))],
    )
    return pl.pallas_call(
        _moe_kernel,
        out_shape=jax.ShapeDtypeStruct((t + 2 * tm,) + ROW_TILE, F32),
        grid_spec=grid_spec,
        compiler_params=_cparams(("arbitrary",)),
        name="moe",
    )(ea, eb, nvalid, src, src, dst, wl, wh, h3, w1, w3, w2, w1, w3, w2,
      ln_g.reshape(1, d), ln_b.reshape(1, d))


def kernel(x, ln0_g, ln0_b, w_in, pool_w, pool_scale, cmp_k_pe, cmp_k_w1, cmp_k_w2, cmp_v_pe, cmp_v_w1, cmp_v_w2, rel_bias, conv_w, conv_b, dt_bias, a_log, d_skip, ssd_norm_g, w_br_pool, w_br_nsa, w_br_ssd, w_out, ln1_g, ln1_b, router_w, router_b, exp_w1, exp_w3, exp_w2, ln2_g, ln2_b):
    bsz, s, d = x.shape
    assert d == D_MODEL and s % MIX_TS == 0 and s // CMP_STRIDE == LANE and (bsz * s) % PROJ_TM == 0
    t = bsz * s
    ts, ws, bc = _nsa_bias_tables(rel_bias)
    et, ovt = _nsa_consts(s)
    h = x.reshape(t, d)
    for i in range(DEPTH):
        if i == 0:
            proj, h = _proj(h, _prep_w_in(w_in[i]), t, ln=(ln0_g, ln0_b))
        else:
            proj = _proj(h, _prep_w_in(w_in[i]), t)
        kc, vc = _compress(proj, bsz, s, cmp_k_pe[i], cmp_k_w1[i], cmp_k_w2[i],
                           cmp_v_pe[i], cmp_v_w1[i], cmp_v_w2[i])
        nsa_out = _nsa(proj, kc, vc, ts, ws, bc, et, ovt, bsz, s)
        ssd_y = _ssd(proj, bsz, s, conv_w[i], conv_b[i], dt_bias[i], a_log[i], d_skip[i], ssd_norm_g[i])
        h1 = _mix(proj, nsa_out, ssd_y, h, bsz, s, pool_w[i], pool_scale[i], w_br_pool[i], w_br_nsa[i],
                  w_br_ssd[i], w_out[i], ln1_g[i], ln1_b[i])
        bucket, w_lo, w_hi, rank, cnt = _router(h1, router_w, router_b)
        counts = cnt[:N_BUCKETS, 0].astype(jnp.int32)
        h = _moe(h1, bucket[0], rank[0], counts, w_lo[0], w_hi[0], exp_w1, exp_w3, exp_w2, i,
                 ln2_g[i], ln2_b[i])
    return h[:t].reshape(bsz, s, d)
```

```python
import functools
import math

import numpy as np
import jax
import jax.numpy as jnp
from jax import lax
from jax.experimental import pallas as pl
from jax.experimental.pallas import tpu as pltpu

F32 = jnp.float32
BF16 = jnp.bfloat16

D_MODEL = 1024
DEPTH = 2
DN_ALPHA = (2.0 * DEPTH) ** 0.25
LN_EPS = 1e-5
NEG = -1e30
BIG = 1e6

POOL_WINDOWS = (2, 4, 8, 16)
POOL_GROUP = 128
POOL_WIDTH = 512
MAX_POOL_WINDOW = 16

NSA_HEAD_DIM = 64
NSA_KV_HEADS = 2
NSA_HPG = 4
NSA_HEADS = 8
NSA_WIDTH = 512
CMP_LEN = 32
CMP_STRIDE = 16
CMP_HIDDEN = 128
SLC_LEN = 64
SLC_TOPN = 8
WIN_LEN = 512
Q_BLOCK = 128
REL_BUCKETS = 32
REL_MAX_DIST = 128

SSD_D_INNER = 1024
SSD_HEAD_DIM = 64
SSD_HEADS = 16
SSD_GROUPS = 4
SSD_STATE = 128
SSD_CONV = 4
SSD_CHUNK = 128
SSD_XBC = 2048

N_EXPERTS = 16
N_EXPERT_GROUPS = 4
EXPERTS_PER_GROUP = 4
D_EXPERT = 512
PAIRS = ((0, 1), (0, 2), (0, 3), (1, 2), (1, 3), (2, 3))
N_BUCKETS = N_EXPERT_GROUPS * len(PAIRS)

LANE = 128
SUBLANE = 8
VMEM_LIMIT = 48 * 1024 * 1024

C_POOL = 0
C_Q = 512
C_KV = 1024
C_NG = 1792
C_DT = 1920
C_XBC = 2048
C_Z = 4096
C_GATE = 5120
D_PAD = 8192

PROJ_TM = 1024
PROJ_TN = 2048
MIX_TS = 256
MOE_TM = 256


def _dot(a, b):
    return jnp.dot(a, b, preferred_element_type=F32)


def _dot_nt(a, b):
    return lax.dot_general(a, b, (((1,), (1,)), ((), ())), preferred_element_type=F32)


def _dot_tn(a, b):
    return lax.dot_general(a, b, (((0,), (0,)), ((), ())), preferred_element_type=F32)


def _split3(x):
    x1 = x.astype(BF16)
    r1 = x - x1.astype(F32)
    x2 = r1.astype(BF16)
    x3 = (r1 - x2.astype(F32)).astype(BF16)
    return x1, x2, x3


def _layer_norm(x, g, b):
    mu = jnp.mean(x, -1, keepdims=True)
    xc = x - mu
    var = jnp.mean(xc * xc, -1, keepdims=True)
    return xc * lax.rsqrt(var + LN_EPS) * g + b


def _cparams(sem):
    return pltpu.CompilerParams(dimension_semantics=sem, vmem_limit_bytes=VMEM_LIMIT)


ROW_TILE = (D_MODEL,)


def _load_rows(ref):
    return ref[...]


def _store_rows(ref, val):
    ref[...] = val


def _row_spec(n, index):
    return pl.BlockSpec((n,) + ROW_TILE, lambda *ids: (index(*ids), 0))


def _proj_kernel(h_ref, w_ref, o_ref, hb_ref):
    @pl.when(pl.program_id(1) == 0)
    def _():
        hb_ref[...] = _load_rows(h_ref).astype(BF16)

    o_ref[...] = _dot(hb_ref[...], w_ref[...])


def _ln_proj_kernel(x_ref, g_ref, b_ref, w_ref, o_ref, h_ref, hb_ref):
    @pl.when(pl.program_id(1) == 0)
    def _():
        h = _layer_norm(x_ref[...], g_ref[...], b_ref[...])
        _store_rows(h_ref, h)
        hb_ref[...] = h.astype(BF16)

    o_ref[...] = _dot(hb_ref[...], w_ref[...])


def _proj(h2d, w_pad, t, ln=None):
    d = D_MODEL
    grid = (t // PROJ_TM, D_PAD // PROJ_TN)
    w_spec = pl.BlockSpec((d, PROJ_TN), lambda i, j: (0, j))
    o_spec = pl.BlockSpec((PROJ_TM, PROJ_TN), lambda i, j: (i, j))
    o_shape = jax.ShapeDtypeStruct((t, D_PAD), F32)
    scratch = [pltpu.VMEM((PROJ_TM, d), BF16)]
    params = _cparams(("parallel", "arbitrary"))
    if ln is None:
        return pl.pallas_call(
            _proj_kernel, out_shape=o_shape, grid=grid,
            in_specs=[_row_spec(PROJ_TM, lambda i, j: i), w_spec], out_specs=o_spec,
            scratch_shapes=scratch, compiler_params=params, name="proj",
        )(h2d, w_pad)
    vec = pl.BlockSpec((1, d), lambda i, j: (0, 0))
    return pl.pallas_call(
        _ln_proj_kernel, out_shape=(o_shape, jax.ShapeDtypeStruct((t,) + ROW_TILE, F32)), grid=grid,
        in_specs=[_row_spec(PROJ_TM, lambda i, j: i), vec, vec, w_spec],
        out_specs=(o_spec, _row_spec(PROJ_TM, lambda i, j: i)),
        scratch_shapes=scratch, compiler_params=params, name="ln_proj",
    )(h2d, ln[0].reshape(1, d), ln[1].reshape(1, d), w_pad)


def _pad_cols(w, n):
    return jnp.pad(w, ((0, 0), (0, n - w.shape[1])))


def _prep_w_in(w_in):
    d = w_in.shape[0]
    o = 0
    w_pool = w_in[:, o:o + POOL_WIDTH]; o += POOL_WIDTH
    w_q = w_in[:, o:o + NSA_WIDTH]; o += NSA_WIDTH
    w_kv = w_in[:, o:o + 768]; o += 768
    w_ng = w_in[:, o:o + 24]; o += 24
    w_z = w_in[:, o:o + SSD_D_INNER]; o += SSD_D_INNER
    w_xbc = w_in[:, o:o + SSD_XBC]; o += SSD_XBC
    w_dt = w_in[:, o:o + SSD_HEADS]; o += SSD_HEADS
    w_gate = w_in[:, o:o + 3 * D_MODEL]
    w_q = w_q.reshape(d, NSA_KV_HEADS, NSA_HPG, NSA_HEAD_DIM).transpose(0, 2, 1, 3).reshape(d, NSA_WIDTH)
    w = jnp.concatenate([w_pool, w_q, w_kv, _pad_cols(w_ng, LANE), _pad_cols(w_dt, LANE),
                         w_xbc, w_z, w_gate], axis=1)
    return w.astype(BF16)


def _compress_one(tok_ref, pe_ref, w1_ref, w2_ref):
    n16 = tok_ref.shape[0] // CMP_STRIDE
    acc_a = jnp.zeros((n16, 2 * CMP_HIDDEN), F32)
    acc_b = jnp.zeros((n16, 2 * CMP_HIDDEN), F32)
    for l in range(CMP_STRIDE):
        rows = tok_ref[pl.ds(l, n16, stride=CMP_STRIDE), :]
        xa = (rows + pe_ref[l:l + 1, :]).astype(BF16)
        xb = (rows + pe_ref[CMP_STRIDE + l:CMP_STRIDE + l + 1, :]).astype(BF16)
        acc_a = acc_a + _dot(xa, w1_ref[l])
        acc_b = acc_b + _dot(xb, w1_ref[CMP_STRIDE + l])
    hid = acc_a + pltpu.roll(acc_b, n16 - 1, axis=0)
    hid = jax.nn.gelu(hid)
    return _dot(hid.astype(BF16), w2_ref[...])


def _compress_kernel(kt_ref, vt_ref, pek_ref, w1k_ref, w2k_ref, pev_ref, w1v_ref, w2v_ref,
                     kc_ref, vc_ref):
    kc_ref[0] = _compress_one(kt_ref, pek_ref, w1k_ref, w2k_ref).astype(BF16)
    vc_ref[0] = _compress_one(vt_ref, pev_ref, w1v_ref, w2v_ref).T.astype(BF16)


def _blockdiag2(w):
    z = jnp.zeros_like(w)
    return jnp.concatenate([jnp.concatenate([w, z], -1), jnp.concatenate([z, w], -1)], -2)


def _compress(proj2d, bsz, s, pe_k, w1_k, w2_k, pe_v, w1_v, w2_v):
    n16 = s // CMP_STRIDE

    def prep(pe, w1, w2):
        pe2 = jnp.concatenate([pe, pe], axis=-1)
        w1b = _blockdiag2(w1.reshape(CMP_LEN, NSA_HEAD_DIM, CMP_HIDDEN)).astype(BF16)
        w2b = _blockdiag2(w2).astype(BF16)
        return pe2, w1b, w2b

    pk = prep(pe_k, w1_k, w2_k)
    pv = prep(pe_v, w1_v, w2_v)
    full = lambda shape: pl.BlockSpec(shape, lambda b: (0,) * len(shape))
    wspecs = [full((CMP_LEN, LANE)), full((CMP_LEN, LANE, 2 * CMP_HIDDEN)), full((2 * CMP_HIDDEN, LANE))]
    return pl.pallas_call(
        _compress_kernel,
        out_shape=(jax.ShapeDtypeStruct((bsz, n16, LANE), BF16),
                   jax.ShapeDtypeStruct((bsz, n16, LANE), BF16)),
        grid=(bsz,),
        in_specs=[pl.BlockSpec((s, LANE), lambda b: (b, C_KV // LANE)),
                  pl.BlockSpec((s, LANE), lambda b: (b, C_KV // LANE + 1))] + wspecs + wspecs,
        out_specs=(pl.BlockSpec((1, n16, LANE), lambda b: (b, 0, 0)),
                   pl.BlockSpec((1, n16, LANE), lambda b: (b, 0, 0))),
        compiler_params=_cparams(("parallel",)),
        name="nsa_compress",
    )(proj2d, proj2d, *pk, *pv)


def _rel_bucket(dist):
    n = jnp.maximum(dist, 0)
    max_exact = REL_BUCKETS // 2
    nf = jnp.maximum(n, 1).astype(F32)
    large = max_exact + (jnp.log(nf / max_exact) / math.log(REL_MAX_DIST / max_exact)
                         * (REL_BUCKETS - max_exact)).astype(jnp.int32)
    large = jnp.minimum(large, REL_BUCKETS - 1)
    return jnp.where(n < max_exact, n, large)


LOG2E = math.log2(math.e)
WIN_BACK = WIN_LEN // Q_BLOCK
WIN_KEYS = (WIN_BACK + 1) * Q_BLOCK
WIN_STRIP_TILES = 2 * WIN_BACK + 1
SEL_TILES = 4
SEL_KEYS = SEL_TILES * Q_BLOCK
SEL_STRIP_TILES = 3 * SEL_TILES - 1
CMP_BIAS_ROWS = 2 * LANE
PV_ROWS = NSA_HEAD_DIM + 16


def _nsa_bias_tables(rel_bias):
    tbl = rel_bias.astype(F32)
    far = tbl[REL_BUCKETS - 1]
    hq = NSA_HPG * Q_BLOCK

    def lookup(dist):
        onehot = (_rel_bucket(jnp.asarray(dist))[..., None] == jnp.arange(REL_BUCKETS)).astype(F32)
        return jnp.einsum("...b,bh->...h", onehot, tbl, precision=lax.Precision.HIGHEST)

    k = np.arange(Q_BLOCK)[None, :, None]
    q = np.arange(Q_BLOCK)[None, None, :]

    def strip(first, last, visible):
        n = first - last + 1
        dist = np.arange(first, last - 1, -1)[:, None, None] * Q_BLOCK + q - k
        v = jnp.where(jnp.asarray(visible(dist))[..., None], (lookup(dist) - far) * LOG2E, NEG)
        v = v.reshape(n, Q_BLOCK, Q_BLOCK, NSA_KV_HEADS, NSA_HPG).transpose(3, 0, 1, 4, 2)
        return v.reshape(NSA_KV_HEADS, n * Q_BLOCK, hq)

    ts = strip(2 * SEL_TILES - 1, 1 - SEL_TILES, lambda d: d >= 0)
    ws = strip(WIN_BACK, -WIN_BACK, lambda d: (d >= 0) & (d < WIN_LEN))
    r = np.arange(CMP_BIAS_ROWS)[:, None]
    dist_c = q[0] - CMP_STRIDE * (r - LANE) - (CMP_LEN - 1)
    bc = jnp.where(jnp.asarray(dist_c >= 0)[..., None], lookup(dist_c) * LOG2E, NEG)
    bc = bc.reshape(CMP_BIAS_ROWS, Q_BLOCK, NSA_KV_HEADS, NSA_HPG).transpose(2, 0, 3, 1)
    bc = bc.reshape(NSA_KV_HEADS, CMP_BIAS_ROWS, hq)
    return ts, ws, bc


def _nsa_consts(s):
    n_slc = s // SLC_LEN
    et = (np.arange(s)[:, None] // SLC_LEN == np.arange(LANE)[None, :]).astype(np.float32)
    n_cmp = (s - CMP_LEN) // CMP_STRIDE + 1
    c0 = np.arange(n_cmp) * CMP_STRIDE
    s0 = np.arange(n_slc) * SLC_LEN
    ov = np.clip(np.minimum(c0[None, :] + CMP_LEN, s0[:, None] + SLC_LEN)
                 - np.maximum(c0[None, :], s0[:, None]), 0, None) / CMP_LEN
    ovt = np.zeros((n_slc, LANE), np.float32)
    ovt[:, :n_cmp] = ov
    return jnp.asarray(et, BF16), jnp.asarray(ovt, BF16)


def _nsa_kernel(q_ref, ng_ref, ks_ref, vs_ref, kw_ref, vw_ref, kc_ref, vct_ref,
                ts_ref, ws_ref, bc_ref, et_ref, ovt_ref, o_ref,
                ksb_ref, kwb_ref, vst_ref, vwt_ref, *, nq, n_slc):
    qi = pl.program_id(1)
    hq = NSA_HPG * Q_BLOCK
    dh = NSA_HEAD_DIM
    groups = range(NSA_KV_HEADS)
    gsl = [slice(g * dh, (g + 1) * dh) for g in groups]

    @pl.when(qi == 0)
    def _():
        ksb_ref[:, 0:LANE] = ks_ref[...].astype(BF16)
        ksb_ref[:, LANE:] = et_ref[...]
        kwb_ref[...] = kw_ref[...].astype(BF16)
        ones_rows = jnp.where(lax.broadcasted_iota(jnp.int32, (PV_ROWS - dh, ks_ref.shape[0]), 0) == 0, 1.0, 0.0)
        for g in groups:
            vst_ref[g, dh:, :] = ones_rows.astype(BF16)
            vwt_ref[g, dh:, :] = ones_rows.astype(BF16)
        for kt in range(nq):
            sl = slice(kt * Q_BLOCK, (kt + 1) * Q_BLOCK)
            vs_t = vs_ref[sl, :].T
            vw_t = vw_ref[sl, :].T
            for g in groups:
                vst_ref[g, 0:dh, sl] = vs_t[gsl[g], :].astype(BF16)
                vwt_ref[g, 0:dh, sl] = vw_t[gsl[g], :].astype(BF16)

    n_super = qi // SEL_TILES + 1
    for n in range(1, nq // SEL_TILES + 1):
        pl.when(n_super == n)(functools.partial(
            _nsa_step, q_ref, ng_ref, kc_ref, vct_ref, ts_ref, ws_ref, bc_ref, ovt_ref, o_ref,
            ksb_ref, kwb_ref, vst_ref, vwt_ref, n_slc=n_slc, n_sel=n))


def _nsa_step(q_ref, ng_ref, kc_ref, vct_ref, ts_ref, ws_ref, bc_ref, ovt_ref, o_ref,
              ksb_ref, kwb_ref, vst_ref, vwt_ref, *, n_slc, n_sel):
    qi = pl.program_id(1)
    hq = NSA_HPG * Q_BLOCK
    dh = NSA_HEAD_DIM
    groups = range(NSA_KV_HEADS)
    gsl = [slice(g * dh, (g + 1) * dh) for g in groups]
    row_lo = lax.broadcasted_iota(jnp.int32, (LANE, hq), 0) < dh

    def heads4(m):
        return jnp.concatenate([m] * NSA_HPG, axis=1)

    gates_t = jax.nn.sigmoid(ng_ref[...]).T
    q_t = jnp.concatenate(
        [(q_ref[:, h * LANE:(h + 1) * LANE] * (dh ** -0.5 * LOG2E)).T for h in range(NSA_HPG)], axis=1)
    qg_t = [jnp.where(row_lo if g == 0 else jnp.logical_not(row_lo), q_t, 0.0).astype(BF16) for g in groups]

    boff = pl.multiple_of(LANE - SUBLANE * qi, SUBLANE)
    jrow = lax.broadcasted_iota(jnp.int32, (n_slc, LANE), 0)
    tj = qi * Q_BLOCK + lax.broadcasted_iota(jnp.int32, (n_slc, LANE), 1)
    jt = lax.shift_right_logical(tj, SLC_LEN.bit_length() - 1)
    forced = (jrow == 0) | (jrow == jt) | (jrow == jt - 1)
    valid = jrow * SLC_LEN <= tj
    o_c = []
    q_aug = []
    for g in groups:
        s = _dot(kc_ref[0], qg_t[g]) + bc_ref[g, pl.ds(boff, LANE), :]
        mx = jnp.max(s, axis=0, keepdims=True)
        p = jnp.exp2(s - mx)
        lsum = jnp.sum(p, axis=0, keepdims=True)
        pb = (p * jnp.where(mx > 0.5 * NEG, 1.0 / lsum, 0.0)).astype(BF16)
        o_c.append(_dot(vct_ref[0, gsl[g], :], pb))
        imp4 = _dot(ovt_ref[...], pb)
        imp = imp4[:, 0:LANE]
        for h in range(1, NSA_HPG):
            imp = imp + imp4[:, h * LANE:(h + 1) * LANE]
        val = jnp.where(forced, BIG, jnp.where(valid, imp, -BIG))
        rank = jnp.zeros((n_slc, LANE), F32)
        for i in range(n_slc):
            ci = val[i:i + 1, :]
            rank = rank + jnp.where(ci > val, 1.0, 0.0) + jnp.where(ci == val, (jrow > i).astype(F32), 0.0)
        block_rows = heads4(jnp.where(rank < SLC_TOPN, 0.0, NEG)).astype(BF16)
        q_aug.append(jnp.concatenate(
            [qg_t[g], block_rows, jnp.zeros((LANE - n_slc, hq), BF16)], axis=0))

    diag = qi - SEL_TILES * (n_sel - 1)
    nk = n_sel * SEL_KEYS
    lo = max(n_sel - 2, 0) * SEL_KEYS
    toff = pl.multiple_of((2 * SEL_TILES - 1 - diag) * Q_BLOCK - (nk - lo - SEL_KEYS), Q_BLOCK)
    o_s = []
    for g in groups:
        parts = [(lo, nk, _dot(ksb_ref[lo:nk, :], q_aug[g]) + ts_ref[g, pl.ds(toff, nk - lo), :])]
        if lo > 0:
            parts.append((0, lo, _dot(ksb_ref[0:lo, :], q_aug[g])))
        m = parts[0][2].max(axis=0, keepdims=True)
        for _, _, s in parts[1:]:
            m = jnp.maximum(m, s.max(axis=0, keepdims=True))
        acc = None
        for a, b, s in parts:
            pv = _dot(vst_ref[g, :, a:b], jnp.exp2(s - m).astype(BF16))
            acc = pv if acc is None else acc + pv
        o_s.append(acc[0:dh] * (1.0 / acc[dh:dh + 1]))

    w0 = jnp.maximum(qi - WIN_BACK, 0)
    woff = pl.multiple_of(w0 * Q_BLOCK, Q_BLOCK)
    boff_w = pl.multiple_of((WIN_BACK - (qi - w0)) * Q_BLOCK, Q_BLOCK)
    kw_bf = kwb_ref[pl.ds(woff, WIN_KEYS), :]
    o_w = []
    for g in groups:
        s = _dot(kw_bf, qg_t[g]) + ws_ref[g, pl.ds(boff_w, WIN_KEYS), :]
        p = jnp.exp2(s - jnp.max(s, axis=0, keepdims=True)).astype(BF16)
        acc = _dot(vwt_ref[g, :, pl.ds(woff, WIN_KEYS)], p)
        o_w.append(acc[0:dh] * (1.0 / acc[dh:dh + 1]))

    for h in range(NSA_HPG):
        hl = slice(h * LANE, (h + 1) * LANE)
        halves = []
        for g in groups:
            c = g * NSA_HPG + h
            halves.append(gates_t[c:c + 1, :] * o_c[g][:, hl]
                          + gates_t[NSA_HEADS + c:NSA_HEADS + c + 1, :] * o_s[g][:, hl]
                          + gates_t[2 * NSA_HEADS + c:2 * NSA_HEADS + c + 1, :] * o_w[g][:, hl])
        o_ref[:, hl] = jnp.concatenate(halves, axis=0).T


def _nsa(proj2d, kc, vct, ts, ws, bc, et, ovt, bsz, s):
    nq = s // Q_BLOCK
    n_slc = s // SLC_LEN
    hq = NSA_HPG * Q_BLOCK
    kvb = C_KV // LANE
    kv_spec = lambda blk: pl.BlockSpec((s, LANE), lambda b, qi: (b, kvb + blk))
    full = lambda shape: pl.BlockSpec(shape, lambda b, qi: (0,) * len(shape), pipeline_mode=pl.Buffered(1))
    return pl.pallas_call(
        functools.partial(_nsa_kernel, nq=nq, n_slc=n_slc),
        out_shape=jax.ShapeDtypeStruct((bsz * s, NSA_WIDTH), F32),
        grid=(bsz, nq),
        in_specs=[pl.BlockSpec((Q_BLOCK, NSA_WIDTH), lambda b, qi: (b * nq + qi, C_Q // NSA_WIDTH)),
                  pl.BlockSpec((Q_BLOCK, LANE), lambda b, qi: (b * nq + qi, C_NG // LANE)),
                  kv_spec(2), kv_spec(3), kv_spec(4), kv_spec(5),
                  pl.BlockSpec((1, s // CMP_STRIDE, LANE), lambda b, qi: (b, 0, 0)),
                  pl.BlockSpec((1, LANE, s // CMP_STRIDE), lambda b, qi: (b, 0, 0)),
                  full((NSA_KV_HEADS, SEL_STRIP_TILES * Q_BLOCK, hq)),
                  full((NSA_KV_HEADS, WIN_STRIP_TILES * Q_BLOCK, hq)),
                  full((NSA_KV_HEADS, CMP_BIAS_ROWS, hq)),
                  full((s, LANE)),
                  full((n_slc, LANE))],
        out_specs=pl.BlockSpec((Q_BLOCK, NSA_WIDTH), lambda b, qi: (b * nq + qi, 0)),
        scratch_shapes=[pltpu.VMEM((s, 2 * LANE), BF16), pltpu.VMEM((s, LANE), BF16),
                        pltpu.VMEM((NSA_KV_HEADS, PV_ROWS, s), BF16), pltpu.VMEM((NSA_KV_HEADS, PV_ROWS, s), BF16)],
        compiler_params=_cparams(("parallel", "arbitrary")),
        name="nsa_attn",
    )(proj2d, proj2d, proj2d, proj2d, proj2d, proj2d, kc, vct, ts, ws, bc, et, ovt)


def _ssd_kernel(xbc_ref, prev_ref, z_ref, dt_ref, cw_ref, cb_ref, dtb_ref, alog_ref,
                dskip_ref, ng_ref, e16_ref, tril_ref, o_ref, state_ref):
    c = pl.program_id(1)
    L = SSD_CHUNK
    pair_w = 2 * SSD_HEAD_DIM

    @pl.when(c == 0)
    def _():
        state_ref[...] = jnp.zeros(state_ref.shape, F32)

    xcat = jnp.concatenate([jnp.where(c > 0, prev_ref[...], 0.0), xbc_ref[...]], axis=0)
    acc = cw_ref[0:1, :] * xcat
    for k in range(1, SSD_CONV):
        acc = pltpu.roll(acc, 1, axis=0) + cw_ref[k:k + 1, :] * xcat
    xbc = jax.nn.silu(acc[SUBLANE:, :] + cb_ref[...])
    xs = xbc[:, :SSD_D_INNER]
    bmat = xbc[:, SSD_D_INNER:SSD_D_INNER + SSD_GROUPS * SSD_STATE]
    cmat = xbc[:, SSD_D_INNER + SSD_GROUPS * SSD_STATE:]

    dt = jax.nn.softplus(dt_ref[...] + dtb_ref[...])
    tril = tril_ref[...]
    e16 = e16_ref[...]
    d1, d2, d3 = _split3(dt)
    dt_x = _dot(d1, e16) + _dot(d2, e16) + _dot(d3, e16)
    a_c = dt * (-jnp.exp(alog_ref[...]) * LOG2E)
    c1, c2, c3 = _split3(a_c)
    acs = _dot(tril, c1) + _dot(tril, c2) + _dot(tril, c3)
    x1, x2, x3 = _split3(acs)
    acs_x = _dot(x1, e16) + _dot(x2, e16) + _dot(x3, e16)
    acs_t = acs.T
    last = acs_x[L - 1:L, :]
    eacs = jnp.exp2(acs_x)
    decay_in = jnp.exp2(last - acs_x)
    cdec = jnp.exp2(last)
    xdt = xs * dt_x
    wst = xdt * decay_in

    row = lax.broadcasted_iota(jnp.int32, (L, L), 0)
    lane = lax.broadcasted_iota(jnp.int32, (L, L), 1)
    tri = row >= lane
    lo = lane < SSD_HEAD_DIM

    ys = []
    for g in range(SSD_GROUPS):
        cg = cmat[:, g * SSD_STATE:(g + 1) * SSD_STATE].astype(BF16)
        bg = bmat[:, g * SSD_STATE:(g + 1) * SSD_STATE].astype(BF16)
        cb = _dot_nt(cg, bg)
        for pr in range(2):
            pp = g * 2 + pr
            ls = slice(pp * pair_w, (pp + 1) * pair_w)
            xp = xdt[:, ls]
            yd = None
            for e in range(2):
                hd = 2 * pp + e
                diff = acs[:, hd:hd + 1] - acs_t[hd:hd + 1, :]
                seg = jnp.exp2(jnp.where(tri, diff, NEG))
                pm = (cb * seg).astype(BF16)
                xe = jnp.where(lo if e == 0 else jnp.logical_not(lo), xp, 0.0).astype(BF16)
                term = _dot(pm, xe)
                yd = term if yd is None else yd + term
            st = state_ref[pp]
            yo = _dot(cg, st.astype(BF16)) * eacs[:, ls]
            ys.append(yd + yo)
            state_ref[pp] = cdec[:, ls] * st + _dot_tn(bg, wst[:, ls].astype(BF16))

    y = jnp.concatenate(ys, axis=1) + xs * dskip_ref[...]
    y = y * jax.nn.silu(z_ref[...])
    gw = SSD_D_INNER // SSD_GROUPS
    outs = []
    for gi in range(SSD_GROUPS):
        yg = y[:, gi * gw:(gi + 1) * gw]
        outs.append(yg * lax.rsqrt(jnp.mean(yg * yg, -1, keepdims=True) + LN_EPS))
    o_ref[...] = jnp.concatenate(outs, axis=1) * ng_ref[...]


def _ssd(proj2d, bsz, s, conv_w, conv_b, dt_bias, a_log, d_skip, norm_g):
    nc = s // SSD_CHUNK
    L = SSD_CHUNK
    pad16 = lambda v: jnp.pad(v.reshape(1, SSD_HEADS), ((0, 0), (0, LANE - SSD_HEADS)))
    rep64 = lambda v: jnp.repeat(v, SSD_HEAD_DIM).reshape(1, SSD_D_INNER)
    e16 = np.zeros((LANE, SSD_D_INNER), np.float32)
    e16[np.arange(SSD_D_INNER) // SSD_HEAD_DIM, np.arange(SSD_D_INNER)] = 1.0
    tril = np.tril(np.ones((L, L), np.float32))
    full = lambda shape: pl.BlockSpec(shape, lambda b, c: (0,) * len(shape))
    rows8 = s // SUBLANE
    return pl.pallas_call(
        _ssd_kernel,
        out_shape=jax.ShapeDtypeStruct((bsz * s, SSD_D_INNER), F32),
        grid=(bsz, nc),
        in_specs=[pl.BlockSpec((L, SSD_XBC), lambda b, c: (b * nc + c, C_XBC // SSD_XBC)),
                  pl.BlockSpec((SUBLANE, SSD_XBC),
                               lambda b, c: (jnp.maximum(b * rows8 + c * (L // SUBLANE) - 1, 0), C_XBC // SSD_XBC)),
                  pl.BlockSpec((L, SSD_D_INNER), lambda b, c: (b * nc + c, C_Z // SSD_D_INNER)),
                  pl.BlockSpec((L, LANE), lambda b, c: (b * nc + c, C_DT // LANE)),
                  full((SSD_CONV, SSD_XBC)), full((1, SSD_XBC)), full((1, LANE)), full((1, LANE)),
                  full((1, SSD_D_INNER)), full((1, SSD_D_INNER)),
                  full((LANE, SSD_D_INNER)), full((L, L))],
        out_specs=pl.BlockSpec((L, SSD_D_INNER), lambda b, c: (b * nc + c, 0)),
        scratch_shapes=[pltpu.VMEM((SSD_HEADS // 2, SSD_STATE, 2 * SSD_HEAD_DIM), F32)],
        compiler_params=_cparams(("parallel", "arbitrary")),
        name="ssd",
    )(proj2d, proj2d, proj2d, proj2d, conv_w, conv_b.reshape(1, SSD_XBC), pad16(dt_bias), pad16(a_log),
      rep64(d_skip), norm_g.reshape(1, SSD_D_INNER),
      jnp.asarray(e16, BF16), jnp.asarray(tril, BF16))


def _mix_kernel(up_ref, upprev_ref, nsa_ref, ssd_ref, ga_ref, gb_ref, gc_ref, h_ref,
                pw_ref, ps_ref, wbp_ref, wbn_ref, wbs_ref, wo_ref, lg_ref, lb_ref, o_ref, *, ts):
    si = pl.program_id(1)
    w0 = MAX_POOL_WINDOW
    prev = jnp.where(si > 0, upprev_ref[...], 0.0)
    xcat = jnp.concatenate([prev, up_ref[...]], axis=0)
    tpos = si * ts + lax.broadcasted_iota(jnp.int32, (ts, POOL_GROUP), 0)
    ygs = []
    for gi, w in enumerate(POOL_WINDOWS):
        x = xcat[:, gi * POOL_GROUP:(gi + 1) * POOL_GROUP]
        acc = x
        d = 1
        while d < w:
            acc = acc + pltpu.roll(acc, d, axis=0)
            d *= 2
        cnt = jnp.minimum(tpos + 1, w).astype(F32)
        r = acc[w0:, :] / cnt - x[w0:, :]
        ygs.append(_dot(r.astype(BF16), pw_ref[gi]))
    y = jnp.concatenate(ygs, axis=1) * ps_ref[...]
    br_a = _dot(y.astype(BF16), wbp_ref[...])
    br_b = _dot(nsa_ref[...].astype(BF16), wbn_ref[...])
    br_c = _dot(ssd_ref[...].astype(BF16), wbs_ref[...])
    mixed = (jax.nn.sigmoid(ga_ref[...]) * br_a + jax.nn.sigmoid(gb_ref[...]) * br_b
             + jax.nn.sigmoid(gc_ref[...]) * br_c)
    mix = _dot(mixed.astype(BF16), wo_ref[...])
    _store_rows(o_ref, _layer_norm(DN_ALPHA * _load_rows(h_ref) + mix, lg_ref[...], lb_ref[...]))


def _mix(proj2d, nsa_out, ssd_y, h3, bsz, s, pool_w, pool_scale, w_br_pool, w_br_nsa, w_br_ssd,
         w_out, ln_g, ln_b):
    ts = MIX_TS
    ns = s // ts
    t = bsz * s
    d = D_MODEL
    w0 = MAX_POOL_WINDOW
    wbn = w_br_nsa.reshape(NSA_KV_HEADS, NSA_HPG, NSA_HEAD_DIM, d).transpose(1, 0, 2, 3).reshape(NSA_WIDTH, d)
    full = lambda shape: pl.BlockSpec(shape, lambda b, si: (0,) * len(shape))
    row = lambda width, cb: pl.BlockSpec((ts, width), lambda b, si: (b * ns + si, cb))
    return pl.pallas_call(
        functools.partial(_mix_kernel, ts=ts),
        out_shape=jax.ShapeDtypeStruct((t,) + ROW_TILE, F32),
        grid=(bsz, ns),
        in_specs=[row(POOL_WIDTH, C_POOL // POOL_WIDTH),
                  pl.BlockSpec((w0, POOL_WIDTH),
                               lambda b, si: (jnp.maximum((b * s + si * ts) // w0 - 1, 0), C_POOL // POOL_WIDTH)),
                  row(NSA_WIDTH, 0), row(d, 0),
                  row(d, C_GATE // d), row(d, C_GATE // d + 1), row(d, C_GATE // d + 2),
                  _row_spec(ts, lambda b, si: b * ns + si),
                  full((len(POOL_WINDOWS), POOL_GROUP, POOL_GROUP)), full((1, POOL_WIDTH)),
                  full((POOL_WIDTH, d)), full((NSA_WIDTH, d)), full((d, d)), full((d, d)),
                  full((1, d)), full((1, d))],
        out_specs=_row_spec(ts, lambda b, si: b * ns + si),
        compiler_params=_cparams(("parallel", "arbitrary")),
        name="mix",
    )(proj2d, proj2d, nsa_out, ssd_y, proj2d, proj2d, proj2d, h3,
      pool_w.astype(BF16), pool_scale.reshape(1, POOL_WIDTH), w_br_pool.astype(BF16), wbn.astype(BF16),
      w_br_ssd.astype(BF16), w_out.astype(BF16), ln_g.reshape(1, d), ln_b.reshape(1, d))


RANK_CHUNK = 256
BUCKET_ROWS = 32


def _route(hb, first_step, rw_ref, rb_ref, tri_ref, bucket_ref, wlo_ref, whi_ref, rank_ref, cnt_ref, run_ref):
    @pl.when(first_step)
    def _():
        run_ref[...] = jnp.zeros(run_ref.shape, F32)

    logits = _dot_nt(rw_ref[...], hb)
    lg = [logits[e:e + 1, :] for e in range(N_EXPERTS)]
    mx = lg[0]
    for e in range(1, N_EXPERTS):
        mx = jnp.maximum(mx, lg[e])
    ex = [jnp.exp(v - mx) for v in lg]
    den = ex[0]
    for e in range(1, N_EXPERTS):
        den = den + ex[e]
    probs = [v / den for v in ex]
    score = [probs[e] + rb_ref[e:e + 1, :] for e in range(N_EXPERTS)]

    def group_vals(vals, grp):
        out = []
        for k in range(EXPERTS_PER_GROUP):
            v = vals[k]
            for gi in range(1, N_EXPERT_GROUPS):
                v = jnp.where(grp == gi, vals[gi * EXPERTS_PER_GROUP + k], v)
            out.append(v)
        return out

    gscore = []
    for gi in range(N_EXPERT_GROUPS):
        sc = score[gi * EXPERTS_PER_GROUP:(gi + 1) * EXPERTS_PER_GROUP]
        best = None
        for a, b in PAIRS:
            v = sc[a] + sc[b]
            best = v if best is None else jnp.maximum(best, v)
        gscore.append(best)
    grp = jnp.zeros(gscore[0].shape, jnp.int32)
    best = gscore[0]
    for gi in range(1, N_EXPERT_GROUPS):
        better = gscore[gi] > best
        grp = jnp.where(better, gi, grp)
        best = jnp.where(better, gscore[gi], best)

    sc = group_vals(score, grp)
    pr = group_vals(probs, grp)
    first = jnp.zeros(grp.shape, jnp.int32)
    fv = sc[0]
    for k in range(1, EXPERTS_PER_GROUP):
        better = sc[k] > fv
        first = jnp.where(better, k, first)
        fv = jnp.where(better, sc[k], fv)
    second = jnp.full(grp.shape, -1, jnp.int32)
    sv = jnp.full(fv.shape, -jnp.inf, F32)
    for k in range(EXPERTS_PER_GROUP):
        better = (first != k) & ((sc[k] > sv) | (second < 0))
        second = jnp.where(better, k, second)
        sv = jnp.where(better, sc[k], sv)
    lo = jnp.minimum(first, second)
    hi = jnp.maximum(first, second)
    p_lo = pr[0]
    p_hi = pr[0]
    for k in range(1, EXPERTS_PER_GROUP):
        p_lo = jnp.where(lo == k, pr[k], p_lo)
        p_hi = jnp.where(hi == k, pr[k], p_hi)
    tot = p_lo + p_hi
    pair = jnp.zeros(grp.shape, jnp.int32)
    for pi, (a, b) in enumerate(PAIRS):
        pair = jnp.where((lo == a) & (hi == b), pi, pair)
    bucket = grp * len(PAIRS) + pair
    bucket_ref[...] = bucket
    wlo_ref[...] = p_lo / tot
    whi_ref[...] = p_hi / tot

    tm = bucket.shape[1]
    onehot = jnp.where(lax.broadcasted_iota(jnp.int32, (BUCKET_ROWS, tm), 0) == bucket, 1.0, 0.0)
    run = run_ref[...]
    ranks = []
    for c in range(tm // RANK_CHUNK):
        oc = onehot[:, c * RANK_CHUNK:(c + 1) * RANK_CHUNK]
        before = _dot(oc.astype(BF16), tri_ref[...]) + jnp.concatenate([run] * (RANK_CHUNK // LANE), axis=1)
        ranks.append(jnp.sum(oc * before, axis=0, keepdims=True))
        run = run + jnp.broadcast_to(jnp.sum(oc, axis=1, keepdims=True), run.shape)
    rank_ref[...] = jnp.concatenate(ranks, axis=1).astype(jnp.int32)
    run_ref[...] = run
    cnt_ref[...] = run


ROUTER_TM = 1024


def _router_kernel(h_ref, *refs):
    _route(_load_rows(h_ref).astype(BF16), pl.program_id(0) == 0, *refs)


def _router(h2d, router_w, router_b):
    t, d = h2d.shape[0], D_MODEL
    tm = ROUTER_TM
    rw = jnp.pad(router_w.T, ((0, LANE - N_EXPERTS), (0, 0))).astype(BF16)
    rb = jnp.broadcast_to(jnp.pad(router_b, (0, LANE - N_EXPERTS))[:, None], (LANE, tm))
    tri = jnp.asarray(np.triu(np.ones((RANK_CHUNK, RANK_CHUNK), np.float32), 1), BF16)
    vec = lambda dt: jax.ShapeDtypeStruct((1, t), dt)
    row = pl.BlockSpec((1, tm), lambda i: (0, i))
    return pl.pallas_call(
        _router_kernel,
        out_shape=(vec(jnp.int32), vec(F32), vec(F32), vec(jnp.int32),
                   jax.ShapeDtypeStruct((BUCKET_ROWS, LANE), F32)),
        grid=(t // tm,),
        in_specs=[_row_spec(tm, lambda i: i),
                  pl.BlockSpec((LANE, d), lambda i: (0, 0)),
                  pl.BlockSpec((LANE, tm), lambda i: (0, 0)),
                  pl.BlockSpec((RANK_CHUNK, RANK_CHUNK), lambda i: (0, 0))],
        out_specs=(row, row, row, row, pl.BlockSpec((BUCKET_ROWS, LANE), lambda i: (0, 0))),
        scratch_shapes=[pltpu.VMEM((BUCKET_ROWS, LANE), F32)],
        compiler_params=_cparams(("arbitrary",)),
        name="router",
    )(h2d, rw, rb, tri)


def _moe_plan(bucket, rank, counts, w_lo, w_hi, t):
    tm = MOE_TM
    n_tiles = t // tm + N_BUCKETS
    p_rows = n_tiles * tm
    tiles_per = (counts + tm - 1) // tm
    tile_end = jnp.cumsum(tiles_per)
    tile_start = tile_end - tiles_per
    in_bucket = bucket[:, None] == jnp.arange(N_BUCKETS)[None, :]
    dest = jnp.sum(jnp.where(in_bucket, (tile_start * tm)[None, :], 0), axis=1) + rank
    payload = jnp.stack([jnp.arange(t, dtype=jnp.int32), lax.bitcast_convert_type(w_lo, jnp.int32),
                         lax.bitcast_convert_type(w_hi, jnp.int32), jnp.ones((t,), jnp.int32)], axis=1)
    plan = jnp.zeros((p_rows, 4), jnp.int32).at[dest].set(payload, unique_indices=True)
    src = plan[:, 0]
    wl = lax.bitcast_convert_type(plan[:, 1], F32)
    wh = lax.bitcast_convert_type(plan[:, 2], F32)
    prow = jnp.arange(p_rows, dtype=jnp.int32)
    spare = t + ((prow // tm) % 2) * tm + prow % tm
    dst = jnp.where(plan[:, 3] > 0, src, spare)
    n_used = tile_end[-1]
    tile_ids = jnp.arange(n_tiles)
    tb = jnp.sum((tile_end[None, :] <= tile_ids[:, None]).astype(jnp.int32), axis=1)
    tb = jnp.where(tile_ids < n_used, tb, tb[jnp.maximum(n_used - 1, 0)])
    tb = jnp.minimum(tb, N_BUCKETS - 1)
    pairs = np.asarray(PAIRS, np.int32)
    grp = tb // len(PAIRS)
    ea = grp * EXPERTS_PER_GROUP + jnp.asarray(pairs[:, 0])[tb % len(PAIRS)]
    eb = grp * EXPERTS_PER_GROUP + jnp.asarray(pairs[:, 1])[tb % len(PAIRS)]
    nvalid = jnp.clip(counts[tb] - (tile_ids - tile_start[tb]) * tm, 0, tm)
    nvalid = jnp.where(tile_ids < n_used, nvalid, 0).astype(jnp.int32)
    return (src.reshape(n_tiles, 1, tm), dst.reshape(n_tiles, 1, tm), wl.reshape(p_rows, 1), wh.reshape(p_rows, 1),
            ea.astype(jnp.int32), eb.astype(jnp.int32), nvalid, n_tiles)


MOE_DMA_GROUPS = 8


def _moe_kernel(ea_ref, eb_ref, nv_ref, src0_ref, srcn_ref, dstp_ref, wl_ref, wh_ref, h_hbm,
                w1a_ref, w3a_ref, w2a_ref, w1b_ref, w3b_ref, w2b_ref, lg_ref, lb_ref,
                o_hbm, xbuf0, xbuf1, obuf0, obuf1, wa1, wa3, wa2, wb1, wb3, wb2, gsem, ssem):
    i = pl.program_id(0)
    tm = xbuf0.shape[0]
    xbufs = (xbuf0, xbuf1)
    obufs = (obuf0, obuf1)
    nv = nv_ref[i]
    nv_p1 = nv_ref[jnp.maximum(i - 1, 0)]
    odd = lax.rem(i, 2) == 1
    even = jnp.logical_not(odd)

    def gather(r, p, row):
        return pltpu.make_async_copy(h_hbm.at[pl.ds(row, 1)], xbufs[p].at[pl.ds(r, 1)], gsem.at[p])

    def scatter(r, p, row):
        return pltpu.make_async_copy(obufs[p].at[pl.ds(r, 1)], o_hbm.at[pl.ds(row, 1)], ssem.at[p])

    def wait_gather(p):
        for r in range(tm):
            gather(r, p, 0).wait()

    def wait_scatter(p):
        for r in range(tm):
            scatter(r, p, 0).wait()

    def tile(p, age):
        q = 1 - p
        half = MOE_DMA_GROUPS // 2
        per = tm // half

        def issue(k):
            for r in range((k % half) * per, (k % half + 1) * per):
                if k < half:
                    gather(r, q, srcn_ref[0, 0, r]).start()
                elif age >= 1:
                    scatter(r, q, dstp_ref[0, 0, r]).start()

        wait_gather(p)
        x = _load_rows(xbufs[p])
        xb = x.astype(BF16)
        issue(0)
        h1 = _dot(xb, wa1[...])
        issue(1)
        h3 = _dot(xb, wa3[...])
        issue(2)
        ya = _dot((jax.nn.silu(h1) * h3).astype(BF16), wa2[...])
        issue(3)
        h1 = _dot(xb, wb1[...])
        issue(4)
        h3 = _dot(xb, wb3[...])
        issue(5)
        yb = _dot((jax.nn.silu(h1) * h3).astype(BF16), wb2[...])
        issue(6)
        y = wl_ref[...] * ya + wh_ref[...] * yb
        issue(7)
        out = _layer_norm(DN_ALPHA * x + y, lg_ref[...], lb_ref[...])
        if age >= 2:
            wait_scatter(p)
        _store_rows(obufs[p], out)

    @pl.when(i == 0)
    def _():
        obuf1[...] = jnp.zeros(obuf1.shape, F32)
        n_real = o_hbm.shape[0] - 2 * tm
        for blk in range(2):
            cp = pltpu.make_async_copy(obuf1, o_hbm.at[pl.ds(n_real + blk * tm, tm)], ssem.at[1])
            cp.start()
            cp.wait()

    prev = jnp.maximum(i - 1, 0)

    @pl.when((nv > 0) & ((i == 0) | (ea_ref[i] != ea_ref[prev])))
    def _():
        wa1[...] = w1a_ref[0, 0].astype(BF16)
        wa3[...] = w3a_ref[0, 0].astype(BF16)
        wa2[...] = w2a_ref[0, 0].astype(BF16)

    @pl.when((nv > 0) & ((i == 0) | (eb_ref[i] != eb_ref[prev])))
    def _():
        wb1[...] = w1b_ref[0, 0].astype(BF16)
        wb3[...] = w3b_ref[0, 0].astype(BF16)
        wb2[...] = w2b_ref[0, 0].astype(BF16)

    @pl.when((nv > 0) & (i == 0))
    def _():
        def start0(r, carry):
            gather(r, 0, src0_ref[0, 0, r]).start()
            return carry
        lax.fori_loop(0, tm, start0, 0)
        tile(0, 0)

    pl.when((nv > 0) & (i == 1))(functools.partial(tile, 1, 1))
    pl.when((nv > 0) & (i >= 2) & even)(functools.partial(tile, 0, 2))
    pl.when((nv > 0) & (i >= 2) & odd)(functools.partial(tile, 1, 2))

    def tail(p):
        q = 1 - p
        wait_gather(p)

        @pl.when(i >= 2)
        def _():
            wait_scatter(p)

        def start(r, carry):
            scatter(r, q, dstp_ref[0, 0, r]).start()
            return carry
        lax.fori_loop(0, tm, start, 0)
        wait_scatter(q)

    is_tail = (nv == 0) & (i >= 1) & (nv_p1 > 0)
    pl.when(is_tail & even)(functools.partial(tail, 0))
    pl.when(is_tail & odd)(functools.partial(tail, 1))


def _moe(h3, bucket, rank, counts, w_lo, w_hi, w1, w3, w2, layer, ln_g, ln_b):
    t, d = h3.shape[0], D_MODEL
    tm = MOE_TM
    assert tm % MOE_DMA_GROUPS == 0
    src, dst, wl, wh, ea, eb, nvalid, n_tiles = _moe_plan(bucket, rank, counts, w_lo, w_hi, t)
    wspec = lambda shape, which: pl.BlockSpec(
        (1,) + shape, (lambda i, ea_r, eb_r, nv_r: (layer, ea_r[i], 0, 0)) if which == 0
        else (lambda i, ea_r, eb_r, nv_r: (layer, eb_r[i], 0, 0)))
    w13 = (1, d, D_EXPERT)
    w2s = (1, D_EXPERT, d)
    smem_tile = lambda fn: pl.BlockSpec((1, 1, tm), lambda i, *_: (fn(i), 0, 0), memory_space=pltpu.SMEM)
    grid_spec = pltpu.PrefetchScalarGridSpec(
        num_scalar_prefetch=3,
        grid=(n_tiles,),
        in_specs=[smem_tile(lambda i: 0),
                  smem_tile(lambda i: jnp.minimum(i + 1, n_tiles - 1)),
                  smem_tile(lambda i: jnp.maximum(i - 1, 0)),
                  pl.BlockSpec((tm, 1), lambda i, *_: (i, 0)),
                  pl.BlockSpec((tm, 1), lambda i, *_: (i, 0)),
                  pl.BlockSpec(memory_space=pl.ANY),
                  wspec(w13, 0), wspec(w13, 0), wspec(w2s, 0),
                  wspec(w13, 1), wspec(w13, 1), wspec(w2s, 1),
                  pl.BlockSpec((1, d), lambda i, *_: (0, 0)),
                  pl.BlockSpec((1, d), lambda i, *_: (0, 0))],
        out_specs=pl.BlockSpec(memory_space=pl.ANY),
        scratch_shapes=[pltpu.VMEM((tm,) + ROW_TILE, F32)] * 4
        + [pltpu.VMEM(w13[1:], BF16), pltpu.VMEM(w13[1:], BF16), pltpu.VMEM(w2s[1:], BF16)] * 2
        + [pltpu.SemaphoreType.DMA((2,)), pltpu.SemaphoreType.DMA((2,))],
    )
    return pl.pallas_call(
        _moe_kernel,
        out_shape=jax.ShapeDtypeStruct((t + 2 * tm,) + ROW_TILE, F32),
        grid_spec=grid_spec,
        compiler_params=_cparams(("arbitrary",)),
        name="moe",
    )(ea, eb, nvalid, src, src, dst, wl, wh, h3, w1, w3, w2, w1, w3, w2,
      ln_g.reshape(1, d), ln_b.reshape(1, d))


def kernel(x, ln0_g, ln0_b, w_in, pool_w, pool_scale, cmp_k_pe, cmp_k_w1, cmp_k_w2, cmp_v_pe, cmp_v_w1, cmp_v_w2, rel_bias, conv_w, conv_b, dt_bias, a_log, d_skip, ssd_norm_g, w_br_pool, w_br_nsa, w_br_ssd, w_out, ln1_g, ln1_b, router_w, router_b, exp_w1, exp_w3, exp_w2, ln2_g, ln2_b):
    bsz, s, d = x.shape
    assert d == D_MODEL and s % MIX_TS == 0 and s // CMP_STRIDE == LANE and (bsz * s) % PROJ_TM == 0
    t = bsz * s
    ts, ws, bc = _nsa_bias_tables(rel_bias)
    et, ovt = _nsa_consts(s)
    h = x.reshape(t, d)
    for i in range(DEPTH):
        if i == 0:
            proj, h = _proj(h, _prep_w_in(w_in[i]), t, ln=(ln0_g, ln0_b))
        else:
            proj = _proj(h, _prep_w_in(w_in[i]), t)
        kc, vc = _compress(proj, bsz, s, cmp_k_pe[i], cmp_k_w1[i], cmp_k_w2[i],
                           cmp_v_pe[i], cmp_v_w1[i], cmp_v_w2[i])
        nsa_out = _nsa(proj, kc, vc, ts, ws, bc, et, ovt, bsz, s)
        ssd_y = _ssd(proj, bsz, s, conv_w[i], conv_b[i], dt_bias[i], a_log[i], d_skip[i], ssd_norm_g[i])
        h1 = _mix(proj, nsa_out, ssd_y, h, bsz, s, pool_w[i], pool_scale[i], w_br_pool[i], w_br_nsa[i],
                  w_br_ssd[i], w_out[i], ln1_g[i], ln1_b[i])
        bucket, w_lo, w_hi, rank, cnt = _router(h1, router_w, router_b)
        counts = cnt[:N_BUCKETS, 0].astype(jnp.int32)
        h = _moe(h1, bucket[0], rank[0], counts, w_lo[0], w_hi[0], exp_w1, exp_w3, exp_w2, i,
                 ln2_g[i], ln2_b[i])
    return h[:t].reshape(bsz, s, d)
```

```python
import functools
import math

import numpy as np
import jax
import jax.numpy as jnp
from jax import lax
from jax.experimental import pallas as pl
from jax.experimental.pallas import tpu as pltpu

F32 = jnp.float32
BF16 = jnp.bfloat16

D_MODEL = 1024
DEPTH = 2
DN_ALPHA = (2.0 * DEPTH) ** 0.25
LN_EPS = 1e-5
NEG = -1e30
BIG = 1e6

POOL_WINDOWS = (2, 4, 8, 16)
POOL_GROUP = 128
POOL_WIDTH = 512
MAX_POOL_WINDOW = 16

NSA_HEAD_DIM = 64
NSA_KV_HEADS = 2
NSA_HPG = 4
NSA_HEADS = 8
NSA_WIDTH = 512
CMP_LEN = 32
CMP_STRIDE = 16
CMP_HIDDEN = 128
SLC_LEN = 64
SLC_TOPN = 8
WIN_LEN = 512
Q_BLOCK = 128
REL_BUCKETS = 32
REL_MAX_DIST = 128

SSD_D_INNER = 1024
SSD_HEAD_DIM = 64
SSD_HEADS = 16
SSD_GROUPS = 4
SSD_STATE = 128
SSD_CONV = 4
SSD_CHUNK = 128
SSD_XBC = 2048

N_EXPERTS = 16
N_EXPERT_GROUPS = 4
EXPERTS_PER_GROUP = 4
D_EXPERT = 512
PAIRS = ((0, 1), (0, 2), (0, 3), (1, 2), (1, 3), (2, 3))
N_BUCKETS = N_EXPERT_GROUPS * len(PAIRS)

LANE = 128
SUBLANE = 8
VMEM_LIMIT = 48 * 1024 * 1024

C_POOL = 0
C_Q = 512
C_KV = 1024
C_NG = 1792
C_DT = 1920
C_XBC = 2048
C_Z = 4096
C_GATE = 5120
D_PAD = 8192

PROJ_TM = 1024
PROJ_TN = 2048
MIX_TS = 512
MOE_TM = 256


def _dot(a, b):
    return jnp.dot(a, b, preferred_element_type=F32)


def _dot_nt(a, b):
    return lax.dot_general(a, b, (((1,), (1,)), ((), ())), preferred_element_type=F32)


def _dot_tn(a, b):
    return lax.dot_general(a, b, (((0,), (0,)), ((), ())), preferred_element_type=F32)


def _split3(x):
    x1 = x.astype(BF16)
    r1 = x - x1.astype(F32)
    x2 = r1.astype(BF16)
    x3 = (r1 - x2.astype(F32)).astype(BF16)
    return x1, x2, x3


def _layer_norm(x, g, b):
    mu = jnp.mean(x, -1, keepdims=True)
    xc = x - mu
    var = jnp.mean(xc * xc, -1, keepdims=True)
    return xc * lax.rsqrt(var + LN_EPS) * g + b


def _cparams(sem):
    return pltpu.CompilerParams(dimension_semantics=sem, vmem_limit_bytes=VMEM_LIMIT)


ROW_TILE = (D_MODEL,)


def _load_rows(ref):
    return ref[...]


def _store_rows(ref, val):
    ref[...] = val


def _row_spec(n, index):
    return pl.BlockSpec((n,) + ROW_TILE, lambda *ids: (index(*ids), 0))


def _proj_kernel(h_ref, w_ref, o_ref, hb_ref):
    @pl.when(pl.program_id(1) == 0)
    def _():
        hb_ref[...] = _load_rows(h_ref).astype(BF16)

    o_ref[...] = _dot(hb_ref[...], w_ref[...])


def _ln_proj_kernel(x_ref, g_ref, b_ref, w_ref, o_ref, h_ref, hb_ref):
    @pl.when(pl.program_id(1) == 0)
    def _():
        h = _layer_norm(x_ref[...], g_ref[...], b_ref[...])
        _store_rows(h_ref, h)
        hb_ref[...] = h.astype(BF16)

    o_ref[...] = _dot(hb_ref[...], w_ref[...])


def _proj(h2d, w_pad, t, ln=None):
    d = D_MODEL
    grid = (t // PROJ_TM, D_PAD // PROJ_TN)
    w_spec = pl.BlockSpec((d, PROJ_TN), lambda i, j: (0, j))
    o_spec = pl.BlockSpec((PROJ_TM, PROJ_TN), lambda i, j: (i, j))
    o_shape = jax.ShapeDtypeStruct((t, D_PAD), F32)
    scratch = [pltpu.VMEM((PROJ_TM, d), BF16)]
    params = _cparams(("parallel", "arbitrary"))
    if ln is None:
        return pl.pallas_call(
            _proj_kernel, out_shape=o_shape, grid=grid,
            in_specs=[_row_spec(PROJ_TM, lambda i, j: i), w_spec], out_specs=o_spec,
            scratch_shapes=scratch, compiler_params=params, name="proj",
        )(h2d, w_pad)
    vec = pl.BlockSpec((1, d), lambda i, j: (0, 0))
    return pl.pallas_call(
        _ln_proj_kernel, out_shape=(o_shape, jax.ShapeDtypeStruct((t,) + ROW_TILE, F32)), grid=grid,
        in_specs=[_row_spec(PROJ_TM, lambda i, j: i), vec, vec, w_spec],
        out_specs=(o_spec, _row_spec(PROJ_TM, lambda i, j: i)),
        scratch_shapes=scratch, compiler_params=params, name="ln_proj",
    )(h2d, ln[0].reshape(1, d), ln[1].reshape(1, d), w_pad)


def _pad_cols(w, n):
    return jnp.pad(w, ((0, 0), (0, n - w.shape[1])))


def _prep_w_in(w_in):
    d = w_in.shape[0]
    o = 0
    w_pool = w_in[:, o:o + POOL_WIDTH]; o += POOL_WIDTH
    w_q = w_in[:, o:o + NSA_WIDTH]; o += NSA_WIDTH
    w_kv = w_in[:, o:o + 768]; o += 768
    w_ng = w_in[:, o:o + 24]; o += 24
    w_z = w_in[:, o:o + SSD_D_INNER]; o += SSD_D_INNER
    w_xbc = w_in[:, o:o + SSD_XBC]; o += SSD_XBC
    w_dt = w_in[:, o:o + SSD_HEADS]; o += SSD_HEADS
    w_gate = w_in[:, o:o + 3 * D_MODEL]
    w_q = w_q.reshape(d, NSA_KV_HEADS, NSA_HPG, NSA_HEAD_DIM).transpose(0, 2, 1, 3).reshape(d, NSA_WIDTH)
    w = jnp.concatenate([w_pool, w_q, w_kv, _pad_cols(w_ng, LANE), _pad_cols(w_dt, LANE),
                         w_xbc, w_z, w_gate], axis=1)
    return w.astype(BF16)


def _compress_one(tok_ref, pe_ref, w1_ref, w2_ref):
    n16 = tok_ref.shape[0] // CMP_STRIDE
    acc_a = jnp.zeros((n16, 2 * CMP_HIDDEN), F32)
    acc_b = jnp.zeros((n16, 2 * CMP_HIDDEN), F32)
    for l in range(CMP_STRIDE):
        rows = tok_ref[pl.ds(l, n16, stride=CMP_STRIDE), :]
        xa = (rows + pe_ref[l:l + 1, :]).astype(BF16)
        xb = (rows + pe_ref[CMP_STRIDE + l:CMP_STRIDE + l + 1, :]).astype(BF16)
        acc_a = acc_a + _dot(xa, w1_ref[l])
        acc_b = acc_b + _dot(xb, w1_ref[CMP_STRIDE + l])
    hid = acc_a + pltpu.roll(acc_b, n16 - 1, axis=0)
    hid = jax.nn.gelu(hid)
    return _dot(hid.astype(BF16), w2_ref[...])


def _compress_kernel(kt_ref, vt_ref, pek_ref, w1k_ref, w2k_ref, pev_ref, w1v_ref, w2v_ref,
                     kc_ref, vc_ref):
    kc_ref[0] = _compress_one(kt_ref, pek_ref, w1k_ref, w2k_ref).astype(BF16)
    vc_ref[0] = _compress_one(vt_ref, pev_ref, w1v_ref, w2v_ref).T.astype(BF16)


def _blockdiag2(w):
    z = jnp.zeros_like(w)
    return jnp.concatenate([jnp.concatenate([w, z], -1), jnp.concatenate([z, w], -1)], -2)


def _compress(proj2d, bsz, s, pe_k, w1_k, w2_k, pe_v, w1_v, w2_v):
    n16 = s // CMP_STRIDE

    def prep(pe, w1, w2):
        pe2 = jnp.concatenate([pe, pe], axis=-1)
        w1b = _blockdiag2(w1.reshape(CMP_LEN, NSA_HEAD_DIM, CMP_HIDDEN)).astype(BF16)
        w2b = _blockdiag2(w2).astype(BF16)
        return pe2, w1b, w2b

    pk = prep(pe_k, w1_k, w2_k)
    pv = prep(pe_v, w1_v, w2_v)
    full = lambda shape: pl.BlockSpec(shape, lambda b: (0,) * len(shape))
    wspecs = [full((CMP_LEN, LANE)), full((CMP_LEN, LANE, 2 * CMP_HIDDEN)), full((2 * CMP_HIDDEN, LANE))]
    return pl.pallas_call(
        _compress_kernel,
        out_shape=(jax.ShapeDtypeStruct((bsz, n16, LANE), BF16),
                   jax.ShapeDtypeStruct((bsz, n16, LANE), BF16)),
        grid=(bsz,),
        in_specs=[pl.BlockSpec((s, LANE), lambda b: (b, C_KV // LANE)),
                  pl.BlockSpec((s, LANE), lambda b: (b, C_KV // LANE + 1))] + wspecs + wspecs,
        out_specs=(pl.BlockSpec((1, n16, LANE), lambda b: (b, 0, 0)),
                   pl.BlockSpec((1, n16, LANE), lambda b: (b, 0, 0))),
        compiler_params=_cparams(("parallel",)),
        name="nsa_compress",
    )(proj2d, proj2d, *pk, *pv)


def _rel_bucket(dist):
    n = jnp.maximum(dist, 0)
    max_exact = REL_BUCKETS // 2
    nf = jnp.maximum(n, 1).astype(F32)
    large = max_exact + (jnp.log(nf / max_exact) / math.log(REL_MAX_DIST / max_exact)
                         * (REL_BUCKETS - max_exact)).astype(jnp.int32)
    large = jnp.minimum(large, REL_BUCKETS - 1)
    return jnp.where(n < max_exact, n, large)


LOG2E = math.log2(math.e)
WIN_BACK = WIN_LEN // Q_BLOCK
WIN_KEYS = (WIN_BACK + 1) * Q_BLOCK
WIN_STRIP_TILES = 2 * WIN_BACK + 1
SEL_TILES = 4
SEL_KEYS = SEL_TILES * Q_BLOCK
SEL_STRIP_TILES = 3 * SEL_TILES - 1
CMP_BIAS_ROWS = 2 * LANE
PV_ROWS = NSA_HEAD_DIM + 16


def _nsa_bias_tables(rel_bias):
    tbl = rel_bias.astype(F32)
    far = tbl[REL_BUCKETS - 1]
    hq = NSA_HPG * Q_BLOCK

    def lookup(dist):
        onehot = (_rel_bucket(jnp.asarray(dist))[..., None] == jnp.arange(REL_BUCKETS)).astype(F32)
        return jnp.einsum("...b,bh->...h", onehot, tbl, precision=lax.Precision.HIGHEST)

    k = np.arange(Q_BLOCK)[None, :, None]
    q = np.arange(Q_BLOCK)[None, None, :]

    def strip(first, last, visible):
        n = first - last + 1
        dist = np.arange(first, last - 1, -1)[:, None, None] * Q_BLOCK + q - k
        v = jnp.where(jnp.asarray(visible(dist))[..., None], (lookup(dist) - far) * LOG2E, NEG)
        v = v.reshape(n, Q_BLOCK, Q_BLOCK, NSA_KV_HEADS, NSA_HPG).transpose(3, 0, 1, 4, 2)
        return v.reshape(NSA_KV_HEADS, n * Q_BLOCK, hq)

    ts = strip(2 * SEL_TILES - 1, 1 - SEL_TILES, lambda d: d >= 0)
    ws = strip(WIN_BACK, -WIN_BACK, lambda d: (d >= 0) & (d < WIN_LEN))
    r = np.arange(CMP_BIAS_ROWS)[:, None]
    dist_c = q[0] - CMP_STRIDE * (r - LANE) - (CMP_LEN - 1)
    bc = jnp.where(jnp.asarray(dist_c >= 0)[..., None], lookup(dist_c) * LOG2E, NEG)
    bc = bc.reshape(CMP_BIAS_ROWS, Q_BLOCK, NSA_KV_HEADS, NSA_HPG).transpose(2, 0, 3, 1)
    bc = bc.reshape(NSA_KV_HEADS, CMP_BIAS_ROWS, hq)
    return ts, ws, bc


def _nsa_consts(s):
    n_slc = s // SLC_LEN
    et = (np.arange(s)[:, None] // SLC_LEN == np.arange(LANE)[None, :]).astype(np.float32)
    n_cmp = (s - CMP_LEN) // CMP_STRIDE + 1
    c0 = np.arange(n_cmp) * CMP_STRIDE
    s0 = np.arange(n_slc) * SLC_LEN
    ov = np.clip(np.minimum(c0[None, :] + CMP_LEN, s0[:, None] + SLC_LEN)
                 - np.maximum(c0[None, :], s0[:, None]), 0, None) / CMP_LEN
    ovt = np.zeros((n_slc, LANE), np.float32)
    ovt[:, :n_cmp] = ov
    return jnp.asarray(et, BF16), jnp.asarray(ovt, BF16)


def _nsa_kernel(q_ref, ng_ref, ks_ref, vs_ref, kw_ref, vw_ref, kc_ref, vct_ref,
                ts_ref, ws_ref, bc_ref, et_ref, ovt_ref, o_ref,
                ksb_ref, kwb_ref, vst_ref, vwt_ref, *, nq, n_slc):
    qi = pl.program_id(1)
    hq = NSA_HPG * Q_BLOCK
    dh = NSA_HEAD_DIM
    groups = range(NSA_KV_HEADS)
    gsl = [slice(g * dh, (g + 1) * dh) for g in groups]

    @pl.when(qi == 0)
    def _():
        ksb_ref[:, 0:LANE] = ks_ref[...].astype(BF16)
        ksb_ref[:, LANE:] = et_ref[...]
        kwb_ref[...] = kw_ref[...].astype(BF16)
        ones_rows = jnp.where(lax.broadcasted_iota(jnp.int32, (PV_ROWS - dh, ks_ref.shape[0]), 0) == 0, 1.0, 0.0)
        for g in groups:
            vst_ref[g, dh:, :] = ones_rows.astype(BF16)
            vwt_ref[g, dh:, :] = ones_rows.astype(BF16)
        for kt in range(nq):
            sl = slice(kt * Q_BLOCK, (kt + 1) * Q_BLOCK)
            vs_t = vs_ref[sl, :].T
            vw_t = vw_ref[sl, :].T
            for g in groups:
                vst_ref[g, 0:dh, sl] = vs_t[gsl[g], :].astype(BF16)
                vwt_ref[g, 0:dh, sl] = vw_t[gsl[g], :].astype(BF16)

    n_super = qi // SEL_TILES + 1
    for n in range(1, nq // SEL_TILES + 1):
        pl.when(n_super == n)(functools.partial(
            _nsa_step, q_ref, ng_ref, kc_ref, vct_ref, ts_ref, ws_ref, bc_ref, ovt_ref, o_ref,
            ksb_ref, kwb_ref, vst_ref, vwt_ref, n_slc=n_slc, n_sel=n))


def _nsa_step(q_ref, ng_ref, kc_ref, vct_ref, ts_ref, ws_ref, bc_ref, ovt_ref, o_ref,
              ksb_ref, kwb_ref, vst_ref, vwt_ref, *, n_slc, n_sel):
    qi = pl.program_id(1)
    hq = NSA_HPG * Q_BLOCK
    dh = NSA_HEAD_DIM
    groups = range(NSA_KV_HEADS)
    gsl = [slice(g * dh, (g + 1) * dh) for g in groups]
    row_lo = lax.broadcasted_iota(jnp.int32, (LANE, hq), 0) < dh

    def heads4(m):
        return jnp.concatenate([m] * NSA_HPG, axis=1)

    gates_t = jax.nn.sigmoid(ng_ref[...]).T
    q_t = jnp.concatenate(
        [(q_ref[:, h * LANE:(h + 1) * LANE] * (dh ** -0.5 * LOG2E)).T for h in range(NSA_HPG)], axis=1)
    qg_t = [jnp.where(row_lo if g == 0 else jnp.logical_not(row_lo), q_t, 0.0).astype(BF16) for g in groups]

    boff = pl.multiple_of(LANE - SUBLANE * qi, SUBLANE)
    jrow = lax.broadcasted_iota(jnp.int32, (n_slc, LANE), 0)
    tj = qi * Q_BLOCK + lax.broadcasted_iota(jnp.int32, (n_slc, LANE), 1)
    jt = lax.shift_right_logical(tj, SLC_LEN.bit_length() - 1)
    forced = (jrow == 0) | (jrow == jt) | (jrow == jt - 1)
    valid = jrow * SLC_LEN <= tj
    o_c = []
    q_aug = []
    for g in groups:
        s = _dot(kc_ref[0], qg_t[g]) + bc_ref[g, pl.ds(boff, LANE), :]
        mx = jnp.max(s, axis=0, keepdims=True)
        p = jnp.exp2(s - mx)
        lsum = jnp.sum(p, axis=0, keepdims=True)
        pb = (p * jnp.where(mx > 0.5 * NEG, 1.0 / lsum, 0.0)).astype(BF16)
        o_c.append(_dot(vct_ref[0, gsl[g], :], pb))
        imp4 = _dot(ovt_ref[...], pb)
        imp = imp4[:, 0:LANE]
        for h in range(1, NSA_HPG):
            imp = imp + imp4[:, h * LANE:(h + 1) * LANE]
        val = jnp.where(forced, BIG, jnp.where(valid, imp, -BIG))
        rank = jnp.zeros((n_slc, LANE), F32)
        for i in range(n_slc):
            ci = val[i:i + 1, :]
            rank = rank + jnp.where(ci > val, 1.0, 0.0) + jnp.where(ci == val, (jrow > i).astype(F32), 0.0)
        block_rows = heads4(jnp.where(rank < SLC_TOPN, 0.0, NEG)).astype(BF16)
        q_aug.append(jnp.concatenate(
            [qg_t[g], block_rows, jnp.zeros((LANE - n_slc, hq), BF16)], axis=0))

    diag = qi - SEL_TILES * (n_sel - 1)
    nk = n_sel * SEL_KEYS
    lo = max(n_sel - 2, 0) * SEL_KEYS
    toff = pl.multiple_of((2 * SEL_TILES - 1 - diag) * Q_BLOCK - (nk - lo - SEL_KEYS), Q_BLOCK)
    o_s = []
    for g in groups:
        parts = [(lo, nk, _dot(ksb_ref[lo:nk, :], q_aug[g]) + ts_ref[g, pl.ds(toff, nk - lo), :])]
        if lo > 0:
            parts.append((0, lo, _dot(ksb_ref[0:lo, :], q_aug[g])))
        m = parts[0][2].max(axis=0, keepdims=True)
        for _, _, s in parts[1:]:
            m = jnp.maximum(m, s.max(axis=0, keepdims=True))
        acc = None
        for a, b, s in parts:
            pv = _dot(vst_ref[g, :, a:b], jnp.exp2(s - m).astype(BF16))
            acc = pv if acc is None else acc + pv
        o_s.append(acc[0:dh] * (1.0 / acc[dh:dh + 1]))

    w0 = jnp.maximum(qi - WIN_BACK, 0)
    woff = pl.multiple_of(w0 * Q_BLOCK, Q_BLOCK)
    boff_w = pl.multiple_of((WIN_BACK - (qi - w0)) * Q_BLOCK, Q_BLOCK)
    kw_bf = kwb_ref[pl.ds(woff, WIN_KEYS), :]
    o_w = []
    for g in groups:
        s = _dot(kw_bf, qg_t[g]) + ws_ref[g, pl.ds(boff_w, WIN_KEYS), :]
        p = jnp.exp2(s - jnp.max(s, axis=0, keepdims=True)).astype(BF16)
        acc = _dot(vwt_ref[g, :, pl.ds(woff, WIN_KEYS)], p)
        o_w.append(acc[0:dh] * (1.0 / acc[dh:dh + 1]))

    for h in range(NSA_HPG):
        hl = slice(h * LANE, (h + 1) * LANE)
        halves = []
        for g in groups:
            c = g * NSA_HPG + h
            halves.append(gates_t[c:c + 1, :] * o_c[g][:, hl]
                          + gates_t[NSA_HEADS + c:NSA_HEADS + c + 1, :] * o_s[g][:, hl]
                          + gates_t[2 * NSA_HEADS + c:2 * NSA_HEADS + c + 1, :] * o_w[g][:, hl])
        o_ref[:, hl] = jnp.concatenate(halves, axis=0).T.astype(o_ref.dtype)


def _nsa(proj2d, kc, vct, ts, ws, bc, et, ovt, bsz, s):
    nq = s // Q_BLOCK
    n_slc = s // SLC_LEN
    hq = NSA_HPG * Q_BLOCK
    kvb = C_KV // LANE
    kv_spec = lambda blk: pl.BlockSpec((s, LANE), lambda b, qi: (b, kvb + blk))
    full = lambda shape: pl.BlockSpec(shape, lambda b, qi: (0,) * len(shape), pipeline_mode=pl.Buffered(1))
    return pl.pallas_call(
        functools.partial(_nsa_kernel, nq=nq, n_slc=n_slc),
        out_shape=jax.ShapeDtypeStruct((bsz * s, NSA_WIDTH), BF16),
        grid=(bsz, nq),
        in_specs=[pl.BlockSpec((Q_BLOCK, NSA_WIDTH), lambda b, qi: (b * nq + qi, C_Q // NSA_WIDTH)),
                  pl.BlockSpec((Q_BLOCK, LANE), lambda b, qi: (b * nq + qi, C_NG // LANE)),
                  kv_spec(2), kv_spec(3), kv_spec(4), kv_spec(5),
                  pl.BlockSpec((1, s // CMP_STRIDE, LANE), lambda b, qi: (b, 0, 0)),
                  pl.BlockSpec((1, LANE, s // CMP_STRIDE), lambda b, qi: (b, 0, 0)),
                  full((NSA_KV_HEADS, SEL_STRIP_TILES * Q_BLOCK, hq)),
                  full((NSA_KV_HEADS, WIN_STRIP_TILES * Q_BLOCK, hq)),
                  full((NSA_KV_HEADS, CMP_BIAS_ROWS, hq)),
                  full((s, LANE)),
                  full((n_slc, LANE))],
        out_specs=pl.BlockSpec((Q_BLOCK, NSA_WIDTH), lambda b, qi: (b * nq + qi, 0)),
        scratch_shapes=[pltpu.VMEM((s, 2 * LANE), BF16), pltpu.VMEM((s, LANE), BF16),
                        pltpu.VMEM((NSA_KV_HEADS, PV_ROWS, s), BF16), pltpu.VMEM((NSA_KV_HEADS, PV_ROWS, s), BF16)],
        compiler_params=_cparams(("parallel", "arbitrary")),
        name="nsa_attn",
    )(proj2d, proj2d, proj2d, proj2d, proj2d, proj2d, kc, vct, ts, ws, bc, et, ovt)


def _ssd_kernel(xbc_ref, prev_ref, z_ref, dt_ref, cw_ref, cb_ref, dtb_ref, alog_ref,
                dskip_ref, ng_ref, e16_ref, tril_ref, o_ref, state_ref):
    c = pl.program_id(1)
    L = SSD_CHUNK
    pair_w = 2 * SSD_HEAD_DIM

    @pl.when(c == 0)
    def _():
        state_ref[...] = jnp.zeros(state_ref.shape, F32)

    xcat = jnp.concatenate([jnp.where(c > 0, prev_ref[...], 0.0), xbc_ref[...]], axis=0)
    acc = cw_ref[0:1, :] * xcat
    for k in range(1, SSD_CONV):
        acc = pltpu.roll(acc, 1, axis=0) + cw_ref[k:k + 1, :] * xcat
    xbc = jax.nn.silu(acc[SUBLANE:, :] + cb_ref[...])
    xs = xbc[:, :SSD_D_INNER]
    bmat = xbc[:, SSD_D_INNER:SSD_D_INNER + SSD_GROUPS * SSD_STATE]
    cmat = xbc[:, SSD_D_INNER + SSD_GROUPS * SSD_STATE:]

    dt = jax.nn.softplus(dt_ref[...] + dtb_ref[...])
    tril = tril_ref[...]
    e16 = e16_ref[...]
    d1, d2, d3 = _split3(dt)
    dt_x = _dot(d1, e16) + _dot(d2, e16) + _dot(d3, e16)
    a_c = dt * (-jnp.exp(alog_ref[...]) * LOG2E)
    c1, c2, c3 = _split3(a_c)
    acs = _dot(tril, c1) + _dot(tril, c2) + _dot(tril, c3)
    x1, x2, x3 = _split3(acs)
    acs_x = _dot(x1, e16) + _dot(x2, e16) + _dot(x3, e16)
    acs_t = acs.T
    last = acs_x[L - 1:L, :]
    eacs = jnp.exp2(acs_x)
    decay_in = jnp.exp2(last - acs_x)
    cdec = jnp.exp2(last)
    xdt = xs * dt_x
    wst = xdt * decay_in

    row = lax.broadcasted_iota(jnp.int32, (L, L), 0)
    lane = lax.broadcasted_iota(jnp.int32, (L, L), 1)
    tri = row >= lane
    lo = lane < SSD_HEAD_DIM

    ys = []
    for g in range(SSD_GROUPS):
        cg = cmat[:, g * SSD_STATE:(g + 1) * SSD_STATE].astype(BF16)
        bg = bmat[:, g * SSD_STATE:(g + 1) * SSD_STATE].astype(BF16)
        cb = _dot_nt(cg, bg)
        for pr in range(2):
            pp = g * 2 + pr
            ls = slice(pp * pair_w, (pp + 1) * pair_w)
            xp = xdt[:, ls]
            yd = None
            for e in range(2):
                hd = 2 * pp + e
                diff = acs[:, hd:hd + 1] - acs_t[hd:hd + 1, :]
                seg = jnp.exp2(jnp.where(tri, diff, NEG))
                pm = (cb * seg).astype(BF16)
                xe = jnp.where(lo if e == 0 else jnp.logical_not(lo), xp, 0.0).astype(BF16)
                term = _dot(pm, xe)
                yd = term if yd is None else yd + term
            st = state_ref[pp]
            yo = _dot(cg, st.astype(BF16)) * eacs[:, ls]
            ys.append(yd + yo)
            state_ref[pp] = cdec[:, ls] * st + _dot_tn(bg, wst[:, ls].astype(BF16))

    y = jnp.concatenate(ys, axis=1) + xs * dskip_ref[...]
    y = y * jax.nn.silu(z_ref[...])
    gw = SSD_D_INNER // SSD_GROUPS
    outs = []
    for gi in range(SSD_GROUPS):
        yg = y[:, gi * gw:(gi + 1) * gw]
        outs.append(yg * lax.rsqrt(jnp.mean(yg * yg, -1, keepdims=True) + LN_EPS))
    o_ref[...] = (jnp.concatenate(outs, axis=1) * ng_ref[...]).astype(o_ref.dtype)


def _ssd(proj2d, bsz, s, conv_w, conv_b, dt_bias, a_log, d_skip, norm_g):
    nc = s // SSD_CHUNK
    L = SSD_CHUNK
    pad16 = lambda v: jnp.pad(v.reshape(1, SSD_HEADS), ((0, 0), (0, LANE - SSD_HEADS)))
    rep64 = lambda v: jnp.repeat(v, SSD_HEAD_DIM).reshape(1, SSD_D_INNER)
    e16 = np.zeros((LANE, SSD_D_INNER), np.float32)
    e16[np.arange(SSD_D_INNER) // SSD_HEAD_DIM, np.arange(SSD_D_INNER)] = 1.0
    tril = np.tril(np.ones((L, L), np.float32))
    full = lambda shape: pl.BlockSpec(shape, lambda b, c: (0,) * len(shape))
    rows8 = s // SUBLANE
    return pl.pallas_call(
        _ssd_kernel,
        out_shape=jax.ShapeDtypeStruct((bsz * s, SSD_D_INNER), BF16),
        grid=(bsz, nc),
        in_specs=[pl.BlockSpec((L, SSD_XBC), lambda b, c: (b * nc + c, C_XBC // SSD_XBC)),
                  pl.BlockSpec((SUBLANE, SSD_XBC),
                               lambda b, c: (jnp.maximum(b * rows8 + c * (L // SUBLANE) - 1, 0), C_XBC // SSD_XBC)),
                  pl.BlockSpec((L, SSD_D_INNER), lambda b, c: (b * nc + c, C_Z // SSD_D_INNER)),
                  pl.BlockSpec((L, LANE), lambda b, c: (b * nc + c, C_DT // LANE)),
                  full((SSD_CONV, SSD_XBC)), full((1, SSD_XBC)), full((1, LANE)), full((1, LANE)),
                  full((1, SSD_D_INNER)), full((1, SSD_D_INNER)),
                  full((LANE, SSD_D_INNER)), full((L, L))],
        out_specs=pl.BlockSpec((L, SSD_D_INNER), lambda b, c: (b * nc + c, 0)),
        scratch_shapes=[pltpu.VMEM((SSD_HEADS // 2, SSD_STATE, 2 * SSD_HEAD_DIM), F32)],
        compiler_params=_cparams(("parallel", "arbitrary")),
        name="ssd",
    )(proj2d, proj2d, proj2d, proj2d, conv_w, conv_b.reshape(1, SSD_XBC), pad16(dt_bias), pad16(a_log),
      rep64(d_skip), norm_g.reshape(1, SSD_D_INNER),
      jnp.asarray(e16, BF16), jnp.asarray(tril, BF16))


def _mix_kernel(up_ref, upprev_ref, nsa_ref, ssd_ref, ga_ref, gb_ref, gc_ref, h_ref,
                pw_ref, ps_ref, wbp_ref, wbn_ref, wbs_ref, wo_ref, lg_ref, lb_ref, o_ref, *, ts):
    si = pl.program_id(1)
    w0 = MAX_POOL_WINDOW
    prev = jnp.where(si > 0, upprev_ref[...], 0.0)
    xcat = jnp.concatenate([prev, up_ref[...]], axis=0)
    tpos = si * ts + lax.broadcasted_iota(jnp.int32, (ts, POOL_GROUP), 0)
    ygs = []
    for gi, w in enumerate(POOL_WINDOWS):
        x = xcat[:, gi * POOL_GROUP:(gi + 1) * POOL_GROUP]
        acc = x
        d = 1
        while d < w:
            acc = acc + pltpu.roll(acc, d, axis=0)
            d *= 2
        cnt = jnp.minimum(tpos + 1, w).astype(F32)
        r = acc[w0:, :] / cnt - x[w0:, :]
        ygs.append(_dot(r.astype(BF16), pw_ref[gi]))
    y = jnp.concatenate(ygs, axis=1) * ps_ref[...]
    br_a = _dot(y.astype(BF16), wbp_ref[...])
    br_b = _dot(nsa_ref[...], wbn_ref[...])
    br_c = _dot(ssd_ref[...], wbs_ref[...])
    mixed = (jax.nn.sigmoid(ga_ref[...]) * br_a + jax.nn.sigmoid(gb_ref[...]) * br_b
             + jax.nn.sigmoid(gc_ref[...]) * br_c)
    mix = _dot(mixed.astype(BF16), wo_ref[...])
    _store_rows(o_ref, _layer_norm(DN_ALPHA * _load_rows(h_ref) + mix, lg_ref[...], lb_ref[...]))


def _mix(proj2d, nsa_out, ssd_y, h3, bsz, s, pool_w, pool_scale, w_br_pool, w_br_nsa, w_br_ssd,
         w_out, ln_g, ln_b):
    ts = MIX_TS
    ns = s // ts
    t = bsz * s
    d = D_MODEL
    w0 = MAX_POOL_WINDOW
    wbn = w_br_nsa.reshape(NSA_KV_HEADS, NSA_HPG, NSA_HEAD_DIM, d).transpose(1, 0, 2, 3).reshape(NSA_WIDTH, d)
    full = lambda shape: pl.BlockSpec(shape, lambda b, si: (0,) * len(shape), pipeline_mode=pl.Buffered(1))
    row = lambda width, cb: pl.BlockSpec((ts, width), lambda b, si: (b * ns + si, cb))
    return pl.pallas_call(
        functools.partial(_mix_kernel, ts=ts),
        out_shape=jax.ShapeDtypeStruct((t,) + ROW_TILE, F32),
        grid=(bsz, ns),
        in_specs=[row(POOL_WIDTH, C_POOL // POOL_WIDTH),
                  pl.BlockSpec((w0, POOL_WIDTH),
                               lambda b, si: (jnp.maximum((b * s + si * ts) // w0 - 1, 0), C_POOL // POOL_WIDTH)),
                  row(NSA_WIDTH, 0), row(d, 0),
                  row(d, C_GATE // d), row(d, C_GATE // d + 1), row(d, C_GATE // d + 2),
                  _row_spec(ts, lambda b, si: b * ns + si),
                  full((len(POOL_WINDOWS), POOL_GROUP, POOL_GROUP)), full((1, POOL_WIDTH)),
                  full((POOL_WIDTH, d)), full((NSA_WIDTH, d)), full((d, d)), full((d, d)),
                  full((1, d)), full((1, d))],
        out_specs=_row_spec(ts, lambda b, si: b * ns + si),
        compiler_params=_cparams(("parallel", "arbitrary")),
        name="mix",
    )(proj2d, proj2d, nsa_out, ssd_y, proj2d, proj2d, proj2d, h3,
      pool_w.astype(BF16), pool_scale.reshape(1, POOL_WIDTH), w_br_pool.astype(BF16), wbn.astype(BF16),
      w_br_ssd.astype(BF16), w_out.astype(BF16), ln_g.reshape(1, d), ln_b.reshape(1, d))


RANK_CHUNK = 256
BUCKET_ROWS = 32


def _route(hb, first_step, rw_ref, rb_ref, tri_ref, bucket_ref, wlo_ref, whi_ref, rank_ref, cnt_ref, run_ref):
    @pl.when(first_step)
    def _():
        run_ref[...] = jnp.zeros(run_ref.shape, F32)

    logits = _dot_nt(rw_ref[...], hb)
    lg = [logits[e:e + 1, :] for e in range(N_EXPERTS)]
    mx = lg[0]
    for e in range(1, N_EXPERTS):
        mx = jnp.maximum(mx, lg[e])
    ex = [jnp.exp(v - mx) for v in lg]
    den = ex[0]
    for e in range(1, N_EXPERTS):
        den = den + ex[e]
    probs = [v / den for v in ex]
    score = [probs[e] + rb_ref[e:e + 1, :] for e in range(N_EXPERTS)]

    def group_vals(vals, grp):
        out = []
        for k in range(EXPERTS_PER_GROUP):
            v = vals[k]
            for gi in range(1, N_EXPERT_GROUPS):
                v = jnp.where(grp == gi, vals[gi * EXPERTS_PER_GROUP + k], v)
            out.append(v)
        return out

    gscore = []
    for gi in range(N_EXPERT_GROUPS):
        sc = score[gi * EXPERTS_PER_GROUP:(gi + 1) * EXPERTS_PER_GROUP]
        best = None
        for a, b in PAIRS:
            v = sc[a] + sc[b]
            best = v if best is None else jnp.maximum(best, v)
        gscore.append(best)
    grp = jnp.zeros(gscore[0].shape, jnp.int32)
    best = gscore[0]
    for gi in range(1, N_EXPERT_GROUPS):
        better = gscore[gi] > best
        grp = jnp.where(better, gi, grp)
        best = jnp.where(better, gscore[gi], best)

    sc = group_vals(score, grp)
    pr = group_vals(probs, grp)
    first = jnp.zeros(grp.shape, jnp.int32)
    fv = sc[0]
    for k in range(1, EXPERTS_PER_GROUP):
        better = sc[k] > fv
        first = jnp.where(better, k, first)
        fv = jnp.where(better, sc[k], fv)
    second = jnp.full(grp.shape, -1, jnp.int32)
    sv = jnp.full(fv.shape, -jnp.inf, F32)
    for k in range(EXPERTS_PER_GROUP):
        better = (first != k) & ((sc[k] > sv) | (second < 0))
        second = jnp.where(better, k, second)
        sv = jnp.where(better, sc[k], sv)
    lo = jnp.minimum(first, second)
    hi = jnp.maximum(first, second)
    p_lo = pr[0]
    p_hi = pr[0]
    for k in range(1, EXPERTS_PER_GROUP):
        p_lo = jnp.where(lo == k, pr[k], p_lo)
        p_hi = jnp.where(hi == k, pr[k], p_hi)
    tot = p_lo + p_hi
    pair = jnp.zeros(grp.shape, jnp.int32)
    for pi, (a, b) in enumerate(PAIRS):
        pair = jnp.where((lo == a) & (hi == b), pi, pair)
    bucket = grp * len(PAIRS) + pair
    bucket_ref[...] = bucket
    wlo_ref[...] = p_lo / tot
    whi_ref[...] = p_hi / tot

    tm = bucket.shape[1]
    onehot = jnp.where(lax.broadcasted_iota(jnp.int32, (BUCKET_ROWS, tm), 0) == bucket, 1.0, 0.0)
    run = run_ref[...]
    ranks = []
    for c in range(tm // RANK_CHUNK):
        oc = onehot[:, c * RANK_CHUNK:(c + 1) * RANK_CHUNK]
        before = _dot(oc.astype(BF16), tri_ref[...]) + jnp.concatenate([run] * (RANK_CHUNK // LANE), axis=1)
        ranks.append(jnp.sum(oc * before, axis=0, keepdims=True))
        run = run + jnp.broadcast_to(jnp.sum(oc, axis=1, keepdims=True), run.shape)
    rank_ref[...] = jnp.concatenate(ranks, axis=1).astype(jnp.int32)
    run_ref[...] = run
    cnt_ref[...] = run


ROUTER_TM = 1024


def _router_kernel(h_ref, *refs):
    _route(_load_rows(h_ref).astype(BF16), pl.program_id(0) == 0, *refs)


def _router(h2d, router_w, router_b):
    t, d = h2d.shape[0], D_MODEL
    tm = ROUTER_TM
    rw = jnp.pad(router_w.T, ((0, LANE - N_EXPERTS), (0, 0))).astype(BF16)
    rb = jnp.broadcast_to(jnp.pad(router_b, (0, LANE - N_EXPERTS))[:, None], (LANE, tm))
    tri = jnp.asarray(np.triu(np.ones((RANK_CHUNK, RANK_CHUNK), np.float32), 1), BF16)
    vec = lambda dt: jax.ShapeDtypeStruct((1, t), dt)
    row = pl.BlockSpec((1, tm), lambda i: (0, i))
    return pl.pallas_call(
        _router_kernel,
        out_shape=(vec(jnp.int32), vec(F32), vec(F32), vec(jnp.int32),
                   jax.ShapeDtypeStruct((BUCKET_ROWS, LANE), F32)),
        grid=(t // tm,),
        in_specs=[_row_spec(tm, lambda i: i),
                  pl.BlockSpec((LANE, d), lambda i: (0, 0)),
                  pl.BlockSpec((LANE, tm), lambda i: (0, 0)),
                  pl.BlockSpec((RANK_CHUNK, RANK_CHUNK), lambda i: (0, 0))],
        out_specs=(row, row, row, row, pl.BlockSpec((BUCKET_ROWS, LANE), lambda i: (0, 0))),
        scratch_shapes=[pltpu.VMEM((BUCKET_ROWS, LANE), F32)],
        compiler_params=_cparams(("arbitrary",)),
        name="router",
    )(h2d, rw, rb, tri)


def _moe_plan(bucket, rank, counts, w_lo, w_hi, t):
    tm = MOE_TM
    n_tiles = t // tm + N_BUCKETS
    p_rows = n_tiles * tm
    tiles_per = (counts + tm - 1) // tm
    tile_end = jnp.cumsum(tiles_per)
    tile_start = tile_end - tiles_per
    in_bucket = bucket[:, None] == jnp.arange(N_BUCKETS)[None, :]
    dest = jnp.sum(jnp.where(in_bucket, (tile_start * tm)[None, :], 0), axis=1) + rank
    payload = jnp.stack([jnp.arange(t, dtype=jnp.int32), lax.bitcast_convert_type(w_lo, jnp.int32),
                         lax.bitcast_convert_type(w_hi, jnp.int32), jnp.ones((t,), jnp.int32)], axis=1)
    plan = jnp.zeros((p_rows, 4), jnp.int32).at[dest].set(payload, unique_indices=True)
    src = plan[:, 0]
    wl = lax.bitcast_convert_type(plan[:, 1], F32)
    wh = lax.bitcast_convert_type(plan[:, 2], F32)
    prow = jnp.arange(p_rows, dtype=jnp.int32)
    spare = t + ((prow // tm) % 2) * tm + prow % tm
    dst = jnp.where(plan[:, 3] > 0, src, spare)
    n_used = tile_end[-1]
    tile_ids = jnp.arange(n_tiles)
    tb = jnp.sum((tile_end[None, :] <= tile_ids[:, None]).astype(jnp.int32), axis=1)
    tb = jnp.where(tile_ids < n_used, tb, tb[jnp.maximum(n_used - 1, 0)])
    tb = jnp.minimum(tb, N_BUCKETS - 1)
    pairs = np.asarray(PAIRS, np.int32)
    grp = tb // len(PAIRS)
    ea = grp * EXPERTS_PER_GROUP + jnp.asarray(pairs[:, 0])[tb % len(PAIRS)]
    eb = grp * EXPERTS_PER_GROUP + jnp.asarray(pairs[:, 1])[tb % len(PAIRS)]
    nvalid = jnp.clip(counts[tb] - (tile_ids - tile_start[tb]) * tm, 0, tm)
    nvalid = jnp.where(tile_ids < n_used, nvalid, 0).astype(jnp.int32)
    return (src.reshape(n_tiles, 1, tm), dst.reshape(n_tiles, 1, tm), wl.reshape(p_rows, 1), wh.reshape(p_rows, 1),
            ea.astype(jnp.int32), eb.astype(jnp.int32), nvalid, n_tiles)


MOE_DMA_GROUPS = 8


def _moe_kernel(ea_ref, eb_ref, nv_ref, src0_ref, srcn_ref, dstp_ref, wl_ref, wh_ref, h_hbm,
                w1a_ref, w3a_ref, w2a_ref, w1b_ref, w3b_ref, w2b_ref, lg_ref, lb_ref,
                o_hbm, xbuf0, xbuf1, obuf0, obuf1, wa1, wa3, wa2, wb1, wb3, wb2, gsem, ssem):
    i = pl.program_id(0)
    tm = xbuf0.shape[0]
    xbufs = (xbuf0, xbuf1)
    obufs = (obuf0, obuf1)
    nv = nv_ref[i]
    nv_p1 = nv_ref[jnp.maximum(i - 1, 0)]
    odd = lax.rem(i, 2) == 1
    even = jnp.logical_not(odd)

    def gather(r, p, row):
        return pltpu.make_async_copy(h_hbm.at[pl.ds(row, 1)], xbufs[p].at[pl.ds(r, 1)], gsem.at[p])

    def scatter(r, p, row):
        return pltpu.make_async_copy(obufs[p].at[pl.ds(r, 1)], o_hbm.at[pl.ds(row, 1)], ssem.at[p])

    def wait_gather(p):
        for r in range(tm):
            gather(r, p, 0).wait()

    def wait_scatter(p):
        for r in range(tm):
            scatter(r, p, 0).wait()

    def tile(p, age):
        q = 1 - p
        half = MOE_DMA_GROUPS // 2
        per = tm // half

        def issue(k):
            for r in range((k % half) * per, (k % half + 1) * per):
                if k < half:
                    gather(r, q, srcn_ref[0, 0, r]).start()
                elif age >= 1:
                    scatter(r, q, dstp_ref[0, 0, r]).start()

        wait_gather(p)
        x = _load_rows(xbufs[p])
        xb = x.astype(BF16)
        issue(0)
        h1 = _dot(xb, wa1[...])
        issue(1)
        h3 = _dot(xb, wa3[...])
        issue(2)
        ya = _dot((jax.nn.silu(h1) * h3).astype(BF16), wa2[...])
        issue(3)
        h1 = _dot(xb, wb1[...])
        issue(4)
        h3 = _dot(xb, wb3[...])
        issue(5)
        yb = _dot((jax.nn.silu(h1) * h3).astype(BF16), wb2[...])
        issue(6)
        y = wl_ref[...] * ya + wh_ref[...] * yb
        issue(7)
        out = _layer_norm(DN_ALPHA * x + y, lg_ref[...], lb_ref[...])
        if age >= 2:
            wait_scatter(p)
        _store_rows(obufs[p], out)

    @pl.when(i == 0)
    def _():
        obuf1[...] = jnp.zeros(obuf1.shape, F32)
        n_real = o_hbm.shape[0] - 2 * tm
        for blk in range(2):
            cp = pltpu.make_async_copy(obuf1, o_hbm.at[pl.ds(n_real + blk * tm, tm)], ssem.at[1])
            cp.start()
            cp.wait()

    prev = jnp.maximum(i - 1, 0)

    @pl.when((nv > 0) & ((i == 0) | (ea_ref[i] != ea_ref[prev])))
    def _():
        wa1[...] = w1a_ref[0, 0].astype(BF16)
        wa3[...] = w3a_ref[0, 0].astype(BF16)
        wa2[...] = w2a_ref[0, 0].astype(BF16)

    @pl.when((nv > 0) & ((i == 0) | (eb_ref[i] != eb_ref[prev])))
    def _():
        wb1[...] = w1b_ref[0, 0].astype(BF16)
        wb3[...] = w3b_ref[0, 0].astype(BF16)
        wb2[...] = w2b_ref[0, 0].astype(BF16)

    @pl.when((nv > 0) & (i == 0))
    def _():
        def start0(r, carry):
            gather(r, 0, src0_ref[0, 0, r]).start()
            return carry
        lax.fori_loop(0, tm, start0, 0)
        tile(0, 0)

    pl.when((nv > 0) & (i == 1))(functools.partial(tile, 1, 1))
    pl.when((nv > 0) & (i >= 2) & even)(functools.partial(tile, 0, 2))
    pl.when((nv > 0) & (i >= 2) & odd)(functools.partial(tile, 1, 2))

    def tail(p):
        q = 1 - p
        wait_gather(p)

        @pl.when(i >= 2)
        def _():
            wait_scatter(p)

        def start(r, carry):
            scatter(r, q, dstp_ref[0, 0, r]).start()
            return carry
        lax.fori_loop(0, tm, start, 0)
        wait_scatter(q)

    is_tail = (nv == 0) & (i >= 1) & (nv_p1 > 0)
    pl.when(is_tail & even)(functools.partial(tail, 0))
    pl.when(is_tail & odd)(functools.partial(tail, 1))


def _moe(h3, bucket, rank, counts, w_lo, w_hi, w1, w3, w2, layer, ln_g, ln_b):
    t, d = h3.shape[0], D_MODEL
    tm = MOE_TM
    assert tm % MOE_DMA_GROUPS == 0
    src, dst, wl, wh, ea, eb, nvalid, n_tiles = _moe_plan(bucket, rank, counts, w_lo, w_hi, t)
    wspec = lambda shape, which: pl.BlockSpec(
        (1,) + shape, (lambda i, ea_r, eb_r, nv_r: (layer, ea_r[i], 0, 0)) if which == 0
        else (lambda i, ea_r, eb_r, nv_r: (layer, eb_r[i], 0, 0)))
    w13 = (1, d, D_EXPERT)
    w2s = (1, D_EXPERT, d)
    smem_tile = lambda fn: pl.BlockSpec((1, 1, tm), lambda i, *_: (fn(i), 0, 0), memory_space=pltpu.SMEM)
    grid_spec = pltpu.PrefetchScalarGridSpec(
        num_scalar_prefetch=3,
        grid=(n_tiles,),
        in_specs=[smem_tile(lambda i: 0),
                  smem_tile(lambda i: jnp.minimum(i + 1, n_tiles - 1)),
                  smem_tile(lambda i: jnp.maximum(i - 1, 0)),
                  pl.BlockSpec((tm, 1), lambda i, *_: (i, 0)),
                  pl.BlockSpec((tm, 1), lambda i, *_: (i, 0)),
                  pl.BlockSpec(memory_space=pl.ANY),
                  wspec(w13, 0), wspec(w13, 0), wspec(w2s, 0),
                  wspec(w13, 1), wspec(w13, 1), wspec(w2s, 1),
                  pl.BlockSpec((1, d), lambda i, *_: (0, 0)),
                  pl.BlockSpec((1, d), lambda i, *_: (0, 0))],
        out_specs=pl.BlockSpec(memory_space=pl.ANY),
        scratch_shapes=[pltpu.VMEM((tm,) + ROW_TILE, F32)] * 4
        + [pltpu.VMEM(w13[1:], BF16), pltpu.VMEM(w13[1:], BF16), pltpu.VMEM(w2s[1:], BF16)] * 2
        + [pltpu.SemaphoreType.DMA((2,)), pltpu.SemaphoreType.DMA((2,))],
    )
    return pl.pallas_call(
        _moe_kernel,
        out_shape=jax.ShapeDtypeStruct((t + 2 * tm,) + ROW_TILE, F32),
        grid_spec=grid_spec,
        compiler_params=_cparams(("arbitrary",)),
        name="moe",
    )(ea, eb, nvalid, src, src, dst, wl, wh, h3, w1, w3, w2, w1, w3, w2,
      ln_g.reshape(1, d), ln_b.reshape(1, d))


def kernel(x, ln0_g, ln0_b, w_in, pool_w, pool_scale, cmp_k_pe, cmp_k_w1, cmp_k_w2, cmp_v_pe, cmp_v_w1, cmp_v_w2, rel_bias, conv_w, conv_b, dt_bias, a_log, d_skip, ssd_norm_g, w_br_pool, w_br_nsa, w_br_ssd, w_out, ln1_g, ln1_b, router_w, router_b, exp_w1, exp_w3, exp_w2, ln2_g, ln2_b):
    bsz, s, d = x.shape
    assert d == D_MODEL and s % MIX_TS == 0 and s // CMP_STRIDE == LANE and (bsz * s) % PROJ_TM == 0
    t = bsz * s
    ts, ws, bc = _nsa_bias_tables(rel_bias)
    et, ovt = _nsa_consts(s)
    h = x.reshape(t, d)
    for i in range(DEPTH):
        if i == 0:
            proj, h = _proj(h, _prep_w_in(w_in[i]), t, ln=(ln0_g, ln0_b))
        else:
            proj = _proj(h, _prep_w_in(w_in[i]), t)
        kc, vc = _compress(proj, bsz, s, cmp_k_pe[i], cmp_k_w1[i], cmp_k_w2[i],
                           cmp_v_pe[i], cmp_v_w1[i], cmp_v_w2[i])
        nsa_out = _nsa(proj, kc, vc, ts, ws, bc, et, ovt, bsz, s)
        ssd_y = _ssd(proj, bsz, s, conv_w[i], conv_b[i], dt_bias[i], a_log[i], d_skip[i], ssd_norm_g[i])
        h1 = _mix(proj, nsa_out, ssd_y, h, bsz, s, pool_w[i], pool_scale[i], w_br_pool[i], w_br_nsa[i],
                  w_br_ssd[i], w_out[i], ln1_g[i], ln1_b[i])
        bucket, w_lo, w_hi, rank, cnt = _router(h1, router_w, router_b)
        counts = cnt[:N_BUCKETS, 0].astype(jnp.int32)
        h = _moe(h1, bucket[0], rank[0], counts, w_lo[0], w_hi[0], exp_w1, exp_w3, exp_w2, i,
                 ln2_g[i], ln2_b[i])
    return h[:t].reshape(bsz, s, d)
```

```python
import functools
import math

import numpy as np
import jax
import jax.numpy as jnp
from jax import lax
from jax.experimental import pallas as pl
from jax.experimental.pallas import tpu as pltpu

F32 = jnp.float32
BF16 = jnp.bfloat16

D_MODEL = 1024
DEPTH = 2
DN_ALPHA = (2.0 * DEPTH) ** 0.25
LN_EPS = 1e-5
NEG = -1e30
BIG = 1e6

POOL_WINDOWS = (2, 4, 8, 16)
POOL_GROUP = 128
POOL_WIDTH = 512
MAX_POOL_WINDOW = 16

NSA_HEAD_DIM = 64
NSA_KV_HEADS = 2
NSA_HPG = 4
NSA_HEADS = 8
NSA_WIDTH = 512
CMP_LEN = 32
CMP_STRIDE = 16
CMP_HIDDEN = 128
SLC_LEN = 64
SLC_TOPN = 8
WIN_LEN = 512
Q_BLOCK = 128
REL_BUCKETS = 32
REL_MAX_DIST = 128

SSD_D_INNER = 1024
SSD_HEAD_DIM = 64
SSD_HEADS = 16
SSD_GROUPS = 4
SSD_STATE = 128
SSD_CONV = 4
SSD_CHUNK = 128
SSD_XBC = 2048

N_EXPERTS = 16
N_EXPERT_GROUPS = 4
EXPERTS_PER_GROUP = 4
D_EXPERT = 512
PAIRS = ((0, 1), (0, 2), (0, 3), (1, 2), (1, 3), (2, 3))
N_BUCKETS = N_EXPERT_GROUPS * len(PAIRS)

LANE = 128
SUBLANE = 8
VMEM_LIMIT = 48 * 1024 * 1024

C_POOL = 0
C_Q = 512
C_KV = 1024
C_NG = 1792
C_DT = 1920
C_XBC = 2048
C_Z = 4096
C_GATE = 5120
D_PAD = 8192

PROJ_TM = 1024
PROJ_TN = 2048
MIX_TS = 512
MOE_TM = 256


def _dot(a, b):
    return jnp.dot(a, b, preferred_element_type=F32)


def _dot_nt(a, b):
    return lax.dot_general(a, b, (((1,), (1,)), ((), ())), preferred_element_type=F32)


def _dot_tn(a, b):
    return lax.dot_general(a, b, (((0,), (0,)), ((), ())), preferred_element_type=F32)


def _split3(x):
    x1 = x.astype(BF16)
    r1 = x - x1.astype(F32)
    x2 = r1.astype(BF16)
    x3 = (r1 - x2.astype(F32)).astype(BF16)
    return x1, x2, x3


def _layer_norm(x, g, b):
    mu = jnp.mean(x, -1, keepdims=True)
    xc = x - mu
    var = jnp.mean(xc * xc, -1, keepdims=True)
    return xc * lax.rsqrt(var + LN_EPS) * g + b


def _cparams(sem):
    return pltpu.CompilerParams(dimension_semantics=sem, vmem_limit_bytes=VMEM_LIMIT)


ROW_TILE = (D_MODEL,)


def _load_rows(ref):
    return ref[...]


def _store_rows(ref, val):
    ref[...] = val


def _row_spec(n, index):
    return pl.BlockSpec((n,) + ROW_TILE, lambda *ids: (index(*ids), 0))


def _proj_kernel(h_ref, w_ref, o_ref, hb_ref):
    @pl.when(pl.program_id(1) == 0)
    def _():
        hb_ref[...] = _load_rows(h_ref).astype(BF16)

    o_ref[...] = _dot(hb_ref[...], w_ref[...])


def _ln_proj_kernel(x_ref, g_ref, b_ref, w_ref, o_ref, h_ref, hb_ref):
    @pl.when(pl.program_id(1) == 0)
    def _():
        h = _layer_norm(x_ref[...], g_ref[...], b_ref[...])
        _store_rows(h_ref, h)
        hb_ref[...] = h.astype(BF16)

    o_ref[...] = _dot(hb_ref[...], w_ref[...])


def _proj(h2d, w_pad, t, ln=None):
    d = D_MODEL
    grid = (t // PROJ_TM, D_PAD // PROJ_TN)
    w_spec = pl.BlockSpec((d, PROJ_TN), lambda i, j: (0, j))
    o_spec = pl.BlockSpec((PROJ_TM, PROJ_TN), lambda i, j: (i, j))
    o_shape = jax.ShapeDtypeStruct((t, D_PAD), F32)
    scratch = [pltpu.VMEM((PROJ_TM, d), BF16)]
    params = _cparams(("parallel", "arbitrary"))
    if ln is None:
        return pl.pallas_call(
            _proj_kernel, out_shape=o_shape, grid=grid,
            in_specs=[_row_spec(PROJ_TM, lambda i, j: i), w_spec], out_specs=o_spec,
            scratch_shapes=scratch, compiler_params=params, name="proj",
        )(h2d, w_pad)
    vec = pl.BlockSpec((1, d), lambda i, j: (0, 0))
    return pl.pallas_call(
        _ln_proj_kernel, out_shape=(o_shape, jax.ShapeDtypeStruct((t,) + ROW_TILE, F32)), grid=grid,
        in_specs=[_row_spec(PROJ_TM, lambda i, j: i), vec, vec, w_spec],
        out_specs=(o_spec, _row_spec(PROJ_TM, lambda i, j: i)),
        scratch_shapes=scratch, compiler_params=params, name="ln_proj",
    )(h2d, ln[0].reshape(1, d), ln[1].reshape(1, d), w_pad)


def _pad_cols(w, n):
    return jnp.pad(w, ((0, 0), (0, n - w.shape[1])))


def _prep_w_in(w_in):
    d = w_in.shape[0]
    o = 0
    w_pool = w_in[:, o:o + POOL_WIDTH]; o += POOL_WIDTH
    w_q = w_in[:, o:o + NSA_WIDTH]; o += NSA_WIDTH
    w_kv = w_in[:, o:o + 768]; o += 768
    w_ng = w_in[:, o:o + 24]; o += 24
    w_z = w_in[:, o:o + SSD_D_INNER]; o += SSD_D_INNER
    w_xbc = w_in[:, o:o + SSD_XBC]; o += SSD_XBC
    w_dt = w_in[:, o:o + SSD_HEADS]; o += SSD_HEADS
    w_gate = w_in[:, o:o + 3 * D_MODEL]
    w_q = w_q.reshape(d, NSA_KV_HEADS, NSA_HPG, NSA_HEAD_DIM).transpose(0, 2, 1, 3).reshape(d, NSA_WIDTH)
    w = jnp.concatenate([w_pool, w_q, w_kv, _pad_cols(w_ng, LANE), _pad_cols(w_dt, LANE),
                         w_xbc, w_z, w_gate], axis=1)
    return w.astype(BF16)


def _compress_one(tok_ref, pe_ref, w1_ref, w2_ref):
    n16 = tok_ref.shape[0] // CMP_STRIDE
    acc_a = jnp.zeros((n16, 2 * CMP_HIDDEN), F32)
    acc_b = jnp.zeros((n16, 2 * CMP_HIDDEN), F32)
    for l in range(CMP_STRIDE):
        rows = tok_ref[pl.ds(l, n16, stride=CMP_STRIDE), :]
        xa = (rows + pe_ref[l:l + 1, :]).astype(BF16)
        xb = (rows + pe_ref[CMP_STRIDE + l:CMP_STRIDE + l + 1, :]).astype(BF16)
        acc_a = acc_a + _dot(xa, w1_ref[l])
        acc_b = acc_b + _dot(xb, w1_ref[CMP_STRIDE + l])
    hid = acc_a + pltpu.roll(acc_b, n16 - 1, axis=0)
    hid = jax.nn.gelu(hid)
    return _dot(hid.astype(BF16), w2_ref[...])


def _compress_kernel(kt_ref, vt_ref, pek_ref, w1k_ref, w2k_ref, pev_ref, w1v_ref, w2v_ref,
                     kc_ref, vc_ref):
    kc_ref[0] = _compress_one(kt_ref, pek_ref, w1k_ref, w2k_ref).astype(BF16)
    vc_ref[0] = _compress_one(vt_ref, pev_ref, w1v_ref, w2v_ref).T.astype(BF16)


def _blockdiag2(w):
    z = jnp.zeros_like(w)
    return jnp.concatenate([jnp.concatenate([w, z], -1), jnp.concatenate([z, w], -1)], -2)


def _compress(proj2d, bsz, s, pe_k, w1_k, w2_k, pe_v, w1_v, w2_v):
    n16 = s // CMP_STRIDE

    def prep(pe, w1, w2):
        pe2 = jnp.concatenate([pe, pe], axis=-1)
        w1b = _blockdiag2(w1.reshape(CMP_LEN, NSA_HEAD_DIM, CMP_HIDDEN)).astype(BF16)
        w2b = _blockdiag2(w2).astype(BF16)
        return pe2, w1b, w2b

    pk = prep(pe_k, w1_k, w2_k)
    pv = prep(pe_v, w1_v, w2_v)
    full = lambda shape: pl.BlockSpec(shape, lambda b: (0,) * len(shape))
    wspecs = [full((CMP_LEN, LANE)), full((CMP_LEN, LANE, 2 * CMP_HIDDEN)), full((2 * CMP_HIDDEN, LANE))]
    return pl.pallas_call(
        _compress_kernel,
        out_shape=(jax.ShapeDtypeStruct((bsz, n16, LANE), BF16),
                   jax.ShapeDtypeStruct((bsz, n16, LANE), BF16)),
        grid=(bsz,),
        in_specs=[pl.BlockSpec((s, LANE), lambda b: (b, C_KV // LANE)),
                  pl.BlockSpec((s, LANE), lambda b: (b, C_KV // LANE + 1))] + wspecs + wspecs,
        out_specs=(pl.BlockSpec((1, n16, LANE), lambda b: (b, 0, 0)),
                   pl.BlockSpec((1, n16, LANE), lambda b: (b, 0, 0))),
        compiler_params=_cparams(("parallel",)),
        name="nsa_compress",
    )(proj2d, proj2d, *pk, *pv)


def _rel_bucket(dist):
    n = jnp.maximum(dist, 0)
    max_exact = REL_BUCKETS // 2
    nf = jnp.maximum(n, 1).astype(F32)
    large = max_exact + (jnp.log(nf / max_exact) / math.log(REL_MAX_DIST / max_exact)
                         * (REL_BUCKETS - max_exact)).astype(jnp.int32)
    large = jnp.minimum(large, REL_BUCKETS - 1)
    return jnp.where(n < max_exact, n, large)


LOG2E = math.log2(math.e)
WIN_BACK = WIN_LEN // Q_BLOCK
WIN_KEYS = (WIN_BACK + 1) * Q_BLOCK
WIN_STRIP_TILES = 2 * WIN_BACK + 1
SEL_TILES = 4
SEL_KEYS = SEL_TILES * Q_BLOCK
SEL_STRIP_TILES = 3 * SEL_TILES - 1
CMP_BIAS_ROWS = 2 * LANE
PV_ROWS = NSA_HEAD_DIM + 16


def _nsa_bias_tables(rel_bias):
    tbl = rel_bias.astype(F32)
    far = tbl[REL_BUCKETS - 1]
    hq = NSA_HPG * Q_BLOCK

    def lookup(dist):
        onehot = (_rel_bucket(jnp.asarray(dist))[..., None] == jnp.arange(REL_BUCKETS)).astype(F32)
        return jnp.einsum("...b,bh->...h", onehot, tbl, precision=lax.Precision.HIGHEST)

    k = np.arange(Q_BLOCK)[None, :, None]
    q = np.arange(Q_BLOCK)[None, None, :]

    def strip(first, last, visible):
        n = first - last + 1
        dist = np.arange(first, last - 1, -1)[:, None, None] * Q_BLOCK + q - k
        v = jnp.where(jnp.asarray(visible(dist))[..., None], (lookup(dist) - far) * LOG2E, NEG)
        v = v.reshape(n, Q_BLOCK, Q_BLOCK, NSA_KV_HEADS, NSA_HPG).transpose(3, 0, 1, 4, 2)
        return v.reshape(NSA_KV_HEADS, n * Q_BLOCK, hq)

    ts = strip(2 * SEL_TILES - 1, 1 - SEL_TILES, lambda d: d >= 0)
    ws = strip(WIN_BACK, -WIN_BACK, lambda d: (d >= 0) & (d < WIN_LEN))
    r = np.arange(CMP_BIAS_ROWS)[:, None]
    dist_c = q[0] - CMP_STRIDE * (r - LANE) - (CMP_LEN - 1)
    bc = jnp.where(jnp.asarray(dist_c >= 0)[..., None], lookup(dist_c) * LOG2E, NEG)
    bc = bc.reshape(CMP_BIAS_ROWS, Q_BLOCK, NSA_KV_HEADS, NSA_HPG).transpose(2, 0, 3, 1)
    bc = bc.reshape(NSA_KV_HEADS, CMP_BIAS_ROWS, hq)
    return ts, ws, bc


def _nsa_consts(s):
    n_slc = s // SLC_LEN
    et = (np.arange(s)[:, None] // SLC_LEN == np.arange(LANE)[None, :]).astype(np.float32)
    n_cmp = (s - CMP_LEN) // CMP_STRIDE + 1
    c0 = np.arange(n_cmp) * CMP_STRIDE
    s0 = np.arange(n_slc) * SLC_LEN
    ov = np.clip(np.minimum(c0[None, :] + CMP_LEN, s0[:, None] + SLC_LEN)
                 - np.maximum(c0[None, :], s0[:, None]), 0, None) / CMP_LEN
    ovt = np.zeros((n_slc, LANE), np.float32)
    ovt[:, :n_cmp] = ov
    return jnp.asarray(et, BF16), jnp.asarray(ovt, BF16)


def _nsa_kernel(q_ref, ng_ref, ks_ref, vs_ref, kw_ref, vw_ref, kc_ref, vct_ref,
                ts_ref, ws_ref, bc_ref, et_ref, ovt_ref, o_ref,
                ksb_ref, kwb_ref, vst_ref, vwt_ref, *, nq, n_slc):
    qi = pl.program_id(1)
    hq = NSA_HPG * Q_BLOCK
    dh = NSA_HEAD_DIM
    groups = range(NSA_KV_HEADS)
    gsl = [slice(g * dh, (g + 1) * dh) for g in groups]

    @pl.when(qi == 0)
    def _():
        ksb_ref[:, 0:LANE] = ks_ref[...].astype(BF16)
        ksb_ref[:, LANE:] = et_ref[...]
        kwb_ref[...] = kw_ref[...].astype(BF16)
        ones_rows = jnp.where(lax.broadcasted_iota(jnp.int32, (PV_ROWS - dh, ks_ref.shape[0]), 0) == 0, 1.0, 0.0)
        for g in groups:
            vst_ref[g, dh:, :] = ones_rows.astype(BF16)
            vwt_ref[g, dh:, :] = ones_rows.astype(BF16)
        for kt in range(nq):
            sl = slice(kt * Q_BLOCK, (kt + 1) * Q_BLOCK)
            vs_t = vs_ref[sl, :].T
            vw_t = vw_ref[sl, :].T
            for g in groups:
                vst_ref[g, 0:dh, sl] = vs_t[gsl[g], :].astype(BF16)
                vwt_ref[g, 0:dh, sl] = vw_t[gsl[g], :].astype(BF16)

    n_super = qi // SEL_TILES + 1
    for n in range(1, nq // SEL_TILES + 1):
        pl.when(n_super == n)(functools.partial(
            _nsa_step, q_ref, ng_ref, kc_ref, vct_ref, ts_ref, ws_ref, bc_ref, ovt_ref, o_ref,
            ksb_ref, kwb_ref, vst_ref, vwt_ref, n_slc=n_slc, n_sel=n))


def _nsa_step(q_ref, ng_ref, kc_ref, vct_ref, ts_ref, ws_ref, bc_ref, ovt_ref, o_ref,
              ksb_ref, kwb_ref, vst_ref, vwt_ref, *, n_slc, n_sel):
    qi = pl.program_id(1)
    hq = NSA_HPG * Q_BLOCK
    dh = NSA_HEAD_DIM
    groups = range(NSA_KV_HEADS)
    gsl = [slice(g * dh, (g + 1) * dh) for g in groups]
    row_lo = lax.broadcasted_iota(jnp.int32, (LANE, hq), 0) < dh

    def heads4(m):
        return jnp.concatenate([m] * NSA_HPG, axis=1)

    gates_t = jax.nn.sigmoid(ng_ref[...]).T
    q_t = jnp.concatenate(
        [(q_ref[:, h * LANE:(h + 1) * LANE] * (dh ** -0.5 * LOG2E)).T for h in range(NSA_HPG)], axis=1)
    qg_t = [jnp.where(row_lo if g == 0 else jnp.logical_not(row_lo), q_t, 0.0).astype(BF16) for g in groups]

    boff = pl.multiple_of(LANE - SUBLANE * qi, SUBLANE)
    jrow = lax.broadcasted_iota(jnp.int32, (n_slc, LANE), 0)
    tj = qi * Q_BLOCK + lax.broadcasted_iota(jnp.int32, (n_slc, LANE), 1)
    jt = lax.shift_right_logical(tj, SLC_LEN.bit_length() - 1)
    forced = (jrow == 0) | (jrow == jt) | (jrow == jt - 1)
    valid = jrow * SLC_LEN <= tj
    o_c = []
    q_aug = []
    for g in groups:
        s = _dot(kc_ref[0], qg_t[g]) + bc_ref[g, pl.ds(boff, LANE), :]
        mx = jnp.max(s, axis=0, keepdims=True)
        p = jnp.exp2(s - mx)
        lsum = jnp.sum(p, axis=0, keepdims=True)
        pb = (p * jnp.where(mx > 0.5 * NEG, 1.0 / lsum, 0.0)).astype(BF16)
        o_c.append(_dot(vct_ref[0, gsl[g], :], pb))
        imp4 = _dot(ovt_ref[...], pb)
        imp = imp4[:, 0:LANE]
        for h in range(1, NSA_HPG):
            imp = imp + imp4[:, h * LANE:(h + 1) * LANE]
        val = jnp.where(forced, BIG, jnp.where(valid, imp, -BIG))
        rank = jnp.zeros((n_slc, LANE), F32)
        for i in range(n_slc):
            ci = val[i:i + 1, :]
            rank = rank + jnp.where(ci > val, 1.0, 0.0) + jnp.where(ci == val, (jrow > i).astype(F32), 0.0)
        block_rows = heads4(jnp.where(rank < SLC_TOPN, 0.0, NEG)).astype(BF16)
        q_aug.append(jnp.concatenate(
            [qg_t[g], block_rows, jnp.zeros((LANE - n_slc, hq), BF16)], axis=0))

    diag = qi - SEL_TILES * (n_sel - 1)
    nk = n_sel * SEL_KEYS
    lo = max(n_sel - 2, 0) * SEL_KEYS
    toff = pl.multiple_of((2 * SEL_TILES - 1 - diag) * Q_BLOCK - (nk - lo - SEL_KEYS), Q_BLOCK)
    o_s = []
    for g in groups:
        parts = [(lo, nk, _dot(ksb_ref[lo:nk, :], q_aug[g]) + ts_ref[g, pl.ds(toff, nk - lo), :])]
        if lo > 0:
            parts.append((0, lo, _dot(ksb_ref[0:lo, :], q_aug[g])))
        m = parts[0][2].max(axis=0, keepdims=True)
        for _, _, s in parts[1:]:
            m = jnp.maximum(m, s.max(axis=0, keepdims=True))
        acc = None
        for a, b, s in parts:
            pv = _dot(vst_ref[g, :, a:b], jnp.exp2(s - m).astype(BF16))
            acc = pv if acc is None else acc + pv
        o_s.append(acc[0:dh] * (1.0 / acc[dh:dh + 1]))

    w0 = jnp.maximum(qi - WIN_BACK, 0)
    woff = pl.multiple_of(w0 * Q_BLOCK, Q_BLOCK)
    boff_w = pl.multiple_of((WIN_BACK - (qi - w0)) * Q_BLOCK, Q_BLOCK)
    kw_bf = kwb_ref[pl.ds(woff, WIN_KEYS), :]
    o_w = []
    for g in groups:
        s = _dot(kw_bf, qg_t[g]) + ws_ref[g, pl.ds(boff_w, WIN_KEYS), :]
        p = jnp.exp2(s - jnp.max(s, axis=0, keepdims=True)).astype(BF16)
        acc = _dot(vwt_ref[g, :, pl.ds(woff, WIN_KEYS)], p)
        o_w.append(acc[0:dh] * (1.0 / acc[dh:dh + 1]))

    for h in range(NSA_HPG):
        hl = slice(h * LANE, (h + 1) * LANE)
        halves = []
        for g in groups:
            c = g * NSA_HPG + h
            halves.append(gates_t[c:c + 1, :] * o_c[g][:, hl]
                          + gates_t[NSA_HEADS + c:NSA_HEADS + c + 1, :] * o_s[g][:, hl]
                          + gates_t[2 * NSA_HEADS + c:2 * NSA_HEADS + c + 1, :] * o_w[g][:, hl])
        o_ref[:, hl] = jnp.concatenate(halves, axis=0).T.astype(o_ref.dtype)


def _nsa(proj2d, kc, vct, ts, ws, bc, et, ovt, bsz, s):
    nq = s // Q_BLOCK
    n_slc = s // SLC_LEN
    hq = NSA_HPG * Q_BLOCK
    kvb = C_KV // LANE
    kv_spec = lambda blk: pl.BlockSpec((s, LANE), lambda b, qi: (b, kvb + blk))
    full = lambda shape: pl.BlockSpec(shape, lambda b, qi: (0,) * len(shape), pipeline_mode=pl.Buffered(1))
    return pl.pallas_call(
        functools.partial(_nsa_kernel, nq=nq, n_slc=n_slc),
        out_shape=jax.ShapeDtypeStruct((bsz * s, NSA_WIDTH), BF16),
        grid=(bsz, nq),
        in_specs=[pl.BlockSpec((Q_BLOCK, NSA_WIDTH), lambda b, qi: (b * nq + qi, C_Q // NSA_WIDTH)),
                  pl.BlockSpec((Q_BLOCK, LANE), lambda b, qi: (b * nq + qi, C_NG // LANE)),
                  kv_spec(2), kv_spec(3), kv_spec(4), kv_spec(5),
                  pl.BlockSpec((1, s // CMP_STRIDE, LANE), lambda b, qi: (b, 0, 0)),
                  pl.BlockSpec((1, LANE, s // CMP_STRIDE), lambda b, qi: (b, 0, 0)),
                  full((NSA_KV_HEADS, SEL_STRIP_TILES * Q_BLOCK, hq)),
                  full((NSA_KV_HEADS, WIN_STRIP_TILES * Q_BLOCK, hq)),
                  full((NSA_KV_HEADS, CMP_BIAS_ROWS, hq)),
                  full((s, LANE)),
                  full((n_slc, LANE))],
        out_specs=pl.BlockSpec((Q_BLOCK, NSA_WIDTH), lambda b, qi: (b * nq + qi, 0)),
        scratch_shapes=[pltpu.VMEM((s, 2 * LANE), BF16), pltpu.VMEM((s, LANE), BF16),
                        pltpu.VMEM((NSA_KV_HEADS, PV_ROWS, s), BF16), pltpu.VMEM((NSA_KV_HEADS, PV_ROWS, s), BF16)],
        compiler_params=_cparams(("parallel", "arbitrary")),
        name="nsa_attn",
    )(proj2d, proj2d, proj2d, proj2d, proj2d, proj2d, kc, vct, ts, ws, bc, et, ovt)


SSD_BATCH = 2


def _ssd_kernel(xbc_ref, prev_ref, z_ref, dt_ref, *rest):
    shared, o_ref, state_ref = rest[:-2], rest[-2], rest[-1]
    c = pl.program_id(1)
    for u in range(SSD_BATCH):
        _ssd_chunk(c, xbc_ref.at[u], prev_ref.at[u], z_ref.at[u], dt_ref.at[u], *shared,
                   o_ref.at[u], state_ref.at[u])


def _ssd_chunk(c, xbc_ref, prev_ref, z_ref, dt_ref, cw_ref, cb_ref, dtb_ref, alog_ref,
               dskip_ref, ng_ref, e16_ref, tril_ref, o_ref, state_ref):
    L = SSD_CHUNK
    pair_w = 2 * SSD_HEAD_DIM

    @pl.when(c == 0)
    def _():
        state_ref[...] = jnp.zeros(state_ref.shape, F32)

    xcat = jnp.concatenate([jnp.where(c > 0, prev_ref[...], 0.0), xbc_ref[...]], axis=0)
    acc = cw_ref[0:1, :] * xcat
    for k in range(1, SSD_CONV):
        acc = pltpu.roll(acc, 1, axis=0) + cw_ref[k:k + 1, :] * xcat
    xbc = jax.nn.silu(acc[SUBLANE:, :] + cb_ref[...])
    xs = xbc[:, :SSD_D_INNER]
    bmat = xbc[:, SSD_D_INNER:SSD_D_INNER + SSD_GROUPS * SSD_STATE]
    cmat = xbc[:, SSD_D_INNER + SSD_GROUPS * SSD_STATE:]

    dt = jax.nn.softplus(dt_ref[...] + dtb_ref[...])
    tril = tril_ref[...]
    e16 = e16_ref[...]
    d1, d2, d3 = _split3(dt)
    dt_x = _dot(d1, e16) + _dot(d2, e16) + _dot(d3, e16)
    a_c = dt * (-jnp.exp(alog_ref[...]) * LOG2E)
    c1, c2, c3 = _split3(a_c)
    acs = _dot(tril, c1) + _dot(tril, c2) + _dot(tril, c3)
    x1, x2, x3 = _split3(acs)
    acs_x = _dot(x1, e16) + _dot(x2, e16) + _dot(x3, e16)
    acs_t = acs.T
    last = acs_x[L - 1:L, :]
    eacs = jnp.exp2(acs_x)
    decay_in = jnp.exp2(last - acs_x)
    cdec = jnp.exp2(last)
    xdt = xs * dt_x
    wst = xdt * decay_in

    row = lax.broadcasted_iota(jnp.int32, (L, L), 0)
    lane = lax.broadcasted_iota(jnp.int32, (L, L), 1)
    tri = row >= lane
    lo = lane < SSD_HEAD_DIM

    ys = []
    for g in range(SSD_GROUPS):
        cg = cmat[:, g * SSD_STATE:(g + 1) * SSD_STATE].astype(BF16)
        bg = bmat[:, g * SSD_STATE:(g + 1) * SSD_STATE].astype(BF16)
        cb = _dot_nt(cg, bg)
        for pr in range(2):
            pp = g * 2 + pr
            ls = slice(pp * pair_w, (pp + 1) * pair_w)
            xp = xdt[:, ls]
            yd = None
            for e in range(2):
                hd = 2 * pp + e
                diff = acs[:, hd:hd + 1] - acs_t[hd:hd + 1, :]
                seg = jnp.exp2(jnp.where(tri, diff, NEG))
                pm = (cb * seg).astype(BF16)
                xe = jnp.where(lo if e == 0 else jnp.logical_not(lo), xp, 0.0).astype(BF16)
                term = _dot(pm, xe)
                yd = term if yd is None else yd + term
            st = state_ref[pp]
            yo = _dot(cg, st.astype(BF16)) * eacs[:, ls]
            ys.append(yd + yo)
            state_ref[pp] = cdec[:, ls] * st + _dot_tn(bg, wst[:, ls].astype(BF16))

    y = jnp.concatenate(ys, axis=1) + xs * dskip_ref[...]
    y = y * jax.nn.silu(z_ref[...])
    gw = SSD_D_INNER // SSD_GROUPS
    outs = []
    for gi in range(SSD_GROUPS):
        yg = y[:, gi * gw:(gi + 1) * gw]
        outs.append(yg * lax.rsqrt(jnp.mean(yg * yg, -1, keepdims=True) + LN_EPS))
    o_ref[...] = (jnp.concatenate(outs, axis=1) * ng_ref[...]).astype(o_ref.dtype)


def _ssd(proj2d, bsz, s, conv_w, conv_b, dt_bias, a_log, d_skip, norm_g):
    nc = s // SSD_CHUNK
    L = SSD_CHUNK
    pad16 = lambda v: jnp.pad(v.reshape(1, SSD_HEADS), ((0, 0), (0, LANE - SSD_HEADS)))
    rep64 = lambda v: jnp.repeat(v, SSD_HEAD_DIM).reshape(1, SSD_D_INNER)
    e16 = np.zeros((LANE, SSD_D_INNER), np.float32)
    e16[np.arange(SSD_D_INNER) // SSD_HEAD_DIM, np.arange(SSD_D_INNER)] = 1.0
    tril = np.tril(np.ones((L, L), np.float32))
    full = lambda shape: pl.BlockSpec(shape, lambda b, c: (0,) * len(shape))
    nb = SSD_BATCH
    assert bsz % nb == 0
    proj3d = proj2d.reshape(bsz, s, D_PAD)
    out = pl.pallas_call(
        _ssd_kernel,
        out_shape=jax.ShapeDtypeStruct((bsz, s, SSD_D_INNER), BF16),
        grid=(bsz // nb, nc),
        in_specs=[pl.BlockSpec((nb, L, SSD_XBC), lambda b, c: (b, c, C_XBC // SSD_XBC)),
                  pl.BlockSpec((nb, SUBLANE, SSD_XBC),
                               lambda b, c: (b, jnp.maximum(c * (L // SUBLANE) - 1, 0), C_XBC // SSD_XBC)),
                  pl.BlockSpec((nb, L, SSD_D_INNER), lambda b, c: (b, c, C_Z // SSD_D_INNER)),
                  pl.BlockSpec((nb, L, LANE), lambda b, c: (b, c, C_DT // LANE)),
                  full((SSD_CONV, SSD_XBC)), full((1, SSD_XBC)), full((1, LANE)), full((1, LANE)),
                  full((1, SSD_D_INNER)), full((1, SSD_D_INNER)),
                  full((LANE, SSD_D_INNER)), full((L, L))],
        out_specs=pl.BlockSpec((nb, L, SSD_D_INNER), lambda b, c: (b, c, 0)),
        scratch_shapes=[pltpu.VMEM((nb, SSD_HEADS // 2, SSD_STATE, 2 * SSD_HEAD_DIM), F32)],
        compiler_params=_cparams(("parallel", "arbitrary")),
        name="ssd",
    )(proj3d, proj3d, proj3d, proj3d, conv_w, conv_b.reshape(1, SSD_XBC), pad16(dt_bias), pad16(a_log),
      rep64(d_skip), norm_g.reshape(1, SSD_D_INNER),
      jnp.asarray(e16, BF16), jnp.asarray(tril, BF16))
    return out.reshape(bsz * s, SSD_D_INNER)


def _mix_kernel(up_ref, upprev_ref, nsa_ref, ssd_ref, ga_ref, gb_ref, gc_ref, h_ref,
                pw_ref, ps_ref, wbp_ref, wbn_ref, wbs_ref, wo_ref, lg_ref, lb_ref, o_ref, *, ts):
    si = pl.program_id(1)
    w0 = MAX_POOL_WINDOW
    prev = jnp.where(si > 0, upprev_ref[...], 0.0)
    xcat = jnp.concatenate([prev, up_ref[...]], axis=0)
    tpos = si * ts + lax.broadcasted_iota(jnp.int32, (ts, POOL_GROUP), 0)
    ygs = []
    for gi, w in enumerate(POOL_WINDOWS):
        x = xcat[:, gi * POOL_GROUP:(gi + 1) * POOL_GROUP]
        acc = x
        d = 1
        while d < w:
            acc = acc + pltpu.roll(acc, d, axis=0)
            d *= 2
        cnt = jnp.minimum(tpos + 1, w).astype(F32)
        r = acc[w0:, :] / cnt - x[w0:, :]
        ygs.append(_dot(r.astype(BF16), pw_ref[gi]))
    y = jnp.concatenate(ygs, axis=1) * ps_ref[...]
    br_a = _dot(y.astype(BF16), wbp_ref[...])
    br_b = _dot(nsa_ref[...], wbn_ref[...])
    br_c = _dot(ssd_ref[...], wbs_ref[...])
    mixed = (jax.nn.sigmoid(ga_ref[...]) * br_a + jax.nn.sigmoid(gb_ref[...]) * br_b
             + jax.nn.sigmoid(gc_ref[...]) * br_c)
    mix = _dot(mixed.astype(BF16), wo_ref[...])
    _store_rows(o_ref, _layer_norm(DN_ALPHA * _load_rows(h_ref) + mix, lg_ref[...], lb_ref[...]))


def _mix(proj2d, nsa_out, ssd_y, h3, bsz, s, pool_w, pool_scale, w_br_pool, w_br_nsa, w_br_ssd,
         w_out, ln_g, ln_b):
    ts = MIX_TS
    ns = s // ts
    t = bsz * s
    d = D_MODEL
    w0 = MAX_POOL_WINDOW
    wbn = w_br_nsa.reshape(NSA_KV_HEADS, NSA_HPG, NSA_HEAD_DIM, d).transpose(1, 0, 2, 3).reshape(NSA_WIDTH, d)
    full = lambda shape: pl.BlockSpec(shape, lambda b, si: (0,) * len(shape), pipeline_mode=pl.Buffered(1))
    row = lambda width, cb: pl.BlockSpec((ts, width), lambda b, si: (b * ns + si, cb))
    return pl.pallas_call(
        functools.partial(_mix_kernel, ts=ts),
        out_shape=jax.ShapeDtypeStruct((t,) + ROW_TILE, F32),
        grid=(bsz, ns),
        in_specs=[row(POOL_WIDTH, C_POOL // POOL_WIDTH),
                  pl.BlockSpec((w0, POOL_WIDTH),
                               lambda b, si: (jnp.maximum((b * s + si * ts) // w0 - 1, 0), C_POOL // POOL_WIDTH)),
                  row(NSA_WIDTH, 0), row(d, 0),
                  row(d, C_GATE // d), row(d, C_GATE // d + 1), row(d, C_GATE // d + 2),
                  _row_spec(ts, lambda b, si: b * ns + si),
                  full((len(POOL_WINDOWS), POOL_GROUP, POOL_GROUP)), full((1, POOL_WIDTH)),
                  full((POOL_WIDTH, d)), full((NSA_WIDTH, d)), full((d, d)), full((d, d)),
                  full((1, d)), full((1, d))],
        out_specs=_row_spec(ts, lambda b, si: b * ns + si),
        compiler_params=_cparams(("parallel", "arbitrary")),
        name="mix",
    )(proj2d, proj2d, nsa_out, ssd_y, proj2d, proj2d, proj2d, h3,
      pool_w.astype(BF16), pool_scale.reshape(1, POOL_WIDTH), w_br_pool.astype(BF16), wbn.astype(BF16),
      w_br_ssd.astype(BF16), w_out.astype(BF16), ln_g.reshape(1, d), ln_b.reshape(1, d))


RANK_CHUNK = 256
BUCKET_ROWS = 32


def _route(hb, first_step, rw_ref, rb_ref, tri_ref, bucket_ref, wlo_ref, whi_ref, rank_ref, cnt_ref, run_ref):
    @pl.when(first_step)
    def _():
        run_ref[...] = jnp.zeros(run_ref.shape, F32)

    logits = _dot_nt(rw_ref[...], hb)
    lg = [logits[e:e + 1, :] for e in range(N_EXPERTS)]
    mx = lg[0]
    for e in range(1, N_EXPERTS):
        mx = jnp.maximum(mx, lg[e])
    ex = [jnp.exp(v - mx) for v in lg]
    den = ex[0]
    for e in range(1, N_EXPERTS):
        den = den + ex[e]
    probs = [v / den for v in ex]
    score = [probs[e] + rb_ref[e:e + 1, :] for e in range(N_EXPERTS)]

    def group_vals(vals, grp):
        out = []
        for k in range(EXPERTS_PER_GROUP):
            v = vals[k]
            for gi in range(1, N_EXPERT_GROUPS):
                v = jnp.where(grp == gi, vals[gi * EXPERTS_PER_GROUP + k], v)
            out.append(v)
        return out

    gscore = []
    for gi in range(N_EXPERT_GROUPS):
        sc = score[gi * EXPERTS_PER_GROUP:(gi + 1) * EXPERTS_PER_GROUP]
        best = None
        for a, b in PAIRS:
            v = sc[a] + sc[b]
            best = v if best is None else jnp.maximum(best, v)
        gscore.append(best)
    grp = jnp.zeros(gscore[0].shape, jnp.int32)
    best = gscore[0]
    for gi in range(1, N_EXPERT_GROUPS):
        better = gscore[gi] > best
        grp = jnp.where(better, gi, grp)
        best = jnp.where(better, gscore[gi], best)

    sc = group_vals(score, grp)
    pr = group_vals(probs, grp)
    first = jnp.zeros(grp.shape, jnp.int32)
    fv = sc[0]
    for k in range(1, EXPERTS_PER_GROUP):
        better = sc[k] > fv
        first = jnp.where(better, k, first)
        fv = jnp.where(better, sc[k], fv)
    second = jnp.full(grp.shape, -1, jnp.int32)
    sv = jnp.full(fv.shape, -jnp.inf, F32)
    for k in range(EXPERTS_PER_GROUP):
        better = (first != k) & ((sc[k] > sv) | (second < 0))
        second = jnp.where(better, k, second)
        sv = jnp.where(better, sc[k], sv)
    lo = jnp.minimum(first, second)
    hi = jnp.maximum(first, second)
    p_lo = pr[0]
    p_hi = pr[0]
    for k in range(1, EXPERTS_PER_GROUP):
        p_lo = jnp.where(lo == k, pr[k], p_lo)
        p_hi = jnp.where(hi == k, pr[k], p_hi)
    tot = p_lo + p_hi
    pair = jnp.zeros(grp.shape, jnp.int32)
    for pi, (a, b) in enumerate(PAIRS):
        pair = jnp.where((lo == a) & (hi == b), pi, pair)
    bucket = grp * len(PAIRS) + pair
    bucket_ref[...] = bucket
    wlo_ref[...] = p_lo / tot
    whi_ref[...] = p_hi / tot

    tm = bucket.shape[1]
    onehot = jnp.where(lax.broadcasted_iota(jnp.int32, (BUCKET_ROWS, tm), 0) == bucket, 1.0, 0.0)
    run = run_ref[...]
    ranks = []
    for c in range(tm // RANK_CHUNK):
        oc = onehot[:, c * RANK_CHUNK:(c + 1) * RANK_CHUNK]
        before = _dot(oc.astype(BF16), tri_ref[...]) + jnp.concatenate([run] * (RANK_CHUNK // LANE), axis=1)
        ranks.append(jnp.sum(oc * before, axis=0, keepdims=True))
        run = run + jnp.broadcast_to(jnp.sum(oc, axis=1, keepdims=True), run.shape)
    rank_ref[...] = jnp.concatenate(ranks, axis=1).astype(jnp.int32)
    run_ref[...] = run
    cnt_ref[...] = run


ROUTER_TM = 1024


def _router_kernel(h_ref, *refs):
    _route(_load_rows(h_ref).astype(BF16), pl.program_id(0) == 0, *refs)


def _router(h2d, router_w, router_b):
    t, d = h2d.shape[0], D_MODEL
    tm = ROUTER_TM
    rw = jnp.pad(router_w.T, ((0, LANE - N_EXPERTS), (0, 0))).astype(BF16)
    rb = jnp.broadcast_to(jnp.pad(router_b, (0, LANE - N_EXPERTS))[:, None], (LANE, tm))
    tri = jnp.asarray(np.triu(np.ones((RANK_CHUNK, RANK_CHUNK), np.float32), 1), BF16)
    vec = lambda dt: jax.ShapeDtypeStruct((1, t), dt)
    row = pl.BlockSpec((1, tm), lambda i: (0, i))
    return pl.pallas_call(
        _router_kernel,
        out_shape=(vec(jnp.int32), vec(F32), vec(F32), vec(jnp.int32),
                   jax.ShapeDtypeStruct((BUCKET_ROWS, LANE), F32)),
        grid=(t // tm,),
        in_specs=[_row_spec(tm, lambda i: i),
                  pl.BlockSpec((LANE, d), lambda i: (0, 0)),
                  pl.BlockSpec((LANE, tm), lambda i: (0, 0)),
                  pl.BlockSpec((RANK_CHUNK, RANK_CHUNK), lambda i: (0, 0))],
        out_specs=(row, row, row, row, pl.BlockSpec((BUCKET_ROWS, LANE), lambda i: (0, 0))),
        scratch_shapes=[pltpu.VMEM((BUCKET_ROWS, LANE), F32)],
        compiler_params=_cparams(("arbitrary",)),
        name="router",
    )(h2d, rw, rb, tri)


def _moe_plan(bucket, rank, counts, w_lo, w_hi, t):
    tm = MOE_TM
    n_tiles = t // tm + N_BUCKETS
    p_rows = n_tiles * tm
    tiles_per = (counts + tm - 1) // tm
    tile_end = jnp.cumsum(tiles_per)
    tile_start = tile_end - tiles_per
    in_bucket = bucket[:, None] == jnp.arange(N_BUCKETS)[None, :]
    dest = jnp.sum(jnp.where(in_bucket, (tile_start * tm)[None, :], 0), axis=1) + rank
    payload = jnp.stack([jnp.arange(t, dtype=jnp.int32), lax.bitcast_convert_type(w_lo, jnp.int32),
                         lax.bitcast_convert_type(w_hi, jnp.int32), jnp.ones((t,), jnp.int32)], axis=1)
    plan = jnp.zeros((p_rows, 4), jnp.int32).at[dest].set(payload, unique_indices=True)
    src = plan[:, 0]
    wl = lax.bitcast_convert_type(plan[:, 1], F32)
    wh = lax.bitcast_convert_type(plan[:, 2], F32)
    prow = jnp.arange(p_rows, dtype=jnp.int32)
    spare = t + ((prow // tm) % 2) * tm + prow % tm
    dst = jnp.where(plan[:, 3] > 0, src, spare)
    n_used = tile_end[-1]
    tile_ids = jnp.arange(n_tiles)
    tb = jnp.sum((tile_end[None, :] <= tile_ids[:, None]).astype(jnp.int32), axis=1)
    tb = jnp.where(tile_ids < n_used, tb, tb[jnp.maximum(n_used - 1, 0)])
    tb = jnp.minimum(tb, N_BUCKETS - 1)
    pairs = np.asarray(PAIRS, np.int32)
    grp = tb // len(PAIRS)
    ea = grp * EXPERTS_PER_GROUP + jnp.asarray(pairs[:, 0])[tb % len(PAIRS)]
    eb = grp * EXPERTS_PER_GROUP + jnp.asarray(pairs[:, 1])[tb % len(PAIRS)]
    nvalid = jnp.clip(counts[tb] - (tile_ids - tile_start[tb]) * tm, 0, tm)
    nvalid = jnp.where(tile_ids < n_used, nvalid, 0).astype(jnp.int32)
    return (src.reshape(n_tiles, 1, tm), dst.reshape(n_tiles, 1, tm), wl.reshape(p_rows, 1), wh.reshape(p_rows, 1),
            ea.astype(jnp.int32), eb.astype(jnp.int32), nvalid, n_tiles)


MOE_DMA_GROUPS = 8


def _moe_kernel(ea_ref, eb_ref, nv_ref, src0_ref, srcn_ref, dstp_ref, wl_ref, wh_ref, h_hbm,
                w1a_ref, w3a_ref, w2a_ref, w1b_ref, w3b_ref, w2b_ref, lg_ref, lb_ref,
                o_hbm, xbuf0, xbuf1, obuf0, obuf1, wa1, wa3, wa2, wb1, wb3, wb2, gsem, ssem):
    i = pl.program_id(0)
    tm = xbuf0.shape[0]
    xbufs = (xbuf0, xbuf1)
    obufs = (obuf0, obuf1)
    nv = nv_ref[i]
    nv_p1 = nv_ref[jnp.maximum(i - 1, 0)]
    odd = lax.rem(i, 2) == 1
    even = jnp.logical_not(odd)

    def gather(r, p, row):
        return pltpu.make_async_copy(h_hbm.at[pl.ds(row, 1)], xbufs[p].at[pl.ds(r, 1)], gsem.at[p])

    def scatter(r, p, row):
        return pltpu.make_async_copy(obufs[p].at[pl.ds(r, 1)], o_hbm.at[pl.ds(row, 1)], ssem.at[p])

    def wait_gather(p):
        for r in range(tm):
            gather(r, p, 0).wait()

    def wait_scatter(p):
        for r in range(tm):
            scatter(r, p, 0).wait()

    def tile(p, age):
        q = 1 - p
        half = MOE_DMA_GROUPS // 2
        per = tm // half

        def issue(k):
            for r in range((k % half) * per, (k % half + 1) * per):
                if k < half:
                    gather(r, q, srcn_ref[0, 0, r]).start()
                elif age >= 1:
                    scatter(r, q, dstp_ref[0, 0, r]).start()

        wait_gather(p)
        x = _load_rows(xbufs[p])
        xb = x.astype(BF16)
        issue(0)
        h1 = _dot(xb, wa1[...])
        issue(1)
        h3 = _dot(xb, wa3[...])
        issue(2)
        ya = _dot((jax.nn.silu(h1) * h3).astype(BF16), wa2[...])
        issue(3)
        h1 = _dot(xb, wb1[...])
        issue(4)
        h3 = _dot(xb, wb3[...])
        issue(5)
        yb = _dot((jax.nn.silu(h1) * h3).astype(BF16), wb2[...])
        issue(6)
        y = wl_ref[...] * ya + wh_ref[...] * yb
        issue(7)
        out = _layer_norm(DN_ALPHA * x + y, lg_ref[...], lb_ref[...])
        if age >= 2:
            wait_scatter(p)
        _store_rows(obufs[p], out)

    @pl.when(i == 0)
    def _():
        obuf1[...] = jnp.zeros(obuf1.shape, F32)
        n_real = o_hbm.shape[0] - 2 * tm
        for blk in range(2):
            cp = pltpu.make_async_copy(obuf1, o_hbm.at[pl.ds(n_real + blk * tm, tm)], ssem.at[1])
            cp.start()
            cp.wait()

    prev = jnp.maximum(i - 1, 0)

    @pl.when((nv > 0) & ((i == 0) | (ea_ref[i] != ea_ref[prev])))
    def _():
        wa1[...] = w1a_ref[0, 0].astype(BF16)
        wa3[...] = w3a_ref[0, 0].astype(BF16)
        wa2[...] = w2a_ref[0, 0].astype(BF16)

    @pl.when((nv > 0) & ((i == 0) | (eb_ref[i] != eb_ref[prev])))
    def _():
        wb1[...] = w1b_ref[0, 0].astype(BF16)
        wb3[...] = w3b_ref[0, 0].astype(BF16)
        wb2[...] = w2b_ref[0, 0].astype(BF16)

    @pl.when((nv > 0) & (i == 0))
    def _():
        def start0(r, carry):
            gather(r, 0, src0_ref[0, 0, r]).start()
            return carry
        lax.fori_loop(0, tm, start0, 0)
        tile(0, 0)

    pl.when((nv > 0) & (i == 1))(functools.partial(tile, 1, 1))
    pl.when((nv > 0) & (i >= 2) & even)(functools.partial(tile, 0, 2))
    pl.when((nv > 0) & (i >= 2) & odd)(functools.partial(tile, 1, 2))

    def tail(p):
        q = 1 - p
        wait_gather(p)

        @pl.when(i >= 2)
        def _():
            wait_scatter(p)

        def start(r, carry):
            scatter(r, q, dstp_ref[0, 0, r]).start()
            return carry
        lax.fori_loop(0, tm, start, 0)
        wait_scatter(q)

    is_tail = (nv == 0) & (i >= 1) & (nv_p1 > 0)
    pl.when(is_tail & even)(functools.partial(tail, 0))
    pl.when(is_tail & odd)(functools.partial(tail, 1))


def _moe(h3, bucket, rank, counts, w_lo, w_hi, w1, w3, w2, layer, ln_g, ln_b):
    t, d = h3.shape[0], D_MODEL
    tm = MOE_TM
    assert tm % MOE_DMA_GROUPS == 0
    src, dst, wl, wh, ea, eb, nvalid, n_tiles = _moe_plan(bucket, rank, counts, w_lo, w_hi, t)
    wspec = lambda shape, which: pl.BlockSpec(
        (1,) + shape, (lambda i, ea_r, eb_r, nv_r: (layer, ea_r[i], 0, 0)) if which == 0
        else (lambda i, ea_r, eb_r, nv_r: (layer, eb_r[i], 0, 0)))
    w13 = (1, d, D_EXPERT)
    w2s = (1, D_EXPERT, d)
    smem_tile = lambda fn: pl.BlockSpec((1, 1, tm), lambda i, *_: (fn(i), 0, 0), memory_space=pltpu.SMEM)
    grid_spec = pltpu.PrefetchScalarGridSpec(
        num_scalar_prefetch=3,
        grid=(n_tiles,),
        in_specs=[smem_tile(lambda i: 0),
                  smem_tile(lambda i: jnp.minimum(i + 1, n_tiles - 1)),
                  smem_tile(lambda i: jnp.maximum(i - 1, 0)),
                  pl.BlockSpec((tm, 1), lambda i, *_: (i, 0)),
                  pl.BlockSpec((tm, 1), lambda i, *_: (i, 0)),
                  pl.BlockSpec(memory_space=pl.ANY),
                  wspec(w13, 0), wspec(w13, 0), wspec(w2s, 0),
                  wspec(w13, 1), wspec(w13, 1), wspec(w2s, 1),
                  pl.BlockSpec((1, d), lambda i, *_: (0, 0)),
                  pl.BlockSpec((1, d), lambda i, *_: (0, 0))],
        out_specs=pl.BlockSpec(memory_space=pl.ANY),
        scratch_shapes=[pltpu.VMEM((tm,) + ROW_TILE, F32)] * 4
        + [pltpu.VMEM(w13[1:], BF16), pltpu.VMEM(w13[1:], BF16), pltpu.VMEM(w2s[1:], BF16)] * 2
        + [pltpu.SemaphoreType.DMA((2,)), pltpu.SemaphoreType.DMA((2,))],
    )
    return pl.pallas_call(
        _moe_kernel,
        out_shape=jax.ShapeDtypeStruct((t + 2 * tm,) + ROW_TILE, F32),
        grid_spec=grid_spec,
        compiler_params=_cparams(("arbitrary",)),
        name="moe",
    )(ea, eb, nvalid, src, src, dst, wl, wh, h3, w1, w3, w2, w1, w3, w2,
      ln_g.reshape(1, d), ln_b.reshape(1, d))


def kernel(x, ln0_g, ln0_b, w_in, pool_w, pool_scale, cmp_k_pe, cmp_k_w1, cmp_k_w2, cmp_v_pe, cmp_v_w1, cmp_v_w2, rel_bias, conv_w, conv_b, dt_bias, a_log, d_skip, ssd_norm_g, w_br_pool, w_br_nsa, w_br_ssd, w_out, ln1_g, ln1_b, router_w, router_b, exp_w1, exp_w3, exp_w2, ln2_g, ln2_b):
    bsz, s, d = x.shape
    assert d == D_MODEL and s % MIX_TS == 0 and s // CMP_STRIDE == LANE and (bsz * s) % PROJ_TM == 0
    t = bsz * s
    ts, ws, bc = _nsa_bias_tables(rel_bias)
    et, ovt = _nsa_consts(s)
    h = x.reshape(t, d)
    for i in range(DEPTH):
        if i == 0:
            proj, h = _proj(h, _prep_w_in(w_in[i]), t, ln=(ln0_g, ln0_b))
        else:
            proj = _proj(h, _prep_w_in(w_in[i]), t)
        kc, vc = _compress(proj, bsz, s, cmp_k_pe[i], cmp_k_w1[i], cmp_k_w2[i],
                           cmp_v_pe[i], cmp_v_w1[i], cmp_v_w2[i])
        nsa_out = _nsa(proj, kc, vc, ts, ws, bc, et, ovt, bsz, s)
        ssd_y = _ssd(proj, bsz, s, conv_w[i], conv_b[i], dt_bias[i], a_log[i], d_skip[i], ssd_norm_g[i])
        h1 = _mix(proj, nsa_out, ssd_y, h, bsz, s, pool_w[i], pool_scale[i], w_br_pool[i], w_br_nsa[i],
                  w_br_ssd[i], w_out[i], ln1_g[i], ln1_b[i])
        bucket, w_lo, w_hi, rank, cnt = _router(h1, router_w, router_b)
        counts = cnt[:N_BUCKETS, 0].astype(jnp.int32)
        h = _moe(h1, bucket[0], rank[0], counts, w_lo[0], w_hi[0], exp_w1, exp_w3, exp_w2, i,
                 ln2_g[i], ln2_b[i])
    return h[:t].reshape(bsz, s, d)
```

```python
import functools
import math

import numpy as np
import jax
import jax.numpy as jnp
from jax import lax
from jax.experimental import pallas as pl
from jax.experimental.pallas import tpu as pltpu

F32 = jnp.float32
BF16 = jnp.bfloat16

D_MODEL = 1024
DEPTH = 2
DN_ALPHA = (2.0 * DEPTH) ** 0.25
LN_EPS = 1e-5
NEG = -1e30
BIG = 1e6

POOL_WINDOWS = (2, 4, 8, 16)
POOL_GROUP = 128
POOL_WIDTH = 512
MAX_POOL_WINDOW = 16

NSA_HEAD_DIM = 64
NSA_KV_HEADS = 2
NSA_HPG = 4
NSA_HEADS = 8
NSA_WIDTH = 512
CMP_LEN = 32
CMP_STRIDE = 16
CMP_HIDDEN = 128
SLC_LEN = 64
SLC_TOPN = 8
WIN_LEN = 512
Q_BLOCK = 128
REL_BUCKETS = 32
REL_MAX_DIST = 128

SSD_D_INNER = 1024
SSD_HEAD_DIM = 64
SSD_HEADS = 16
SSD_GROUPS = 4
SSD_STATE = 128
SSD_CONV = 4
SSD_CHUNK = 128
SSD_XBC = 2048

N_EXPERTS = 16
N_EXPERT_GROUPS = 4
EXPERTS_PER_GROUP = 4
D_EXPERT = 512
PAIRS = ((0, 1), (0, 2), (0, 3), (1, 2), (1, 3), (2, 3))
N_BUCKETS = N_EXPERT_GROUPS * len(PAIRS)

LANE = 128
SUBLANE = 8
VMEM_LIMIT = 48 * 1024 * 1024

C_POOL = 0
C_Q = 512
C_KV = 1024
C_NG = 1792
C_DT = 1920
C_XBC = 2048
C_Z = 4096
C_GATE = 5120
D_PAD = 8192

PROJ_TM = 1024
PROJ_TN = 2048
MIX_TS = 512
MOE_TM = 256


def _dot(a, b):
    return jnp.dot(a, b, preferred_element_type=F32)


def _dot_nt(a, b):
    return lax.dot_general(a, b, (((1,), (1,)), ((), ())), preferred_element_type=F32)


def _dot_tn(a, b):
    return lax.dot_general(a, b, (((0,), (0,)), ((), ())), preferred_element_type=F32)


def _split3(x):
    x1 = x.astype(BF16)
    r1 = x - x1.astype(F32)
    x2 = r1.astype(BF16)
    x3 = (r1 - x2.astype(F32)).astype(BF16)
    return x1, x2, x3


def _layer_norm(x, g, b):
    mu = jnp.mean(x, -1, keepdims=True)
    xc = x - mu
    var = jnp.mean(xc * xc, -1, keepdims=True)
    return xc * lax.rsqrt(var + LN_EPS) * g + b


def _cparams(sem):
    return pltpu.CompilerParams(dimension_semantics=sem, vmem_limit_bytes=VMEM_LIMIT)


ROW_TILE = (D_MODEL,)


def _load_rows(ref):
    return ref[...]


def _store_rows(ref, val):
    ref[...] = val


def _row_spec(n, index):
    return pl.BlockSpec((n,) + ROW_TILE, lambda *ids: (index(*ids), 0))


def _proj_kernel(h_ref, w_ref, o_ref, hb_ref):
    @pl.when(pl.program_id(1) == 0)
    def _():
        hb_ref[...] = _load_rows(h_ref).astype(BF16)

    o_ref[...] = _dot(hb_ref[...], w_ref[...])


def _ln_proj_kernel(x_ref, g_ref, b_ref, w_ref, o_ref, h_ref, hb_ref):
    @pl.when(pl.program_id(1) == 0)
    def _():
        h = _layer_norm(x_ref[...], g_ref[...], b_ref[...])
        _store_rows(h_ref, h)
        hb_ref[...] = h.astype(BF16)

    o_ref[...] = _dot(hb_ref[...], w_ref[...])


def _proj(h2d, w_pad, t, ln=None):
    d = D_MODEL
    grid = (t // PROJ_TM, D_PAD // PROJ_TN)
    w_spec = pl.BlockSpec((d, PROJ_TN), lambda i, j: (0, j))
    o_spec = pl.BlockSpec((PROJ_TM, PROJ_TN), lambda i, j: (i, j))
    o_shape = jax.ShapeDtypeStruct((t, D_PAD), F32)
    scratch = [pltpu.VMEM((PROJ_TM, d), BF16)]
    params = _cparams(("parallel", "arbitrary"))
    if ln is None:
        return pl.pallas_call(
            _proj_kernel, out_shape=o_shape, grid=grid,
            in_specs=[_row_spec(PROJ_TM, lambda i, j: i), w_spec], out_specs=o_spec,
            scratch_shapes=scratch, compiler_params=params, name="proj",
        )(h2d, w_pad)
    vec = pl.BlockSpec((1, d), lambda i, j: (0, 0))
    return pl.pallas_call(
        _ln_proj_kernel, out_shape=(o_shape, jax.ShapeDtypeStruct((t,) + ROW_TILE, F32)), grid=grid,
        in_specs=[_row_spec(PROJ_TM, lambda i, j: i), vec, vec, w_spec],
        out_specs=(o_spec, _row_spec(PROJ_TM, lambda i, j: i)),
        scratch_shapes=scratch, compiler_params=params, name="ln_proj",
    )(h2d, ln[0].reshape(1, d), ln[1].reshape(1, d), w_pad)


def _pad_cols(w, n):
    return jnp.pad(w, ((0, 0), (0, n - w.shape[1])))


def _prep_w_in(w_in):
    d = w_in.shape[0]
    o = 0
    w_pool = w_in[:, o:o + POOL_WIDTH]; o += POOL_WIDTH
    w_q = w_in[:, o:o + NSA_WIDTH]; o += NSA_WIDTH
    w_kv = w_in[:, o:o + 768]; o += 768
    w_ng = w_in[:, o:o + 24]; o += 24
    w_z = w_in[:, o:o + SSD_D_INNER]; o += SSD_D_INNER
    w_xbc = w_in[:, o:o + SSD_XBC]; o += SSD_XBC
    w_dt = w_in[:, o:o + SSD_HEADS]; o += SSD_HEADS
    w_gate = w_in[:, o:o + 3 * D_MODEL]
    w_q = w_q.reshape(d, NSA_KV_HEADS, NSA_HPG, NSA_HEAD_DIM).transpose(0, 2, 1, 3).reshape(d, NSA_WIDTH)
    w = jnp.concatenate([w_pool, w_q, w_kv, _pad_cols(w_ng, LANE), _pad_cols(w_dt, LANE),
                         w_xbc, w_z, w_gate], axis=1)
    return w.astype(BF16)


def _compress_one(tok_ref, pe_ref, w1_ref, w2_ref):
    n16 = tok_ref.shape[0] // CMP_STRIDE
    acc_a = jnp.zeros((n16, 2 * CMP_HIDDEN), F32)
    acc_b = jnp.zeros((n16, 2 * CMP_HIDDEN), F32)
    for l in range(CMP_STRIDE):
        rows = tok_ref[pl.ds(l, n16, stride=CMP_STRIDE), :]
        xa = (rows + pe_ref[l:l + 1, :]).astype(BF16)
        xb = (rows + pe_ref[CMP_STRIDE + l:CMP_STRIDE + l + 1, :]).astype(BF16)
        acc_a = acc_a + _dot(xa, w1_ref[l])
        acc_b = acc_b + _dot(xb, w1_ref[CMP_STRIDE + l])
    hid = acc_a + pltpu.roll(acc_b, n16 - 1, axis=0)
    hid = jax.nn.gelu(hid)
    return _dot(hid.astype(BF16), w2_ref[...])


def _compress_kernel(kt_ref, vt_ref, pek_ref, w1k_ref, w2k_ref, pev_ref, w1v_ref, w2v_ref,
                     kc_ref, vc_ref):
    kc_ref[0] = _compress_one(kt_ref, pek_ref, w1k_ref, w2k_ref).astype(BF16)
    vc_ref[0] = _compress_one(vt_ref, pev_ref, w1v_ref, w2v_ref).T.astype(BF16)


def _blockdiag2(w):
    z = jnp.zeros_like(w)
    return jnp.concatenate([jnp.concatenate([w, z], -1), jnp.concatenate([z, w], -1)], -2)


def _compress(proj2d, bsz, s, pe_k, w1_k, w2_k, pe_v, w1_v, w2_v):
    n16 = s // CMP_STRIDE

    def prep(pe, w1, w2):
        pe2 = jnp.concatenate([pe, pe], axis=-1)
        w1b = _blockdiag2(w1.reshape(CMP_LEN, NSA_HEAD_DIM, CMP_HIDDEN)).astype(BF16)
        w2b = _blockdiag2(w2).astype(BF16)
        return pe2, w1b, w2b

    pk = prep(pe_k, w1_k, w2_k)
    pv = prep(pe_v, w1_v, w2_v)
    full = lambda shape: pl.BlockSpec(shape, lambda b: (0,) * len(shape))
    wspecs = [full((CMP_LEN, LANE)), full((CMP_LEN, LANE, 2 * CMP_HIDDEN)), full((2 * CMP_HIDDEN, LANE))]
    return pl.pallas_call(
        _compress_kernel,
        out_shape=(jax.ShapeDtypeStruct((bsz, n16, LANE), BF16),
                   jax.ShapeDtypeStruct((bsz, n16, LANE), BF16)),
        grid=(bsz,),
        in_specs=[pl.BlockSpec((s, LANE), lambda b: (b, C_KV // LANE)),
                  pl.BlockSpec((s, LANE), lambda b: (b, C_KV // LANE + 1))] + wspecs + wspecs,
        out_specs=(pl.BlockSpec((1, n16, LANE), lambda b: (b, 0, 0)),
                   pl.BlockSpec((1, n16, LANE), lambda b: (b, 0, 0))),
        compiler_params=_cparams(("parallel",)),
        name="nsa_compress",
    )(proj2d, proj2d, *pk, *pv)


def _rel_bucket(dist):
    n = jnp.maximum(dist, 0)
    max_exact = REL_BUCKETS // 2
    nf = jnp.maximum(n, 1).astype(F32)
    large = max_exact + (jnp.log(nf / max_exact) / math.log(REL_MAX_DIST / max_exact)
                         * (REL_BUCKETS - max_exact)).astype(jnp.int32)
    large = jnp.minimum(large, REL_BUCKETS - 1)
    return jnp.where(n < max_exact, n, large)


LOG2E = math.log2(math.e)
WIN_BACK = WIN_LEN // Q_BLOCK
WIN_KEYS = (WIN_BACK + 1) * Q_BLOCK
WIN_STRIP_TILES = 2 * WIN_BACK + 1
SEL_TILES = 4
SEL_KEYS = SEL_TILES * Q_BLOCK
SEL_STRIP_TILES = 3 * SEL_TILES - 1
CMP_BIAS_ROWS = 2 * LANE
PV_ROWS = NSA_HEAD_DIM + 16


def _nsa_bias_tables(rel_bias):
    tbl = rel_bias.astype(F32)
    far = tbl[REL_BUCKETS - 1]
    hq = NSA_HPG * Q_BLOCK

    def lookup(dist):
        onehot = (_rel_bucket(jnp.asarray(dist))[..., None] == jnp.arange(REL_BUCKETS)).astype(F32)
        return jnp.einsum("...b,bh->...h", onehot, tbl, precision=lax.Precision.HIGHEST)

    k = np.arange(Q_BLOCK)[None, :, None]
    q = np.arange(Q_BLOCK)[None, None, :]

    def strip(first, last, visible):
        n = first - last + 1
        dist = np.arange(first, last - 1, -1)[:, None, None] * Q_BLOCK + q - k
        v = jnp.where(jnp.asarray(visible(dist))[..., None], (lookup(dist) - far) * LOG2E, NEG)
        v = v.reshape(n, Q_BLOCK, Q_BLOCK, NSA_KV_HEADS, NSA_HPG).transpose(3, 0, 1, 4, 2)
        return v.reshape(NSA_KV_HEADS, n * Q_BLOCK, hq)

    ts = strip(2 * SEL_TILES - 1, 1 - SEL_TILES, lambda d: d >= 0)
    ws = strip(WIN_BACK, -WIN_BACK, lambda d: (d >= 0) & (d < WIN_LEN))
    r = np.arange(CMP_BIAS_ROWS)[:, None]
    dist_c = q[0] - CMP_STRIDE * (r - LANE) - (CMP_LEN - 1)
    bc = jnp.where(jnp.asarray(dist_c >= 0)[..., None], lookup(dist_c) * LOG2E, NEG)
    bc = bc.reshape(CMP_BIAS_ROWS, Q_BLOCK, NSA_KV_HEADS, NSA_HPG).transpose(2, 0, 3, 1)
    bc = bc.reshape(NSA_KV_HEADS, CMP_BIAS_ROWS, hq)
    return ts, ws, bc


def _nsa_consts(s):
    n_slc = s // SLC_LEN
    et = (np.arange(s)[:, None] // SLC_LEN == np.arange(LANE)[None, :]).astype(np.float32)
    n_cmp = (s - CMP_LEN) // CMP_STRIDE + 1
    c0 = np.arange(n_cmp) * CMP_STRIDE
    s0 = np.arange(n_slc) * SLC_LEN
    ov = np.clip(np.minimum(c0[None, :] + CMP_LEN, s0[:, None] + SLC_LEN)
                 - np.maximum(c0[None, :], s0[:, None]), 0, None) / CMP_LEN
    ovt = np.zeros((n_slc, LANE), np.float32)
    ovt[:, :n_cmp] = ov
    return jnp.asarray(et, BF16), jnp.asarray(ovt, BF16)


def _nsa_kernel(q_ref, ng_ref, ks_ref, vs_ref, kw_ref, vw_ref, kc_ref, vct_ref,
                ts_ref, ws_ref, bc_ref, et_ref, ovt_ref, o_ref,
                ksb_ref, kwb_ref, vst_ref, vwt_ref, *, nq, n_slc):
    qi = pl.program_id(1)
    hq = NSA_HPG * Q_BLOCK
    dh = NSA_HEAD_DIM
    groups = range(NSA_KV_HEADS)
    gsl = [slice(g * dh, (g + 1) * dh) for g in groups]

    @pl.when(qi == 0)
    def _():
        ksb_ref[:, 0:LANE] = ks_ref[...].astype(BF16)
        ksb_ref[:, LANE:] = et_ref[...]
        kwb_ref[...] = kw_ref[...].astype(BF16)
        ones_rows = jnp.where(lax.broadcasted_iota(jnp.int32, (PV_ROWS - dh, ks_ref.shape[0]), 0) == 0, 1.0, 0.0)
        for g in groups:
            vst_ref[g, dh:, :] = ones_rows.astype(BF16)
            vwt_ref[g, dh:, :] = ones_rows.astype(BF16)
        for kt in range(nq):
            sl = slice(kt * Q_BLOCK, (kt + 1) * Q_BLOCK)
            vs_t = vs_ref[sl, :].T
            vw_t = vw_ref[sl, :].T
            for g in groups:
                vst_ref[g, 0:dh, sl] = vs_t[gsl[g], :].astype(BF16)
                vwt_ref[g, 0:dh, sl] = vw_t[gsl[g], :].astype(BF16)

    n_super = qi // SEL_TILES + 1
    for n in range(1, nq // SEL_TILES + 1):
        pl.when(n_super == n)(functools.partial(
            _nsa_step, q_ref, ng_ref, kc_ref, vct_ref, ts_ref, ws_ref, bc_ref, ovt_ref, o_ref,
            ksb_ref, kwb_ref, vst_ref, vwt_ref, n_slc=n_slc, n_sel=n))


def _nsa_step(q_ref, ng_ref, kc_ref, vct_ref, ts_ref, ws_ref, bc_ref, ovt_ref, o_ref,
              ksb_ref, kwb_ref, vst_ref, vwt_ref, *, n_slc, n_sel):
    qi = pl.program_id(1)
    hq = NSA_HPG * Q_BLOCK
    dh = NSA_HEAD_DIM
    groups = range(NSA_KV_HEADS)
    gsl = [slice(g * dh, (g + 1) * dh) for g in groups]
    row_lo = lax.broadcasted_iota(jnp.int32, (LANE, hq), 0) < dh

    def heads4(m):
        return jnp.concatenate([m] * NSA_HPG, axis=1)

    gates_t = jax.nn.sigmoid(ng_ref[...]).T
    q_t = jnp.concatenate(
        [(q_ref[:, h * LANE:(h + 1) * LANE] * (dh ** -0.5 * LOG2E)).T for h in range(NSA_HPG)], axis=1)
    qg_t = [jnp.where(row_lo if g == 0 else jnp.logical_not(row_lo), q_t, 0.0).astype(BF16) for g in groups]

    boff = pl.multiple_of(LANE - SUBLANE * qi, SUBLANE)
    jrow = lax.broadcasted_iota(jnp.int32, (n_slc, LANE), 0)
    tj = qi * Q_BLOCK + lax.broadcasted_iota(jnp.int32, (n_slc, LANE), 1)
    jt = lax.shift_right_logical(tj, SLC_LEN.bit_length() - 1)
    forced = (jrow == 0) | (jrow == jt) | (jrow == jt - 1)
    valid = jrow * SLC_LEN <= tj
    o_c = []
    q_aug = []
    for g in groups:
        s = _dot(kc_ref[0], qg_t[g]) + bc_ref[g, pl.ds(boff, LANE), :]
        mx = jnp.max(s, axis=0, keepdims=True)
        p = jnp.exp2(s - mx)
        lsum = jnp.sum(p, axis=0, keepdims=True)
        pb = (p * jnp.where(mx > 0.5 * NEG, 1.0 / lsum, 0.0)).astype(BF16)
        o_c.append(_dot(vct_ref[0, gsl[g], :], pb))
        imp4 = _dot(ovt_ref[...], pb)
        imp = imp4[:, 0:LANE]
        for h in range(1, NSA_HPG):
            imp = imp + imp4[:, h * LANE:(h + 1) * LANE]
        val = jnp.where(forced, BIG, jnp.where(valid, imp, -BIG))
        rank = jnp.zeros((n_slc, LANE), F32)
        for i in range(n_slc):
            ci = val[i:i + 1, :]
            rank = rank + jnp.where(ci > val, 1.0, 0.0) + jnp.where(ci == val, (jrow > i).astype(F32), 0.0)
        block_rows = heads4(jnp.where(rank < SLC_TOPN, 0.0, NEG)).astype(BF16)
        q_aug.append(jnp.concatenate(
            [qg_t[g], block_rows, jnp.zeros((LANE - n_slc, hq), BF16)], axis=0))

    diag = qi - SEL_TILES * (n_sel - 1)
    nk = n_sel * SEL_KEYS
    lo = max(n_sel - 2, 0) * SEL_KEYS
    toff = pl.multiple_of((2 * SEL_TILES - 1 - diag) * Q_BLOCK - (nk - lo - SEL_KEYS), Q_BLOCK)
    o_s = []
    for g in groups:
        parts = [(lo, nk, _dot(ksb_ref[lo:nk, :], q_aug[g]) + ts_ref[g, pl.ds(toff, nk - lo), :])]
        if lo > 0:
            parts.append((0, lo, _dot(ksb_ref[0:lo, :], q_aug[g])))
        m = parts[0][2].max(axis=0, keepdims=True)
        for _, _, s in parts[1:]:
            m = jnp.maximum(m, s.max(axis=0, keepdims=True))
        acc = None
        for a, b, s in parts:
            pv = _dot(vst_ref[g, :, a:b], jnp.exp2(s - m).astype(BF16))
            acc = pv if acc is None else acc + pv
        o_s.append(acc[0:dh] * (1.0 / acc[dh:dh + 1]))

    w0 = jnp.maximum(qi - WIN_BACK, 0)
    woff = pl.multiple_of(w0 * Q_BLOCK, Q_BLOCK)
    boff_w = pl.multiple_of((WIN_BACK - (qi - w0)) * Q_BLOCK, Q_BLOCK)
    kw_bf = kwb_ref[pl.ds(woff, WIN_KEYS), :]
    o_w = []
    for g in groups:
        s = _dot(kw_bf, qg_t[g]) + ws_ref[g, pl.ds(boff_w, WIN_KEYS), :]
        p = jnp.exp2(s - jnp.max(s, axis=0, keepdims=True)).astype(BF16)
        acc = _dot(vwt_ref[g, :, pl.ds(woff, WIN_KEYS)], p)
        o_w.append(acc[0:dh] * (1.0 / acc[dh:dh + 1]))

    for h in range(NSA_HPG):
        hl = slice(h * LANE, (h + 1) * LANE)
        halves = []
        for g in groups:
            c = g * NSA_HPG + h
            halves.append(gates_t[c:c + 1, :] * o_c[g][:, hl]
                          + gates_t[NSA_HEADS + c:NSA_HEADS + c + 1, :] * o_s[g][:, hl]
                          + gates_t[2 * NSA_HEADS + c:2 * NSA_HEADS + c + 1, :] * o_w[g][:, hl])
        o_ref[:, hl] = jnp.concatenate(halves, axis=0).T.astype(o_ref.dtype)


def _nsa(proj2d, kc, vct, ts, ws, bc, et, ovt, bsz, s):
    nq = s // Q_BLOCK
    n_slc = s // SLC_LEN
    hq = NSA_HPG * Q_BLOCK
    kvb = C_KV // LANE
    kv_spec = lambda blk: pl.BlockSpec((s, LANE), lambda b, qi: (b, kvb + blk))
    full = lambda shape: pl.BlockSpec(shape, lambda b, qi: (0,) * len(shape), pipeline_mode=pl.Buffered(1))
    return pl.pallas_call(
        functools.partial(_nsa_kernel, nq=nq, n_slc=n_slc),
        out_shape=jax.ShapeDtypeStruct((bsz * s, NSA_WIDTH), BF16),
        grid=(bsz, nq),
        in_specs=[pl.BlockSpec((Q_BLOCK, NSA_WIDTH), lambda b, qi: (b * nq + qi, C_Q // NSA_WIDTH)),
                  pl.BlockSpec((Q_BLOCK, LANE), lambda b, qi: (b * nq + qi, C_NG // LANE)),
                  kv_spec(2), kv_spec(3), kv_spec(4), kv_spec(5),
                  pl.BlockSpec((1, s // CMP_STRIDE, LANE), lambda b, qi: (b, 0, 0)),
                  pl.BlockSpec((1, LANE, s // CMP_STRIDE), lambda b, qi: (b, 0, 0)),
                  full((NSA_KV_HEADS, SEL_STRIP_TILES * Q_BLOCK, hq)),
                  full((NSA_KV_HEADS, WIN_STRIP_TILES * Q_BLOCK, hq)),
                  full((NSA_KV_HEADS, CMP_BIAS_ROWS, hq)),
                  full((s, LANE)),
                  full((n_slc, LANE))],
        out_specs=pl.BlockSpec((Q_BLOCK, NSA_WIDTH), lambda b, qi: (b * nq + qi, 0)),
        scratch_shapes=[pltpu.VMEM((s, 2 * LANE), BF16), pltpu.VMEM((s, LANE), BF16),
                        pltpu.VMEM((NSA_KV_HEADS, PV_ROWS, s), BF16), pltpu.VMEM((NSA_KV_HEADS, PV_ROWS, s), BF16)],
        compiler_params=_cparams(("parallel", "arbitrary")),
        name="nsa_attn",
    )(proj2d, proj2d, proj2d, proj2d, proj2d, proj2d, kc, vct, ts, ws, bc, et, ovt)


SSD_BATCH = 2


def _ssd_kernel(xbc_ref, prev_ref, z_ref, dt_ref, *rest):
    shared, o_ref, state_ref = rest[:-2], rest[-2], rest[-1]
    c = pl.program_id(1)
    for u in range(SSD_BATCH):
        _ssd_chunk(c, xbc_ref.at[u], prev_ref.at[u], z_ref.at[u], dt_ref.at[u], *shared,
                   o_ref.at[u], state_ref.at[u])


def _ssd_chunk(c, xbc_ref, prev_ref, z_ref, dt_ref, cw_ref, cb_ref, dtb_ref, alog_ref,
               dskip_ref, ng_ref, e16_ref, tril_ref, o_ref, state_ref):
    L = SSD_CHUNK
    pair_w = 2 * SSD_HEAD_DIM

    @pl.when(c == 0)
    def _():
        state_ref[...] = jnp.zeros(state_ref.shape, F32)

    xcat = jnp.concatenate([jnp.where(c > 0, prev_ref[...], 0.0), xbc_ref[...]], axis=0)
    acc = cw_ref[0:1, :] * xcat
    for k in range(1, SSD_CONV):
        acc = pltpu.roll(acc, 1, axis=0) + cw_ref[k:k + 1, :] * xcat
    xbc = jax.nn.silu(acc[SUBLANE:, :] + cb_ref[...])
    xs = xbc[:, :SSD_D_INNER]
    bmat = xbc[:, SSD_D_INNER:SSD_D_INNER + SSD_GROUPS * SSD_STATE]
    cmat = xbc[:, SSD_D_INNER + SSD_GROUPS * SSD_STATE:]

    dt = jax.nn.softplus(dt_ref[...] + dtb_ref[...])
    tril = tril_ref[...]
    e16 = e16_ref[...]
    d1, d2, d3 = _split3(dt)
    dt_x = _dot(d1, e16) + _dot(d2, e16) + _dot(d3, e16)
    a_c = dt * (-jnp.exp(alog_ref[...]) * LOG2E)
    c1, c2, c3 = _split3(a_c)
    acs = _dot(tril, c1) + _dot(tril, c2) + _dot(tril, c3)
    x1, x2, x3 = _split3(acs)
    acs_x = _dot(x1, e16) + _dot(x2, e16) + _dot(x3, e16)
    acs_t = acs.T
    last = acs_x[L - 1:L, :]
    eacs = jnp.exp2(acs_x)
    decay_in = jnp.exp2(last - acs_x)
    cdec = jnp.exp2(last)
    xdt = xs * dt_x
    wst = xdt * decay_in

    row = lax.broadcasted_iota(jnp.int32, (L, L), 0)
    lane = lax.broadcasted_iota(jnp.int32, (L, L), 1)
    tri = row >= lane
    lo = lane < SSD_HEAD_DIM

    ys = []
    for g in range(SSD_GROUPS):
        cg = cmat[:, g * SSD_STATE:(g + 1) * SSD_STATE].astype(BF16)
        bg = bmat[:, g * SSD_STATE:(g + 1) * SSD_STATE].astype(BF16)
        cb = _dot_nt(cg, bg)
        for pr in range(2):
            pp = g * 2 + pr
            ls = slice(pp * pair_w, (pp + 1) * pair_w)
            xp = xdt[:, ls]
            yd = None
            for e in range(2):
                hd = 2 * pp + e
                diff = acs[:, hd:hd + 1] - acs_t[hd:hd + 1, :]
                seg = jnp.exp2(jnp.where(tri, diff, NEG))
                pm = (cb * seg).astype(BF16)
                xe = jnp.where(lo if e == 0 else jnp.logical_not(lo), xp, 0.0).astype(BF16)
                term = _dot(pm, xe)
                yd = term if yd is None else yd + term
            st = state_ref[pp]
            yo = _dot(cg, st.astype(BF16)) * eacs[:, ls]
            ys.append(yd + yo)
            state_ref[pp] = cdec[:, ls] * st + _dot_tn(bg, wst[:, ls].astype(BF16))

    y = jnp.concatenate(ys, axis=1) + xs * dskip_ref[...]
    y = y * jax.nn.silu(z_ref[...])
    gw = SSD_D_INNER // SSD_GROUPS
    outs = []
    for gi in range(SSD_GROUPS):
        yg = y[:, gi * gw:(gi + 1) * gw]
        outs.append(yg * lax.rsqrt(jnp.mean(yg * yg, -1, keepdims=True) + LN_EPS))
    o_ref[...] = (jnp.concatenate(outs, axis=1) * ng_ref[...]).astype(o_ref.dtype)


def _ssd(proj2d, bsz, s, conv_w, conv_b, dt_bias, a_log, d_skip, norm_g):
    nc = s // SSD_CHUNK
    L = SSD_CHUNK
    pad16 = lambda v: jnp.pad(v.reshape(1, SSD_HEADS), ((0, 0), (0, LANE - SSD_HEADS)))
    rep64 = lambda v: jnp.repeat(v, SSD_HEAD_DIM).reshape(1, SSD_D_INNER)
    e16 = np.zeros((LANE, SSD_D_INNER), np.float32)
    e16[np.arange(SSD_D_INNER) // SSD_HEAD_DIM, np.arange(SSD_D_INNER)] = 1.0
    tril = np.tril(np.ones((L, L), np.float32))
    full = lambda shape: pl.BlockSpec(shape, lambda b, c: (0,) * len(shape))
    nb = SSD_BATCH
    assert bsz % nb == 0
    proj3d = proj2d.reshape(bsz, s, D_PAD)
    out = pl.pallas_call(
        _ssd_kernel,
        out_shape=jax.ShapeDtypeStruct((bsz, s, SSD_D_INNER), BF16),
        grid=(bsz // nb, nc),
        in_specs=[pl.BlockSpec((nb, L, SSD_XBC), lambda b, c: (b, c, C_XBC // SSD_XBC)),
                  pl.BlockSpec((nb, SUBLANE, SSD_XBC),
                               lambda b, c: (b, jnp.maximum(c * (L // SUBLANE) - 1, 0), C_XBC // SSD_XBC)),
                  pl.BlockSpec((nb, L, SSD_D_INNER), lambda b, c: (b, c, C_Z // SSD_D_INNER)),
                  pl.BlockSpec((nb, L, LANE), lambda b, c: (b, c, C_DT // LANE)),
                  full((SSD_CONV, SSD_XBC)), full((1, SSD_XBC)), full((1, LANE)), full((1, LANE)),
                  full((1, SSD_D_INNER)), full((1, SSD_D_INNER)),
                  full((LANE, SSD_D_INNER)), full((L, L))],
        out_specs=pl.BlockSpec((nb, L, SSD_D_INNER), lambda b, c: (b, c, 0)),
        scratch_shapes=[pltpu.VMEM((nb, SSD_HEADS // 2, SSD_STATE, 2 * SSD_HEAD_DIM), F32)],
        compiler_params=_cparams(("parallel", "arbitrary")),
        name="ssd",
    )(proj3d, proj3d, proj3d, proj3d, conv_w, conv_b.reshape(1, SSD_XBC), pad16(dt_bias), pad16(a_log),
      rep64(d_skip), norm_g.reshape(1, SSD_D_INNER),
      jnp.asarray(e16, BF16), jnp.asarray(tril, BF16))
    return out.reshape(bsz * s, SSD_D_INNER)


def _mix_kernel(up_ref, upprev_ref, nsa_ref, ssd_ref, ga_ref, gb_ref, gc_ref, h_ref,
                pw_ref, ps_ref, wbp_ref, wbn_ref, wbs_ref, wo_ref, lg_ref, lb_ref, o_ref, *, ts):
    si = pl.program_id(1)
    w0 = MAX_POOL_WINDOW
    prev = jnp.where(si > 0, upprev_ref[...], 0.0)
    xcat = jnp.concatenate([prev, up_ref[...]], axis=0)
    tpos = si * ts + lax.broadcasted_iota(jnp.int32, (ts, POOL_GROUP), 0)
    ygs = []
    for gi, w in enumerate(POOL_WINDOWS):
        x = xcat[:, gi * POOL_GROUP:(gi + 1) * POOL_GROUP]
        acc = x
        d = 1
        while d < w:
            acc = acc + pltpu.roll(acc, d, axis=0)
            d *= 2
        cnt = jnp.minimum(tpos + 1, w).astype(F32)
        r = acc[w0:, :] / cnt - x[w0:, :]
        ygs.append(_dot(r.astype(BF16), pw_ref[gi]))
    y = jnp.concatenate(ygs, axis=1) * ps_ref[...]
    br_a = _dot(y.astype(BF16), wbp_ref[...])
    br_b = _dot(nsa_ref[...], wbn_ref[...])
    br_c = _dot(ssd_ref[...], wbs_ref[...])
    mixed = (jax.nn.sigmoid(ga_ref[...]) * br_a + jax.nn.sigmoid(gb_ref[...]) * br_b
             + jax.nn.sigmoid(gc_ref[...]) * br_c)
    mix = _dot(mixed.astype(BF16), wo_ref[...])
    _store_rows(o_ref, _layer_norm(DN_ALPHA * _load_rows(h_ref) + mix, lg_ref[...], lb_ref[...]))


def _mix(proj2d, nsa_out, ssd_y, h3, bsz, s, pool_w, pool_scale, w_br_pool, w_br_nsa, w_br_ssd,
         w_out, ln_g, ln_b):
    ts = MIX_TS
    ns = s // ts
    t = bsz * s
    d = D_MODEL
    w0 = MAX_POOL_WINDOW
    wbn = w_br_nsa.reshape(NSA_KV_HEADS, NSA_HPG, NSA_HEAD_DIM, d).transpose(1, 0, 2, 3).reshape(NSA_WIDTH, d)
    full = lambda shape: pl.BlockSpec(shape, lambda b, si: (0,) * len(shape), pipeline_mode=pl.Buffered(1))
    row = lambda width, cb: pl.BlockSpec((ts, width), lambda b, si: (b * ns + si, cb))
    return pl.pallas_call(
        functools.partial(_mix_kernel, ts=ts),
        out_shape=jax.ShapeDtypeStruct((t,) + ROW_TILE, F32),
        grid=(bsz, ns),
        in_specs=[row(POOL_WIDTH, C_POOL // POOL_WIDTH),
                  pl.BlockSpec((w0, POOL_WIDTH),
                               lambda b, si: (jnp.maximum((b * s + si * ts) // w0 - 1, 0), C_POOL // POOL_WIDTH)),
                  row(NSA_WIDTH, 0), row(d, 0),
                  row(d, C_GATE // d), row(d, C_GATE // d + 1), row(d, C_GATE // d + 2),
                  _row_spec(ts, lambda b, si: b * ns + si),
                  full((len(POOL_WINDOWS), POOL_GROUP, POOL_GROUP)), full((1, POOL_WIDTH)),
                  full((POOL_WIDTH, d)), full((NSA_WIDTH, d)), full((d, d)), full((d, d)),
                  full((1, d)), full((1, d))],
        out_specs=_row_spec(ts, lambda b, si: b * ns + si),
        compiler_params=_cparams(("parallel", "arbitrary")),
        name="mix",
    )(proj2d, proj2d, nsa_out, ssd_y, proj2d, proj2d, proj2d, h3,
      pool_w.astype(BF16), pool_scale.reshape(1, POOL_WIDTH), w_br_pool.astype(BF16), wbn.astype(BF16),
      w_br_ssd.astype(BF16), w_out.astype(BF16), ln_g.reshape(1, d), ln_b.reshape(1, d))


RANK_CHUNK = 256
BUCKET_ROWS = 32


def _route(hb, first_step, rw_ref, rb_ref, tri_ref, bucket_ref, wlo_ref, whi_ref, rank_ref, cnt_ref, run_ref):
    @pl.when(first_step)
    def _():
        run_ref[...] = jnp.zeros(run_ref.shape, F32)

    logits = _dot_nt(rw_ref[...], hb)
    lg = [logits[e:e + 1, :] for e in range(N_EXPERTS)]
    mx = lg[0]
    for e in range(1, N_EXPERTS):
        mx = jnp.maximum(mx, lg[e])
    ex = [jnp.exp(v - mx) for v in lg]
    den = ex[0]
    for e in range(1, N_EXPERTS):
        den = den + ex[e]
    probs = [v / den for v in ex]
    score = [probs[e] + rb_ref[e:e + 1, :] for e in range(N_EXPERTS)]

    def group_vals(vals, grp):
        out = []
        for k in range(EXPERTS_PER_GROUP):
            v = vals[k]
            for gi in range(1, N_EXPERT_GROUPS):
                v = jnp.where(grp == gi, vals[gi * EXPERTS_PER_GROUP + k], v)
            out.append(v)
        return out

    gscore = []
    for gi in range(N_EXPERT_GROUPS):
        sc = score[gi * EXPERTS_PER_GROUP:(gi + 1) * EXPERTS_PER_GROUP]
        best = None
        for a, b in PAIRS:
            v = sc[a] + sc[b]
            best = v if best is None else jnp.maximum(best, v)
        gscore.append(best)
    grp = jnp.zeros(gscore[0].shape, jnp.int32)
    best = gscore[0]
    for gi in range(1, N_EXPERT_GROUPS):
        better = gscore[gi] > best
        grp = jnp.where(better, gi, grp)
        best = jnp.where(better, gscore[gi], best)

    sc = group_vals(score, grp)
    pr = group_vals(probs, grp)
    first = jnp.zeros(grp.shape, jnp.int32)
    fv = sc[0]
    for k in range(1, EXPERTS_PER_GROUP):
        better = sc[k] > fv
        first = jnp.where(better, k, first)
        fv = jnp.where(better, sc[k], fv)
    second = jnp.full(grp.shape, -1, jnp.int32)
    sv = jnp.full(fv.shape, -jnp.inf, F32)
    for k in range(EXPERTS_PER_GROUP):
        better = (first != k) & ((sc[k] > sv) | (second < 0))
        second = jnp.where(better, k, second)
        sv = jnp.where(better, sc[k], sv)
    lo = jnp.minimum(first, second)
    hi = jnp.maximum(first, second)
    p_lo = pr[0]
    p_hi = pr[0]
    for k in range(1, EXPERTS_PER_GROUP):
        p_lo = jnp.where(lo == k, pr[k], p_lo)
        p_hi = jnp.where(hi == k, pr[k], p_hi)
    tot = p_lo + p_hi
    pair = jnp.zeros(grp.shape, jnp.int32)
    for pi, (a, b) in enumerate(PAIRS):
        pair = jnp.where((lo == a) & (hi == b), pi, pair)
    bucket = grp * len(PAIRS) + pair
    bucket_ref[...] = bucket
    wlo_ref[...] = p_lo / tot
    whi_ref[...] = p_hi / tot

    tm = bucket.shape[1]
    onehot = jnp.where(lax.broadcasted_iota(jnp.int32, (BUCKET_ROWS, tm), 0) == bucket, 1.0, 0.0)
    run = run_ref[...]
    ranks = []
    for c in range(tm // RANK_CHUNK):
        oc = onehot[:, c * RANK_CHUNK:(c + 1) * RANK_CHUNK]
        before = _dot(oc.astype(BF16), tri_ref[...]) + jnp.concatenate([run] * (RANK_CHUNK // LANE), axis=1)
        ranks.append(jnp.sum(oc * before, axis=0, keepdims=True))
        run = run + jnp.broadcast_to(jnp.sum(oc, axis=1, keepdims=True), run.shape)
    rank_ref[...] = jnp.concatenate(ranks, axis=1).astype(jnp.int32)
    run_ref[...] = run
    cnt_ref[...] = run


ROUTER_TM = 2048


def _router_kernel(h_ref, *refs):
    _route(_load_rows(h_ref).astype(BF16), pl.program_id(0) == 0, *refs)


def _router(h2d, router_w, router_b):
    t, d = h2d.shape[0], D_MODEL
    tm = ROUTER_TM
    rw = jnp.pad(router_w.T, ((0, LANE - N_EXPERTS), (0, 0))).astype(BF16)
    rb = jnp.broadcast_to(jnp.pad(router_b, (0, LANE - N_EXPERTS))[:, None], (LANE, tm))
    tri = jnp.asarray(np.triu(np.ones((RANK_CHUNK, RANK_CHUNK), np.float32), 1), BF16)
    vec = lambda dt: jax.ShapeDtypeStruct((1, t), dt)
    row = pl.BlockSpec((1, tm), lambda i: (0, i))
    return pl.pallas_call(
        _router_kernel,
        out_shape=(vec(jnp.int32), vec(F32), vec(F32), vec(jnp.int32),
                   jax.ShapeDtypeStruct((BUCKET_ROWS, LANE), F32)),
        grid=(t // tm,),
        in_specs=[_row_spec(tm, lambda i: i),
                  pl.BlockSpec((LANE, d), lambda i: (0, 0)),
                  pl.BlockSpec((LANE, tm), lambda i: (0, 0)),
                  pl.BlockSpec((RANK_CHUNK, RANK_CHUNK), lambda i: (0, 0))],
        out_specs=(row, row, row, row, pl.BlockSpec((BUCKET_ROWS, LANE), lambda i: (0, 0))),
        scratch_shapes=[pltpu.VMEM((BUCKET_ROWS, LANE), F32)],
        compiler_params=_cparams(("arbitrary",)),
        name="router",
    )(h2d, rw, rb, tri)


def _moe_plan(bucket, rank, counts, w_lo, w_hi, t):
    tm = MOE_TM
    n_tiles = t // tm + N_BUCKETS
    p_rows = n_tiles * tm
    tiles_per = (counts + tm - 1) // tm
    tile_end = jnp.cumsum(tiles_per)
    tile_start = tile_end - tiles_per
    in_bucket = bucket[:, None] == jnp.arange(N_BUCKETS)[None, :]
    dest = jnp.sum(jnp.where(in_bucket, (tile_start * tm)[None, :], 0), axis=1) + rank
    payload = jnp.stack([jnp.arange(t, dtype=jnp.int32), lax.bitcast_convert_type(w_lo, jnp.int32),
                         lax.bitcast_convert_type(w_hi, jnp.int32), jnp.ones((t,), jnp.int32)], axis=1)
    plan = jnp.zeros((p_rows, 4), jnp.int32).at[dest].set(payload, unique_indices=True,
                                                          mode="promise_in_bounds")
    src = plan[:, 0]
    wl = lax.bitcast_convert_type(plan[:, 1], F32)
    wh = lax.bitcast_convert_type(plan[:, 2], F32)
    prow = jnp.arange(p_rows, dtype=jnp.int32)
    spare = t + ((prow // tm) % 2) * tm + prow % tm
    dst = jnp.where(plan[:, 3] > 0, src, spare)
    n_used = tile_end[-1]
    tile_ids = jnp.arange(n_tiles)
    tb = jnp.sum((tile_end[None, :] <= tile_ids[:, None]).astype(jnp.int32), axis=1)
    tb = jnp.where(tile_ids < n_used, tb, tb[jnp.maximum(n_used - 1, 0)])
    tb = jnp.minimum(tb, N_BUCKETS - 1)
    pairs = np.asarray(PAIRS, np.int32)
    grp = tb // len(PAIRS)
    ea = grp * EXPERTS_PER_GROUP + jnp.asarray(pairs[:, 0])[tb % len(PAIRS)]
    eb = grp * EXPERTS_PER_GROUP + jnp.asarray(pairs[:, 1])[tb % len(PAIRS)]
    nvalid = jnp.clip(counts[tb] - (tile_ids - tile_start[tb]) * tm, 0, tm)
    nvalid = jnp.where(tile_ids < n_used, nvalid, 0).astype(jnp.int32)
    return (src.reshape(n_tiles, 1, tm), dst.reshape(n_tiles, 1, tm), wl.reshape(p_rows, 1), wh.reshape(p_rows, 1),
            ea.astype(jnp.int32), eb.astype(jnp.int32), nvalid, n_tiles)


MOE_DMA_GROUPS = 8


def _moe_kernel(ea_ref, eb_ref, nv_ref, src0_ref, srcn_ref, dstp_ref, wl_ref, wh_ref, h_hbm,
                w1a_ref, w3a_ref, w2a_ref, w1b_ref, w3b_ref, w2b_ref, lg_ref, lb_ref,
                o_hbm, xbuf0, xbuf1, obuf0, obuf1, wa1, wa3, wa2, wb1, wb3, wb2, gsem, ssem):
    i = pl.program_id(0)
    tm = xbuf0.shape[0]
    xbufs = (xbuf0, xbuf1)
    obufs = (obuf0, obuf1)
    nv = nv_ref[i]
    nv_p1 = nv_ref[jnp.maximum(i - 1, 0)]
    odd = lax.rem(i, 2) == 1
    even = jnp.logical_not(odd)

    def gather(r, p, row):
        return pltpu.make_async_copy(h_hbm.at[pl.ds(row, 1)], xbufs[p].at[pl.ds(r, 1)], gsem.at[p])

    def scatter(r, p, row):
        return pltpu.make_async_copy(obufs[p].at[pl.ds(r, 1)], o_hbm.at[pl.ds(row, 1)], ssem.at[p])

    def wait_gather(p):
        for r in range(tm):
            gather(r, p, 0).wait()

    def wait_scatter(p):
        for r in range(tm):
            scatter(r, p, 0).wait()

    def tile(p, age):
        q = 1 - p
        half = MOE_DMA_GROUPS // 2
        per = tm // half

        def issue(k):
            for r in range((k % half) * per, (k % half + 1) * per):
                if k < half:
                    gather(r, q, srcn_ref[0, 0, r]).start()
                elif age >= 1:
                    scatter(r, q, dstp_ref[0, 0, r]).start()

        wait_gather(p)
        x = _load_rows(xbufs[p])
        xb = x.astype(BF16)
        issue(0)
        h1 = _dot(xb, wa1[...])
        issue(1)
        h3 = _dot(xb, wa3[...])
        issue(2)
        ya = _dot((jax.nn.silu(h1) * h3).astype(BF16), wa2[...])
        issue(3)
        h1 = _dot(xb, wb1[...])
        issue(4)
        h3 = _dot(xb, wb3[...])
        issue(5)
        yb = _dot((jax.nn.silu(h1) * h3).astype(BF16), wb2[...])
        issue(6)
        y = wl_ref[...] * ya + wh_ref[...] * yb
        issue(7)
        out = _layer_norm(DN_ALPHA * x + y, lg_ref[...], lb_ref[...])
        if age >= 2:
            wait_scatter(p)
        _store_rows(obufs[p], out)

    @pl.when(i == 0)
    def _():
        obuf1[...] = jnp.zeros(obuf1.shape, F32)
        n_real = o_hbm.shape[0] - 2 * tm
        for blk in range(2):
            cp = pltpu.make_async_copy(obuf1, o_hbm.at[pl.ds(n_real + blk * tm, tm)], ssem.at[1])
            cp.start()
            cp.wait()

    prev = jnp.maximum(i - 1, 0)

    @pl.when((nv > 0) & ((i == 0) | (ea_ref[i] != ea_ref[prev])))
    def _():
        wa1[...] = w1a_ref[0, 0].astype(BF16)
        wa3[...] = w3a_ref[0, 0].astype(BF16)
        wa2[...] = w2a_ref[0, 0].astype(BF16)

    @pl.when((nv > 0) & ((i == 0) | (eb_ref[i] != eb_ref[prev])))
    def _():
        wb1[...] = w1b_ref[0, 0].astype(BF16)
        wb3[...] = w3b_ref[0, 0].astype(BF16)
        wb2[...] = w2b_ref[0, 0].astype(BF16)

    @pl.when((nv > 0) & (i == 0))
    def _():
        def start0(r, carry):
            gather(r, 0, src0_ref[0, 0, r]).start()
            return carry
        lax.fori_loop(0, tm, start0, 0)
        tile(0, 0)

    pl.when((nv > 0) & (i == 1))(functools.partial(tile, 1, 1))
    pl.when((nv > 0) & (i >= 2) & even)(functools.partial(tile, 0, 2))
    pl.when((nv > 0) & (i >= 2) & odd)(functools.partial(tile, 1, 2))

    def tail(p):
        q = 1 - p
        wait_gather(p)

        @pl.when(i >= 2)
        def _():
            wait_scatter(p)

        def start(r, carry):
            scatter(r, q, dstp_ref[0, 0, r]).start()
            return carry
        lax.fori_loop(0, tm, start, 0)
        wait_scatter(q)

    is_tail = (nv == 0) & (i >= 1) & (nv_p1 > 0)
    pl.when(is_tail & even)(functools.partial(tail, 0))
    pl.when(is_tail & odd)(functools.partial(tail, 1))


def _moe(h3, bucket, rank, counts, w_lo, w_hi, w1, w3, w2, layer, ln_g, ln_b):
    t, d = h3.shape[0], D_MODEL
    tm = MOE_TM
    assert tm % MOE_DMA_GROUPS == 0
    src, dst, wl, wh, ea, eb, nvalid, n_tiles = _moe_plan(bucket, rank, counts, w_lo, w_hi, t)
    wspec = lambda shape, which: pl.BlockSpec(
        (1,) + shape, (lambda i, ea_r, eb_r, nv_r: (layer, ea_r[i], 0, 0)) if which == 0
        else (lambda i, ea_r, eb_r, nv_r: (layer, eb_r[i], 0, 0)))
    w13 = (1, d, D_EXPERT)
    w2s = (1, D_EXPERT, d)
    smem_tile = lambda fn: pl.BlockSpec((1, 1, tm), lambda i, *_: (fn(i), 0, 0), memory_space=pltpu.SMEM)
    grid_spec = pltpu.PrefetchScalarGridSpec(
        num_scalar_prefetch=3,
        grid=(n_tiles,),
        in_specs=[smem_tile(lambda i: 0),
                  smem_tile(lambda i: jnp.minimum(i + 1, n_tiles - 1)),
                  smem_tile(lambda i: jnp.maximum(i - 1, 0)),
                  pl.BlockSpec((tm, 1), lambda i, *_: (i, 0)),
                  pl.BlockSpec((tm, 1), lambda i, *_: (i, 0)),
                  pl.BlockSpec(memory_space=pl.ANY),
                  wspec(w13, 0), wspec(w13, 0), wspec(w2s, 0),
                  wspec(w13, 1), wspec(w13, 1), wspec(w2s, 1),
                  pl.BlockSpec((1, d), lambda i, *_: (0, 0)),
                  pl.BlockSpec((1, d), lambda i, *_: (0, 0))],
        out_specs=pl.BlockSpec(memory_space=pl.ANY),
        scratch_shapes=[pltpu.VMEM((tm,) + ROW_TILE, F32)] * 4
        + [pltpu.VMEM(w13[1:], BF16), pltpu.VMEM(w13[1:], BF16), pltpu.VMEM(w2s[1:], BF16)] * 2
        + [pltpu.SemaphoreType.DMA((2,)), pltpu.SemaphoreType.DMA((2,))],
    )
    return pl.pallas_call(
        _moe_kernel,
        out_shape=jax.ShapeDtypeStruct((t + 2 * tm,) + ROW_TILE, F32),
        grid_spec=grid_spec,
        compiler_params=_cparams(("arbitrary",)),
        name="moe",
    )(ea, eb, nvalid, src, src, dst, wl, wh, h3, w1, w3, w2, w1, w3, w2,
      ln_g.reshape(1, d), ln_b.reshape(1, d))


def kernel(x, ln0_g, ln0_b, w_in, pool_w, pool_scale, cmp_k_pe, cmp_k_w1, cmp_k_w2, cmp_v_pe, cmp_v_w1, cmp_v_w2, rel_bias, conv_w, conv_b, dt_bias, a_log, d_skip, ssd_norm_g, w_br_pool, w_br_nsa, w_br_ssd, w_out, ln1_g, ln1_b, router_w, router_b, exp_w1, exp_w3, exp_w2, ln2_g, ln2_b):
    bsz, s, d = x.shape
    assert d == D_MODEL and s % MIX_TS == 0 and s // CMP_STRIDE == LANE and (bsz * s) % PROJ_TM == 0
    t = bsz * s
    ts, ws, bc = _nsa_bias_tables(rel_bias)
    et, ovt = _nsa_consts(s)
    h = x.reshape(t, d)
    for i in range(DEPTH):
        if i == 0:
            proj, h = _proj(h, _prep_w_in(w_in[i]), t, ln=(ln0_g, ln0_b))
        else:
            proj = _proj(h, _prep_w_in(w_in[i]), t)
        kc, vc = _compress(proj, bsz, s, cmp_k_pe[i], cmp_k_w1[i], cmp_k_w2[i],
                           cmp_v_pe[i], cmp_v_w1[i], cmp_v_w2[i])
        nsa_out = _nsa(proj, kc, vc, ts, ws, bc, et, ovt, bsz, s)
        ssd_y = _ssd(proj, bsz, s, conv_w[i], conv_b[i], dt_bias[i], a_log[i], d_skip[i], ssd_norm_g[i])
        h1 = _mix(proj, nsa_out, ssd_y, h, bsz, s, pool_w[i], pool_scale[i], w_br_pool[i], w_br_nsa[i],
                  w_br_ssd[i], w_out[i], ln1_g[i], ln1_b[i])
        bucket, w_lo, w_hi, rank, cnt = _router(h1, router_w, router_b)
        counts = cnt[:N_BUCKETS, 0].astype(jnp.int32)
        h = _moe(h1, bucket[0], rank[0], counts, w_lo[0], w_hi[0], exp_w1, exp_w3, exp_w2, i,
                 ln2_g[i], ln2_b[i])
    return h[:t].reshape(bsz, s, d)
```

```python
import functools
import math

import numpy as np
import jax
import jax.numpy as jnp
from jax import lax
from jax.experimental import pallas as pl
from jax.experimental.pallas import tpu as pltpu

F32 = jnp.float32
BF16 = jnp.bfloat16

D_MODEL = 1024
DEPTH = 2
DN_ALPHA = (2.0 * DEPTH) ** 0.25
LN_EPS = 1e-5
NEG = -1e30
BIG = 1e6

POOL_WINDOWS = (2, 4, 8, 16)
POOL_GROUP = 128
POOL_WIDTH = 512
MAX_POOL_WINDOW = 16

NSA_HEAD_DIM = 64
NSA_KV_HEADS = 2
NSA_HPG = 4
NSA_HEADS = 8
NSA_WIDTH = 512
CMP_LEN = 32
CMP_STRIDE = 16
CMP_HIDDEN = 128
SLC_LEN = 64
SLC_TOPN = 8
WIN_LEN = 512
Q_BLOCK = 128
REL_BUCKETS = 32
REL_MAX_DIST = 128

SSD_D_INNER = 1024
SSD_HEAD_DIM = 64
SSD_HEADS = 16
SSD_GROUPS = 4
SSD_STATE = 128
SSD_CONV = 4
SSD_CHUNK = 128
SSD_XBC = 2048

N_EXPERTS = 16
N_EXPERT_GROUPS = 4
EXPERTS_PER_GROUP = 4
D_EXPERT = 512
PAIRS = ((0, 1), (0, 2), (0, 3), (1, 2), (1, 3), (2, 3))
N_BUCKETS = N_EXPERT_GROUPS * len(PAIRS)

LANE = 128
SUBLANE = 8
VMEM_LIMIT = 48 * 1024 * 1024

C_POOL = 0
C_Q = 512
C_KV = 1024
C_NG = 1792
C_DT = 1920
C_XBC = 2048
C_Z = 4096
C_GATE = 5120
D_PAD = 8192

PROJ_TM = 1024
PROJ_TN = 2048
MIX_TS = 512
MOE_TM = 256


def _dot(a, b):
    return jnp.dot(a, b, preferred_element_type=F32)


def _dot_nt(a, b):
    return lax.dot_general(a, b, (((1,), (1,)), ((), ())), preferred_element_type=F32)


def _dot_tn(a, b):
    return lax.dot_general(a, b, (((0,), (0,)), ((), ())), preferred_element_type=F32)


def _split3(x):
    x1 = x.astype(BF16)
    r1 = x - x1.astype(F32)
    x2 = r1.astype(BF16)
    x3 = (r1 - x2.astype(F32)).astype(BF16)
    return x1, x2, x3


def _layer_norm(x, g, b):
    mu = jnp.mean(x, -1, keepdims=True)
    xc = x - mu
    var = jnp.mean(xc * xc, -1, keepdims=True)
    return xc * lax.rsqrt(var + LN_EPS) * g + b


def _cparams(sem):
    return pltpu.CompilerParams(dimension_semantics=sem, vmem_limit_bytes=VMEM_LIMIT)


ROW_TILE = (D_MODEL,)


def _load_rows(ref):
    return ref[...]


def _store_rows(ref, val):
    ref[...] = val


def _row_spec(n, index):
    return pl.BlockSpec((n,) + ROW_TILE, lambda *ids: (index(*ids), 0))


def _proj_kernel(h_ref, w_ref, o_ref, hb_ref):
    @pl.when(pl.program_id(1) == 0)
    def _():
        hb_ref[...] = _load_rows(h_ref).astype(BF16)

    o_ref[...] = _dot(hb_ref[...], w_ref[...])


def _ln_proj_kernel(x_ref, g_ref, b_ref, w_ref, o_ref, h_ref, hb_ref):
    @pl.when(pl.program_id(1) == 0)
    def _():
        h = _layer_norm(x_ref[...], g_ref[...], b_ref[...])
        _store_rows(h_ref, h)
        hb_ref[...] = h.astype(BF16)

    o_ref[...] = _dot(hb_ref[...], w_ref[...])


def _proj(h2d, w_pad, t, ln=None):
    d = D_MODEL
    grid = (t // PROJ_TM, D_PAD // PROJ_TN)
    w_spec = pl.BlockSpec((d, PROJ_TN), lambda i, j: (0, j))
    o_spec = pl.BlockSpec((PROJ_TM, PROJ_TN), lambda i, j: (i, j))
    o_shape = jax.ShapeDtypeStruct((t, D_PAD), F32)
    scratch = [pltpu.VMEM((PROJ_TM, d), BF16)]
    params = _cparams(("parallel", "arbitrary"))
    if ln is None:
        return pl.pallas_call(
            _proj_kernel, out_shape=o_shape, grid=grid,
            in_specs=[_row_spec(PROJ_TM, lambda i, j: i), w_spec], out_specs=o_spec,
            scratch_shapes=scratch, compiler_params=params, name="proj",
        )(h2d, w_pad)
    vec = pl.BlockSpec((1, d), lambda i, j: (0, 0))
    return pl.pallas_call(
        _ln_proj_kernel, out_shape=(o_shape, jax.ShapeDtypeStruct((t,) + ROW_TILE, F32)), grid=grid,
        in_specs=[_row_spec(PROJ_TM, lambda i, j: i), vec, vec, w_spec],
        out_specs=(o_spec, _row_spec(PROJ_TM, lambda i, j: i)),
        scratch_shapes=scratch, compiler_params=params, name="ln_proj",
    )(h2d, ln[0].reshape(1, d), ln[1].reshape(1, d), w_pad)


def _pad_cols(w, n):
    return jnp.pad(w, ((0, 0), (0, n - w.shape[1])))


def _prep_w_in(w_in):
    d = w_in.shape[0]
    o = 0
    w_pool = w_in[:, o:o + POOL_WIDTH]; o += POOL_WIDTH
    w_q = w_in[:, o:o + NSA_WIDTH]; o += NSA_WIDTH
    w_kv = w_in[:, o:o + 768]; o += 768
    w_ng = w_in[:, o:o + 24]; o += 24
    w_z = w_in[:, o:o + SSD_D_INNER]; o += SSD_D_INNER
    w_xbc = w_in[:, o:o + SSD_XBC]; o += SSD_XBC
    w_dt = w_in[:, o:o + SSD_HEADS]; o += SSD_HEADS
    w_gate = w_in[:, o:o + 3 * D_MODEL]
    w_q = w_q.reshape(d, NSA_KV_HEADS, NSA_HPG, NSA_HEAD_DIM).transpose(0, 2, 1, 3).reshape(d, NSA_WIDTH)
    w = jnp.concatenate([w_pool, w_q, w_kv, _pad_cols(w_ng, LANE), _pad_cols(w_dt, LANE),
                         w_xbc, w_z, w_gate], axis=1)
    return w.astype(BF16)


def _compress_one(tok_ref, pe_ref, w1_ref, w2_ref):
    n16 = tok_ref.shape[0] // CMP_STRIDE
    acc_a = jnp.zeros((n16, 2 * CMP_HIDDEN), F32)
    acc_b = jnp.zeros((n16, 2 * CMP_HIDDEN), F32)
    for l in range(CMP_STRIDE):
        rows = tok_ref[pl.ds(l, n16, stride=CMP_STRIDE), :]
        xa = (rows + pe_ref[l:l + 1, :]).astype(BF16)
        xb = (rows + pe_ref[CMP_STRIDE + l:CMP_STRIDE + l + 1, :]).astype(BF16)
        acc_a = acc_a + _dot(xa, w1_ref[l])
        acc_b = acc_b + _dot(xb, w1_ref[CMP_STRIDE + l])
    hid = acc_a + pltpu.roll(acc_b, n16 - 1, axis=0)
    hid = jax.nn.gelu(hid)
    return _dot(hid.astype(BF16), w2_ref[...])


def _compress_kernel(kt_ref, vt_ref, pek_ref, w1k_ref, w2k_ref, pev_ref, w1v_ref, w2v_ref,
                     kc_ref, vc_ref):
    kc_ref[0] = _compress_one(kt_ref, pek_ref, w1k_ref, w2k_ref).astype(BF16)
    vc_ref[0] = _compress_one(vt_ref, pev_ref, w1v_ref, w2v_ref).T.astype(BF16)


def _blockdiag2(w):
    z = jnp.zeros_like(w)
    return jnp.concatenate([jnp.concatenate([w, z], -1), jnp.concatenate([z, w], -1)], -2)


def _compress(proj2d, bsz, s, pe_k, w1_k, w2_k, pe_v, w1_v, w2_v):
    n16 = s // CMP_STRIDE

    def prep(pe, w1, w2):
        pe2 = jnp.concatenate([pe, pe], axis=-1)
        w1b = _blockdiag2(w1.reshape(CMP_LEN, NSA_HEAD_DIM, CMP_HIDDEN)).astype(BF16)
        w2b = _blockdiag2(w2).astype(BF16)
        return pe2, w1b, w2b

    pk = prep(pe_k, w1_k, w2_k)
    pv = prep(pe_v, w1_v, w2_v)
    full = lambda shape: pl.BlockSpec(shape, lambda b: (0,) * len(shape))
    wspecs = [full((CMP_LEN, LANE)), full((CMP_LEN, LANE, 2 * CMP_HIDDEN)), full((2 * CMP_HIDDEN, LANE))]
    return pl.pallas_call(
        _compress_kernel,
        out_shape=(jax.ShapeDtypeStruct((bsz, n16, LANE), BF16),
                   jax.ShapeDtypeStruct((bsz, n16, LANE), BF16)),
        grid=(bsz,),
        in_specs=[pl.BlockSpec((s, LANE), lambda b: (b, C_KV // LANE)),
                  pl.BlockSpec((s, LANE), lambda b: (b, C_KV // LANE + 1))] + wspecs + wspecs,
        out_specs=(pl.BlockSpec((1, n16, LANE), lambda b: (b, 0, 0)),
                   pl.BlockSpec((1, n16, LANE), lambda b: (b, 0, 0))),
        compiler_params=_cparams(("parallel",)),
        name="nsa_compress",
    )(proj2d, proj2d, *pk, *pv)


def _rel_bucket(dist):
    n = jnp.maximum(dist, 0)
    max_exact = REL_BUCKETS // 2
    nf = jnp.maximum(n, 1).astype(F32)
    large = max_exact + (jnp.log(nf / max_exact) / math.log(REL_MAX_DIST / max_exact)
                         * (REL_BUCKETS - max_exact)).astype(jnp.int32)
    large = jnp.minimum(large, REL_BUCKETS - 1)
    return jnp.where(n < max_exact, n, large)


LOG2E = math.log2(math.e)
WIN_BACK = WIN_LEN // Q_BLOCK
WIN_KEYS = (WIN_BACK + 1) * Q_BLOCK
WIN_STRIP_TILES = 2 * WIN_BACK + 1
SEL_TILES = 4
SEL_KEYS = SEL_TILES * Q_BLOCK
SEL_STRIP_TILES = 3 * SEL_TILES - 1
CMP_BIAS_ROWS = 2 * LANE
PV_ROWS = NSA_HEAD_DIM + 16
NSA_BATCH = 2


def _nsa_bias_tables(rel_bias):
    tbl = rel_bias.astype(F32)
    far = tbl[REL_BUCKETS - 1]
    hq = NSA_HPG * Q_BLOCK

    def lookup(dist):
        onehot = (_rel_bucket(jnp.asarray(dist))[..., None] == jnp.arange(REL_BUCKETS)).astype(F32)
        return jnp.einsum("...b,bh->...h", onehot, tbl, precision=lax.Precision.HIGHEST)

    k = np.arange(Q_BLOCK)[None, :, None]
    q = np.arange(Q_BLOCK)[None, None, :]

    def strip(first, last, visible):
        n = first - last + 1
        dist = np.arange(first, last - 1, -1)[:, None, None] * Q_BLOCK + q - k
        v = jnp.where(jnp.asarray(visible(dist))[..., None], (lookup(dist) - far) * LOG2E, NEG)
        v = v.reshape(n, Q_BLOCK, Q_BLOCK, NSA_KV_HEADS, NSA_HPG).transpose(3, 0, 1, 4, 2)
        return v.reshape(NSA_KV_HEADS, n * Q_BLOCK, hq)

    ts = strip(2 * SEL_TILES - 1, 1 - SEL_TILES, lambda d: d >= 0)
    ws = strip(WIN_BACK, -WIN_BACK, lambda d: (d >= 0) & (d < WIN_LEN))
    r = np.arange(CMP_BIAS_ROWS)[:, None]
    dist_c = q[0] - CMP_STRIDE * (r - LANE) - (CMP_LEN - 1)
    bc = jnp.where(jnp.asarray(dist_c >= 0)[..., None], lookup(dist_c) * LOG2E, NEG)
    bc = bc.reshape(CMP_BIAS_ROWS, Q_BLOCK, NSA_KV_HEADS, NSA_HPG).transpose(2, 0, 3, 1)
    bc = bc.reshape(NSA_KV_HEADS, CMP_BIAS_ROWS, hq)
    return ts, ws, bc


def _nsa_consts(s):
    n_slc = s // SLC_LEN
    et = (np.arange(s)[:, None] // SLC_LEN == np.arange(LANE)[None, :]).astype(np.float32)
    n_cmp = (s - CMP_LEN) // CMP_STRIDE + 1
    c0 = np.arange(n_cmp) * CMP_STRIDE
    s0 = np.arange(n_slc) * SLC_LEN
    ov = np.clip(np.minimum(c0[None, :] + CMP_LEN, s0[:, None] + SLC_LEN)
                 - np.maximum(c0[None, :], s0[:, None]), 0, None) / CMP_LEN
    ovt = np.zeros((n_slc, LANE), np.float32)
    ovt[:, :n_cmp] = ov
    return jnp.asarray(et, BF16), jnp.asarray(ovt, BF16)


def _nsa_kernel(q_ref, ng_ref, ks_ref, vs_ref, kw_ref, vw_ref, kc_ref, vct_ref,
                ts_ref, ws_ref, bc_ref, et_ref, ovt_ref, o_ref,
                ksb_ref, kwb_ref, vst_ref, vwt_ref, *, nq, n_slc):
    qi = pl.program_id(1)
    hq = NSA_HPG * Q_BLOCK
    dh = NSA_HEAD_DIM
    groups = range(NSA_KV_HEADS)
    gsl = [slice(g * dh, (g + 1) * dh) for g in groups]

    seqs = range(q_ref.shape[0])

    @pl.when(qi == 0)
    def _():
        ones_rows = jnp.where(lax.broadcasted_iota(jnp.int32, (PV_ROWS - dh, ks_ref.shape[1]), 0) == 0, 1.0, 0.0)
        for u in seqs:
            ksb_ref[u, :, 0:LANE] = ks_ref[u].astype(BF16)
            ksb_ref[u, :, LANE:] = et_ref[...]
            kwb_ref[u] = kw_ref[u].astype(BF16)
            for g in groups:
                vst_ref[u, g, dh:, :] = ones_rows.astype(BF16)
                vwt_ref[u, g, dh:, :] = ones_rows.astype(BF16)
            for kt in range(nq):
                sl = slice(kt * Q_BLOCK, (kt + 1) * Q_BLOCK)
                vs_t = vs_ref[u, sl, :].T
                vw_t = vw_ref[u, sl, :].T
                for g in groups:
                    vst_ref[u, g, 0:dh, sl] = vs_t[gsl[g], :].astype(BF16)
                    vwt_ref[u, g, 0:dh, sl] = vw_t[gsl[g], :].astype(BF16)

    def steps(n):
        for u in seqs:
            _nsa_step(q_ref.at[u], ng_ref.at[u], kc_ref.at[u], vct_ref.at[u], ts_ref, ws_ref, bc_ref, ovt_ref,
                      o_ref.at[u], ksb_ref.at[u], kwb_ref.at[u], vst_ref.at[u], vwt_ref.at[u],
                      n_slc=n_slc, n_sel=n)

    n_super = qi // SEL_TILES + 1
    for n in range(1, nq // SEL_TILES + 1):
        pl.when(n_super == n)(functools.partial(steps, n))


def _nsa_step(q_ref, ng_ref, kc_ref, vct_ref, ts_ref, ws_ref, bc_ref, ovt_ref, o_ref,
              ksb_ref, kwb_ref, vst_ref, vwt_ref, *, n_slc, n_sel):
    qi = pl.program_id(1)
    hq = NSA_HPG * Q_BLOCK
    dh = NSA_HEAD_DIM
    groups = range(NSA_KV_HEADS)
    gsl = [slice(g * dh, (g + 1) * dh) for g in groups]
    row_lo = lax.broadcasted_iota(jnp.int32, (LANE, hq), 0) < dh

    def heads4(m):
        return jnp.concatenate([m] * NSA_HPG, axis=1)

    gates_t = jax.nn.sigmoid(ng_ref[...]).T
    q_t = jnp.concatenate(
        [(q_ref[:, h * LANE:(h + 1) * LANE] * (dh ** -0.5 * LOG2E)).T for h in range(NSA_HPG)], axis=1)
    qg_t = [jnp.where(row_lo if g == 0 else jnp.logical_not(row_lo), q_t, 0.0).astype(BF16) for g in groups]

    boff = pl.multiple_of(LANE - SUBLANE * qi, SUBLANE)
    jrow = lax.broadcasted_iota(jnp.int32, (n_slc, LANE), 0)
    tj = qi * Q_BLOCK + lax.broadcasted_iota(jnp.int32, (n_slc, LANE), 1)
    jt = lax.shift_right_logical(tj, SLC_LEN.bit_length() - 1)
    forced = (jrow == 0) | (jrow == jt) | (jrow == jt - 1)
    valid = jrow * SLC_LEN <= tj
    o_c = []
    q_aug = []
    for g in groups:
        s = _dot(kc_ref[...], qg_t[g]) + bc_ref[g, pl.ds(boff, LANE), :]
        mx = jnp.max(s, axis=0, keepdims=True)
        p = jnp.exp2(s - mx)
        lsum = jnp.sum(p, axis=0, keepdims=True)
        pb = (p * jnp.where(mx > 0.5 * NEG, 1.0 / lsum, 0.0)).astype(BF16)
        o_c.append(_dot(vct_ref[gsl[g], :], pb))
        imp4 = _dot(ovt_ref[...], pb)
        imp = imp4[:, 0:LANE]
        for h in range(1, NSA_HPG):
            imp = imp + imp4[:, h * LANE:(h + 1) * LANE]
        val = jnp.where(forced, BIG, jnp.where(valid, imp, -BIG))
        rank = jnp.zeros((n_slc, LANE), F32)
        for i in range(n_slc):
            ci = val[i:i + 1, :]
            rank = rank + jnp.where(ci > val, 1.0, 0.0) + jnp.where(ci == val, (jrow > i).astype(F32), 0.0)
        block_rows = heads4(jnp.where(rank < SLC_TOPN, 0.0, NEG)).astype(BF16)
        q_aug.append(jnp.concatenate(
            [qg_t[g], block_rows, jnp.zeros((LANE - n_slc, hq), BF16)], axis=0))

    diag = qi - SEL_TILES * (n_sel - 1)
    nk = n_sel * SEL_KEYS
    lo = max(n_sel - 2, 0) * SEL_KEYS
    toff = pl.multiple_of((2 * SEL_TILES - 1 - diag) * Q_BLOCK - (nk - lo - SEL_KEYS), Q_BLOCK)
    o_s = []
    for g in groups:
        parts = [(lo, nk, _dot(ksb_ref[lo:nk, :], q_aug[g]) + ts_ref[g, pl.ds(toff, nk - lo), :])]
        if lo > 0:
            parts.append((0, lo, _dot(ksb_ref[0:lo, :], q_aug[g])))
        m = parts[0][2].max(axis=0, keepdims=True)
        for _, _, s in parts[1:]:
            m = jnp.maximum(m, s.max(axis=0, keepdims=True))
        acc = None
        for a, b, s in parts:
            pv = _dot(vst_ref[g, :, a:b], jnp.exp2(s - m).astype(BF16))
            acc = pv if acc is None else acc + pv
        o_s.append(acc[0:dh] * (1.0 / acc[dh:dh + 1]))

    w0 = jnp.maximum(qi - WIN_BACK, 0)
    woff = pl.multiple_of(w0 * Q_BLOCK, Q_BLOCK)
    boff_w = pl.multiple_of((WIN_BACK - (qi - w0)) * Q_BLOCK, Q_BLOCK)
    kw_bf = kwb_ref[pl.ds(woff, WIN_KEYS), :]
    o_w = []
    for g in groups:
        s = _dot(kw_bf, qg_t[g]) + ws_ref[g, pl.ds(boff_w, WIN_KEYS), :]
        p = jnp.exp2(s - jnp.max(s, axis=0, keepdims=True)).astype(BF16)
        acc = _dot(vwt_ref[g, :, pl.ds(woff, WIN_KEYS)], p)
        o_w.append(acc[0:dh] * (1.0 / acc[dh:dh + 1]))

    for h in range(NSA_HPG):
        hl = slice(h * LANE, (h + 1) * LANE)
        halves = []
        for g in groups:
            c = g * NSA_HPG + h
            halves.append(gates_t[c:c + 1, :] * o_c[g][:, hl]
                          + gates_t[NSA_HEADS + c:NSA_HEADS + c + 1, :] * o_s[g][:, hl]
                          + gates_t[2 * NSA_HEADS + c:2 * NSA_HEADS + c + 1, :] * o_w[g][:, hl])
        o_ref[:, hl] = jnp.concatenate(halves, axis=0).T.astype(o_ref.dtype)


def _nsa(proj2d, kc, vct, ts, ws, bc, et, ovt, bsz, s):
    nq = s // Q_BLOCK
    n_slc = s // SLC_LEN
    hq = NSA_HPG * Q_BLOCK
    kvb = C_KV // LANE
    nb = NSA_BATCH
    assert bsz % nb == 0
    proj3d = proj2d.reshape(bsz, s, D_PAD)
    once = dict(pipeline_mode=pl.Buffered(1))
    kv_spec = lambda blk: pl.BlockSpec((nb, s, LANE), lambda b, qi: (b, 0, kvb + blk), **once)
    full = lambda shape: pl.BlockSpec(shape, lambda b, qi: (0,) * len(shape), **once)
    out = pl.pallas_call(
        functools.partial(_nsa_kernel, nq=nq, n_slc=n_slc),
        out_shape=jax.ShapeDtypeStruct((bsz, s, NSA_WIDTH), BF16),
        grid=(bsz // nb, nq),
        in_specs=[pl.BlockSpec((nb, Q_BLOCK, NSA_WIDTH), lambda b, qi: (b, qi, C_Q // NSA_WIDTH)),
                  pl.BlockSpec((nb, Q_BLOCK, LANE), lambda b, qi: (b, qi, C_NG // LANE)),
                  kv_spec(2), kv_spec(3), kv_spec(4), kv_spec(5),
                  pl.BlockSpec((nb, s // CMP_STRIDE, LANE), lambda b, qi: (b, 0, 0)),
                  pl.BlockSpec((nb, LANE, s // CMP_STRIDE), lambda b, qi: (b, 0, 0)),
                  full((NSA_KV_HEADS, SEL_STRIP_TILES * Q_BLOCK, hq)),
                  full((NSA_KV_HEADS, WIN_STRIP_TILES * Q_BLOCK, hq)),
                  full((NSA_KV_HEADS, CMP_BIAS_ROWS, hq)),
                  full((s, LANE)),
                  full((n_slc, LANE))],
        out_specs=pl.BlockSpec((nb, Q_BLOCK, NSA_WIDTH), lambda b, qi: (b, qi, 0)),
        scratch_shapes=[pltpu.VMEM((nb, s, 2 * LANE), BF16), pltpu.VMEM((nb, s, LANE), BF16),
                        pltpu.VMEM((nb, NSA_KV_HEADS, PV_ROWS, s), BF16),
                        pltpu.VMEM((nb, NSA_KV_HEADS, PV_ROWS, s), BF16)],
        compiler_params=_cparams(("parallel", "arbitrary")),
        name="nsa_attn",
    )(proj3d, proj3d, proj3d, proj3d, proj3d, proj3d, kc, vct, ts, ws, bc, et, ovt)
    return out.reshape(bsz * s, NSA_WIDTH)


SSD_BATCH = 2


def _ssd_kernel(xbc_ref, prev_ref, z_ref, dt_ref, *rest):
    shared, o_ref, state_ref = rest[:-2], rest[-2], rest[-1]
    c = pl.program_id(1)
    for u in range(SSD_BATCH):
        _ssd_chunk(c, xbc_ref.at[u], prev_ref.at[u], z_ref.at[u], dt_ref.at[u], *shared,
                   o_ref.at[u], state_ref.at[u])


def _ssd_chunk(c, xbc_ref, prev_ref, z_ref, dt_ref, cw_ref, cb_ref, dtb_ref, alog_ref,
               dskip_ref, ng_ref, e16_ref, tril_ref, o_ref, state_ref):
    L = SSD_CHUNK
    pair_w = 2 * SSD_HEAD_DIM

    @pl.when(c == 0)
    def _():
        state_ref[...] = jnp.zeros(state_ref.shape, F32)

    xcat = jnp.concatenate([jnp.where(c > 0, prev_ref[...], 0.0), xbc_ref[...]], axis=0)
    acc = cw_ref[0:1, :] * xcat
    for k in range(1, SSD_CONV):
        acc = pltpu.roll(acc, 1, axis=0) + cw_ref[k:k + 1, :] * xcat
    xbc = jax.nn.silu(acc[SUBLANE:, :] + cb_ref[...])
    xs = xbc[:, :SSD_D_INNER]
    bmat = xbc[:, SSD_D_INNER:SSD_D_INNER + SSD_GROUPS * SSD_STATE]
    cmat = xbc[:, SSD_D_INNER + SSD_GROUPS * SSD_STATE:]

    dt = jax.nn.softplus(dt_ref[...] + dtb_ref[...])
    tril = tril_ref[...]
    e16 = e16_ref[...]
    d1, d2, d3 = _split3(dt)
    dt_x = _dot(d1, e16) + _dot(d2, e16) + _dot(d3, e16)
    a_c = dt * (-jnp.exp(alog_ref[...]) * LOG2E)
    c1, c2, c3 = _split3(a_c)
    acs = _dot(tril, c1) + _dot(tril, c2) + _dot(tril, c3)
    x1, x2, x3 = _split3(acs)
    acs_x = _dot(x1, e16) + _dot(x2, e16) + _dot(x3, e16)
    acs_t = acs.T
    last = acs_x[L - 1:L, :]
    eacs = jnp.exp2(acs_x)
    decay_in = jnp.exp2(last - acs_x)
    cdec = jnp.exp2(last)
    xdt = xs * dt_x
    wst = xdt * decay_in

    row = lax.broadcasted_iota(jnp.int32, (L, L), 0)
    lane = lax.broadcasted_iota(jnp.int32, (L, L), 1)
    tri = row >= lane
    lo = lane < SSD_HEAD_DIM

    ys = []
    for g in range(SSD_GROUPS):
        cg = cmat[:, g * SSD_STATE:(g + 1) * SSD_STATE].astype(BF16)
        bg = bmat[:, g * SSD_STATE:(g + 1) * SSD_STATE].astype(BF16)
        cb = _dot_nt(cg, bg)
        for pr in range(2):
            pp = g * 2 + pr
            ls = slice(pp * pair_w, (pp + 1) * pair_w)
            xp = xdt[:, ls]
            yd = None
            for e in range(2):
                hd = 2 * pp + e
                diff = acs[:, hd:hd + 1] - acs_t[hd:hd + 1, :]
                seg = jnp.exp2(jnp.where(tri, diff, NEG))
                pm = (cb * seg).astype(BF16)
                xe = jnp.where(lo if e == 0 else jnp.logical_not(lo), xp, 0.0).astype(BF16)
                term = _dot(pm, xe)
                yd = term if yd is None else yd + term
            st = state_ref[pp]
            yo = _dot(cg, st.astype(BF16)) * eacs[:, ls]
            ys.append(yd + yo)
            state_ref[pp] = cdec[:, ls] * st + _dot_tn(bg, wst[:, ls].astype(BF16))

    y = jnp.concatenate(ys, axis=1) + xs * dskip_ref[...]
    y = y * jax.nn.silu(z_ref[...])
    gw = SSD_D_INNER // SSD_GROUPS
    outs = []
    for gi in range(SSD_GROUPS):
        yg = y[:, gi * gw:(gi + 1) * gw]
        outs.append(yg * lax.rsqrt(jnp.mean(yg * yg, -1, keepdims=True) + LN_EPS))
    o_ref[...] = (jnp.concatenate(outs, axis=1) * ng_ref[...]).astype(o_ref.dtype)


def _ssd(proj2d, bsz, s, conv_w, conv_b, dt_bias, a_log, d_skip, norm_g):
    nc = s // SSD_CHUNK
    L = SSD_CHUNK
    pad16 = lambda v: jnp.pad(v.reshape(1, SSD_HEADS), ((0, 0), (0, LANE - SSD_HEADS)))
    rep64 = lambda v: jnp.repeat(v, SSD_HEAD_DIM).reshape(1, SSD_D_INNER)
    e16 = np.zeros((LANE, SSD_D_INNER), np.float32)
    e16[np.arange(SSD_D_INNER) // SSD_HEAD_DIM, np.arange(SSD_D_INNER)] = 1.0
    tril = np.tril(np.ones((L, L), np.float32))
    full = lambda shape: pl.BlockSpec(shape, lambda b, c: (0,) * len(shape))
    nb = SSD_BATCH
    assert bsz % nb == 0
    proj3d = proj2d.reshape(bsz, s, D_PAD)
    out = pl.pallas_call(
        _ssd_kernel,
        out_shape=jax.ShapeDtypeStruct((bsz, s, SSD_D_INNER), BF16),
        grid=(bsz // nb, nc),
        in_specs=[pl.BlockSpec((nb, L, SSD_XBC), lambda b, c: (b, c, C_XBC // SSD_XBC)),
                  pl.BlockSpec((nb, SUBLANE, SSD_XBC),
                               lambda b, c: (b, jnp.maximum(c * (L // SUBLANE) - 1, 0), C_XBC // SSD_XBC)),
                  pl.BlockSpec((nb, L, SSD_D_INNER), lambda b, c: (b, c, C_Z // SSD_D_INNER)),
                  pl.BlockSpec((nb, L, LANE), lambda b, c: (b, c, C_DT // LANE)),
                  full((SSD_CONV, SSD_XBC)), full((1, SSD_XBC)), full((1, LANE)), full((1, LANE)),
                  full((1, SSD_D_INNER)), full((1, SSD_D_INNER)),
                  full((LANE, SSD_D_INNER)), full((L, L))],
        out_specs=pl.BlockSpec((nb, L, SSD_D_INNER), lambda b, c: (b, c, 0)),
        scratch_shapes=[pltpu.VMEM((nb, SSD_HEADS // 2, SSD_STATE, 2 * SSD_HEAD_DIM), F32)],
        compiler_params=_cparams(("parallel", "arbitrary")),
        name="ssd",
    )(proj3d, proj3d, proj3d, proj3d, conv_w, conv_b.reshape(1, SSD_XBC), pad16(dt_bias), pad16(a_log),
      rep64(d_skip), norm_g.reshape(1, SSD_D_INNER),
      jnp.asarray(e16, BF16), jnp.asarray(tril, BF16))
    return out.reshape(bsz * s, SSD_D_INNER)


def _mix_kernel(up_ref, upprev_ref, nsa_ref, ssd_ref, ga_ref, gb_ref, gc_ref, h_ref,
                pw_ref, ps_ref, wbp_ref, wbn_ref, wbs_ref, wo_ref, lg_ref, lb_ref, o_ref, *, ts):
    si = pl.program_id(1)
    w0 = MAX_POOL_WINDOW
    prev = jnp.where(si > 0, upprev_ref[...], 0.0)
    xcat = jnp.concatenate([prev, up_ref[...]], axis=0)
    tpos = si * ts + lax.broadcasted_iota(jnp.int32, (ts, POOL_GROUP), 0)
    ygs = []
    for gi, w in enumerate(POOL_WINDOWS):
        x = xcat[:, gi * POOL_GROUP:(gi + 1) * POOL_GROUP]
        acc = x
        d = 1
        while d < w:
            acc = acc + pltpu.roll(acc, d, axis=0)
            d *= 2
        cnt = jnp.minimum(tpos + 1, w).astype(F32)
        r = acc[w0:, :] / cnt - x[w0:, :]
        ygs.append(_dot(r.astype(BF16), pw_ref[gi]))
    y = jnp.concatenate(ygs, axis=1) * ps_ref[...]
    br_a = _dot(y.astype(BF16), wbp_ref[...])
    br_b = _dot(nsa_ref[...], wbn_ref[...])
    br_c = _dot(ssd_ref[...], wbs_ref[...])
    mixed = (jax.nn.sigmoid(ga_ref[...]) * br_a + jax.nn.sigmoid(gb_ref[...]) * br_b
             + jax.nn.sigmoid(gc_ref[...]) * br_c)
    mix = _dot(mixed.astype(BF16), wo_ref[...])
    _store_rows(o_ref, _layer_norm(DN_ALPHA * _load_rows(h_ref) + mix, lg_ref[...], lb_ref[...]))


def _mix(proj2d, nsa_out, ssd_y, h3, bsz, s, pool_w, pool_scale, w_br_pool, w_br_nsa, w_br_ssd,
         w_out, ln_g, ln_b):
    ts = MIX_TS
    ns = s // ts
    t = bsz * s
    d = D_MODEL
    w0 = MAX_POOL_WINDOW
    wbn = w_br_nsa.reshape(NSA_KV_HEADS, NSA_HPG, NSA_HEAD_DIM, d).transpose(1, 0, 2, 3).reshape(NSA_WIDTH, d)
    full = lambda shape: pl.BlockSpec(shape, lambda b, si: (0,) * len(shape), pipeline_mode=pl.Buffered(1))
    row = lambda width, cb: pl.BlockSpec((ts, width), lambda b, si: (b * ns + si, cb))
    return pl.pallas_call(
        functools.partial(_mix_kernel, ts=ts),
        out_shape=jax.ShapeDtypeStruct((t,) + ROW_TILE, F32),
        grid=(bsz, ns),
        in_specs=[row(POOL_WIDTH, C_POOL // POOL_WIDTH),
                  pl.BlockSpec((w0, POOL_WIDTH),
                               lambda b, si: (jnp.maximum((b * s + si * ts) // w0 - 1, 0), C_POOL // POOL_WIDTH)),
                  row(NSA_WIDTH, 0), row(d, 0),
                  row(d, C_GATE // d), row(d, C_GATE // d + 1), row(d, C_GATE // d + 2),
                  _row_spec(ts, lambda b, si: b * ns + si),
                  full((len(POOL_WINDOWS), POOL_GROUP, POOL_GROUP)), full((1, POOL_WIDTH)),
                  full((POOL_WIDTH, d)), full((NSA_WIDTH, d)), full((d, d)), full((d, d)),
                  full((1, d)), full((1, d))],
        out_specs=_row_spec(ts, lambda b, si: b * ns + si),
        compiler_params=_cparams(("parallel", "arbitrary")),
        name="mix",
    )(proj2d, proj2d, nsa_out, ssd_y, proj2d, proj2d, proj2d, h3,
      pool_w.astype(BF16), pool_scale.reshape(1, POOL_WIDTH), w_br_pool.astype(BF16), wbn.astype(BF16),
      w_br_ssd.astype(BF16), w_out.astype(BF16), ln_g.reshape(1, d), ln_b.reshape(1, d))


RANK_CHUNK = 256
BUCKET_ROWS = 32


def _route(hb, first_step, rw_ref, rb_ref, tri_ref, bucket_ref, wlo_ref, whi_ref, rank_ref, cnt_ref, run_ref):
    @pl.when(first_step)
    def _():
        run_ref[...] = jnp.zeros(run_ref.shape, F32)

    logits = _dot_nt(rw_ref[...], hb)
    lg = [logits[e:e + 1, :] for e in range(N_EXPERTS)]
    mx = lg[0]
    for e in range(1, N_EXPERTS):
        mx = jnp.maximum(mx, lg[e])
    ex = [jnp.exp(v - mx) for v in lg]
    den = ex[0]
    for e in range(1, N_EXPERTS):
        den = den + ex[e]
    probs = [v / den for v in ex]
    score = [probs[e] + rb_ref[e:e + 1, :] for e in range(N_EXPERTS)]

    def group_vals(vals, grp):
        out = []
        for k in range(EXPERTS_PER_GROUP):
            v = vals[k]
            for gi in range(1, N_EXPERT_GROUPS):
                v = jnp.where(grp == gi, vals[gi * EXPERTS_PER_GROUP + k], v)
            out.append(v)
        return out

    gscore = []
    for gi in range(N_EXPERT_GROUPS):
        sc = score[gi * EXPERTS_PER_GROUP:(gi + 1) * EXPERTS_PER_GROUP]
        best = None
        for a, b in PAIRS:
            v = sc[a] + sc[b]
            best = v if best is None else jnp.maximum(best, v)
        gscore.append(best)
    grp = jnp.zeros(gscore[0].shape, jnp.int32)
    best = gscore[0]
    for gi in range(1, N_EXPERT_GROUPS):
        better = gscore[gi] > best
        grp = jnp.where(better, gi, grp)
        best = jnp.where(better, gscore[gi], best)

    sc = group_vals(score, grp)
    pr = group_vals(probs, grp)
    first = jnp.zeros(grp.shape, jnp.int32)
    fv = sc[0]
    for k in range(1, EXPERTS_PER_GROUP):
        better = sc[k] > fv
        first = jnp.where(better, k, first)
        fv = jnp.where(better, sc[k], fv)
    second = jnp.full(grp.shape, -1, jnp.int32)
    sv = jnp.full(fv.shape, -jnp.inf, F32)
    for k in range(EXPERTS_PER_GROUP):
        better = (first != k) & ((sc[k] > sv) | (second < 0))
        second = jnp.where(better, k, second)
        sv = jnp.where(better, sc[k], sv)
    lo = jnp.minimum(first, second)
    hi = jnp.maximum(first, second)
    p_lo = pr[0]
    p_hi = pr[0]
    for k in range(1, EXPERTS_PER_GROUP):
        p_lo = jnp.where(lo == k, pr[k], p_lo)
        p_hi = jnp.where(hi == k, pr[k], p_hi)
    tot = p_lo + p_hi
    pair = jnp.zeros(grp.shape, jnp.int32)
    for pi, (a, b) in enumerate(PAIRS):
        pair = jnp.where((lo == a) & (hi == b), pi, pair)
    bucket = grp * len(PAIRS) + pair
    bucket_ref[...] = bucket
    wlo_ref[...] = p_lo / tot
    whi_ref[...] = p_hi / tot

    tm = bucket.shape[1]
    onehot = jnp.where(lax.broadcasted_iota(jnp.int32, (BUCKET_ROWS, tm), 0) == bucket, 1.0, 0.0)
    run = run_ref[...]
    ranks = []
    for c in range(tm // RANK_CHUNK):
        oc = onehot[:, c * RANK_CHUNK:(c + 1) * RANK_CHUNK]
        before = _dot(oc.astype(BF16), tri_ref[...]) + jnp.concatenate([run] * (RANK_CHUNK // LANE), axis=1)
        ranks.append(jnp.sum(oc * before, axis=0, keepdims=True))
        run = run + jnp.broadcast_to(jnp.sum(oc, axis=1, keepdims=True), run.shape)
    rank_ref[...] = jnp.concatenate(ranks, axis=1).astype(jnp.int32)
    run_ref[...] = run
    cnt_ref[...] = run


ROUTER_TM = 2048


def _router_kernel(h_ref, *refs):
    _route(_load_rows(h_ref).astype(BF16), pl.program_id(0) == 0, *refs)


def _router(h2d, router_w, router_b):
    t, d = h2d.shape[0], D_MODEL
    tm = ROUTER_TM
    rw = jnp.pad(router_w.T, ((0, LANE - N_EXPERTS), (0, 0))).astype(BF16)
    rb = jnp.broadcast_to(jnp.pad(router_b, (0, LANE - N_EXPERTS))[:, None], (LANE, tm))
    tri = jnp.asarray(np.triu(np.ones((RANK_CHUNK, RANK_CHUNK), np.float32), 1), BF16)
    vec = lambda dt: jax.ShapeDtypeStruct((1, t), dt)
    row = pl.BlockSpec((1, tm), lambda i: (0, i))
    return pl.pallas_call(
        _router_kernel,
        out_shape=(vec(jnp.int32), vec(F32), vec(F32), vec(jnp.int32),
                   jax.ShapeDtypeStruct((BUCKET_ROWS, LANE), F32)),
        grid=(t // tm,),
        in_specs=[_row_spec(tm, lambda i: i),
                  pl.BlockSpec((LANE, d), lambda i: (0, 0)),
                  pl.BlockSpec((LANE, tm), lambda i: (0, 0)),
                  pl.BlockSpec((RANK_CHUNK, RANK_CHUNK), lambda i: (0, 0))],
        out_specs=(row, row, row, row, pl.BlockSpec((BUCKET_ROWS, LANE), lambda i: (0, 0))),
        scratch_shapes=[pltpu.VMEM((BUCKET_ROWS, LANE), F32)],
        compiler_params=_cparams(("arbitrary",)),
        name="router",
    )(h2d, rw, rb, tri)


def _moe_plan(bucket, rank, counts, w_lo, w_hi, t):
    tm = MOE_TM
    n_tiles = t // tm + N_BUCKETS
    p_rows = n_tiles * tm
    tiles_per = (counts + tm - 1) // tm
    tile_end = jnp.cumsum(tiles_per)
    tile_start = tile_end - tiles_per
    in_bucket = bucket[:, None] == jnp.arange(N_BUCKETS)[None, :]
    dest = jnp.sum(jnp.where(in_bucket, (tile_start * tm)[None, :], 0), axis=1) + rank
    payload = jnp.stack([jnp.arange(t, dtype=jnp.int32), lax.bitcast_convert_type(w_lo, jnp.int32),
                         lax.bitcast_convert_type(w_hi, jnp.int32), jnp.ones((t,), jnp.int32)], axis=1)
    plan = jnp.zeros((p_rows, 4), jnp.int32).at[dest].set(payload, unique_indices=True,
                                                          mode="promise_in_bounds")
    src = plan[:, 0]
    wl = lax.bitcast_convert_type(plan[:, 1], F32)
    wh = lax.bitcast_convert_type(plan[:, 2], F32)
    prow = jnp.arange(p_rows, dtype=jnp.int32)
    spare = t + ((prow // tm) % 2) * tm + prow % tm
    dst = jnp.where(plan[:, 3] > 0, src, spare)
    n_used = tile_end[-1]
    tile_ids = jnp.arange(n_tiles)
    tb = jnp.sum((tile_end[None, :] <= tile_ids[:, None]).astype(jnp.int32), axis=1)
    tb = jnp.where(tile_ids < n_used, tb, tb[jnp.maximum(n_used - 1, 0)])
    tb = jnp.minimum(tb, N_BUCKETS - 1)
    pairs = np.asarray(PAIRS, np.int32)
    grp = tb // len(PAIRS)
    ea = grp * EXPERTS_PER_GROUP + jnp.asarray(pairs[:, 0])[tb % len(PAIRS)]
    eb = grp * EXPERTS_PER_GROUP + jnp.asarray(pairs[:, 1])[tb % len(PAIRS)]
    nvalid = jnp.clip(counts[tb] - (tile_ids - tile_start[tb]) * tm, 0, tm)
    nvalid = jnp.where(tile_ids < n_used, nvalid, 0).astype(jnp.int32)
    return (src.reshape(n_tiles, 1, tm), dst.reshape(n_tiles, 1, tm), wl.reshape(p_rows, 1), wh.reshape(p_rows, 1),
            ea.astype(jnp.int32), eb.astype(jnp.int32), nvalid, n_tiles)


MOE_DMA_GROUPS = 8


def _moe_kernel(ea_ref, eb_ref, nv_ref, src0_ref, srcn_ref, dstp_ref, wl_ref, wh_ref, h_hbm,
                w1a_ref, w3a_ref, w2a_ref, w1b_ref, w3b_ref, w2b_ref, lg_ref, lb_ref,
                o_hbm, xbuf0, xbuf1, obuf0, obuf1, wa1, wa3, wa2, wb1, wb3, wb2, gsem, ssem):
    i = pl.program_id(0)
    tm = xbuf0.shape[0]
    xbufs = (xbuf0, xbuf1)
    obufs = (obuf0, obuf1)
    nv = nv_ref[i]
    nv_p1 = nv_ref[jnp.maximum(i - 1, 0)]
    odd = lax.rem(i, 2) == 1
    even = jnp.logical_not(odd)

    def gather(r, p, row):
        return pltpu.make_async_copy(h_hbm.at[pl.ds(row, 1)], xbufs[p].at[pl.ds(r, 1)], gsem.at[p])

    def scatter(r, p, row):
        return pltpu.make_async_copy(obufs[p].at[pl.ds(r, 1)], o_hbm.at[pl.ds(row, 1)], ssem.at[p])

    def wait_gather(p):
        for r in range(tm):
            gather(r, p, 0).wait()

    def wait_scatter(p):
        for r in range(tm):
            scatter(r, p, 0).wait()

    def tile(p, age):
        q = 1 - p
        half = MOE_DMA_GROUPS // 2
        per = tm // half

        def issue(k):
            for r in range((k % half) * per, (k % half + 1) * per):
                if k < half:
                    gather(r, q, srcn_ref[0, 0, r]).start()
                elif age >= 1:
                    scatter(r, q, dstp_ref[0, 0, r]).start()

        wait_gather(p)
        x = _load_rows(xbufs[p])
        xb = x.astype(BF16)
        issue(0)
        h1 = _dot(xb, wa1[...])
        issue(1)
        h3 = _dot(xb, wa3[...])
        issue(2)
        ya = _dot((jax.nn.silu(h1) * h3).astype(BF16), wa2[...])
        issue(3)
        h1 = _dot(xb, wb1[...])
        issue(4)
        h3 = _dot(xb, wb3[...])
        issue(5)
        yb = _dot((jax.nn.silu(h1) * h3).astype(BF16), wb2[...])
        issue(6)
        y = wl_ref[...] * ya + wh_ref[...] * yb
        issue(7)
        out = _layer_norm(DN_ALPHA * x + y, lg_ref[...], lb_ref[...])
        if age >= 2:
            wait_scatter(p)
        _store_rows(obufs[p], out)

    @pl.when(i == 0)
    def _():
        obuf1[...] = jnp.zeros(obuf1.shape, F32)
        n_real = o_hbm.shape[0] - 2 * tm
        for blk in range(2):
            cp = pltpu.make_async_copy(obuf1, o_hbm.at[pl.ds(n_real + blk * tm, tm)], ssem.at[1])
            cp.start()
            cp.wait()

    prev = jnp.maximum(i - 1, 0)

    @pl.when((nv > 0) & ((i == 0) | (ea_ref[i] != ea_ref[prev])))
    def _():
        wa1[...] = w1a_ref[0, 0].astype(BF16)
        wa3[...] = w3a_ref[0, 0].astype(BF16)
        wa2[...] = w2a_ref[0, 0].astype(BF16)

    @pl.when((nv > 0) & ((i == 0) | (eb_ref[i] != eb_ref[prev])))
    def _():
        wb1[...] = w1b_ref[0, 0].astype(BF16)
        wb3[...] = w3b_ref[0, 0].astype(BF16)
        wb2[...] = w2b_ref[0, 0].astype(BF16)

    @pl.when((nv > 0) & (i == 0))
    def _():
        def start0(r, carry):
            gather(r, 0, src0_ref[0, 0, r]).start()
            return carry
        lax.fori_loop(0, tm, start0, 0)
        tile(0, 0)

    pl.when((nv > 0) & (i == 1))(functools.partial(tile, 1, 1))
    pl.when((nv > 0) & (i >= 2) & even)(functools.partial(tile, 0, 2))
    pl.when((nv > 0) & (i >= 2) & odd)(functools.partial(tile, 1, 2))

    def tail(p):
        q = 1 - p
        wait_gather(p)

        @pl.when(i >= 2)
        def _():
            wait_scatter(p)

        def start(r, carry):
            scatter(r, q, dstp_ref[0, 0, r]).start()
            return carry
        lax.fori_loop(0, tm, start, 0)
        wait_scatter(q)

    is_tail = (nv == 0) & (i >= 1) & (nv_p1 > 0)
    pl.when(is_tail & even)(functools.partial(tail, 0))
    pl.when(is_tail & odd)(functools.partial(tail, 1))


def _moe(h3, bucket, rank, counts, w_lo, w_hi, w1, w3, w2, layer, ln_g, ln_b):
    t, d = h3.shape[0], D_MODEL
    tm = MOE_TM
    assert tm % MOE_DMA_GROUPS == 0
    src, dst, wl, wh, ea, eb, nvalid, n_tiles = _moe_plan(bucket, rank, counts, w_lo, w_hi, t)
    wspec = lambda shape, which: pl.BlockSpec(
        (1,) + shape, (lambda i, ea_r, eb_r, nv_r: (layer, ea_r[i], 0, 0)) if which == 0
        else (lambda i, ea_r, eb_r, nv_r: (layer, eb_r[i], 0, 0)))
    w13 = (1, d, D_EXPERT)
    w2s = (1, D_EXPERT, d)
    smem_tile = lambda fn: pl.BlockSpec((1, 1, tm), lambda i, *_: (fn(i), 0, 0), memory_space=pltpu.SMEM)
    grid_spec = pltpu.PrefetchScalarGridSpec(
        num_scalar_prefetch=3,
        grid=(n_tiles,),
        in_specs=[smem_tile(lambda i: 0),
                  smem_tile(lambda i: jnp.minimum(i + 1, n_tiles - 1)),
                  smem_tile(lambda i: jnp.maximum(i - 1, 0)),
                  pl.BlockSpec((tm, 1), lambda i, *_: (i, 0)),
                  pl.BlockSpec((tm, 1), lambda i, *_: (i, 0)),
                  pl.BlockSpec(memory_space=pl.ANY),
                  wspec(w13, 0), wspec(w13, 0), wspec(w2s, 0),
                  wspec(w13, 1), wspec(w13, 1), wspec(w2s, 1),
                  pl.BlockSpec((1, d), lambda i, *_: (0, 0)),
                  pl.BlockSpec((1, d), lambda i, *_: (0, 0))],
        out_specs=pl.BlockSpec(memory_space=pl.ANY),
        scratch_shapes=[pltpu.VMEM((tm,) + ROW_TILE, F32)] * 4
        + [pltpu.VMEM(w13[1:], BF16), pltpu.VMEM(w13[1:], BF16), pltpu.VMEM(w2s[1:], BF16)] * 2
        + [pltpu.SemaphoreType.DMA((2,)), pltpu.SemaphoreType.DMA((2,))],
    )
    return pl.pallas_call(
        _moe_kernel,
        out_shape=jax.ShapeDtypeStruct((t + 2 * tm,) + ROW_TILE, F32),
        grid_spec=grid_spec,
        compiler_params=_cparams(("arbitrary",)),
        name="moe",
    )(ea, eb, nvalid, src, src, dst, wl, wh, h3, w1, w3, w2, w1, w3, w2,
      ln_g.reshape(1, d), ln_b.reshape(1, d))


def kernel(x, ln0_g, ln0_b, w_in, pool_w, pool_scale, cmp_k_pe, cmp_k_w1, cmp_k_w2, cmp_v_pe, cmp_v_w1, cmp_v_w2, rel_bias, conv_w, conv_b, dt_bias, a_log, d_skip, ssd_norm_g, w_br_pool, w_br_nsa, w_br_ssd, w_out, ln1_g, ln1_b, router_w, router_b, exp_w1, exp_w3, exp_w2, ln2_g, ln2_b):
    bsz, s, d = x.shape
    assert d == D_MODEL and s % MIX_TS == 0 and s // CMP_STRIDE == LANE and (bsz * s) % PROJ_TM == 0
    t = bsz * s
    ts, ws, bc = _nsa_bias_tables(rel_bias)
    et, ovt = _nsa_consts(s)
    h = x.reshape(t, d)
    for i in range(DEPTH):
        if i == 0:
            proj, h = _proj(h, _prep_w_in(w_in[i]), t, ln=(ln0_g, ln0_b))
        else:
            proj = _proj(h, _prep_w_in(w_in[i]), t)
        kc, vc = _compress(proj, bsz, s, cmp_k_pe[i], cmp_k_w1[i], cmp_k_w2[i],
                           cmp_v_pe[i], cmp_v_w1[i], cmp_v_w2[i])
        nsa_out = _nsa(proj, kc, vc, ts, ws, bc, et, ovt, bsz, s)
        ssd_y = _ssd(proj, bsz, s, conv_w[i], conv_b[i], dt_bias[i], a_log[i], d_skip[i], ssd_norm_g[i])
        h1 = _mix(proj, nsa_out, ssd_y, h, bsz, s, pool_w[i], pool_scale[i], w_br_pool[i], w_br_nsa[i],
                  w_br_ssd[i], w_out[i], ln1_g[i], ln1_b[i])
        bucket, w_lo, w_hi, rank, cnt = _router(h1, router_w, router_b)
        counts = cnt[:N_BUCKETS, 0].astype(jnp.int32)
        h = _moe(h1, bucket[0], rank[0], counts, w_lo[0], w_hi[0], exp_w1, exp_w3, exp_w2, i,
                 ln2_g[i], ln2_b[i])
    return h[:t].reshape(bsz, s, d)
```

```python
import functools
import math

import numpy as np
import jax
import jax.numpy as jnp
from jax import lax
from jax.experimental import pallas as pl
from jax.experimental.pallas import tpu as pltpu

F32 = jnp.float32
BF16 = jnp.bfloat16

D_MODEL = 1024
DEPTH = 2
DN_ALPHA = (2.0 * DEPTH) ** 0.25
LN_EPS = 1e-5
NEG = -1e30
BIG = 1e6

POOL_WINDOWS = (2, 4, 8, 16)
POOL_GROUP = 128
POOL_WIDTH = 512
MAX_POOL_WINDOW = 16

NSA_HEAD_DIM = 64
NSA_KV_HEADS = 2
NSA_HPG = 4
NSA_HEADS = 8
NSA_WIDTH = 512
CMP_LEN = 32
CMP_STRIDE = 16
CMP_HIDDEN = 128
SLC_LEN = 64
SLC_TOPN = 8
WIN_LEN = 512
Q_BLOCK = 128
REL_BUCKETS = 32
REL_MAX_DIST = 128

SSD_D_INNER = 1024
SSD_HEAD_DIM = 64
SSD_HEADS = 16
SSD_GROUPS = 4
SSD_STATE = 128
SSD_CONV = 4
SSD_CHUNK = 128
SSD_XBC = 2048

N_EXPERTS = 16
N_EXPERT_GROUPS = 4
EXPERTS_PER_GROUP = 4
D_EXPERT = 512
PAIRS = ((0, 1), (0, 2), (0, 3), (1, 2), (1, 3), (2, 3))
N_BUCKETS = N_EXPERT_GROUPS * len(PAIRS)

LANE = 128
SUBLANE = 8
VMEM_LIMIT = 48 * 1024 * 1024

C_POOL = 0
C_Q = 512
C_KV = 1024
C_NG = 1792
C_DT = 1920
C_XBC = 2048
C_Z = 4096
C_GATE = 5120
D_PAD = 8192

PROJ_TM = 1024
PROJ_TN = 2048
MIX_TS = 512
MOE_TM = 256


def _dot(a, b):
    return jnp.dot(a, b, preferred_element_type=F32)


def _dot_nt(a, b):
    return lax.dot_general(a, b, (((1,), (1,)), ((), ())), preferred_element_type=F32)


def _dot_tn(a, b):
    return lax.dot_general(a, b, (((0,), (0,)), ((), ())), preferred_element_type=F32)


def _split3(x):
    x1 = x.astype(BF16)
    r1 = x - x1.astype(F32)
    x2 = r1.astype(BF16)
    x3 = (r1 - x2.astype(F32)).astype(BF16)
    return x1, x2, x3


def _layer_norm(x, g, b):
    mu = jnp.mean(x, -1, keepdims=True)
    xc = x - mu
    var = jnp.mean(xc * xc, -1, keepdims=True)
    return xc * lax.rsqrt(var + LN_EPS) * g + b


def _cparams(sem):
    return pltpu.CompilerParams(dimension_semantics=sem, vmem_limit_bytes=VMEM_LIMIT)


ROW_TILE = (D_MODEL,)


def _load_rows(ref):
    return ref[...]


def _store_rows(ref, val):
    ref[...] = val


def _row_spec(n, index):
    return pl.BlockSpec((n,) + ROW_TILE, lambda *ids: (index(*ids), 0))


def _proj_kernel(h_ref, w_ref, o_ref, hb_ref):
    @pl.when(pl.program_id(1) == 0)
    def _():
        hb_ref[...] = _load_rows(h_ref).astype(BF16)

    o_ref[...] = _dot(hb_ref[...], w_ref[...])


def _ln_proj_kernel(x_ref, g_ref, b_ref, w_ref, o_ref, h_ref, hb_ref):
    @pl.when(pl.program_id(1) == 0)
    def _():
        h = _layer_norm(x_ref[...], g_ref[...], b_ref[...])
        _store_rows(h_ref, h)
        hb_ref[...] = h.astype(BF16)

    o_ref[...] = _dot(hb_ref[...], w_ref[...])


def _proj(h2d, w_pad, t, ln=None):
    d = D_MODEL
    grid = (t // PROJ_TM, D_PAD // PROJ_TN)
    w_spec = pl.BlockSpec((d, PROJ_TN), lambda i, j: (0, j))
    o_spec = pl.BlockSpec((PROJ_TM, PROJ_TN), lambda i, j: (i, j))
    o_shape = jax.ShapeDtypeStruct((t, D_PAD), F32)
    scratch = [pltpu.VMEM((PROJ_TM, d), BF16)]
    params = _cparams(("parallel", "arbitrary"))
    if ln is None:
        return pl.pallas_call(
            _proj_kernel, out_shape=o_shape, grid=grid,
            in_specs=[_row_spec(PROJ_TM, lambda i, j: i), w_spec], out_specs=o_spec,
            scratch_shapes=scratch, compiler_params=params, name="proj",
        )(h2d, w_pad)
    vec = pl.BlockSpec((1, d), lambda i, j: (0, 0))
    return pl.pallas_call(
        _ln_proj_kernel, out_shape=(o_shape, jax.ShapeDtypeStruct((t,) + ROW_TILE, F32)), grid=grid,
        in_specs=[_row_spec(PROJ_TM, lambda i, j: i), vec, vec, w_spec],
        out_specs=(o_spec, _row_spec(PROJ_TM, lambda i, j: i)),
        scratch_shapes=scratch, compiler_params=params, name="ln_proj",
    )(h2d, ln[0].reshape(1, d), ln[1].reshape(1, d), w_pad)


def _pad_cols(w, n):
    return jnp.pad(w, ((0, 0), (0, n - w.shape[1])))


def _prep_w_in(w_in):
    d = w_in.shape[0]
    o = 0
    w_pool = w_in[:, o:o + POOL_WIDTH]; o += POOL_WIDTH
    w_q = w_in[:, o:o + NSA_WIDTH]; o += NSA_WIDTH
    w_kv = w_in[:, o:o + 768]; o += 768
    w_ng = w_in[:, o:o + 24]; o += 24
    w_z = w_in[:, o:o + SSD_D_INNER]; o += SSD_D_INNER
    w_xbc = w_in[:, o:o + SSD_XBC]; o += SSD_XBC
    w_dt = w_in[:, o:o + SSD_HEADS]; o += SSD_HEADS
    w_gate = w_in[:, o:o + 3 * D_MODEL]
    w_q = w_q.reshape(d, NSA_KV_HEADS, NSA_HPG, NSA_HEAD_DIM).transpose(0, 2, 1, 3).reshape(d, NSA_WIDTH)
    w = jnp.concatenate([w_pool, w_q, w_kv, _pad_cols(w_ng, LANE), _pad_cols(w_dt, LANE),
                         w_xbc, w_z, w_gate], axis=1)
    return w.astype(BF16)


def _compress_one(tok_ref, pe_ref, w1_ref, w2_ref):
    n16 = tok_ref.shape[0] // CMP_STRIDE
    acc_a = jnp.zeros((n16, 2 * CMP_HIDDEN), F32)
    acc_b = jnp.zeros((n16, 2 * CMP_HIDDEN), F32)
    for l in range(CMP_STRIDE):
        rows = tok_ref[pl.ds(l, n16, stride=CMP_STRIDE), :]
        xa = (rows + pe_ref[l:l + 1, :]).astype(BF16)
        xb = (rows + pe_ref[CMP_STRIDE + l:CMP_STRIDE + l + 1, :]).astype(BF16)
        acc_a = acc_a + _dot(xa, w1_ref[l])
        acc_b = acc_b + _dot(xb, w1_ref[CMP_STRIDE + l])
    hid = acc_a + pltpu.roll(acc_b, n16 - 1, axis=0)
    hid = jax.nn.gelu(hid)
    return _dot(hid.astype(BF16), w2_ref[...])


def _compress_kernel(kt_ref, vt_ref, pek_ref, w1k_ref, w2k_ref, pev_ref, w1v_ref, w2v_ref,
                     kc_ref, vc_ref):
    kc_ref[0] = _compress_one(kt_ref, pek_ref, w1k_ref, w2k_ref).astype(BF16)
    vc_ref[0] = _compress_one(vt_ref, pev_ref, w1v_ref, w2v_ref).T.astype(BF16)


def _blockdiag2(w):
    z = jnp.zeros_like(w)
    return jnp.concatenate([jnp.concatenate([w, z], -1), jnp.concatenate([z, w], -1)], -2)


def _compress(proj2d, bsz, s, pe_k, w1_k, w2_k, pe_v, w1_v, w2_v):
    n16 = s // CMP_STRIDE

    def prep(pe, w1, w2):
        pe2 = jnp.concatenate([pe, pe], axis=-1)
        w1b = _blockdiag2(w1.reshape(CMP_LEN, NSA_HEAD_DIM, CMP_HIDDEN)).astype(BF16)
        w2b = _blockdiag2(w2).astype(BF16)
        return pe2, w1b, w2b

    pk = prep(pe_k, w1_k, w2_k)
    pv = prep(pe_v, w1_v, w2_v)
    full = lambda shape: pl.BlockSpec(shape, lambda b: (0,) * len(shape))
    wspecs = [full((CMP_LEN, LANE)), full((CMP_LEN, LANE, 2 * CMP_HIDDEN)), full((2 * CMP_HIDDEN, LANE))]
    return pl.pallas_call(
        _compress_kernel,
        out_shape=(jax.ShapeDtypeStruct((bsz, n16, LANE), BF16),
                   jax.ShapeDtypeStruct((bsz, n16, LANE), BF16)),
        grid=(bsz,),
        in_specs=[pl.BlockSpec((s, LANE), lambda b: (b, C_KV // LANE)),
                  pl.BlockSpec((s, LANE), lambda b: (b, C_KV // LANE + 1))] + wspecs + wspecs,
        out_specs=(pl.BlockSpec((1, n16, LANE), lambda b: (b, 0, 0)),
                   pl.BlockSpec((1, n16, LANE), lambda b: (b, 0, 0))),
        compiler_params=_cparams(("parallel",)),
        name="nsa_compress",
    )(proj2d, proj2d, *pk, *pv)


def _rel_bucket(dist):
    n = jnp.maximum(dist, 0)
    max_exact = REL_BUCKETS // 2
    nf = jnp.maximum(n, 1).astype(F32)
    large = max_exact + (jnp.log(nf / max_exact) / math.log(REL_MAX_DIST / max_exact)
                         * (REL_BUCKETS - max_exact)).astype(jnp.int32)
    large = jnp.minimum(large, REL_BUCKETS - 1)
    return jnp.where(n < max_exact, n, large)


LOG2E = math.log2(math.e)
WIN_BACK = WIN_LEN // Q_BLOCK
WIN_KEYS = (WIN_BACK + 1) * Q_BLOCK
WIN_STRIP_TILES = 2 * WIN_BACK + 1
SEL_TILES = 4
SEL_KEYS = SEL_TILES * Q_BLOCK
SEL_STRIP_TILES = 3 * SEL_TILES - 1
CMP_BIAS_ROWS = 2 * LANE
PV_ROWS = NSA_HEAD_DIM + 16


def _nsa_bias_tables(rel_bias):
    tbl = rel_bias.astype(F32)
    far = tbl[REL_BUCKETS - 1]
    hq = NSA_HPG * Q_BLOCK

    def lookup(dist):
        onehot = (_rel_bucket(jnp.asarray(dist))[..., None] == jnp.arange(REL_BUCKETS)).astype(F32)
        return jnp.einsum("...b,bh->...h", onehot, tbl, precision=lax.Precision.HIGHEST)

    k = np.arange(Q_BLOCK)[None, :, None]
    q = np.arange(Q_BLOCK)[None, None, :]

    def strip(first, last, visible):
        n = first - last + 1
        dist = np.arange(first, last - 1, -1)[:, None, None] * Q_BLOCK + q - k
        v = jnp.where(jnp.asarray(visible(dist))[..., None], (lookup(dist) - far) * LOG2E, NEG)
        v = v.reshape(n, Q_BLOCK, Q_BLOCK, NSA_KV_HEADS, NSA_HPG).transpose(3, 0, 1, 4, 2)
        return v.reshape(NSA_KV_HEADS, n * Q_BLOCK, hq)

    ts = strip(2 * SEL_TILES - 1, 1 - SEL_TILES, lambda d: d >= 0)
    ws = strip(WIN_BACK, -WIN_BACK, lambda d: (d >= 0) & (d < WIN_LEN))
    r = np.arange(CMP_BIAS_ROWS)[:, None]
    dist_c = q[0] - CMP_STRIDE * (r - LANE) - (CMP_LEN - 1)
    bc = jnp.where(jnp.asarray(dist_c >= 0)[..., None], lookup(dist_c) * LOG2E, NEG)
    bc = bc.reshape(CMP_BIAS_ROWS, Q_BLOCK, NSA_KV_HEADS, NSA_HPG).transpose(2, 0, 3, 1)
    bc = bc.reshape(NSA_KV_HEADS, CMP_BIAS_ROWS, hq)
    return ts, ws, bc


def _nsa_consts(s):
    n_slc = s // SLC_LEN
    et = (np.arange(s)[:, None] // SLC_LEN == np.arange(LANE)[None, :]).astype(np.float32)
    n_cmp = (s - CMP_LEN) // CMP_STRIDE + 1
    c0 = np.arange(n_cmp) * CMP_STRIDE
    s0 = np.arange(n_slc) * SLC_LEN
    ov = np.clip(np.minimum(c0[None, :] + CMP_LEN, s0[:, None] + SLC_LEN)
                 - np.maximum(c0[None, :], s0[:, None]), 0, None) / CMP_LEN
    ovt = np.zeros((n_slc, LANE), np.float32)
    ovt[:, :n_cmp] = ov
    return jnp.asarray(et, BF16), jnp.asarray(ovt, BF16)


def _nsa_kernel(q_ref, ng_ref, ks_ref, vs_ref, kw_ref, vw_ref, kc_ref, vct_ref,
                ts_ref, ws_ref, bc_ref, et_ref, ovt_ref, o_ref,
                ksb_ref, kwb_ref, vst_ref, vwt_ref, *, nq, n_slc):
    qi = pl.program_id(1)
    hq = NSA_HPG * Q_BLOCK
    dh = NSA_HEAD_DIM
    groups = range(NSA_KV_HEADS)
    gsl = [slice(g * dh, (g + 1) * dh) for g in groups]

    @pl.when(qi == 0)
    def _():
        ksb_ref[:, 0:LANE] = ks_ref[...].astype(BF16)
        ksb_ref[:, LANE:] = et_ref[...]
        kwb_ref[...] = kw_ref[...].astype(BF16)
        ones_rows = jnp.where(lax.broadcasted_iota(jnp.int32, (PV_ROWS - dh, ks_ref.shape[0]), 0) == 0, 1.0, 0.0)
        for g in groups:
            vst_ref[g, dh:, :] = ones_rows.astype(BF16)
            vwt_ref[g, dh:, :] = ones_rows.astype(BF16)
        for kt in range(nq):
            sl = slice(kt * Q_BLOCK, (kt + 1) * Q_BLOCK)
            vs_t = vs_ref[sl, :].T
            vw_t = vw_ref[sl, :].T
            for g in groups:
                vst_ref[g, 0:dh, sl] = vs_t[gsl[g], :].astype(BF16)
                vwt_ref[g, 0:dh, sl] = vw_t[gsl[g], :].astype(BF16)

    n_super = qi // SEL_TILES + 1
    for n in range(1, nq // SEL_TILES + 1):
        pl.when(n_super == n)(functools.partial(
            _nsa_step, q_ref, ng_ref, kc_ref, vct_ref, ts_ref, ws_ref, bc_ref, ovt_ref, o_ref,
            ksb_ref, kwb_ref, vst_ref, vwt_ref, n_slc=n_slc, n_sel=n))


def _nsa_step(q_ref, ng_ref, kc_ref, vct_ref, ts_ref, ws_ref, bc_ref, ovt_ref, o_ref,
              ksb_ref, kwb_ref, vst_ref, vwt_ref, *, n_slc, n_sel):
    qi = pl.program_id(1)
    hq = NSA_HPG * Q_BLOCK
    dh = NSA_HEAD_DIM
    groups = range(NSA_KV_HEADS)
    gsl = [slice(g * dh, (g + 1) * dh) for g in groups]
    row_lo = lax.broadcasted_iota(jnp.int32, (LANE, hq), 0) < dh

    def heads4(m):
        return jnp.concatenate([m] * NSA_HPG, axis=1)

    gates_t = jax.nn.sigmoid(ng_ref[...]).T
    q_t = jnp.concatenate(
        [(q_ref[:, h * LANE:(h + 1) * LANE] * (dh ** -0.5 * LOG2E)).T for h in range(NSA_HPG)], axis=1)
    qg_t = [jnp.where(row_lo if g == 0 else jnp.logical_not(row_lo), q_t, 0.0).astype(BF16) for g in groups]

    boff = pl.multiple_of(LANE - SUBLANE * qi, SUBLANE)
    jrow = lax.broadcasted_iota(jnp.int32, (n_slc, LANE), 0)
    tj = qi * Q_BLOCK + lax.broadcasted_iota(jnp.int32, (n_slc, LANE), 1)
    jt = lax.shift_right_logical(tj, SLC_LEN.bit_length() - 1)
    forced = (jrow == 0) | (jrow == jt) | (jrow == jt - 1)
    valid = jrow * SLC_LEN <= tj
    o_c = []
    q_aug = []
    for g in groups:
        s = _dot(kc_ref[0], qg_t[g]) + bc_ref[g, pl.ds(boff, LANE), :]
        mx = jnp.max(s, axis=0, keepdims=True)
        p = jnp.exp2(s - mx)
        lsum = jnp.sum(p, axis=0, keepdims=True)
        pb = (p * jnp.where(mx > 0.5 * NEG, 1.0 / lsum, 0.0)).astype(BF16)
        o_c.append(_dot(vct_ref[0, gsl[g], :], pb))
        imp4 = _dot(ovt_ref[...], pb)
        imp = imp4[:, 0:LANE]
        for h in range(1, NSA_HPG):
            imp = imp + imp4[:, h * LANE:(h + 1) * LANE]
        val = jnp.where(forced, BIG, jnp.where(valid, imp, -BIG))
        rank = jnp.zeros((n_slc, LANE), F32)
        for i in range(n_slc):
            ci = val[i:i + 1, :]
            rank = rank + jnp.where(ci > val, 1.0, 0.0) + jnp.where(ci == val, (jrow > i).astype(F32), 0.0)
        block_rows = heads4(jnp.where(rank < SLC_TOPN, 0.0, NEG)).astype(BF16)
        q_aug.append(jnp.concatenate(
            [qg_t[g], block_rows, jnp.zeros((LANE - n_slc, hq), BF16)], axis=0))

    diag = qi - SEL_TILES * (n_sel - 1)
    nk = n_sel * SEL_KEYS
    lo = max(n_sel - 2, 0) * SEL_KEYS
    toff = pl.multiple_of((2 * SEL_TILES - 1 - diag) * Q_BLOCK - (nk - lo - SEL_KEYS), Q_BLOCK)
    o_s = []
    for g in groups:
        parts = [(lo, nk, _dot(ksb_ref[lo:nk, :], q_aug[g]) + ts_ref[g, pl.ds(toff, nk - lo), :])]
        if lo > 0:
            parts.append((0, lo, _dot(ksb_ref[0:lo, :], q_aug[g])))
        m = parts[0][2].max(axis=0, keepdims=True)
        for _, _, s in parts[1:]:
            m = jnp.maximum(m, s.max(axis=0, keepdims=True))
        acc = None
        for a, b, s in parts:
            pv = _dot(vst_ref[g, :, a:b], jnp.exp2(s - m).astype(BF16))
            acc = pv if acc is None else acc + pv
        o_s.append(acc[0:dh] * (1.0 / acc[dh:dh + 1]))

    w0 = jnp.maximum(qi - WIN_BACK, 0)
    woff = pl.multiple_of(w0 * Q_BLOCK, Q_BLOCK)
    boff_w = pl.multiple_of((WIN_BACK - (qi - w0)) * Q_BLOCK, Q_BLOCK)
    kw_bf = kwb_ref[pl.ds(woff, WIN_KEYS), :]
    o_w = []
    for g in groups:
        s = _dot(kw_bf, qg_t[g]) + ws_ref[g, pl.ds(boff_w, WIN_KEYS), :]
        p = jnp.exp2(s - jnp.max(s, axis=0, keepdims=True)).astype(BF16)
        acc = _dot(vwt_ref[g, :, pl.ds(woff, WIN_KEYS)], p)
        o_w.append(acc[0:dh] * (1.0 / acc[dh:dh + 1]))

    for h in range(NSA_HPG):
        hl = slice(h * LANE, (h + 1) * LANE)
        halves = []
        for g in groups:
            c = g * NSA_HPG + h
            halves.append(gates_t[c:c + 1, :] * o_c[g][:, hl]
                          + gates_t[NSA_HEADS + c:NSA_HEADS + c + 1, :] * o_s[g][:, hl]
                          + gates_t[2 * NSA_HEADS + c:2 * NSA_HEADS + c + 1, :] * o_w[g][:, hl])
        o_ref[:, hl] = jnp.concatenate(halves, axis=0).T.astype(o_ref.dtype)


def _nsa(proj2d, kc, vct, ts, ws, bc, et, ovt, bsz, s):
    nq = s // Q_BLOCK
    n_slc = s // SLC_LEN
    hq = NSA_HPG * Q_BLOCK
    kvb = C_KV // LANE
    kv_spec = lambda blk: pl.BlockSpec((s, LANE), lambda b, qi: (b, kvb + blk))
    full = lambda shape: pl.BlockSpec(shape, lambda b, qi: (0,) * len(shape), pipeline_mode=pl.Buffered(1))
    return pl.pallas_call(
        functools.partial(_nsa_kernel, nq=nq, n_slc=n_slc),
        out_shape=jax.ShapeDtypeStruct((bsz * s, NSA_WIDTH), BF16),
        grid=(bsz, nq),
        in_specs=[pl.BlockSpec((Q_BLOCK, NSA_WIDTH), lambda b, qi: (b * nq + qi, C_Q // NSA_WIDTH)),
                  pl.BlockSpec((Q_BLOCK, LANE), lambda b, qi: (b * nq + qi, C_NG // LANE)),
                  kv_spec(2), kv_spec(3), kv_spec(4), kv_spec(5),
                  pl.BlockSpec((1, s // CMP_STRIDE, LANE), lambda b, qi: (b, 0, 0)),
                  pl.BlockSpec((1, LANE, s // CMP_STRIDE), lambda b, qi: (b, 0, 0)),
                  full((NSA_KV_HEADS, SEL_STRIP_TILES * Q_BLOCK, hq)),
                  full((NSA_KV_HEADS, WIN_STRIP_TILES * Q_BLOCK, hq)),
                  full((NSA_KV_HEADS, CMP_BIAS_ROWS, hq)),
                  full((s, LANE)),
                  full((n_slc, LANE))],
        out_specs=pl.BlockSpec((Q_BLOCK, NSA_WIDTH), lambda b, qi: (b * nq + qi, 0)),
        scratch_shapes=[pltpu.VMEM((s, 2 * LANE), BF16), pltpu.VMEM((s, LANE), BF16),
                        pltpu.VMEM((NSA_KV_HEADS, PV_ROWS, s), BF16), pltpu.VMEM((NSA_KV_HEADS, PV_ROWS, s), BF16)],
        compiler_params=_cparams(("parallel", "arbitrary")),
        name="nsa_attn",
    )(proj2d, proj2d, proj2d, proj2d, proj2d, proj2d, kc, vct, ts, ws, bc, et, ovt)


SSD_BATCH = 4


def _ssd_kernel(xbc_ref, prev_ref, z_ref, dt_ref, *rest):
    shared, o_ref, state_ref = rest[:-2], rest[-2], rest[-1]
    c = pl.program_id(1)
    for u in range(SSD_BATCH):
        _ssd_chunk(c, xbc_ref.at[u], prev_ref.at[u], z_ref.at[u], dt_ref.at[u], *shared,
                   o_ref.at[u], state_ref.at[u])


def _ssd_chunk(c, xbc_ref, prev_ref, z_ref, dt_ref, cw_ref, cb_ref, dtb_ref, alog_ref,
               dskip_ref, ng_ref, e16_ref, tril_ref, o_ref, state_ref):
    L = SSD_CHUNK
    pair_w = 2 * SSD_HEAD_DIM

    @pl.when(c == 0)
    def _():
        state_ref[...] = jnp.zeros(state_ref.shape, F32)

    xcat = jnp.concatenate([jnp.where(c > 0, prev_ref[...], 0.0), xbc_ref[...]], axis=0)
    acc = cw_ref[0:1, :] * xcat
    for k in range(1, SSD_CONV):
        acc = pltpu.roll(acc, 1, axis=0) + cw_ref[k:k + 1, :] * xcat
    xbc = jax.nn.silu(acc[SUBLANE:, :] + cb_ref[...])
    xs = xbc[:, :SSD_D_INNER]
    bmat = xbc[:, SSD_D_INNER:SSD_D_INNER + SSD_GROUPS * SSD_STATE]
    cmat = xbc[:, SSD_D_INNER + SSD_GROUPS * SSD_STATE:]

    dt = jax.nn.softplus(dt_ref[...] + dtb_ref[...])
    tril = tril_ref[...]
    e16 = e16_ref[...]
    d1, d2, d3 = _split3(dt)
    dt_x = _dot(d1, e16) + _dot(d2, e16) + _dot(d3, e16)
    a_c = dt * (-jnp.exp(alog_ref[...]) * LOG2E)
    c1, c2, c3 = _split3(a_c)
    acs = _dot(tril, c1) + _dot(tril, c2) + _dot(tril, c3)
    x1, x2, x3 = _split3(acs)
    acs_x = _dot(x1, e16) + _dot(x2, e16) + _dot(x3, e16)
    acs_t = acs.T
    last = acs_x[L - 1:L, :]
    eacs = jnp.exp2(acs_x)
    decay_in = jnp.exp2(last - acs_x)
    cdec = jnp.exp2(last)
    xdt = xs * dt_x
    wst = xdt * decay_in

    row = lax.broadcasted_iota(jnp.int32, (L, L), 0)
    lane = lax.broadcasted_iota(jnp.int32, (L, L), 1)
    tri = row >= lane
    lo = lane < SSD_HEAD_DIM

    ys = []
    for g in range(SSD_GROUPS):
        cg = cmat[:, g * SSD_STATE:(g + 1) * SSD_STATE].astype(BF16)
        bg = bmat[:, g * SSD_STATE:(g + 1) * SSD_STATE].astype(BF16)
        cb = _dot_nt(cg, bg)
        for pr in range(2):
            pp = g * 2 + pr
            ls = slice(pp * pair_w, (pp + 1) * pair_w)
            xp = xdt[:, ls]
            yd = None
            for e in range(2):
                hd = 2 * pp + e
                diff = acs[:, hd:hd + 1] - acs_t[hd:hd + 1, :]
                seg = jnp.exp2(jnp.where(tri, diff, NEG))
                pm = (cb * seg).astype(BF16)
                xe = jnp.where(lo if e == 0 else jnp.logical_not(lo), xp, 0.0).astype(BF16)
                term = _dot(pm, xe)
                yd = term if yd is None else yd + term
            st = state_ref[pp]
            yo = _dot(cg, st.astype(BF16)) * eacs[:, ls]
            ys.append(yd + yo)
            state_ref[pp] = cdec[:, ls] * st + _dot_tn(bg, wst[:, ls].astype(BF16))

    y = jnp.concatenate(ys, axis=1) + xs * dskip_ref[...]
    y = y * jax.nn.silu(z_ref[...])
    gw = SSD_D_INNER // SSD_GROUPS
    outs = []
    for gi in range(SSD_GROUPS):
        yg = y[:, gi * gw:(gi + 1) * gw]
        outs.append(yg * lax.rsqrt(jnp.mean(yg * yg, -1, keepdims=True) + LN_EPS))
    o_ref[...] = (jnp.concatenate(outs, axis=1) * ng_ref[...]).astype(o_ref.dtype)


def _ssd(proj2d, bsz, s, conv_w, conv_b, dt_bias, a_log, d_skip, norm_g):
    nc = s // SSD_CHUNK
    L = SSD_CHUNK
    pad16 = lambda v: jnp.pad(v.reshape(1, SSD_HEADS), ((0, 0), (0, LANE - SSD_HEADS)))
    rep64 = lambda v: jnp.repeat(v, SSD_HEAD_DIM).reshape(1, SSD_D_INNER)
    e16 = np.zeros((LANE, SSD_D_INNER), np.float32)
    e16[np.arange(SSD_D_INNER) // SSD_HEAD_DIM, np.arange(SSD_D_INNER)] = 1.0
    tril = np.tril(np.ones((L, L), np.float32))
    full = lambda shape: pl.BlockSpec(shape, lambda b, c: (0,) * len(shape))
    nb = SSD_BATCH
    assert bsz % nb == 0
    proj3d = proj2d.reshape(bsz, s, D_PAD)
    out = pl.pallas_call(
        _ssd_kernel,
        out_shape=jax.ShapeDtypeStruct((bsz, s, SSD_D_INNER), BF16),
        grid=(bsz // nb, nc),
        in_specs=[pl.BlockSpec((nb, L, SSD_XBC), lambda b, c: (b, c, C_XBC // SSD_XBC)),
                  pl.BlockSpec((nb, SUBLANE, SSD_XBC),
                               lambda b, c: (b, jnp.maximum(c * (L // SUBLANE) - 1, 0), C_XBC // SSD_XBC)),
                  pl.BlockSpec((nb, L, SSD_D_INNER), lambda b, c: (b, c, C_Z // SSD_D_INNER)),
                  pl.BlockSpec((nb, L, LANE), lambda b, c: (b, c, C_DT // LANE)),
                  full((SSD_CONV, SSD_XBC)), full((1, SSD_XBC)), full((1, LANE)), full((1, LANE)),
                  full((1, SSD_D_INNER)), full((1, SSD_D_INNER)),
                  full((LANE, SSD_D_INNER)), full((L, L))],
        out_specs=pl.BlockSpec((nb, L, SSD_D_INNER), lambda b, c: (b, c, 0)),
        scratch_shapes=[pltpu.VMEM((nb, SSD_HEADS // 2, SSD_STATE, 2 * SSD_HEAD_DIM), F32)],
        compiler_params=_cparams(("parallel", "arbitrary")),
        name="ssd",
    )(proj3d, proj3d, proj3d, proj3d, conv_w, conv_b.reshape(1, SSD_XBC), pad16(dt_bias), pad16(a_log),
      rep64(d_skip), norm_g.reshape(1, SSD_D_INNER),
      jnp.asarray(e16, BF16), jnp.asarray(tril, BF16))
    return out.reshape(bsz * s, SSD_D_INNER)


def _mix_kernel(up_ref, upprev_ref, nsa_ref, ssd_ref, ga_ref, gb_ref, gc_ref, h_ref,
                pw_ref, ps_ref, wbp_ref, wbn_ref, wbs_ref, wo_ref, lg_ref, lb_ref, o_ref, *, ts):
    si = pl.program_id(1)
    w0 = MAX_POOL_WINDOW
    prev = jnp.where(si > 0, upprev_ref[...], 0.0)
    xcat = jnp.concatenate([prev, up_ref[...]], axis=0)
    tpos = si * ts + lax.broadcasted_iota(jnp.int32, (ts, POOL_GROUP), 0)
    ygs = []
    for gi, w in enumerate(POOL_WINDOWS):
        x = xcat[:, gi * POOL_GROUP:(gi + 1) * POOL_GROUP]
        acc = x
        d = 1
        while d < w:
            acc = acc + pltpu.roll(acc, d, axis=0)
            d *= 2
        cnt = jnp.minimum(tpos + 1, w).astype(F32)
        r = acc[w0:, :] / cnt - x[w0:, :]
        ygs.append(_dot(r.astype(BF16), pw_ref[gi]))
    y = jnp.concatenate(ygs, axis=1) * ps_ref[...]
    br_a = _dot(y.astype(BF16), wbp_ref[...])
    br_b = _dot(nsa_ref[...], wbn_ref[...])
    br_c = _dot(ssd_ref[...], wbs_ref[...])
    mixed = (jax.nn.sigmoid(ga_ref[...]) * br_a + jax.nn.sigmoid(gb_ref[...]) * br_b
             + jax.nn.sigmoid(gc_ref[...]) * br_c)
    mix = _dot(mixed.astype(BF16), wo_ref[...])
    _store_rows(o_ref, _layer_norm(DN_ALPHA * _load_rows(h_ref) + mix, lg_ref[...], lb_ref[...]))


def _mix(proj2d, nsa_out, ssd_y, h3, bsz, s, pool_w, pool_scale, w_br_pool, w_br_nsa, w_br_ssd,
         w_out, ln_g, ln_b):
    ts = MIX_TS
    ns = s // ts
    t = bsz * s
    d = D_MODEL
    w0 = MAX_POOL_WINDOW
    wbn = w_br_nsa.reshape(NSA_KV_HEADS, NSA_HPG, NSA_HEAD_DIM, d).transpose(1, 0, 2, 3).reshape(NSA_WIDTH, d)
    full = lambda shape: pl.BlockSpec(shape, lambda b, si: (0,) * len(shape), pipeline_mode=pl.Buffered(1))
    row = lambda width, cb: pl.BlockSpec((ts, width), lambda b, si: (b * ns + si, cb))
    return pl.pallas_call(
        functools.partial(_mix_kernel, ts=ts),
        out_shape=jax.ShapeDtypeStruct((t,) + ROW_TILE, F32),
        grid=(bsz, ns),
        in_specs=[row(POOL_WIDTH, C_POOL // POOL_WIDTH),
                  pl.BlockSpec((w0, POOL_WIDTH),
                               lambda b, si: (jnp.maximum((b * s + si * ts) // w0 - 1, 0), C_POOL // POOL_WIDTH)),
                  row(NSA_WIDTH, 0), row(d, 0),
                  row(d, C_GATE // d), row(d, C_GATE // d + 1), row(d, C_GATE // d + 2),
                  _row_spec(ts, lambda b, si: b * ns + si),
                  full((len(POOL_WINDOWS), POOL_GROUP, POOL_GROUP)), full((1, POOL_WIDTH)),
                  full((POOL_WIDTH, d)), full((NSA_WIDTH, d)), full((d, d)), full((d, d)),
                  full((1, d)), full((1, d))],
        out_specs=_row_spec(ts, lambda b, si: b * ns + si),
        compiler_params=_cparams(("parallel", "arbitrary")),
        name="mix",
    )(proj2d, proj2d, nsa_out, ssd_y, proj2d, proj2d, proj2d, h3,
      pool_w.astype(BF16), pool_scale.reshape(1, POOL_WIDTH), w_br_pool.astype(BF16), wbn.astype(BF16),
      w_br_ssd.astype(BF16), w_out.astype(BF16), ln_g.reshape(1, d), ln_b.reshape(1, d))


RANK_CHUNK = 256
BUCKET_ROWS = 32


def _route(hb, first_step, rw_ref, rb_ref, tri_ref, bucket_ref, wlo_ref, whi_ref, rank_ref, cnt_ref, run_ref):
    @pl.when(first_step)
    def _():
        run_ref[...] = jnp.zeros(run_ref.shape, F32)

    logits = _dot_nt(rw_ref[...], hb)
    lg = [logits[e:e + 1, :] for e in range(N_EXPERTS)]
    mx = lg[0]
    for e in range(1, N_EXPERTS):
        mx = jnp.maximum(mx, lg[e])
    ex = [jnp.exp(v - mx) for v in lg]
    den = ex[0]
    for e in range(1, N_EXPERTS):
        den = den + ex[e]
    probs = [v / den for v in ex]
    score = [probs[e] + rb_ref[e:e + 1, :] for e in range(N_EXPERTS)]

    def group_vals(vals, grp):
        out = []
        for k in range(EXPERTS_PER_GROUP):
            v = vals[k]
            for gi in range(1, N_EXPERT_GROUPS):
                v = jnp.where(grp == gi, vals[gi * EXPERTS_PER_GROUP + k], v)
            out.append(v)
        return out

    gscore = []
    for gi in range(N_EXPERT_GROUPS):
        sc = score[gi * EXPERTS_PER_GROUP:(gi + 1) * EXPERTS_PER_GROUP]
        best = None
        for a, b in PAIRS:
            v = sc[a] + sc[b]
            best = v if best is None else jnp.maximum(best, v)
        gscore.append(best)
    grp = jnp.zeros(gscore[0].shape, jnp.int32)
    best = gscore[0]
    for gi in range(1, N_EXPERT_GROUPS):
        better = gscore[gi] > best
        grp = jnp.where(better, gi, grp)
        best = jnp.where(better, gscore[gi], best)

    sc = group_vals(score, grp)
    pr = group_vals(probs, grp)
    first = jnp.zeros(grp.shape, jnp.int32)
    fv = sc[0]
    for k in range(1, EXPERTS_PER_GROUP):
        better = sc[k] > fv
        first = jnp.where(better, k, first)
        fv = jnp.where(better, sc[k], fv)
    second = jnp.full(grp.shape, -1, jnp.int32)
    sv = jnp.full(fv.shape, -jnp.inf, F32)
    for k in range(EXPERTS_PER_GROUP):
        better = (first != k) & ((sc[k] > sv) | (second < 0))
        second = jnp.where(better, k, second)
        sv = jnp.where(better, sc[k], sv)
    lo = jnp.minimum(first, second)
    hi = jnp.maximum(first, second)
    p_lo = pr[0]
    p_hi = pr[0]
    for k in range(1, EXPERTS_PER_GROUP):
        p_lo = jnp.where(lo == k, pr[k], p_lo)
        p_hi = jnp.where(hi == k, pr[k], p_hi)
    tot = p_lo + p_hi
    pair = jnp.zeros(grp.shape, jnp.int32)
    for pi, (a, b) in enumerate(PAIRS):
        pair = jnp.where((lo == a) & (hi == b), pi, pair)
    bucket = grp * len(PAIRS) + pair
    bucket_ref[...] = bucket
    wlo_ref[...] = p_lo / tot
    whi_ref[...] = p_hi / tot

    tm = bucket.shape[1]
    onehot = jnp.where(lax.broadcasted_iota(jnp.int32, (BUCKET_ROWS, tm), 0) == bucket, 1.0, 0.0)
    run = run_ref[...]
    ranks = []
    for c in range(tm // RANK_CHUNK):
        oc = onehot[:, c * RANK_CHUNK:(c + 1) * RANK_CHUNK]
        before = _dot(oc.astype(BF16), tri_ref[...]) + jnp.concatenate([run] * (RANK_CHUNK // LANE), axis=1)
        ranks.append(jnp.sum(oc * before, axis=0, keepdims=True))
        run = run + jnp.broadcast_to(jnp.sum(oc, axis=1, keepdims=True), run.shape)
    rank_ref[...] = jnp.concatenate(ranks, axis=1).astype(jnp.int32)
    run_ref[...] = run
    cnt_ref[...] = run


ROUTER_TM = 2048


def _router_kernel(h_ref, *refs):
    _route(_load_rows(h_ref).astype(BF16), pl.program_id(0) == 0, *refs)


def _router(h2d, router_w, router_b):
    t, d = h2d.shape[0], D_MODEL
    tm = ROUTER_TM
    rw = jnp.pad(router_w.T, ((0, LANE - N_EXPERTS), (0, 0))).astype(BF16)
    rb = jnp.broadcast_to(jnp.pad(router_b, (0, LANE - N_EXPERTS))[:, None], (LANE, tm))
    tri = jnp.asarray(np.triu(np.ones((RANK_CHUNK, RANK_CHUNK), np.float32), 1), BF16)
    vec = lambda dt: jax.ShapeDtypeStruct((1, t), dt)
    row = pl.BlockSpec((1, tm), lambda i: (0, i))
    return pl.pallas_call(
        _router_kernel,
        out_shape=(vec(jnp.int32), vec(F32), vec(F32), vec(jnp.int32),
                   jax.ShapeDtypeStruct((BUCKET_ROWS, LANE), F32)),
        grid=(t // tm,),
        in_specs=[_row_spec(tm, lambda i: i),
                  pl.BlockSpec((LANE, d), lambda i: (0, 0)),
                  pl.BlockSpec((LANE, tm), lambda i: (0, 0)),
                  pl.BlockSpec((RANK_CHUNK, RANK_CHUNK), lambda i: (0, 0))],
        out_specs=(row, row, row, row, pl.BlockSpec((BUCKET_ROWS, LANE), lambda i: (0, 0))),
        scratch_shapes=[pltpu.VMEM((BUCKET_ROWS, LANE), F32)],
        compiler_params=_cparams(("arbitrary",)),
        name="router",
    )(h2d, rw, rb, tri)


def _moe_plan(bucket, rank, counts, w_lo, w_hi, t):
    tm = MOE_TM
    n_tiles = t // tm + N_BUCKETS
    p_rows = n_tiles * tm
    tiles_per = (counts + tm - 1) // tm
    tile_end = jnp.cumsum(tiles_per)
    tile_start = tile_end - tiles_per
    in_bucket = bucket[:, None] == jnp.arange(N_BUCKETS)[None, :]
    dest = jnp.sum(jnp.where(in_bucket, (tile_start * tm)[None, :], 0), axis=1) + rank
    payload = jnp.stack([jnp.arange(t, dtype=jnp.int32), lax.bitcast_convert_type(w_lo, jnp.int32),
                         lax.bitcast_convert_type(w_hi, jnp.int32), jnp.ones((t,), jnp.int32)], axis=1)
    plan = jnp.zeros((p_rows, 4), jnp.int32).at[dest].set(payload, unique_indices=True,
                                                          mode="promise_in_bounds")
    src = plan[:, 0]
    wl = lax.bitcast_convert_type(plan[:, 1], F32)
    wh = lax.bitcast_convert_type(plan[:, 2], F32)
    prow = jnp.arange(p_rows, dtype=jnp.int32)
    spare = t + ((prow // tm) % 2) * tm + prow % tm
    dst = jnp.where(plan[:, 3] > 0, src, spare)
    n_used = tile_end[-1]
    tile_ids = jnp.arange(n_tiles)
    tb = jnp.sum((tile_end[None, :] <= tile_ids[:, None]).astype(jnp.int32), axis=1)
    tb = jnp.where(tile_ids < n_used, tb, tb[jnp.maximum(n_used - 1, 0)])
    tb = jnp.minimum(tb, N_BUCKETS - 1)
    pairs = np.asarray(PAIRS, np.int32)
    grp = tb // len(PAIRS)
    ea = grp * EXPERTS_PER_GROUP + jnp.asarray(pairs[:, 0])[tb % len(PAIRS)]
    eb = grp * EXPERTS_PER_GROUP + jnp.asarray(pairs[:, 1])[tb % len(PAIRS)]
    nvalid = jnp.clip(counts[tb] - (tile_ids - tile_start[tb]) * tm, 0, tm)
    nvalid = jnp.where(tile_ids < n_used, nvalid, 0).astype(jnp.int32)
    return (src.reshape(n_tiles, 1, tm), dst.reshape(n_tiles, 1, tm), wl.reshape(p_rows, 1), wh.reshape(p_rows, 1),
            ea.astype(jnp.int32), eb.astype(jnp.int32), nvalid, n_tiles)


MOE_DMA_GROUPS = 8


def _moe_kernel(ea_ref, eb_ref, nv_ref, src0_ref, srcn_ref, dstp_ref, wl_ref, wh_ref, h_hbm,
                w1a_ref, w3a_ref, w2a_ref, w1b_ref, w3b_ref, w2b_ref, lg_ref, lb_ref,
                o_hbm, xbuf0, xbuf1, obuf0, obuf1, wa1, wa3, wa2, wb1, wb3, wb2, gsem, ssem):
    i = pl.program_id(0)
    tm = xbuf0.shape[0]
    xbufs = (xbuf0, xbuf1)
    obufs = (obuf0, obuf1)
    nv = nv_ref[i]
    nv_p1 = nv_ref[jnp.maximum(i - 1, 0)]
    odd = lax.rem(i, 2) == 1
    even = jnp.logical_not(odd)

    def gather(r, p, row):
        return pltpu.make_async_copy(h_hbm.at[pl.ds(row, 1)], xbufs[p].at[pl.ds(r, 1)], gsem.at[p])

    def scatter(r, p, row):
        return pltpu.make_async_copy(obufs[p].at[pl.ds(r, 1)], o_hbm.at[pl.ds(row, 1)], ssem.at[p])

    def wait_gather(p):
        for r in range(tm):
            gather(r, p, 0).wait()

    def wait_scatter(p):
        for r in range(tm):
            scatter(r, p, 0).wait()

    def tile(p, age):
        q = 1 - p
        half = MOE_DMA_GROUPS // 2
        per = tm // half

        def issue(k):
            for r in range((k % half) * per, (k % half + 1) * per):
                if k < half:
                    gather(r, q, srcn_ref[0, 0, r]).start()
                elif age >= 1:
                    scatter(r, q, dstp_ref[0, 0, r]).start()

        wait_gather(p)
        x = _load_rows(xbufs[p])
        xb = x.astype(BF16)
        issue(0)
        h1 = _dot(xb, wa1[...])
        issue(1)
        h3 = _dot(xb, wa3[...])
        issue(2)
        ya = _dot((jax.nn.silu(h1) * h3).astype(BF16), wa2[...])
        issue(3)
        h1 = _dot(xb, wb1[...])
        issue(4)
        h3 = _dot(xb, wb3[...])
        issue(5)
        yb = _dot((jax.nn.silu(h1) * h3).astype(BF16), wb2[...])
        issue(6)
        y = wl_ref[...] * ya + wh_ref[...] * yb
        issue(7)
        out = _layer_norm(DN_ALPHA * x + y, lg_ref[...], lb_ref[...])
        if age >= 2:
            wait_scatter(p)
        _store_rows(obufs[p], out)

    @pl.when(i == 0)
    def _():
        obuf1[...] = jnp.zeros(obuf1.shape, F32)
        n_real = o_hbm.shape[0] - 2 * tm
        for blk in range(2):
            cp = pltpu.make_async_copy(obuf1, o_hbm.at[pl.ds(n_real + blk * tm, tm)], ssem.at[1])
            cp.start()
            cp.wait()

    prev = jnp.maximum(i - 1, 0)

    @pl.when((nv > 0) & ((i == 0) | (ea_ref[i] != ea_ref[prev])))
    def _():
        wa1[...] = w1a_ref[0, 0].astype(BF16)
        wa3[...] = w3a_ref[0, 0].astype(BF16)
        wa2[...] = w2a_ref[0, 0].astype(BF16)

    @pl.when((nv > 0) & ((i == 0) | (eb_ref[i] != eb_ref[prev])))
    def _():
        wb1[...] = w1b_ref[0, 0].astype(BF16)
        wb3[...] = w3b_ref[0, 0].astype(BF16)
        wb2[...] = w2b_ref[0, 0].astype(BF16)

    @pl.when((nv > 0) & (i == 0))
    def _():
        def start0(r, carry):
            gather(r, 0, src0_ref[0, 0, r]).start()
            return carry
        lax.fori_loop(0, tm, start0, 0)
        tile(0, 0)

    pl.when((nv > 0) & (i == 1))(functools.partial(tile, 1, 1))
    pl.when((nv > 0) & (i >= 2) & even)(functools.partial(tile, 0, 2))
    pl.when((nv > 0) & (i >= 2) & odd)(functools.partial(tile, 1, 2))

    def tail(p):
        q = 1 - p
        wait_gather(p)

        @pl.when(i >= 2)
        def _():
            wait_scatter(p)

        def start(r, carry):
            scatter(r, q, dstp_ref[0, 0, r]).start()
            return carry
        lax.fori_loop(0, tm, start, 0)
        wait_scatter(q)

    is_tail = (nv == 0) & (i >= 1) & (nv_p1 > 0)
    pl.when(is_tail & even)(functools.partial(tail, 0))
    pl.when(is_tail & odd)(functools.partial(tail, 1))


def _moe(h3, bucket, rank, counts, w_lo, w_hi, w1, w3, w2, layer, ln_g, ln_b):
    t, d = h3.shape[0], D_MODEL
    tm = MOE_TM
    assert tm % MOE_DMA_GROUPS == 0
    src, dst, wl, wh, ea, eb, nvalid, n_tiles = _moe_plan(bucket, rank, counts, w_lo, w_hi, t)
    wspec = lambda shape, which: pl.BlockSpec(
        (1,) + shape, (lambda i, ea_r, eb_r, nv_r: (layer, ea_r[i], 0, 0)) if which == 0
        else (lambda i, ea_r, eb_r, nv_r: (layer, eb_r[i], 0, 0)))
    w13 = (1, d, D_EXPERT)
    w2s = (1, D_EXPERT, d)
    smem_tile = lambda fn: pl.BlockSpec((1, 1, tm), lambda i, *_: (fn(i), 0, 0), memory_space=pltpu.SMEM)
    grid_spec = pltpu.PrefetchScalarGridSpec(
        num_scalar_prefetch=3,
        grid=(n_tiles,),
        in_specs=[smem_tile(lambda i: 0),
                  smem_tile(lambda i: jnp.minimum(i + 1, n_tiles - 1)),
                  smem_tile(lambda i: jnp.maximum(i - 1, 0)),
                  pl.BlockSpec((tm, 1), lambda i, *_: (i, 0)),
                  pl.BlockSpec((tm, 1), lambda i, *_: (i, 0)),
                  pl.BlockSpec(memory_space=pl.ANY),
                  wspec(w13, 0), wspec(w13, 0), wspec(w2s, 0),
                  wspec(w13, 1), wspec(w13, 1), wspec(w2s, 1),
                  pl.BlockSpec((1, d), lambda i, *_: (0, 0)),
                  pl.BlockSpec((1, d), lambda i, *_: (0, 0))],
        out_specs=pl.BlockSpec(memory_space=pl.ANY),
        scratch_shapes=[pltpu.VMEM((tm,) + ROW_TILE, F32)] * 4
        + [pltpu.VMEM(w13[1:], BF16), pltpu.VMEM(w13[1:], BF16), pltpu.VMEM(w2s[1:], BF16)] * 2
        + [pltpu.SemaphoreType.DMA((2,)), pltpu.SemaphoreType.DMA((2,))],
    )
    return pl.pallas_call(
        _moe_kernel,
        out_shape=jax.ShapeDtypeStruct((t + 2 * tm,) + ROW_TILE, F32),
        grid_spec=grid_spec,
        compiler_params=_cparams(("arbitrary",)),
        name="moe",
    )(ea, eb, nvalid, src, src, dst, wl, wh, h3, w1, w3, w2, w1, w3, w2,
      ln_g.reshape(1, d), ln_b.reshape(1, d))


def kernel(x, ln0_g, ln0_b, w_in, pool_w, pool_scale, cmp_k_pe, cmp_k_w1, cmp_k_w2, cmp_v_pe, cmp_v_w1, cmp_v_w2, rel_bias, conv_w, conv_b, dt_bias, a_log, d_skip, ssd_norm_g, w_br_pool, w_br_nsa, w_br_ssd, w_out, ln1_g, ln1_b, router_w, router_b, exp_w1, exp_w3, exp_w2, ln2_g, ln2_b):
    bsz, s, d = x.shape
    assert d == D_MODEL and s % MIX_TS == 0 and s // CMP_STRIDE == LANE and (bsz * s) % PROJ_TM == 0
    t = bsz * s
    ts, ws, bc = _nsa_bias_tables(rel_bias)
    et, ovt = _nsa_consts(s)
    h = x.reshape(t, d)
    for i in range(DEPTH):
        if i == 0:
            proj, h = _proj(h, _prep_w_in(w_in[i]), t, ln=(ln0_g, ln0_b))
        else:
            proj = _proj(h, _prep_w_in(w_in[i]), t)
        kc, vc = _compress(proj, bsz, s, cmp_k_pe[i], cmp_k_w1[i], cmp_k_w2[i],
                           cmp_v_pe[i], cmp_v_w1[i], cmp_v_w2[i])
        nsa_out = _nsa(proj, kc, vc, ts, ws, bc, et, ovt, bsz, s)
        ssd_y = _ssd(proj, bsz, s, conv_w[i], conv_b[i], dt_bias[i], a_log[i], d_skip[i], ssd_norm_g[i])
        h1 = _mix(proj, nsa_out, ssd_y, h, bsz, s, pool_w[i], pool_scale[i], w_br_pool[i], w_br_nsa[i],
                  w_br_ssd[i], w_out[i], ln1_g[i], ln1_b[i])
        bucket, w_lo, w_hi, rank, cnt = _router(h1, router_w, router_b)
        counts = cnt[:N_BUCKETS, 0].astype(jnp.int32)
        h = _moe(h1, bucket[0], rank[0], counts, w_lo[0], w_hi[0], exp_w1, exp_w3, exp_w2, i,
                 ln2_g[i], ln2_b[i])
    return h[:t].reshape(bsz, s, d)
```
